```python
import math
import jax, jax.numpy as jnp
from jax import lax
import numpy as np

D_MODEL = 1024
BATCH = 8
SEQ = 8192
DEPTH = 1

CHUNK = 64
MEM_LEN = 256
W_A = D_MODEL
CONV_A = 3
W_B = D_MODEL
CONV_B = 31
W_X = D_MODEL
N_HEADS_X = 4
HEAD_DIM_X = W_X // N_HEADS_X
EPS = 1e-6

SPLIT_SIZES = (
    W_A, W_A, W_A, W_A,
    W_B, W_B, W_B,
    W_X, W_X,
    D_MODEL, D_MODEL, D_MODEL,
)
IN_COLS = sum(SPLIT_SIZES)
SPLIT_POINTS = tuple(int(v) for v in np.cumsum(SPLIT_SIZES)[:-1])

kernel_name = "hybrid_shortconv_conformer_memxattn_gated"


def rmsnorm(x, g):
    xf = x.astype(jnp.float32)
    y = xf * lax.rsqrt(jnp.mean(xf * xf, axis=-1, keepdims=True) + EPS)
    return (y * g.astype(jnp.float32)).astype(x.dtype)


def layernorm(x, g, b):
    xf = x.astype(jnp.float32)
    mu = jnp.mean(xf, axis=-1, keepdims=True)
    xc = xf - mu
    var = jnp.mean(xc * xc, axis=-1, keepdims=True)
    y = xc * lax.rsqrt(var + EPS) * g.astype(jnp.float32) + b.astype(jnp.float32)
    return y.astype(x.dtype)


def causal_dwconv(x, w):
    k = w.shape[0]
    return lax.conv_general_dilated(
        x, w.astype(x.dtype)[:, None, :],
        window_strides=(1,), padding=[(k - 1, 0)],
        dimension_numbers=("NWC", "WIO", "NWC"),
        feature_group_count=x.shape[-1])


def memory_cross_attention(q, mem_n, w_kv):
    b, s, _ = q.shape
    kv = jnp.einsum("bmd,de->bme", mem_n, w_kv)
    k, v = jnp.split(kv, 2, axis=-1)
    qh = q.reshape(b, s, N_HEADS_X, HEAD_DIM_X)
    kh = k.reshape(b, -1, N_HEADS_X, HEAD_DIM_X)
    vh = v.reshape(b, -1, N_HEADS_X, HEAD_DIM_X)
    scores = jnp.einsum("bshd,bmhd->bhsm", qh, kh).astype(jnp.float32) * (HEAD_DIM_X ** -0.5)
    probs = jax.nn.softmax(scores, axis=-1).astype(q.dtype)
    o = jnp.einsum("bhsm,bmhd->bshd", probs, vh)
    return o.reshape(b, s, W_X)


def _fwd_setup_inputs(seed: int = 0) -> dict:
    key = jax.random.key(seed)
    ks = jax.random.split(key, 20)
    f32 = jnp.float32
    nrm = lambda k, shape: jax.random.normal(k, shape, f32)
    return {
        "x": nrm(ks[0], (BATCH, SEQ, D_MODEL)),
        "mem": nrm(ks[1], (BATCH, MEM_LEN, D_MODEL)),
        "norm_g": 1.0 + 0.02 * nrm(ks[2], (DEPTH, D_MODEL)),
        "w_in": nrm(ks[3], (DEPTH, D_MODEL, IN_COLS)) * D_MODEL ** -0.5,
        "conv_a_w": nrm(ks[4], (DEPTH, CONV_A, W_A)) * CONV_A ** -0.5,
        "w_out_a": nrm(ks[5], (DEPTH, W_A, D_MODEL)) * W_A ** -0.5,
        "conv_b_w": nrm(ks[6], (DEPTH, CONV_B, W_B)) * CONV_B ** -0.5,
        "conv_b_b": 0.02 * nrm(ks[7], (DEPTH, W_B)),
        "ln_b_g": 1.0 + 0.02 * nrm(ks[8], (DEPTH, W_B)),
        "ln_b_b": 0.02 * nrm(ks[9], (DEPTH, W_B)),
        "w_out_b": nrm(ks[10], (DEPTH, W_B, D_MODEL)) * W_B ** -0.5,
        "mem_norm_g": 1.0 + 0.02 * nrm(ks[11], (DEPTH, D_MODEL)),
        "w_kv": nrm(ks[12], (DEPTH, D_MODEL, 2 * W_X)) * D_MODEL ** -0.5,
        "w_out_x": nrm(ks[13], (DEPTH, W_X, D_MODEL)) * W_X ** -0.5,
        "w_o": nrm(ks[14], (DEPTH, D_MODEL, D_MODEL)) * D_MODEL ** -0.5,
        "final_g": 1.0 + 0.02 * nrm(ks[15], (D_MODEL,)),
    }


def _fwd_reference(x, mem, norm_g, w_in, conv_a_w, w_out_a, conv_b_w, conv_b_b,
              ln_b_g, ln_b_b, w_out_b, mem_norm_g, w_kv, w_out_x, w_o, final_g):
    h = x
    for l in range(DEPTH):
        u = rmsnorm(h, norm_g[l])
        proj = jnp.einsum("bsd,de->bse", u, w_in[l])
        (b_a, c_a, xin_a, z_a,
         val_b, gate_b, z_b,
         q_x, z_x,
         g_a, g_b, g_x) = jnp.split(proj, SPLIT_POINTS, axis=-1)

        ya = b_a * causal_dwconv(c_a * xin_a, conv_a_w[l])
        ya = jnp.einsum("bsc,cd->bsd", jax.nn.silu(z_a) * ya, w_out_a[l])

        yb = val_b * jax.nn.sigmoid(gate_b)
        yb = causal_dwconv(yb, conv_b_w[l]) + conv_b_b[l].astype(yb.dtype)
        yb = jax.nn.silu(layernorm(yb, ln_b_g[l], ln_b_b[l]))
        yb = jnp.einsum("bsc,cd->bsd", jax.nn.silu(z_b) * yb, w_out_b[l])

        mem_n = rmsnorm(mem, mem_norm_g[l])
        yx = memory_cross_attention(q_x, mem_n, w_kv[l])
        yx = jnp.einsum("bsc,cd->bsd", jax.nn.silu(z_x) * yx, w_out_x[l])

        merged = jax.nn.sigmoid(g_a) * ya + jax.nn.sigmoid(g_b) * yb + jax.nn.sigmoid(g_x) * yx
        h = h + jnp.einsum("bsd,de->bse", merged, w_o[l])
    return rmsnorm(h, final_g)


import jax as _jax
import jax.numpy as _jnp

TWIN_FORMAT = 'train_step'
FWD_PARAMS = ['x', 'mem', 'norm_g', 'w_in', 'conv_a_w', 'w_out_a', 'conv_b_w', 'conv_b_b', 'ln_b_g', 'ln_b_b', 'w_out_b', 'mem_norm_g', 'w_kv', 'w_out_x', 'w_o', 'final_g']
TWIN_WEIGHTS = ['norm_g', 'w_in', 'conv_a_w', 'w_out_a', 'conv_b_w', 'conv_b_b', 'ln_b_g', 'ln_b_b', 'w_out_b', 'mem_norm_g', 'w_kv', 'w_out_x', 'w_o', 'final_g']
TWIN_DIFF_INPUT = 'x'
TWIN_INPUTS = ['x', 'mem', 'norm_g', 'w_in', 'conv_a_w', 'w_out_a', 'conv_b_w', 'conv_b_b', 'ln_b_g', 'ln_b_b', 'w_out_b', 'mem_norm_g', 'w_kv', 'w_out_x', 'w_o', 'final_g', 'loss_target', 'm_norm_g', 'm_w_in', 'm_conv_a_w', 'm_w_out_a', 'm_conv_b_w', 'm_conv_b_b', 'm_ln_b_g', 'm_ln_b_b', 'm_w_out_b', 'm_mem_norm_g', 'm_w_kv', 'm_w_out_x', 'm_w_o', 'm_final_g', 'v_norm_g', 'v_w_in', 'v_conv_a_w', 'v_w_out_a', 'v_conv_b_w', 'v_conv_b_b', 'v_ln_b_g', 'v_ln_b_b', 'v_w_out_b', 'v_mem_norm_g', 'v_w_kv', 'v_w_out_x', 'v_w_o', 'v_final_g']
TWIN_OUTPUTS = ['loss', 'grad_x', 'grad_norm_g', 'grad_w_in', 'grad_conv_a_w', 'grad_w_out_a', 'grad_conv_b_w', 'grad_conv_b_b', 'grad_ln_b_g', 'grad_ln_b_b', 'grad_w_out_b', 'grad_mem_norm_g', 'grad_w_kv', 'grad_w_out_x', 'grad_w_o', 'grad_final_g', 'delta_norm_g', 'delta_w_in', 'delta_conv_a_w', 'delta_w_out_a', 'delta_conv_b_w', 'delta_conv_b_b', 'delta_ln_b_g', 'delta_ln_b_b', 'delta_w_out_b', 'delta_mem_norm_g', 'delta_w_kv', 'delta_w_out_x', 'delta_w_o', 'delta_final_g', 'new_m_norm_g', 'new_m_w_in', 'new_m_conv_a_w', 'new_m_w_out_a', 'new_m_conv_b_w', 'new_m_conv_b_b', 'new_m_ln_b_g', 'new_m_ln_b_b', 'new_m_w_out_b', 'new_m_mem_norm_g', 'new_m_w_kv', 'new_m_w_out_x', 'new_m_w_o', 'new_m_final_g', 'new_v_norm_g', 'new_v_w_in', 'new_v_conv_a_w', 'new_v_w_out_a', 'new_v_conv_b_w', 'new_v_conv_b_b', 'new_v_ln_b_g', 'new_v_ln_b_b', 'new_v_w_out_b', 'new_v_mem_norm_g', 'new_v_w_kv', 'new_v_w_out_x', 'new_v_w_o', 'new_v_final_g']
TWIN_LEAF_KINDS = {'loss': 'loss', 'grad_x': 'grad_x', 'grad_norm_g': 'grad_w', 'grad_w_in': 'grad_w', 'grad_conv_a_w': 'grad_w', 'grad_w_out_a': 'grad_w', 'grad_conv_b_w': 'grad_w', 'grad_conv_b_b': 'grad_w', 'grad_ln_b_g': 'grad_w', 'grad_ln_b_b': 'grad_w', 'grad_w_out_b': 'grad_w', 'grad_mem_norm_g': 'grad_w', 'grad_w_kv': 'grad_w', 'grad_w_out_x': 'grad_w', 'grad_w_o': 'grad_w', 'grad_final_g': 'grad_w', 'delta_norm_g': 'delta_w', 'delta_w_in': 'delta_w', 'delta_conv_a_w': 'delta_w', 'delta_w_out_a': 'delta_w', 'delta_conv_b_w': 'delta_w', 'delta_conv_b_b': 'delta_w', 'delta_ln_b_g': 'delta_w', 'delta_ln_b_b': 'delta_w', 'delta_w_out_b': 'delta_w', 'delta_mem_norm_g': 'delta_w', 'delta_w_kv': 'delta_w', 'delta_w_out_x': 'delta_w', 'delta_w_o': 'delta_w', 'delta_final_g': 'delta_w', 'new_m_norm_g': 'new_m', 'new_m_w_in': 'new_m', 'new_m_conv_a_w': 'new_m', 'new_m_w_out_a': 'new_m', 'new_m_conv_b_w': 'new_m', 'new_m_conv_b_b': 'new_m', 'new_m_ln_b_g': 'new_m', 'new_m_ln_b_b': 'new_m', 'new_m_w_out_b': 'new_m', 'new_m_mem_norm_g': 'new_m', 'new_m_w_kv': 'new_m', 'new_m_w_out_x': 'new_m', 'new_m_w_o': 'new_m', 'new_m_final_g': 'new_m', 'new_v_norm_g': 'new_v', 'new_v_w_in': 'new_v', 'new_v_conv_a_w': 'new_v', 'new_v_w_out_a': 'new_v', 'new_v_conv_b_w': 'new_v', 'new_v_conv_b_b': 'new_v', 'new_v_ln_b_g': 'new_v', 'new_v_ln_b_b': 'new_v', 'new_v_w_out_b': 'new_v', 'new_v_mem_norm_g': 'new_v', 'new_v_w_kv': 'new_v', 'new_v_w_out_x': 'new_v', 'new_v_w_o': 'new_v', 'new_v_final_g': 'new_v'}


def _forward(args):
    return _fwd_reference(*[args[k] for k in FWD_PARAMS])


def _output_shape():
    def fwd():
        inp = _fwd_setup_inputs(0)
        return _fwd_reference(*[inp[k] for k in FWD_PARAMS])
    out = _jax.eval_shape(fwd)
    return out.shape, out.dtype

N_MICROBATCH = 1
ADAM_LR = 0.001
ADAM_B1 = 0.9
ADAM_B2 = 0.999
ADAM_EPS = 1e-08
ADAM_WD = 0.01
ADAM_STEP = 10
PER_EXAMPLE_BATCH_AXIS = {'x': 0, 'mem': 0, 'loss_target': 0}
SHARED_INPUTS = []
_WEIGHT_DTYPES = {'norm_g': _jnp.float32, 'w_in': _jnp.float32, 'conv_a_w': _jnp.float32, 'w_out_a': _jnp.float32, 'conv_b_w': _jnp.float32, 'conv_b_b': _jnp.float32, 'ln_b_g': _jnp.float32, 'ln_b_b': _jnp.float32, 'w_out_b': _jnp.float32, 'mem_norm_g': _jnp.float32, 'w_kv': _jnp.float32, 'w_out_x': _jnp.float32, 'w_o': _jnp.float32, 'final_g': _jnp.float32}
MOMENT_SCALE = {'norm_g': 1.695613e-01, 'w_in': 4.939974e-02, 'conv_a_w': 7.575682e-02, 'w_out_a': 7.534518e-02, 'conv_b_w': 4.754516e-02, 'conv_b_b': 1.000991e-01, 'ln_b_g': 5.671043e-02, 'ln_b_b': 5.058775e-02, 'w_out_b': 4.566702e-02, 'mem_norm_g': 1.201133e-02, 'w_kv': 7.973945e-03, 'w_out_x': 8.037957e-03, 'w_o': 8.844687e-02, 'final_g': 6.393854e+01}


def _to_microbatches(a, axis):
    t = _jnp.moveaxis(a, axis, 0)
    t = t.reshape((N_MICROBATCH, t.shape[0] // N_MICROBATCH) + t.shape[1:])
    return _jnp.moveaxis(t, 1, axis + 1)


def setup_inputs(seed: int = 0) -> dict:
    inp = _fwd_setup_inputs(seed)
    key = _jax.random.fold_in(_jax.random.key(seed), 7919)
    shape, _ = _output_shape()
    out = dict(inp)
    out["loss_target"] = _jax.random.normal(_jax.random.fold_in(key, 0), shape, _jnp.float32)
    for i, name in enumerate(TWIN_WEIGHTS):
        w = inp[name].astype(_jnp.float32)
        if MOMENT_SCALE is None:
            s = _jnp.sqrt(_jnp.mean(_jnp.square(w)) + 1e-30)
        else:
            s = MOMENT_SCALE[name]
        km, kv = _jax.random.split(_jax.random.fold_in(key, i + 1))
        out[name] = w
        out["m_" + name] = s * _jax.random.normal(km, w.shape, _jnp.float32)
        out["v_" + name] = (s * s) * _jax.random.uniform(kv, w.shape, _jnp.float32, 0.5, 1.5)
    if N_MICROBATCH > 1:
        for name, axis in PER_EXAMPLE_BATCH_AXIS.items():
            out[name] = _to_microbatches(out[name], axis)
    return {'x': out['x'], 'mem': out['mem'], 'norm_g': out['norm_g'], 'w_in': out['w_in'], 'conv_a_w': out['conv_a_w'], 'w_out_a': out['w_out_a'], 'conv_b_w': out['conv_b_w'], 'conv_b_b': out['conv_b_b'], 'ln_b_g': out['ln_b_g'], 'ln_b_b': out['ln_b_b'], 'w_out_b': out['w_out_b'], 'mem_norm_g': out['mem_norm_g'], 'w_kv': out['w_kv'], 'w_out_x': out['w_out_x'], 'w_o': out['w_o'], 'final_g': out['final_g'], 'loss_target': out['loss_target'], 'm_norm_g': out['m_norm_g'], 'm_w_in': out['m_w_in'], 'm_conv_a_w': out['m_conv_a_w'], 'm_w_out_a': out['m_w_out_a'], 'm_conv_b_w': out['m_conv_b_w'], 'm_conv_b_b': out['m_conv_b_b'], 'm_ln_b_g': out['m_ln_b_g'], 'm_ln_b_b': out['m_ln_b_b'], 'm_w_out_b': out['m_w_out_b'], 'm_mem_norm_g': out['m_mem_norm_g'], 'm_w_kv': out['m_w_kv'], 'm_w_out_x': out['m_w_out_x'], 'm_w_o': out['m_w_o'], 'm_final_g': out['m_final_g'], 'v_norm_g': out['v_norm_g'], 'v_w_in': out['v_w_in'], 'v_conv_a_w': out['v_conv_a_w'], 'v_w_out_a': out['v_w_out_a'], 'v_conv_b_w': out['v_conv_b_w'], 'v_conv_b_b': out['v_conv_b_b'], 'v_ln_b_g': out['v_ln_b_g'], 'v_ln_b_b': out['v_ln_b_b'], 'v_w_out_b': out['v_w_out_b'], 'v_mem_norm_g': out['v_mem_norm_g'], 'v_w_kv': out['v_w_kv'], 'v_w_out_x': out['v_w_out_x'], 'v_w_o': out['v_w_o'], 'v_final_g': out['v_final_g']}


def _loss(weights, diff, rest, loss_target):
    with _jax.named_scope("forward"):
        args = {**rest, TWIN_DIFF_INPUT: diff, **{k: w.astype(_WEIGHT_DTYPES[k]) for k, w in weights.items()}}
        y = _forward(args)
    with _jax.named_scope("loss_head"):
        err = _jnp.square(y.astype(_jnp.float32) - loss_target)
        return 0.5 * _jnp.sum(_jnp.mean(err, axis=-1)) if err.ndim else 0.5 * err


def _adamw(w, g, m, v):
    m = ADAM_B1 * m + (1.0 - ADAM_B1) * g
    v = ADAM_B2 * v + (1.0 - ADAM_B2) * _jnp.square(g)
    m_hat = m / (1.0 - ADAM_B1 ** ADAM_STEP)
    v_hat = v / (1.0 - ADAM_B2 ** ADAM_STEP)
    delta = -ADAM_LR * (m_hat / (_jnp.sqrt(v_hat) + ADAM_EPS) + ADAM_WD * w)
    return delta, m, v


def reference(x, mem, norm_g, w_in, conv_a_w, w_out_a, conv_b_w, conv_b_b, ln_b_g, ln_b_b, w_out_b, mem_norm_g, w_kv, w_out_x, w_o, final_g, loss_target, m_norm_g, m_w_in, m_conv_a_w, m_w_out_a, m_conv_b_w, m_conv_b_b, m_ln_b_g, m_ln_b_b, m_w_out_b, m_mem_norm_g, m_w_kv, m_w_out_x, m_w_o, m_final_g, v_norm_g, v_w_in, v_conv_a_w, v_w_out_a, v_conv_b_w, v_conv_b_b, v_ln_b_g, v_ln_b_b, v_w_out_b, v_mem_norm_g, v_w_kv, v_w_out_x, v_w_o, v_final_g):
    given = dict(x=x, mem=mem, norm_g=norm_g, w_in=w_in, conv_a_w=conv_a_w, w_out_a=w_out_a, conv_b_w=conv_b_w, conv_b_b=conv_b_b, ln_b_g=ln_b_g, ln_b_b=ln_b_b, w_out_b=w_out_b, mem_norm_g=mem_norm_g, w_kv=w_kv, w_out_x=w_out_x, w_o=w_o, final_g=final_g, loss_target=loss_target, m_norm_g=m_norm_g, m_w_in=m_w_in, m_conv_a_w=m_conv_a_w, m_w_out_a=m_w_out_a, m_conv_b_w=m_conv_b_w, m_conv_b_b=m_conv_b_b, m_ln_b_g=m_ln_b_g, m_ln_b_b=m_ln_b_b, m_w_out_b=m_w_out_b, m_mem_norm_g=m_mem_norm_g, m_w_kv=m_w_kv, m_w_out_x=m_w_out_x, m_w_o=m_w_o, m_final_g=m_final_g, v_norm_g=v_norm_g, v_w_in=v_w_in, v_conv_a_w=v_conv_a_w, v_w_out_a=v_w_out_a, v_conv_b_w=v_conv_b_w, v_conv_b_b=v_conv_b_b, v_ln_b_g=v_ln_b_g, v_ln_b_b=v_ln_b_b, v_w_out_b=v_w_out_b, v_mem_norm_g=v_mem_norm_g, v_w_kv=v_w_kv, v_w_out_x=v_w_out_x, v_w_o=v_w_o, v_final_g=v_final_g)
    weights = {n: given[n] for n in TWIN_WEIGHTS}
    shared = {n: given[n] for n in SHARED_INPUTS}
    per_example = {n: given[n] for n in ['x', 'mem']}
    grad_fn = _jax.value_and_grad(_loss, argnums=(0, 1))

    def one_microbatch(ex, loss_target):
        ex = dict(ex)
        diff = ex.pop(TWIN_DIFF_INPUT)
        return grad_fn(weights, diff, {**shared, **ex}, loss_target)

    if N_MICROBATCH == 1:
        loss, (grad_w, grad_x) = one_microbatch(per_example, given["loss_target"])
    else:
        def body(carry, xs):
            loss_sum, grad_sum = carry
            l_k, (gw_k, gx_k) = one_microbatch(xs[0], xs[1])
            with _jax.named_scope("update"):
                return (loss_sum + l_k, _jax.tree.map(_jnp.add, grad_sum, gw_k)), gx_k

        init = (_jnp.zeros((), _jnp.float32), _jax.tree.map(_jnp.zeros_like, weights))
        (loss, grad_w), grad_x = _jax.lax.scan(body, init, (per_example, given["loss_target"]))
    with _jax.named_scope("update"):
        delta_w, new_m, new_v = {}, {}, {}
        for n in TWIN_WEIGHTS:
            delta_w[n], new_m[n], new_v[n] = _adamw(weights[n], grad_w[n], given["m_" + n], given["v_" + n])
    return (loss, grad_x, *[grad_w[n] for n in TWIN_WEIGHTS], *[delta_w[n] for n in TWIN_WEIGHTS],
            *[new_m[n] for n in TWIN_WEIGHTS], *[new_v[n] for n in TWIN_WEIGHTS])
```

```python
import functools

import jax
import jax.numpy as jnp
from jax import lax
from jax.experimental import pallas as pl
from jax.experimental.pallas import tpu as pltpu

F32, BF16 = jnp.float32, jnp.bfloat16
D = 1024
N_DEV = 8
N_HEADS = 4
HEAD_DIM = D // N_HEADS
N_GROUPS = 12
W_IN_SHARD = N_GROUPS * D // N_DEV
UNIT = 512
K_A, K_B = 3, 31
EPS = 1e-6
HALO = 32
CONV_CHUNK = 16
SUBLANES = 8
VMEM_LIMIT = 56 << 20
MESH = pl.DeviceIdType.MESH
ANY = pl.BlockSpec(memory_space=pl.ANY)

G_BA, G_CA, G_XA, G_ZA, G_VB, G_GB, G_ZB, G_Q, G_ZX, G_GA, G_GBB, G_GX = range(N_GROUPS)
DP_POS = (0, 1, 2, 3, 6, 7, 8, 4, 5, 9, 10, 11)

ADAM_LR, ADAM_B1, ADAM_B2, ADAM_EPS, ADAM_WD, ADAM_STEP = 0.001, 0.9, 0.999, 1e-08, 0.01, 10

SV_NORM_G, SV_CONV_B_B, SV_LN_G, SV_LN_B, SV_MEM_G, SV_FINAL_G, SV_LOSS = range(7)
SV_CONV_A, SV_CONV_B, SV_ROWS = 8, 16, 48


def _dot(a, b):
    return jnp.dot(a, b, preferred_element_type=F32)


def _dot_nt(a, b):
    return lax.dot_general(a, b, (((1,), (1,)), ((), ())), preferred_element_type=F32)


def _dot_tn(a, b):
    return lax.dot_general(a, b, (((0,), (0,)), ((), ())), preferred_element_type=F32)


def _silu_and_grad(z):
    s = jax.nn.sigmoid(z)
    return z * s, s * (1.0 + z * (1.0 - s))


def _fold8(a):
    return a.reshape(a.shape[0] // SUBLANES, SUBLANES, a.shape[1]).sum(axis=0)


def _mean(a):
    return jnp.mean(a, axis=-1, keepdims=True)


def _params(n_grid):
    return pltpu.CompilerParams(dimension_semantics=("arbitrary",) * n_grid, vmem_limit_bytes=VMEM_LIMIT)


def _rows(tm, col=0):
    return pl.BlockSpec((tm, D), lambda i: (i, col))


def _prev_halo(tm, col=0):
    return pl.BlockSpec((HALO, D), lambda i: (jnp.maximum(i * (tm // HALO) - 1, 0), col))


def _next_halo(tm, n_rows, col=0):
    last = n_rows // HALO - 1
    return pl.BlockSpec((HALO, D), lambda i: (jnp.minimum((i + 1) * (tm // HALO), last), col))


def _const(shape):
    return pl.BlockSpec(shape, lambda *_: (0,) * len(shape))


def _w_out_spec(which):
    return pl.BlockSpec((N_DEV, None, D // N_DEV, D), lambda *_: (0, which, 0, 0))


def _shift_slots(taps):
    shifts = sorted({off % SUBLANES for _, off in taps} - {0})
    return {s: n for n, s in enumerate(shifts)}


def _fill_shifted(g_ref, e_ref, slots):
    n = e_ref.shape[0] - SUBLANES
    for s, slot in slots.items():
        g_ref[slot, pl.ds(0, n), :] = e_ref[pl.ds(s, n), :]


def _tap_block(e_ref, g_ref, slots, base, off):
    q, s = divmod(off, SUBLANES)
    start = pl.multiple_of(base + SUBLANES * q, SUBLANES)
    if s == 0:
        return e_ref[pl.ds(start, CONV_CHUNK), :]
    return g_ref[slots[s], pl.ds(start, CONV_CHUNK), :]


def _conv(out_ref, e_ref, g_ref, w_ref, taps, tm, bias_ref=None):
    slots = _shift_slots(taps)
    _fill_shifted(g_ref, e_ref, slots)

    def chunk(c, carry):
        base = pl.multiple_of(c * CONV_CHUNK, CONV_CHUNK)
        acc = jnp.zeros((CONV_CHUNK, D), F32)
        for k, off in taps:
            acc = acc + w_ref[k:k + 1, :] * _tap_block(e_ref, g_ref, slots, base, off)
        if bias_ref is not None:
            acc = acc + bias_ref[...]
        out_ref[pl.ds(base, CONV_CHUNK), :] = acc
        return carry

    lax.fori_loop(0, tm // CONV_CHUNK, chunk, 0)


def _conv_wgrad(dw_ref, d_ref, e_ref, g_ref, taps, tm):
    slots = _shift_slots(taps)
    for k, off in taps:
        def chunk(c, acc, off=off):
            base = pl.multiple_of(c * CONV_CHUNK, CONV_CHUNK)
            prod = d_ref[pl.ds(base, CONV_CHUNK), :] * _tap_block(e_ref, g_ref, slots, base, off)
            return acc + _fold8(prod)

        dw_ref[k] += lax.fori_loop(0, tm // CONV_CHUNK, chunk, jnp.zeros((SUBLANES, D), F32))


FWD_TAPS_A = [(k, HALO - (K_A - 1) + k) for k in range(K_A)]
BWD_TAPS_A = [(k, K_A - 1 - k) for k in range(K_A)]
FWD_TAPS_B = [(k, HALO - (K_B - 1) + k) for k in range(K_B)]
BWD_TAPS_B = [(k, K_B - 1 - k) for k in range(K_B)]


def _n_slots(*tap_lists):
    return max(len(_shift_slots(t)) for t in tap_lists)


def _kv_fwd(mem, mem_g, wkv_g):
    m_len = mem.shape[0]

    def body(mem_ref, g_ref, w_ref, kv_ref, mn_ref):
        mf = mem_ref[...]
        r = lax.rsqrt(_mean(mf * mf) + EPS)
        mn = ((mf * r) * g_ref[...]).astype(BF16)
        mn_ref[...] = mn
        for b in range(2 * N_HEADS):
            kv_ref[b] = _dot(mn, w_ref[b]).astype(BF16)

    return pl.pallas_call(
        body, name="kv_fwd", grid=(1,),
        in_specs=[_const((m_len, D)), _const((1, D)), _const((2 * N_HEADS, D, HEAD_DIM))],
        out_specs=[_const((2 * N_HEADS, m_len, HEAD_DIM)), _const((m_len, D))],
        out_shape=[jax.ShapeDtypeStruct((2 * N_HEADS, m_len, HEAD_DIM), BF16), jax.ShapeDtypeStruct((m_len, D), BF16)],
        compiler_params=_params(1),
    )(mem, mem_g, wkv_g)


def _kv_bwd(dkv, mem, mem_g, mn16, wkv_g):
    m_len = mem.shape[0]

    def body(dkv_ref, mem_ref, g_ref, mn_ref, w_ref, dw_ref, dg_ref):
        mn = mn_ref[...]
        dmn = jnp.zeros((m_len, D), F32)
        for b in range(2 * N_HEADS):
            d16 = dkv_ref[b].astype(BF16)
            dw_ref[b] = _dot_tn(mn, d16)
            dmn = dmn + _dot_nt(d16, w_ref[b])
        mf = mem_ref[...]
        r = lax.rsqrt(_mean(mf * mf) + EPS)
        dg_ref[...] = _fold8(dmn * (mf * r))

    return pl.pallas_call(
        body, name="kv_bwd", grid=(1,),
        in_specs=[_const((2 * N_HEADS, m_len, HEAD_DIM)), _const((m_len, D)), _const((1, D)), _const((m_len, D)),
                  _const((2 * N_HEADS, D, HEAD_DIM))],
        out_specs=[_const((2 * N_HEADS, D, HEAD_DIM)), _const((SUBLANES, D))],
        out_shape=[jax.ShapeDtypeStruct((2 * N_HEADS, D, HEAD_DIM), F32), jax.ShapeDtypeStruct((SUBLANES, D), F32)],
        compiler_params=_params(1),
    )(dkv, mem, mem_g, mn16, wkv_g)


def _proj_fwd(x, norm_g, win_g, tm):
    n_rows = x.shape[0]

    def body(x_ref, g_ref, w_ref, proj_ref, ut_ref, u_scr):
        @pl.when(pl.program_id(1) == 0)
        def _():
            xf = x_ref[...]
            u = (xf * lax.rsqrt(_mean(xf * xf) + EPS)) * g_ref[...]
            u_scr[...] = u.astype(BF16)
            ut_ref[...] = u.T.astype(BF16)

        proj_ref[...] = _dot(u_scr[...], w_ref[0])

    return pl.pallas_call(
        body, name="proj_fwd", grid=(n_rows // tm, N_DEV),
        in_specs=[pl.BlockSpec((tm, D), lambda i, j: (i, 0)), _const((1, D)),
                  pl.BlockSpec((1, D, W_IN_SHARD), lambda i, j: (j, 0, 0))],
        out_specs=[pl.BlockSpec((tm, W_IN_SHARD), lambda i, j: (i, j)), pl.BlockSpec((D, tm), lambda i, j: (0, i))],
        out_shape=[jax.ShapeDtypeStruct((n_rows, N_GROUPS * D), F32), jax.ShapeDtypeStruct((D, n_rows), BF16)],
        scratch_shapes=[pltpu.VMEM((tm, D), BF16)],
        compiler_params=_params(2),
    )(x, norm_g, win_g)


def _branch_a_fwd(proj, wo4_g, cw_a, tm):
    n_rows = proj.shape[0]

    def body(bp, cp, xp, za, cph, xph, w_ref, cw_ref, sa_ref, ca_ref, ya_ref, e_scr, g_scr):
        i = pl.program_id(0)
        e_scr[pl.ds(0, HALO), :] = jnp.where(i > 0, cph[...] * xph[...], 0.0)
        e_scr[pl.ds(HALO, tm), :] = cp[...] * xp[...]
        _conv(ca_ref, e_scr, g_scr, cw_ref, FWD_TAPS_A, tm)
        sa = (jax.nn.silu(za[...]) * (bp[...] * ca_ref[...])).astype(BF16)
        sa_ref[...] = sa
        ya_ref[...] = _dot(sa, w_ref[...].reshape(D, D))

    return pl.pallas_call(
        body, name="branch_a_fwd", grid=(n_rows // tm,),
        in_specs=[_rows(tm, G_BA), _rows(tm, G_CA), _rows(tm, G_XA), _rows(tm, G_ZA),
                  _prev_halo(tm, G_CA), _prev_halo(tm, G_XA), _w_out_spec(0), _const((SUBLANES, D))],
        out_specs=[_rows(tm), _rows(tm), _rows(tm)],
        out_shape=[jax.ShapeDtypeStruct((n_rows, D), BF16), jax.ShapeDtypeStruct((n_rows, D), F32),
                   jax.ShapeDtypeStruct((n_rows, D), F32)],
        scratch_shapes=[pltpu.VMEM((tm + HALO, D), F32), pltpu.VMEM((_n_slots(FWD_TAPS_A), tm + HALO, D), F32)],
        compiler_params=_params(1),
    )(proj, proj, proj, proj, proj, proj, wo4_g, cw_a)


def _layernorm_parts(cb, lg, lb):
    xc = cb - _mean(cb)
    rstd = lax.rsqrt(_mean(xc * xc) + EPS)
    xhat = xc * rstd
    return xhat, rstd, xhat * lg + lb


def _branch_b_fwd(proj, wo4_g, cw_b, conv_b_b, ln_g, ln_b, tm):
    n_rows = proj.shape[0]

    def body(vb, gb, zb, vbh, gbh, w_ref, cw_ref, bb_ref, lg_ref, lb_ref, cb_ref, sb_ref, yb_ref, e_scr, g_scr):
        i = pl.program_id(0)
        e_scr[pl.ds(0, HALO), :] = jnp.where(i > 0, vbh[...] * jax.nn.sigmoid(gbh[...]), 0.0)
        e_scr[pl.ds(HALO, tm), :] = vb[...] * jax.nn.sigmoid(gb[...])
        _conv(cb_ref, e_scr, g_scr, cw_ref, FWD_TAPS_B, tm, bias_ref=bb_ref)
        _, _, ln = _layernorm_parts(cb_ref[...], lg_ref[...], lb_ref[...])
        sb = (jax.nn.silu(zb[...]) * jax.nn.silu(ln)).astype(BF16)
        sb_ref[...] = sb
        yb_ref[...] = _dot(sb, w_ref[...].reshape(D, D))

    return pl.pallas_call(
        body, name="branch_b_fwd", grid=(n_rows // tm,),
        in_specs=[_rows(tm, G_VB), _rows(tm, G_GB), _rows(tm, G_ZB), _prev_halo(tm, G_VB), _prev_halo(tm, G_GB),
                  _w_out_spec(1), _const((HALO, D)), _const((1, D)), _const((1, D)), _const((1, D))],
        out_specs=[_rows(tm), _rows(tm), _rows(tm)],
        out_shape=[jax.ShapeDtypeStruct((n_rows, D), F32), jax.ShapeDtypeStruct((n_rows, D), BF16),
                   jax.ShapeDtypeStruct((n_rows, D), F32)],
        scratch_shapes=[pltpu.VMEM((tm + HALO, D), F32), pltpu.VMEM((_n_slots(FWD_TAPS_B), tm + HALO, D), F32)],
        compiler_params=_params(1),
    )(proj, proj, proj, proj, proj, wo4_g, cw_b, conv_b_b, ln_g, ln_b)


def _attention(q16, kv_ref):
    probs, outs = [], []
    for h in range(N_HEADS):
        s = _dot_nt(q16[:, h * HEAD_DIM:(h + 1) * HEAD_DIM], kv_ref[h]) * (HEAD_DIM ** -0.5)
        e = jnp.exp(s - jnp.max(s, axis=-1, keepdims=True))
        p = e / jnp.sum(e, axis=-1, keepdims=True)
        probs.append(p)
        outs.append(_dot(p.astype(BF16), kv_ref[N_HEADS + h]))
    return probs, outs


def _branch_x_fwd(proj, kv16, wo4_g, tm):
    n_rows = proj.shape[0]

    def body(q, zx, kv_ref, w_ref, sx_ref, yx_ref):
        _, outs = _attention(q[...].astype(BF16), kv_ref)
        sx = (jax.nn.silu(zx[...]) * jnp.concatenate(outs, axis=-1)).astype(BF16)
        sx_ref[...] = sx
        yx_ref[...] = _dot(sx, w_ref[...].reshape(D, D))

    return pl.pallas_call(
        body, name="branch_x_fwd", grid=(n_rows // tm,),
        in_specs=[_rows(tm, G_Q), _rows(tm, G_ZX), _const(kv16.shape), _w_out_spec(2)],
        out_specs=[_rows(tm), _rows(tm)],
        out_shape=[jax.ShapeDtypeStruct((n_rows, D), BF16), jax.ShapeDtypeStruct((n_rows, D), F32)],
        compiler_params=_params(1),
    )(proj, proj, kv16, wo4_g)


def _merge_fwd_bwd(proj, ya, yb, yx, x, target, wo4_g, final_g, tm):
    n_rows = proj.shape[0]
    inv_d = 1.0 / D

    def body(ga, gb, gx, ya_ref, yb_ref, yx_ref, x_ref, t_ref, w_ref, fg_ref,
             dh_ref, dya_ref, dyb_ref, dyx_ref, dp_ref, dw_ref, dfg_ref, sq_ref):
        i = pl.program_id(0)
        wo = w_ref[...].reshape(D, D)
        sig = [jax.nn.sigmoid(g[...]) for g in (ga, gb, gx)]
        ys = [ya_ref[...], yb_ref[...], yx_ref[...]]
        m16 = (sig[0] * ys[0] + sig[1] * ys[1] + sig[2] * ys[2]).astype(BF16)
        h = x_ref[...] + _dot(m16, wo)
        r = lax.rsqrt(_mean(h * h) + EPS)
        hn = h * r
        fg = fg_ref[...]
        err = hn * fg - t_ref[...]
        dy = err * inv_d
        dhn = dy * fg
        dh = r * (dhn - hn * _mean(dhn * hn))
        dh_ref[...] = dh
        dh16 = dh.astype(BF16)
        dm = _dot_nt(dh16, wo)
        for n, out in enumerate((dya_ref, dyb_ref, dyx_ref)):
            out[...] = (sig[n] * dm).astype(BF16)
            dp_ref[:, n * D:(n + 1) * D] = (dm * ys[n] * (sig[n] * (1.0 - sig[n]))).astype(BF16)

        @pl.when(i == 0)
        def _():
            dw_ref[...] = jnp.zeros_like(dw_ref)
            dfg_ref[...] = jnp.zeros_like(dfg_ref)
            sq_ref[...] = jnp.zeros_like(sq_ref)

        dw_ref[0] += _dot_tn(m16, dh16)
        dfg_ref[...] += _fold8(dy * hn)
        sq_ref[...] += _fold8(err * err)

    vec = jax.ShapeDtypeStruct((SUBLANES, D), F32)
    return pl.pallas_call(
        body, name="merge_fwd_bwd", grid=(n_rows // tm,),
        in_specs=[_rows(tm, G_GA), _rows(tm, G_GBB), _rows(tm, G_GX), _rows(tm), _rows(tm), _rows(tm), _rows(tm),
                  _rows(tm), _w_out_spec(3), _const((1, D))],
        out_specs=[_rows(tm), _rows(tm), _rows(tm), _rows(tm), pl.BlockSpec((tm, 3 * D), lambda i: (i, 3)),
                   pl.BlockSpec((1, D, D), lambda i: (3, 0, 0)), _const((SUBLANES, D)), _const((SUBLANES, D))],
        out_shape=[jax.ShapeDtypeStruct((n_rows, D), F32), jax.ShapeDtypeStruct((n_rows, D), BF16),
                   jax.ShapeDtypeStruct((n_rows, D), BF16), jax.ShapeDtypeStruct((n_rows, D), BF16),
                   jax.ShapeDtypeStruct((n_rows, N_GROUPS * D), BF16), jax.ShapeDtypeStruct((4, D, D), F32), vec, vec],
        compiler_params=_params(1),
    )(proj, proj, proj, ya, yb, yx, x, target, wo4_g, final_g)


def _branch_a_bwd(dya, proj, sa16, wo4_g, cw_a, dproj, dw4, tm):
    n_rows = proj.shape[0]
    n_tiles = n_rows // tm

    def body(dya_ref, bp, cp, xp, za, sa_ref, dyan, bpn, zan, cph, xph, w_ref, cw_ref, dp_in, dw_in,
             dp_ref, dw_ref, dwa_ref, e1, e2, g_scr, d_scr):
        del dp_in, dw_in
        i = pl.program_id(0)
        woa = w_ref[...].reshape(D, D)
        dya16 = dya_ref[...]
        e1[pl.ds(0, HALO), :] = jnp.where(i > 0, cph[...] * xph[...], 0.0)
        e1[pl.ds(HALO, tm), :] = cp[...] * xp[...]
        _conv(d_scr, e1, g_scr, cw_ref, FWD_TAPS_A, tm)
        ca = d_scr[...]
        dsa = _dot_nt(dya16, woa)
        silu_z, dsilu_z = _silu_and_grad(za[...])
        t = dsa * silu_z
        dp_ref[:, 0 * D:1 * D] = (t * ca).astype(BF16)
        dp_ref[:, 3 * D:4 * D] = (dsa * (bp[...] * ca) * dsilu_z).astype(BF16)
        dca = t * bp[...]
        d_scr[...] = dca

        @pl.when(i == 0)
        def _():
            dw_ref[...] = jnp.zeros_like(dw_ref)
            dwa_ref[...] = jnp.zeros_like(dwa_ref)

        _conv_wgrad(dwa_ref, d_scr, e1, g_scr, FWD_TAPS_A, tm)
        dw_ref[0] += _dot_tn(sa_ref[...], dya16)
        dsan = _dot_nt(dyan[...], woa)
        dcan = (dsan * jax.nn.silu(zan[...])) * bpn[...]
        e2[pl.ds(0, tm), :] = dca
        e2[pl.ds(tm, HALO), :] = jnp.where(i < n_tiles - 1, dcan, 0.0)
        _conv(d_scr, e2, g_scr, cw_ref, BWD_TAPS_A, tm)
        dprod = d_scr[...]
        dp_ref[:, 1 * D:2 * D] = (dprod * xp[...]).astype(BF16)
        dp_ref[:, 2 * D:3 * D] = (dprod * cp[...]).astype(BF16)

    return pl.pallas_call(
        body, name="branch_a_bwd", grid=(n_tiles,),
        in_specs=[_rows(tm), _rows(tm, G_BA), _rows(tm, G_CA), _rows(tm, G_XA), _rows(tm, G_ZA), _rows(tm),
                  _next_halo(tm, n_rows), _next_halo(tm, n_rows, G_BA), _next_halo(tm, n_rows, G_ZA),
                  _prev_halo(tm, G_CA), _prev_halo(tm, G_XA), _w_out_spec(0), _const((SUBLANES, D)), ANY, ANY],
        out_specs=[pl.BlockSpec((tm, 4 * D), lambda i: (i, 0)), pl.BlockSpec((1, D, D), lambda i: (0, 0, 0)),
                   _const((K_A, SUBLANES, D))],
        out_shape=[jax.ShapeDtypeStruct(dproj.shape, BF16), jax.ShapeDtypeStruct(dw4.shape, F32),
                   jax.ShapeDtypeStruct((K_A, SUBLANES, D), F32)],
        input_output_aliases={13: 0, 14: 1},
        scratch_shapes=[pltpu.VMEM((tm + HALO, D), F32), pltpu.VMEM((tm + HALO, D), F32),
                        pltpu.VMEM((_n_slots(FWD_TAPS_A, BWD_TAPS_A), tm + HALO, D), F32), pltpu.VMEM((tm, D), F32)],
        compiler_params=_params(1),
    )(dya, proj, proj, proj, proj, sa16, dya, proj, proj, proj, proj, wo4_g, cw_a, dproj, dw4)


def _branch_b_bwd(dyb, proj, cb, sb16, wo4_g, cw_b, ln_g, ln_b, dproj, dw4, tm):
    n_rows = proj.shape[0]
    n_tiles = n_rows // tm

    def body(dyb_ref, zb, cb_ref, vb, gb, sb_ref, dybn, zbn, cbn, vbh, gbh, w_ref, cw_ref, lg_ref, lb_ref,
             dp_in, dw_in, dp_ref, dw_ref, dwb_ref, dbb_ref, dlg_ref, dlb_ref, e1, e2, g_scr, d_scr):
        del dp_in, dw_in
        i = pl.program_id(0)
        wob = w_ref[...].reshape(D, D)
        lg, lb = lg_ref[...], lb_ref[...]

        def conv_out_grad(dy16, z, c):
            dsb = _dot_nt(dy16, wob)
            xhat, rstd, ln = _layernorm_parts(c, lg, lb)
            sw, dsw = _silu_and_grad(ln)
            sz, dsz = _silu_and_grad(z)
            dln = (dsb * sz) * dsw
            dxhat = dln * lg
            dcb = rstd * (dxhat - _mean(dxhat) - xhat * _mean(dxhat * xhat))
            return dsb * sw * dsz, dln, xhat, dcb

        dyb16 = dyb_ref[...]
        dzb, dln, xhat, dcb = conv_out_grad(dyb16, zb[...], cb_ref[...])
        dp_ref[:, 2 * D:3 * D] = dzb.astype(BF16)
        d_scr[...] = dcb

        @pl.when(i == 0)
        def _():
            dw_ref[...] = jnp.zeros_like(dw_ref)
            dwb_ref[...] = jnp.zeros_like(dwb_ref)
            dbb_ref[...] = jnp.zeros_like(dbb_ref)
            dlg_ref[...] = jnp.zeros_like(dlg_ref)
            dlb_ref[...] = jnp.zeros_like(dlb_ref)

        dlg_ref[...] += _fold8(dln * xhat)
        dlb_ref[...] += _fold8(dln)
        dbb_ref[...] += _fold8(dcb)
        dw_ref[0] += _dot_tn(sb_ref[...], dyb16)
        sg = jax.nn.sigmoid(gb[...])
        e1[pl.ds(0, HALO), :] = jnp.where(i > 0, vbh[...] * jax.nn.sigmoid(gbh[...]), 0.0)
        e1[pl.ds(HALO, tm), :] = vb[...] * sg
        _fill_shifted(g_scr, e1, _shift_slots(FWD_TAPS_B))
        _conv_wgrad(dwb_ref, d_scr, e1, g_scr, FWD_TAPS_B, tm)
        _, _, _, dcbn = conv_out_grad(dybn[...], zbn[...], cbn[...])
        e2[pl.ds(0, tm), :] = dcb
        e2[pl.ds(tm, HALO), :] = jnp.where(i < n_tiles - 1, dcbn, 0.0)
        _conv(d_scr, e2, g_scr, cw_ref, BWD_TAPS_B, tm)
        dglu = d_scr[...]
        dp_ref[:, 0 * D:1 * D] = (dglu * sg).astype(BF16)
        dp_ref[:, 1 * D:2 * D] = (dglu * vb[...] * (sg * (1.0 - sg))).astype(BF16)

    vec = jax.ShapeDtypeStruct((SUBLANES, D), F32)
    return pl.pallas_call(
        body, name="branch_b_bwd", grid=(n_tiles,),
        in_specs=[_rows(tm), _rows(tm, G_ZB), _rows(tm), _rows(tm, G_VB), _rows(tm, G_GB), _rows(tm),
                  _next_halo(tm, n_rows), _next_halo(tm, n_rows, G_ZB), _next_halo(tm, n_rows),
                  _prev_halo(tm, G_VB), _prev_halo(tm, G_GB), _w_out_spec(1), _const((HALO, D)), _const((1, D)),
                  _const((1, D)), ANY, ANY],
        out_specs=[pl.BlockSpec((tm, 3 * D), lambda i: (i, 2)), pl.BlockSpec((1, D, D), lambda i: (1, 0, 0)),
                   _const((K_B, SUBLANES, D)), _const((SUBLANES, D)), _const((SUBLANES, D)), _const((SUBLANES, D))],
        out_shape=[jax.ShapeDtypeStruct(dproj.shape, BF16), jax.ShapeDtypeStruct(dw4.shape, F32),
                   jax.ShapeDtypeStruct((K_B, SUBLANES, D), F32), vec, vec, vec],
        input_output_aliases={15: 0, 16: 1},
        scratch_shapes=[pltpu.VMEM((tm + HALO, D), F32), pltpu.VMEM((tm + HALO, D), F32),
                        pltpu.VMEM((_n_slots(FWD_TAPS_B, BWD_TAPS_B), tm + HALO, D), F32), pltpu.VMEM((tm, D), F32)],
        compiler_params=_params(1),
    )(dyb, proj, cb, proj, proj, sb16, dyb, proj, cb, proj, proj, wo4_g, cw_b, ln_g, ln_b, dproj, dw4)


def _branch_x_bwd(dyx, proj, sx16, kv16, wo4_g, dproj, dw4, tm):
    n_rows = proj.shape[0]
    scale = HEAD_DIM ** -0.5

    def body(dyx_ref, q, zx, sx_ref, kv_ref, w_ref, dp_in, dw_in, dp_ref, dw_ref, dkv_ref):
        del dp_in, dw_in
        i = pl.program_id(0)
        dyx16 = dyx_ref[...]
        q16 = q[...].astype(BF16)
        probs, outs = _attention(q16, kv_ref)
        dsx = _dot_nt(dyx16, w_ref[...].reshape(D, D))
        silu_z, dsilu_z = _silu_and_grad(zx[...])
        dp_ref[:, D:2 * D] = (dsx * jnp.concatenate(outs, axis=-1) * dsilu_z).astype(BF16)
        do16 = (dsx * silu_z).astype(BF16)

        @pl.when(i == 0)
        def _():
            dw_ref[...] = jnp.zeros_like(dw_ref)
            dkv_ref[...] = jnp.zeros_like(dkv_ref)

        for h in range(N_HEADS):
            cols = slice(h * HEAD_DIM, (h + 1) * HEAD_DIM)
            p = probs[h]
            dprob = _dot_nt(do16[:, cols], kv_ref[N_HEADS + h])
            ds16 = ((p * (dprob - jnp.sum(p * dprob, axis=-1, keepdims=True))) * scale).astype(BF16)
            dp_ref[:, cols] = _dot(ds16, kv_ref[h]).astype(BF16)
            dkv_ref[h] += _dot_tn(ds16, q16[:, cols])
            dkv_ref[N_HEADS + h] += _dot_tn(p.astype(BF16), do16[:, cols])
        dw_ref[0] += _dot_tn(sx_ref[...], dyx16)

    return pl.pallas_call(
        body, name="branch_x_bwd", grid=(n_rows // tm,),
        in_specs=[_rows(tm), _rows(tm, G_Q), _rows(tm, G_ZX), _rows(tm), _const(kv16.shape), _w_out_spec(2), ANY, ANY],
        out_specs=[pl.BlockSpec((tm, 2 * D), lambda i: (i, 2)), pl.BlockSpec((1, D, D), lambda i: (2, 0, 0)),
                   _const(kv16.shape)],
        out_shape=[jax.ShapeDtypeStruct(dproj.shape, BF16), jax.ShapeDtypeStruct(dw4.shape, F32),
                   jax.ShapeDtypeStruct(kv16.shape, F32)],
        input_output_aliases={6: 0, 7: 1},
        compiler_params=_params(1),
    )(dyx, proj, proj, sx16, kv16, wo4_g, dproj, dw4)


def _dp_unit(u):
    g = u // 2
    pos = jnp.where(g < G_VB, g, jnp.where(g < G_Q, g + 2, jnp.where(g < G_GA, g - 3, g)))
    return 2 * pos + u % 2


def _w_in_grad(ut, dproj, tk):
    n_rows = dproj.shape[0]
    n_k = n_rows // tk
    per_shard = W_IN_SHARD // UNIT

    def body(ut_ref, dp_ref, out_ref, acc):
        t = pl.program_id(1)

        @pl.when(t == 0)
        def _():
            acc[...] = jnp.zeros_like(acc)

        acc[...] += _dot(ut_ref[...], dp_ref[...])

        @pl.when(t == n_k - 1)
        def _():
            out_ref[0] = acc[...]

    return pl.pallas_call(
        body, name="w_in_grad", grid=(N_GROUPS * D // UNIT, n_k),
        in_specs=[pl.BlockSpec((D, tk), lambda u, t: (0, t)), pl.BlockSpec((tk, UNIT), lambda u, t: (t, _dp_unit(u)))],
        out_specs=pl.BlockSpec((1, D, UNIT), lambda u, t: (u // per_shard, 0, u % per_shard)),
        out_shape=jax.ShapeDtypeStruct((N_DEV, D, W_IN_SHARD), F32),
        scratch_shapes=[pltpu.VMEM((D, UNIT), F32)],
        compiler_params=_params(2),
    )(ut, dproj)


def _x_grad(dproj, win_g, x, dh, norm_g, tm):
    n_rows = x.shape[0]
    per_shard = W_IN_SHARD // UNIT

    def w_unit(half):
        def index(i, g):
            u = 2 * g + half
            return (u // per_shard, 0, u % per_shard)
        return pl.BlockSpec((1, D, UNIT), index)

    def dp_index(i, g):
        return (i, _dp_unit(2 * g) // 2)

    def body(dp_ref, wa_ref, wb_ref, x_ref, dh_ref, g_ref, gx_ref, dg_ref, acc):
        i, g = pl.program_id(0), pl.program_id(1)

        @pl.when(g == 0)
        def _():
            acc[...] = jnp.zeros_like(acc)

        acc[...] += _dot_nt(dp_ref[:, :UNIT], wa_ref[0]) + _dot_nt(dp_ref[:, UNIT:], wb_ref[0])

        @pl.when((i == 0) & (g == 0))
        def _():
            dg_ref[...] = jnp.zeros_like(dg_ref)

        @pl.when(g == N_GROUPS - 1)
        def _():
            du = acc[...]
            xf = x_ref[...]
            r = lax.rsqrt(_mean(xf * xf) + EPS)
            xn = xf * r
            dun = du * g_ref[...]
            gx_ref[...] = dh_ref[...] + r * (dun - xn * _mean(dun * xn))
            dg_ref[...] += _fold8(du * xn)

    return pl.pallas_call(
        body, name="x_grad", grid=(n_rows // tm, N_GROUPS),
        in_specs=[pl.BlockSpec((tm, D), dp_index), w_unit(0), w_unit(1), pl.BlockSpec((tm, D), lambda i, g: (i, 0)),
                  pl.BlockSpec((tm, D), lambda i, g: (i, 0)), _const((1, D))],
        out_specs=[pl.BlockSpec((tm, D), lambda i, g: (i, 0)), _const((SUBLANES, D))],
        out_shape=[jax.ShapeDtypeStruct((n_rows, D), F32), jax.ShapeDtypeStruct((SUBLANES, D), F32)],
        scratch_shapes=[pltpu.VMEM((tm, D), F32)],
        compiler_params=_params(2),
    )(dproj, win_g, win_g, x, dh, norm_g)


def _local_step(x, mem, target, norm_g, conv_b_b, ln_g, ln_b, mem_g, final_g, win_g, wkv_g, wo4_g, cw_a, cw_b):
    n_rows = x.shape[0]
    tm = min(256, n_rows)
    big = min(1024, n_rows)
    kv16, mn16 = _kv_fwd(mem, mem_g, wkv_g)
    proj, ut = _proj_fwd(x, norm_g, win_g, big)
    sa16, _, ya = _branch_a_fwd(proj, wo4_g, cw_a, tm)
    cb, sb16, yb = _branch_b_fwd(proj, wo4_g, cw_b, conv_b_b, ln_g, ln_b, tm)
    sx16, yx = _branch_x_fwd(proj, kv16, wo4_g, tm)
    dh, dya, dyb, dyx, dproj, dw4, dfg, sq = _merge_fwd_bwd(proj, ya, yb, yx, x, target, wo4_g, final_g, tm)
    dproj, dw4, dwa = _branch_a_bwd(dya, proj, sa16, wo4_g, cw_a, dproj, dw4, tm)
    dproj, dw4, dwb, dbb, dlg, dlb = _branch_b_bwd(dyb, proj, cb, sb16, wo4_g, cw_b, ln_g, ln_b, dproj, dw4, tm)
    dproj, dw4, dkv = _branch_x_bwd(dyx, proj, sx16, kv16, wo4_g, dproj, dw4, tm)
    dwkv_g, dmg = _kv_bwd(dkv, mem, mem_g, mn16, wkv_g)
    dwin_g = _w_in_grad(ut, dproj, min(4096, n_rows))
    gx, dng = _x_grad(dproj, win_g, x, dh, norm_g, big)
    small = {SV_NORM_G: dng, SV_CONV_B_B: dbb, SV_LN_G: dlg, SV_LN_B: dlb, SV_MEM_G: dmg, SV_FINAL_G: dfg, SV_LOSS: sq}
    return gx, dwin_g, dw4, dwkv_g, small, dwa, dwb


def _place():
    x, y, c = lax.axis_index("x"), lax.axis_index("y"), lax.axis_index("c")
    other_chips = [(1 - x, y), (x, 1 - y), (1 - x, 1 - y)]
    return x, y, c, other_chips


def _allgather(blocks):
    n = len(blocks)

    def body(*refs):
        src, out, (send, recv, local) = refs[:n], refs[n:2 * n], refs[2 * n:]
        x, y, c, chips = _place()
        me, sibling = 4 * x + 2 * y + c, (x, y, 1 - c)

        def copy(t, k, block, to, from_input=False):
            return pltpu.make_async_remote_copy(
                src_ref=src[t] if from_input else out[t].at[block], dst_ref=out[t].at[block],
                send_sem=send.at[t, k], recv_sem=recv.at[t, k], device_id=to, device_id_type=MESH)

        own = [pltpu.make_async_copy(src[t], out[t].at[me], local.at[t]) for t in range(n)]
        first = []
        for t in range(n):
            first.append(copy(t, 0, me, sibling, from_input=True))
            first += [copy(t, 1 + j, me, (*chip, c), from_input=True) for j, chip in enumerate(chips)]
        for cp in own + first:
            cp.start()
        passed = []
        for j, (px, py) in enumerate(chips):
            for t in range(n):
                block = 4 * px + 2 * py + c
                copy(t, 1 + j, block, sibling).wait_recv()
                passed.append(copy(t, 4 + j, block, sibling))
                passed[-1].start()
        for t in range(n):
            copy(t, 0, 4 * x + 2 * y + 1 - c, sibling).wait_recv()
            for j, (px, py) in enumerate(chips):
                copy(t, 4 + j, 4 * px + 2 * py + 1 - c, sibling).wait_recv()
        for cp in first + passed:
            cp.wait_send()
        for cp in own:
            cp.wait()

    return pl.pallas_call(
        body, name="allgather_weights",
        in_specs=[ANY] * n, out_specs=[ANY] * n,
        out_shape=[jax.ShapeDtypeStruct((N_DEV, *b.shape), b.dtype) for b in blocks],
        scratch_shapes=[pltpu.SemaphoreType.DMA((n, 7)), pltpu.SemaphoreType.DMA((n, 7)), pltpu.SemaphoreType.DMA((n,))],
    )(*blocks)


def _pieces(arrays):
    return [(a, l) for a, arr in enumerate(arrays) for l in range(arr.shape[0])]


def _send_to_sibling(grads):
    n = len(grads)
    pieces = _pieces(grads)

    def body(*refs):
        src, out, (send, recv) = refs[:n], refs[n:2 * n], refs[2 * n:]
        x, y, c, _ = _place()
        copies = []
        for p, (a, l) in enumerate(pieces):
            for s in range(4):
                copies.append(pltpu.make_async_remote_copy(
                    src_ref=src[a].at[l, 2 * s + 1 - c], dst_ref=out[a].at[l, s],
                    send_sem=send.at[p, s], recv_sem=recv.at[p, s], device_id=(x, y, 1 - c), device_id_type=MESH))
        for cp in copies:
            cp.start()
        for cp in copies:
            cp.wait_recv()
        for cp in copies:
            cp.wait_send()

    return pl.pallas_call(
        body, name="grads_to_sibling",
        in_specs=[ANY] * n, out_specs=[ANY] * n,
        out_shape=[jax.ShapeDtypeStruct((g.shape[0], 4, *g.shape[2:]), F32) for g in grads],
        scratch_shapes=[pltpu.SemaphoreType.DMA((len(pieces), 4)), pltpu.SemaphoreType.DMA((len(pieces), 4))],
    )(*grads)


def _chip_sum(grad, landed, c_arr, tr):
    n_l, _, n_r, n_c = grad.shape

    def body(c_ref, g_ref, l_ref, s32_ref, s16_ref):
        del c_ref
        total = g_ref[...] + l_ref[...]
        s32_ref[...] = total
        s16_ref[...] = total.astype(BF16)

    blk = (1, 1, tr, n_c)
    return pl.pallas_call(
        body, name="chip_sum",
        grid_spec=pltpu.PrefetchScalarGridSpec(
            num_scalar_prefetch=1, grid=(n_l, 4, n_r // tr),
            in_specs=[pl.BlockSpec(blk, lambda l, s, r, c_ref: (l, 2 * s + c_ref[0], r, 0)),
                      pl.BlockSpec(blk, lambda l, s, r, c_ref: (l, s, r, 0))],
            out_specs=[pl.BlockSpec(blk, lambda l, s, r, c_ref: (l, s, r, 0)),
                       pl.BlockSpec(blk, lambda l, s, r, c_ref: (l, s, r, 0))]),
        out_shape=[jax.ShapeDtypeStruct((n_l, 4, n_r, n_c), F32), jax.ShapeDtypeStruct((n_l, 4, n_r, n_c), BF16)],
        compiler_params=_params(3),
    )(c_arr, grad, landed)


def _send_to_chips(sums):
    n = len(sums)
    pieces = _pieces(sums)

    def body(*refs):
        src, out, (send, recv) = refs[:n], refs[n:2 * n], refs[2 * n:]
        _, _, c, chips = _place()
        copies = []
        for p, (a, l) in enumerate(pieces):
            for j, (px, py) in enumerate(chips):
                copies.append(pltpu.make_async_remote_copy(
                    src_ref=src[a].at[l, 2 * px + py], dst_ref=out[a].at[l, j],
                    send_sem=send.at[p, j], recv_sem=recv.at[p, j], device_id=(px, py, c), device_id_type=MESH))
        for cp in copies:
            cp.start()
        for cp in copies:
            cp.wait_recv()
        for cp in copies:
            cp.wait_send()

    return pl.pallas_call(
        body, name="grads_to_chips",
        in_specs=[ANY] * n, out_specs=[ANY] * n,
        out_shape=[jax.ShapeDtypeStruct((s.shape[0], 3, *s.shape[2:]), BF16) for s in sums],
        scratch_shapes=[pltpu.SemaphoreType.DMA((len(pieces), 3)), pltpu.SemaphoreType.DMA((len(pieces), 3))],
    )(*sums)


def _allgather_small(small, dwa, dwb):
    keys = sorted(small)

    def body(*refs):
        parts, (dwa_ref, dwb_ref, out_ref, mine, send, recv) = refs[:len(keys)], refs[len(keys):]
        x, y, c, chips = _place()
        me, sibling = 4 * x + 2 * y + c, (x, y, 1 - c)
        mine[...] = jnp.zeros_like(mine)
        for key, part in zip(keys, parts):
            mine[key:key + 1, :] = jnp.sum(part[...], axis=0, keepdims=True)
        for k in range(K_A):
            mine[SV_CONV_A + k:SV_CONV_A + k + 1, :] = jnp.sum(dwa_ref[k], axis=0, keepdims=True)
        for k in range(K_B):
            mine[SV_CONV_B + k:SV_CONV_B + k + 1, :] = jnp.sum(dwb_ref[k], axis=0, keepdims=True)
        out_ref[me] = mine[...]

        def copy(k, block, to, from_mine=False):
            return pltpu.make_async_remote_copy(
                src_ref=mine if from_mine else out_ref.at[block], dst_ref=out_ref.at[block],
                send_sem=send.at[k], recv_sem=recv.at[k], device_id=to, device_id_type=MESH)

        first = [copy(0, me, sibling, from_mine=True)]
        first += [copy(1 + j, me, (*chip, c), from_mine=True) for j, chip in enumerate(chips)]
        for cp in first:
            cp.start()
        passed = []
        for j, (px, py) in enumerate(chips):
            block = 4 * px + 2 * py + c
            copy(1 + j, block, sibling).wait_recv()
            passed.append(copy(4 + j, block, sibling))
            passed[-1].start()
        copy(0, 4 * x + 2 * y + 1 - c, sibling).wait_recv()
        for j, (px, py) in enumerate(chips):
            copy(4 + j, 4 * px + 2 * py + 1 - c, sibling).wait_recv()
        for cp in first + passed:
            cp.wait_send()

    vmem = pl.BlockSpec(memory_space=pltpu.VMEM)
    return pl.pallas_call(
        body, name="allgather_small",
        in_specs=[vmem] * (len(keys) + 2), out_specs=vmem,
        out_shape=jax.ShapeDtypeStruct((N_DEV, SV_ROWS, D), F32),
        scratch_shapes=[pltpu.VMEM((SV_ROWS, D), F32), pltpu.SemaphoreType.DMA((7,)), pltpu.SemaphoreType.DMA((7,))],
    )(*[small[k] for k in keys], dwa, dwb)


def _adamw(w, g, m, v):
    m = ADAM_B1 * m + (1.0 - ADAM_B1) * g
    v = ADAM_B2 * v + (1.0 - ADAM_B2) * (g * g)
    m_hat = m / (1.0 - ADAM_B1 ** ADAM_STEP)
    v_hat = v / (1.0 - ADAM_B2 ** ADAM_STEP)
    return -ADAM_LR * (m_hat / (jnp.sqrt(v_hat) + ADAM_EPS) + ADAM_WD * w), m, v


def _adamw_shard(own, landed, piece, slot_arr, w, m, v, tr):
    n_r, n_c = w.shape

    def body(slot_ref, own_ref, l0, l1, l2, w_ref, m_ref, v_ref, g_out, d_out, m_out, v_out):
        del slot_ref
        g = own_ref[0, 0]
        for landed_ref in (l0, l1, l2):
            g = g + landed_ref[0, 0].astype(F32)
        g_out[...] = g
        d_out[...], m_out[...], v_out[...] = _adamw(w_ref[...], g, m_ref[...], v_ref[...])

    blk = (1, 1, tr, n_c)
    flat = pl.BlockSpec((tr, n_c), lambda r, s: (r, 0))
    return pl.pallas_call(
        body, name="adamw_shard",
        grid_spec=pltpu.PrefetchScalarGridSpec(
            num_scalar_prefetch=1, grid=(n_r // tr,),
            in_specs=[pl.BlockSpec(blk, lambda r, s: (piece, s[0], r, 0))]
            + [pl.BlockSpec(blk, functools.partial(lambda r, s, j: (piece, j, r, 0), j=j)) for j in range(3)]
            + [flat] * 3,
            out_specs=[flat] * 4),
        out_shape=[jax.ShapeDtypeStruct((n_r, n_c), F32)] * 4,
        compiler_params=_params(1),
    )(slot_arr, own, landed, landed, landed, w, m, v)


def _adamw_small(gathered, k_arr, vectors, conv_a, conv_b):
    n_vec = len(vectors)
    cols = D // N_DEV

    def body(k_ref, full_ref, cols_ref, *refs):
        del k_ref
        ins, outs = refs[:3 * (n_vec + 2)], refs[3 * (n_vec + 2):-2]
        tot, tot_cols = refs[-2:]
        tot[...] = full_ref[0]
        tot_cols[...] = cols_ref[0]
        for dev in range(1, N_DEV):
            tot[...] += full_ref[dev]
            tot_cols[...] += cols_ref[dev]
        loss = (0.5 / D) * jnp.sum(tot[SV_LOSS:SV_LOSS + 1, :])
        outs[0][...] = jnp.full(outs[0].shape, loss, F32)
        grads = [tot[n:n + 1, :] for n in range(n_vec)]
        grads += [tot_cols[pl.ds(SV_CONV_A, K_A), :], tot_cols[pl.ds(SV_CONV_B, K_B), :]]
        for n, g in enumerate(grads):
            w_ref, m_ref, v_ref = ins[3 * n:3 * n + 3]
            g_out, d_out, m_out, v_out = outs[1 + 4 * n:5 + 4 * n]
            g_out[...] = g
            d_out[...], m_out[...], v_out[...] = _adamw(w_ref[...], g, m_ref[...], v_ref[...])

    weights = list(vectors) + [conv_a, conv_b]
    flat_in = [a for wmv in weights for a in wmv]
    out_shape = [jax.ShapeDtypeStruct((SUBLANES, 128), F32)]
    for wmv in weights:
        out_shape += [jax.ShapeDtypeStruct(wmv[0].shape, F32)] * 4
    return pl.pallas_call(
        body, name="adamw_small",
        grid_spec=pltpu.PrefetchScalarGridSpec(
            num_scalar_prefetch=1, grid=(1,),
            in_specs=[pl.BlockSpec((N_DEV, SV_ROWS, D), lambda i, k: (0, 0, 0)),
                      pl.BlockSpec((N_DEV, SV_ROWS, cols), lambda i, k: (0, 0, k[0]))]
            + [pl.BlockSpec(a.shape, lambda i, k: (0, 0)) for a in flat_in],
            out_specs=[pl.BlockSpec(s.shape, lambda i, k: (0, 0)) for s in out_shape],
            scratch_shapes=[pltpu.VMEM((SV_ROWS, D), F32), pltpu.VMEM((SV_ROWS, cols), F32)]),
        out_shape=out_shape,
        compiler_params=_params(1),
    )(k_arr, gathered, gathered, *flat_in)


def kernel(x, mem, norm_g, w_in, conv_a_w, w_out_a, conv_b_w, conv_b_b, ln_b_g, ln_b_b, w_out_b, mem_norm_g, w_kv, w_out_x, w_o, final_g, loss_target, m_norm_g, m_w_in, m_conv_a_w, m_w_out_a, m_conv_b_w, m_conv_b_b, m_ln_b_g, m_ln_b_b, m_w_out_b, m_mem_norm_g, m_w_kv, m_w_out_x, m_w_o, m_final_g, v_norm_g, v_w_in, v_conv_a_w, v_w_out_a, v_conv_b_w, v_conv_b_b, v_ln_b_g, v_ln_b_b, v_w_out_b, v_mem_norm_g, v_w_kv, v_w_out_x, v_w_o, v_final_g):
    xi, yi, ci = lax.axis_index("x"), lax.axis_index("y"), lax.axis_index("c")
    c_arr = jnp.reshape(ci, (1,)).astype(jnp.int32)
    slot_arr = jnp.reshape(2 * xi + yi, (1,)).astype(jnp.int32)
    k_arr = jnp.reshape(4 * xi + 2 * yi + ci, (1,)).astype(jnp.int32)

    cw = jnp.concatenate([jnp.pad(conv_a_w[0], ((0, SUBLANES - K_A), (0, 0))),
                          jnp.pad(conv_b_w[0], ((0, HALO - K_B), (0, 0)))], axis=0)
    wo4 = jnp.stack([w_out_a[0], w_out_b[0], w_out_x[0], w_o[0]]).astype(BF16)
    win_g, wkv_g, wo4_g, cw_g = _allgather([w_in[0].astype(BF16), w_kv[0].astype(BF16), wo4, cw])
    cw_full = cw_g.transpose(1, 0, 2).reshape(SUBLANES + HALO, D)
    cw_a, cw_b = cw_full[:SUBLANES], cw_full[SUBLANES:]
    wkv_heads = wkv_g

    final_g2 = final_g.reshape(1, D)
    gx, dwin_g, dw4, dwkv_g, small, dwa, dwb = _local_step(
        x[0], mem[0], loss_target[0], norm_g, conv_b_b, ln_b_g, ln_b_b, mem_norm_g, final_g2,
        win_g, wkv_heads, wo4_g, cw_a, cw_b)

    grads = [dwin_g[None], dw4.reshape(4, N_DEV, D // N_DEV, D), dwkv_g[None]]
    tiles = [256, D // N_DEV, 256]
    from_sibling = _send_to_sibling(grads)
    sums = [_chip_sum(g, l, c_arr, tr) for g, l, tr in zip(grads, from_sibling, tiles)]
    from_chips = _send_to_chips([s16 for _, s16 in sums])
    gathered_small = _allgather_small(small, dwa, dwb)

    def shard(a, l, w, m, v):
        return _adamw_shard(sums[a][0], from_chips[a], l, slot_arr, w[0], m[0], v[0], tiles[a])

    res = {
        "w_in": shard(0, 0, w_in, m_w_in, v_w_in),
        "w_out_a": shard(1, 0, w_out_a, m_w_out_a, v_w_out_a),
        "w_out_b": shard(1, 1, w_out_b, m_w_out_b, v_w_out_b),
        "w_out_x": shard(1, 2, w_out_x, m_w_out_x, v_w_out_x),
        "w_o": shard(1, 3, w_o, m_w_o, v_w_o),
        "w_kv": shard(2, 0, w_kv, m_w_kv, v_w_kv),
    }
    res = {name: tuple(r[None] for r in four) for name, four in res.items()}
    vectors = [(norm_g, m_norm_g, v_norm_g), (conv_b_b, m_conv_b_b, v_conv_b_b), (ln_b_g, m_ln_b_g, v_ln_b_g),
               (ln_b_b, m_ln_b_b, v_ln_b_b), (mem_norm_g, m_mem_norm_g, v_mem_norm_g),
               (final_g2, m_final_g.reshape(1, D), v_final_g.reshape(1, D))]
    out = _adamw_small(gathered_small, k_arr, vectors, (conv_a_w[0], m_conv_a_w[0], v_conv_a_w[0]),
                       (conv_b_w[0], m_conv_b_w[0], v_conv_b_w[0]))
    loss = out[0][0, 0]
    names = ["norm_g", "conv_b_b", "ln_b_g", "ln_b_b", "mem_norm_g", "final_g", "conv_a_w", "conv_b_w"]
    for n, name in enumerate(names):
        four = out[1 + 4 * n:5 + 4 * n]
        if name == "final_g":
            four = [r.reshape(D) for r in four]
        elif name.startswith("conv_") and name.endswith("_w"):
            four = [r[None] for r in four]
        res[name] = tuple(four)

    order = ["norm_g", "w_in", "conv_a_w", "w_out_a", "conv_b_w", "conv_b_b", "ln_b_g", "ln_b_b", "w_out_b",
             "mem_norm_g", "w_kv", "w_out_x", "w_o", "final_g"]
    return (loss, gx[None], *[res[n][0] for n in order], *[res[n][1] for n in order],
            *[res[n][2] for n in order], *[res[n][3] for n in order])
```

```python
import functools

import jax
import jax.numpy as jnp
from jax import lax
from jax.experimental import pallas as pl
from jax.experimental.pallas import tpu as pltpu

F32, BF16 = jnp.float32, jnp.bfloat16
D = 1024
N_DEV = 8
N_HEADS = 4
HEAD_DIM = D // N_HEADS
N_GROUPS = 12
W_IN_SHARD = N_GROUPS * D // N_DEV
UNIT = 512
K_A, K_B = 3, 31
EPS = 1e-6
HALO = 32
SUBLANES = 8
LANES = 128
LANE_GROUPS = D // LANES
UNROLL_A, UNROLL_B = 8, 4
WGRAD_TAPS = 16
VMEM_LIMIT = 56 << 20
MESH = pl.DeviceIdType.MESH
ANY = pl.BlockSpec(memory_space=pl.ANY)

G_BA, G_CA, G_XA, G_ZA, G_VB, G_GB, G_ZB, G_Q, G_ZX, G_GA, G_GBB, G_GX = range(N_GROUPS)
DP_POS = (0, 1, 2, 3, 6, 7, 8, 4, 5, 9, 10, 11)

ADAM_LR, ADAM_B1, ADAM_B2, ADAM_EPS, ADAM_WD, ADAM_STEP = 0.001, 0.9, 0.999, 1e-08, 0.01, 10

SV_NORM_G, SV_CONV_B_B, SV_LN_G, SV_LN_B, SV_MEM_G, SV_FINAL_G, SV_LOSS = range(7)
SV_CONV_A, SV_CONV_B, SV_ROWS = 8, 16, 48


def _dot(a, b):
    return jnp.dot(a, b, preferred_element_type=F32)


def _dot_nt(a, b):
    return lax.dot_general(a, b, (((1,), (1,)), ((), ())), preferred_element_type=F32)


def _dot_tn(a, b):
    return lax.dot_general(a, b, (((0,), (0,)), ((), ())), preferred_element_type=F32)


def _silu_and_grad(z):
    s = jax.nn.sigmoid(z)
    return z * s, s * (1.0 + z * (1.0 - s))


def _fold8(a):
    return a.reshape(a.shape[0] // SUBLANES, SUBLANES, a.shape[1]).sum(axis=0)


def _mean(a):
    return jnp.mean(a, axis=-1, keepdims=True)


def _params(n_grid):
    return pltpu.CompilerParams(dimension_semantics=("arbitrary",) * n_grid, vmem_limit_bytes=VMEM_LIMIT)


def _rows(tm, col=0):
    return pl.BlockSpec((tm, D), lambda i: (i, col))


def _prev_halo(tm, col=0):
    return pl.BlockSpec((HALO, D), lambda i: (jnp.maximum(i * (tm // HALO) - 1, 0), col))


def _next_halo(tm, n_rows, col=0):
    last = n_rows // HALO - 1
    return pl.BlockSpec((HALO, D), lambda i: (jnp.minimum((i + 1) * (tm // HALO), last), col))


def _const(shape):
    return pl.BlockSpec(shape, lambda *_: (0,) * len(shape))


def _w_out_spec(which):
    return pl.BlockSpec((N_DEV, None, D // N_DEV, D), lambda *_: (0, which, 0, 0))


def _to_time_major(t_ref, row0, x):
    n = x.shape[0]
    for j in range(LANE_GROUPS):
        t_ref[pl.ds(row0 * LANE_GROUPS + j, n, stride=LANE_GROUPS), :] = x[:, j * LANES:(j + 1) * LANES]


def _from_time_major(t_ref, n):
    return jnp.concatenate([t_ref[pl.ds(j, n, stride=LANE_GROUPS), :] for j in range(LANE_GROUPS)], axis=-1)


def _row(ref, t):
    start = t * LANE_GROUPS
    if not isinstance(start, int):
        start = pl.multiple_of(start, LANE_GROUPS)
    return ref[pl.ds(start, LANE_GROUPS), :]


def _conv(o_ref, e_ref, w_ref, taps, n_rows, unroll, bias_ref=None):
    lo = min(off for _, off in taps)
    hi = max(off for _, off in taps)

    def chunk(c, carry):
        t0 = c * unroll
        window = [_row(e_ref, t0 + lo + n) for n in range(hi - lo + unroll)]
        acc = [[None, None] for _ in range(unroll)]
        for n, (k, off) in enumerate(taps):
            wk = _row(w_ref, k)
            for u in range(unroll):
                term = wk * window[off - lo + u]
                acc[u][n % 2] = term if acc[u][n % 2] is None else acc[u][n % 2] + term
        for u in range(unroll):
            out = acc[u][0] + acc[u][1]
            if bias_ref is not None:
                out = out + bias_ref[...]
            o_ref[pl.ds(pl.multiple_of((t0 + u) * LANE_GROUPS, LANE_GROUPS), LANE_GROUPS), :] = out
        return carry

    lax.fori_loop(0, n_rows // unroll, chunk, 0)


def _conv_wgrad(dw_ref, d_ref, e_ref, taps, n_rows, unroll):
    for first in range(0, len(taps), WGRAD_TAPS):
        group = taps[first:first + WGRAD_TAPS]
        lo = min(off for _, off in group)
        hi = max(off for _, off in group)

        def chunk(c, accs, group=group, lo=lo, hi=hi):
            t0 = c * unroll
            d = [_row(d_ref, t0 + u) for u in range(unroll)]
            window = [_row(e_ref, t0 + lo + n) for n in range(hi - lo + unroll)]
            accs = list(accs)
            for n, (_, off) in enumerate(group):
                for u in range(unroll):
                    accs[n] = accs[n] + d[u] * window[off - lo + u]
            return tuple(accs)

        zeros = tuple(jnp.zeros((LANE_GROUPS, LANES), F32) for _ in group)
        accs = lax.fori_loop(0, n_rows // unroll, chunk, zeros)
        for (k, _), acc in zip(group, accs):
            dw_ref[pl.ds(k * LANE_GROUPS, LANE_GROUPS), :] += acc


FWD_TAPS_A = [(k, HALO - (K_A - 1) + k) for k in range(K_A)]
BWD_TAPS_A = [(k, K_A - 1 - k) for k in range(K_A)]
FWD_TAPS_B = [(k, HALO - (K_B - 1) + k) for k in range(K_B)]
BWD_TAPS_B = [(k, K_B - 1 - k) for k in range(K_B)]


def _time_major(n_rows):
    return pltpu.VMEM((n_rows * LANE_GROUPS, LANES), F32)


def _kv_fwd(mem, mem_g, wkv_g):
    m_len = mem.shape[0]

    def body(mem_ref, g_ref, w_ref, kv_ref, mn_ref):
        mf = mem_ref[...]
        r = lax.rsqrt(_mean(mf * mf) + EPS)
        mn = ((mf * r) * g_ref[...]).astype(BF16)
        mn_ref[...] = mn
        for b in range(2 * N_HEADS):
            kv_ref[b] = _dot(mn, w_ref[b]).astype(BF16)

    return pl.pallas_call(
        body, name="kv_fwd", grid=(1,),
        in_specs=[_const((m_len, D)), _const((1, D)), _const((2 * N_HEADS, D, HEAD_DIM))],
        out_specs=[_const((2 * N_HEADS, m_len, HEAD_DIM)), _const((m_len, D))],
        out_shape=[jax.ShapeDtypeStruct((2 * N_HEADS, m_len, HEAD_DIM), BF16), jax.ShapeDtypeStruct((m_len, D), BF16)],
        compiler_params=_params(1),
    )(mem, mem_g, wkv_g)


def _kv_bwd(dkv, mem, mem_g, mn16, wkv_g):
    m_len = mem.shape[0]

    def body(dkv_ref, mem_ref, g_ref, mn_ref, w_ref, dw_ref, dg_ref):
        mn = mn_ref[...]
        dmn = jnp.zeros((m_len, D), F32)
        for b in range(2 * N_HEADS):
            d16 = dkv_ref[b].astype(BF16)
            dw_ref[b] = _dot_tn(mn, d16)
            dmn = dmn + _dot_nt(d16, w_ref[b])
        mf = mem_ref[...]
        r = lax.rsqrt(_mean(mf * mf) + EPS)
        dg_ref[...] = _fold8(dmn * (mf * r))

    return pl.pallas_call(
        body, name="kv_bwd", grid=(1,),
        in_specs=[_const((2 * N_HEADS, m_len, HEAD_DIM)), _const((m_len, D)), _const((1, D)), _const((m_len, D)),
                  _const((2 * N_HEADS, D, HEAD_DIM))],
        out_specs=[_const((2 * N_HEADS, D, HEAD_DIM)), _const((SUBLANES, D))],
        out_shape=[jax.ShapeDtypeStruct((2 * N_HEADS, D, HEAD_DIM), F32), jax.ShapeDtypeStruct((SUBLANES, D), F32)],
        compiler_params=_params(1),
    )(dkv, mem, mem_g, mn16, wkv_g)


def _proj_fwd(x, norm_g, win_g, tm):
    n_rows = x.shape[0]

    def body(x_ref, g_ref, w_ref, proj_ref, ut_ref, u_scr):
        @pl.when(pl.program_id(1) == 0)
        def _():
            xf = x_ref[...]
            u = (xf * lax.rsqrt(_mean(xf * xf) + EPS)) * g_ref[...]
            u_scr[...] = u.astype(BF16)
            ut_ref[...] = u.T.astype(BF16)

        proj_ref[...] = _dot(u_scr[...], w_ref[0])

    return pl.pallas_call(
        body, name="proj_fwd", grid=(n_rows // tm, N_DEV),
        in_specs=[pl.BlockSpec((tm, D), lambda i, j: (i, 0)), _const((1, D)),
                  pl.BlockSpec((1, D, W_IN_SHARD), lambda i, j: (j, 0, 0))],
        out_specs=[pl.BlockSpec((tm, W_IN_SHARD), lambda i, j: (i, j)), pl.BlockSpec((D, tm), lambda i, j: (0, i))],
        out_shape=[jax.ShapeDtypeStruct((n_rows, N_GROUPS * D), F32), jax.ShapeDtypeStruct((D, n_rows), BF16)],
        scratch_shapes=[pltpu.VMEM((tm, D), BF16)],
        compiler_params=_params(2),
    )(x, norm_g, win_g)


def _branch_a_fwd(proj, wo4_g, cw_a, tm):
    n_rows = proj.shape[0]

    def body(bp, cp, xp, za, cph, xph, w_ref, cw_ref, sa_ref, ya_ref, e_scr, o_scr):
        i = pl.program_id(0)
        _to_time_major(e_scr, 0, jnp.where(i > 0, cph[...] * xph[...], 0.0))
        _to_time_major(e_scr, HALO, cp[...] * xp[...])
        _conv(o_scr, e_scr, cw_ref, FWD_TAPS_A, tm, UNROLL_A)
        sa = (jax.nn.silu(za[...]) * (bp[...] * _from_time_major(o_scr, tm))).astype(BF16)
        sa_ref[...] = sa
        ya_ref[...] = _dot(sa, w_ref[...].reshape(D, D))

    return pl.pallas_call(
        body, name="branch_a_fwd", grid=(n_rows // tm,),
        in_specs=[_rows(tm, G_BA), _rows(tm, G_CA), _rows(tm, G_XA), _rows(tm, G_ZA),
                  _prev_halo(tm, G_CA), _prev_halo(tm, G_XA), _w_out_spec(0), _const(cw_a.shape)],
        out_specs=[_rows(tm), _rows(tm)],
        out_shape=[jax.ShapeDtypeStruct((n_rows, D), BF16), jax.ShapeDtypeStruct((n_rows, D), F32)],
        scratch_shapes=[_time_major(tm + HALO), _time_major(tm)],
        compiler_params=_params(1),
    )(proj, proj, proj, proj, proj, proj, wo4_g, cw_a)


def _layernorm_parts(cb, lg, lb):
    xc = cb - _mean(cb)
    rstd = lax.rsqrt(_mean(xc * xc) + EPS)
    xhat = xc * rstd
    return xhat, rstd, xhat * lg + lb


def _branch_b_fwd(proj, wo4_g, cw_b, conv_b_b, ln_g, ln_b, tm):
    n_rows = proj.shape[0]

    def body(vb, gb, zb, vbh, gbh, w_ref, cw_ref, bb_ref, lg_ref, lb_ref, cb_ref, sb_ref, yb_ref, e_scr, o_scr):
        i = pl.program_id(0)
        _to_time_major(e_scr, 0, jnp.where(i > 0, vbh[...] * jax.nn.sigmoid(gbh[...]), 0.0))
        _to_time_major(e_scr, HALO, vb[...] * jax.nn.sigmoid(gb[...]))
        _conv(o_scr, e_scr, cw_ref, FWD_TAPS_B, tm, UNROLL_B, bias_ref=bb_ref)
        cb = _from_time_major(o_scr, tm)
        cb_ref[...] = cb
        _, _, ln = _layernorm_parts(cb, lg_ref[...], lb_ref[...])
        sb = (jax.nn.silu(zb[...]) * jax.nn.silu(ln)).astype(BF16)
        sb_ref[...] = sb
        yb_ref[...] = _dot(sb, w_ref[...].reshape(D, D))

    return pl.pallas_call(
        body, name="branch_b_fwd", grid=(n_rows // tm,),
        in_specs=[_rows(tm, G_VB), _rows(tm, G_GB), _rows(tm, G_ZB), _prev_halo(tm, G_VB), _prev_halo(tm, G_GB),
                  _w_out_spec(1), _const(cw_b.shape), _const((LANE_GROUPS, LANES)), _const((1, D)), _const((1, D))],
        out_specs=[_rows(tm), _rows(tm), _rows(tm)],
        out_shape=[jax.ShapeDtypeStruct((n_rows, D), F32), jax.ShapeDtypeStruct((n_rows, D), BF16),
                   jax.ShapeDtypeStruct((n_rows, D), F32)],
        scratch_shapes=[_time_major(tm + HALO), _time_major(tm)],
        compiler_params=_params(1),
    )(proj, proj, proj, proj, proj, wo4_g, cw_b, conv_b_b.reshape(LANE_GROUPS, LANES), ln_g, ln_b)


def _attention(q16, kv_ref):
    probs, outs = [], []
    for h in range(N_HEADS):
        s = _dot_nt(q16[:, h * HEAD_DIM:(h + 1) * HEAD_DIM], kv_ref[h]) * (HEAD_DIM ** -0.5)
        e = jnp.exp(s - jnp.max(s, axis=-1, keepdims=True))
        p = e / jnp.sum(e, axis=-1, keepdims=True)
        probs.append(p)
        outs.append(_dot(p.astype(BF16), kv_ref[N_HEADS + h]))
    return probs, outs


def _branch_x_fwd(proj, kv16, wo4_g, tm):
    n_rows = proj.shape[0]

    def body(q, zx, kv_ref, w_ref, sx_ref, yx_ref):
        _, outs = _attention(q[...].astype(BF16), kv_ref)
        sx = (jax.nn.silu(zx[...]) * jnp.concatenate(outs, axis=-1)).astype(BF16)
        sx_ref[...] = sx
        yx_ref[...] = _dot(sx, w_ref[...].reshape(D, D))

    return pl.pallas_call(
        body, name="branch_x_fwd", grid=(n_rows // tm,),
        in_specs=[_rows(tm, G_Q), _rows(tm, G_ZX), _const(kv16.shape), _w_out_spec(2)],
        out_specs=[_rows(tm), _rows(tm)],
        out_shape=[jax.ShapeDtypeStruct((n_rows, D), BF16), jax.ShapeDtypeStruct((n_rows, D), F32)],
        compiler_params=_params(1),
    )(proj, proj, kv16, wo4_g)


def _merge_fwd_bwd(proj, ya, yb, yx, x, target, wo4_g, final_g, tm):
    n_rows = proj.shape[0]
    inv_d = 1.0 / D

    def body(ga, gb, gx, ya_ref, yb_ref, yx_ref, x_ref, t_ref, w_ref, fg_ref,
             dh_ref, dya_ref, dyb_ref, dyx_ref, dp_ref, dw_ref, dfg_ref, sq_ref):
        i = pl.program_id(0)
        wo = w_ref[...].reshape(D, D)
        sig = [jax.nn.sigmoid(g[...]) for g in (ga, gb, gx)]
        ys = [ya_ref[...], yb_ref[...], yx_ref[...]]
        m16 = (sig[0] * ys[0] + sig[1] * ys[1] + sig[2] * ys[2]).astype(BF16)
        h = x_ref[...] + _dot(m16, wo)
        r = lax.rsqrt(_mean(h * h) + EPS)
        hn = h * r
        fg = fg_ref[...]
        err = hn * fg - t_ref[...]
        dy = err * inv_d
        dhn = dy * fg
        dh = r * (dhn - hn * _mean(dhn * hn))
        dh_ref[...] = dh
        dh16 = dh.astype(BF16)
        dm = _dot_nt(dh16, wo)
        for n, out in enumerate((dya_ref, dyb_ref, dyx_ref)):
            out[...] = (sig[n] * dm).astype(BF16)
            dp_ref[:, n * D:(n + 1) * D] = (dm * ys[n] * (sig[n] * (1.0 - sig[n]))).astype(BF16)

        @pl.when(i == 0)
        def _():
            dw_ref[...] = jnp.zeros_like(dw_ref)
            dfg_ref[...] = jnp.zeros_like(dfg_ref)
            sq_ref[...] = jnp.zeros_like(sq_ref)

        dw_ref[0] += _dot_tn(m16, dh16)
        dfg_ref[...] += _fold8(dy * hn)
        sq_ref[...] += _fold8(err * err)

    vec = jax.ShapeDtypeStruct((SUBLANES, D), F32)
    return pl.pallas_call(
        body, name="merge_fwd_bwd", grid=(n_rows // tm,),
        in_specs=[_rows(tm, G_GA), _rows(tm, G_GBB), _rows(tm, G_GX), _rows(tm), _rows(tm), _rows(tm), _rows(tm),
                  _rows(tm), _w_out_spec(3), _const((1, D))],
        out_specs=[_rows(tm), _rows(tm), _rows(tm), _rows(tm), pl.BlockSpec((tm, 3 * D), lambda i: (i, 3)),
                   pl.BlockSpec((1, D, D), lambda i: (3, 0, 0)), _const((SUBLANES, D)), _const((SUBLANES, D))],
        out_shape=[jax.ShapeDtypeStruct((n_rows, D), F32), jax.ShapeDtypeStruct((n_rows, D), BF16),
                   jax.ShapeDtypeStruct((n_rows, D), BF16), jax.ShapeDtypeStruct((n_rows, D), BF16),
                   jax.ShapeDtypeStruct((n_rows, N_GROUPS * D), BF16), jax.ShapeDtypeStruct((4, D, D), F32), vec, vec],
        compiler_params=_params(1),
    )(proj, proj, proj, ya, yb, yx, x, target, wo4_g, final_g)


def _branch_a_bwd(dya, proj, sa16, wo4_g, cw_a, dproj, dw4, tm):
    n_rows = proj.shape[0]
    n_tiles = n_rows // tm

    def body(dya_ref, bp, cp, xp, za, sa_ref, dyan, bpn, zan, cph, xph, w_ref, cw_ref, dp_in, dw_in,
             dp_ref, dw_ref, dwa_ref, e1, e2, o_scr):
        del dp_in, dw_in
        i = pl.program_id(0)
        woa = w_ref[...].reshape(D, D)
        dya16 = dya_ref[...]
        _to_time_major(e1, 0, jnp.where(i > 0, cph[...] * xph[...], 0.0))
        _to_time_major(e1, HALO, cp[...] * xp[...])
        _conv(o_scr, e1, cw_ref, FWD_TAPS_A, tm, UNROLL_A)
        ca = _from_time_major(o_scr, tm)
        dsa = _dot_nt(dya16, woa)
        silu_z, dsilu_z = _silu_and_grad(za[...])
        t = dsa * silu_z
        dp_ref[:, 0 * D:1 * D] = (t * ca).astype(BF16)
        dp_ref[:, 3 * D:4 * D] = (dsa * (bp[...] * ca) * dsilu_z).astype(BF16)
        dsan = _dot_nt(dyan[...], woa)
        dcan = (dsan * jax.nn.silu(zan[...])) * bpn[...]
        _to_time_major(e2, 0, t * bp[...])
        _to_time_major(e2, tm, jnp.where(i < n_tiles - 1, dcan, 0.0))

        @pl.when(i == 0)
        def _():
            dw_ref[...] = jnp.zeros_like(dw_ref)
            dwa_ref[...] = jnp.zeros_like(dwa_ref)

        _conv_wgrad(dwa_ref, e2, e1, FWD_TAPS_A, tm, UNROLL_A)
        dw_ref[0] += _dot_tn(sa_ref[...], dya16)
        _conv(o_scr, e2, cw_ref, BWD_TAPS_A, tm, UNROLL_A)
        dprod = _from_time_major(o_scr, tm)
        dp_ref[:, 1 * D:2 * D] = (dprod * xp[...]).astype(BF16)
        dp_ref[:, 2 * D:3 * D] = (dprod * cp[...]).astype(BF16)

    return pl.pallas_call(
        body, name="branch_a_bwd", grid=(n_tiles,),
        in_specs=[_rows(tm), _rows(tm, G_BA), _rows(tm, G_CA), _rows(tm, G_XA), _rows(tm, G_ZA), _rows(tm),
                  _next_halo(tm, n_rows), _next_halo(tm, n_rows, G_BA), _next_halo(tm, n_rows, G_ZA),
                  _prev_halo(tm, G_CA), _prev_halo(tm, G_XA), _w_out_spec(0), _const(cw_a.shape), ANY, ANY],
        out_specs=[pl.BlockSpec((tm, 4 * D), lambda i: (i, 0)), pl.BlockSpec((1, D, D), lambda i: (0, 0, 0)),
                   _const((K_A * LANE_GROUPS, LANES))],
        out_shape=[jax.ShapeDtypeStruct(dproj.shape, BF16), jax.ShapeDtypeStruct(dw4.shape, F32),
                   jax.ShapeDtypeStruct((K_A * LANE_GROUPS, LANES), F32)],
        input_output_aliases={13: 0, 14: 1},
        scratch_shapes=[_time_major(tm + HALO), _time_major(tm + HALO), _time_major(tm)],
        compiler_params=_params(1),
    )(dya, proj, proj, proj, proj, sa16, dya, proj, proj, proj, proj, wo4_g, cw_a, dproj, dw4)


def _branch_b_bwd(dyb, proj, cb, sb16, wo4_g, cw_b, ln_g, ln_b, dproj, dw4, tm):
    n_rows = proj.shape[0]
    n_tiles = n_rows // tm

    def body(dyb_ref, zb, cb_ref, vb, gb, sb_ref, dybn, zbn, cbn, vbh, gbh, w_ref, cw_ref, lg_ref, lb_ref,
             dp_in, dw_in, dp_ref, dw_ref, dwb_ref, dbb_ref, dlg_ref, dlb_ref, e1, e2, o_scr):
        del dp_in, dw_in
        i = pl.program_id(0)
        wob = w_ref[...].reshape(D, D)
        lg, lb = lg_ref[...], lb_ref[...]

        def conv_out_grad(dy16, z, c):
            dsb = _dot_nt(dy16, wob)
            xhat, rstd, ln = _layernorm_parts(c, lg, lb)
            sw, dsw = _silu_and_grad(ln)
            sz, dsz = _silu_and_grad(z)
            dln = (dsb * sz) * dsw
            dxhat = dln * lg
            dcb = rstd * (dxhat - _mean(dxhat) - xhat * _mean(dxhat * xhat))
            return dsb * sw * dsz, dln, xhat, dcb

        dyb16 = dyb_ref[...]
        dzb, dln, xhat, dcb = conv_out_grad(dyb16, zb[...], cb_ref[...])
        dp_ref[:, 2 * D:3 * D] = dzb.astype(BF16)
        _, _, _, dcbn = conv_out_grad(dybn[...], zbn[...], cbn[...])
        _to_time_major(e2, 0, dcb)
        _to_time_major(e2, tm, jnp.where(i < n_tiles - 1, dcbn, 0.0))

        @pl.when(i == 0)
        def _():
            dw_ref[...] = jnp.zeros_like(dw_ref)
            dwb_ref[...] = jnp.zeros_like(dwb_ref)
            dbb_ref[...] = jnp.zeros_like(dbb_ref)
            dlg_ref[...] = jnp.zeros_like(dlg_ref)
            dlb_ref[...] = jnp.zeros_like(dlb_ref)

        dlg_ref[...] += _fold8(dln * xhat)
        dlb_ref[...] += _fold8(dln)
        dbb_ref[...] += _fold8(dcb)
        dw_ref[0] += _dot_tn(sb_ref[...], dyb16)
        sg = jax.nn.sigmoid(gb[...])
        _to_time_major(e1, 0, jnp.where(i > 0, vbh[...] * jax.nn.sigmoid(gbh[...]), 0.0))
        _to_time_major(e1, HALO, vb[...] * sg)
        _conv_wgrad(dwb_ref, e2, e1, FWD_TAPS_B, tm, UNROLL_B)
        _conv(o_scr, e2, cw_ref, BWD_TAPS_B, tm, UNROLL_B)
        dglu = _from_time_major(o_scr, tm)
        dp_ref[:, 0 * D:1 * D] = (dglu * sg).astype(BF16)
        dp_ref[:, 1 * D:2 * D] = (dglu * vb[...] * (sg * (1.0 - sg))).astype(BF16)

    vec = jax.ShapeDtypeStruct((SUBLANES, D), F32)
    return pl.pallas_call(
        body, name="branch_b_bwd", grid=(n_tiles,),
        in_specs=[_rows(tm), _rows(tm, G_ZB), _rows(tm), _rows(tm, G_VB), _rows(tm, G_GB), _rows(tm),
                  _next_halo(tm, n_rows), _next_halo(tm, n_rows, G_ZB), _next_halo(tm, n_rows),
                  _prev_halo(tm, G_VB), _prev_halo(tm, G_GB), _w_out_spec(1), _const(cw_b.shape), _const((1, D)),
                  _const((1, D)), ANY, ANY],
        out_specs=[pl.BlockSpec((tm, 3 * D), lambda i: (i, 2)), pl.BlockSpec((1, D, D), lambda i: (1, 0, 0)),
                   _const((K_B * LANE_GROUPS, LANES)), _const((SUBLANES, D)), _const((SUBLANES, D)),
                   _const((SUBLANES, D))],
        out_shape=[jax.ShapeDtypeStruct(dproj.shape, BF16), jax.ShapeDtypeStruct(dw4.shape, F32),
                   jax.ShapeDtypeStruct((K_B * LANE_GROUPS, LANES), F32), vec, vec, vec],
        input_output_aliases={15: 0, 16: 1},
        scratch_shapes=[_time_major(tm + HALO), _time_major(tm + HALO), _time_major(tm)],
        compiler_params=_params(1),
    )(dyb, proj, cb, proj, proj, sb16, dyb, proj, cb, proj, proj, wo4_g, cw_b, ln_g, ln_b, dproj, dw4)


def _branch_x_bwd(dyx, proj, sx16, kv16, wo4_g, dproj, dw4, tm):
    n_rows = proj.shape[0]
    scale = HEAD_DIM ** -0.5

    def body(dyx_ref, q, zx, sx_ref, kv_ref, w_ref, dp_in, dw_in, dp_ref, dw_ref, dkv_ref):
        del dp_in, dw_in
        i = pl.program_id(0)
        dyx16 = dyx_ref[...]
        q16 = q[...].astype(BF16)
        probs, outs = _attention(q16, kv_ref)
        dsx = _dot_nt(dyx16, w_ref[...].reshape(D, D))
        silu_z, dsilu_z = _silu_and_grad(zx[...])
        dp_ref[:, D:2 * D] = (dsx * jnp.concatenate(outs, axis=-1) * dsilu_z).astype(BF16)
        do16 = (dsx * silu_z).astype(BF16)

        @pl.when(i == 0)
        def _():
            dw_ref[...] = jnp.zeros_like(dw_ref)
            dkv_ref[...] = jnp.zeros_like(dkv_ref)

        for h in range(N_HEADS):
            cols = slice(h * HEAD_DIM, (h + 1) * HEAD_DIM)
            p = probs[h]
            dprob = _dot_nt(do16[:, cols], kv_ref[N_HEADS + h])
            ds16 = ((p * (dprob - jnp.sum(p * dprob, axis=-1, keepdims=True))) * scale).astype(BF16)
            dp_ref[:, cols] = _dot(ds16, kv_ref[h]).astype(BF16)
            dkv_ref[h] += _dot_tn(ds16, q16[:, cols])
            dkv_ref[N_HEADS + h] += _dot_tn(p.astype(BF16), do16[:, cols])
        dw_ref[0] += _dot_tn(sx_ref[...], dyx16)

    return pl.pallas_call(
        body, name="branch_x_bwd", grid=(n_rows // tm,),
        in_specs=[_rows(tm), _rows(tm, G_Q), _rows(tm, G_ZX), _rows(tm), _const(kv16.shape), _w_out_spec(2), ANY, ANY],
        out_specs=[pl.BlockSpec((tm, 2 * D), lambda i: (i, 2)), pl.BlockSpec((1, D, D), lambda i: (2, 0, 0)),
                   _const(kv16.shape)],
        out_shape=[jax.ShapeDtypeStruct(dproj.shape, BF16), jax.ShapeDtypeStruct(dw4.shape, F32),
                   jax.ShapeDtypeStruct(kv16.shape, F32)],
        input_output_aliases={6: 0, 7: 1},
        compiler_params=_params(1),
    )(dyx, proj, proj, sx16, kv16, wo4_g, dproj, dw4)


def _dp_unit(u):
    g = u // 2
    pos = jnp.where(g < G_VB, g, jnp.where(g < G_Q, g + 2, jnp.where(g < G_GA, g - 3, g)))
    return 2 * pos + u % 2


def _scatter_copies(srcs, lands, send, recv):
    x, y, c = lax.axis_index("x"), lax.axis_index("y"), lax.axis_index("c")
    copies = []
    for n in range(N_DEV - 1):
        flip = n + 1
        px = 1 - x if flip & 4 else x
        py = 1 - y if flip & 2 else y
        pc = 1 - c if flip & 1 else c
        for t, (src, land) in enumerate(zip(srcs, lands)):
            copies.append(pltpu.make_async_remote_copy(
                src_ref=src.at[4 * px + 2 * py + pc], dst_ref=land.at[n], send_sem=send.at[t, n],
                recv_sem=recv.at[t, n], device_id=(px, py, pc), device_id_type=MESH))
    return copies


def _scatter_sems(n_tensors):
    return [pltpu.SemaphoreType.DMA((n_tensors, N_DEV - 1)), pltpu.SemaphoreType.DMA((n_tensors, N_DEV - 1))]


def _landing(shard_shape, lead=()):
    return jax.ShapeDtypeStruct((*lead, N_DEV - 1, *shard_shape), BF16)


def _w_in_grad(ut, dproj, tk, dw4_16, dwkv16):
    n_rows = dproj.shape[0]
    n_k = n_rows // tk
    n_u = N_GROUPS * D // UNIT
    per_shard = W_IN_SHARD // UNIT

    def body(ut_ref, dp_ref, dw4_ref, dwkv_ref, out_ref, out16_ref, l4_ref, lkv_ref, acc, send, recv):
        u, t = pl.program_id(0), pl.program_id(1)

        def copies():
            return _scatter_copies([dw4_ref.at[w] for w in range(4)] + [dwkv_ref],
                                   [l4_ref.at[w] for w in range(4)] + [lkv_ref], send, recv)

        @pl.when((u == 0) & (t == 0))
        def _():
            for cp in copies():
                cp.start()

        @pl.when(t == 0)
        def _():
            acc[...] = jnp.zeros_like(acc)

        acc[...] += _dot(ut_ref[...], dp_ref[...])

        @pl.when(t == n_k - 1)
        def _():
            out_ref[0] = acc[...]
            out16_ref[0] = acc[...].astype(BF16)

        @pl.when((u == n_u - 1) & (t == n_k - 1))
        def _():
            for cp in copies():
                cp.wait()

    out_spec = pl.BlockSpec((1, D, UNIT), lambda u, t: (u // per_shard, 0, u % per_shard))
    return pl.pallas_call(
        body, name="w_in_grad", grid=(n_u, n_k),
        in_specs=[pl.BlockSpec((D, tk), lambda u, t: (0, t)), pl.BlockSpec((tk, UNIT), lambda u, t: (t, _dp_unit(u))),
                  ANY, ANY],
        out_specs=[out_spec, out_spec, ANY, ANY],
        out_shape=[jax.ShapeDtypeStruct((N_DEV, D, W_IN_SHARD), F32), jax.ShapeDtypeStruct((N_DEV, D, W_IN_SHARD), BF16),
                   _landing(dw4_16.shape[2:], lead=(4,)), _landing(dwkv16.shape[1:])],
        scratch_shapes=[pltpu.VMEM((D, UNIT), F32)] + _scatter_sems(5),
        compiler_params=_params(2),
    )(ut, dproj, dw4_16, dwkv16)


def _x_grad(dproj, win_g, x, dh, norm_g, dwin16, tm):
    n_rows = x.shape[0]
    n_tiles = n_rows // tm
    per_shard = W_IN_SHARD // UNIT

    def w_unit(half):
        def index(i, g):
            u = 2 * g + half
            return (u // per_shard, 0, u % per_shard)
        return pl.BlockSpec((1, D, UNIT), index)

    def dp_index(i, g):
        return (i, _dp_unit(2 * g) // 2)

    def body(dp_ref, wa_ref, wb_ref, x_ref, dh_ref, g_ref, dwin_ref, gx_ref, dg_ref, land_ref, acc, send, recv):
        i, g = pl.program_id(0), pl.program_id(1)

        @pl.when((i == 0) & (g == 0))
        def _():
            for cp in _scatter_copies([dwin_ref], [land_ref], send, recv):
                cp.start()
            dg_ref[...] = jnp.zeros_like(dg_ref)

        @pl.when(g == 0)
        def _():
            acc[...] = jnp.zeros_like(acc)

        acc[...] += _dot_nt(dp_ref[:, :UNIT], wa_ref[0]) + _dot_nt(dp_ref[:, UNIT:], wb_ref[0])

        @pl.when(g == N_GROUPS - 1)
        def _():
            du = acc[...]
            xf = x_ref[...]
            r = lax.rsqrt(_mean(xf * xf) + EPS)
            xn = xf * r
            dun = du * g_ref[...]
            gx_ref[...] = dh_ref[...] + r * (dun - xn * _mean(dun * xn))
            dg_ref[...] += _fold8(du * xn)

        @pl.when((i == n_tiles - 1) & (g == N_GROUPS - 1))
        def _():
            for cp in _scatter_copies([dwin_ref], [land_ref], send, recv):
                cp.wait()

    return pl.pallas_call(
        body, name="x_grad", grid=(n_tiles, N_GROUPS),
        in_specs=[pl.BlockSpec((tm, D), dp_index), w_unit(0), w_unit(1), pl.BlockSpec((tm, D), lambda i, g: (i, 0)),
                  pl.BlockSpec((tm, D), lambda i, g: (i, 0)), _const((1, D)), ANY],
        out_specs=[pl.BlockSpec((tm, D), lambda i, g: (i, 0)), _const((SUBLANES, D)), ANY],
        out_shape=[jax.ShapeDtypeStruct((n_rows, D), F32), jax.ShapeDtypeStruct((SUBLANES, D), F32),
                   _landing(dwin16.shape[1:])],
        scratch_shapes=[pltpu.VMEM((tm, D), F32)] + _scatter_sems(1),
        compiler_params=_params(2),
    )(dproj, win_g, win_g, x, dh, norm_g, dwin16)


def _local_step(x, mem, target, norm_g, conv_b_b, ln_g, ln_b, mem_g, final_g, win_g, wkv_g, wo4_g, cw_a, cw_b):
    n_rows = x.shape[0]
    tm = min(256, n_rows)
    big = min(1024, n_rows)
    kv16, mn16 = _kv_fwd(mem, mem_g, wkv_g)
    proj, ut = _proj_fwd(x, norm_g, win_g, big)
    sa16, ya = _branch_a_fwd(proj, wo4_g, cw_a, tm)
    cb, sb16, yb = _branch_b_fwd(proj, wo4_g, cw_b, conv_b_b, ln_g, ln_b, tm)
    sx16, yx = _branch_x_fwd(proj, kv16, wo4_g, tm)
    dh, dya, dyb, dyx, dproj, dw4, dfg, sq = _merge_fwd_bwd(proj, ya, yb, yx, x, target, wo4_g, final_g, tm)
    dproj, dw4, dwa = _branch_a_bwd(dya, proj, sa16, wo4_g, cw_a, dproj, dw4, tm)
    dproj, dw4, dwb, dbb, dlg, dlb = _branch_b_bwd(dyb, proj, cb, sb16, wo4_g, cw_b, ln_g, ln_b, dproj, dw4, tm)
    dproj, dw4, dkv = _branch_x_bwd(dyx, proj, sx16, kv16, wo4_g, dproj, dw4, tm)
    dwkv_g, dmg = _kv_bwd(dkv, mem, mem_g, mn16, wkv_g)
    dw4 = dw4.reshape(4, N_DEV, D // N_DEV, D)
    dwin_g, dwin16, land4, landkv = _w_in_grad(ut, dproj, min(4096, n_rows), dw4.astype(BF16), dwkv_g.astype(BF16))
    gx, dng, landin = _x_grad(dproj, win_g, x, dh, norm_g, dwin16, big)
    small = {SV_NORM_G: dng, SV_CONV_B_B: dbb, SV_LN_G: dlg, SV_LN_B: dlb, SV_MEM_G: dmg, SV_FINAL_G: dfg, SV_LOSS: sq}
    grads = [(dwin_g[None], landin[None]), (dw4, land4), (dwkv_g[None], landkv[None])]
    return gx, grads, small, dwa.reshape(K_A, D), dwb.reshape(K_B, D)


def _place():
    x, y, c = lax.axis_index("x"), lax.axis_index("y"), lax.axis_index("c")
    other_chips = [(1 - x, y), (x, 1 - y), (1 - x, 1 - y)]
    return x, y, c, other_chips


def _allgather(blocks):
    n = len(blocks)

    def body(*refs):
        src, out, (send, recv, local) = refs[:n], refs[n:2 * n], refs[2 * n:]
        x, y, c, chips = _place()
        me, sibling = 4 * x + 2 * y + c, (x, y, 1 - c)

        def copy(t, k, block, to, from_input=False):
            return pltpu.make_async_remote_copy(
                src_ref=src[t] if from_input else out[t].at[block], dst_ref=out[t].at[block],
                send_sem=send.at[t, k], recv_sem=recv.at[t, k], device_id=to, device_id_type=MESH)

        own = [pltpu.make_async_copy(src[t], out[t].at[me], local.at[t]) for t in range(n)]
        first = []
        for t in range(n):
            first.append(copy(t, 0, me, sibling, from_input=True))
            first += [copy(t, 1 + j, me, (*chip, c), from_input=True) for j, chip in enumerate(chips)]
        for cp in own + first:
            cp.start()
        passed = []
        for j, (px, py) in enumerate(chips):
            for t in range(n):
                block = 4 * px + 2 * py + c
                copy(t, 1 + j, block, sibling).wait_recv()
                passed.append(copy(t, 4 + j, block, sibling))
                passed[-1].start()
        for t in range(n):
            copy(t, 0, 4 * x + 2 * y + 1 - c, sibling).wait_recv()
            for j, (px, py) in enumerate(chips):
                copy(t, 4 + j, 4 * px + 2 * py + 1 - c, sibling).wait_recv()
        for cp in first + passed:
            cp.wait_send()
        for cp in own:
            cp.wait()

    return pl.pallas_call(
        body, name="allgather_weights",
        in_specs=[ANY] * n, out_specs=[ANY] * n,
        out_shape=[jax.ShapeDtypeStruct((N_DEV, *b.shape), b.dtype) for b in blocks],
        scratch_shapes=[pltpu.SemaphoreType.DMA((n, 7)), pltpu.SemaphoreType.DMA((n, 7)), pltpu.SemaphoreType.DMA((n,))],
    )(*blocks)


def _allgather_small(small, conv_rows):
    keys = sorted(small)

    def body(*refs):
        parts, (conv_ref, out_ref, mine, send, recv) = refs[:len(keys)], refs[len(keys):]
        x, y, c, chips = _place()
        me, sibling = 4 * x + 2 * y + c, (x, y, 1 - c)
        mine[pl.ds(0, SV_CONV_A), :] = jnp.zeros((SV_CONV_A, D), F32)
        for key, part in zip(keys, parts):
            mine[key:key + 1, :] = jnp.sum(part[...], axis=0, keepdims=True)
        mine[pl.ds(SV_CONV_A, SV_ROWS - SV_CONV_A), :] = conv_ref[...]
        out_ref[me] = mine[...]

        def copy(k, block, to, from_mine=False):
            return pltpu.make_async_remote_copy(
                src_ref=mine if from_mine else out_ref.at[block], dst_ref=out_ref.at[block],
                send_sem=send.at[k], recv_sem=recv.at[k], device_id=to, device_id_type=MESH)

        first = [copy(0, me, sibling, from_mine=True)]
        first += [copy(1 + j, me, (*chip, c), from_mine=True) for j, chip in enumerate(chips)]
        for cp in first:
            cp.start()
        passed = []
        for j, (px, py) in enumerate(chips):
            block = 4 * px + 2 * py + c
            copy(1 + j, block, sibling).wait_recv()
            passed.append(copy(4 + j, block, sibling))
            passed[-1].start()
        copy(0, 4 * x + 2 * y + 1 - c, sibling).wait_recv()
        for j, (px, py) in enumerate(chips):
            copy(4 + j, 4 * px + 2 * py + 1 - c, sibling).wait_recv()
        for cp in first + passed:
            cp.wait_send()

    vmem = pl.BlockSpec(memory_space=pltpu.VMEM)
    return pl.pallas_call(
        body, name="allgather_small",
        in_specs=[vmem] * (len(keys) + 1), out_specs=vmem,
        out_shape=jax.ShapeDtypeStruct((N_DEV, SV_ROWS, D), F32),
        scratch_shapes=[pltpu.VMEM((SV_ROWS, D), F32), pltpu.SemaphoreType.DMA((7,)), pltpu.SemaphoreType.DMA((7,))],
    )(*[small[k] for k in keys], conv_rows)


def _adamw(w, g, m, v):
    m = ADAM_B1 * m + (1.0 - ADAM_B1) * g
    v = ADAM_B2 * v + (1.0 - ADAM_B2) * (g * g)
    m_hat = m / (1.0 - ADAM_B1 ** ADAM_STEP)
    v_hat = v / (1.0 - ADAM_B2 ** ADAM_STEP)
    return -ADAM_LR * (m_hat / (jnp.sqrt(v_hat) + ADAM_EPS) + ADAM_WD * w), m, v


def _adamw_shard(own, landed, piece, k_arr, w, m, v, tr):
    n_r, n_c = w.shape
    n_landed = landed.shape[1]

    def body(k_ref, own_ref, *refs):
        del k_ref
        landed_refs, (w_ref, m_ref, v_ref, g_out, d_out, m_out, v_out) = refs[:n_landed], refs[n_landed:]
        g = own_ref[0, 0]
        for landed_ref in landed_refs:
            g = g + landed_ref[0, 0].astype(F32)
        g_out[...] = g
        d_out[...], m_out[...], v_out[...] = _adamw(w_ref[...], g, m_ref[...], v_ref[...])

    blk = (1, 1, tr, n_c)
    flat = pl.BlockSpec((tr, n_c), lambda r, k: (r, 0))
    return pl.pallas_call(
        body, name="adamw_shard",
        grid_spec=pltpu.PrefetchScalarGridSpec(
            num_scalar_prefetch=1, grid=(n_r // tr,),
            in_specs=[pl.BlockSpec(blk, lambda r, k: (piece, k[0], r, 0))]
            + [pl.BlockSpec(blk, functools.partial(lambda r, k, j: (piece, j, r, 0), j=j)) for j in range(n_landed)]
            + [flat] * 3,
            out_specs=[flat] * 4),
        out_shape=[jax.ShapeDtypeStruct((n_r, n_c), F32)] * 4,
        compiler_params=_params(1),
    )(k_arr, own, *([landed] * n_landed), w, m, v)


def _adamw_small(gathered, k_arr, vectors, conv_a, conv_b):
    n_vec = len(vectors)
    cols = D // N_DEV

    def body(k_ref, full_ref, cols_ref, *refs):
        del k_ref
        ins, outs = refs[:3 * (n_vec + 2)], refs[3 * (n_vec + 2):-2]
        tot, tot_cols = refs[-2:]
        tot[...] = full_ref[0]
        tot_cols[...] = cols_ref[0]
        for dev in range(1, N_DEV):
            tot[...] += full_ref[dev]
            tot_cols[...] += cols_ref[dev]
        loss = (0.5 / D) * jnp.sum(tot[SV_LOSS:SV_LOSS + 1, :])
        outs[0][...] = jnp.full(outs[0].shape, loss, F32)
        grads = [tot[n:n + 1, :] for n in range(n_vec)]
        grads += [tot_cols[pl.ds(SV_CONV_A, K_A), :], tot_cols[pl.ds(SV_CONV_B, K_B), :]]
        for n, g in enumerate(grads):
            w_ref, m_ref, v_ref = ins[3 * n:3 * n + 3]
            g_out, d_out, m_out, v_out = outs[1 + 4 * n:5 + 4 * n]
            g_out[...] = g
            d_out[...], m_out[...], v_out[...] = _adamw(w_ref[...], g, m_ref[...], v_ref[...])

    weights = list(vectors) + [conv_a, conv_b]
    flat_in = [a for wmv in weights for a in wmv]
    out_shape = [jax.ShapeDtypeStruct((SUBLANES, 128), F32)]
    for wmv in weights:
        out_shape += [jax.ShapeDtypeStruct(wmv[0].shape, F32)] * 4
    return pl.pallas_call(
        body, name="adamw_small",
        grid_spec=pltpu.PrefetchScalarGridSpec(
            num_scalar_prefetch=1, grid=(1,),
            in_specs=[pl.BlockSpec((N_DEV, SV_ROWS, D), lambda i, k: (0, 0, 0)),
                      pl.BlockSpec((N_DEV, SV_ROWS, cols), lambda i, k: (0, 0, k[0]))]
            + [pl.BlockSpec(a.shape, lambda i, k: (0, 0)) for a in flat_in],
            out_specs=[pl.BlockSpec(s.shape, lambda i, k: (0, 0)) for s in out_shape],
            scratch_shapes=[pltpu.VMEM((SV_ROWS, D), F32), pltpu.VMEM((SV_ROWS, cols), F32)]),
        out_shape=out_shape,
        compiler_params=_params(1),
    )(k_arr, gathered, gathered, *flat_in)


def kernel(x, mem, norm_g, w_in, conv_a_w, w_out_a, conv_b_w, conv_b_b, ln_b_g, ln_b_b, w_out_b, mem_norm_g, w_kv, w_out_x, w_o, final_g, loss_target, m_norm_g, m_w_in, m_conv_a_w, m_w_out_a, m_conv_b_w, m_conv_b_b, m_ln_b_g, m_ln_b_b, m_w_out_b, m_mem_norm_g, m_w_kv, m_w_out_x, m_w_o, m_final_g, v_norm_g, v_w_in, v_conv_a_w, v_w_out_a, v_conv_b_w, v_conv_b_b, v_ln_b_g, v_ln_b_b, v_w_out_b, v_mem_norm_g, v_w_kv, v_w_out_x, v_w_o, v_final_g):
    xi, yi, ci = lax.axis_index("x"), lax.axis_index("y"), lax.axis_index("c")
    k_arr = jnp.reshape(4 * xi + 2 * yi + ci, (1,)).astype(jnp.int32)

    cw = jnp.concatenate([jnp.pad(conv_a_w[0], ((0, SUBLANES - K_A), (0, 0))),
                          jnp.pad(conv_b_w[0], ((0, HALO - K_B), (0, 0)))], axis=0)
    wo4 = jnp.stack([w_out_a[0], w_out_b[0], w_out_x[0], w_o[0]]).astype(BF16)
    win_g, wkv_g, wo4_g, cw_g = _allgather([w_in[0].astype(BF16), w_kv[0].astype(BF16), wo4, cw])
    cw_rows = cw_g.transpose(1, 0, 2).reshape((SUBLANES + HALO) * LANE_GROUPS, LANES)
    cw_a, cw_b = cw_rows[:SUBLANES * LANE_GROUPS], cw_rows[SUBLANES * LANE_GROUPS:]
    wkv_heads = wkv_g

    final_g2 = final_g.reshape(1, D)
    gx, grads, small, dwa, dwb = _local_step(
        x[0], mem[0], loss_target[0], norm_g, conv_b_b, ln_b_g, ln_b_b, mem_norm_g, final_g2,
        win_g, wkv_heads, wo4_g, cw_a, cw_b)

    conv_rows = jnp.concatenate([jnp.pad(dwa, ((0, SUBLANES - K_A), (0, 0))),
                                 jnp.pad(dwb, ((0, HALO - K_B), (0, 0)))], axis=0)
    gathered_small = _allgather_small(small, conv_rows)

    tiles = [256, D // N_DEV, 256]

    def shard(a, l, w, m, v):
        return _adamw_shard(grads[a][0], grads[a][1], l, k_arr, w[0], m[0], v[0], tiles[a])

    res = {
        "w_in": shard(0, 0, w_in, m_w_in, v_w_in),
        "w_out_a": shard(1, 0, w_out_a, m_w_out_a, v_w_out_a),
        "w_out_b": shard(1, 1, w_out_b, m_w_out_b, v_w_out_b),
        "w_out_x": shard(1, 2, w_out_x, m_w_out_x, v_w_out_x),
        "w_o": shard(1, 3, w_o, m_w_o, v_w_o),
        "w_kv": shard(2, 0, w_kv, m_w_kv, v_w_kv),
    }
    res = {name: tuple(r[None] for r in four) for name, four in res.items()}
    vectors = [(norm_g, m_norm_g, v_norm_g), (conv_b_b, m_conv_b_b, v_conv_b_b), (ln_b_g, m_ln_b_g, v_ln_b_g),
               (ln_b_b, m_ln_b_b, v_ln_b_b), (mem_norm_g, m_mem_norm_g, v_mem_norm_g),
               (final_g2, m_final_g.reshape(1, D), v_final_g.reshape(1, D))]
    out = _adamw_small(gathered_small, k_arr, vectors, (conv_a_w[0], m_conv_a_w[0], v_conv_a_w[0]),
                       (conv_b_w[0], m_conv_b_w[0], v_conv_b_w[0]))
    loss = out[0][0, 0]
    names = ["norm_g", "conv_b_b", "ln_b_g", "ln_b_b", "mem_norm_g", "final_g", "conv_a_w", "conv_b_w"]
    for n, name in enumerate(names):
        four = out[1 + 4 * n:5 + 4 * n]
        if name == "final_g":
            four = [r.reshape(D) for r in four]
        elif name.startswith("conv_") and name.endswith("_w"):
            four = [r[None] for r in four]
        res[name] = tuple(four)

    order = ["norm_g", "w_in", "conv_a_w", "w_out_a", "conv_b_w", "conv_b_b", "ln_b_g", "ln_b_b", "w_out_b",
             "mem_norm_g", "w_kv", "w_out_x", "w_o", "final_g"]
    return (loss, gx[None], *[res[n][0] for n in order], *[res[n][1] for n in order],
            *[res[n][2] for n in order], *[res[n][3] for n in order])
```

```python
import functools

import jax
import jax.numpy as jnp
from jax import lax
from jax.experimental import pallas as pl
from jax.experimental.pallas import tpu as pltpu

F32, BF16 = jnp.float32, jnp.bfloat16
D = 1024
N_DEV = 8
N_HEADS = 4
HEAD_DIM = D // N_HEADS
N_GROUPS = 12
W_IN_SHARD = N_GROUPS * D // N_DEV
UNIT = 512
K_A, K_B = 3, 31
EPS = 1e-6
HALO = 32
SUBLANES = 8
LANES = 128
LANE_GROUPS = D // LANES
UNROLL_A, UNROLL_B = 8, 4
WGRAD_TAPS = 16
VMEM_LIMIT = 56 << 20
MESH = pl.DeviceIdType.MESH
ANY = pl.BlockSpec(memory_space=pl.ANY)

G_BA, G_CA, G_XA, G_ZA, G_VB, G_GB, G_ZB, G_Q, G_ZX, G_GA, G_GBB, G_GX = range(N_GROUPS)
DP_POS = (0, 1, 2, 3, 6, 7, 8, 4, 5, 9, 10, 11)

ADAM_LR, ADAM_B1, ADAM_B2, ADAM_EPS, ADAM_WD, ADAM_STEP = 0.001, 0.9, 0.999, 1e-08, 0.01, 10

SV_NORM_G, SV_CONV_B_B, SV_LN_G, SV_LN_B, SV_MEM_G, SV_FINAL_G, SV_LOSS = range(7)
SV_CONV_A, SV_CONV_B, SV_ROWS = 8, 16, 48


def _dot(a, b):
    return jnp.dot(a, b, preferred_element_type=F32)


def _dot_nt(a, b):
    return lax.dot_general(a, b, (((1,), (1,)), ((), ())), preferred_element_type=F32)


def _dot_tn(a, b):
    return lax.dot_general(a, b, (((0,), (0,)), ((), ())), preferred_element_type=F32)


def _silu_and_grad(z):
    s = jax.nn.sigmoid(z)
    return z * s, s * (1.0 + z * (1.0 - s))


def _fold8(a):
    return a.reshape(a.shape[0] // SUBLANES, SUBLANES, a.shape[1]).sum(axis=0)


def _mean(a):
    return jnp.mean(a, axis=-1, keepdims=True)


def _params(n_grid):
    return pltpu.CompilerParams(dimension_semantics=("arbitrary",) * n_grid, vmem_limit_bytes=VMEM_LIMIT)


def _rows(tm, col=0):
    return pl.BlockSpec((tm, D), lambda i: (i, col))


def _prev_halo(tm, col=0):
    return pl.BlockSpec((HALO, D), lambda i: (jnp.maximum(i * (tm // HALO) - 1, 0), col))


def _next_halo(tm, n_rows, col=0):
    last = n_rows // HALO - 1
    return pl.BlockSpec((HALO, D), lambda i: (jnp.minimum((i + 1) * (tm // HALO), last), col))


def _const(shape):
    return pl.BlockSpec(shape, lambda *_: (0,) * len(shape))


def _w_out_spec(which):
    return pl.BlockSpec((N_DEV, None, D // N_DEV, D), lambda *_: (0, which, 0, 0))


def _to_time_major(t_ref, row0, x):
    n = x.shape[0]
    for j in range(LANE_GROUPS):
        t_ref[pl.ds(row0 * LANE_GROUPS + j, n, stride=LANE_GROUPS), :] = x[:, j * LANES:(j + 1) * LANES]


def _from_time_major(t_ref, n):
    return jnp.concatenate([t_ref[pl.ds(j, n, stride=LANE_GROUPS), :] for j in range(LANE_GROUPS)], axis=-1)


def _row(ref, t):
    start = t * LANE_GROUPS
    if not isinstance(start, int):
        start = pl.multiple_of(start, LANE_GROUPS)
    return ref[pl.ds(start, LANE_GROUPS), :]


def _conv(o_ref, e_ref, w_ref, taps, n_rows, unroll, bias_ref=None):
    lo = min(off for _, off in taps)
    hi = max(off for _, off in taps)

    def chunk(c, carry):
        t0 = c * unroll
        window = [_row(e_ref, t0 + lo + n) for n in range(hi - lo + unroll)]
        acc = [[None, None] for _ in range(unroll)]
        for n, (k, off) in enumerate(taps):
            wk = _row(w_ref, k)
            for u in range(unroll):
                term = wk * window[off - lo + u]
                acc[u][n % 2] = term if acc[u][n % 2] is None else acc[u][n % 2] + term
        for u in range(unroll):
            out = acc[u][0] + acc[u][1]
            if bias_ref is not None:
                out = out + bias_ref[...]
            o_ref[pl.ds(pl.multiple_of((t0 + u) * LANE_GROUPS, LANE_GROUPS), LANE_GROUPS), :] = out
        return carry

    lax.fori_loop(0, n_rows // unroll, chunk, 0)


def _conv_wgrad(dw_ref, d_ref, e_ref, taps, n_rows, unroll):
    for first in range(0, len(taps), WGRAD_TAPS):
        group = taps[first:first + WGRAD_TAPS]
        lo = min(off for _, off in group)
        hi = max(off for _, off in group)

        def chunk(c, accs, group=group, lo=lo, hi=hi):
            t0 = c * unroll
            d = [_row(d_ref, t0 + u) for u in range(unroll)]
            window = [_row(e_ref, t0 + lo + n) for n in range(hi - lo + unroll)]
            accs = list(accs)
            for n, (_, off) in enumerate(group):
                for u in range(unroll):
                    accs[n] = accs[n] + d[u] * window[off - lo + u]
            return tuple(accs)

        zeros = tuple(jnp.zeros((LANE_GROUPS, LANES), F32) for _ in group)
        accs = lax.fori_loop(0, n_rows // unroll, chunk, zeros)
        for (k, _), acc in zip(group, accs):
            dw_ref[pl.ds(k * LANE_GROUPS, LANE_GROUPS), :] += acc


FWD_TAPS_A = [(k, HALO - (K_A - 1) + k) for k in range(K_A)]
BWD_TAPS_A = [(k, K_A - 1 - k) for k in range(K_A)]
FWD_TAPS_B = [(k, HALO - (K_B - 1) + k) for k in range(K_B)]
BWD_TAPS_B = [(k, K_B - 1 - k) for k in range(K_B)]


def _time_major(n_rows):
    return pltpu.VMEM((n_rows * LANE_GROUPS, LANES), F32)


def _kv_fwd(mem, mem_g, wkv_g):
    m_len = mem.shape[0]

    def body(mem_ref, g_ref, w_ref, kv_ref, mn_ref):
        mf = mem_ref[...]
        r = lax.rsqrt(_mean(mf * mf) + EPS)
        mn = ((mf * r) * g_ref[...]).astype(BF16)
        mn_ref[...] = mn
        for b in range(2 * N_HEADS):
            kv_ref[b] = _dot(mn, w_ref[b]).astype(BF16)

    return pl.pallas_call(
        body, name="kv_fwd", grid=(1,),
        in_specs=[_const((m_len, D)), _const((1, D)), _const((2 * N_HEADS, D, HEAD_DIM))],
        out_specs=[_const((2 * N_HEADS, m_len, HEAD_DIM)), _const((m_len, D))],
        out_shape=[jax.ShapeDtypeStruct((2 * N_HEADS, m_len, HEAD_DIM), BF16), jax.ShapeDtypeStruct((m_len, D), BF16)],
        compiler_params=_params(1),
    )(mem, mem_g, wkv_g)


def _kv_bwd(dkv, mem, mem_g, mn16, wkv_g):
    m_len = mem.shape[0]

    def body(dkv_ref, mem_ref, g_ref, mn_ref, w_ref, dw_ref, dg_ref):
        mn = mn_ref[...]
        dmn = jnp.zeros((m_len, D), F32)
        for b in range(2 * N_HEADS):
            d16 = dkv_ref[b].astype(BF16)
            dw_ref[b] = _dot_tn(mn, d16)
            dmn = dmn + _dot_nt(d16, w_ref[b])
        mf = mem_ref[...]
        r = lax.rsqrt(_mean(mf * mf) + EPS)
        dg_ref[...] = _fold8(dmn * (mf * r))

    return pl.pallas_call(
        body, name="kv_bwd", grid=(1,),
        in_specs=[_const((2 * N_HEADS, m_len, HEAD_DIM)), _const((m_len, D)), _const((1, D)), _const((m_len, D)),
                  _const((2 * N_HEADS, D, HEAD_DIM))],
        out_specs=[_const((2 * N_HEADS, D, HEAD_DIM)), _const((SUBLANES, D))],
        out_shape=[jax.ShapeDtypeStruct((2 * N_HEADS, D, HEAD_DIM), F32), jax.ShapeDtypeStruct((SUBLANES, D), F32)],
        compiler_params=_params(1),
    )(dkv, mem, mem_g, mn16, wkv_g)


def _rmsnorm_fwd(x, norm_g, tm):
    n_rows = x.shape[0]

    def body(x_ref, g_ref, u_ref, ut_ref):
        xf = x_ref[...]
        u = (xf * lax.rsqrt(_mean(xf * xf) + EPS)) * g_ref[...]
        u_ref[...] = u.astype(BF16)
        ut_ref[...] = u.T.astype(BF16)

    return pl.pallas_call(
        body, name="rmsnorm_fwd", grid=(n_rows // tm,),
        in_specs=[_rows(tm), _const((1, D))],
        out_specs=[_rows(tm), pl.BlockSpec((D, tm), lambda i: (0, i))],
        out_shape=[jax.ShapeDtypeStruct((n_rows, D), BF16), jax.ShapeDtypeStruct((D, n_rows), BF16)],
        compiler_params=_params(1),
    )(x, norm_g)


def _place():
    x, y, c = lax.axis_index("x"), lax.axis_index("y"), lax.axis_index("c")
    other_chips = [(1 - x, y), (x, 1 - y), (1 - x, 1 - y)]
    return x, y, c, other_chips


def _arrival_order():
    x, y, c, chips = _place()
    order = [4 * x + 2 * y + c, 4 * x + 2 * y + 1 - c]
    for px, py in chips:
        order += [4 * px + 2 * py + c, 4 * px + 2 * py + 1 - c]
    return order


def _proj_fwd_gather(u16, blocks, tm):
    n = len(blocks)
    n_rows = u16.shape[0]
    n_i = n_rows // tm

    def body(order_ref, u_ref, *refs):
        src, proj_ref, out = refs[:n], refs[n], refs[n + 1:2 * n + 1]
        wbuf, stage_sem, send, recv, own_sem = refs[2 * n + 1:]
        p, i = pl.program_id(0), pl.program_id(1)
        x, y, c, chips = _place()
        me, sibling = 4 * x + 2 * y + c, (x, y, 1 - c)

        def copy(t, k, block, to, from_input=False):
            return pltpu.make_async_remote_copy(
                src_ref=src[t] if from_input else out[t].at[block], dst_ref=out[t].at[block],
                send_sem=send.at[t, k], recv_sem=recv.at[t, k], device_id=to, device_id_type=MESH)

        def own_copies():
            return [pltpu.make_async_copy(src[t], out[t].at[me], own_sem.at[t]) for t in range(n)]

        def first_copies():
            first = []
            for t in range(n):
                first.append(copy(t, 0, me, sibling, from_input=True))
                first += [copy(t, 1 + j, me, (*chip, c), from_input=True) for j, chip in enumerate(chips)]
            return first

        def stage(slot, block):
            return pltpu.make_async_copy(out[0].at[block], wbuf.at[slot], stage_sem.at[slot])

        @pl.when((p == 0) & (i == 0))
        def _():
            for cp in own_copies() + first_copies():
                cp.start()
            mine = pltpu.make_async_copy(src[0], wbuf.at[0], stage_sem.at[0])
            mine.start()
            mine.wait()

        @pl.when((p > 0) & (i == 0))
        def _():
            stage(p % 2, order_ref[p]).wait()

        proj_ref[...] = _dot(u_ref[...], wbuf[p % 2])

        for nxt in range(1, N_DEV):
            @pl.when((p == nxt - 1) & (i == n_i - 1))
            def _(nxt=nxt):
                if nxt == 1:
                    block = 4 * x + 2 * y + 1 - c
                    copy(0, 0, block, sibling).wait_recv()
                else:
                    j, passed_on = divmod(nxt - 2, 2)
                    px, py = chips[j]
                    if passed_on:
                        block = 4 * px + 2 * py + 1 - c
                        copy(0, 4 + j, block, sibling).wait_recv()
                    else:
                        block = 4 * px + 2 * py + c
                        copy(0, 1 + j, block, sibling).wait_recv()
                        copy(0, 4 + j, block, sibling).start()
                stage(nxt % 2, block).start()

        @pl.when((p == N_DEV - 1) & (i == n_i - 1))
        def _():
            passed = [copy(0, 4 + j, 4 * px + 2 * py + c, sibling) for j, (px, py) in enumerate(chips)]
            for j, (px, py) in enumerate(chips):
                for t in range(1, n):
                    block = 4 * px + 2 * py + c
                    copy(t, 1 + j, block, sibling).wait_recv()
                    passed.append(copy(t, 4 + j, block, sibling))
                    passed[-1].start()
            for t in range(1, n):
                copy(t, 0, 4 * x + 2 * y + 1 - c, sibling).wait_recv()
                for j, (px, py) in enumerate(chips):
                    copy(t, 4 + j, 4 * px + 2 * py + 1 - c, sibling).wait_recv()
            for cp in first_copies() + passed:
                cp.wait_send()
            for cp in own_copies():
                cp.wait()

    return pl.pallas_call(
        body, name="proj_fwd_gather",
        grid_spec=pltpu.PrefetchScalarGridSpec(
            num_scalar_prefetch=1, grid=(N_DEV, n_i),
            in_specs=[pl.BlockSpec((tm, D), lambda p, i, order: (i, 0))] + [ANY] * n,
            out_specs=[pl.BlockSpec((tm, W_IN_SHARD), lambda p, i, order: (i, order[p]))] + [ANY] * n,
            scratch_shapes=[pltpu.VMEM((2, D, W_IN_SHARD), BF16), pltpu.SemaphoreType.DMA((2,)),
                            pltpu.SemaphoreType.DMA((n, 7)), pltpu.SemaphoreType.DMA((n, 7)),
                            pltpu.SemaphoreType.DMA((n,))]),
        out_shape=[jax.ShapeDtypeStruct((n_rows, N_GROUPS * D), F32)]
        + [jax.ShapeDtypeStruct((N_DEV, *b.shape), b.dtype) for b in blocks],
        compiler_params=_params(2),
    )(jnp.stack(_arrival_order()).astype(jnp.int32), u16, *blocks)


def _branch_a_fwd(proj, wo4_g, cw_a, tm):
    n_rows = proj.shape[0]

    def body(bp, cp, xp, za, cph, xph, w_ref, cw_ref, sa_ref, ya_ref, e_scr, o_scr):
        i = pl.program_id(0)
        _to_time_major(e_scr, 0, jnp.where(i > 0, cph[...] * xph[...], 0.0))
        _to_time_major(e_scr, HALO, cp[...] * xp[...])
        _conv(o_scr, e_scr, cw_ref, FWD_TAPS_A, tm, UNROLL_A)
        sa = (jax.nn.silu(za[...]) * (bp[...] * _from_time_major(o_scr, tm))).astype(BF16)
        sa_ref[...] = sa
        ya_ref[...] = _dot(sa, w_ref[...].reshape(D, D))

    return pl.pallas_call(
        body, name="branch_a_fwd", grid=(n_rows // tm,),
        in_specs=[_rows(tm, G_BA), _rows(tm, G_CA), _rows(tm, G_XA), _rows(tm, G_ZA),
                  _prev_halo(tm, G_CA), _prev_halo(tm, G_XA), _w_out_spec(0), _const(cw_a.shape)],
        out_specs=[_rows(tm), _rows(tm)],
        out_shape=[jax.ShapeDtypeStruct((n_rows, D), BF16), jax.ShapeDtypeStruct((n_rows, D), F32)],
        scratch_shapes=[_time_major(tm + HALO), _time_major(tm)],
        compiler_params=_params(1),
    )(proj, proj, proj, proj, proj, proj, wo4_g, cw_a)


def _layernorm_parts(cb, lg, lb):
    xc = cb - _mean(cb)
    rstd = lax.rsqrt(_mean(xc * xc) + EPS)
    xhat = xc * rstd
    return xhat, rstd, xhat * lg + lb


def _branch_b_fwd(proj, wo4_g, cw_b, conv_b_b, ln_g, ln_b, tm):
    n_rows = proj.shape[0]

    def body(vb, gb, zb, vbh, gbh, w_ref, cw_ref, bb_ref, lg_ref, lb_ref, cb_ref, sb_ref, yb_ref, e_scr, o_scr):
        i = pl.program_id(0)
        _to_time_major(e_scr, 0, jnp.where(i > 0, vbh[...] * jax.nn.sigmoid(gbh[...]), 0.0))
        _to_time_major(e_scr, HALO, vb[...] * jax.nn.sigmoid(gb[...]))
        _conv(o_scr, e_scr, cw_ref, FWD_TAPS_B, tm, UNROLL_B, bias_ref=bb_ref)
        cb = _from_time_major(o_scr, tm)
        cb_ref[...] = cb
        _, _, ln = _layernorm_parts(cb, lg_ref[...], lb_ref[...])
        sb = (jax.nn.silu(zb[...]) * jax.nn.silu(ln)).astype(BF16)
        sb_ref[...] = sb
        yb_ref[...] = _dot(sb, w_ref[...].reshape(D, D))

    return pl.pallas_call(
        body, name="branch_b_fwd", grid=(n_rows // tm,),
        in_specs=[_rows(tm, G_VB), _rows(tm, G_GB), _rows(tm, G_ZB), _prev_halo(tm, G_VB), _prev_halo(tm, G_GB),
                  _w_out_spec(1), _const(cw_b.shape), _const((LANE_GROUPS, LANES)), _const((1, D)), _const((1, D))],
        out_specs=[_rows(tm), _rows(tm), _rows(tm)],
        out_shape=[jax.ShapeDtypeStruct((n_rows, D), F32), jax.ShapeDtypeStruct((n_rows, D), BF16),
                   jax.ShapeDtypeStruct((n_rows, D), F32)],
        scratch_shapes=[_time_major(tm + HALO), _time_major(tm)],
        compiler_params=_params(1),
    )(proj, proj, proj, proj, proj, wo4_g, cw_b, conv_b_b.reshape(LANE_GROUPS, LANES), ln_g, ln_b)


def _attention(q16, kv_ref):
    probs, outs = [], []
    for h in range(N_HEADS):
        s = _dot_nt(q16[:, h * HEAD_DIM:(h + 1) * HEAD_DIM], kv_ref[h]) * (HEAD_DIM ** -0.5)
        e = jnp.exp(s - jnp.max(s, axis=-1, keepdims=True))
        p = e / jnp.sum(e, axis=-1, keepdims=True)
        probs.append(p)
        outs.append(_dot(p.astype(BF16), kv_ref[N_HEADS + h]))
    return probs, outs


def _branch_x_fwd(proj, kv16, wo4_g, tm):
    n_rows = proj.shape[0]

    def body(q, zx, kv_ref, w_ref, sx_ref, yx_ref):
        _, outs = _attention(q[...].astype(BF16), kv_ref)
        sx = (jax.nn.silu(zx[...]) * jnp.concatenate(outs, axis=-1)).astype(BF16)
        sx_ref[...] = sx
        yx_ref[...] = _dot(sx, w_ref[...].reshape(D, D))

    return pl.pallas_call(
        body, name="branch_x_fwd", grid=(n_rows // tm,),
        in_specs=[_rows(tm, G_Q), _rows(tm, G_ZX), _const(kv16.shape), _w_out_spec(2)],
        out_specs=[_rows(tm), _rows(tm)],
        out_shape=[jax.ShapeDtypeStruct((n_rows, D), BF16), jax.ShapeDtypeStruct((n_rows, D), F32)],
        compiler_params=_params(1),
    )(proj, proj, kv16, wo4_g)


def _merge_fwd_bwd(proj, ya, yb, yx, x, target, wo4_g, final_g, tm):
    n_rows = proj.shape[0]
    inv_d = 1.0 / D

    def body(ga, gb, gx, ya_ref, yb_ref, yx_ref, x_ref, t_ref, w_ref, fg_ref,
             dh_ref, dya_ref, dyb_ref, dyx_ref, dp_ref, dw_ref, dfg_ref, sq_ref):
        i = pl.program_id(0)
        wo = w_ref[...].reshape(D, D)
        sig = [jax.nn.sigmoid(g[...]) for g in (ga, gb, gx)]
        ys = [ya_ref[...], yb_ref[...], yx_ref[...]]
        m16 = (sig[0] * ys[0] + sig[1] * ys[1] + sig[2] * ys[2]).astype(BF16)
        h = x_ref[...] + _dot(m16, wo)
        r = lax.rsqrt(_mean(h * h) + EPS)
        hn = h * r
        fg = fg_ref[...]
        err = hn * fg - t_ref[...]
        dy = err * inv_d
        dhn = dy * fg
        dh = r * (dhn - hn * _mean(dhn * hn))
        dh_ref[...] = dh
        dh16 = dh.astype(BF16)
        dm = _dot_nt(dh16, wo)
        for n, out in enumerate((dya_ref, dyb_ref, dyx_ref)):
            out[...] = (sig[n] * dm).astype(BF16)
            dp_ref[:, n * D:(n + 1) * D] = (dm * ys[n] * (sig[n] * (1.0 - sig[n]))).astype(BF16)

        @pl.when(i == 0)
        def _():
            dw_ref[...] = jnp.zeros_like(dw_ref)
            dfg_ref[...] = jnp.zeros_like(dfg_ref)
            sq_ref[...] = jnp.zeros_like(sq_ref)

        dw_ref[0] += _dot_tn(m16, dh16)
        dfg_ref[...] += _fold8(dy * hn)
        sq_ref[...] += _fold8(err * err)

    vec = jax.ShapeDtypeStruct((SUBLANES, D), F32)
    return pl.pallas_call(
        body, name="merge_fwd_bwd", grid=(n_rows // tm,),
        in_specs=[_rows(tm, G_GA), _rows(tm, G_GBB), _rows(tm, G_GX), _rows(tm), _rows(tm), _rows(tm), _rows(tm),
                  _rows(tm), _w_out_spec(3), _const((1, D))],
        out_specs=[_rows(tm), _rows(tm), _rows(tm), _rows(tm), pl.BlockSpec((tm, 3 * D), lambda i: (i, 3)),
                   pl.BlockSpec((1, D, D), lambda i: (3, 0, 0)), _const((SUBLANES, D)), _const((SUBLANES, D))],
        out_shape=[jax.ShapeDtypeStruct((n_rows, D), F32), jax.ShapeDtypeStruct((n_rows, D), BF16),
                   jax.ShapeDtypeStruct((n_rows, D), BF16), jax.ShapeDtypeStruct((n_rows, D), BF16),
                   jax.ShapeDtypeStruct((n_rows, N_GROUPS * D), BF16), jax.ShapeDtypeStruct((4, D, D), F32), vec, vec],
        compiler_params=_params(1),
    )(proj, proj, proj, ya, yb, yx, x, target, wo4_g, final_g)


def _branch_a_bwd(dya, proj, sa16, wo4_g, cw_a, dproj, dw4, tm):
    n_rows = proj.shape[0]
    n_tiles = n_rows // tm

    def body(dya_ref, bp, cp, xp, za, sa_ref, dyan, bpn, zan, cph, xph, w_ref, cw_ref, dp_in, dw_in,
             dp_ref, dw_ref, dwa_ref, e1, e2, o_scr):
        del dp_in, dw_in
        i = pl.program_id(0)
        woa = w_ref[...].reshape(D, D)
        dya16 = dya_ref[...]
        _to_time_major(e1, 0, jnp.where(i > 0, cph[...] * xph[...], 0.0))
        _to_time_major(e1, HALO, cp[...] * xp[...])
        _conv(o_scr, e1, cw_ref, FWD_TAPS_A, tm, UNROLL_A)
        ca = _from_time_major(o_scr, tm)
        dsa = _dot_nt(dya16, woa)
        silu_z, dsilu_z = _silu_and_grad(za[...])
        t = dsa * silu_z
        dp_ref[:, 0 * D:1 * D] = (t * ca).astype(BF16)
        dp_ref[:, 3 * D:4 * D] = (dsa * (bp[...] * ca) * dsilu_z).astype(BF16)
        dsan = _dot_nt(dyan[...], woa)
        dcan = (dsan * jax.nn.silu(zan[...])) * bpn[...]
        _to_time_major(e2, 0, t * bp[...])
        _to_time_major(e2, tm, jnp.where(i < n_tiles - 1, dcan, 0.0))

        @pl.when(i == 0)
        def _():
            dw_ref[...] = jnp.zeros_like(dw_ref)
            dwa_ref[...] = jnp.zeros_like(dwa_ref)

        _conv_wgrad(dwa_ref, e2, e1, FWD_TAPS_A, tm, UNROLL_A)
        dw_ref[0] += _dot_tn(sa_ref[...], dya16)
        _conv(o_scr, e2, cw_ref, BWD_TAPS_A, tm, UNROLL_A)
        dprod = _from_time_major(o_scr, tm)
        dp_ref[:, 1 * D:2 * D] = (dprod * xp[...]).astype(BF16)
        dp_ref[:, 2 * D:3 * D] = (dprod * cp[...]).astype(BF16)

    return pl.pallas_call(
        body, name="branch_a_bwd", grid=(n_tiles,),
        in_specs=[_rows(tm), _rows(tm, G_BA), _rows(tm, G_CA), _rows(tm, G_XA), _rows(tm, G_ZA), _rows(tm),
                  _next_halo(tm, n_rows), _next_halo(tm, n_rows, G_BA), _next_halo(tm, n_rows, G_ZA),
                  _prev_halo(tm, G_CA), _prev_halo(tm, G_XA), _w_out_spec(0), _const(cw_a.shape), ANY, ANY],
        out_specs=[pl.BlockSpec((tm, 4 * D), lambda i: (i, 0)), pl.BlockSpec((1, D, D), lambda i: (0, 0, 0)),
                   _const((K_A * LANE_GROUPS, LANES))],
        out_shape=[jax.ShapeDtypeStruct(dproj.shape, BF16), jax.ShapeDtypeStruct(dw4.shape, F32),
                   jax.ShapeDtypeStruct((K_A * LANE_GROUPS, LANES), F32)],
        input_output_aliases={13: 0, 14: 1},
        scratch_shapes=[_time_major(tm + HALO), _time_major(tm + HALO), _time_major(tm)],
        compiler_params=_params(1),
    )(dya, proj, proj, proj, proj, sa16, dya, proj, proj, proj, proj, wo4_g, cw_a, dproj, dw4)


def _branch_b_bwd(dyb, proj, cb, sb16, wo4_g, cw_b, ln_g, ln_b, dproj, dw4, tm):
    n_rows = proj.shape[0]
    n_tiles = n_rows // tm

    def body(dyb_ref, zb, cb_ref, vb, gb, sb_ref, dybn, zbn, cbn, vbh, gbh, w_ref, cw_ref, lg_ref, lb_ref,
             dp_in, dw_in, dp_ref, dw_ref, dwb_ref, dbb_ref, dlg_ref, dlb_ref, e1, e2, o_scr):
        del dp_in, dw_in
        i = pl.program_id(0)
        wob = w_ref[...].reshape(D, D)
        lg, lb = lg_ref[...], lb_ref[...]

        def conv_out_grad(dy16, z, c):
            dsb = _dot_nt(dy16, wob)
            xhat, rstd, ln = _layernorm_parts(c, lg, lb)
            sw, dsw = _silu_and_grad(ln)
            sz, dsz = _silu_and_grad(z)
            dln = (dsb * sz) * dsw
            dxhat = dln * lg
            dcb = rstd * (dxhat - _mean(dxhat) - xhat * _mean(dxhat * xhat))
            return dsb * sw * dsz, dln, xhat, dcb

        dyb16 = dyb_ref[...]
        dzb, dln, xhat, dcb = conv_out_grad(dyb16, zb[...], cb_ref[...])
        dp_ref[:, 2 * D:3 * D] = dzb.astype(BF16)
        _, _, _, dcbn = conv_out_grad(dybn[...], zbn[...], cbn[...])
        _to_time_major(e2, 0, dcb)
        _to_time_major(e2, tm, jnp.where(i < n_tiles - 1, dcbn, 0.0))

        @pl.when(i == 0)
        def _():
            dw_ref[...] = jnp.zeros_like(dw_ref)
            dwb_ref[...] = jnp.zeros_like(dwb_ref)
            dbb_ref[...] = jnp.zeros_like(dbb_ref)
            dlg_ref[...] = jnp.zeros_like(dlg_ref)
            dlb_ref[...] = jnp.zeros_like(dlb_ref)

        dlg_ref[...] += _fold8(dln * xhat)
        dlb_ref[...] += _fold8(dln)
        dbb_ref[...] += _fold8(dcb)
        dw_ref[0] += _dot_tn(sb_ref[...], dyb16)
        sg = jax.nn.sigmoid(gb[...])
        _to_time_major(e1, 0, jnp.where(i > 0, vbh[...] * jax.nn.sigmoid(gbh[...]), 0.0))
        _to_time_major(e1, HALO, vb[...] * sg)
        _conv_wgrad(dwb_ref, e2, e1, FWD_TAPS_B, tm, UNROLL_B)
        _conv(o_scr, e2, cw_ref, BWD_TAPS_B, tm, UNROLL_B)
        dglu = _from_time_major(o_scr, tm)
        dp_ref[:, 0 * D:1 * D] = (dglu * sg).astype(BF16)
        dp_ref[:, 1 * D:2 * D] = (dglu * vb[...] * (sg * (1.0 - sg))).astype(BF16)

    vec = jax.ShapeDtypeStruct((SUBLANES, D), F32)
    return pl.pallas_call(
        body, name="branch_b_bwd", grid=(n_tiles,),
        in_specs=[_rows(tm), _rows(tm, G_ZB), _rows(tm), _rows(tm, G_VB), _rows(tm, G_GB), _rows(tm),
                  _next_halo(tm, n_rows), _next_halo(tm, n_rows, G_ZB), _next_halo(tm, n_rows),
                  _prev_halo(tm, G_VB), _prev_halo(tm, G_GB), _w_out_spec(1), _const(cw_b.shape), _const((1, D)),
                  _const((1, D)), ANY, ANY],
        out_specs=[pl.BlockSpec((tm, 3 * D), lambda i: (i, 2)), pl.BlockSpec((1, D, D), lambda i: (1, 0, 0)),
                   _const((K_B * LANE_GROUPS, LANES)), _const((SUBLANES, D)), _const((SUBLANES, D)),
                   _const((SUBLANES, D))],
        out_shape=[jax.ShapeDtypeStruct(dproj.shape, BF16), jax.ShapeDtypeStruct(dw4.shape, F32),
                   jax.ShapeDtypeStruct((K_B * LANE_GROUPS, LANES), F32), vec, vec, vec],
        input_output_aliases={15: 0, 16: 1},
        scratch_shapes=[_time_major(tm + HALO), _time_major(tm + HALO), _time_major(tm)],
        compiler_params=_params(1),
    )(dyb, proj, cb, proj, proj, sb16, dyb, proj, cb, proj, proj, wo4_g, cw_b, ln_g, ln_b, dproj, dw4)


def _branch_x_bwd(dyx, proj, sx16, kv16, wo4_g, dproj, dw4, tm):
    n_rows = proj.shape[0]
    scale = HEAD_DIM ** -0.5

    def body(dyx_ref, q, zx, sx_ref, kv_ref, w_ref, dp_in, dw_in, dp_ref, dw_ref, dkv_ref):
        del dp_in, dw_in
        i = pl.program_id(0)
        dyx16 = dyx_ref[...]
        q16 = q[...].astype(BF16)
        probs, outs = _attention(q16, kv_ref)
        dsx = _dot_nt(dyx16, w_ref[...].reshape(D, D))
        silu_z, dsilu_z = _silu_and_grad(zx[...])
        dp_ref[:, D:2 * D] = (dsx * jnp.concatenate(outs, axis=-1) * dsilu_z).astype(BF16)
        do16 = (dsx * silu_z).astype(BF16)

        @pl.when(i == 0)
        def _():
            dw_ref[...] = jnp.zeros_like(dw_ref)
            dkv_ref[...] = jnp.zeros_like(dkv_ref)

        for h in range(N_HEADS):
            cols = slice(h * HEAD_DIM, (h + 1) * HEAD_DIM)
            p = probs[h]
            dprob = _dot_nt(do16[:, cols], kv_ref[N_HEADS + h])
            ds16 = ((p * (dprob - jnp.sum(p * dprob, axis=-1, keepdims=True))) * scale).astype(BF16)
            dp_ref[:, cols] = _dot(ds16, kv_ref[h]).astype(BF16)
            dkv_ref[h] += _dot_tn(ds16, q16[:, cols])
            dkv_ref[N_HEADS + h] += _dot_tn(p.astype(BF16), do16[:, cols])
        dw_ref[0] += _dot_tn(sx_ref[...], dyx16)

    return pl.pallas_call(
        body, name="branch_x_bwd", grid=(n_rows // tm,),
        in_specs=[_rows(tm), _rows(tm, G_Q), _rows(tm, G_ZX), _rows(tm), _const(kv16.shape), _w_out_spec(2), ANY, ANY],
        out_specs=[pl.BlockSpec((tm, 2 * D), lambda i: (i, 2)), pl.BlockSpec((1, D, D), lambda i: (2, 0, 0)),
                   _const(kv16.shape)],
        out_shape=[jax.ShapeDtypeStruct(dproj.shape, BF16), jax.ShapeDtypeStruct(dw4.shape, F32),
                   jax.ShapeDtypeStruct(kv16.shape, F32)],
        input_output_aliases={6: 0, 7: 1},
        compiler_params=_params(1),
    )(dyx, proj, proj, sx16, kv16, wo4_g, dproj, dw4)


def _dp_unit(u):
    g = u // 2
    pos = jnp.where(g < G_VB, g, jnp.where(g < G_Q, g + 2, jnp.where(g < G_GA, g - 3, g)))
    return 2 * pos + u % 2


def _scatter_copies(srcs, lands, send, recv):
    x, y, c = lax.axis_index("x"), lax.axis_index("y"), lax.axis_index("c")
    copies = []
    for n in range(N_DEV - 1):
        flip = n + 1
        px = 1 - x if flip & 4 else x
        py = 1 - y if flip & 2 else y
        pc = 1 - c if flip & 1 else c
        for t, (src, land) in enumerate(zip(srcs, lands)):
            copies.append(pltpu.make_async_remote_copy(
                src_ref=src.at[4 * px + 2 * py + pc], dst_ref=land.at[n], send_sem=send.at[t, n],
                recv_sem=recv.at[t, n], device_id=(px, py, pc), device_id_type=MESH))
    return copies


def _scatter_sems(n_tensors):
    return [pltpu.SemaphoreType.DMA((n_tensors, N_DEV - 1)), pltpu.SemaphoreType.DMA((n_tensors, N_DEV - 1))]


def _landing(shard_shape, lead=()):
    return jax.ShapeDtypeStruct((*lead, N_DEV - 1, *shard_shape), BF16)


def _w_in_grad(ut, dproj, tk, dw4_16, dwkv16):
    n_rows = dproj.shape[0]
    n_k = n_rows // tk
    n_u = N_GROUPS * D // UNIT
    per_shard = W_IN_SHARD // UNIT

    def body(ut_ref, dp_ref, dw4_ref, dwkv_ref, out_ref, out16_ref, l4_ref, lkv_ref, acc, send, recv):
        u, t = pl.program_id(0), pl.program_id(1)

        def copies():
            return _scatter_copies([dw4_ref.at[w] for w in range(4)] + [dwkv_ref],
                                   [l4_ref.at[w] for w in range(4)] + [lkv_ref], send, recv)

        @pl.when((u == 0) & (t == 0))
        def _():
            for cp in copies():
                cp.start()

        @pl.when(t == 0)
        def _():
            acc[...] = jnp.zeros_like(acc)

        acc[...] += _dot(ut_ref[...], dp_ref[...])

        @pl.when(t == n_k - 1)
        def _():
            out_ref[0] = acc[...]
            out16_ref[0] = acc[...].astype(BF16)

        @pl.when((u == n_u - 1) & (t == n_k - 1))
        def _():
            for cp in copies():
                cp.wait()

    out_spec = pl.BlockSpec((1, D, UNIT), lambda u, t: (u // per_shard, 0, u % per_shard))
    return pl.pallas_call(
        body, name="w_in_grad", grid=(n_u, n_k),
        in_specs=[pl.BlockSpec((D, tk), lambda u, t: (0, t)), pl.BlockSpec((tk, UNIT), lambda u, t: (t, _dp_unit(u))),
                  ANY, ANY],
        out_specs=[out_spec, out_spec, ANY, ANY],
        out_shape=[jax.ShapeDtypeStruct((N_DEV, D, W_IN_SHARD), F32), jax.ShapeDtypeStruct((N_DEV, D, W_IN_SHARD), BF16),
                   _landing(dw4_16.shape[2:], lead=(4,)), _landing(dwkv16.shape[1:])],
        scratch_shapes=[pltpu.VMEM((D, UNIT), F32)] + _scatter_sems(5),
        compiler_params=_params(2),
    )(ut, dproj, dw4_16, dwkv16)


def _x_grad(dproj, win_g, x, dh, norm_g, dwin16, tm):
    n_rows = x.shape[0]
    n_tiles = n_rows // tm
    per_shard = W_IN_SHARD // UNIT

    def w_unit(half):
        def index(i, g):
            u = 2 * g + half
            return (u // per_shard, 0, u % per_shard)
        return pl.BlockSpec((1, D, UNIT), index)

    def dp_index(i, g):
        return (i, _dp_unit(2 * g) // 2)

    def body(dp_ref, wa_ref, wb_ref, x_ref, dh_ref, g_ref, dwin_ref, gx_ref, dg_ref, land_ref, acc, send, recv):
        i, g = pl.program_id(0), pl.program_id(1)

        @pl.when((i == 0) & (g == 0))
        def _():
            for cp in _scatter_copies([dwin_ref], [land_ref], send, recv):
                cp.start()
            dg_ref[...] = jnp.zeros_like(dg_ref)

        @pl.when(g == 0)
        def _():
            acc[...] = jnp.zeros_like(acc)

        acc[...] += _dot_nt(dp_ref[:, :UNIT], wa_ref[0]) + _dot_nt(dp_ref[:, UNIT:], wb_ref[0])

        @pl.when(g == N_GROUPS - 1)
        def _():
            du = acc[...]
            xf = x_ref[...]
            r = lax.rsqrt(_mean(xf * xf) + EPS)
            xn = xf * r
            dun = du * g_ref[...]
            gx_ref[...] = dh_ref[...] + r * (dun - xn * _mean(dun * xn))
            dg_ref[...] += _fold8(du * xn)

        @pl.when((i == n_tiles - 1) & (g == N_GROUPS - 1))
        def _():
            for cp in _scatter_copies([dwin_ref], [land_ref], send, recv):
                cp.wait()

    return pl.pallas_call(
        body, name="x_grad", grid=(n_tiles, N_GROUPS),
        in_specs=[pl.BlockSpec((tm, D), dp_index), w_unit(0), w_unit(1), pl.BlockSpec((tm, D), lambda i, g: (i, 0)),
                  pl.BlockSpec((tm, D), lambda i, g: (i, 0)), _const((1, D)), ANY],
        out_specs=[pl.BlockSpec((tm, D), lambda i, g: (i, 0)), _const((SUBLANES, D)), ANY],
        out_shape=[jax.ShapeDtypeStruct((n_rows, D), F32), jax.ShapeDtypeStruct((SUBLANES, D), F32),
                   _landing(dwin16.shape[1:])],
        scratch_shapes=[pltpu.VMEM((tm, D), F32)] + _scatter_sems(1),
        compiler_params=_params(2),
    )(dproj, win_g, win_g, x, dh, norm_g, dwin16)


def _local_step(x, mem, target, norm_g, conv_b_b, ln_g, ln_b, mem_g, final_g, shards):
    n_rows = x.shape[0]
    tm = min(256, n_rows)
    big = min(1024, n_rows)
    u16, ut = _rmsnorm_fwd(x, norm_g, big)
    proj, win_g, wkv_g, wo4_g, cw_g = _proj_fwd_gather(u16, shards, big)
    cw_rows = cw_g.transpose(1, 0, 2).reshape((SUBLANES + HALO) * LANE_GROUPS, LANES)
    cw_a, cw_b = cw_rows[:SUBLANES * LANE_GROUPS], cw_rows[SUBLANES * LANE_GROUPS:]
    kv16, mn16 = _kv_fwd(mem, mem_g, wkv_g)
    sa16, ya = _branch_a_fwd(proj, wo4_g, cw_a, tm)
    cb, sb16, yb = _branch_b_fwd(proj, wo4_g, cw_b, conv_b_b, ln_g, ln_b, tm)
    sx16, yx = _branch_x_fwd(proj, kv16, wo4_g, tm)
    dh, dya, dyb, dyx, dproj, dw4, dfg, sq = _merge_fwd_bwd(proj, ya, yb, yx, x, target, wo4_g, final_g, tm)
    dproj, dw4, dwa = _branch_a_bwd(dya, proj, sa16, wo4_g, cw_a, dproj, dw4, tm)
    dproj, dw4, dwb, dbb, dlg, dlb = _branch_b_bwd(dyb, proj, cb, sb16, wo4_g, cw_b, ln_g, ln_b, dproj, dw4, tm)
    dproj, dw4, dkv = _branch_x_bwd(dyx, proj, sx16, kv16, wo4_g, dproj, dw4, tm)
    dwkv_g, dmg = _kv_bwd(dkv, mem, mem_g, mn16, wkv_g)
    dw4 = dw4.reshape(4, N_DEV, D // N_DEV, D)
    dwin_g, dwin16, land4, landkv = _w_in_grad(ut, dproj, min(4096, n_rows), dw4.astype(BF16), dwkv_g.astype(BF16))
    gx, dng, landin = _x_grad(dproj, win_g, x, dh, norm_g, dwin16, big)
    small = {SV_NORM_G: dng, SV_CONV_B_B: dbb, SV_LN_G: dlg, SV_LN_B: dlb, SV_MEM_G: dmg, SV_FINAL_G: dfg, SV_LOSS: sq}
    grads = [(dwin_g[None], landin[None]), (dw4, land4), (dwkv_g[None], landkv[None])]
    return gx, grads, small, dwa.reshape(K_A, D), dwb.reshape(K_B, D)


def _allgather_small(small, conv_rows):
    keys = sorted(small)

    def body(*refs):
        parts, (conv_ref, out_ref, mine, send, recv) = refs[:len(keys)], refs[len(keys):]
        x, y, c, chips = _place()
        me, sibling = 4 * x + 2 * y + c, (x, y, 1 - c)
        mine[pl.ds(0, SV_CONV_A), :] = jnp.zeros((SV_CONV_A, D), F32)
        for key, part in zip(keys, parts):
            mine[key:key + 1, :] = jnp.sum(part[...], axis=0, keepdims=True)
        mine[pl.ds(SV_CONV_A, SV_ROWS - SV_CONV_A), :] = conv_ref[...]
        out_ref[me] = mine[...]

        def copy(k, block, to, from_mine=False):
            return pltpu.make_async_remote_copy(
                src_ref=mine if from_mine else out_ref.at[block], dst_ref=out_ref.at[block],
                send_sem=send.at[k], recv_sem=recv.at[k], device_id=to, device_id_type=MESH)

        first = [copy(0, me, sibling, from_mine=True)]
        first += [copy(1 + j, me, (*chip, c), from_mine=True) for j, chip in enumerate(chips)]
        for cp in first:
            cp.start()
        passed = []
        for j, (px, py) in enumerate(chips):
            block = 4 * px + 2 * py + c
            copy(1 + j, block, sibling).wait_recv()
            passed.append(copy(4 + j, block, sibling))
            passed[-1].start()
        copy(0, 4 * x + 2 * y + 1 - c, sibling).wait_recv()
        for j, (px, py) in enumerate(chips):
            copy(4 + j, 4 * px + 2 * py + 1 - c, sibling).wait_recv()
        for cp in first + passed:
            cp.wait_send()

    vmem = pl.BlockSpec(memory_space=pltpu.VMEM)
    return pl.pallas_call(
        body, name="allgather_small",
        in_specs=[vmem] * (len(keys) + 1), out_specs=vmem,
        out_shape=jax.ShapeDtypeStruct((N_DEV, SV_ROWS, D), F32),
        scratch_shapes=[pltpu.VMEM((SV_ROWS, D), F32), pltpu.SemaphoreType.DMA((7,)), pltpu.SemaphoreType.DMA((7,))],
    )(*[small[k] for k in keys], conv_rows)


def _adamw(w, g, m, v):
    m = ADAM_B1 * m + (1.0 - ADAM_B1) * g
    v = ADAM_B2 * v + (1.0 - ADAM_B2) * (g * g)
    m_hat = m / (1.0 - ADAM_B1 ** ADAM_STEP)
    v_hat = v / (1.0 - ADAM_B2 ** ADAM_STEP)
    return -ADAM_LR * (m_hat / (jnp.sqrt(v_hat) + ADAM_EPS) + ADAM_WD * w), m, v


def _adamw_shard(own, landed, piece, k_arr, w, m, v, tr):
    n_r, n_c = w.shape
    n_landed = landed.shape[1]

    def body(k_ref, own_ref, *refs):
        del k_ref
        landed_refs, (w_ref, m_ref, v_ref, g_out, d_out, m_out, v_out) = refs[:n_landed], refs[n_landed:]
        g = own_ref[0, 0]
        for landed_ref in landed_refs:
            g = g + landed_ref[0, 0].astype(F32)
        g_out[...] = g
        d_out[...], m_out[...], v_out[...] = _adamw(w_ref[...], g, m_ref[...], v_ref[...])

    blk = (1, 1, tr, n_c)
    flat = pl.BlockSpec((tr, n_c), lambda r, k: (r, 0))
    return pl.pallas_call(
        body, name="adamw_shard",
        grid_spec=pltpu.PrefetchScalarGridSpec(
            num_scalar_prefetch=1, grid=(n_r // tr,),
            in_specs=[pl.BlockSpec(blk, lambda r, k: (piece, k[0], r, 0))]
            + [pl.BlockSpec(blk, functools.partial(lambda r, k, j: (piece, j, r, 0), j=j)) for j in range(n_landed)]
            + [flat] * 3,
            out_specs=[flat] * 4),
        out_shape=[jax.ShapeDtypeStruct((n_r, n_c), F32)] * 4,
        compiler_params=_params(1),
    )(k_arr, own, *([landed] * n_landed), w, m, v)


def _adamw_small(gathered, k_arr, vectors, conv_a, conv_b):
    n_vec = len(vectors)
    cols = D // N_DEV

    def body(k_ref, full_ref, cols_ref, *refs):
        del k_ref
        ins, outs = refs[:3 * (n_vec + 2)], refs[3 * (n_vec + 2):-2]
        tot, tot_cols = refs[-2:]
        tot[...] = full_ref[0]
        tot_cols[...] = cols_ref[0]
        for dev in range(1, N_DEV):
            tot[...] += full_ref[dev]
            tot_cols[...] += cols_ref[dev]
        loss = (0.5 / D) * jnp.sum(tot[SV_LOSS:SV_LOSS + 1, :])
        outs[0][...] = jnp.full(outs[0].shape, loss, F32)
        grads = [tot[n:n + 1, :] for n in range(n_vec)]
        grads += [tot_cols[pl.ds(SV_CONV_A, K_A), :], tot_cols[pl.ds(SV_CONV_B, K_B), :]]
        for n, g in enumerate(grads):
            w_ref, m_ref, v_ref = ins[3 * n:3 * n + 3]
            g_out, d_out, m_out, v_out = outs[1 + 4 * n:5 + 4 * n]
            g_out[...] = g
            d_out[...], m_out[...], v_out[...] = _adamw(w_ref[...], g, m_ref[...], v_ref[...])

    weights = list(vectors) + [conv_a, conv_b]
    flat_in = [a for wmv in weights for a in wmv]
    out_shape = [jax.ShapeDtypeStruct((SUBLANES, 128), F32)]
    for wmv in weights:
        out_shape += [jax.ShapeDtypeStruct(wmv[0].shape, F32)] * 4
    return pl.pallas_call(
        body, name="adamw_small",
        grid_spec=pltpu.PrefetchScalarGridSpec(
            num_scalar_prefetch=1, grid=(1,),
            in_specs=[pl.BlockSpec((N_DEV, SV_ROWS, D), lambda i, k: (0, 0, 0)),
                      pl.BlockSpec((N_DEV, SV_ROWS, cols), lambda i, k: (0, 0, k[0]))]
            + [pl.BlockSpec(a.shape, lambda i, k: (0, 0)) for a in flat_in],
            out_specs=[pl.BlockSpec(s.shape, lambda i, k: (0, 0)) for s in out_shape],
            scratch_shapes=[pltpu.VMEM((SV_ROWS, D), F32), pltpu.VMEM((SV_ROWS, cols), F32)]),
        out_shape=out_shape,
        compiler_params=_params(1),
    )(k_arr, gathered, gathered, *flat_in)


def kernel(x, mem, norm_g, w_in, conv_a_w, w_out_a, conv_b_w, conv_b_b, ln_b_g, ln_b_b, w_out_b, mem_norm_g, w_kv, w_out_x, w_o, final_g, loss_target, m_norm_g, m_w_in, m_conv_a_w, m_w_out_a, m_conv_b_w, m_conv_b_b, m_ln_b_g, m_ln_b_b, m_w_out_b, m_mem_norm_g, m_w_kv, m_w_out_x, m_w_o, m_final_g, v_norm_g, v_w_in, v_conv_a_w, v_w_out_a, v_conv_b_w, v_conv_b_b, v_ln_b_g, v_ln_b_b, v_w_out_b, v_mem_norm_g, v_w_kv, v_w_out_x, v_w_o, v_final_g):
    xi, yi, ci = lax.axis_index("x"), lax.axis_index("y"), lax.axis_index("c")
    k_arr = jnp.reshape(4 * xi + 2 * yi + ci, (1,)).astype(jnp.int32)

    cw = jnp.concatenate([jnp.pad(conv_a_w[0], ((0, SUBLANES - K_A), (0, 0))),
                          jnp.pad(conv_b_w[0], ((0, HALO - K_B), (0, 0)))], axis=0)
    wo4 = jnp.stack([w_out_a[0], w_out_b[0], w_out_x[0], w_o[0]]).astype(BF16)
    shards = [w_in[0].astype(BF16), w_kv[0].astype(BF16), wo4, cw]

    final_g2 = final_g.reshape(1, D)
    gx, grads, small, dwa, dwb = _local_step(
        x[0], mem[0], loss_target[0], norm_g, conv_b_b, ln_b_g, ln_b_b, mem_norm_g, final_g2, shards)

    conv_rows = jnp.concatenate([jnp.pad(dwa, ((0, SUBLANES - K_A), (0, 0))),
                                 jnp.pad(dwb, ((0, HALO - K_B), (0, 0)))], axis=0)
    gathered_small = _allgather_small(small, conv_rows)

    tiles = [256, D // N_DEV, 256]

    def shard(a, l, w, m, v):
        return _adamw_shard(grads[a][0], grads[a][1], l, k_arr, w[0], m[0], v[0], tiles[a])

    res = {
        "w_in": shard(0, 0, w_in, m_w_in, v_w_in),
        "w_out_a": shard(1, 0, w_out_a, m_w_out_a, v_w_out_a),
        "w_out_b": shard(1, 1, w_out_b, m_w_out_b, v_w_out_b),
        "w_out_x": shard(1, 2, w_out_x, m_w_out_x, v_w_out_x),
        "w_o": shard(1, 3, w_o, m_w_o, v_w_o),
        "w_kv": shard(2, 0, w_kv, m_w_kv, v_w_kv),
    }
    res = {name: tuple(r[None] for r in four) for name, four in res.items()}
    vectors = [(norm_g, m_norm_g, v_norm_g), (conv_b_b, m_conv_b_b, v_conv_b_b), (ln_b_g, m_ln_b_g, v_ln_b_g),
               (ln_b_b, m_ln_b_b, v_ln_b_b), (mem_norm_g, m_mem_norm_g, v_mem_norm_g),
               (final_g2, m_final_g.reshape(1, D), v_final_g.reshape(1, D))]
    out = _adamw_small(gathered_small, k_arr, vectors, (conv_a_w[0], m_conv_a_w[0], v_conv_a_w[0]),
                       (conv_b_w[0], m_conv_b_w[0], v_conv_b_w[0]))
    loss = out[0][0, 0]
    names = ["norm_g", "conv_b_b", "ln_b_g", "ln_b_b", "mem_norm_g", "final_g", "conv_a_w", "conv_b_w"]
    for n, name in enumerate(names):
        four = out[1 + 4 * n:5 + 4 * n]
        if name == "final_g":
            four = [r.reshape(D) for r in four]
        elif name.startswith("conv_") and name.endswith("_w"):
            four = [r[None] for r in four]
        res[name] = tuple(four)

    order = ["norm_g", "w_in", "conv_a_w", "w_out_a", "conv_b_w", "conv_b_b", "ln_b_g", "ln_b_b", "w_out_b",
             "mem_norm_g", "w_kv", "w_out_x", "w_o", "final_g"]
    return (loss, gx[None], *[res[n][0] for n in order], *[res[n][1] for n in order],
            *[res[n][2] for n in order], *[res[n][3] for n in order])
```

```python
import functools

import jax
import jax.numpy as jnp
from jax import lax
from jax.experimental import pallas as pl
from jax.experimental.pallas import tpu as pltpu

F32, BF16 = jnp.float32, jnp.bfloat16
D = 1024
N_DEV = 8
N_HEADS = 4
HEAD_DIM = D // N_HEADS
N_GROUPS = 12
W_IN_SHARD = N_GROUPS * D // N_DEV
UNIT = 512
X_GRAD_K = 3 * D
K_A, K_B = 3, 31
EPS = 1e-6
HALO = 32
SUBLANES = 8
LANES = 128
LANE_GROUPS = D // LANES
UNROLL_A, UNROLL_B = 8, 4
UNROLL_WGRAD_B = 4
WGRAD_TAPS = 16
CONV_PARTIAL_SUMS = 4
VMEM_LIMIT = 56 << 20
MESH = pl.DeviceIdType.MESH
ANY = pl.BlockSpec(memory_space=pl.ANY)

G_BA, G_CA, G_XA, G_ZA, G_VB, G_GB, G_ZB, G_Q, G_ZX, G_GA, G_GBB, G_GX = range(N_GROUPS)
DP_POS = (0, 1, 2, 3, 6, 7, 8, 4, 5, 9, 10, 11)

ADAM_LR, ADAM_B1, ADAM_B2, ADAM_EPS, ADAM_WD, ADAM_STEP = 0.001, 0.9, 0.999, 1e-08, 0.01, 10

SV_NORM_G, SV_CONV_B_B, SV_LN_G, SV_LN_B, SV_MEM_G, SV_FINAL_G, SV_LOSS = range(7)
SV_CONV_A, SV_CONV_B, SV_ROWS = 8, 16, 48


def _dot(a, b):
    return jnp.dot(a, b, preferred_element_type=F32)


def _dot_nt(a, b):
    return lax.dot_general(a, b, (((1,), (1,)), ((), ())), preferred_element_type=F32)


def _dot_tn(a, b):
    return lax.dot_general(a, b, (((0,), (0,)), ((), ())), preferred_element_type=F32)


def _silu_and_grad(z):
    s = jax.nn.sigmoid(z)
    return z * s, s * (1.0 + z * (1.0 - s))


def _fold8(a):
    return a.reshape(a.shape[0] // SUBLANES, SUBLANES, a.shape[1]).sum(axis=0)


def _mean(a):
    return jnp.mean(a, axis=-1, keepdims=True)


def _params(n_grid):
    return pltpu.CompilerParams(dimension_semantics=("arbitrary",) * n_grid, vmem_limit_bytes=VMEM_LIMIT)


def _rows(tm, col=0):
    return pl.BlockSpec((tm, D), lambda i: (i, col))


def _prev_halo(tm, col=0):
    return pl.BlockSpec((HALO, D), lambda i: (jnp.maximum(i * (tm // HALO) - 1, 0), col))


def _next_halo(tm, n_rows, col=0):
    last = n_rows // HALO - 1
    return pl.BlockSpec((HALO, D), lambda i: (jnp.minimum((i + 1) * (tm // HALO), last), col))


def _const(shape):
    return pl.BlockSpec(shape, lambda *_: (0,) * len(shape))


def _w_out_spec(which):
    return pl.BlockSpec((N_DEV, None, D // N_DEV, D), lambda *_: (0, which, 0, 0))


def _to_time_major(t_ref, row0, x):
    n = x.shape[0]
    for j in range(LANE_GROUPS):
        t_ref[pl.ds(row0 * LANE_GROUPS + j, n, stride=LANE_GROUPS), :] = x[:, j * LANES:(j + 1) * LANES]


def _from_time_major(t_ref, n):
    return jnp.concatenate([t_ref[pl.ds(j, n, stride=LANE_GROUPS), :] for j in range(LANE_GROUPS)], axis=-1)


def _row(ref, t):
    start = t * LANE_GROUPS
    if not isinstance(start, int):
        start = pl.multiple_of(start, LANE_GROUPS)
    return ref[pl.ds(start, LANE_GROUPS), :]


def _conv(o_ref, e_ref, w_ref, taps, n_rows, unroll, bias_ref=None):
    weights = [_row(w_ref, k) for k, _ in taps]

    def chunk(c, carry):
        t0 = c * unroll
        n_part = min(CONV_PARTIAL_SUMS, len(taps))
        acc = [[None] * n_part for _ in range(unroll)]
        for n, (_, off) in enumerate(taps):
            for u in range(unroll):
                term = weights[n] * _row(e_ref, t0 + off + u)
                acc[u][n % n_part] = term if acc[u][n % n_part] is None else acc[u][n % n_part] + term
        for u in range(unroll):
            parts = acc[u]
            while len(parts) > 1:
                parts = [parts[n] + parts[n + 1] for n in range(0, len(parts) - 1, 2)] + parts[len(parts) & ~1:]
            out = parts[0]
            if bias_ref is not None:
                out = out + bias_ref[...]
            o_ref[pl.ds(pl.multiple_of((t0 + u) * LANE_GROUPS, LANE_GROUPS), LANE_GROUPS), :] = out
        return carry

    lax.fori_loop(0, n_rows // unroll, chunk, 0)


def _conv_wgrad(dw_ref, d_ref, e_ref, taps, n_rows, unroll):
    for first in range(0, len(taps), WGRAD_TAPS):
        group = taps[first:first + WGRAD_TAPS]
        lo = min(off for _, off in group)
        hi = max(off for _, off in group)

        def chunk(c, accs, group=group, lo=lo, hi=hi):
            t0 = c * unroll
            d = [_row(d_ref, t0 + u) for u in range(unroll)]
            window = [_row(e_ref, t0 + lo + n) for n in range(hi - lo + unroll)]
            accs = list(accs)
            for n, (_, off) in enumerate(group):
                for u in range(unroll):
                    accs[n] = accs[n] + d[u] * window[off - lo + u]
            return tuple(accs)

        zeros = tuple(jnp.zeros((LANE_GROUPS, LANES), F32) for _ in group)
        accs = lax.fori_loop(0, n_rows // unroll, chunk, zeros)
        for (k, _), acc in zip(group, accs):
            dw_ref[pl.ds(k * LANE_GROUPS, LANE_GROUPS), :] += acc


FWD_TAPS_A = [(k, HALO - (K_A - 1) + k) for k in range(K_A)]
BWD_TAPS_A = [(k, K_A - 1 - k) for k in range(K_A)]
FWD_TAPS_B = [(k, HALO - (K_B - 1) + k) for k in range(K_B)]
BWD_TAPS_B = [(k, K_B - 1 - k) for k in range(K_B)]


def _time_major(n_rows):
    return pltpu.VMEM((n_rows * LANE_GROUPS, LANES), F32)


def _kv_fwd(mem, mem_g, wkv_g):
    m_len = mem.shape[0]

    def body(mem_ref, g_ref, w_ref, kv_ref, mn_ref):
        mf = mem_ref[...]
        r = lax.rsqrt(_mean(mf * mf) + EPS)
        mn = ((mf * r) * g_ref[...]).astype(BF16)
        mn_ref[...] = mn
        for b in range(2 * N_HEADS):
            kv_ref[b] = _dot(mn, w_ref[b]).astype(BF16)

    return pl.pallas_call(
        body, name="kv_fwd", grid=(1,),
        in_specs=[_const((m_len, D)), _const((1, D)), _const((2 * N_HEADS, D, HEAD_DIM))],
        out_specs=[_const((2 * N_HEADS, m_len, HEAD_DIM)), _const((m_len, D))],
        out_shape=[jax.ShapeDtypeStruct((2 * N_HEADS, m_len, HEAD_DIM), BF16), jax.ShapeDtypeStruct((m_len, D), BF16)],
        compiler_params=_params(1),
    )(mem, mem_g, wkv_g)


def _kv_bwd(dkv, mem, mem_g, mn16, wkv_g):
    m_len = mem.shape[0]

    def body(dkv_ref, mem_ref, g_ref, mn_ref, w_ref, dw_ref, dg_ref):
        mn = mn_ref[...]
        dmn = jnp.zeros((m_len, D), F32)
        for b in range(2 * N_HEADS):
            d16 = dkv_ref[b].astype(BF16)
            dw_ref[b] = _dot_tn(mn, d16)
            dmn = dmn + _dot_nt(d16, w_ref[b])
        mf = mem_ref[...]
        r = lax.rsqrt(_mean(mf * mf) + EPS)
        dg_ref[...] = _fold8(dmn * (mf * r))

    return pl.pallas_call(
        body, name="kv_bwd", grid=(1,),
        in_specs=[_const((2 * N_HEADS, m_len, HEAD_DIM)), _const((m_len, D)), _const((1, D)), _const((m_len, D)),
                  _const((2 * N_HEADS, D, HEAD_DIM))],
        out_specs=[_const((2 * N_HEADS, D, HEAD_DIM)), _const((SUBLANES, D))],
        out_shape=[jax.ShapeDtypeStruct((2 * N_HEADS, D, HEAD_DIM), F32), jax.ShapeDtypeStruct((SUBLANES, D), F32)],
        compiler_params=_params(1),
    )(dkv, mem, mem_g, mn16, wkv_g)


def _rmsnorm_fwd(x, norm_g, tm):
    n_rows = x.shape[0]

    def body(x_ref, g_ref, u_ref, ut_ref):
        xf = x_ref[...]
        u = (xf * lax.rsqrt(_mean(xf * xf) + EPS)) * g_ref[...]
        u_ref[...] = u.astype(BF16)
        ut_ref[...] = u.T.astype(BF16)

    return pl.pallas_call(
        body, name="rmsnorm_fwd", grid=(n_rows // tm,),
        in_specs=[_rows(tm), _const((1, D))],
        out_specs=[_rows(tm), pl.BlockSpec((D, tm), lambda i: (0, i))],
        out_shape=[jax.ShapeDtypeStruct((n_rows, D), BF16), jax.ShapeDtypeStruct((D, n_rows), BF16)],
        compiler_params=_params(1),
    )(x, norm_g)


def _place():
    x, y, c = lax.axis_index("x"), lax.axis_index("y"), lax.axis_index("c")
    other_chips = [(1 - x, y), (x, 1 - y), (1 - x, 1 - y)]
    return x, y, c, other_chips


def _arrival_order():
    x, y, c, chips = _place()
    order = [4 * x + 2 * y + c, 4 * x + 2 * y + 1 - c]
    for px, py in chips:
        order += [4 * px + 2 * py + c, 4 * px + 2 * py + 1 - c]
    return order


def _proj_fwd_gather(u16, blocks, tm):
    n = len(blocks)
    n_rows = u16.shape[0]
    n_i = n_rows // tm

    def body(order_ref, u_ref, *refs):
        src, proj_ref, out = refs[:n], refs[n], refs[n + 1:2 * n + 1]
        wbuf, stage_sem, send, recv, own_sem = refs[2 * n + 1:]
        p, i = pl.program_id(0), pl.program_id(1)
        x, y, c, chips = _place()
        me, sibling = 4 * x + 2 * y + c, (x, y, 1 - c)

        def copy(t, k, block, to, from_input=False):
            return pltpu.make_async_remote_copy(
                src_ref=src[t] if from_input else out[t].at[block], dst_ref=out[t].at[block],
                send_sem=send.at[t, k], recv_sem=recv.at[t, k], device_id=to, device_id_type=MESH)

        def own_copies():
            return [pltpu.make_async_copy(src[t], out[t].at[me], own_sem.at[t]) for t in range(n)]

        def first_copies():
            first = []
            for t in range(n):
                first.append(copy(t, 0, me, sibling, from_input=True))
                first += [copy(t, 1 + j, me, (*chip, c), from_input=True) for j, chip in enumerate(chips)]
            return first

        def stage(slot, block):
            return pltpu.make_async_copy(out[0].at[block], wbuf.at[slot], stage_sem.at[slot])

        @pl.when((p == 0) & (i == 0))
        def _():
            for cp in own_copies() + first_copies():
                cp.start()
            mine = pltpu.make_async_copy(src[0], wbuf.at[0], stage_sem.at[0])
            mine.start()
            mine.wait()

        @pl.when((p > 0) & (i == 0))
        def _():
            stage(p % 2, order_ref[p]).wait()

        proj_ref[...] = _dot(u_ref[...], wbuf[p % 2])

        for nxt in range(1, N_DEV):
            @pl.when((p == nxt - 1) & (i == n_i - 1))
            def _(nxt=nxt):
                if nxt == 1:
                    block = 4 * x + 2 * y + 1 - c
                    copy(0, 0, block, sibling).wait_recv()
                else:
                    j, passed_on = divmod(nxt - 2, 2)
                    px, py = chips[j]
                    if passed_on:
                        block = 4 * px + 2 * py + 1 - c
                        copy(0, 4 + j, block, sibling).wait_recv()
                    else:
                        block = 4 * px + 2 * py + c
                        copy(0, 1 + j, block, sibling).wait_recv()
                        copy(0, 4 + j, block, sibling).start()
                stage(nxt % 2, block).start()

        @pl.when((p == N_DEV - 1) & (i == n_i - 1))
        def _():
            passed = [copy(0, 4 + j, 4 * px + 2 * py + c, sibling) for j, (px, py) in enumerate(chips)]
            for j, (px, py) in enumerate(chips):
                for t in range(1, n):
                    block = 4 * px + 2 * py + c
                    copy(t, 1 + j, block, sibling).wait_recv()
                    passed.append(copy(t, 4 + j, block, sibling))
                    passed[-1].start()
            for t in range(1, n):
                copy(t, 0, 4 * x + 2 * y + 1 - c, sibling).wait_recv()
                for j, (px, py) in enumerate(chips):
                    copy(t, 4 + j, 4 * px + 2 * py + 1 - c, sibling).wait_recv()
            for cp in first_copies() + passed:
                cp.wait_send()
            for cp in own_copies():
                cp.wait()

    return pl.pallas_call(
        body, name="proj_fwd_gather",
        grid_spec=pltpu.PrefetchScalarGridSpec(
            num_scalar_prefetch=1, grid=(N_DEV, n_i),
            in_specs=[pl.BlockSpec((tm, D), lambda p, i, order: (i, 0))] + [ANY] * n,
            out_specs=[pl.BlockSpec((tm, W_IN_SHARD), lambda p, i, order: (i, order[p]))] + [ANY] * n,
            scratch_shapes=[pltpu.VMEM((2, D, W_IN_SHARD), BF16), pltpu.SemaphoreType.DMA((2,)),
                            pltpu.SemaphoreType.DMA((n, 7)), pltpu.SemaphoreType.DMA((n, 7)),
                            pltpu.SemaphoreType.DMA((n,))]),
        out_shape=[jax.ShapeDtypeStruct((n_rows, N_GROUPS * D), F32)]
        + [jax.ShapeDtypeStruct((N_DEV, *b.shape), b.dtype) for b in blocks],
        compiler_params=_params(2),
    )(jnp.stack(_arrival_order()).astype(jnp.int32), u16, *blocks)


def _branch_a_fwd(proj, wo4_g, cw_a, tm):
    n_rows = proj.shape[0]

    def body(bp, cp, xp, za, cph, xph, w_ref, cw_ref, sa_ref, ya_ref, e_scr, o_scr):
        i = pl.program_id(0)
        _to_time_major(e_scr, 0, jnp.where(i > 0, cph[...] * xph[...], 0.0))
        _to_time_major(e_scr, HALO, cp[...] * xp[...])
        _conv(o_scr, e_scr, cw_ref, FWD_TAPS_A, tm, UNROLL_A)
        sa = (jax.nn.silu(za[...]) * (bp[...] * _from_time_major(o_scr, tm))).astype(BF16)
        sa_ref[...] = sa
        ya_ref[...] = _dot(sa, w_ref[...].reshape(D, D))

    return pl.pallas_call(
        body, name="branch_a_fwd", grid=(n_rows // tm,),
        in_specs=[_rows(tm, G_BA), _rows(tm, G_CA), _rows(tm, G_XA), _rows(tm, G_ZA),
                  _prev_halo(tm, G_CA), _prev_halo(tm, G_XA), _w_out_spec(0), _const(cw_a.shape)],
        out_specs=[_rows(tm), _rows(tm)],
        out_shape=[jax.ShapeDtypeStruct((n_rows, D), BF16), jax.ShapeDtypeStruct((n_rows, D), F32)],
        scratch_shapes=[_time_major(tm + HALO), _time_major(tm)],
        compiler_params=_params(1),
    )(proj, proj, proj, proj, proj, proj, wo4_g, cw_a)


def _layernorm_parts(cb, lg, lb):
    xc = cb - _mean(cb)
    rstd = lax.rsqrt(_mean(xc * xc) + EPS)
    xhat = xc * rstd
    return xhat, rstd, xhat * lg + lb


def _branch_b_fwd(proj, wo4_g, cw_b, conv_b_b, ln_g, ln_b, tm):
    n_rows = proj.shape[0]

    def body(vb, gb, zb, vbh, gbh, w_ref, cw_ref, bb_ref, lg_ref, lb_ref, cb_ref, sb_ref, yb_ref, e_scr, o_scr):
        i = pl.program_id(0)
        _to_time_major(e_scr, 0, jnp.where(i > 0, vbh[...] * jax.nn.sigmoid(gbh[...]), 0.0))
        _to_time_major(e_scr, HALO, vb[...] * jax.nn.sigmoid(gb[...]))
        _conv(o_scr, e_scr, cw_ref, FWD_TAPS_B, tm, UNROLL_B, bias_ref=bb_ref)
        cb = _from_time_major(o_scr, tm)
        cb_ref[...] = cb
        _, _, ln = _layernorm_parts(cb, lg_ref[...], lb_ref[...])
        sb = (jax.nn.silu(zb[...]) * jax.nn.silu(ln)).astype(BF16)
        sb_ref[...] = sb
        yb_ref[...] = _dot(sb, w_ref[...].reshape(D, D))

    return pl.pallas_call(
        body, name="branch_b_fwd", grid=(n_rows // tm,),
        in_specs=[_rows(tm, G_VB), _rows(tm, G_GB), _rows(tm, G_ZB), _prev_halo(tm, G_VB), _prev_halo(tm, G_GB),
                  _w_out_spec(1), _const(cw_b.shape), _const((LANE_GROUPS, LANES)), _const((1, D)), _const((1, D))],
        out_specs=[_rows(tm), _rows(tm), _rows(tm)],
        out_shape=[jax.ShapeDtypeStruct((n_rows, D), F32), jax.ShapeDtypeStruct((n_rows, D), BF16),
                   jax.ShapeDtypeStruct((n_rows, D), F32)],
        scratch_shapes=[_time_major(tm + HALO), _time_major(tm)],
        compiler_params=_params(1),
    )(proj, proj, proj, proj, proj, wo4_g, cw_b, conv_b_b.reshape(LANE_GROUPS, LANES), ln_g, ln_b)


def _attention(q16, kv_ref):
    probs, outs = [], []
    for h in range(N_HEADS):
        s = _dot_nt(q16[:, h * HEAD_DIM:(h + 1) * HEAD_DIM], kv_ref[h]) * (HEAD_DIM ** -0.5)
        e = jnp.exp(s - jnp.max(s, axis=-1, keepdims=True))
        p = e / jnp.sum(e, axis=-1, keepdims=True)
        probs.append(p)
        outs.append(_dot(p.astype(BF16), kv_ref[N_HEADS + h]))
    return probs, outs


def _branch_x_fwd(proj, kv16, wo4_g, tm):
    n_rows = proj.shape[0]

    def body(q, zx, kv_ref, w_ref, sx_ref, yx_ref):
        _, outs = _attention(q[...].astype(BF16), kv_ref)
        sx = (jax.nn.silu(zx[...]) * jnp.concatenate(outs, axis=-1)).astype(BF16)
        sx_ref[...] = sx
        yx_ref[...] = _dot(sx, w_ref[...].reshape(D, D))

    return pl.pallas_call(
        body, name="branch_x_fwd", grid=(n_rows // tm,),
        in_specs=[_rows(tm, G_Q), _rows(tm, G_ZX), _const(kv16.shape), _w_out_spec(2)],
        out_specs=[_rows(tm), _rows(tm)],
        out_shape=[jax.ShapeDtypeStruct((n_rows, D), BF16), jax.ShapeDtypeStruct((n_rows, D), F32)],
        compiler_params=_params(1),
    )(proj, proj, kv16, wo4_g)


def _merge_fwd_bwd(proj, ya, yb, yx, x, target, wo4_g, final_g, tm):
    n_rows = proj.shape[0]
    inv_d = 1.0 / D

    def body(ga, gb, gx, ya_ref, yb_ref, yx_ref, x_ref, t_ref, w_ref, fg_ref,
             dh_ref, dya_ref, dyb_ref, dyx_ref, dp_ref, dw_ref, dfg_ref, sq_ref):
        i = pl.program_id(0)
        wo = w_ref[...].reshape(D, D)
        sig = [jax.nn.sigmoid(g[...]) for g in (ga, gb, gx)]
        ys = [ya_ref[...], yb_ref[...], yx_ref[...]]
        m16 = (sig[0] * ys[0] + sig[1] * ys[1] + sig[2] * ys[2]).astype(BF16)
        h = x_ref[...] + _dot(m16, wo)
        r = lax.rsqrt(_mean(h * h) + EPS)
        hn = h * r
        fg = fg_ref[...]
        err = hn * fg - t_ref[...]
        dy = err * inv_d
        dhn = dy * fg
        dh = r * (dhn - hn * _mean(dhn * hn))
        dh_ref[...] = dh
        dh16 = dh.astype(BF16)
        dm = _dot_nt(dh16, wo)
        for n, out in enumerate((dya_ref, dyb_ref, dyx_ref)):
            out[...] = (sig[n] * dm).astype(BF16)
            dp_ref[:, n * D:(n + 1) * D] = (dm * ys[n] * (sig[n] * (1.0 - sig[n]))).astype(BF16)

        @pl.when(i == 0)
        def _():
            dw_ref[...] = jnp.zeros_like(dw_ref)
            dfg_ref[...] = jnp.zeros_like(dfg_ref)
            sq_ref[...] = jnp.zeros_like(sq_ref)

        dw_ref[0] += _dot_tn(m16, dh16)
        dfg_ref[...] += _fold8(dy * hn)
        sq_ref[...] += _fold8(err * err)

    vec = jax.ShapeDtypeStruct((SUBLANES, D), F32)
    return pl.pallas_call(
        body, name="merge_fwd_bwd", grid=(n_rows // tm,),
        in_specs=[_rows(tm, G_GA), _rows(tm, G_GBB), _rows(tm, G_GX), _rows(tm), _rows(tm), _rows(tm), _rows(tm),
                  _rows(tm), _w_out_spec(3), _const((1, D))],
        out_specs=[_rows(tm), _rows(tm), _rows(tm), _rows(tm), pl.BlockSpec((tm, 3 * D), lambda i: (i, 3)),
                   pl.BlockSpec((1, D, D), lambda i: (3, 0, 0)), _const((SUBLANES, D)), _const((SUBLANES, D))],
        out_shape=[jax.ShapeDtypeStruct((n_rows, D), F32), jax.ShapeDtypeStruct((n_rows, D), BF16),
                   jax.ShapeDtypeStruct((n_rows, D), BF16), jax.ShapeDtypeStruct((n_rows, D), BF16),
                   jax.ShapeDtypeStruct((n_rows, N_GROUPS * D), BF16), jax.ShapeDtypeStruct((4, D, D), F32), vec, vec],
        compiler_params=_params(1),
    )(proj, proj, proj, ya, yb, yx, x, target, wo4_g, final_g)


def _branch_a_bwd(dya, proj, sa16, wo4_g, cw_a, dproj, dw4, tm):
    n_rows = proj.shape[0]
    n_tiles = n_rows // tm

    def body(dya_ref, bp, cp, xp, za, sa_ref, dyan, bpn, zan, cph, xph, w_ref, cw_ref, dp_in, dw_in,
             dp_ref, dw_ref, dwa_ref, e1, e2, o_scr):
        del dp_in, dw_in
        i = pl.program_id(0)
        woa = w_ref[...].reshape(D, D)
        dya16 = dya_ref[...]
        _to_time_major(e1, 0, jnp.where(i > 0, cph[...] * xph[...], 0.0))
        _to_time_major(e1, HALO, cp[...] * xp[...])
        _conv(o_scr, e1, cw_ref, FWD_TAPS_A, tm, UNROLL_A)
        ca = _from_time_major(o_scr, tm)
        dsa = _dot_nt(dya16, woa)
        silu_z, dsilu_z = _silu_and_grad(za[...])
        t = dsa * silu_z
        dp_ref[:, 0 * D:1 * D] = (t * ca).astype(BF16)
        dp_ref[:, 3 * D:4 * D] = (dsa * (bp[...] * ca) * dsilu_z).astype(BF16)
        dsan = _dot_nt(dyan[...], woa)
        dcan = (dsan * jax.nn.silu(zan[...])) * bpn[...]
        _to_time_major(e2, 0, t * bp[...])
        _to_time_major(e2, tm, jnp.where(i < n_tiles - 1, dcan, 0.0))

        @pl.when(i == 0)
        def _():
            dw_ref[...] = jnp.zeros_like(dw_ref)
            dwa_ref[...] = jnp.zeros_like(dwa_ref)

        _conv_wgrad(dwa_ref, e2, e1, FWD_TAPS_A, tm, UNROLL_A)
        dw_ref[0] += _dot_tn(sa_ref[...], dya16)
        _conv(o_scr, e2, cw_ref, BWD_TAPS_A, tm, UNROLL_A)
        dprod = _from_time_major(o_scr, tm)
        dp_ref[:, 1 * D:2 * D] = (dprod * xp[...]).astype(BF16)
        dp_ref[:, 2 * D:3 * D] = (dprod * cp[...]).astype(BF16)

    return pl.pallas_call(
        body, name="branch_a_bwd", grid=(n_tiles,),
        in_specs=[_rows(tm), _rows(tm, G_BA), _rows(tm, G_CA), _rows(tm, G_XA), _rows(tm, G_ZA), _rows(tm),
                  _next_halo(tm, n_rows), _next_halo(tm, n_rows, G_BA), _next_halo(tm, n_rows, G_ZA),
                  _prev_halo(tm, G_CA), _prev_halo(tm, G_XA), _w_out_spec(0), _const(cw_a.shape), ANY, ANY],
        out_specs=[pl.BlockSpec((tm, 4 * D), lambda i: (i, 0)), pl.BlockSpec((1, D, D), lambda i: (0, 0, 0)),
                   _const((K_A * LANE_GROUPS, LANES))],
        out_shape=[jax.ShapeDtypeStruct(dproj.shape, BF16), jax.ShapeDtypeStruct(dw4.shape, F32),
                   jax.ShapeDtypeStruct((K_A * LANE_GROUPS, LANES), F32)],
        input_output_aliases={13: 0, 14: 1},
        scratch_shapes=[_time_major(tm + HALO), _time_major(tm + HALO), _time_major(tm)],
        compiler_params=_params(1),
    )(dya, proj, proj, proj, proj, sa16, dya, proj, proj, proj, proj, wo4_g, cw_a, dproj, dw4)


def _branch_b_bwd(dyb, proj, cb, sb16, wo4_g, cw_b, ln_g, ln_b, dproj, dw4, tm):
    n_rows = proj.shape[0]
    n_tiles = n_rows // tm

    def body(dyb_ref, zb, cb_ref, vb, gb, sb_ref, dybn, zbn, cbn, vbh, gbh, w_ref, cw_ref, lg_ref, lb_ref,
             dp_in, dw_in, dp_ref, dw_ref, dwb_ref, dbb_ref, dlg_ref, dlb_ref, e1, e2, o_scr):
        del dp_in, dw_in
        i = pl.program_id(0)
        wob = w_ref[...].reshape(D, D)
        lg, lb = lg_ref[...], lb_ref[...]

        def conv_out_grad(dy16, z, c):
            dsb = _dot_nt(dy16, wob)
            xhat, rstd, ln = _layernorm_parts(c, lg, lb)
            sw, dsw = _silu_and_grad(ln)
            sz, dsz = _silu_and_grad(z)
            dln = (dsb * sz) * dsw
            dxhat = dln * lg
            dcb = rstd * (dxhat - _mean(dxhat) - xhat * _mean(dxhat * xhat))
            return dsb * sw * dsz, dln, xhat, dcb

        dyb16 = dyb_ref[...]
        dzb, dln, xhat, dcb = conv_out_grad(dyb16, zb[...], cb_ref[...])
        dp_ref[:, 2 * D:3 * D] = dzb.astype(BF16)
        _, _, _, dcbn = conv_out_grad(dybn[...], zbn[...], cbn[...])
        _to_time_major(e2, 0, dcb)
        _to_time_major(e2, tm, jnp.where(i < n_tiles - 1, dcbn, 0.0))

        @pl.when(i == 0)
        def _():
            dw_ref[...] = jnp.zeros_like(dw_ref)
            dwb_ref[...] = jnp.zeros_like(dwb_ref)
            dbb_ref[...] = jnp.zeros_like(dbb_ref)
            dlg_ref[...] = jnp.zeros_like(dlg_ref)
            dlb_ref[...] = jnp.zeros_like(dlb_ref)

        dlg_ref[...] += _fold8(dln * xhat)
        dlb_ref[...] += _fold8(dln)
        dbb_ref[...] += _fold8(dcb)
        dw_ref[0] += _dot_tn(sb_ref[...], dyb16)
        sg = jax.nn.sigmoid(gb[...])
        _to_time_major(e1, 0, jnp.where(i > 0, vbh[...] * jax.nn.sigmoid(gbh[...]), 0.0))
        _to_time_major(e1, HALO, vb[...] * sg)
        _conv_wgrad(dwb_ref, e2, e1, FWD_TAPS_B, tm, UNROLL_WGRAD_B)
        _conv(o_scr, e2, cw_ref, BWD_TAPS_B, tm, UNROLL_B)
        dglu = _from_time_major(o_scr, tm)
        dp_ref[:, 0 * D:1 * D] = (dglu * sg).astype(BF16)
        dp_ref[:, 1 * D:2 * D] = (dglu * vb[...] * (sg * (1.0 - sg))).astype(BF16)

    vec = jax.ShapeDtypeStruct((SUBLANES, D), F32)
    return pl.pallas_call(
        body, name="branch_b_bwd", grid=(n_tiles,),
        in_specs=[_rows(tm), _rows(tm, G_ZB), _rows(tm), _rows(tm, G_VB), _rows(tm, G_GB), _rows(tm),
                  _next_halo(tm, n_rows), _next_halo(tm, n_rows, G_ZB), _next_halo(tm, n_rows),
                  _prev_halo(tm, G_VB), _prev_halo(tm, G_GB), _w_out_spec(1), _const(cw_b.shape), _const((1, D)),
                  _const((1, D)), ANY, ANY],
        out_specs=[pl.BlockSpec((tm, 3 * D), lambda i: (i, 2)), pl.BlockSpec((1, D, D), lambda i: (1, 0, 0)),
                   _const((K_B * LANE_GROUPS, LANES)), _const((SUBLANES, D)), _const((SUBLANES, D)),
                   _const((SUBLANES, D))],
        out_shape=[jax.ShapeDtypeStruct(dproj.shape, BF16), jax.ShapeDtypeStruct(dw4.shape, F32),
                   jax.ShapeDtypeStruct((K_B * LANE_GROUPS, LANES), F32), vec, vec, vec],
        input_output_aliases={15: 0, 16: 1},
        scratch_shapes=[_time_major(tm + HALO), _time_major(tm + HALO), _time_major(tm)],
        compiler_params=_params(1),
    )(dyb, proj, cb, proj, proj, sb16, dyb, proj, cb, proj, proj, wo4_g, cw_b, ln_g, ln_b, dproj, dw4)


def _branch_x_bwd(dyx, proj, sx16, kv16, wo4_g, dproj, dw4, tm):
    n_rows = proj.shape[0]
    scale = HEAD_DIM ** -0.5

    def body(dyx_ref, q, zx, sx_ref, kv_ref, w_ref, dp_in, dw_in, dp_ref, dw_ref, dkv_ref):
        del dp_in, dw_in
        i = pl.program_id(0)
        dyx16 = dyx_ref[...]
        q16 = q[...].astype(BF16)
        probs, outs = _attention(q16, kv_ref)
        dsx = _dot_nt(dyx16, w_ref[...].reshape(D, D))
        silu_z, dsilu_z = _silu_and_grad(zx[...])
        dp_ref[:, D:2 * D] = (dsx * jnp.concatenate(outs, axis=-1) * dsilu_z).astype(BF16)
        do16 = (dsx * silu_z).astype(BF16)

        @pl.when(i == 0)
        def _():
            dw_ref[...] = jnp.zeros_like(dw_ref)
            dkv_ref[...] = jnp.zeros_like(dkv_ref)

        for h in range(N_HEADS):
            cols = slice(h * HEAD_DIM, (h + 1) * HEAD_DIM)
            p = probs[h]
            dprob = _dot_nt(do16[:, cols], kv_ref[N_HEADS + h])
            ds16 = ((p * (dprob - jnp.sum(p * dprob, axis=-1, keepdims=True))) * scale).astype(BF16)
            dp_ref[:, cols] = _dot(ds16, kv_ref[h]).astype(BF16)
            dkv_ref[h] += _dot_tn(ds16, q16[:, cols])
            dkv_ref[N_HEADS + h] += _dot_tn(p.astype(BF16), do16[:, cols])
        dw_ref[0] += _dot_tn(sx_ref[...], dyx16)

    return pl.pallas_call(
        body, name="branch_x_bwd", grid=(n_rows // tm,),
        in_specs=[_rows(tm), _rows(tm, G_Q), _rows(tm, G_ZX), _rows(tm), _const(kv16.shape), _w_out_spec(2), ANY, ANY],
        out_specs=[pl.BlockSpec((tm, 2 * D), lambda i: (i, 2)), pl.BlockSpec((1, D, D), lambda i: (2, 0, 0)),
                   _const(kv16.shape)],
        out_shape=[jax.ShapeDtypeStruct(dproj.shape, BF16), jax.ShapeDtypeStruct(dw4.shape, F32),
                   jax.ShapeDtypeStruct(kv16.shape, F32)],
        input_output_aliases={6: 0, 7: 1},
        compiler_params=_params(1),
    )(dyx, proj, proj, sx16, kv16, wo4_g, dproj, dw4)


def _dp_unit(u):
    g = u // 2
    pos = jnp.where(g < G_VB, g, jnp.where(g < G_Q, g + 2, jnp.where(g < G_GA, g - 3, g)))
    return 2 * pos + u % 2


def _scatter_copies(srcs, lands, send, recv):
    x, y, c = lax.axis_index("x"), lax.axis_index("y"), lax.axis_index("c")
    copies = []
    for n in range(N_DEV - 1):
        flip = n + 1
        px = 1 - x if flip & 4 else x
        py = 1 - y if flip & 2 else y
        pc = 1 - c if flip & 1 else c
        for t, (src, land) in enumerate(zip(srcs, lands)):
            copies.append(pltpu.make_async_remote_copy(
                src_ref=src.at[4 * px + 2 * py + pc], dst_ref=land.at[n], send_sem=send.at[t, n],
                recv_sem=recv.at[t, n], device_id=(px, py, pc), device_id_type=MESH))
    return copies


def _scatter_sems(n_tensors):
    return [pltpu.SemaphoreType.DMA((n_tensors, N_DEV - 1)), pltpu.SemaphoreType.DMA((n_tensors, N_DEV - 1))]


def _landing(shard_shape, lead=()):
    return jax.ShapeDtypeStruct((*lead, N_DEV - 1, *shard_shape), BF16)


def _w_in_grad(ut, dproj, tk, dw4_16, dwkv16):
    n_rows = dproj.shape[0]
    n_k = n_rows // tk
    n_u = N_GROUPS * D // UNIT
    per_shard = W_IN_SHARD // UNIT

    def body(ut_ref, dp_ref, dw4_ref, dwkv_ref, out_ref, out16_ref, l4_ref, lkv_ref, acc, send, recv):
        u, t = pl.program_id(0), pl.program_id(1)

        def copies():
            return _scatter_copies([dw4_ref.at[w] for w in range(4)] + [dwkv_ref],
                                   [l4_ref.at[w] for w in range(4)] + [lkv_ref], send, recv)

        @pl.when((u == 0) & (t == 0))
        def _():
            for cp in copies():
                cp.start()

        @pl.when(t == 0)
        def _():
            acc[...] = jnp.zeros_like(acc)

        acc[...] += _dot(ut_ref[...], dp_ref[...])

        @pl.when(t == n_k - 1)
        def _():
            out_ref[0] = acc[...]
            out16_ref[0] = acc[...].astype(BF16)

        @pl.when((u == n_u - 1) & (t == n_k - 1))
        def _():
            for cp in copies():
                cp.wait()

    out_spec = pl.BlockSpec((1, D, UNIT), lambda u, t: (u // per_shard, 0, u % per_shard))
    return pl.pallas_call(
        body, name="w_in_grad", grid=(n_u, n_k),
        in_specs=[pl.BlockSpec((D, tk), lambda u, t: (0, t)), pl.BlockSpec((tk, UNIT), lambda u, t: (t, _dp_unit(u))),
                  ANY, ANY],
        out_specs=[out_spec, out_spec, ANY, ANY],
        out_shape=[jax.ShapeDtypeStruct((N_DEV, D, W_IN_SHARD), F32), jax.ShapeDtypeStruct((N_DEV, D, W_IN_SHARD), BF16),
                   _landing(dw4_16.shape[2:], lead=(4,)), _landing(dwkv16.shape[1:])],
        scratch_shapes=[pltpu.VMEM((D, UNIT), F32)] + _scatter_sems(5),
        compiler_params=_params(2),
    )(ut, dproj, dw4_16, dwkv16)


def _w_in_transposed(win_g):
    per_shard = W_IN_SHARD // UNIT

    def body(w_ref, wt_ref):
        wt_ref[...] = w_ref[0].astype(F32).T.astype(BF16)

    return pl.pallas_call(
        body, name="w_in_transposed", grid=(N_GROUPS * D // UNIT,),
        in_specs=[pl.BlockSpec((1, D, UNIT), lambda u: (u // per_shard, 0, u % per_shard))],
        out_specs=pl.BlockSpec((UNIT, D), lambda u: (_dp_unit(u), 0)),
        out_shape=jax.ShapeDtypeStruct((N_GROUPS * D, D), BF16),
        compiler_params=_params(1),
    )(win_g)


def _x_grad(dproj, win_t, x, dh, norm_g, dwin16, tm):
    n_rows = x.shape[0]
    n_tiles = n_rows // tm
    n_k = N_GROUPS * D // X_GRAD_K

    def body(dp_ref, wt_ref, x_ref, dh_ref, g_ref, dwin_ref, gx_ref, dg_ref, land_ref, acc, send, recv):
        i, g = pl.program_id(0), pl.program_id(1)

        @pl.when((i == 0) & (g == 0))
        def _():
            for cp in _scatter_copies([dwin_ref], [land_ref], send, recv):
                cp.start()
            dg_ref[...] = jnp.zeros_like(dg_ref)

        @pl.when(g == 0)
        def _():
            acc[...] = _dot(dp_ref[...], wt_ref[...])

        @pl.when(g > 0)
        def _():
            acc[...] += _dot(dp_ref[...], wt_ref[...])

        @pl.when(g == n_k - 1)
        def _():
            du = acc[...]
            xf = x_ref[...]
            r = lax.rsqrt(_mean(xf * xf) + EPS)
            xn = xf * r
            dun = du * g_ref[...]
            gx_ref[...] = dh_ref[...] + r * (dun - xn * _mean(dun * xn))
            dg_ref[...] += _fold8(du * xn)

        @pl.when((i == n_tiles - 1) & (g == n_k - 1))
        def _():
            for cp in _scatter_copies([dwin_ref], [land_ref], send, recv):
                cp.wait()

    return pl.pallas_call(
        body, name="x_grad", grid=(n_tiles, n_k),
        in_specs=[pl.BlockSpec((tm, X_GRAD_K), lambda i, g: (i, g)), pl.BlockSpec((X_GRAD_K, D), lambda i, g: (g, 0)),
                  pl.BlockSpec((tm, D), lambda i, g: (i, 0)), pl.BlockSpec((tm, D), lambda i, g: (i, 0)),
                  _const((1, D)), ANY],
        out_specs=[pl.BlockSpec((tm, D), lambda i, g: (i, 0)), _const((SUBLANES, D)), ANY],
        out_shape=[jax.ShapeDtypeStruct((n_rows, D), F32), jax.ShapeDtypeStruct((SUBLANES, D), F32),
                   _landing(dwin16.shape[1:])],
        scratch_shapes=[pltpu.VMEM((tm, D), F32)] + _scatter_sems(1),
        compiler_params=_params(2),
    )(dproj, win_t, x, dh, norm_g, dwin16)


def _local_step(x, mem, target, norm_g, conv_b_b, ln_g, ln_b, mem_g, final_g, shards):
    n_rows = x.shape[0]
    tm = min(512, n_rows)
    big = min(1024, n_rows)
    u16, ut = _rmsnorm_fwd(x, norm_g, big)
    proj, win_g, wkv_g, wo4_g, cw_g = _proj_fwd_gather(u16, shards, big)
    cw_rows = cw_g.transpose(1, 0, 2).reshape((SUBLANES + HALO) * LANE_GROUPS, LANES)
    cw_a, cw_b = cw_rows[:SUBLANES * LANE_GROUPS], cw_rows[SUBLANES * LANE_GROUPS:]
    kv16, mn16 = _kv_fwd(mem, mem_g, wkv_g)
    sa16, ya = _branch_a_fwd(proj, wo4_g, cw_a, tm)
    cb, sb16, yb = _branch_b_fwd(proj, wo4_g, cw_b, conv_b_b, ln_g, ln_b, tm)
    sx16, yx = _branch_x_fwd(proj, kv16, wo4_g, tm)
    dh, dya, dyb, dyx, dproj, dw4, dfg, sq = _merge_fwd_bwd(proj, ya, yb, yx, x, target, wo4_g, final_g,
                                                             min(256, n_rows))
    dproj, dw4, dwa = _branch_a_bwd(dya, proj, sa16, wo4_g, cw_a, dproj, dw4, tm)
    dproj, dw4, dwb, dbb, dlg, dlb = _branch_b_bwd(dyb, proj, cb, sb16, wo4_g, cw_b, ln_g, ln_b, dproj, dw4, tm)
    dproj, dw4, dkv = _branch_x_bwd(dyx, proj, sx16, kv16, wo4_g, dproj, dw4, tm)
    dwkv_g, dmg = _kv_bwd(dkv, mem, mem_g, mn16, wkv_g)
    dw4 = dw4.reshape(4, N_DEV, D // N_DEV, D)
    dwin_g, dwin16, land4, landkv = _w_in_grad(ut, dproj, min(4096, n_rows), dw4.astype(BF16), dwkv_g.astype(BF16))
    gx, dng, landin = _x_grad(dproj, _w_in_transposed(win_g), x, dh, norm_g, dwin16, tm)
    small = {SV_NORM_G: dng, SV_CONV_B_B: dbb, SV_LN_G: dlg, SV_LN_B: dlb, SV_MEM_G: dmg, SV_FINAL_G: dfg, SV_LOSS: sq}
    grads = [(dwin_g[None], landin[None]), (dw4, land4), (dwkv_g[None], landkv[None])]
    return gx, grads, small, dwa.reshape(K_A, D), dwb.reshape(K_B, D)


def _allgather_small(small, conv_rows):
    keys = sorted(small)

    def body(*refs):
        parts, (conv_ref, out_ref, mine, send, recv) = refs[:len(keys)], refs[len(keys):]
        x, y, c, chips = _place()
        me, sibling = 4 * x + 2 * y + c, (x, y, 1 - c)
        mine[pl.ds(0, SV_CONV_A), :] = jnp.zeros((SV_CONV_A, D), F32)
        for key, part in zip(keys, parts):
            mine[key:key + 1, :] = jnp.sum(part[...], axis=0, keepdims=True)
        mine[pl.ds(SV_CONV_A, SV_ROWS - SV_CONV_A), :] = conv_ref[...]
        out_ref[me] = mine[...]

        def copy(k, block, to, from_mine=False):
            return pltpu.make_async_remote_copy(
                src_ref=mine if from_mine else out_ref.at[block], dst_ref=out_ref.at[block],
                send_sem=send.at[k], recv_sem=recv.at[k], device_id=to, device_id_type=MESH)

        first = [copy(0, me, sibling, from_mine=True)]
        first += [copy(1 + j, me, (*chip, c), from_mine=True) for j, chip in enumerate(chips)]
        for cp in first:
            cp.start()
        passed = []
        for j, (px, py) in enumerate(chips):
            block = 4 * px + 2 * py + c
            copy(1 + j, block, sibling).wait_recv()
            passed.append(copy(4 + j, block, sibling))
            passed[-1].start()
        copy(0, 4 * x + 2 * y + 1 - c, sibling).wait_recv()
        for j, (px, py) in enumerate(chips):
            copy(4 + j, 4 * px + 2 * py + 1 - c, sibling).wait_recv()
        for cp in first + passed:
            cp.wait_send()

    vmem = pl.BlockSpec(memory_space=pltpu.VMEM)
    return pl.pallas_call(
        body, name="allgather_small",
        in_specs=[vmem] * (len(keys) + 1), out_specs=vmem,
        out_shape=jax.ShapeDtypeStruct((N_DEV, SV_ROWS, D), F32),
        scratch_shapes=[pltpu.VMEM((SV_ROWS, D), F32), pltpu.SemaphoreType.DMA((7,)), pltpu.SemaphoreType.DMA((7,))],
    )(*[small[k] for k in keys], conv_rows)


def _adamw(w, g, m, v):
    m = ADAM_B1 * m + (1.0 - ADAM_B1) * g
    v = ADAM_B2 * v + (1.0 - ADAM_B2) * (g * g)
    m_hat = m / (1.0 - ADAM_B1 ** ADAM_STEP)
    v_hat = v / (1.0 - ADAM_B2 ** ADAM_STEP)
    return -ADAM_LR * (m_hat / (jnp.sqrt(v_hat) + ADAM_EPS) + ADAM_WD * w), m, v


def _adamw_shard(own, landed, piece, k_arr, w, m, v, tr):
    n_r, n_c = w.shape
    n_landed = landed.shape[1]

    def body(k_ref, own_ref, *refs):
        del k_ref
        landed_refs, (w_ref, m_ref, v_ref, g_out, d_out, m_out, v_out) = refs[:n_landed], refs[n_landed:]
        g = own_ref[0, 0]
        for landed_ref in landed_refs:
            g = g + landed_ref[0, 0].astype(F32)
        g_out[...] = g
        d_out[...], m_out[...], v_out[...] = _adamw(w_ref[...], g, m_ref[...], v_ref[...])

    blk = (1, 1, tr, n_c)
    flat = pl.BlockSpec((tr, n_c), lambda r, k: (r, 0))
    return pl.pallas_call(
        body, name="adamw_shard",
        grid_spec=pltpu.PrefetchScalarGridSpec(
            num_scalar_prefetch=1, grid=(n_r // tr,),
            in_specs=[pl.BlockSpec(blk, lambda r, k: (piece, k[0], r, 0))]
            + [pl.BlockSpec(blk, functools.partial(lambda r, k, j: (piece, j, r, 0), j=j)) for j in range(n_landed)]
            + [flat] * 3,
            out_specs=[flat] * 4),
        out_shape=[jax.ShapeDtypeStruct((n_r, n_c), F32)] * 4,
        compiler_params=_params(1),
    )(k_arr, own, *([landed] * n_landed), w, m, v)


def _adamw_small(gathered, k_arr, vectors, conv_a, conv_b):
    n_vec = len(vectors)
    cols = D // N_DEV

    def body(k_ref, full_ref, cols_ref, *refs):
        del k_ref
        ins, outs = refs[:3 * (n_vec + 2)], refs[3 * (n_vec + 2):-2]
        tot, tot_cols = refs[-2:]
        tot[...] = full_ref[0]
        tot_cols[...] = cols_ref[0]
        for dev in range(1, N_DEV):
            tot[...] += full_ref[dev]
            tot_cols[...] += cols_ref[dev]
        loss = (0.5 / D) * jnp.sum(tot[SV_LOSS:SV_LOSS + 1, :])
        outs[0][...] = jnp.full(outs[0].shape, loss, F32)
        grads = [tot[n:n + 1, :] for n in range(n_vec)]
        grads += [tot_cols[pl.ds(SV_CONV_A, K_A), :], tot_cols[pl.ds(SV_CONV_B, K_B), :]]
        for n, g in enumerate(grads):
            w_ref, m_ref, v_ref = ins[3 * n:3 * n + 3]
            g_out, d_out, m_out, v_out = outs[1 + 4 * n:5 + 4 * n]
            g_out[...] = g
            d_out[...], m_out[...], v_out[...] = _adamw(w_ref[...], g, m_ref[...], v_ref[...])

    weights = list(vectors) + [conv_a, conv_b]
    flat_in = [a for wmv in weights for a in wmv]
    out_shape = [jax.ShapeDtypeStruct((SUBLANES, 128), F32)]
    for wmv in weights:
        out_shape += [jax.ShapeDtypeStruct(wmv[0].shape, F32)] * 4
    return pl.pallas_call(
        body, name="adamw_small",
        grid_spec=pltpu.PrefetchScalarGridSpec(
            num_scalar_prefetch=1, grid=(1,),
            in_specs=[pl.BlockSpec((N_DEV, SV_ROWS, D), lambda i, k: (0, 0, 0)),
                      pl.BlockSpec((N_DEV, SV_ROWS, cols), lambda i, k: (0, 0, k[0]))]
            + [pl.BlockSpec(a.shape, lambda i, k: (0, 0)) for a in flat_in],
            out_specs=[pl.BlockSpec(s.shape, lambda i, k: (0, 0)) for s in out_shape],
            scratch_shapes=[pltpu.VMEM((SV_ROWS, D), F32), pltpu.VMEM((SV_ROWS, cols), F32)]),
        out_shape=out_shape,
        compiler_params=_params(1),
    )(k_arr, gathered, gathered, *flat_in)


def kernel(x, mem, norm_g, w_in, conv_a_w, w_out_a, conv_b_w, conv_b_b, ln_b_g, ln_b_b, w_out_b, mem_norm_g, w_kv, w_out_x, w_o, final_g, loss_target, m_norm_g, m_w_in, m_conv_a_w, m_w_out_a, m_conv_b_w, m_conv_b_b, m_ln_b_g, m_ln_b_b, m_w_out_b, m_mem_norm_g, m_w_kv, m_w_out_x, m_w_o, m_final_g, v_norm_g, v_w_in, v_conv_a_w, v_w_out_a, v_conv_b_w, v_conv_b_b, v_ln_b_g, v_ln_b_b, v_w_out_b, v_mem_norm_g, v_w_kv, v_w_out_x, v_w_o, v_final_g):
    xi, yi, ci = lax.axis_index("x"), lax.axis_index("y"), lax.axis_index("c")
    k_arr = jnp.reshape(4 * xi + 2 * yi + ci, (1,)).astype(jnp.int32)

    cw = jnp.concatenate([jnp.pad(conv_a_w[0], ((0, SUBLANES - K_A), (0, 0))),
                          jnp.pad(conv_b_w[0], ((0, HALO - K_B), (0, 0)))], axis=0)
    wo4 = jnp.stack([w_out_a[0], w_out_b[0], w_out_x[0], w_o[0]]).astype(BF16)
    shards = [w_in[0].astype(BF16), w_kv[0].astype(BF16), wo4, cw]

    final_g2 = final_g.reshape(1, D)
    gx, grads, small, dwa, dwb = _local_step(
        x[0], mem[0], loss_target[0], norm_g, conv_b_b, ln_b_g, ln_b_b, mem_norm_g, final_g2, shards)

    conv_rows = jnp.concatenate([jnp.pad(dwa, ((0, SUBLANES - K_A), (0, 0))),
                                 jnp.pad(dwb, ((0, HALO - K_B), (0, 0)))], axis=0)
    gathered_small = _allgather_small(small, conv_rows)

    tiles = [256, D // N_DEV, 256]

    def shard(a, l, w, m, v):
        return _adamw_shard(grads[a][0], grads[a][1], l, k_arr, w[0], m[0], v[0], tiles[a])

    res = {
        "w_in": shard(0, 0, w_in, m_w_in, v_w_in),
        "w_out_a": shard(1, 0, w_out_a, m_w_out_a, v_w_out_a),
        "w_out_b": shard(1, 1, w_out_b, m_w_out_b, v_w_out_b),
        "w_out_x": shard(1, 2, w_out_x, m_w_out_x, v_w_out_x),
        "w_o": shard(1, 3, w_o, m_w_o, v_w_o),
        "w_kv": shard(2, 0, w_kv, m_w_kv, v_w_kv),
    }
    res = {name: tuple(r[None] for r in four) for name, four in res.items()}
    vectors = [(norm_g, m_norm_g, v_norm_g), (conv_b_b, m_conv_b_b, v_conv_b_b), (ln_b_g, m_ln_b_g, v_ln_b_g),
               (ln_b_b, m_ln_b_b, v_ln_b_b), (mem_norm_g, m_mem_norm_g, v_mem_norm_g),
               (final_g2, m_final_g.reshape(1, D), v_final_g.reshape(1, D))]
    out = _adamw_small(gathered_small, k_arr, vectors, (conv_a_w[0], m_conv_a_w[0], v_conv_a_w[0]),
                       (conv_b_w[0], m_conv_b_w[0], v_conv_b_w[0]))
    loss = out[0][0, 0]
    names = ["norm_g", "conv_b_b", "ln_b_g", "ln_b_b", "mem_norm_g", "final_g", "conv_a_w", "conv_b_w"]
    for n, name in enumerate(names):
        four = out[1 + 4 * n:5 + 4 * n]
        if name == "final_g":
            four = [r.reshape(D) for r in four]
        elif name.startswith("conv_") and name.endswith("_w"):
            four = [r[None] for r in four]
        res[name] = tuple(four)

    order = ["norm_g", "w_in", "conv_a_w", "w_out_a", "conv_b_w", "conv_b_b", "ln_b_g", "ln_b_b", "w_out_b",
             "mem_norm_g", "w_kv", "w_out_x", "w_o", "final_g"]
    return (loss, gx[None], *[res[n][0] for n in order], *[res[n][1] for n in order],
            *[res[n][2] for n in order], *[res[n][3] for n in order])
```

```python
import functools

import jax
import jax.numpy as jnp
from jax import lax
from jax.experimental import pallas as pl
from jax.experimental.pallas import tpu as pltpu

F32, BF16 = jnp.float32, jnp.bfloat16
D = 1024
N_DEV = 8
N_HEADS = 4
HEAD_DIM = D // N_HEADS
N_GROUPS = 12
W_IN_SHARD = N_GROUPS * D // N_DEV
UNIT = 512
X_GRAD_K = 2 * D
K_A, K_B = 3, 31
EPS = 1e-6
HALO = 32
SUBLANES = 8
LANES = 128
LANE_GROUPS = D // LANES
UNROLL_A, UNROLL_B = 8, 4
UNROLL_WGRAD_B = 4
WGRAD_TAPS = 16
CONV_PARTIAL_SUMS = 4
VMEM_LIMIT = 56 << 20
MESH = pl.DeviceIdType.MESH
ANY = pl.BlockSpec(memory_space=pl.ANY)

G_BA, G_CA, G_XA, G_ZA, G_VB, G_GB, G_ZB, G_Q, G_ZX, G_GA, G_GBB, G_GX = range(N_GROUPS)
DP_POS = (0, 1, 2, 3, 6, 7, 8, 4, 5, 9, 10, 11)

ADAM_LR, ADAM_B1, ADAM_B2, ADAM_EPS, ADAM_WD, ADAM_STEP = 0.001, 0.9, 0.999, 1e-08, 0.01, 10

SV_NORM_G, SV_CONV_B_B, SV_LN_G, SV_LN_B, SV_MEM_G, SV_FINAL_G, SV_LOSS = range(7)
SV_CONV_A, SV_CONV_B, SV_ROWS = 8, 16, 48


def _dot(a, b):
    return jnp.dot(a, b, preferred_element_type=F32)


def _dot_nt(a, b):
    return lax.dot_general(a, b, (((1,), (1,)), ((), ())), preferred_element_type=F32)


def _dot_tn(a, b):
    return lax.dot_general(a, b, (((0,), (0,)), ((), ())), preferred_element_type=F32)


def _silu_and_grad(z):
    s = jax.nn.sigmoid(z)
    return z * s, s * (1.0 + z * (1.0 - s))


def _fold8(a):
    return a.reshape(a.shape[0] // SUBLANES, SUBLANES, a.shape[1]).sum(axis=0)


def _mean(a):
    return jnp.mean(a, axis=-1, keepdims=True)


def _params(n_grid):
    return pltpu.CompilerParams(dimension_semantics=("arbitrary",) * n_grid, vmem_limit_bytes=VMEM_LIMIT)


def _rows(tm, col=0):
    return pl.BlockSpec((tm, D), lambda i: (i, col))


def _prev_halo(tm, col=0):
    return pl.BlockSpec((HALO, D), lambda i: (jnp.maximum(i * (tm // HALO) - 1, 0), col))


def _next_halo(tm, n_rows, col=0):
    last = n_rows // HALO - 1
    return pl.BlockSpec((HALO, D), lambda i: (jnp.minimum((i + 1) * (tm // HALO), last), col))


def _const(shape):
    return pl.BlockSpec(shape, lambda *_: (0,) * len(shape))


def _w_out_spec(which):
    return pl.BlockSpec((N_DEV, None, D // N_DEV, D), lambda *_: (0, which, 0, 0))


def _to_time_major(t_ref, row0, x):
    n = x.shape[0]
    for j in range(LANE_GROUPS):
        t_ref[pl.ds(row0 * LANE_GROUPS + j, n, stride=LANE_GROUPS), :] = x[:, j * LANES:(j + 1) * LANES]


def _from_time_major(t_ref, n):
    return jnp.concatenate([t_ref[pl.ds(j, n, stride=LANE_GROUPS), :] for j in range(LANE_GROUPS)], axis=-1)


def _row(ref, t):
    start = t * LANE_GROUPS
    if not isinstance(start, int):
        start = pl.multiple_of(start, LANE_GROUPS)
    return ref[pl.ds(start, LANE_GROUPS), :]


def _conv(o_ref, e_ref, w_ref, taps, n_rows, unroll, bias_ref=None):
    weights = [_row(w_ref, k) for k, _ in taps]

    def chunk(c, carry):
        t0 = c * unroll
        n_part = min(CONV_PARTIAL_SUMS, len(taps))
        acc = [[None] * n_part for _ in range(unroll)]
        for n, (_, off) in enumerate(taps):
            for u in range(unroll):
                term = weights[n] * _row(e_ref, t0 + off + u)
                acc[u][n % n_part] = term if acc[u][n % n_part] is None else acc[u][n % n_part] + term
        for u in range(unroll):
            parts = acc[u]
            while len(parts) > 1:
                parts = [parts[n] + parts[n + 1] for n in range(0, len(parts) - 1, 2)] + parts[len(parts) & ~1:]
            out = parts[0]
            if bias_ref is not None:
                out = out + bias_ref[...]
            o_ref[pl.ds(pl.multiple_of((t0 + u) * LANE_GROUPS, LANE_GROUPS), LANE_GROUPS), :] = out
        return carry

    lax.fori_loop(0, n_rows // unroll, chunk, 0)


def _conv_wgrad(dw_ref, d_ref, e_ref, taps, n_rows, unroll):
    for first in range(0, len(taps), WGRAD_TAPS):
        group = taps[first:first + WGRAD_TAPS]
        lo = min(off for _, off in group)
        hi = max(off for _, off in group)

        def chunk(c, accs, group=group, lo=lo, hi=hi):
            t0 = c * unroll
            d = [_row(d_ref, t0 + u) for u in range(unroll)]
            window = [_row(e_ref, t0 + lo + n) for n in range(hi - lo + unroll)]
            accs = list(accs)
            for n, (_, off) in enumerate(group):
                for u in range(unroll):
                    accs[n] = accs[n] + d[u] * window[off - lo + u]
            return tuple(accs)

        zeros = tuple(jnp.zeros((LANE_GROUPS, LANES), F32) for _ in group)
        accs = lax.fori_loop(0, n_rows // unroll, chunk, zeros)
        for (k, _), acc in zip(group, accs):
            dw_ref[pl.ds(k * LANE_GROUPS, LANE_GROUPS), :] += acc


FWD_TAPS_A = [(k, HALO - (K_A - 1) + k) for k in range(K_A)]
BWD_TAPS_A = [(k, K_A - 1 - k) for k in range(K_A)]
FWD_TAPS_B = [(k, HALO - (K_B - 1) + k) for k in range(K_B)]
BWD_TAPS_B = [(k, K_B - 1 - k) for k in range(K_B)]


def _time_major(n_rows):
    return pltpu.VMEM((n_rows * LANE_GROUPS, LANES), F32)


def _kv_fwd(mem, mem_g, wkv_g):
    m_len = mem.shape[0]

    def body(mem_ref, g_ref, w_ref, kv_ref, mn_ref):
        mf = mem_ref[...]
        r = lax.rsqrt(_mean(mf * mf) + EPS)
        mn = ((mf * r) * g_ref[...]).astype(BF16)
        mn_ref[...] = mn
        for b in range(2 * N_HEADS):
            kv_ref[b] = _dot(mn, w_ref[b]).astype(BF16)

    return pl.pallas_call(
        body, name="kv_fwd", grid=(1,),
        in_specs=[_const((m_len, D)), _const((1, D)), _const((2 * N_HEADS, D, HEAD_DIM))],
        out_specs=[_const((2 * N_HEADS, m_len, HEAD_DIM)), _const((m_len, D))],
        out_shape=[jax.ShapeDtypeStruct((2 * N_HEADS, m_len, HEAD_DIM), BF16), jax.ShapeDtypeStruct((m_len, D), BF16)],
        compiler_params=_params(1),
    )(mem, mem_g, wkv_g)


def _kv_bwd(dkv, mem, mem_g, mn16, wkv_g):
    m_len = mem.shape[0]

    def body(dkv_ref, mem_ref, g_ref, mn_ref, w_ref, dw_ref, dg_ref):
        mn = mn_ref[...]
        dmn = jnp.zeros((m_len, D), F32)
        for b in range(2 * N_HEADS):
            d16 = dkv_ref[b].astype(BF16)
            dw_ref[b] = _dot_tn(mn, d16)
            dmn = dmn + _dot_nt(d16, w_ref[b])
        mf = mem_ref[...]
        r = lax.rsqrt(_mean(mf * mf) + EPS)
        dg_ref[...] = _fold8(dmn * (mf * r))

    return pl.pallas_call(
        body, name="kv_bwd", grid=(1,),
        in_specs=[_const((2 * N_HEADS, m_len, HEAD_DIM)), _const((m_len, D)), _const((1, D)), _const((m_len, D)),
                  _const((2 * N_HEADS, D, HEAD_DIM))],
        out_specs=[_const((2 * N_HEADS, D, HEAD_DIM)), _const((SUBLANES, D))],
        out_shape=[jax.ShapeDtypeStruct((2 * N_HEADS, D, HEAD_DIM), F32), jax.ShapeDtypeStruct((SUBLANES, D), F32)],
        compiler_params=_params(1),
    )(dkv, mem, mem_g, mn16, wkv_g)


def _rmsnorm_fwd(x, norm_g, tm):
    n_rows = x.shape[0]

    def body(x_ref, g_ref, u_ref, ut_ref):
        xf = x_ref[...]
        u = (xf * lax.rsqrt(_mean(xf * xf) + EPS)) * g_ref[...]
        u_ref[...] = u.astype(BF16)
        ut_ref[...] = u.T.astype(BF16)

    return pl.pallas_call(
        body, name="rmsnorm_fwd", grid=(n_rows // tm,),
        in_specs=[_rows(tm), _const((1, D))],
        out_specs=[_rows(tm), pl.BlockSpec((D, tm), lambda i: (0, i))],
        out_shape=[jax.ShapeDtypeStruct((n_rows, D), BF16), jax.ShapeDtypeStruct((D, n_rows), BF16)],
        compiler_params=_params(1),
    )(x, norm_g)


def _place():
    x, y, c = lax.axis_index("x"), lax.axis_index("y"), lax.axis_index("c")
    other_chips = [(1 - x, y), (x, 1 - y), (1 - x, 1 - y)]
    return x, y, c, other_chips


def _arrival_order():
    x, y, c, chips = _place()
    order = [4 * x + 2 * y + c, 4 * x + 2 * y + 1 - c]
    for px, py in chips:
        order += [4 * px + 2 * py + c, 4 * px + 2 * py + 1 - c]
    return order


def _proj_fwd_gather(u16, blocks, tm):
    n = len(blocks)
    n_rows = u16.shape[0]
    n_i = n_rows // tm
    per_shard = W_IN_SHARD // UNIT
    assert n_i >= per_shard

    def wt_index(p, i, order):
        return (_dp_unit(per_shard * order[p] + jnp.minimum(i, per_shard - 1)), 0)

    def body(order_ref, u_ref, *refs):
        src, proj_ref, wt_ref, out = refs[:n], refs[n], refs[n + 1], refs[n + 2:2 * n + 2]
        wbuf, stage_sem, send, recv, own_sem = refs[2 * n + 2:]
        p, i = pl.program_id(0), pl.program_id(1)
        x, y, c, chips = _place()
        me, sibling = 4 * x + 2 * y + c, (x, y, 1 - c)

        def copy(t, k, block, to, from_input=False):
            return pltpu.make_async_remote_copy(
                src_ref=src[t] if from_input else out[t].at[block], dst_ref=out[t].at[block],
                send_sem=send.at[t, k], recv_sem=recv.at[t, k], device_id=to, device_id_type=MESH)

        def own_copies():
            return [pltpu.make_async_copy(src[t], out[t].at[me], own_sem.at[t]) for t in range(n)]

        def first_copies():
            first = []
            for t in range(n):
                first.append(copy(t, 0, me, sibling, from_input=True))
                first += [copy(t, 1 + j, me, (*chip, c), from_input=True) for j, chip in enumerate(chips)]
            return first

        def stage(slot, block):
            return pltpu.make_async_copy(out[0].at[block], wbuf.at[slot], stage_sem.at[slot])

        @pl.when((p == 0) & (i == 0))
        def _():
            for cp in own_copies() + first_copies():
                cp.start()
            mine = pltpu.make_async_copy(src[0], wbuf.at[0], stage_sem.at[0])
            mine.start()
            mine.wait()

        @pl.when((p > 0) & (i == 0))
        def _():
            stage(p % 2, order_ref[p]).wait()

        proj_ref[...] = _dot(u_ref[...], wbuf[p % 2])
        for r in range(per_shard):
            @pl.when(i == r)
            def _(r=r):
                wt_ref[...] = wbuf[p % 2, :, r * UNIT:(r + 1) * UNIT].astype(F32).T.astype(BF16)

        for nxt in range(1, N_DEV):
            @pl.when((p == nxt - 1) & (i == n_i - 1))
            def _(nxt=nxt):
                if nxt == 1:
                    block = 4 * x + 2 * y + 1 - c
                    copy(0, 0, block, sibling).wait_recv()
                else:
                    j, passed_on = divmod(nxt - 2, 2)
                    px, py = chips[j]
                    if passed_on:
                        block = 4 * px + 2 * py + 1 - c
                        copy(0, 4 + j, block, sibling).wait_recv()
                    else:
                        block = 4 * px + 2 * py + c
                        copy(0, 1 + j, block, sibling).wait_recv()
                        copy(0, 4 + j, block, sibling).start()
                stage(nxt % 2, block).start()

        @pl.when((p == N_DEV - 1) & (i == n_i - 1))
        def _():
            passed = [copy(0, 4 + j, 4 * px + 2 * py + c, sibling) for j, (px, py) in enumerate(chips)]
            for j, (px, py) in enumerate(chips):
                for t in range(1, n):
                    block = 4 * px + 2 * py + c
                    copy(t, 1 + j, block, sibling).wait_recv()
                    passed.append(copy(t, 4 + j, block, sibling))
                    passed[-1].start()
            for t in range(1, n):
                copy(t, 0, 4 * x + 2 * y + 1 - c, sibling).wait_recv()
                for j, (px, py) in enumerate(chips):
                    copy(t, 4 + j, 4 * px + 2 * py + 1 - c, sibling).wait_recv()
            for cp in first_copies() + passed:
                cp.wait_send()
            for cp in own_copies():
                cp.wait()

    return pl.pallas_call(
        body, name="proj_fwd_gather",
        grid_spec=pltpu.PrefetchScalarGridSpec(
            num_scalar_prefetch=1, grid=(N_DEV, n_i),
            in_specs=[pl.BlockSpec((tm, D), lambda p, i, order: (i, 0))] + [ANY] * n,
            out_specs=[pl.BlockSpec((tm, W_IN_SHARD), lambda p, i, order: (i, order[p])),
                       pl.BlockSpec((UNIT, D), wt_index)] + [ANY] * n,
            scratch_shapes=[pltpu.VMEM((2, D, W_IN_SHARD), BF16), pltpu.SemaphoreType.DMA((2,)),
                            pltpu.SemaphoreType.DMA((n, 7)), pltpu.SemaphoreType.DMA((n, 7)),
                            pltpu.SemaphoreType.DMA((n,))]),
        out_shape=[jax.ShapeDtypeStruct((n_rows, N_GROUPS * D), F32), jax.ShapeDtypeStruct((N_GROUPS * D, D), BF16)]
        + [jax.ShapeDtypeStruct((N_DEV, *b.shape), b.dtype) for b in blocks],
        compiler_params=_params(2),
    )(jnp.stack(_arrival_order()).astype(jnp.int32), u16, *blocks)


def _branch_a_fwd(proj, wo4_g, cw_a, tm):
    n_rows = proj.shape[0]

    def body(bp, cp, xp, za, cph, xph, w_ref, cw_ref, sa_ref, ya_ref, e_scr, o_scr):
        i = pl.program_id(0)
        _to_time_major(e_scr, 0, jnp.where(i > 0, cph[...] * xph[...], 0.0))
        _to_time_major(e_scr, HALO, cp[...] * xp[...])
        _conv(o_scr, e_scr, cw_ref, FWD_TAPS_A, tm, UNROLL_A)
        sa = (jax.nn.silu(za[...]) * (bp[...] * _from_time_major(o_scr, tm))).astype(BF16)
        sa_ref[...] = sa
        ya_ref[...] = _dot(sa, w_ref[...].reshape(D, D))

    return pl.pallas_call(
        body, name="branch_a_fwd", grid=(n_rows // tm,),
        in_specs=[_rows(tm, G_BA), _rows(tm, G_CA), _rows(tm, G_XA), _rows(tm, G_ZA),
                  _prev_halo(tm, G_CA), _prev_halo(tm, G_XA), _w_out_spec(0), _const(cw_a.shape)],
        out_specs=[_rows(tm), _rows(tm)],
        out_shape=[jax.ShapeDtypeStruct((n_rows, D), BF16), jax.ShapeDtypeStruct((n_rows, D), F32)],
        scratch_shapes=[_time_major(tm + HALO), _time_major(tm)],
        compiler_params=_params(1),
    )(proj, proj, proj, proj, proj, proj, wo4_g, cw_a)


def _layernorm_parts(cb, lg, lb):
    xc = cb - _mean(cb)
    rstd = lax.rsqrt(_mean(xc * xc) + EPS)
    xhat = xc * rstd
    return xhat, rstd, xhat * lg + lb


def _branch_b_fwd(proj, wo4_g, cw_b, conv_b_b, ln_g, ln_b, tm):
    n_rows = proj.shape[0]

    def body(vb, gb, zb, vbh, gbh, w_ref, cw_ref, bb_ref, lg_ref, lb_ref, cb_ref, sb_ref, yb_ref, e_scr, o_scr):
        i = pl.program_id(0)
        _to_time_major(e_scr, 0, jnp.where(i > 0, vbh[...] * jax.nn.sigmoid(gbh[...]), 0.0))
        _to_time_major(e_scr, HALO, vb[...] * jax.nn.sigmoid(gb[...]))
        _conv(o_scr, e_scr, cw_ref, FWD_TAPS_B, tm, UNROLL_B, bias_ref=bb_ref)
        cb = _from_time_major(o_scr, tm)
        cb_ref[...] = cb
        _, _, ln = _layernorm_parts(cb, lg_ref[...], lb_ref[...])
        sb = (jax.nn.silu(zb[...]) * jax.nn.silu(ln)).astype(BF16)
        sb_ref[...] = sb
        yb_ref[...] = _dot(sb, w_ref[...].reshape(D, D))

    return pl.pallas_call(
        body, name="branch_b_fwd", grid=(n_rows // tm,),
        in_specs=[_rows(tm, G_VB), _rows(tm, G_GB), _rows(tm, G_ZB), _prev_halo(tm, G_VB), _prev_halo(tm, G_GB),
                  _w_out_spec(1), _const(cw_b.shape), _const((LANE_GROUPS, LANES)), _const((1, D)), _const((1, D))],
        out_specs=[_rows(tm), _rows(tm), _rows(tm)],
        out_shape=[jax.ShapeDtypeStruct((n_rows, D), F32), jax.ShapeDtypeStruct((n_rows, D), BF16),
                   jax.ShapeDtypeStruct((n_rows, D), F32)],
        scratch_shapes=[_time_major(tm + HALO), _time_major(tm)],
        compiler_params=_params(1),
    )(proj, proj, proj, proj, proj, wo4_g, cw_b, conv_b_b.reshape(LANE_GROUPS, LANES), ln_g, ln_b)


def _attention(q16, kv_ref):
    probs, outs = [], []
    for h in range(N_HEADS):
        s = _dot_nt(q16[:, h * HEAD_DIM:(h + 1) * HEAD_DIM], kv_ref[h]) * (HEAD_DIM ** -0.5)
        e = jnp.exp(s - jnp.max(s, axis=-1, keepdims=True))
        p = e / jnp.sum(e, axis=-1, keepdims=True)
        probs.append(p)
        outs.append(_dot(p.astype(BF16), kv_ref[N_HEADS + h]))
    return probs, outs


def _branch_x_fwd(proj, kv16, wo4_g, tm):
    n_rows = proj.shape[0]

    def body(q, zx, kv_ref, w_ref, sx_ref, yx_ref):
        _, outs = _attention(q[...].astype(BF16), kv_ref)
        sx = (jax.nn.silu(zx[...]) * jnp.concatenate(outs, axis=-1)).astype(BF16)
        sx_ref[...] = sx
        yx_ref[...] = _dot(sx, w_ref[...].reshape(D, D))

    return pl.pallas_call(
        body, name="branch_x_fwd", grid=(n_rows // tm,),
        in_specs=[_rows(tm, G_Q), _rows(tm, G_ZX), _const(kv16.shape), _w_out_spec(2)],
        out_specs=[_rows(tm), _rows(tm)],
        out_shape=[jax.ShapeDtypeStruct((n_rows, D), BF16), jax.ShapeDtypeStruct((n_rows, D), F32)],
        compiler_params=_params(1),
    )(proj, proj, kv16, wo4_g)


def _merge_fwd_bwd(proj, ya, yb, yx, x, target, wo4_g, final_g, tm):
    n_rows = proj.shape[0]
    inv_d = 1.0 / D

    def body(ga, gb, gx, ya_ref, yb_ref, yx_ref, x_ref, t_ref, w_ref, fg_ref,
             dh_ref, dya_ref, dyb_ref, dyx_ref, dp_ref, dw_ref, dfg_ref, sq_ref):
        i = pl.program_id(0)
        wo = w_ref[...].reshape(D, D)
        sig = [jax.nn.sigmoid(g[...]) for g in (ga, gb, gx)]
        ys = [ya_ref[...], yb_ref[...], yx_ref[...]]
        m16 = (sig[0] * ys[0] + sig[1] * ys[1] + sig[2] * ys[2]).astype(BF16)
        h = x_ref[...] + _dot(m16, wo)
        r = lax.rsqrt(_mean(h * h) + EPS)
        hn = h * r
        fg = fg_ref[...]
        err = hn * fg - t_ref[...]
        dy = err * inv_d
        dhn = dy * fg
        dh = r * (dhn - hn * _mean(dhn * hn))
        dh_ref[...] = dh
        dh16 = dh.astype(BF16)
        dm = _dot_nt(dh16, wo)
        for n, out in enumerate((dya_ref, dyb_ref, dyx_ref)):
            out[...] = (sig[n] * dm).astype(BF16)
            dp_ref[:, n * D:(n + 1) * D] = (dm * ys[n] * (sig[n] * (1.0 - sig[n]))).astype(BF16)

        @pl.when(i == 0)
        def _():
            dw_ref[...] = jnp.zeros_like(dw_ref)
            dfg_ref[...] = jnp.zeros_like(dfg_ref)
            sq_ref[...] = jnp.zeros_like(sq_ref)

        dw_ref[0] += _dot_tn(m16, dh16)
        dfg_ref[...] += _fold8(dy * hn)
        sq_ref[...] += _fold8(err * err)

    vec = jax.ShapeDtypeStruct((SUBLANES, D), F32)
    return pl.pallas_call(
        body, name="merge_fwd_bwd", grid=(n_rows // tm,),
        in_specs=[_rows(tm, G_GA), _rows(tm, G_GBB), _rows(tm, G_GX), _rows(tm), _rows(tm), _rows(tm), _rows(tm),
                  _rows(tm), _w_out_spec(3), _const((1, D))],
        out_specs=[_rows(tm), _rows(tm), _rows(tm), _rows(tm), pl.BlockSpec((tm, 3 * D), lambda i: (i, 3)),
                   pl.BlockSpec((1, D, D), lambda i: (3, 0, 0)), _const((SUBLANES, D)), _const((SUBLANES, D))],
        out_shape=[jax.ShapeDtypeStruct((n_rows, D), F32), jax.ShapeDtypeStruct((n_rows, D), BF16),
                   jax.ShapeDtypeStruct((n_rows, D), BF16), jax.ShapeDtypeStruct((n_rows, D), BF16),
                   jax.ShapeDtypeStruct((n_rows, N_GROUPS * D), BF16), jax.ShapeDtypeStruct((4, D, D), F32), vec, vec],
        compiler_params=_params(1),
    )(proj, proj, proj, ya, yb, yx, x, target, wo4_g, final_g)


def _branch_a_bwd(dya, proj, sa16, wo4_g, cw_a, dproj, dw4, tm):
    n_rows = proj.shape[0]
    n_tiles = n_rows // tm

    def body(dya_ref, bp, cp, xp, za, sa_ref, dyan, bpn, zan, cph, xph, w_ref, cw_ref, dp_in, dw_in,
             dp_ref, dw_ref, dwa_ref, e1, e2, o_scr):
        del dp_in, dw_in
        i = pl.program_id(0)
        woa = w_ref[...].reshape(D, D)
        dya16 = dya_ref[...]
        _to_time_major(e1, 0, jnp.where(i > 0, cph[...] * xph[...], 0.0))
        _to_time_major(e1, HALO, cp[...] * xp[...])
        _conv(o_scr, e1, cw_ref, FWD_TAPS_A, tm, UNROLL_A)
        ca = _from_time_major(o_scr, tm)
        dsa = _dot_nt(dya16, woa)
        silu_z, dsilu_z = _silu_and_grad(za[...])
        t = dsa * silu_z
        dp_ref[:, 0 * D:1 * D] = (t * ca).astype(BF16)
        dp_ref[:, 3 * D:4 * D] = (dsa * (bp[...] * ca) * dsilu_z).astype(BF16)
        dsan = _dot_nt(dyan[...], woa)
        dcan = (dsan * jax.nn.silu(zan[...])) * bpn[...]
        _to_time_major(e2, 0, t * bp[...])
        _to_time_major(e2, tm, jnp.where(i < n_tiles - 1, dcan, 0.0))

        @pl.when(i == 0)
        def _():
            dw_ref[...] = jnp.zeros_like(dw_ref)
            dwa_ref[...] = jnp.zeros_like(dwa_ref)

        _conv_wgrad(dwa_ref, e2, e1, FWD_TAPS_A, tm, UNROLL_A)
        dw_ref[0] += _dot_tn(sa_ref[...], dya16)
        _conv(o_scr, e2, cw_ref, BWD_TAPS_A, tm, UNROLL_A)
        dprod = _from_time_major(o_scr, tm)
        dp_ref[:, 1 * D:2 * D] = (dprod * xp[...]).astype(BF16)
        dp_ref[:, 2 * D:3 * D] = (dprod * cp[...]).astype(BF16)

    return pl.pallas_call(
        body, name="branch_a_bwd", grid=(n_tiles,),
        in_specs=[_rows(tm), _rows(tm, G_BA), _rows(tm, G_CA), _rows(tm, G_XA), _rows(tm, G_ZA), _rows(tm),
                  _next_halo(tm, n_rows), _next_halo(tm, n_rows, G_BA), _next_halo(tm, n_rows, G_ZA),
                  _prev_halo(tm, G_CA), _prev_halo(tm, G_XA), _w_out_spec(0), _const(cw_a.shape), ANY, ANY],
        out_specs=[pl.BlockSpec((tm, 4 * D), lambda i: (i, 0)), pl.BlockSpec((1, D, D), lambda i: (0, 0, 0)),
                   _const((K_A * LANE_GROUPS, LANES))],
        out_shape=[jax.ShapeDtypeStruct(dproj.shape, BF16), jax.ShapeDtypeStruct(dw4.shape, F32),
                   jax.ShapeDtypeStruct((K_A * LANE_GROUPS, LANES), F32)],
        input_output_aliases={13: 0, 14: 1},
        scratch_shapes=[_time_major(tm + HALO), _time_major(tm + HALO), _time_major(tm)],
        compiler_params=_params(1),
    )(dya, proj, proj, proj, proj, sa16, dya, proj, proj, proj, proj, wo4_g, cw_a, dproj, dw4)


def _branch_b_bwd(dyb, proj, cb, sb16, wo4_g, cw_b, ln_g, ln_b, dproj, dw4, tm):
    n_rows = proj.shape[0]
    n_tiles = n_rows // tm

    def body(dyb_ref, zb, cb_ref, vb, gb, sb_ref, dybn, zbn, cbn, vbh, gbh, w_ref, cw_ref, lg_ref, lb_ref,
             dp_in, dw_in, dp_ref, dw_ref, dwb_ref, dbb_ref, dlg_ref, dlb_ref, e1, e2, o_scr):
        del dp_in, dw_in
        i = pl.program_id(0)
        wob = w_ref[...].reshape(D, D)
        lg, lb = lg_ref[...], lb_ref[...]

        def conv_out_grad(dy16, z, c):
            dsb = _dot_nt(dy16, wob)
            xhat, rstd, ln = _layernorm_parts(c, lg, lb)
            sw, dsw = _silu_and_grad(ln)
            sz, dsz = _silu_and_grad(z)
            dln = (dsb * sz) * dsw
            dxhat = dln * lg
            dcb = rstd * (dxhat - _mean(dxhat) - xhat * _mean(dxhat * xhat))
            return dsb * sw * dsz, dln, xhat, dcb

        dyb16 = dyb_ref[...]
        dzb, dln, xhat, dcb = conv_out_grad(dyb16, zb[...], cb_ref[...])
        dp_ref[:, 2 * D:3 * D] = dzb.astype(BF16)
        _, _, _, dcbn = conv_out_grad(dybn[...], zbn[...], cbn[...])
        _to_time_major(e2, 0, dcb)
        _to_time_major(e2, tm, jnp.where(i < n_tiles - 1, dcbn, 0.0))

        @pl.when(i == 0)
        def _():
            dw_ref[...] = jnp.zeros_like(dw_ref)
            dwb_ref[...] = jnp.zeros_like(dwb_ref)
            dbb_ref[...] = jnp.zeros_like(dbb_ref)
            dlg_ref[...] = jnp.zeros_like(dlg_ref)
            dlb_ref[...] = jnp.zeros_like(dlb_ref)

        dlg_ref[...] += _fold8(dln * xhat)
        dlb_ref[...] += _fold8(dln)
        dbb_ref[...] += _fold8(dcb)
        dw_ref[0] += _dot_tn(sb_ref[...], dyb16)
        sg = jax.nn.sigmoid(gb[...])
        _to_time_major(e1, 0, jnp.where(i > 0, vbh[...] * jax.nn.sigmoid(gbh[...]), 0.0))
        _to_time_major(e1, HALO, vb[...] * sg)
        _conv_wgrad(dwb_ref, e2, e1, FWD_TAPS_B, tm, UNROLL_WGRAD_B)
        _conv(o_scr, e2, cw_ref, BWD_TAPS_B, tm, UNROLL_B)
        dglu = _from_time_major(o_scr, tm)
        dp_ref[:, 0 * D:1 * D] = (dglu * sg).astype(BF16)
        dp_ref[:, 1 * D:2 * D] = (dglu * vb[...] * (sg * (1.0 - sg))).astype(BF16)

    vec = jax.ShapeDtypeStruct((SUBLANES, D), F32)
    return pl.pallas_call(
        body, name="branch_b_bwd", grid=(n_tiles,),
        in_specs=[_rows(tm), _rows(tm, G_ZB), _rows(tm), _rows(tm, G_VB), _rows(tm, G_GB), _rows(tm),
                  _next_halo(tm, n_rows), _next_halo(tm, n_rows, G_ZB), _next_halo(tm, n_rows),
                  _prev_halo(tm, G_VB), _prev_halo(tm, G_GB), _w_out_spec(1), _const(cw_b.shape), _const((1, D)),
                  _const((1, D)), ANY, ANY],
        out_specs=[pl.BlockSpec((tm, 3 * D), lambda i: (i, 2)), pl.BlockSpec((1, D, D), lambda i: (1, 0, 0)),
                   _const((K_B * LANE_GROUPS, LANES)), _const((SUBLANES, D)), _const((SUBLANES, D)),
                   _const((SUBLANES, D))],
        out_shape=[jax.ShapeDtypeStruct(dproj.shape, BF16), jax.ShapeDtypeStruct(dw4.shape, F32),
                   jax.ShapeDtypeStruct((K_B * LANE_GROUPS, LANES), F32), vec, vec, vec],
        input_output_aliases={15: 0, 16: 1},
        scratch_shapes=[_time_major(tm + HALO), _time_major(tm + HALO), _time_major(tm)],
        compiler_params=_params(1),
    )(dyb, proj, cb, proj, proj, sb16, dyb, proj, cb, proj, proj, wo4_g, cw_b, ln_g, ln_b, dproj, dw4)


def _branch_x_bwd(dyx, proj, sx16, kv16, wo4_g, dproj, dw4, tm):
    n_rows = proj.shape[0]
    scale = HEAD_DIM ** -0.5

    def body(dyx_ref, q, zx, sx_ref, kv_ref, w_ref, dp_in, dw_in, dp_ref, dw_ref, dkv_ref):
        del dp_in, dw_in
        i = pl.program_id(0)
        dyx16 = dyx_ref[...]
        q16 = q[...].astype(BF16)
        probs, outs = _attention(q16, kv_ref)
        dsx = _dot_nt(dyx16, w_ref[...].reshape(D, D))
        silu_z, dsilu_z = _silu_and_grad(zx[...])
        dp_ref[:, D:2 * D] = (dsx * jnp.concatenate(outs, axis=-1) * dsilu_z).astype(BF16)
        do16 = (dsx * silu_z).astype(BF16)

        @pl.when(i == 0)
        def _():
            dw_ref[...] = jnp.zeros_like(dw_ref)
            dkv_ref[...] = jnp.zeros_like(dkv_ref)

        for h in range(N_HEADS):
            cols = slice(h * HEAD_DIM, (h + 1) * HEAD_DIM)
            p = probs[h]
            dprob = _dot_nt(do16[:, cols], kv_ref[N_HEADS + h])
            ds16 = ((p * (dprob - jnp.sum(p * dprob, axis=-1, keepdims=True))) * scale).astype(BF16)
            dp_ref[:, cols] = _dot(ds16, kv_ref[h]).astype(BF16)
            dkv_ref[h] += _dot_tn(ds16, q16[:, cols])
            dkv_ref[N_HEADS + h] += _dot_tn(p.astype(BF16), do16[:, cols])
        dw_ref[0] += _dot_tn(sx_ref[...], dyx16)

    return pl.pallas_call(
        body, name="branch_x_bwd", grid=(n_rows // tm,),
        in_specs=[_rows(tm), _rows(tm, G_Q), _rows(tm, G_ZX), _rows(tm), _const(kv16.shape), _w_out_spec(2), ANY, ANY],
        out_specs=[pl.BlockSpec((tm, 2 * D), lambda i: (i, 2)), pl.BlockSpec((1, D, D), lambda i: (2, 0, 0)),
                   _const(kv16.shape)],
        out_shape=[jax.ShapeDtypeStruct(dproj.shape, BF16), jax.ShapeDtypeStruct(dw4.shape, F32),
                   jax.ShapeDtypeStruct(kv16.shape, F32)],
        input_output_aliases={6: 0, 7: 1},
        compiler_params=_params(1),
    )(dyx, proj, proj, sx16, kv16, wo4_g, dproj, dw4)


def _dp_unit(u):
    g = u // 2
    pos = jnp.where(g < G_VB, g, jnp.where(g < G_Q, g + 2, jnp.where(g < G_GA, g - 3, g)))
    return 2 * pos + u % 2


def _scatter_copies(srcs, lands, send, recv):
    x, y, c = lax.axis_index("x"), lax.axis_index("y"), lax.axis_index("c")
    copies = []
    for n in range(N_DEV - 1):
        flip = n + 1
        px = 1 - x if flip & 4 else x
        py = 1 - y if flip & 2 else y
        pc = 1 - c if flip & 1 else c
        for t, (src, land) in enumerate(zip(srcs, lands)):
            copies.append(pltpu.make_async_remote_copy(
                src_ref=src.at[4 * px + 2 * py + pc], dst_ref=land.at[n], send_sem=send.at[t, n],
                recv_sem=recv.at[t, n], device_id=(px, py, pc), device_id_type=MESH))
    return copies


def _scatter_sems(n_tensors):
    return [pltpu.SemaphoreType.DMA((n_tensors, N_DEV - 1)), pltpu.SemaphoreType.DMA((n_tensors, N_DEV - 1))]


def _landing(shard_shape, lead=()):
    return jax.ShapeDtypeStruct((*lead, N_DEV - 1, *shard_shape), BF16)


W_IN_GRAD_FLIPS = (4, 2, 6, 5, 3, 7, 1, 0)


def _flipped(flip):
    x, y, c = lax.axis_index("x"), lax.axis_index("y"), lax.axis_index("c")
    return (1 - x if flip & 4 else x, 1 - y if flip & 2 else y, 1 - c if flip & 1 else c)


def _w_in_grad(ut, dproj, tk, dw4_16, dwkv16):
    n_rows = dproj.shape[0]
    n_k = n_rows // tk
    per_shard = W_IN_SHARD // UNIT
    order = jnp.stack([4 * px + 2 * py + pc for px, py, pc in map(_flipped, W_IN_GRAD_FLIPS)]).astype(jnp.int32)

    def body(order_ref, ut_ref, dp0, dp1, dp2, dw4_ref, dwkv_ref, out_ref, lin_ref, l4_ref, lkv_ref,
             acc, stage, send, recv, send_in, recv_in):
        del order_ref
        q, t = pl.program_id(0), pl.program_id(1)

        def small_copies():
            return _scatter_copies([dw4_ref.at[w] for w in range(4)] + [dwkv_ref],
                                   [l4_ref.at[w] for w in range(4)] + [lkv_ref], send, recv)

        def shard_copy(pos):
            n = W_IN_GRAD_FLIPS[pos] - 1
            return pltpu.make_async_remote_copy(
                src_ref=stage.at[pos % 2], dst_ref=lin_ref.at[n], send_sem=send_in.at[n], recv_sem=recv_in.at[n],
                device_id=_flipped(W_IN_GRAD_FLIPS[pos]), device_id_type=MESH)

        @pl.when((q == 0) & (t == 0))
        def _():
            for cp in small_copies():
                cp.start()

        for r, dp_ref in enumerate((dp0, dp1, dp2)):
            cols = slice(r * UNIT, (r + 1) * UNIT)

            @pl.when(t == 0)
            def _(dp_ref=dp_ref, cols=cols):
                acc[:, cols] = _dot(ut_ref[...], dp_ref[...])

            @pl.when(t > 0)
            def _(dp_ref=dp_ref, cols=cols):
                acc[:, cols] += _dot(ut_ref[...], dp_ref[...])

        @pl.when(t == n_k - 1)
        def _():
            out_ref[0] = acc[...]

        for pos in range(N_DEV - 1):
            @pl.when((q == pos) & (t == n_k - 1))
            def _(pos=pos):
                if pos >= 2:
                    shard_copy(pos - 2).wait_send()
                stage[pos % 2] = acc[...].astype(BF16)
                shard_copy(pos).start()

        @pl.when((q == N_DEV - 1) & (t == n_k - 1))
        def _():
            for pos in (N_DEV - 3, N_DEV - 2):
                shard_copy(pos).wait_send()
            for pos in range(N_DEV - 1):
                shard_copy(pos).wait_recv()
            for cp in small_copies():
                cp.wait()

    def dp_spec(r):
        return pl.BlockSpec((tk, UNIT), lambda q, t, order: (t, _dp_unit(per_shard * order[q] + r)))

    return pl.pallas_call(
        body, name="w_in_grad",
        grid_spec=pltpu.PrefetchScalarGridSpec(
            num_scalar_prefetch=1, grid=(N_DEV, n_k),
            in_specs=[pl.BlockSpec((D, tk), lambda q, t, order: (0, t)), dp_spec(0), dp_spec(1), dp_spec(2), ANY, ANY],
            out_specs=[pl.BlockSpec((1, D, W_IN_SHARD), lambda q, t, order: (order[q], 0, 0)), ANY, ANY, ANY],
            scratch_shapes=[pltpu.VMEM((D, W_IN_SHARD), F32), pltpu.VMEM((2, D, W_IN_SHARD), BF16)]
            + _scatter_sems(5) + [pltpu.SemaphoreType.DMA((N_DEV - 1,)), pltpu.SemaphoreType.DMA((N_DEV - 1,))]),
        out_shape=[jax.ShapeDtypeStruct((N_DEV, D, W_IN_SHARD), F32), _landing((D, W_IN_SHARD)),
                   _landing(dw4_16.shape[2:], lead=(4,)), _landing(dwkv16.shape[1:])],
        compiler_params=_params(2),
    )(order, ut, dproj, dproj, dproj, dw4_16, dwkv16)


def _x_grad(dproj, win_t, x, dh, norm_g, tm):
    n_rows = x.shape[0]
    n_k = N_GROUPS * D // X_GRAD_K

    def body(dp_ref, wt_ref, x_ref, dh_ref, g_ref, gx_ref, dg_ref, acc):
        i, g = pl.program_id(0), pl.program_id(1)

        @pl.when((i == 0) & (g == 0))
        def _():
            dg_ref[...] = jnp.zeros_like(dg_ref)

        @pl.when(g == 0)
        def _():
            acc[...] = _dot(dp_ref[...], wt_ref[...])

        @pl.when(g > 0)
        def _():
            acc[...] += _dot(dp_ref[...], wt_ref[...])

        @pl.when(g == n_k - 1)
        def _():
            du = acc[...]
            xf = x_ref[...]
            r = lax.rsqrt(_mean(xf * xf) + EPS)
            xn = xf * r
            dun = du * g_ref[...]
            gx_ref[...] = dh_ref[...] + r * (dun - xn * _mean(dun * xn))
            dg_ref[...] += _fold8(du * xn)

    return pl.pallas_call(
        body, name="x_grad", grid=(n_rows // tm, n_k),
        in_specs=[pl.BlockSpec((tm, X_GRAD_K), lambda i, g: (i, g)), pl.BlockSpec((X_GRAD_K, D), lambda i, g: (g, 0)),
                  pl.BlockSpec((tm, D), lambda i, g: (i, 0)), pl.BlockSpec((tm, D), lambda i, g: (i, 0)),
                  _const((1, D))],
        out_specs=[pl.BlockSpec((tm, D), lambda i, g: (i, 0)), _const((SUBLANES, D))],
        out_shape=[jax.ShapeDtypeStruct((n_rows, D), F32), jax.ShapeDtypeStruct((SUBLANES, D), F32)],
        scratch_shapes=[pltpu.VMEM((tm, D), F32)],
        compiler_params=_params(2),
    )(dproj, win_t, x, dh, norm_g)


def _local_step(x, mem, target, norm_g, conv_b_b, ln_g, ln_b, mem_g, final_g, shards):
    n_rows = x.shape[0]
    tm = min(512, n_rows)
    big = min(1024, n_rows)
    u16, ut = _rmsnorm_fwd(x, norm_g, big)
    proj, win_t, _, wkv_g, wo4_g, cw_g = _proj_fwd_gather(u16, shards, big)
    cw_rows = cw_g.transpose(1, 0, 2).reshape((SUBLANES + HALO) * LANE_GROUPS, LANES)
    cw_a, cw_b = cw_rows[:SUBLANES * LANE_GROUPS], cw_rows[SUBLANES * LANE_GROUPS:]
    kv16, mn16 = _kv_fwd(mem, mem_g, wkv_g)
    sa16, ya = _branch_a_fwd(proj, wo4_g, cw_a, tm)
    cb, sb16, yb = _branch_b_fwd(proj, wo4_g, cw_b, conv_b_b, ln_g, ln_b, tm)
    sx16, yx = _branch_x_fwd(proj, kv16, wo4_g, tm)
    dh, dya, dyb, dyx, dproj, dw4, dfg, sq = _merge_fwd_bwd(proj, ya, yb, yx, x, target, wo4_g, final_g,
                                                             min(256, n_rows))
    dproj, dw4, dwa = _branch_a_bwd(dya, proj, sa16, wo4_g, cw_a, dproj, dw4, tm)
    dproj, dw4, dwb, dbb, dlg, dlb = _branch_b_bwd(dyb, proj, cb, sb16, wo4_g, cw_b, ln_g, ln_b, dproj, dw4, tm)
    dproj, dw4, dkv = _branch_x_bwd(dyx, proj, sx16, kv16, wo4_g, dproj, dw4, tm)
    dwkv_g, dmg = _kv_bwd(dkv, mem, mem_g, mn16, wkv_g)
    dw4 = dw4.reshape(4, N_DEV, D // N_DEV, D)
    dwin_g, landin, land4, landkv = _w_in_grad(ut, dproj, min(2048, n_rows), dw4.astype(BF16), dwkv_g.astype(BF16))
    gx, dng = _x_grad(dproj, win_t, x, dh, norm_g, big)
    small = {SV_NORM_G: dng, SV_CONV_B_B: dbb, SV_LN_G: dlg, SV_LN_B: dlb, SV_MEM_G: dmg, SV_FINAL_G: dfg, SV_LOSS: sq}
    grads = [(dwin_g[None], landin[None]), (dw4, land4), (dwkv_g[None], landkv[None])]
    return gx, grads, small, dwa.reshape(K_A, D), dwb.reshape(K_B, D)


def _allgather_small(small, conv_rows):
    keys = sorted(small)

    def body(*refs):
        parts, (conv_ref, out_ref, mine, send, recv) = refs[:len(keys)], refs[len(keys):]
        x, y, c, chips = _place()
        me, sibling = 4 * x + 2 * y + c, (x, y, 1 - c)
        mine[pl.ds(0, SV_CONV_A), :] = jnp.zeros((SV_CONV_A, D), F32)
        for key, part in zip(keys, parts):
            mine[key:key + 1, :] = jnp.sum(part[...], axis=0, keepdims=True)
        mine[pl.ds(SV_CONV_A, SV_ROWS - SV_CONV_A), :] = conv_ref[...]
        out_ref[me] = mine[...]

        def copy(k, block, to, from_mine=False):
            return pltpu.make_async_remote_copy(
                src_ref=mine if from_mine else out_ref.at[block], dst_ref=out_ref.at[block],
                send_sem=send.at[k], recv_sem=recv.at[k], device_id=to, device_id_type=MESH)

        first = [copy(0, me, sibling, from_mine=True)]
        first += [copy(1 + j, me, (*chip, c), from_mine=True) for j, chip in enumerate(chips)]
        for cp in first:
            cp.start()
        passed = []
        for j, (px, py) in enumerate(chips):
            block = 4 * px + 2 * py + c
            copy(1 + j, block, sibling).wait_recv()
            passed.append(copy(4 + j, block, sibling))
            passed[-1].start()
        copy(0, 4 * x + 2 * y + 1 - c, sibling).wait_recv()
        for j, (px, py) in enumerate(chips):
            copy(4 + j, 4 * px + 2 * py + 1 - c, sibling).wait_recv()
        for cp in first + passed:
            cp.wait_send()

    vmem = pl.BlockSpec(memory_space=pltpu.VMEM)
    return pl.pallas_call(
        body, name="allgather_small",
        in_specs=[vmem] * (len(keys) + 1), out_specs=vmem,
        out_shape=jax.ShapeDtypeStruct((N_DEV, SV_ROWS, D), F32),
        scratch_shapes=[pltpu.VMEM((SV_ROWS, D), F32), pltpu.SemaphoreType.DMA((7,)), pltpu.SemaphoreType.DMA((7,))],
    )(*[small[k] for k in keys], conv_rows)


def _adamw(w, g, m, v):
    m = ADAM_B1 * m + (1.0 - ADAM_B1) * g
    v = ADAM_B2 * v + (1.0 - ADAM_B2) * (g * g)
    m_hat = m / (1.0 - ADAM_B1 ** ADAM_STEP)
    v_hat = v / (1.0 - ADAM_B2 ** ADAM_STEP)
    return -ADAM_LR * (m_hat / (jnp.sqrt(v_hat) + ADAM_EPS) + ADAM_WD * w), m, v


def _adamw_shard(own, landed, piece, k_arr, w, m, v, tr):
    n_r, n_c = w.shape
    n_landed = landed.shape[1]

    def body(k_ref, own_ref, *refs):
        del k_ref
        landed_refs, (w_ref, m_ref, v_ref, g_out, d_out, m_out, v_out) = refs[:n_landed], refs[n_landed:]
        g = own_ref[0, 0]
        for landed_ref in landed_refs:
            g = g + landed_ref[0, 0].astype(F32)
        g_out[...] = g
        d_out[...], m_out[...], v_out[...] = _adamw(w_ref[...], g, m_ref[...], v_ref[...])

    blk = (1, 1, tr, n_c)
    flat = pl.BlockSpec((tr, n_c), lambda r, k: (r, 0))
    return pl.pallas_call(
        body, name="adamw_shard",
        grid_spec=pltpu.PrefetchScalarGridSpec(
            num_scalar_prefetch=1, grid=(n_r // tr,),
            in_specs=[pl.BlockSpec(blk, lambda r, k: (piece, k[0], r, 0))]
            + [pl.BlockSpec(blk, functools.partial(lambda r, k, j: (piece, j, r, 0), j=j)) for j in range(n_landed)]
            + [flat] * 3,
            out_specs=[flat] * 4),
        out_shape=[jax.ShapeDtypeStruct((n_r, n_c), F32)] * 4,
        compiler_params=_params(1),
    )(k_arr, own, *([landed] * n_landed), w, m, v)


def _adamw_small(gathered, k_arr, vectors, conv_a, conv_b):
    n_vec = len(vectors)
    cols = D // N_DEV

    def body(k_ref, full_ref, cols_ref, *refs):
        del k_ref
        ins, outs = refs[:3 * (n_vec + 2)], refs[3 * (n_vec + 2):-2]
        tot, tot_cols = refs[-2:]
        tot[...] = full_ref[0]
        tot_cols[...] = cols_ref[0]
        for dev in range(1, N_DEV):
            tot[...] += full_ref[dev]
            tot_cols[...] += cols_ref[dev]
        loss = (0.5 / D) * jnp.sum(tot[SV_LOSS:SV_LOSS + 1, :])
        outs[0][...] = jnp.full(outs[0].shape, loss, F32)
        grads = [tot[n:n + 1, :] for n in range(n_vec)]
        grads += [tot_cols[pl.ds(SV_CONV_A, K_A), :], tot_cols[pl.ds(SV_CONV_B, K_B), :]]
        for n, g in enumerate(grads):
            w_ref, m_ref, v_ref = ins[3 * n:3 * n + 3]
            g_out, d_out, m_out, v_out = outs[1 + 4 * n:5 + 4 * n]
            g_out[...] = g
            d_out[...], m_out[...], v_out[...] = _adamw(w_ref[...], g, m_ref[...], v_ref[...])

    weights = list(vectors) + [conv_a, conv_b]
    flat_in = [a for wmv in weights for a in wmv]
    out_shape = [jax.ShapeDtypeStruct((SUBLANES, 128), F32)]
    for wmv in weights:
        out_shape += [jax.ShapeDtypeStruct(wmv[0].shape, F32)] * 4
    return pl.pallas_call(
        body, name="adamw_small",
        grid_spec=pltpu.PrefetchScalarGridSpec(
            num_scalar_prefetch=1, grid=(1,),
            in_specs=[pl.BlockSpec((N_DEV, SV_ROWS, D), lambda i, k: (0, 0, 0)),
                      pl.BlockSpec((N_DEV, SV_ROWS, cols), lambda i, k: (0, 0, k[0]))]
            + [pl.BlockSpec(a.shape, lambda i, k: (0, 0)) for a in flat_in],
            out_specs=[pl.BlockSpec(s.shape, lambda i, k: (0, 0)) for s in out_shape],
            scratch_shapes=[pltpu.VMEM((SV_ROWS, D), F32), pltpu.VMEM((SV_ROWS, cols), F32)]),
        out_shape=out_shape,
        compiler_params=_params(1),
    )(k_arr, gathered, gathered, *flat_in)


def kernel(x, mem, norm_g, w_in, conv_a_w, w_out_a, conv_b_w, conv_b_b, ln_b_g, ln_b_b, w_out_b, mem_norm_g, w_kv, w_out_x, w_o, final_g, loss_target, m_norm_g, m_w_in, m_conv_a_w, m_w_out_a, m_conv_b_w, m_conv_b_b, m_ln_b_g, m_ln_b_b, m_w_out_b, m_mem_norm_g, m_w_kv, m_w_out_x, m_w_o, m_final_g, v_norm_g, v_w_in, v_conv_a_w, v_w_out_a, v_conv_b_w, v_conv_b_b, v_ln_b_g, v_ln_b_b, v_w_out_b, v_mem_norm_g, v_w_kv, v_w_out_x, v_w_o, v_final_g):
    xi, yi, ci = lax.axis_index("x"), lax.axis_index("y"), lax.axis_index("c")
    k_arr = jnp.reshape(4 * xi + 2 * yi + ci, (1,)).astype(jnp.int32)

    cw = jnp.concatenate([jnp.pad(conv_a_w[0], ((0, SUBLANES - K_A), (0, 0))),
                          jnp.pad(conv_b_w[0], ((0, HALO - K_B), (0, 0)))], axis=0)
    wo4 = jnp.stack([w_out_a[0], w_out_b[0], w_out_x[0], w_o[0]]).astype(BF16)
    shards = [w_in[0].astype(BF16), w_kv[0].astype(BF16), wo4, cw]

    final_g2 = final_g.reshape(1, D)
    gx, grads, small, dwa, dwb = _local_step(
        x[0], mem[0], loss_target[0], norm_g, conv_b_b, ln_b_g, ln_b_b, mem_norm_g, final_g2, shards)

    conv_rows = jnp.concatenate([jnp.pad(dwa, ((0, SUBLANES - K_A), (0, 0))),
                                 jnp.pad(dwb, ((0, HALO - K_B), (0, 0)))], axis=0)
    gathered_small = _allgather_small(small, conv_rows)

    tiles = [256, D // N_DEV, 256]

    def shard(a, l, w, m, v):
        return _adamw_shard(grads[a][0], grads[a][1], l, k_arr, w[0], m[0], v[0], tiles[a])

    res = {
        "w_in": shard(0, 0, w_in, m_w_in, v_w_in),
        "w_out_a": shard(1, 0, w_out_a, m_w_out_a, v_w_out_a),
        "w_out_b": shard(1, 1, w_out_b, m_w_out_b, v_w_out_b),
        "w_out_x": shard(1, 2, w_out_x, m_w_out_x, v_w_out_x),
        "w_o": shard(1, 3, w_o, m_w_o, v_w_o),
        "w_kv": shard(2, 0, w_kv, m_w_kv, v_w_kv),
    }
    res = {name: tuple(r[None] for r in four) for name, four in res.items()}
    vectors = [(norm_g, m_norm_g, v_norm_g), (conv_b_b, m_conv_b_b, v_conv_b_b), (ln_b_g, m_ln_b_g, v_ln_b_g),
               (ln_b_b, m_ln_b_b, v_ln_b_b), (mem_norm_g, m_mem_norm_g, v_mem_norm_g),
               (final_g2, m_final_g.reshape(1, D), v_final_g.reshape(1, D))]
    out = _adamw_small(gathered_small, k_arr, vectors, (conv_a_w[0], m_conv_a_w[0], v_conv_a_w[0]),
                       (conv_b_w[0], m_conv_b_w[0], v_conv_b_w[0]))
    loss = out[0][0, 0]
    names = ["norm_g", "conv_b_b", "ln_b_g", "ln_b_b", "mem_norm_g", "final_g", "conv_a_w", "conv_b_w"]
    for n, name in enumerate(names):
        four = out[1 + 4 * n:5 + 4 * n]
        if name == "final_g":
            four = [r.reshape(D) for r in four]
        elif name.startswith("conv_") and name.endswith("_w"):
            four = [r[None] for r in four]
        res[name] = tuple(four)

    order = ["norm_g", "w_in", "conv_a_w", "w_out_a", "conv_b_w", "conv_b_b", "ln_b_g", "ln_b_b", "w_out_b",
             "mem_norm_g", "w_kv", "w_out_x", "w_o", "final_g"]
    return (loss, gx[None], *[res[n][0] for n in order], *[res[n][1] for n in order],
            *[res[n][2] for n in order], *[res[n][3] for n in order])
```

```python
import functools

import jax
import jax.numpy as jnp
from jax import lax
from jax.experimental import pallas as pl
from jax.experimental.pallas import tpu as pltpu

F32, BF16 = jnp.float32, jnp.bfloat16
D = 1024
N_DEV = 8
N_HEADS = 4
HEAD_DIM = D // N_HEADS
N_GROUPS = 12
W_IN_SHARD = N_GROUPS * D // N_DEV
UNIT = 512
X_GRAD_K = 2 * D
K_A, K_B = 3, 31
EPS = 1e-6
HALO = 32
SUBLANES = 8
LANES = 128
LANE_GROUPS = D // LANES
UNROLL_A, UNROLL_B = 8, 4
UNROLL_WGRAD_B = 4
WGRAD_TAPS = 16
CONV_PARTIAL_SUMS = 4
VMEM_LIMIT = 56 << 20
MESH = pl.DeviceIdType.MESH
ANY = pl.BlockSpec(memory_space=pl.ANY)

G_BA, G_CA, G_XA, G_ZA, G_VB, G_GB, G_ZB, G_Q, G_ZX, G_GA, G_GBB, G_GX = range(N_GROUPS)
DP_POS = (0, 1, 2, 3, 6, 7, 8, 4, 5, 9, 10, 11)

ADAM_LR, ADAM_B1, ADAM_B2, ADAM_EPS, ADAM_WD, ADAM_STEP = 0.001, 0.9, 0.999, 1e-08, 0.01, 10

SV_NORM_G, SV_CONV_B_B, SV_LN_G, SV_LN_B, SV_MEM_G, SV_FINAL_G, SV_LOSS = range(7)
SV_CONV_A, SV_CONV_B, SV_ROWS = 8, 16, 48


def _dot(a, b):
    return jnp.dot(a, b, preferred_element_type=F32)


def _dot_nt(a, b):
    return lax.dot_general(a, b, (((1,), (1,)), ((), ())), preferred_element_type=F32)


def _dot_tn(a, b):
    return lax.dot_general(a, b, (((0,), (0,)), ((), ())), preferred_element_type=F32)


def _silu_and_grad(z):
    s = jax.nn.sigmoid(z)
    return z * s, s * (1.0 + z * (1.0 - s))


def _fold8(a):
    return a.reshape(a.shape[0] // SUBLANES, SUBLANES, a.shape[1]).sum(axis=0)


def _mean(a):
    return jnp.mean(a, axis=-1, keepdims=True)


def _params(n_grid):
    return pltpu.CompilerParams(dimension_semantics=("arbitrary",) * n_grid, vmem_limit_bytes=VMEM_LIMIT)


def _rows(tm, col=0):
    return pl.BlockSpec((tm, D), lambda i: (i, col))


def _prev_halo(tm, col=0):
    return pl.BlockSpec((HALO, D), lambda i: (jnp.maximum(i * (tm // HALO) - 1, 0), col))


def _next_halo(tm, n_rows, col=0):
    last = n_rows // HALO - 1
    return pl.BlockSpec((HALO, D), lambda i: (jnp.minimum((i + 1) * (tm // HALO), last), col))


def _const(shape):
    return pl.BlockSpec(shape, lambda *_: (0,) * len(shape))


def _w_out_spec(which):
    return pl.BlockSpec((N_DEV, None, D // N_DEV, D), lambda *_: (0, which, 0, 0))


def _to_time_major(t_ref, row0, x):
    n = x.shape[0]
    for j in range(LANE_GROUPS):
        t_ref[pl.ds(row0 * LANE_GROUPS + j, n, stride=LANE_GROUPS), :] = x[:, j * LANES:(j + 1) * LANES]


def _from_time_major(t_ref, n):
    return jnp.concatenate([t_ref[pl.ds(j, n, stride=LANE_GROUPS), :] for j in range(LANE_GROUPS)], axis=-1)


def _row(ref, t):
    start = t * LANE_GROUPS
    if not isinstance(start, int):
        start = pl.multiple_of(start, LANE_GROUPS)
    return ref[pl.ds(start, LANE_GROUPS), :]


def _conv(o_ref, e_ref, w_ref, taps, n_rows, unroll, bias_ref=None):
    weights = [_row(w_ref, k) for k, _ in taps]

    def chunk(c, carry):
        t0 = c * unroll
        n_part = min(CONV_PARTIAL_SUMS, len(taps))
        acc = [[None] * n_part for _ in range(unroll)]
        for n, (_, off) in enumerate(taps):
            for u in range(unroll):
                term = weights[n] * _row(e_ref, t0 + off + u)
                acc[u][n % n_part] = term if acc[u][n % n_part] is None else acc[u][n % n_part] + term
        for u in range(unroll):
            parts = acc[u]
            while len(parts) > 1:
                parts = [parts[n] + parts[n + 1] for n in range(0, len(parts) - 1, 2)] + parts[len(parts) & ~1:]
            out = parts[0]
            if bias_ref is not None:
                out = out + bias_ref[...]
            o_ref[pl.ds(pl.multiple_of((t0 + u) * LANE_GROUPS, LANE_GROUPS), LANE_GROUPS), :] = out
        return carry

    lax.fori_loop(0, n_rows // unroll, chunk, 0)


def _conv_wgrad(dw_ref, d_ref, e_ref, taps, n_rows, unroll):
    for first in range(0, len(taps), WGRAD_TAPS):
        group = taps[first:first + WGRAD_TAPS]
        lo = min(off for _, off in group)
        hi = max(off for _, off in group)

        def chunk(c, accs, group=group, lo=lo, hi=hi):
            t0 = c * unroll
            d = [_row(d_ref, t0 + u) for u in range(unroll)]
            window = [_row(e_ref, t0 + lo + n) for n in range(hi - lo + unroll)]
            accs = list(accs)
            for n, (_, off) in enumerate(group):
                for u in range(unroll):
                    accs[n] = accs[n] + d[u] * window[off - lo + u]
            return tuple(accs)

        zeros = tuple(jnp.zeros((LANE_GROUPS, LANES), F32) for _ in group)
        accs = lax.fori_loop(0, n_rows // unroll, chunk, zeros)
        for (k, _), acc in zip(group, accs):
            dw_ref[pl.ds(k * LANE_GROUPS, LANE_GROUPS), :] += acc


FWD_TAPS_A = [(k, HALO - (K_A - 1) + k) for k in range(K_A)]
BWD_TAPS_A = [(k, K_A - 1 - k) for k in range(K_A)]
FWD_TAPS_B = [(k, HALO - (K_B - 1) + k) for k in range(K_B)]
BWD_TAPS_B = [(k, K_B - 1 - k) for k in range(K_B)]


def _time_major(n_rows):
    return pltpu.VMEM((n_rows * LANE_GROUPS, LANES), F32)


def _kv_fwd(mem, mem_g, wkv_g):
    m_len = mem.shape[0]

    def body(mem_ref, g_ref, w_ref, kv_ref, mn_ref):
        mf = mem_ref[...]
        r = lax.rsqrt(_mean(mf * mf) + EPS)
        mn = ((mf * r) * g_ref[...]).astype(BF16)
        mn_ref[...] = mn
        for b in range(2 * N_HEADS):
            kv_ref[b] = _dot(mn, w_ref[b]).astype(BF16)

    return pl.pallas_call(
        body, name="kv_fwd", grid=(1,),
        in_specs=[_const((m_len, D)), _const((1, D)), _const((2 * N_HEADS, D, HEAD_DIM))],
        out_specs=[_const((2 * N_HEADS, m_len, HEAD_DIM)), _const((m_len, D))],
        out_shape=[jax.ShapeDtypeStruct((2 * N_HEADS, m_len, HEAD_DIM), BF16), jax.ShapeDtypeStruct((m_len, D), BF16)],
        compiler_params=_params(1),
    )(mem, mem_g, wkv_g)


def _kv_bwd(dkv, mem, mem_g, mn16, wkv_g):
    m_len = mem.shape[0]

    def body(dkv_ref, mem_ref, g_ref, mn_ref, w_ref, dw_ref, dg_ref):
        mn = mn_ref[...]
        dmn = jnp.zeros((m_len, D), F32)
        for b in range(2 * N_HEADS):
            d16 = dkv_ref[b].astype(BF16)
            dw_ref[b] = _dot_tn(mn, d16)
            dmn = dmn + _dot_nt(d16, w_ref[b])
        mf = mem_ref[...]
        r = lax.rsqrt(_mean(mf * mf) + EPS)
        dg_ref[...] = _fold8(dmn * (mf * r))

    return pl.pallas_call(
        body, name="kv_bwd", grid=(1,),
        in_specs=[_const((2 * N_HEADS, m_len, HEAD_DIM)), _const((m_len, D)), _const((1, D)), _const((m_len, D)),
                  _const((2 * N_HEADS, D, HEAD_DIM))],
        out_specs=[_const((2 * N_HEADS, D, HEAD_DIM)), _const((SUBLANES, D))],
        out_shape=[jax.ShapeDtypeStruct((2 * N_HEADS, D, HEAD_DIM), F32), jax.ShapeDtypeStruct((SUBLANES, D), F32)],
        compiler_params=_params(1),
    )(dkv, mem, mem_g, mn16, wkv_g)


def _rmsnorm_fwd(x, norm_g, tm):
    n_rows = x.shape[0]

    def body(x_ref, g_ref, u_ref, ut_ref):
        xf = x_ref[...]
        u = (xf * lax.rsqrt(_mean(xf * xf) + EPS)) * g_ref[...]
        u_ref[...] = u.astype(BF16)
        ut_ref[...] = u.T.astype(BF16)

    return pl.pallas_call(
        body, name="rmsnorm_fwd", grid=(n_rows // tm,),
        in_specs=[_rows(tm), _const((1, D))],
        out_specs=[_rows(tm), pl.BlockSpec((D, tm), lambda i: (0, i))],
        out_shape=[jax.ShapeDtypeStruct((n_rows, D), BF16), jax.ShapeDtypeStruct((D, n_rows), BF16)],
        compiler_params=_params(1),
    )(x, norm_g)


def _place():
    x, y, c = lax.axis_index("x"), lax.axis_index("y"), lax.axis_index("c")
    other_chips = [(1 - x, y), (x, 1 - y), (1 - x, 1 - y)]
    return x, y, c, other_chips


def _arrival_order():
    x, y, c, chips = _place()
    order = [4 * x + 2 * y + c, 4 * x + 2 * y + 1 - c]
    for px, py in chips:
        order += [4 * px + 2 * py + c, 4 * px + 2 * py + 1 - c]
    return order


def _proj_fwd_gather(u16, blocks, tm):
    n = len(blocks)
    n_rows = u16.shape[0]
    n_i = n_rows // tm
    per_shard = W_IN_SHARD // UNIT
    assert n_i >= per_shard

    def wt_index(p, i, order):
        return (_dp_unit(per_shard * order[p] + jnp.minimum(i, per_shard - 1)), 0)

    def body(order_ref, u_ref, *refs):
        src, proj_ref, wt_ref, out = refs[:n], refs[n], refs[n + 1], refs[n + 2:2 * n + 2]
        wbuf, stage_sem, send, recv, own_sem = refs[2 * n + 2:]
        p, i = pl.program_id(0), pl.program_id(1)
        x, y, c, chips = _place()
        me, sibling = 4 * x + 2 * y + c, (x, y, 1 - c)

        def copy(t, k, block, to, from_input=False):
            return pltpu.make_async_remote_copy(
                src_ref=src[t] if from_input else out[t].at[block], dst_ref=out[t].at[block],
                send_sem=send.at[t, k], recv_sem=recv.at[t, k], device_id=to, device_id_type=MESH)

        def own_copies():
            return [pltpu.make_async_copy(src[t], out[t].at[me], own_sem.at[t]) for t in range(n)]

        def first_copies():
            first = []
            for t in range(n):
                first.append(copy(t, 0, me, sibling, from_input=True))
                first += [copy(t, 1 + j, me, (*chip, c), from_input=True) for j, chip in enumerate(chips)]
            return first

        def stage(slot, block):
            return pltpu.make_async_copy(out[0].at[block], wbuf.at[slot], stage_sem.at[slot])

        @pl.when((p == 0) & (i == 0))
        def _():
            for cp in own_copies() + first_copies():
                cp.start()
            mine = pltpu.make_async_copy(src[0], wbuf.at[0], stage_sem.at[0])
            mine.start()
            mine.wait()

        @pl.when((p > 0) & (i == 0))
        def _():
            stage(p % 2, order_ref[p]).wait()

        proj_ref[...] = _dot(u_ref[...], wbuf[p % 2])
        for r in range(per_shard):
            @pl.when(i == r)
            def _(r=r):
                wt_ref[...] = wbuf[p % 2, :, r * UNIT:(r + 1) * UNIT].astype(F32).T.astype(BF16)

        for nxt in range(1, N_DEV):
            @pl.when((p == nxt - 1) & (i == n_i - 1))
            def _(nxt=nxt):
                if nxt == 1:
                    block = 4 * x + 2 * y + 1 - c
                    copy(0, 0, block, sibling).wait_recv()
                else:
                    j, passed_on = divmod(nxt - 2, 2)
                    px, py = chips[j]
                    if passed_on:
                        block = 4 * px + 2 * py + 1 - c
                        copy(0, 4 + j, block, sibling).wait_recv()
                    else:
                        block = 4 * px + 2 * py + c
                        copy(0, 1 + j, block, sibling).wait_recv()
                        copy(0, 4 + j, block, sibling).start()
                stage(nxt % 2, block).start()

        @pl.when((p == N_DEV - 1) & (i == n_i - 1))
        def _():
            passed = [copy(0, 4 + j, 4 * px + 2 * py + c, sibling) for j, (px, py) in enumerate(chips)]
            for j, (px, py) in enumerate(chips):
                for t in range(1, n):
                    block = 4 * px + 2 * py + c
                    copy(t, 1 + j, block, sibling).wait_recv()
                    passed.append(copy(t, 4 + j, block, sibling))
                    passed[-1].start()
            for t in range(1, n):
                copy(t, 0, 4 * x + 2 * y + 1 - c, sibling).wait_recv()
                for j, (px, py) in enumerate(chips):
                    copy(t, 4 + j, 4 * px + 2 * py + 1 - c, sibling).wait_recv()
            for cp in first_copies() + passed:
                cp.wait_send()
            for cp in own_copies():
                cp.wait()

    return pl.pallas_call(
        body, name="proj_fwd_gather",
        grid_spec=pltpu.PrefetchScalarGridSpec(
            num_scalar_prefetch=1, grid=(N_DEV, n_i),
            in_specs=[pl.BlockSpec((tm, D), lambda p, i, order: (i, 0))] + [ANY] * n,
            out_specs=[pl.BlockSpec((tm, W_IN_SHARD), lambda p, i, order: (i, order[p])),
                       pl.BlockSpec((UNIT, D), wt_index)] + [ANY] * n,
            scratch_shapes=[pltpu.VMEM((2, D, W_IN_SHARD), BF16), pltpu.SemaphoreType.DMA((2,)),
                            pltpu.SemaphoreType.DMA((n, 7)), pltpu.SemaphoreType.DMA((n, 7)),
                            pltpu.SemaphoreType.DMA((n,))]),
        out_shape=[jax.ShapeDtypeStruct((n_rows, N_GROUPS * D), F32), jax.ShapeDtypeStruct((N_GROUPS * D, D), BF16)]
        + [jax.ShapeDtypeStruct((N_DEV, *b.shape), b.dtype) for b in blocks],
        compiler_params=_params(2),
    )(jnp.stack(_arrival_order()).astype(jnp.int32), u16, *blocks)


def _branch_a_fwd(proj, wo4_g, cw_a, tm):
    n_rows = proj.shape[0]

    def body(bp, cp, xp, za, cph, xph, w_ref, cw_ref, sa_ref, ya_ref, e_scr, o_scr):
        i = pl.program_id(0)
        _to_time_major(e_scr, 0, jnp.where(i > 0, cph[...] * xph[...], 0.0))
        _to_time_major(e_scr, HALO, cp[...] * xp[...])
        _conv(o_scr, e_scr, cw_ref, FWD_TAPS_A, tm, UNROLL_A)
        sa = (jax.nn.silu(za[...]) * (bp[...] * _from_time_major(o_scr, tm))).astype(BF16)
        sa_ref[...] = sa
        ya_ref[...] = _dot(sa, w_ref[...].reshape(D, D))

    return pl.pallas_call(
        body, name="branch_a_fwd", grid=(n_rows // tm,),
        in_specs=[_rows(tm, G_BA), _rows(tm, G_CA), _rows(tm, G_XA), _rows(tm, G_ZA),
                  _prev_halo(tm, G_CA), _prev_halo(tm, G_XA), _w_out_spec(0), _const(cw_a.shape)],
        out_specs=[_rows(tm), _rows(tm)],
        out_shape=[jax.ShapeDtypeStruct((n_rows, D), BF16), jax.ShapeDtypeStruct((n_rows, D), F32)],
        scratch_shapes=[_time_major(tm + HALO), _time_major(tm)],
        compiler_params=_params(1),
    )(proj, proj, proj, proj, proj, proj, wo4_g, cw_a)


def _layernorm_parts(cb, lg, lb):
    xc = cb - _mean(cb)
    rstd = lax.rsqrt(_mean(xc * xc) + EPS)
    xhat = xc * rstd
    return xhat, rstd, xhat * lg + lb


def _branch_b_fwd(proj, wo4_g, cw_b, conv_b_b, ln_g, ln_b, tm):
    n_rows = proj.shape[0]

    def body(vb, gb, zb, vbh, gbh, w_ref, cw_ref, bb_ref, lg_ref, lb_ref, cb_ref, sb_ref, yb_ref, e_scr, o_scr):
        i = pl.program_id(0)
        _to_time_major(e_scr, 0, jnp.where(i > 0, vbh[...] * jax.nn.sigmoid(gbh[...]), 0.0))
        _to_time_major(e_scr, HALO, vb[...] * jax.nn.sigmoid(gb[...]))
        _conv(o_scr, e_scr, cw_ref, FWD_TAPS_B, tm, UNROLL_B, bias_ref=bb_ref)
        cb = _from_time_major(o_scr, tm)
        cb_ref[...] = cb
        _, _, ln = _layernorm_parts(cb, lg_ref[...], lb_ref[...])
        sb = (jax.nn.silu(zb[...]) * jax.nn.silu(ln)).astype(BF16)
        sb_ref[...] = sb
        yb_ref[...] = _dot(sb, w_ref[...].reshape(D, D))

    return pl.pallas_call(
        body, name="branch_b_fwd", grid=(n_rows // tm,),
        in_specs=[_rows(tm, G_VB), _rows(tm, G_GB), _rows(tm, G_ZB), _prev_halo(tm, G_VB), _prev_halo(tm, G_GB),
                  _w_out_spec(1), _const(cw_b.shape), _const((LANE_GROUPS, LANES)), _const((1, D)), _const((1, D))],
        out_specs=[_rows(tm), _rows(tm), _rows(tm)],
        out_shape=[jax.ShapeDtypeStruct((n_rows, D), F32), jax.ShapeDtypeStruct((n_rows, D), BF16),
                   jax.ShapeDtypeStruct((n_rows, D), F32)],
        scratch_shapes=[_time_major(tm + HALO), _time_major(tm)],
        compiler_params=_params(1),
    )(proj, proj, proj, proj, proj, wo4_g, cw_b, conv_b_b.reshape(LANE_GROUPS, LANES), ln_g, ln_b)


def _attention(q16, kv_ref):
    probs, outs = [], []
    for h in range(N_HEADS):
        s = _dot_nt(q16[:, h * HEAD_DIM:(h + 1) * HEAD_DIM], kv_ref[h]) * (HEAD_DIM ** -0.5)
        e = jnp.exp(s - jnp.max(s, axis=-1, keepdims=True))
        p = e / jnp.sum(e, axis=-1, keepdims=True)
        probs.append(p)
        outs.append(_dot(p.astype(BF16), kv_ref[N_HEADS + h]))
    return probs, outs


def _branch_x_fwd(proj, kv16, wo4_g, tm):
    n_rows = proj.shape[0]

    def body(q, zx, kv_ref, w_ref, sx_ref, yx_ref):
        _, outs = _attention(q[...].astype(BF16), kv_ref)
        sx = (jax.nn.silu(zx[...]) * jnp.concatenate(outs, axis=-1)).astype(BF16)
        sx_ref[...] = sx
        yx_ref[...] = _dot(sx, w_ref[...].reshape(D, D))

    return pl.pallas_call(
        body, name="branch_x_fwd", grid=(n_rows // tm,),
        in_specs=[_rows(tm, G_Q), _rows(tm, G_ZX), _const(kv16.shape), _w_out_spec(2)],
        out_specs=[_rows(tm), _rows(tm)],
        out_shape=[jax.ShapeDtypeStruct((n_rows, D), BF16), jax.ShapeDtypeStruct((n_rows, D), F32)],
        compiler_params=_params(1),
    )(proj, proj, kv16, wo4_g)


def _merge_fwd_bwd(proj, ya, yb, yx, x, target, wo4_g, final_g, tm):
    n_rows = proj.shape[0]
    inv_d = 1.0 / D

    def body(ga, gb, gx, ya_ref, yb_ref, yx_ref, x_ref, t_ref, w_ref, fg_ref,
             dh_ref, dya_ref, dyb_ref, dyx_ref, dp_ref, dw_ref, dfg_ref, sq_ref):
        i = pl.program_id(0)
        wo = w_ref[...].reshape(D, D)
        sig = [jax.nn.sigmoid(g[...]) for g in (ga, gb, gx)]
        ys = [ya_ref[...], yb_ref[...], yx_ref[...]]
        m16 = (sig[0] * ys[0] + sig[1] * ys[1] + sig[2] * ys[2]).astype(BF16)
        h = x_ref[...] + _dot(m16, wo)
        r = lax.rsqrt(_mean(h * h) + EPS)
        hn = h * r
        fg = fg_ref[...]
        err = hn * fg - t_ref[...]
        dy = err * inv_d
        dhn = dy * fg
        dh = r * (dhn - hn * _mean(dhn * hn))
        dh_ref[...] = dh
        dh16 = dh.astype(BF16)
        dm = _dot_nt(dh16, wo)
        for n, out in enumerate((dya_ref, dyb_ref, dyx_ref)):
            out[...] = (sig[n] * dm).astype(BF16)
            dp_ref[:, n * D:(n + 1) * D] = (dm * ys[n] * (sig[n] * (1.0 - sig[n]))).astype(BF16)

        @pl.when(i == 0)
        def _():
            dw_ref[...] = jnp.zeros_like(dw_ref)
            dfg_ref[...] = jnp.zeros_like(dfg_ref)
            sq_ref[...] = jnp.zeros_like(sq_ref)

        dw_ref[0] += _dot_tn(m16, dh16)
        dfg_ref[...] += _fold8(dy * hn)
        sq_ref[...] += _fold8(err * err)

    vec = jax.ShapeDtypeStruct((SUBLANES, D), F32)
    return pl.pallas_call(
        body, name="merge_fwd_bwd", grid=(n_rows // tm,),
        in_specs=[_rows(tm, G_GA), _rows(tm, G_GBB), _rows(tm, G_GX), _rows(tm), _rows(tm), _rows(tm), _rows(tm),
                  _rows(tm), _w_out_spec(3), _const((1, D))],
        out_specs=[_rows(tm), _rows(tm), _rows(tm), _rows(tm), pl.BlockSpec((tm, 3 * D), lambda i: (i, 3)),
                   pl.BlockSpec((1, D, D), lambda i: (3, 0, 0)), _const((SUBLANES, D)), _const((SUBLANES, D))],
        out_shape=[jax.ShapeDtypeStruct((n_rows, D), F32), jax.ShapeDtypeStruct((n_rows, D), BF16),
                   jax.ShapeDtypeStruct((n_rows, D), BF16), jax.ShapeDtypeStruct((n_rows, D), BF16),
                   jax.ShapeDtypeStruct((n_rows, N_GROUPS * D), BF16), jax.ShapeDtypeStruct((4, D, D), F32), vec, vec],
        compiler_params=_params(1),
    )(proj, proj, proj, ya, yb, yx, x, target, wo4_g, final_g)


def _branch_a_bwd(dya, proj, sa16, wo4_g, cw_a, dproj, dw4, tm):
    n_rows = proj.shape[0]
    n_tiles = n_rows // tm

    def body(dya_ref, bp, cp, xp, za, sa_ref, dyan, bpn, zan, cph, xph, w_ref, cw_ref, dp_in, dw_in,
             dp_ref, dw_ref, dwa_ref, e1, e2, o_scr):
        del dp_in, dw_in
        i = pl.program_id(0)
        woa = w_ref[...].reshape(D, D)
        dya16 = dya_ref[...]
        _to_time_major(e1, 0, jnp.where(i > 0, cph[...] * xph[...], 0.0))
        _to_time_major(e1, HALO, cp[...] * xp[...])
        _conv(o_scr, e1, cw_ref, FWD_TAPS_A, tm, UNROLL_A)
        ca = _from_time_major(o_scr, tm)
        dsa = _dot_nt(dya16, woa)
        silu_z, dsilu_z = _silu_and_grad(za[...])
        t = dsa * silu_z
        dp_ref[:, 0 * D:1 * D] = (t * ca).astype(BF16)
        dp_ref[:, 3 * D:4 * D] = (dsa * (bp[...] * ca) * dsilu_z).astype(BF16)
        dsan = _dot_nt(dyan[...], woa)
        dcan = (dsan * jax.nn.silu(zan[...])) * bpn[...]
        _to_time_major(e2, 0, t * bp[...])
        _to_time_major(e2, tm, jnp.where(i < n_tiles - 1, dcan, 0.0))

        @pl.when(i == 0)
        def _():
            dw_ref[...] = jnp.zeros_like(dw_ref)
            dwa_ref[...] = jnp.zeros_like(dwa_ref)

        _conv_wgrad(dwa_ref, e2, e1, FWD_TAPS_A, tm, UNROLL_A)
        dw_ref[0] += _dot_tn(sa_ref[...], dya16)
        _conv(o_scr, e2, cw_ref, BWD_TAPS_A, tm, UNROLL_A)
        dprod = _from_time_major(o_scr, tm)
        dp_ref[:, 1 * D:2 * D] = (dprod * xp[...]).astype(BF16)
        dp_ref[:, 2 * D:3 * D] = (dprod * cp[...]).astype(BF16)

    return pl.pallas_call(
        body, name="branch_a_bwd", grid=(n_tiles,),
        in_specs=[_rows(tm), _rows(tm, G_BA), _rows(tm, G_CA), _rows(tm, G_XA), _rows(tm, G_ZA), _rows(tm),
                  _next_halo(tm, n_rows), _next_halo(tm, n_rows, G_BA), _next_halo(tm, n_rows, G_ZA),
                  _prev_halo(tm, G_CA), _prev_halo(tm, G_XA), _w_out_spec(0), _const(cw_a.shape), ANY, ANY],
        out_specs=[pl.BlockSpec((tm, 4 * D), lambda i: (i, 0)), pl.BlockSpec((1, D, D), lambda i: (0, 0, 0)),
                   _const((K_A * LANE_GROUPS, LANES))],
        out_shape=[jax.ShapeDtypeStruct(dproj.shape, BF16), jax.ShapeDtypeStruct(dw4.shape, F32),
                   jax.ShapeDtypeStruct((K_A * LANE_GROUPS, LANES), F32)],
        input_output_aliases={13: 0, 14: 1},
        scratch_shapes=[_time_major(tm + HALO), _time_major(tm + HALO), _time_major(tm)],
        compiler_params=_params(1),
    )(dya, proj, proj, proj, proj, sa16, dya, proj, proj, proj, proj, wo4_g, cw_a, dproj, dw4)


def _branch_b_bwd(dyb, proj, cb, sb16, wo4_g, cw_b, ln_g, ln_b, dproj, dw4, tm):
    n_rows = proj.shape[0]
    n_tiles = n_rows // tm

    def body(dyb_ref, zb, cb_ref, vb, gb, sb_ref, dybn, zbn, cbn, vbh, gbh, w_ref, cw_ref, lg_ref, lb_ref,
             dp_in, dw_in, dp_ref, dw_ref, dwb_ref, dbb_ref, dlg_ref, dlb_ref, e1, e2, o_scr):
        del dp_in, dw_in
        i = pl.program_id(0)
        wob = w_ref[...].reshape(D, D)
        lg, lb = lg_ref[...], lb_ref[...]

        def conv_out_grad(dy16, z, c):
            dsb = _dot_nt(dy16, wob)
            xhat, rstd, ln = _layernorm_parts(c, lg, lb)
            sw, dsw = _silu_and_grad(ln)
            sz, dsz = _silu_and_grad(z)
            dln = (dsb * sz) * dsw
            dxhat = dln * lg
            dcb = rstd * (dxhat - _mean(dxhat) - xhat * _mean(dxhat * xhat))
            return dsb * sw * dsz, dln, xhat, dcb

        dyb16 = dyb_ref[...]
        dzb, dln, xhat, dcb = conv_out_grad(dyb16, zb[...], cb_ref[...])
        dp_ref[:, 2 * D:3 * D] = dzb.astype(BF16)
        _, _, _, dcbn = conv_out_grad(dybn[...], zbn[...], cbn[...])
        _to_time_major(e2, 0, dcb)
        _to_time_major(e2, tm, jnp.where(i < n_tiles - 1, dcbn, 0.0))

        @pl.when(i == 0)
        def _():
            dw_ref[...] = jnp.zeros_like(dw_ref)
            dwb_ref[...] = jnp.zeros_like(dwb_ref)
            dbb_ref[...] = jnp.zeros_like(dbb_ref)
            dlg_ref[...] = jnp.zeros_like(dlg_ref)
            dlb_ref[...] = jnp.zeros_like(dlb_ref)

        dlg_ref[...] += _fold8(dln * xhat)
        dlb_ref[...] += _fold8(dln)
        dbb_ref[...] += _fold8(dcb)
        dw_ref[0] += _dot_tn(sb_ref[...], dyb16)
        sg = jax.nn.sigmoid(gb[...])
        _to_time_major(e1, 0, jnp.where(i > 0, vbh[...] * jax.nn.sigmoid(gbh[...]), 0.0))
        _to_time_major(e1, HALO, vb[...] * sg)
        _conv_wgrad(dwb_ref, e2, e1, FWD_TAPS_B, tm, UNROLL_WGRAD_B)
        _conv(o_scr, e2, cw_ref, BWD_TAPS_B, tm, UNROLL_B)
        dglu = _from_time_major(o_scr, tm)
        dp_ref[:, 0 * D:1 * D] = (dglu * sg).astype(BF16)
        dp_ref[:, 1 * D:2 * D] = (dglu * vb[...] * (sg * (1.0 - sg))).astype(BF16)

    vec = jax.ShapeDtypeStruct((SUBLANES, D), F32)
    return pl.pallas_call(
        body, name="branch_b_bwd", grid=(n_tiles,),
        in_specs=[_rows(tm), _rows(tm, G_ZB), _rows(tm), _rows(tm, G_VB), _rows(tm, G_GB), _rows(tm),
                  _next_halo(tm, n_rows), _next_halo(tm, n_rows, G_ZB), _next_halo(tm, n_rows),
                  _prev_halo(tm, G_VB), _prev_halo(tm, G_GB), _w_out_spec(1), _const(cw_b.shape), _const((1, D)),
                  _const((1, D)), ANY, ANY],
        out_specs=[pl.BlockSpec((tm, 3 * D), lambda i: (i, 2)), pl.BlockSpec((1, D, D), lambda i: (1, 0, 0)),
                   _const((K_B * LANE_GROUPS, LANES)), _const((SUBLANES, D)), _const((SUBLANES, D)),
                   _const((SUBLANES, D))],
        out_shape=[jax.ShapeDtypeStruct(dproj.shape, BF16), jax.ShapeDtypeStruct(dw4.shape, F32),
                   jax.ShapeDtypeStruct((K_B * LANE_GROUPS, LANES), F32), vec, vec, vec],
        input_output_aliases={15: 0, 16: 1},
        scratch_shapes=[_time_major(tm + HALO), _time_major(tm + HALO), _time_major(tm)],
        compiler_params=_params(1),
    )(dyb, proj, cb, proj, proj, sb16, dyb, proj, cb, proj, proj, wo4_g, cw_b, ln_g, ln_b, dproj, dw4)


def _branch_x_bwd(dyx, proj, sx16, kv16, wo4_g, dproj, dw4, tm):
    n_rows = proj.shape[0]
    scale = HEAD_DIM ** -0.5

    def body(dyx_ref, q, zx, sx_ref, kv_ref, w_ref, dp_in, dw_in, dp_ref, dw_ref, dkv_ref):
        del dp_in, dw_in
        i = pl.program_id(0)
        dyx16 = dyx_ref[...]
        q16 = q[...].astype(BF16)
        probs, outs = _attention(q16, kv_ref)
        dsx = _dot_nt(dyx16, w_ref[...].reshape(D, D))
        silu_z, dsilu_z = _silu_and_grad(zx[...])
        dp_ref[:, D:2 * D] = (dsx * jnp.concatenate(outs, axis=-1) * dsilu_z).astype(BF16)
        do16 = (dsx * silu_z).astype(BF16)

        @pl.when(i == 0)
        def _():
            dw_ref[...] = jnp.zeros_like(dw_ref)
            dkv_ref[...] = jnp.zeros_like(dkv_ref)

        for h in range(N_HEADS):
            cols = slice(h * HEAD_DIM, (h + 1) * HEAD_DIM)
            p = probs[h]
            dprob = _dot_nt(do16[:, cols], kv_ref[N_HEADS + h])
            ds16 = ((p * (dprob - jnp.sum(p * dprob, axis=-1, keepdims=True))) * scale).astype(BF16)
            dp_ref[:, cols] = _dot(ds16, kv_ref[h]).astype(BF16)
            dkv_ref[h] += _dot_tn(ds16, q16[:, cols])
            dkv_ref[N_HEADS + h] += _dot_tn(p.astype(BF16), do16[:, cols])
        dw_ref[0] += _dot_tn(sx_ref[...], dyx16)

    return pl.pallas_call(
        body, name="branch_x_bwd", grid=(n_rows // tm,),
        in_specs=[_rows(tm), _rows(tm, G_Q), _rows(tm, G_ZX), _rows(tm), _const(kv16.shape), _w_out_spec(2), ANY, ANY],
        out_specs=[pl.BlockSpec((tm, 2 * D), lambda i: (i, 2)), pl.BlockSpec((1, D, D), lambda i: (2, 0, 0)),
                   _const(kv16.shape)],
        out_shape=[jax.ShapeDtypeStruct(dproj.shape, BF16), jax.ShapeDtypeStruct(dw4.shape, F32),
                   jax.ShapeDtypeStruct(kv16.shape, F32)],
        input_output_aliases={6: 0, 7: 1},
        compiler_params=_params(1),
    )(dyx, proj, proj, sx16, kv16, wo4_g, dproj, dw4)


def _dp_unit(u):
    g = u // 2
    pos = jnp.where(g < G_VB, g, jnp.where(g < G_Q, g + 2, jnp.where(g < G_GA, g - 3, g)))
    return 2 * pos + u % 2


def _scatter_copies(srcs, lands, send, recv):
    x, y, c = lax.axis_index("x"), lax.axis_index("y"), lax.axis_index("c")
    copies = []
    for n in range(N_DEV - 1):
        flip = n + 1
        px = 1 - x if flip & 4 else x
        py = 1 - y if flip & 2 else y
        pc = 1 - c if flip & 1 else c
        for t, (src, land) in enumerate(zip(srcs, lands)):
            copies.append(pltpu.make_async_remote_copy(
                src_ref=src.at[4 * px + 2 * py + pc], dst_ref=land.at[n], send_sem=send.at[t, n],
                recv_sem=recv.at[t, n], device_id=(px, py, pc), device_id_type=MESH))
    return copies


def _scatter_sems(n_tensors):
    return [pltpu.SemaphoreType.DMA((n_tensors, N_DEV - 1)), pltpu.SemaphoreType.DMA((n_tensors, N_DEV - 1))]


def _landing(shard_shape, lead=()):
    return jax.ShapeDtypeStruct((*lead, N_DEV - 1, *shard_shape), BF16)


def _w_in_grad(ut, dproj, tk, dw4_16, dwkv16):
    n_rows = dproj.shape[0]
    n_k = n_rows // tk
    per_shard = W_IN_SHARD // UNIT

    def body(ut_ref, dp0, dp1, dp2, dw4_ref, dwkv_ref, out_ref, out16_ref, l4_ref, lkv_ref, acc, send, recv):
        q, t = pl.program_id(0), pl.program_id(1)

        def small_copies():
            return _scatter_copies([dw4_ref.at[w] for w in range(4)] + [dwkv_ref],
                                   [l4_ref.at[w] for w in range(4)] + [lkv_ref], send, recv)

        @pl.when((q == 0) & (t == 0))
        def _():
            for cp in small_copies():
                cp.start()

        for r, dp_ref in enumerate((dp0, dp1, dp2)):
            cols = slice(r * UNIT, (r + 1) * UNIT)

            @pl.when(t == 0)
            def _(dp_ref=dp_ref, cols=cols):
                acc[:, cols] = _dot(ut_ref[...], dp_ref[...])

            @pl.when(t > 0)
            def _(dp_ref=dp_ref, cols=cols):
                acc[:, cols] += _dot(ut_ref[...], dp_ref[...])

        @pl.when(t == n_k - 1)
        def _():
            out_ref[0] = acc[...]
            out16_ref[0] = acc[...].astype(BF16)

        @pl.when((q == N_DEV - 1) & (t == n_k - 1))
        def _():
            for cp in small_copies():
                cp.wait()

    def dp_spec(r):
        return pl.BlockSpec((tk, UNIT), lambda q, t: (t, _dp_unit(per_shard * q + r)))

    shard = pl.BlockSpec((1, D, W_IN_SHARD), lambda q, t: (q, 0, 0))
    return pl.pallas_call(
        body, name="w_in_grad", grid=(N_DEV, n_k),
        in_specs=[pl.BlockSpec((D, tk), lambda q, t: (0, t)), dp_spec(0), dp_spec(1), dp_spec(2), ANY, ANY],
        out_specs=[shard, shard, ANY, ANY],
        out_shape=[jax.ShapeDtypeStruct((N_DEV, D, W_IN_SHARD), F32), jax.ShapeDtypeStruct((N_DEV, D, W_IN_SHARD), BF16),
                   _landing(dw4_16.shape[2:], lead=(4,)), _landing(dwkv16.shape[1:])],
        scratch_shapes=[pltpu.VMEM((D, W_IN_SHARD), F32)] + _scatter_sems(5),
        compiler_params=_params(2),
    )(ut, dproj, dproj, dproj, dw4_16, dwkv16)


def _x_grad(dproj, win_t, x, dh, norm_g, dwin16, tm):
    n_rows = x.shape[0]
    n_tiles = n_rows // tm
    n_k = N_GROUPS * D // X_GRAD_K

    def body(dp_ref, wt_ref, x_ref, dh_ref, g_ref, dwin_ref, gx_ref, dg_ref, land_ref, acc, send, recv):
        i, g = pl.program_id(0), pl.program_id(1)

        @pl.when((i == 0) & (g == 0))
        def _():
            for cp in _scatter_copies([dwin_ref], [land_ref], send, recv):
                cp.start()
            dg_ref[...] = jnp.zeros_like(dg_ref)

        @pl.when(g == 0)
        def _():
            acc[...] = _dot(dp_ref[...], wt_ref[...])

        @pl.when(g > 0)
        def _():
            acc[...] += _dot(dp_ref[...], wt_ref[...])

        @pl.when(g == n_k - 1)
        def _():
            du = acc[...]
            xf = x_ref[...]
            r = lax.rsqrt(_mean(xf * xf) + EPS)
            xn = xf * r
            dun = du * g_ref[...]
            gx_ref[...] = dh_ref[...] + r * (dun - xn * _mean(dun * xn))
            dg_ref[...] += _fold8(du * xn)

        @pl.when((i == n_tiles - 1) & (g == n_k - 1))
        def _():
            for cp in _scatter_copies([dwin_ref], [land_ref], send, recv):
                cp.wait()

    return pl.pallas_call(
        body, name="x_grad", grid=(n_tiles, n_k),
        in_specs=[pl.BlockSpec((tm, X_GRAD_K), lambda i, g: (i, g)), pl.BlockSpec((X_GRAD_K, D), lambda i, g: (g, 0)),
                  pl.BlockSpec((tm, D), lambda i, g: (i, 0)), pl.BlockSpec((tm, D), lambda i, g: (i, 0)),
                  _const((1, D)), ANY],
        out_specs=[pl.BlockSpec((tm, D), lambda i, g: (i, 0)), _const((SUBLANES, D)), ANY],
        out_shape=[jax.ShapeDtypeStruct((n_rows, D), F32), jax.ShapeDtypeStruct((SUBLANES, D), F32),
                   _landing(dwin16.shape[1:])],
        scratch_shapes=[pltpu.VMEM((tm, D), F32)] + _scatter_sems(1),
        compiler_params=_params(2),
    )(dproj, win_t, x, dh, norm_g, dwin16)


def _local_step(x, mem, target, norm_g, conv_b_b, ln_g, ln_b, mem_g, final_g, shards):
    n_rows = x.shape[0]
    tm = min(512, n_rows)
    big = min(1024, n_rows)
    u16, ut = _rmsnorm_fwd(x, norm_g, big)
    proj, win_t, _, wkv_g, wo4_g, cw_g = _proj_fwd_gather(u16, shards, big)
    cw_rows = cw_g.transpose(1, 0, 2).reshape((SUBLANES + HALO) * LANE_GROUPS, LANES)
    cw_a, cw_b = cw_rows[:SUBLANES * LANE_GROUPS], cw_rows[SUBLANES * LANE_GROUPS:]
    kv16, mn16 = _kv_fwd(mem, mem_g, wkv_g)
    sa16, ya = _branch_a_fwd(proj, wo4_g, cw_a, tm)
    cb, sb16, yb = _branch_b_fwd(proj, wo4_g, cw_b, conv_b_b, ln_g, ln_b, tm)
    sx16, yx = _branch_x_fwd(proj, kv16, wo4_g, tm)
    dh, dya, dyb, dyx, dproj, dw4, dfg, sq = _merge_fwd_bwd(proj, ya, yb, yx, x, target, wo4_g, final_g,
                                                             min(256, n_rows))
    dproj, dw4, dwa = _branch_a_bwd(dya, proj, sa16, wo4_g, cw_a, dproj, dw4, tm)
    dproj, dw4, dwb, dbb, dlg, dlb = _branch_b_bwd(dyb, proj, cb, sb16, wo4_g, cw_b, ln_g, ln_b, dproj, dw4, tm)
    dproj, dw4, dkv = _branch_x_bwd(dyx, proj, sx16, kv16, wo4_g, dproj, dw4, tm)
    dwkv_g, dmg = _kv_bwd(dkv, mem, mem_g, mn16, wkv_g)
    dw4 = dw4.reshape(4, N_DEV, D // N_DEV, D)
    dwin_g, dwin16, land4, landkv = _w_in_grad(ut, dproj, min(2048, n_rows), dw4.astype(BF16), dwkv_g.astype(BF16))
    gx, dng, landin = _x_grad(dproj, win_t, x, dh, norm_g, dwin16, big)
    small = {SV_NORM_G: dng, SV_CONV_B_B: dbb, SV_LN_G: dlg, SV_LN_B: dlb, SV_MEM_G: dmg, SV_FINAL_G: dfg, SV_LOSS: sq}
    grads = [(dwin_g[None], landin[None]), (dw4, land4), (dwkv_g[None], landkv[None])]
    return gx, grads, small, dwa.reshape(K_A, D), dwb.reshape(K_B, D)


def _allgather_small(small, conv_rows):
    keys = sorted(small)

    def body(*refs):
        parts, (conv_ref, out_ref, mine, send, recv) = refs[:len(keys)], refs[len(keys):]
        x, y, c, chips = _place()
        me, sibling = 4 * x + 2 * y + c, (x, y, 1 - c)
        mine[pl.ds(0, SV_CONV_A), :] = jnp.zeros((SV_CONV_A, D), F32)
        for key, part in zip(keys, parts):
            mine[key:key + 1, :] = jnp.sum(part[...], axis=0, keepdims=True)
        mine[pl.ds(SV_CONV_A, SV_ROWS - SV_CONV_A), :] = conv_ref[...]
        out_ref[me] = mine[...]

        def copy(k, block, to, from_mine=False):
            return pltpu.make_async_remote_copy(
                src_ref=mine if from_mine else out_ref.at[block], dst_ref=out_ref.at[block],
                send_sem=send.at[k], recv_sem=recv.at[k], device_id=to, device_id_type=MESH)

        first = [copy(0, me, sibling, from_mine=True)]
        first += [copy(1 + j, me, (*chip, c), from_mine=True) for j, chip in enumerate(chips)]
        for cp in first:
            cp.start()
        passed = []
        for j, (px, py) in enumerate(chips):
            block = 4 * px + 2 * py + c
            copy(1 + j, block, sibling).wait_recv()
            passed.append(copy(4 + j, block, sibling))
            passed[-1].start()
        copy(0, 4 * x + 2 * y + 1 - c, sibling).wait_recv()
        for j, (px, py) in enumerate(chips):
            copy(4 + j, 4 * px + 2 * py + 1 - c, sibling).wait_recv()
        for cp in first + passed:
            cp.wait_send()

    vmem = pl.BlockSpec(memory_space=pltpu.VMEM)
    return pl.pallas_call(
        body, name="allgather_small",
        in_specs=[vmem] * (len(keys) + 1), out_specs=vmem,
        out_shape=jax.ShapeDtypeStruct((N_DEV, SV_ROWS, D), F32),
        scratch_shapes=[pltpu.VMEM((SV_ROWS, D), F32), pltpu.SemaphoreType.DMA((7,)), pltpu.SemaphoreType.DMA((7,))],
    )(*[small[k] for k in keys], conv_rows)


def _adamw(w, g, m, v):
    m = ADAM_B1 * m + (1.0 - ADAM_B1) * g
    v = ADAM_B2 * v + (1.0 - ADAM_B2) * (g * g)
    m_hat = m / (1.0 - ADAM_B1 ** ADAM_STEP)
    v_hat = v / (1.0 - ADAM_B2 ** ADAM_STEP)
    return -ADAM_LR * (m_hat / (jnp.sqrt(v_hat) + ADAM_EPS) + ADAM_WD * w), m, v


def _adamw_shard(own, landed, piece, k_arr, w, m, v, tr):
    n_r, n_c = w.shape
    n_landed = landed.shape[1]

    def body(k_ref, own_ref, *refs):
        del k_ref
        landed_refs, (w_ref, m_ref, v_ref, g_out, d_out, m_out, v_out) = refs[:n_landed], refs[n_landed:]
        g = own_ref[0, 0]
        for landed_ref in landed_refs:
            g = g + landed_ref[0, 0].astype(F32)
        g_out[...] = g
        d_out[...], m_out[...], v_out[...] = _adamw(w_ref[...], g, m_ref[...], v_ref[...])

    blk = (1, 1, tr, n_c)
    flat = pl.BlockSpec((tr, n_c), lambda r, k: (r, 0))
    return pl.pallas_call(
        body, name="adamw_shard",
        grid_spec=pltpu.PrefetchScalarGridSpec(
            num_scalar_prefetch=1, grid=(n_r // tr,),
            in_specs=[pl.BlockSpec(blk, lambda r, k: (piece, k[0], r, 0))]
            + [pl.BlockSpec(blk, functools.partial(lambda r, k, j: (piece, j, r, 0), j=j)) for j in range(n_landed)]
            + [flat] * 3,
            out_specs=[flat] * 4),
        out_shape=[jax.ShapeDtypeStruct((n_r, n_c), F32)] * 4,
        compiler_params=_params(1),
    )(k_arr, own, *([landed] * n_landed), w, m, v)


def _adamw_small(gathered, k_arr, vectors, conv_a, conv_b):
    n_vec = len(vectors)
    cols = D // N_DEV

    def body(k_ref, full_ref, cols_ref, *refs):
        del k_ref
        ins, outs = refs[:3 * (n_vec + 2)], refs[3 * (n_vec + 2):-2]
        tot, tot_cols = refs[-2:]
        tot[...] = full_ref[0]
        tot_cols[...] = cols_ref[0]
        for dev in range(1, N_DEV):
            tot[...] += full_ref[dev]
            tot_cols[...] += cols_ref[dev]
        loss = (0.5 / D) * jnp.sum(tot[SV_LOSS:SV_LOSS + 1, :])
        outs[0][...] = jnp.full(outs[0].shape, loss, F32)
        grads = [tot[n:n + 1, :] for n in range(n_vec)]
        grads += [tot_cols[pl.ds(SV_CONV_A, K_A), :], tot_cols[pl.ds(SV_CONV_B, K_B), :]]
        for n, g in enumerate(grads):
            w_ref, m_ref, v_ref = ins[3 * n:3 * n + 3]
            g_out, d_out, m_out, v_out = outs[1 + 4 * n:5 + 4 * n]
            g_out[...] = g
            d_out[...], m_out[...], v_out[...] = _adamw(w_ref[...], g, m_ref[...], v_ref[...])

    weights = list(vectors) + [conv_a, conv_b]
    flat_in = [a for wmv in weights for a in wmv]
    out_shape = [jax.ShapeDtypeStruct((SUBLANES, 128), F32)]
    for wmv in weights:
        out_shape += [jax.ShapeDtypeStruct(wmv[0].shape, F32)] * 4
    return pl.pallas_call(
        body, name="adamw_small",
        grid_spec=pltpu.PrefetchScalarGridSpec(
            num_scalar_prefetch=1, grid=(1,),
            in_specs=[pl.BlockSpec((N_DEV, SV_ROWS, D), lambda i, k: (0, 0, 0)),
                      pl.BlockSpec((N_DEV, SV_ROWS, cols), lambda i, k: (0, 0, k[0]))]
            + [pl.BlockSpec(a.shape, lambda i, k: (0, 0)) for a in flat_in],
            out_specs=[pl.BlockSpec(s.shape, lambda i, k: (0, 0)) for s in out_shape],
            scratch_shapes=[pltpu.VMEM((SV_ROWS, D), F32), pltpu.VMEM((SV_ROWS, cols), F32)]),
        out_shape=out_shape,
        compiler_params=_params(1),
    )(k_arr, gathered, gathered, *flat_in)


def kernel(x, mem, norm_g, w_in, conv_a_w, w_out_a, conv_b_w, conv_b_b, ln_b_g, ln_b_b, w_out_b, mem_norm_g, w_kv, w_out_x, w_o, final_g, loss_target, m_norm_g, m_w_in, m_conv_a_w, m_w_out_a, m_conv_b_w, m_conv_b_b, m_ln_b_g, m_ln_b_b, m_w_out_b, m_mem_norm_g, m_w_kv, m_w_out_x, m_w_o, m_final_g, v_norm_g, v_w_in, v_conv_a_w, v_w_out_a, v_conv_b_w, v_conv_b_b, v_ln_b_g, v_ln_b_b, v_w_out_b, v_mem_norm_g, v_w_kv, v_w_out_x, v_w_o, v_final_g):
    xi, yi, ci = lax.axis_index("x"), lax.axis_index("y"), lax.axis_index("c")
    k_arr = jnp.reshape(4 * xi + 2 * yi + ci, (1,)).astype(jnp.int32)

    cw = jnp.concatenate([jnp.pad(conv_a_w[0], ((0, SUBLANES - K_A), (0, 0))),
                          jnp.pad(conv_b_w[0], ((0, HALO - K_B), (0, 0)))], axis=0)
    wo4 = jnp.stack([w_out_a[0], w_out_b[0], w_out_x[0], w_o[0]]).astype(BF16)
    shards = [w_in[0].astype(BF16), w_kv[0].astype(BF16), wo4, cw]

    final_g2 = final_g.reshape(1, D)
    gx, grads, small, dwa, dwb = _local_step(
        x[0], mem[0], loss_target[0], norm_g, conv_b_b, ln_b_g, ln_b_b, mem_norm_g, final_g2, shards)

    conv_rows = jnp.concatenate([jnp.pad(dwa, ((0, SUBLANES - K_A), (0, 0))),
                                 jnp.pad(dwb, ((0, HALO - K_B), (0, 0)))], axis=0)
    gathered_small = _allgather_small(small, conv_rows)

    tiles = [256, D // N_DEV, 256]

    def shard(a, l, w, m, v):
        return _adamw_shard(grads[a][0], grads[a][1], l, k_arr, w[0], m[0], v[0], tiles[a])

    res = {
        "w_in": shard(0, 0, w_in, m_w_in, v_w_in),
        "w_out_a": shard(1, 0, w_out_a, m_w_out_a, v_w_out_a),
        "w_out_b": shard(1, 1, w_out_b, m_w_out_b, v_w_out_b),
        "w_out_x": shard(1, 2, w_out_x, m_w_out_x, v_w_out_x),
        "w_o": shard(1, 3, w_o, m_w_o, v_w_o),
        "w_kv": shard(2, 0, w_kv, m_w_kv, v_w_kv),
    }
    res = {name: tuple(r[None] for r in four) for name, four in res.items()}
    vectors = [(norm_g, m_norm_g, v_norm_g), (conv_b_b, m_conv_b_b, v_conv_b_b), (ln_b_g, m_ln_b_g, v_ln_b_g),
               (ln_b_b, m_ln_b_b, v_ln_b_b), (mem_norm_g, m_mem_norm_g, v_mem_norm_g),
               (final_g2, m_final_g.reshape(1, D), v_final_g.reshape(1, D))]
    out = _adamw_small(gathered_small, k_arr, vectors, (conv_a_w[0], m_conv_a_w[0], v_conv_a_w[0]),
                       (conv_b_w[0], m_conv_b_w[0], v_conv_b_w[0]))
    loss = out[0][0, 0]
    names = ["norm_g", "conv_b_b", "ln_b_g", "ln_b_b", "mem_norm_g", "final_g", "conv_a_w", "conv_b_w"]
    for n, name in enumerate(names):
        four = out[1 + 4 * n:5 + 4 * n]
        if name == "final_g":
            four = [r.reshape(D) for r in four]
        elif name.startswith("conv_") and name.endswith("_w"):
            four = [r[None] for r in four]
        res[name] = tuple(four)

    order = ["norm_g", "w_in", "conv_a_w", "w_out_a", "conv_b_w", "conv_b_b", "ln_b_g", "ln_b_b", "w_out_b",
             "mem_norm_g", "w_kv", "w_out_x", "w_o", "final_g"]
    return (loss, gx[None], *[res[n][0] for n in order], *[res[n][1] for n in order],
            *[res[n][2] for n in order], *[res[n][3] for n in order])
```

```python
import functools

import jax
import jax.numpy as jnp
from jax import lax
from jax.experimental import pallas as pl
from jax.experimental.pallas import tpu as pltpu

F32, BF16 = jnp.float32, jnp.bfloat16
D = 1024
N_DEV = 8
N_HEADS = 4
HEAD_DIM = D // N_HEADS
N_GROUPS = 12
W_IN_SHARD = N_GROUPS * D // N_DEV
UNIT = 512
X_GRAD_K = 2 * D
K_A, K_B = 3, 31
EPS = 1e-6
HALO = 32
SUBLANES = 8
LANES = 128
LANE_GROUPS = D // LANES
UNROLL_A, UNROLL_B = 8, 4
UNROLL_WGRAD_B = 4
WGRAD_TAPS = 16
CONV_PARTIAL_SUMS = 4
VMEM_LIMIT = 56 << 20
MESH = pl.DeviceIdType.MESH
ANY = pl.BlockSpec(memory_space=pl.ANY)

G_BA, G_CA, G_XA, G_ZA, G_VB, G_GB, G_ZB, G_Q, G_ZX, G_GA, G_GBB, G_GX = range(N_GROUPS)
DP_POS = (0, 1, 2, 3, 6, 7, 8, 4, 5, 9, 10, 11)

ADAM_LR, ADAM_B1, ADAM_B2, ADAM_EPS, ADAM_WD, ADAM_STEP = 0.001, 0.9, 0.999, 1e-08, 0.01, 10

SV_NORM_G, SV_CONV_B_B, SV_LN_G, SV_LN_B, SV_MEM_G, SV_FINAL_G, SV_LOSS = range(7)
SV_CONV_A, SV_CONV_B, SV_ROWS = 8, 16, 48


def _dot(a, b):
    return jnp.dot(a, b, preferred_element_type=F32)


def _dot_nt(a, b):
    return lax.dot_general(a, b, (((1,), (1,)), ((), ())), preferred_element_type=F32)


def _dot_tn(a, b):
    return lax.dot_general(a, b, (((0,), (0,)), ((), ())), preferred_element_type=F32)


def _silu_and_grad(z):
    s = jax.nn.sigmoid(z)
    return z * s, s * (1.0 + z * (1.0 - s))


def _fold8(a):
    return a.reshape(a.shape[0] // SUBLANES, SUBLANES, a.shape[1]).sum(axis=0)


def _mean(a):
    return jnp.mean(a, axis=-1, keepdims=True)


def _f32(ref):
    return ref[...].astype(F32)


def _params(n_grid):
    return pltpu.CompilerParams(dimension_semantics=("arbitrary",) * n_grid, vmem_limit_bytes=VMEM_LIMIT)


def _rows(tm, col=0):
    return pl.BlockSpec((tm, D), lambda i: (i, col))


def _prev_halo(tm, col=0):
    return pl.BlockSpec((HALO, D), lambda i: (jnp.maximum(i * (tm // HALO) - 1, 0), col))


def _next_halo(tm, n_rows, col=0):
    last = n_rows // HALO - 1
    return pl.BlockSpec((HALO, D), lambda i: (jnp.minimum((i + 1) * (tm // HALO), last), col))


def _const(shape):
    return pl.BlockSpec(shape, lambda *_: (0,) * len(shape))


def _w_out_spec(which):
    return pl.BlockSpec((N_DEV, None, D // N_DEV, D), lambda *_: (0, which, 0, 0))


def _to_time_major(t_ref, row0, x):
    n = x.shape[0]
    for j in range(LANE_GROUPS):
        t_ref[pl.ds(row0 * LANE_GROUPS + j, n, stride=LANE_GROUPS), :] = x[:, j * LANES:(j + 1) * LANES]


def _from_time_major(t_ref, n):
    return jnp.concatenate([t_ref[pl.ds(j, n, stride=LANE_GROUPS), :] for j in range(LANE_GROUPS)], axis=-1)


def _row(ref, t):
    start = t * LANE_GROUPS
    if not isinstance(start, int):
        start = pl.multiple_of(start, LANE_GROUPS)
    return ref[pl.ds(start, LANE_GROUPS), :]


def _conv(o_ref, e_ref, w_ref, taps, n_rows, unroll, bias_ref=None):
    weights = [_row(w_ref, k) for k, _ in taps]

    def chunk(c, carry):
        t0 = c * unroll
        n_part = min(CONV_PARTIAL_SUMS, len(taps))
        acc = [[None] * n_part for _ in range(unroll)]
        for n, (_, off) in enumerate(taps):
            for u in range(unroll):
                term = weights[n] * _row(e_ref, t0 + off + u)
                acc[u][n % n_part] = term if acc[u][n % n_part] is None else acc[u][n % n_part] + term
        for u in range(unroll):
            parts = acc[u]
            while len(parts) > 1:
                parts = [parts[n] + parts[n + 1] for n in range(0, len(parts) - 1, 2)] + parts[len(parts) & ~1:]
            out = parts[0]
            if bias_ref is not None:
                out = out + bias_ref[...]
            o_ref[pl.ds(pl.multiple_of((t0 + u) * LANE_GROUPS, LANE_GROUPS), LANE_GROUPS), :] = out
        return carry

    lax.fori_loop(0, n_rows // unroll, chunk, 0)


def _conv_wgrad(dw_ref, d_ref, e_ref, taps, n_rows, unroll):
    for first in range(0, len(taps), WGRAD_TAPS):
        group = taps[first:first + WGRAD_TAPS]
        lo = min(off for _, off in group)
        hi = max(off for _, off in group)

        def chunk(c, accs, group=group, lo=lo, hi=hi):
            t0 = c * unroll
            d = [_row(d_ref, t0 + u) for u in range(unroll)]
            window = [_row(e_ref, t0 + lo + n) for n in range(hi - lo + unroll)]
            accs = list(accs)
            for n, (_, off) in enumerate(group):
                for u in range(unroll):
                    accs[n] = accs[n] + d[u] * window[off - lo + u]
            return tuple(accs)

        zeros = tuple(jnp.zeros((LANE_GROUPS, LANES), F32) for _ in group)
        accs = lax.fori_loop(0, n_rows // unroll, chunk, zeros)
        for (k, _), acc in zip(group, accs):
            dw_ref[pl.ds(k * LANE_GROUPS, LANE_GROUPS), :] += acc


FWD_TAPS_A = [(k, HALO - (K_A - 1) + k) for k in range(K_A)]
BWD_TAPS_A = [(k, K_A - 1 - k) for k in range(K_A)]
FWD_TAPS_B = [(k, HALO - (K_B - 1) + k) for k in range(K_B)]
BWD_TAPS_B = [(k, K_B - 1 - k) for k in range(K_B)]


def _time_major(n_rows):
    return pltpu.VMEM((n_rows * LANE_GROUPS, LANES), F32)


def _kv_fwd(mem, mem_g, wkv_g):
    m_len = mem.shape[0]

    def body(mem_ref, g_ref, w_ref, kv_ref, mn_ref):
        mf = mem_ref[...]
        r = lax.rsqrt(_mean(mf * mf) + EPS)
        mn = ((mf * r) * g_ref[...]).astype(BF16)
        mn_ref[...] = mn
        for b in range(2 * N_HEADS):
            kv_ref[b] = _dot(mn, w_ref[b]).astype(BF16)

    return pl.pallas_call(
        body, name="kv_fwd", grid=(1,),
        in_specs=[_const((m_len, D)), _const((1, D)), _const((2 * N_HEADS, D, HEAD_DIM))],
        out_specs=[_const((2 * N_HEADS, m_len, HEAD_DIM)), _const((m_len, D))],
        out_shape=[jax.ShapeDtypeStruct((2 * N_HEADS, m_len, HEAD_DIM), BF16), jax.ShapeDtypeStruct((m_len, D), BF16)],
        compiler_params=_params(1),
    )(mem, mem_g, wkv_g)


def _kv_bwd(dkv, mem, mem_g, mn16, wkv_g):
    m_len = mem.shape[0]

    def body(dkv_ref, mem_ref, g_ref, mn_ref, w_ref, dw_ref, dg_ref):
        mn = mn_ref[...]
        dmn = jnp.zeros((m_len, D), F32)
        for b in range(2 * N_HEADS):
            d16 = dkv_ref[b].astype(BF16)
            dw_ref[b] = _dot_tn(mn, d16)
            dmn = dmn + _dot_nt(d16, w_ref[b])
        mf = mem_ref[...]
        r = lax.rsqrt(_mean(mf * mf) + EPS)
        dg_ref[...] = _fold8(dmn * (mf * r))

    return pl.pallas_call(
        body, name="kv_bwd", grid=(1,),
        in_specs=[_const((2 * N_HEADS, m_len, HEAD_DIM)), _const((m_len, D)), _const((1, D)), _const((m_len, D)),
                  _const((2 * N_HEADS, D, HEAD_DIM))],
        out_specs=[_const((2 * N_HEADS, D, HEAD_DIM)), _const((SUBLANES, D))],
        out_shape=[jax.ShapeDtypeStruct((2 * N_HEADS, D, HEAD_DIM), F32), jax.ShapeDtypeStruct((SUBLANES, D), F32)],
        compiler_params=_params(1),
    )(dkv, mem, mem_g, mn16, wkv_g)


def _rmsnorm_fwd(x, norm_g, tm):
    n_rows = x.shape[0]

    def body(x_ref, g_ref, u_ref, ut_ref):
        xf = x_ref[...]
        u = (xf * lax.rsqrt(_mean(xf * xf) + EPS)) * g_ref[...]
        u_ref[...] = u.astype(BF16)
        ut_ref[...] = u.T.astype(BF16)

    return pl.pallas_call(
        body, name="rmsnorm_fwd", grid=(n_rows // tm,),
        in_specs=[_rows(tm), _const((1, D))],
        out_specs=[_rows(tm), pl.BlockSpec((D, tm), lambda i: (0, i))],
        out_shape=[jax.ShapeDtypeStruct((n_rows, D), BF16), jax.ShapeDtypeStruct((D, n_rows), BF16)],
        compiler_params=_params(1),
    )(x, norm_g)


def _place():
    x, y, c = lax.axis_index("x"), lax.axis_index("y"), lax.axis_index("c")
    other_chips = [(1 - x, y), (x, 1 - y), (1 - x, 1 - y)]
    return x, y, c, other_chips


def _arrival_order():
    x, y, c, chips = _place()
    order = [4 * x + 2 * y + c, 4 * x + 2 * y + 1 - c]
    for px, py in chips:
        order += [4 * px + 2 * py + c, 4 * px + 2 * py + 1 - c]
    return order


def _proj_fwd_gather(u16, blocks, tm):
    n = len(blocks)
    n_rows = u16.shape[0]
    n_i = n_rows // tm
    per_shard = W_IN_SHARD // UNIT
    assert n_i >= per_shard

    def wt_index(p, i, order):
        return (_dp_unit(per_shard * order[p] + jnp.minimum(i, per_shard - 1)), 0)

    def body(order_ref, u_ref, *refs):
        src, proj_ref, wt_ref, out = refs[:n], refs[n], refs[n + 1], refs[n + 2:2 * n + 2]
        wbuf, stage_sem, send, recv, own_sem = refs[2 * n + 2:]
        p, i = pl.program_id(0), pl.program_id(1)
        x, y, c, chips = _place()
        me, sibling = 4 * x + 2 * y + c, (x, y, 1 - c)

        def copy(t, k, block, to, from_input=False):
            return pltpu.make_async_remote_copy(
                src_ref=src[t] if from_input else out[t].at[block], dst_ref=out[t].at[block],
                send_sem=send.at[t, k], recv_sem=recv.at[t, k], device_id=to, device_id_type=MESH)

        def own_copies():
            return [pltpu.make_async_copy(src[t], out[t].at[me], own_sem.at[t]) for t in range(n)]

        def first_copies():
            first = []
            for t in range(n):
                first.append(copy(t, 0, me, sibling, from_input=True))
                first += [copy(t, 1 + j, me, (*chip, c), from_input=True) for j, chip in enumerate(chips)]
            return first

        def stage(slot, block):
            return pltpu.make_async_copy(out[0].at[block], wbuf.at[slot], stage_sem.at[slot])

        @pl.when((p == 0) & (i == 0))
        def _():
            for cp in own_copies() + first_copies():
                cp.start()
            mine = pltpu.make_async_copy(src[0], wbuf.at[0], stage_sem.at[0])
            mine.start()
            mine.wait()

        @pl.when((p > 0) & (i == 0))
        def _():
            stage(p % 2, order_ref[p]).wait()

        proj_ref[...] = _dot(u_ref[...], wbuf[p % 2]).astype(BF16)
        for r in range(per_shard):
            @pl.when(i == r)
            def _(r=r):
                wt_ref[...] = wbuf[p % 2, :, r * UNIT:(r + 1) * UNIT].astype(F32).T.astype(BF16)

        for nxt in range(1, N_DEV):
            @pl.when((p == nxt - 1) & (i == n_i - 1))
            def _(nxt=nxt):
                if nxt == 1:
                    block = 4 * x + 2 * y + 1 - c
                    copy(0, 0, block, sibling).wait_recv()
                else:
                    j, passed_on = divmod(nxt - 2, 2)
                    px, py = chips[j]
                    if passed_on:
                        block = 4 * px + 2 * py + 1 - c
                        copy(0, 4 + j, block, sibling).wait_recv()
                    else:
                        block = 4 * px + 2 * py + c
                        copy(0, 1 + j, block, sibling).wait_recv()
                        copy(0, 4 + j, block, sibling).start()
                stage(nxt % 2, block).start()

        @pl.when((p == N_DEV - 1) & (i == n_i - 1))
        def _():
            passed = [copy(0, 4 + j, 4 * px + 2 * py + c, sibling) for j, (px, py) in enumerate(chips)]
            for j, (px, py) in enumerate(chips):
                for t in range(1, n):
                    block = 4 * px + 2 * py + c
                    copy(t, 1 + j, block, sibling).wait_recv()
                    passed.append(copy(t, 4 + j, block, sibling))
                    passed[-1].start()
            for t in range(1, n):
                copy(t, 0, 4 * x + 2 * y + 1 - c, sibling).wait_recv()
                for j, (px, py) in enumerate(chips):
                    copy(t, 4 + j, 4 * px + 2 * py + 1 - c, sibling).wait_recv()
            for cp in first_copies() + passed:
                cp.wait_send()
            for cp in own_copies():
                cp.wait()

    return pl.pallas_call(
        body, name="proj_fwd_gather",
        grid_spec=pltpu.PrefetchScalarGridSpec(
            num_scalar_prefetch=1, grid=(N_DEV, n_i),
            in_specs=[pl.BlockSpec((tm, D), lambda p, i, order: (i, 0))] + [ANY] * n,
            out_specs=[pl.BlockSpec((tm, W_IN_SHARD), lambda p, i, order: (i, order[p])),
                       pl.BlockSpec((UNIT, D), wt_index)] + [ANY] * n,
            scratch_shapes=[pltpu.VMEM((2, D, W_IN_SHARD), BF16), pltpu.SemaphoreType.DMA((2,)),
                            pltpu.SemaphoreType.DMA((n, 7)), pltpu.SemaphoreType.DMA((n, 7)),
                            pltpu.SemaphoreType.DMA((n,))]),
        out_shape=[jax.ShapeDtypeStruct((n_rows, N_GROUPS * D), BF16), jax.ShapeDtypeStruct((N_GROUPS * D, D), BF16)]
        + [jax.ShapeDtypeStruct((N_DEV, *b.shape), b.dtype) for b in blocks],
        compiler_params=_params(2),
    )(jnp.stack(_arrival_order()).astype(jnp.int32), u16, *blocks)


def _branch_a_fwd(proj, wo4_g, cw_a, tm):
    n_rows = proj.shape[0]

    def body(bp, cp, xp, za, cph, xph, w_ref, cw_ref, sa_ref, ya_ref, e_scr, o_scr):
        i = pl.program_id(0)
        bp, cp, xp, za, cph, xph = (_f32(r) for r in (bp, cp, xp, za, cph, xph))
        _to_time_major(e_scr, 0, jnp.where(i > 0, cph[...] * xph[...], 0.0))
        _to_time_major(e_scr, HALO, cp[...] * xp[...])
        _conv(o_scr, e_scr, cw_ref, FWD_TAPS_A, tm, UNROLL_A)
        sa = (jax.nn.silu(za[...]) * (bp[...] * _from_time_major(o_scr, tm))).astype(BF16)
        sa_ref[...] = sa
        ya_ref[...] = _dot(sa, w_ref[...].reshape(D, D))

    return pl.pallas_call(
        body, name="branch_a_fwd", grid=(n_rows // tm,),
        in_specs=[_rows(tm, G_BA), _rows(tm, G_CA), _rows(tm, G_XA), _rows(tm, G_ZA),
                  _prev_halo(tm, G_CA), _prev_halo(tm, G_XA), _w_out_spec(0), _const(cw_a.shape)],
        out_specs=[_rows(tm), _rows(tm)],
        out_shape=[jax.ShapeDtypeStruct((n_rows, D), BF16), jax.ShapeDtypeStruct((n_rows, D), F32)],
        scratch_shapes=[_time_major(tm + HALO), _time_major(tm)],
        compiler_params=_params(1),
    )(proj, proj, proj, proj, proj, proj, wo4_g, cw_a)


def _layernorm_parts(cb, lg, lb):
    xc = cb - _mean(cb)
    rstd = lax.rsqrt(_mean(xc * xc) + EPS)
    xhat = xc * rstd
    return xhat, rstd, xhat * lg + lb


def _branch_b_fwd(proj, wo4_g, cw_b, conv_b_b, ln_g, ln_b, tm):
    n_rows = proj.shape[0]

    def body(vb, gb, zb, vbh, gbh, w_ref, cw_ref, bb_ref, lg_ref, lb_ref, cb_ref, sb_ref, yb_ref, e_scr, o_scr):
        i = pl.program_id(0)
        vb, gb, zb, vbh, gbh = (_f32(r) for r in (vb, gb, zb, vbh, gbh))
        _to_time_major(e_scr, 0, jnp.where(i > 0, vbh[...] * jax.nn.sigmoid(gbh[...]), 0.0))
        _to_time_major(e_scr, HALO, vb[...] * jax.nn.sigmoid(gb[...]))
        _conv(o_scr, e_scr, cw_ref, FWD_TAPS_B, tm, UNROLL_B, bias_ref=bb_ref)
        cb = _from_time_major(o_scr, tm)
        cb_ref[...] = cb
        _, _, ln = _layernorm_parts(cb, lg_ref[...], lb_ref[...])
        sb = (jax.nn.silu(zb[...]) * jax.nn.silu(ln)).astype(BF16)
        sb_ref[...] = sb
        yb_ref[...] = _dot(sb, w_ref[...].reshape(D, D))

    return pl.pallas_call(
        body, name="branch_b_fwd", grid=(n_rows // tm,),
        in_specs=[_rows(tm, G_VB), _rows(tm, G_GB), _rows(tm, G_ZB), _prev_halo(tm, G_VB), _prev_halo(tm, G_GB),
                  _w_out_spec(1), _const(cw_b.shape), _const((LANE_GROUPS, LANES)), _const((1, D)), _const((1, D))],
        out_specs=[_rows(tm), _rows(tm), _rows(tm)],
        out_shape=[jax.ShapeDtypeStruct((n_rows, D), F32), jax.ShapeDtypeStruct((n_rows, D), BF16),
                   jax.ShapeDtypeStruct((n_rows, D), F32)],
        scratch_shapes=[_time_major(tm + HALO), _time_major(tm)],
        compiler_params=_params(1),
    )(proj, proj, proj, proj, proj, wo4_g, cw_b, conv_b_b.reshape(LANE_GROUPS, LANES), ln_g, ln_b)


def _attention(q16, kv_ref):
    probs, outs = [], []
    for h in range(N_HEADS):
        s = _dot_nt(q16[:, h * HEAD_DIM:(h + 1) * HEAD_DIM], kv_ref[h]) * (HEAD_DIM ** -0.5)
        e = jnp.exp(s - jnp.max(s, axis=-1, keepdims=True))
        p = e / jnp.sum(e, axis=-1, keepdims=True)
        probs.append(p)
        outs.append(_dot(p.astype(BF16), kv_ref[N_HEADS + h]))
    return probs, outs


def _branch_x_fwd(proj, kv16, wo4_g, tm):
    n_rows = proj.shape[0]

    def body(q, zx, kv_ref, w_ref, sx_ref, yx_ref):
        _, outs = _attention(q[...], kv_ref)
        sx = (jax.nn.silu(_f32(zx)) * jnp.concatenate(outs, axis=-1)).astype(BF16)
        sx_ref[...] = sx
        yx_ref[...] = _dot(sx, w_ref[...].reshape(D, D))

    return pl.pallas_call(
        body, name="branch_x_fwd", grid=(n_rows // tm,),
        in_specs=[_rows(tm, G_Q), _rows(tm, G_ZX), _const(kv16.shape), _w_out_spec(2)],
        out_specs=[_rows(tm), _rows(tm)],
        out_shape=[jax.ShapeDtypeStruct((n_rows, D), BF16), jax.ShapeDtypeStruct((n_rows, D), F32)],
        compiler_params=_params(1),
    )(proj, proj, kv16, wo4_g)


def _merge_fwd_bwd(proj, ya, yb, yx, x, target, wo4_g, final_g, tm):
    n_rows = proj.shape[0]
    inv_d = 1.0 / D

    def body(ga, gb, gx, ya_ref, yb_ref, yx_ref, x_ref, t_ref, w_ref, fg_ref,
             dh_ref, dya_ref, dyb_ref, dyx_ref, dp_ref, dw_ref, dfg_ref, sq_ref):
        i = pl.program_id(0)
        wo = w_ref[...].reshape(D, D)
        sig = [jax.nn.sigmoid(_f32(g)) for g in (ga, gb, gx)]
        ys = [ya_ref[...], yb_ref[...], yx_ref[...]]
        m16 = (sig[0] * ys[0] + sig[1] * ys[1] + sig[2] * ys[2]).astype(BF16)
        h = x_ref[...] + _dot(m16, wo)
        r = lax.rsqrt(_mean(h * h) + EPS)
        hn = h * r
        fg = fg_ref[...]
        err = hn * fg - t_ref[...]
        dy = err * inv_d
        dhn = dy * fg
        dh = r * (dhn - hn * _mean(dhn * hn))
        dh_ref[...] = dh
        dh16 = dh.astype(BF16)
        dm = _dot_nt(dh16, wo)
        for n, out in enumerate((dya_ref, dyb_ref, dyx_ref)):
            out[...] = (sig[n] * dm).astype(BF16)
            dp_ref[:, n * D:(n + 1) * D] = (dm * ys[n] * (sig[n] * (1.0 - sig[n]))).astype(BF16)

        @pl.when(i == 0)
        def _():
            dw_ref[...] = jnp.zeros_like(dw_ref)
            dfg_ref[...] = jnp.zeros_like(dfg_ref)
            sq_ref[...] = jnp.zeros_like(sq_ref)

        dw_ref[0] += _dot_tn(m16, dh16)
        dfg_ref[...] += _fold8(dy * hn)
        sq_ref[...] += _fold8(err * err)

    vec = jax.ShapeDtypeStruct((SUBLANES, D), F32)
    return pl.pallas_call(
        body, name="merge_fwd_bwd", grid=(n_rows // tm,),
        in_specs=[_rows(tm, G_GA), _rows(tm, G_GBB), _rows(tm, G_GX), _rows(tm), _rows(tm), _rows(tm), _rows(tm),
                  _rows(tm), _w_out_spec(3), _const((1, D))],
        out_specs=[_rows(tm), _rows(tm), _rows(tm), _rows(tm), pl.BlockSpec((tm, 3 * D), lambda i: (i, 3)),
                   pl.BlockSpec((1, D, D), lambda i: (3, 0, 0)), _const((SUBLANES, D)), _const((SUBLANES, D))],
        out_shape=[jax.ShapeDtypeStruct((n_rows, D), F32), jax.ShapeDtypeStruct((n_rows, D), BF16),
                   jax.ShapeDtypeStruct((n_rows, D), BF16), jax.ShapeDtypeStruct((n_rows, D), BF16),
                   jax.ShapeDtypeStruct((n_rows, N_GROUPS * D), BF16), jax.ShapeDtypeStruct((4, D, D), F32), vec, vec],
        compiler_params=_params(1),
    )(proj, proj, proj, ya, yb, yx, x, target, wo4_g, final_g)


def _branch_a_bwd(dya, proj, sa16, wo4_g, cw_a, dproj, dw4, tm):
    n_rows = proj.shape[0]
    n_tiles = n_rows // tm

    def body(dya_ref, bp, cp, xp, za, sa_ref, dyan, bpn, zan, cph, xph, w_ref, cw_ref, dp_in, dw_in,
             dp_ref, dw_ref, dwa_ref, e1, e2, o_scr):
        del dp_in, dw_in
        i = pl.program_id(0)
        bp, cp, xp, za, bpn, zan, cph, xph = (_f32(r) for r in (bp, cp, xp, za, bpn, zan, cph, xph))
        woa = w_ref[...].reshape(D, D)
        dya16 = dya_ref[...]
        _to_time_major(e1, 0, jnp.where(i > 0, cph[...] * xph[...], 0.0))
        _to_time_major(e1, HALO, cp[...] * xp[...])
        _conv(o_scr, e1, cw_ref, FWD_TAPS_A, tm, UNROLL_A)
        ca = _from_time_major(o_scr, tm)
        dsa = _dot_nt(dya16, woa)
        silu_z, dsilu_z = _silu_and_grad(za[...])
        t = dsa * silu_z
        dp_ref[:, 0 * D:1 * D] = (t * ca).astype(BF16)
        dp_ref[:, 3 * D:4 * D] = (dsa * (bp[...] * ca) * dsilu_z).astype(BF16)
        dsan = _dot_nt(dyan[...], woa)
        dcan = (dsan * jax.nn.silu(zan[...])) * bpn[...]
        _to_time_major(e2, 0, t * bp[...])
        _to_time_major(e2, tm, jnp.where(i < n_tiles - 1, dcan, 0.0))

        @pl.when(i == 0)
        def _():
            dw_ref[...] = jnp.zeros_like(dw_ref)
            dwa_ref[...] = jnp.zeros_like(dwa_ref)

        _conv_wgrad(dwa_ref, e2, e1, FWD_TAPS_A, tm, UNROLL_A)
        dw_ref[0] += _dot_tn(sa_ref[...], dya16)
        _conv(o_scr, e2, cw_ref, BWD_TAPS_A, tm, UNROLL_A)
        dprod = _from_time_major(o_scr, tm)
        dp_ref[:, 1 * D:2 * D] = (dprod * xp[...]).astype(BF16)
        dp_ref[:, 2 * D:3 * D] = (dprod * cp[...]).astype(BF16)

    return pl.pallas_call(
        body, name="branch_a_bwd", grid=(n_tiles,),
        in_specs=[_rows(tm), _rows(tm, G_BA), _rows(tm, G_CA), _rows(tm, G_XA), _rows(tm, G_ZA), _rows(tm),
                  _next_halo(tm, n_rows), _next_halo(tm, n_rows, G_BA), _next_halo(tm, n_rows, G_ZA),
                  _prev_halo(tm, G_CA), _prev_halo(tm, G_XA), _w_out_spec(0), _const(cw_a.shape), ANY, ANY],
        out_specs=[pl.BlockSpec((tm, 4 * D), lambda i: (i, 0)), pl.BlockSpec((1, D, D), lambda i: (0, 0, 0)),
                   _const((K_A * LANE_GROUPS, LANES))],
        out_shape=[jax.ShapeDtypeStruct(dproj.shape, BF16), jax.ShapeDtypeStruct(dw4.shape, F32),
                   jax.ShapeDtypeStruct((K_A * LANE_GROUPS, LANES), F32)],
        input_output_aliases={13: 0, 14: 1},
        scratch_shapes=[_time_major(tm + HALO), _time_major(tm + HALO), _time_major(tm)],
        compiler_params=_params(1),
    )(dya, proj, proj, proj, proj, sa16, dya, proj, proj, proj, proj, wo4_g, cw_a, dproj, dw4)


def _branch_b_bwd(dyb, proj, cb, sb16, wo4_g, cw_b, ln_g, ln_b, dproj, dw4, tm):
    n_rows = proj.shape[0]
    n_tiles = n_rows // tm

    def body(dyb_ref, zb, cb_ref, vb, gb, sb_ref, dybn, zbn, cbn, vbh, gbh, w_ref, cw_ref, lg_ref, lb_ref,
             dp_in, dw_in, dp_ref, dw_ref, dwb_ref, dbb_ref, dlg_ref, dlb_ref, e1, e2, o_scr):
        del dp_in, dw_in
        zb, vb, gb, zbn, vbh, gbh = (_f32(r) for r in (zb, vb, gb, zbn, vbh, gbh))
        i = pl.program_id(0)
        wob = w_ref[...].reshape(D, D)
        lg, lb = lg_ref[...], lb_ref[...]

        def conv_out_grad(dy16, z, c):
            dsb = _dot_nt(dy16, wob)
            xhat, rstd, ln = _layernorm_parts(c, lg, lb)
            sw, dsw = _silu_and_grad(ln)
            sz, dsz = _silu_and_grad(z)
            dln = (dsb * sz) * dsw
            dxhat = dln * lg
            dcb = rstd * (dxhat - _mean(dxhat) - xhat * _mean(dxhat * xhat))
            return dsb * sw * dsz, dln, xhat, dcb

        dyb16 = dyb_ref[...]
        dzb, dln, xhat, dcb = conv_out_grad(dyb16, zb[...], cb_ref[...])
        dp_ref[:, 2 * D:3 * D] = dzb.astype(BF16)
        _, _, _, dcbn = conv_out_grad(dybn[...], zbn[...], cbn[...])
        _to_time_major(e2, 0, dcb)
        _to_time_major(e2, tm, jnp.where(i < n_tiles - 1, dcbn, 0.0))

        @pl.when(i == 0)
        def _():
            dw_ref[...] = jnp.zeros_like(dw_ref)
            dwb_ref[...] = jnp.zeros_like(dwb_ref)
            dbb_ref[...] = jnp.zeros_like(dbb_ref)
            dlg_ref[...] = jnp.zeros_like(dlg_ref)
            dlb_ref[...] = jnp.zeros_like(dlb_ref)

        dlg_ref[...] += _fold8(dln * xhat)
        dlb_ref[...] += _fold8(dln)
        dbb_ref[...] += _fold8(dcb)
        dw_ref[0] += _dot_tn(sb_ref[...], dyb16)
        sg = jax.nn.sigmoid(gb[...])
        _to_time_major(e1, 0, jnp.where(i > 0, vbh[...] * jax.nn.sigmoid(gbh[...]), 0.0))
        _to_time_major(e1, HALO, vb[...] * sg)
        _conv_wgrad(dwb_ref, e2, e1, FWD_TAPS_B, tm, UNROLL_WGRAD_B)
        _conv(o_scr, e2, cw_ref, BWD_TAPS_B, tm, UNROLL_B)
        dglu = _from_time_major(o_scr, tm)
        dp_ref[:, 0 * D:1 * D] = (dglu * sg).astype(BF16)
        dp_ref[:, 1 * D:2 * D] = (dglu * vb[...] * (sg * (1.0 - sg))).astype(BF16)

    vec = jax.ShapeDtypeStruct((SUBLANES, D), F32)
    return pl.pallas_call(
        body, name="branch_b_bwd", grid=(n_tiles,),
        in_specs=[_rows(tm), _rows(tm, G_ZB), _rows(tm), _rows(tm, G_VB), _rows(tm, G_GB), _rows(tm),
                  _next_halo(tm, n_rows), _next_halo(tm, n_rows, G_ZB), _next_halo(tm, n_rows),
                  _prev_halo(tm, G_VB), _prev_halo(tm, G_GB), _w_out_spec(1), _const(cw_b.shape), _const((1, D)),
                  _const((1, D)), ANY, ANY],
        out_specs=[pl.BlockSpec((tm, 3 * D), lambda i: (i, 2)), pl.BlockSpec((1, D, D), lambda i: (1, 0, 0)),
                   _const((K_B * LANE_GROUPS, LANES)), _const((SUBLANES, D)), _const((SUBLANES, D)),
                   _const((SUBLANES, D))],
        out_shape=[jax.ShapeDtypeStruct(dproj.shape, BF16), jax.ShapeDtypeStruct(dw4.shape, F32),
                   jax.ShapeDtypeStruct((K_B * LANE_GROUPS, LANES), F32), vec, vec, vec],
        input_output_aliases={15: 0, 16: 1},
        scratch_shapes=[_time_major(tm + HALO), _time_major(tm + HALO), _time_major(tm)],
        compiler_params=_params(1),
    )(dyb, proj, cb, proj, proj, sb16, dyb, proj, cb, proj, proj, wo4_g, cw_b, ln_g, ln_b, dproj, dw4)


def _branch_x_bwd(dyx, proj, sx16, kv16, wo4_g, dproj, dw4, tm):
    n_rows = proj.shape[0]
    scale = HEAD_DIM ** -0.5

    def body(dyx_ref, q, zx, sx_ref, kv_ref, w_ref, dp_in, dw_in, dp_ref, dw_ref, dkv_ref):
        del dp_in, dw_in
        i = pl.program_id(0)
        dyx16 = dyx_ref[...]
        q16 = q[...]
        probs, outs = _attention(q16, kv_ref)
        dsx = _dot_nt(dyx16, w_ref[...].reshape(D, D))
        silu_z, dsilu_z = _silu_and_grad(_f32(zx))
        dp_ref[:, D:2 * D] = (dsx * jnp.concatenate(outs, axis=-1) * dsilu_z).astype(BF16)
        do16 = (dsx * silu_z).astype(BF16)

        @pl.when(i == 0)
        def _():
            dw_ref[...] = jnp.zeros_like(dw_ref)
            dkv_ref[...] = jnp.zeros_like(dkv_ref)

        for h in range(N_HEADS):
            cols = slice(h * HEAD_DIM, (h + 1) * HEAD_DIM)
            p = probs[h]
            dprob = _dot_nt(do16[:, cols], kv_ref[N_HEADS + h])
            ds16 = ((p * (dprob - jnp.sum(p * dprob, axis=-1, keepdims=True))) * scale).astype(BF16)
            dp_ref[:, cols] = _dot(ds16, kv_ref[h]).astype(BF16)
            dkv_ref[h] += _dot_tn(ds16, q16[:, cols])
            dkv_ref[N_HEADS + h] += _dot_tn(p.astype(BF16), do16[:, cols])
        dw_ref[0] += _dot_tn(sx_ref[...], dyx16)

    return pl.pallas_call(
        body, name="branch_x_bwd", grid=(n_rows // tm,),
        in_specs=[_rows(tm), _rows(tm, G_Q), _rows(tm, G_ZX), _rows(tm), _const(kv16.shape), _w_out_spec(2), ANY, ANY],
        out_specs=[pl.BlockSpec((tm, 2 * D), lambda i: (i, 2)), pl.BlockSpec((1, D, D), lambda i: (2, 0, 0)),
                   _const(kv16.shape)],
        out_shape=[jax.ShapeDtypeStruct(dproj.shape, BF16), jax.ShapeDtypeStruct(dw4.shape, F32),
                   jax.ShapeDtypeStruct(kv16.shape, F32)],
        input_output_aliases={6: 0, 7: 1},
        compiler_params=_params(1),
    )(dyx, proj, proj, sx16, kv16, wo4_g, dproj, dw4)


def _dp_unit(u):
    g = u // 2
    pos = jnp.where(g < G_VB, g, jnp.where(g < G_Q, g + 2, jnp.where(g < G_GA, g - 3, g)))
    return 2 * pos + u % 2


def _scatter_copies(srcs, lands, send, recv):
    x, y, c = lax.axis_index("x"), lax.axis_index("y"), lax.axis_index("c")
    copies = []
    for n in range(N_DEV - 1):
        flip = n + 1
        px = 1 - x if flip & 4 else x
        py = 1 - y if flip & 2 else y
        pc = 1 - c if flip & 1 else c
        for t, (src, land) in enumerate(zip(srcs, lands)):
            copies.append(pltpu.make_async_remote_copy(
                src_ref=src.at[4 * px + 2 * py + pc], dst_ref=land.at[n], send_sem=send.at[t, n],
                recv_sem=recv.at[t, n], device_id=(px, py, pc), device_id_type=MESH))
    return copies


def _scatter_sems(n_tensors):
    return [pltpu.SemaphoreType.DMA((n_tensors, N_DEV - 1)), pltpu.SemaphoreType.DMA((n_tensors, N_DEV - 1))]


def _landing(shard_shape, lead=()):
    return jax.ShapeDtypeStruct((*lead, N_DEV - 1, *shard_shape), BF16)


def _w_in_grad(ut, dproj, tk, dw4_16, dwkv16):
    n_rows = dproj.shape[0]
    n_k = n_rows // tk
    per_shard = W_IN_SHARD // UNIT

    def body(ut_ref, dp0, dp1, dp2, dw4_ref, dwkv_ref, out_ref, out16_ref, l4_ref, lkv_ref, acc, send, recv):
        q, t = pl.program_id(0), pl.program_id(1)

        def small_copies():
            return _scatter_copies([dw4_ref.at[w] for w in range(4)] + [dwkv_ref],
                                   [l4_ref.at[w] for w in range(4)] + [lkv_ref], send, recv)

        @pl.when((q == 0) & (t == 0))
        def _():
            for cp in small_copies():
                cp.start()

        for r, dp_ref in enumerate((dp0, dp1, dp2)):
            cols = slice(r * UNIT, (r + 1) * UNIT)

            @pl.when(t == 0)
            def _(dp_ref=dp_ref, cols=cols):
                acc[:, cols] = _dot(ut_ref[...], dp_ref[...])

            @pl.when(t > 0)
            def _(dp_ref=dp_ref, cols=cols):
                acc[:, cols] += _dot(ut_ref[...], dp_ref[...])

        @pl.when(t == n_k - 1)
        def _():
            out_ref[0] = acc[...]
            out16_ref[0] = acc[...].astype(BF16)

        @pl.when((q == N_DEV - 1) & (t == n_k - 1))
        def _():
            for cp in small_copies():
                cp.wait()

    def dp_spec(r):
        return pl.BlockSpec((tk, UNIT), lambda q, t: (t, _dp_unit(per_shard * q + r)))

    shard = pl.BlockSpec((1, D, W_IN_SHARD), lambda q, t: (q, 0, 0))
    return pl.pallas_call(
        body, name="w_in_grad", grid=(N_DEV, n_k),
        in_specs=[pl.BlockSpec((D, tk), lambda q, t: (0, t)), dp_spec(0), dp_spec(1), dp_spec(2), ANY, ANY],
        out_specs=[shard, shard, ANY, ANY],
        out_shape=[jax.ShapeDtypeStruct((N_DEV, D, W_IN_SHARD), F32), jax.ShapeDtypeStruct((N_DEV, D, W_IN_SHARD), BF16),
                   _landing(dw4_16.shape[2:], lead=(4,)), _landing(dwkv16.shape[1:])],
        scratch_shapes=[pltpu.VMEM((D, W_IN_SHARD), F32)] + _scatter_sems(5),
        compiler_params=_params(2),
    )(ut, dproj, dproj, dproj, dw4_16, dwkv16)


def _x_grad(dproj, win_t, x, dh, norm_g, dwin16, tm):
    n_rows = x.shape[0]
    n_tiles = n_rows // tm
    n_k = N_GROUPS * D // X_GRAD_K

    def body(dp_ref, wt_ref, x_ref, dh_ref, g_ref, dwin_ref, gx_ref, dg_ref, land_ref, acc, send, recv):
        i, g = pl.program_id(0), pl.program_id(1)

        @pl.when((i == 0) & (g == 0))
        def _():
            for cp in _scatter_copies([dwin_ref], [land_ref], send, recv):
                cp.start()
            dg_ref[...] = jnp.zeros_like(dg_ref)

        @pl.when(g == 0)
        def _():
            acc[...] = _dot(dp_ref[...], wt_ref[...])

        @pl.when(g > 0)
        def _():
            acc[...] += _dot(dp_ref[...], wt_ref[...])

        @pl.when(g == n_k - 1)
        def _():
            du = acc[...]
            xf = x_ref[...]
            r = lax.rsqrt(_mean(xf * xf) + EPS)
            xn = xf * r
            dun = du * g_ref[...]
            gx_ref[...] = dh_ref[...] + r * (dun - xn * _mean(dun * xn))
            dg_ref[...] += _fold8(du * xn)

        @pl.when((i == n_tiles - 1) & (g == n_k - 1))
        def _():
            for cp in _scatter_copies([dwin_ref], [land_ref], send, recv):
                cp.wait()

    return pl.pallas_call(
        body, name="x_grad", grid=(n_tiles, n_k),
        in_specs=[pl.BlockSpec((tm, X_GRAD_K), lambda i, g: (i, g)), pl.BlockSpec((X_GRAD_K, D), lambda i, g: (g, 0)),
                  pl.BlockSpec((tm, D), lambda i, g: (i, 0)), pl.BlockSpec((tm, D), lambda i, g: (i, 0)),
                  _const((1, D)), ANY],
        out_specs=[pl.BlockSpec((tm, D), lambda i, g: (i, 0)), _const((SUBLANES, D)), ANY],
        out_shape=[jax.ShapeDtypeStruct((n_rows, D), F32), jax.ShapeDtypeStruct((SUBLANES, D), F32),
                   _landing(dwin16.shape[1:])],
        scratch_shapes=[pltpu.VMEM((tm, D), F32)] + _scatter_sems(1),
        compiler_params=_params(2),
    )(dproj, win_t, x, dh, norm_g, dwin16)


def _local_step(x, mem, target, norm_g, conv_b_b, ln_g, ln_b, mem_g, final_g, shards):
    n_rows = x.shape[0]
    tm = min(512, n_rows)
    big = min(1024, n_rows)
    u16, ut = _rmsnorm_fwd(x, norm_g, big)
    proj, win_t, _, wkv_g, wo4_g, cw_g = _proj_fwd_gather(u16, shards, big)
    cw_rows = cw_g.transpose(1, 0, 2).reshape((SUBLANES + HALO) * LANE_GROUPS, LANES)
    cw_a, cw_b = cw_rows[:SUBLANES * LANE_GROUPS], cw_rows[SUBLANES * LANE_GROUPS:]
    kv16, mn16 = _kv_fwd(mem, mem_g, wkv_g)
    sa16, ya = _branch_a_fwd(proj, wo4_g, cw_a, tm)
    cb, sb16, yb = _branch_b_fwd(proj, wo4_g, cw_b, conv_b_b, ln_g, ln_b, tm)
    sx16, yx = _branch_x_fwd(proj, kv16, wo4_g, tm)
    dh, dya, dyb, dyx, dproj, dw4, dfg, sq = _merge_fwd_bwd(proj, ya, yb, yx, x, target, wo4_g, final_g,
                                                             min(256, n_rows))
    dproj, dw4, dwa = _branch_a_bwd(dya, proj, sa16, wo4_g, cw_a, dproj, dw4, tm)
    dproj, dw4, dwb, dbb, dlg, dlb = _branch_b_bwd(dyb, proj, cb, sb16, wo4_g, cw_b, ln_g, ln_b, dproj, dw4, tm)
    dproj, dw4, dkv = _branch_x_bwd(dyx, proj, sx16, kv16, wo4_g, dproj, dw4, tm)
    dwkv_g, dmg = _kv_bwd(dkv, mem, mem_g, mn16, wkv_g)
    dw4 = dw4.reshape(4, N_DEV, D // N_DEV, D)
    dwin_g, dwin16, land4, landkv = _w_in_grad(ut, dproj, min(2048, n_rows), dw4.astype(BF16), dwkv_g.astype(BF16))
    gx, dng, landin = _x_grad(dproj, win_t, x, dh, norm_g, dwin16, big)
    small = {SV_NORM_G: dng, SV_CONV_B_B: dbb, SV_LN_G: dlg, SV_LN_B: dlb, SV_MEM_G: dmg, SV_FINAL_G: dfg, SV_LOSS: sq}
    grads = [(dwin_g[None], landin[None]), (dw4, land4), (dwkv_g[None], landkv[None])]
    return gx, grads, small, dwa.reshape(K_A, D), dwb.reshape(K_B, D)


def _allgather_small(small, conv_rows):
    keys = sorted(small)

    def body(*refs):
        parts, (conv_ref, out_ref, mine, send, recv) = refs[:len(keys)], refs[len(keys):]
        x, y, c, chips = _place()
        me, sibling = 4 * x + 2 * y + c, (x, y, 1 - c)
        mine[pl.ds(0, SV_CONV_A), :] = jnp.zeros((SV_CONV_A, D), F32)
        for key, part in zip(keys, parts):
            mine[key:key + 1, :] = jnp.sum(part[...], axis=0, keepdims=True)
        mine[pl.ds(SV_CONV_A, SV_ROWS - SV_CONV_A), :] = conv_ref[...]
        out_ref[me] = mine[...]

        def copy(k, block, to, from_mine=False):
            return pltpu.make_async_remote_copy(
                src_ref=mine if from_mine else out_ref.at[block], dst_ref=out_ref.at[block],
                send_sem=send.at[k], recv_sem=recv.at[k], device_id=to, device_id_type=MESH)

        first = [copy(0, me, sibling, from_mine=True)]
        first += [copy(1 + j, me, (*chip, c), from_mine=True) for j, chip in enumerate(chips)]
        for cp in first:
            cp.start()
        passed = []
        for j, (px, py) in enumerate(chips):
            block = 4 * px + 2 * py + c
            copy(1 + j, block, sibling).wait_recv()
            passed.append(copy(4 + j, block, sibling))
            passed[-1].start()
        copy(0, 4 * x + 2 * y + 1 - c, sibling).wait_recv()
        for j, (px, py) in enumerate(chips):
            copy(4 + j, 4 * px + 2 * py + 1 - c, sibling).wait_recv()
        for cp in first + passed:
            cp.wait_send()

    vmem = pl.BlockSpec(memory_space=pltpu.VMEM)
    return pl.pallas_call(
        body, name="allgather_small",
        in_specs=[vmem] * (len(keys) + 1), out_specs=vmem,
        out_shape=jax.ShapeDtypeStruct((N_DEV, SV_ROWS, D), F32),
        scratch_shapes=[pltpu.VMEM((SV_ROWS, D), F32), pltpu.SemaphoreType.DMA((7,)), pltpu.SemaphoreType.DMA((7,))],
    )(*[small[k] for k in keys], conv_rows)


def _adamw(w, g, m, v):
    m = ADAM_B1 * m + (1.0 - ADAM_B1) * g
    v = ADAM_B2 * v + (1.0 - ADAM_B2) * (g * g)
    m_hat = m / (1.0 - ADAM_B1 ** ADAM_STEP)
    v_hat = v / (1.0 - ADAM_B2 ** ADAM_STEP)
    return -ADAM_LR * (m_hat / (jnp.sqrt(v_hat) + ADAM_EPS) + ADAM_WD * w), m, v


def _adamw_shard(own, landed, piece, k_arr, w, m, v, tr):
    n_r, n_c = w.shape
    n_landed = landed.shape[1]

    def body(k_ref, own_ref, *refs):
        del k_ref
        landed_refs, (w_ref, m_ref, v_ref, g_out, d_out, m_out, v_out) = refs[:n_landed], refs[n_landed:]
        g = own_ref[0, 0]
        for landed_ref in landed_refs:
            g = g + landed_ref[0, 0].astype(F32)
        g_out[...] = g
        d_out[...], m_out[...], v_out[...] = _adamw(w_ref[...], g, m_ref[...], v_ref[...])

    blk = (1, 1, tr, n_c)
    flat = pl.BlockSpec((tr, n_c), lambda r, k: (r, 0))
    return pl.pallas_call(
        body, name="adamw_shard",
        grid_spec=pltpu.PrefetchScalarGridSpec(
            num_scalar_prefetch=1, grid=(n_r // tr,),
            in_specs=[pl.BlockSpec(blk, lambda r, k: (piece, k[0], r, 0))]
            + [pl.BlockSpec(blk, functools.partial(lambda r, k, j: (piece, j, r, 0), j=j)) for j in range(n_landed)]
            + [flat] * 3,
            out_specs=[flat] * 4),
        out_shape=[jax.ShapeDtypeStruct((n_r, n_c), F32)] * 4,
        compiler_params=_params(1),
    )(k_arr, own, *([landed] * n_landed), w, m, v)


def _adamw_small(gathered, k_arr, vectors, conv_a, conv_b):
    n_vec = len(vectors)
    cols = D // N_DEV

    def body(k_ref, full_ref, cols_ref, *refs):
        del k_ref
        ins, outs = refs[:3 * (n_vec + 2)], refs[3 * (n_vec + 2):-2]
        tot, tot_cols = refs[-2:]
        tot[...] = full_ref[0]
        tot_cols[...] = cols_ref[0]
        for dev in range(1, N_DEV):
            tot[...] += full_ref[dev]
            tot_cols[...] += cols_ref[dev]
        loss = (0.5 / D) * jnp.sum(tot[SV_LOSS:SV_LOSS + 1, :])
        outs[0][...] = jnp.full(outs[0].shape, loss, F32)
        grads = [tot[n:n + 1, :] for n in range(n_vec)]
        grads += [tot_cols[pl.ds(SV_CONV_A, K_A), :], tot_cols[pl.ds(SV_CONV_B, K_B), :]]
        for n, g in enumerate(grads):
            w_ref, m_ref, v_ref = ins[3 * n:3 * n + 3]
            g_out, d_out, m_out, v_out = outs[1 + 4 * n:5 + 4 * n]
            g_out[...] = g
            d_out[...], m_out[...], v_out[...] = _adamw(w_ref[...], g, m_ref[...], v_ref[...])

    weights = list(vectors) + [conv_a, conv_b]
    flat_in = [a for wmv in weights for a in wmv]
    out_shape = [jax.ShapeDtypeStruct((SUBLANES, 128), F32)]
    for wmv in weights:
        out_shape += [jax.ShapeDtypeStruct(wmv[0].shape, F32)] * 4
    return pl.pallas_call(
        body, name="adamw_small",
        grid_spec=pltpu.PrefetchScalarGridSpec(
            num_scalar_prefetch=1, grid=(1,),
            in_specs=[pl.BlockSpec((N_DEV, SV_ROWS, D), lambda i, k: (0, 0, 0)),
                      pl.BlockSpec((N_DEV, SV_ROWS, cols), lambda i, k: (0, 0, k[0]))]
            + [pl.BlockSpec(a.shape, lambda i, k: (0, 0)) for a in flat_in],
            out_specs=[pl.BlockSpec(s.shape, lambda i, k: (0, 0)) for s in out_shape],
            scratch_shapes=[pltpu.VMEM((SV_ROWS, D), F32), pltpu.VMEM((SV_ROWS, cols), F32)]),
        out_shape=out_shape,
        compiler_params=_params(1),
    )(k_arr, gathered, gathered, *flat_in)


def kernel(x, mem, norm_g, w_in, conv_a_w, w_out_a, conv_b_w, conv_b_b, ln_b_g, ln_b_b, w_out_b, mem_norm_g, w_kv, w_out_x, w_o, final_g, loss_target, m_norm_g, m_w_in, m_conv_a_w, m_w_out_a, m_conv_b_w, m_conv_b_b, m_ln_b_g, m_ln_b_b, m_w_out_b, m_mem_norm_g, m_w_kv, m_w_out_x, m_w_o, m_final_g, v_norm_g, v_w_in, v_conv_a_w, v_w_out_a, v_conv_b_w, v_conv_b_b, v_ln_b_g, v_ln_b_b, v_w_out_b, v_mem_norm_g, v_w_kv, v_w_out_x, v_w_o, v_final_g):
    xi, yi, ci = lax.axis_index("x"), lax.axis_index("y"), lax.axis_index("c")
    k_arr = jnp.reshape(4 * xi + 2 * yi + ci, (1,)).astype(jnp.int32)

    cw = jnp.concatenate([jnp.pad(conv_a_w[0], ((0, SUBLANES - K_A), (0, 0))),
                          jnp.pad(conv_b_w[0], ((0, HALO - K_B), (0, 0)))], axis=0)
    wo4 = jnp.stack([w_out_a[0], w_out_b[0], w_out_x[0], w_o[0]]).astype(BF16)
    shards = [w_in[0].astype(BF16), w_kv[0].astype(BF16), wo4, cw]

    final_g2 = final_g.reshape(1, D)
    gx, grads, small, dwa, dwb = _local_step(
        x[0], mem[0], loss_target[0], norm_g, conv_b_b, ln_b_g, ln_b_b, mem_norm_g, final_g2, shards)

    conv_rows = jnp.concatenate([jnp.pad(dwa, ((0, SUBLANES - K_A), (0, 0))),
                                 jnp.pad(dwb, ((0, HALO - K_B), (0, 0)))], axis=0)
    gathered_small = _allgather_small(small, conv_rows)

    tiles = [256, D // N_DEV, 256]

    def shard(a, l, w, m, v):
        return _adamw_shard(grads[a][0], grads[a][1], l, k_arr, w[0], m[0], v[0], tiles[a])

    res = {
        "w_in": shard(0, 0, w_in, m_w_in, v_w_in),
        "w_out_a": shard(1, 0, w_out_a, m_w_out_a, v_w_out_a),
        "w_out_b": shard(1, 1, w_out_b, m_w_out_b, v_w_out_b),
        "w_out_x": shard(1, 2, w_out_x, m_w_out_x, v_w_out_x),
        "w_o": shard(1, 3, w_o, m_w_o, v_w_o),
        "w_kv": shard(2, 0, w_kv, m_w_kv, v_w_kv),
    }
    res = {name: tuple(r[None] for r in four) for name, four in res.items()}
    vectors = [(norm_g, m_norm_g, v_norm_g), (conv_b_b, m_conv_b_b, v_conv_b_b), (ln_b_g, m_ln_b_g, v_ln_b_g),
               (ln_b_b, m_ln_b_b, v_ln_b_b), (mem_norm_g, m_mem_norm_g, v_mem_norm_g),
               (final_g2, m_final_g.reshape(1, D), v_final_g.reshape(1, D))]
    out = _adamw_small(gathered_small, k_arr, vectors, (conv_a_w[0], m_conv_a_w[0], v_conv_a_w[0]),
                       (conv_b_w[0], m_conv_b_w[0], v_conv_b_w[0]))
    loss = out[0][0, 0]
    names = ["norm_g", "conv_b_b", "ln_b_g", "ln_b_b", "mem_norm_g", "final_g", "conv_a_w", "conv_b_w"]
    for n, name in enumerate(names):
        four = out[1 + 4 * n:5 + 4 * n]
        if name == "final_g":
            four = [r.reshape(D) for r in four]
        elif name.startswith("conv_") and name.endswith("_w"):
            four = [r[None] for r in four]
        res[name] = tuple(four)

    order = ["norm_g", "w_in", "conv_a_w", "w_out_a", "conv_b_w", "conv_b_b", "ln_b_g", "ln_b_b", "w_out_b",
             "mem_norm_g", "w_kv", "w_out_x", "w_o", "final_g"]
    return (loss, gx[None], *[res[n][0] for n in order], *[res[n][1] for n in order],
            *[res[n][2] for n in order], *[res[n][3] for n in order])
```

```python
import functools

import jax
import jax.numpy as jnp
from jax import lax
from jax.experimental import pallas as pl
from jax.experimental.pallas import tpu as pltpu

F32, BF16 = jnp.float32, jnp.bfloat16
D = 1024
N_DEV = 8
N_HEADS = 4
HEAD_DIM = D // N_HEADS
N_GROUPS = 12
W_IN_SHARD = N_GROUPS * D // N_DEV
UNIT = 512
X_GRAD_K = 2 * D
K_A, K_B = 3, 31
EPS = 1e-6
HALO = 32
SUBLANES = 8
LANES = 128
LANE_GROUPS = D // LANES
UNROLL_A, UNROLL_B = 8, 4
UNROLL_WGRAD_B = 4
WGRAD_TAPS = 16
CONV_PARTIAL_SUMS = 4
VMEM_LIMIT = 56 << 20
MESH = pl.DeviceIdType.MESH
ANY = pl.BlockSpec(memory_space=pl.ANY)
HBM = pl.BlockSpec(memory_space=pltpu.HBM)
SEM = pl.BlockSpec(memory_space=pltpu.SEMAPHORE)
SIDE_EFFECT = pltpu.SideEffectType.DATAFLOW_SIDE_EFFECTING

G_BA, G_CA, G_XA, G_ZA, G_VB, G_GB, G_ZB, G_Q, G_ZX, G_GA, G_GBB, G_GX = range(N_GROUPS)
DP_POS = (0, 1, 2, 3, 6, 7, 8, 4, 5, 9, 10, 11)

ADAM_LR, ADAM_B1, ADAM_B2, ADAM_EPS, ADAM_WD, ADAM_STEP = 0.001, 0.9, 0.999, 1e-08, 0.01, 10

SV_NORM_G, SV_CONV_B_B, SV_LN_G, SV_LN_B, SV_MEM_G, SV_FINAL_G, SV_LOSS = range(7)
SV_CONV_A, SV_CONV_B, SV_ROWS = 8, 16, 48


def _dot(a, b):
    return jnp.dot(a, b, preferred_element_type=F32)


def _dot_nt(a, b):
    return lax.dot_general(a, b, (((1,), (1,)), ((), ())), preferred_element_type=F32)


def _dot_tn(a, b):
    return lax.dot_general(a, b, (((0,), (0,)), ((), ())), preferred_element_type=F32)


def _silu_and_grad(z):
    s = jax.nn.sigmoid(z)
    return z * s, s * (1.0 + z * (1.0 - s))


def _fold8(a):
    return a.reshape(a.shape[0] // SUBLANES, SUBLANES, a.shape[1]).sum(axis=0)


def _mean(a):
    return jnp.mean(a, axis=-1, keepdims=True)


def _f32(ref):
    return ref[...].astype(F32)


def _params(n_grid):
    return pltpu.CompilerParams(dimension_semantics=("arbitrary",) * n_grid, vmem_limit_bytes=VMEM_LIMIT)


def _rows(tm, col=0):
    return pl.BlockSpec((tm, D), lambda i: (i, col))


def _prev_halo(tm, col=0):
    return pl.BlockSpec((HALO, D), lambda i: (jnp.maximum(i * (tm // HALO) - 1, 0), col))


def _next_halo(tm, n_rows, col=0):
    last = n_rows // HALO - 1
    return pl.BlockSpec((HALO, D), lambda i: (jnp.minimum((i + 1) * (tm // HALO), last), col))


def _const(shape):
    return pl.BlockSpec(shape, lambda *_: (0,) * len(shape))


def _w_out_spec(which):
    return pl.BlockSpec((N_DEV, None, D // N_DEV, D), lambda *_: (0, which, 0, 0))


def _to_time_major(t_ref, row0, x):
    n = x.shape[0]
    for j in range(LANE_GROUPS):
        t_ref[pl.ds(row0 * LANE_GROUPS + j, n, stride=LANE_GROUPS), :] = x[:, j * LANES:(j + 1) * LANES]


def _from_time_major(t_ref, n):
    return jnp.concatenate([t_ref[pl.ds(j, n, stride=LANE_GROUPS), :] for j in range(LANE_GROUPS)], axis=-1)


def _row(ref, t):
    start = t * LANE_GROUPS
    if not isinstance(start, int):
        start = pl.multiple_of(start, LANE_GROUPS)
    return ref[pl.ds(start, LANE_GROUPS), :]


def _conv(o_ref, e_ref, w_ref, taps, n_rows, unroll, bias_ref=None):
    weights = [_row(w_ref, k) for k, _ in taps]

    def chunk(c, carry):
        t0 = c * unroll
        n_part = min(CONV_PARTIAL_SUMS, len(taps))
        acc = [[None] * n_part for _ in range(unroll)]
        for n, (_, off) in enumerate(taps):
            for u in range(unroll):
                term = weights[n] * _row(e_ref, t0 + off + u)
                acc[u][n % n_part] = term if acc[u][n % n_part] is None else acc[u][n % n_part] + term
        for u in range(unroll):
            parts = acc[u]
            while len(parts) > 1:
                parts = [parts[n] + parts[n + 1] for n in range(0, len(parts) - 1, 2)] + parts[len(parts) & ~1:]
            out = parts[0]
            if bias_ref is not None:
                out = out + bias_ref[...]
            o_ref[pl.ds(pl.multiple_of((t0 + u) * LANE_GROUPS, LANE_GROUPS), LANE_GROUPS), :] = out
        return carry

    lax.fori_loop(0, n_rows // unroll, chunk, 0)


def _conv_wgrad(dw_ref, d_ref, e_ref, taps, n_rows, unroll):
    for first in range(0, len(taps), WGRAD_TAPS):
        group = taps[first:first + WGRAD_TAPS]
        lo = min(off for _, off in group)
        hi = max(off for _, off in group)

        def chunk(c, accs, group=group, lo=lo, hi=hi):
            t0 = c * unroll
            d = [_row(d_ref, t0 + u) for u in range(unroll)]
            window = [_row(e_ref, t0 + lo + n) for n in range(hi - lo + unroll)]
            accs = list(accs)
            for n, (_, off) in enumerate(group):
                for u in range(unroll):
                    accs[n] = accs[n] + d[u] * window[off - lo + u]
            return tuple(accs)

        zeros = tuple(jnp.zeros((LANE_GROUPS, LANES), F32) for _ in group)
        accs = lax.fori_loop(0, n_rows // unroll, chunk, zeros)
        for (k, _), acc in zip(group, accs):
            dw_ref[pl.ds(k * LANE_GROUPS, LANE_GROUPS), :] += acc


FWD_TAPS_A = [(k, HALO - (K_A - 1) + k) for k in range(K_A)]
BWD_TAPS_A = [(k, K_A - 1 - k) for k in range(K_A)]
FWD_TAPS_B = [(k, HALO - (K_B - 1) + k) for k in range(K_B)]
BWD_TAPS_B = [(k, K_B - 1 - k) for k in range(K_B)]


def _time_major(n_rows):
    return pltpu.VMEM((n_rows * LANE_GROUPS, LANES), F32)


def _kv_fwd(mem, mem_g, wkv_g):
    m_len = mem.shape[0]

    def body(mem_ref, g_ref, w_ref, kv_ref, mn_ref):
        mf = mem_ref[...]
        r = lax.rsqrt(_mean(mf * mf) + EPS)
        mn = ((mf * r) * g_ref[...]).astype(BF16)
        mn_ref[...] = mn
        for b in range(2 * N_HEADS):
            kv_ref[b] = _dot(mn, w_ref[b]).astype(BF16)

    return pl.pallas_call(
        body, name="kv_fwd", grid=(1,),
        in_specs=[_const((m_len, D)), _const((1, D)), _const((2 * N_HEADS, D, HEAD_DIM))],
        out_specs=[_const((2 * N_HEADS, m_len, HEAD_DIM)), _const((m_len, D))],
        out_shape=[jax.ShapeDtypeStruct((2 * N_HEADS, m_len, HEAD_DIM), BF16), jax.ShapeDtypeStruct((m_len, D), BF16)],
        compiler_params=_params(1),
    )(mem, mem_g, wkv_g)


def _kv_bwd(dkv, mem, mem_g, mn16, wkv_g):
    m_len = mem.shape[0]

    def body(dkv_ref, mem_ref, g_ref, mn_ref, w_ref, dw_ref, dg_ref):
        mn = mn_ref[...]
        dmn = jnp.zeros((m_len, D), F32)
        for b in range(2 * N_HEADS):
            d16 = dkv_ref[b].astype(BF16)
            dw_ref[b] = _dot_tn(mn, d16)
            dmn = dmn + _dot_nt(d16, w_ref[b])
        mf = mem_ref[...]
        r = lax.rsqrt(_mean(mf * mf) + EPS)
        dg_ref[...] = _fold8(dmn * (mf * r))

    return pl.pallas_call(
        body, name="kv_bwd", grid=(1,),
        in_specs=[_const((2 * N_HEADS, m_len, HEAD_DIM)), _const((m_len, D)), _const((1, D)), _const((m_len, D)),
                  _const((2 * N_HEADS, D, HEAD_DIM))],
        out_specs=[_const((2 * N_HEADS, D, HEAD_DIM)), _const((SUBLANES, D))],
        out_shape=[jax.ShapeDtypeStruct((2 * N_HEADS, D, HEAD_DIM), F32), jax.ShapeDtypeStruct((SUBLANES, D), F32)],
        compiler_params=_params(1),
    )(dkv, mem, mem_g, mn16, wkv_g)


def _rmsnorm_fwd(x, norm_g, tm):
    n_rows = x.shape[0]

    def body(x_ref, g_ref, u_ref, ut_ref):
        xf = x_ref[...]
        u = (xf * lax.rsqrt(_mean(xf * xf) + EPS)) * g_ref[...]
        u_ref[...] = u.astype(BF16)
        ut_ref[...] = u.T.astype(BF16)

    return pl.pallas_call(
        body, name="rmsnorm_fwd", grid=(n_rows // tm,),
        in_specs=[_rows(tm), _const((1, D))],
        out_specs=[_rows(tm), pl.BlockSpec((D, tm), lambda i: (0, i))],
        out_shape=[jax.ShapeDtypeStruct((n_rows, D), BF16), jax.ShapeDtypeStruct((D, n_rows), BF16)],
        compiler_params=_params(1),
    )(x, norm_g)


def _place():
    x, y, c = lax.axis_index("x"), lax.axis_index("y"), lax.axis_index("c")
    other_chips = [(1 - x, y), (x, 1 - y), (1 - x, 1 - y)]
    return x, y, c, other_chips


def _arrival_order():
    x, y, c, chips = _place()
    order = [4 * x + 2 * y + c, 4 * x + 2 * y + 1 - c]
    for px, py in chips:
        order += [4 * px + 2 * py + c, 4 * px + 2 * py + 1 - c]
    return order


def _proj_fwd_gather(u16, blocks, tm):
    n = len(blocks)
    n_rows = u16.shape[0]
    n_i = n_rows // tm
    per_shard = W_IN_SHARD // UNIT
    assert n_i >= per_shard

    def wt_index(p, i, order):
        return (_dp_unit(per_shard * order[p] + jnp.minimum(i, per_shard - 1)), 0)

    def body(order_ref, u_ref, *refs):
        src, proj_ref, wt_ref, out = refs[:n], refs[n], refs[n + 1], refs[n + 2:2 * n + 2]
        wbuf, stage_sem, send, recv, own_sem = refs[2 * n + 2:]
        p, i = pl.program_id(0), pl.program_id(1)
        x, y, c, chips = _place()
        me, sibling = 4 * x + 2 * y + c, (x, y, 1 - c)

        def copy(t, k, block, to, from_input=False):
            return pltpu.make_async_remote_copy(
                src_ref=src[t] if from_input else out[t].at[block], dst_ref=out[t].at[block],
                send_sem=send.at[t, k], recv_sem=recv.at[t, k], device_id=to, device_id_type=MESH)

        def own_copies():
            return [pltpu.make_async_copy(src[t], out[t].at[me], own_sem.at[t]) for t in range(n)]

        def first_copies():
            first = []
            for t in range(n):
                first.append(copy(t, 0, me, sibling, from_input=True))
                first += [copy(t, 1 + j, me, (*chip, c), from_input=True) for j, chip in enumerate(chips)]
            return first

        def stage(slot, block):
            return pltpu.make_async_copy(out[0].at[block], wbuf.at[slot], stage_sem.at[slot])

        @pl.when((p == 0) & (i == 0))
        def _():
            for cp in own_copies() + first_copies():
                cp.start()
            mine = pltpu.make_async_copy(src[0], wbuf.at[0], stage_sem.at[0])
            mine.start()
            mine.wait()

        @pl.when((p > 0) & (i == 0))
        def _():
            stage(p % 2, order_ref[p]).wait()

        proj_ref[...] = _dot(u_ref[...], wbuf[p % 2])
        for r in range(per_shard):
            @pl.when(i == r)
            def _(r=r):
                wt_ref[...] = wbuf[p % 2, :, r * UNIT:(r + 1) * UNIT].astype(F32).T.astype(BF16)

        for nxt in range(1, N_DEV):
            @pl.when((p == nxt - 1) & (i == n_i - 1))
            def _(nxt=nxt):
                if nxt == 1:
                    block = 4 * x + 2 * y + 1 - c
                    copy(0, 0, block, sibling).wait_recv()
                else:
                    j, passed_on = divmod(nxt - 2, 2)
                    px, py = chips[j]
                    if passed_on:
                        block = 4 * px + 2 * py + 1 - c
                        copy(0, 4 + j, block, sibling).wait_recv()
                    else:
                        block = 4 * px + 2 * py + c
                        copy(0, 1 + j, block, sibling).wait_recv()
                        copy(0, 4 + j, block, sibling).start()
                stage(nxt % 2, block).start()

        @pl.when((p == N_DEV - 1) & (i == n_i - 1))
        def _():
            passed = [copy(0, 4 + j, 4 * px + 2 * py + c, sibling) for j, (px, py) in enumerate(chips)]
            for j, (px, py) in enumerate(chips):
                for t in range(1, n):
                    block = 4 * px + 2 * py + c
                    copy(t, 1 + j, block, sibling).wait_recv()
                    passed.append(copy(t, 4 + j, block, sibling))
                    passed[-1].start()
            for t in range(1, n):
                copy(t, 0, 4 * x + 2 * y + 1 - c, sibling).wait_recv()
                for j, (px, py) in enumerate(chips):
                    copy(t, 4 + j, 4 * px + 2 * py + 1 - c, sibling).wait_recv()
            for cp in first_copies() + passed:
                cp.wait_send()
            for cp in own_copies():
                cp.wait()

    return pl.pallas_call(
        body, name="proj_fwd_gather",
        grid_spec=pltpu.PrefetchScalarGridSpec(
            num_scalar_prefetch=1, grid=(N_DEV, n_i),
            in_specs=[pl.BlockSpec((tm, D), lambda p, i, order: (i, 0))] + [ANY] * n,
            out_specs=[pl.BlockSpec((tm, W_IN_SHARD), lambda p, i, order: (i, order[p])),
                       pl.BlockSpec((UNIT, D), wt_index)] + [ANY] * n,
            scratch_shapes=[pltpu.VMEM((2, D, W_IN_SHARD), BF16), pltpu.SemaphoreType.DMA((2,)),
                            pltpu.SemaphoreType.DMA((n, 7)), pltpu.SemaphoreType.DMA((n, 7)),
                            pltpu.SemaphoreType.DMA((n,))]),
        out_shape=[jax.ShapeDtypeStruct((n_rows, N_GROUPS * D), F32), jax.ShapeDtypeStruct((N_GROUPS * D, D), BF16)]
        + [jax.ShapeDtypeStruct((N_DEV, *b.shape), b.dtype) for b in blocks],
        compiler_params=_params(2),
    )(jnp.stack(_arrival_order()).astype(jnp.int32), u16, *blocks)


def _branch_a_fwd(proj, wo4_g, cw_a, tm):
    n_rows = proj.shape[0]

    def body(bp, cp, xp, za, cph, xph, w_ref, cw_ref, sa_ref, ya_ref, e_scr, o_scr):
        i = pl.program_id(0)
        bp, cp, xp, za, cph, xph = (_f32(r) for r in (bp, cp, xp, za, cph, xph))
        _to_time_major(e_scr, 0, jnp.where(i > 0, cph[...] * xph[...], 0.0))
        _to_time_major(e_scr, HALO, cp[...] * xp[...])
        _conv(o_scr, e_scr, cw_ref, FWD_TAPS_A, tm, UNROLL_A)
        sa = (jax.nn.silu(za[...]) * (bp[...] * _from_time_major(o_scr, tm))).astype(BF16)
        sa_ref[...] = sa
        ya_ref[...] = _dot(sa, w_ref[...].reshape(D, D))

    return pl.pallas_call(
        body, name="branch_a_fwd", grid=(n_rows // tm,),
        in_specs=[_rows(tm, G_BA), _rows(tm, G_CA), _rows(tm, G_XA), _rows(tm, G_ZA),
                  _prev_halo(tm, G_CA), _prev_halo(tm, G_XA), _w_out_spec(0), _const(cw_a.shape)],
        out_specs=[_rows(tm), _rows(tm)],
        out_shape=[jax.ShapeDtypeStruct((n_rows, D), BF16), jax.ShapeDtypeStruct((n_rows, D), F32)],
        scratch_shapes=[_time_major(tm + HALO), _time_major(tm)],
        compiler_params=_params(1),
    )(proj, proj, proj, proj, proj, proj, wo4_g, cw_a)


def _layernorm_parts(cb, lg, lb):
    xc = cb - _mean(cb)
    rstd = lax.rsqrt(_mean(xc * xc) + EPS)
    xhat = xc * rstd
    return xhat, rstd, xhat * lg + lb


def _branch_b_fwd(proj, wo4_g, cw_b, conv_b_b, ln_g, ln_b, tm):
    n_rows = proj.shape[0]

    def body(vb, gb, zb, vbh, gbh, w_ref, cw_ref, bb_ref, lg_ref, lb_ref, cb_ref, sb_ref, yb_ref, e_scr, o_scr):
        i = pl.program_id(0)
        vb, gb, zb, vbh, gbh = (_f32(r) for r in (vb, gb, zb, vbh, gbh))
        _to_time_major(e_scr, 0, jnp.where(i > 0, vbh[...] * jax.nn.sigmoid(gbh[...]), 0.0))
        _to_time_major(e_scr, HALO, vb[...] * jax.nn.sigmoid(gb[...]))
        _conv(o_scr, e_scr, cw_ref, FWD_TAPS_B, tm, UNROLL_B, bias_ref=bb_ref)
        cb = _from_time_major(o_scr, tm)
        cb_ref[...] = cb
        _, _, ln = _layernorm_parts(cb, lg_ref[...], lb_ref[...])
        sb = (jax.nn.silu(zb[...]) * jax.nn.silu(ln)).astype(BF16)
        sb_ref[...] = sb
        yb_ref[...] = _dot(sb, w_ref[...].reshape(D, D))

    return pl.pallas_call(
        body, name="branch_b_fwd", grid=(n_rows // tm,),
        in_specs=[_rows(tm, G_VB), _rows(tm, G_GB), _rows(tm, G_ZB), _prev_halo(tm, G_VB), _prev_halo(tm, G_GB),
                  _w_out_spec(1), _const(cw_b.shape), _const((LANE_GROUPS, LANES)), _const((1, D)), _const((1, D))],
        out_specs=[_rows(tm), _rows(tm), _rows(tm)],
        out_shape=[jax.ShapeDtypeStruct((n_rows, D), F32), jax.ShapeDtypeStruct((n_rows, D), BF16),
                   jax.ShapeDtypeStruct((n_rows, D), F32)],
        scratch_shapes=[_time_major(tm + HALO), _time_major(tm)],
        compiler_params=_params(1),
    )(proj, proj, proj, proj, proj, wo4_g, cw_b, conv_b_b.reshape(LANE_GROUPS, LANES), ln_g, ln_b)


def _attention(q16, kv_ref):
    probs, outs = [], []
    for h in range(N_HEADS):
        s = _dot_nt(q16[:, h * HEAD_DIM:(h + 1) * HEAD_DIM], kv_ref[h]) * (HEAD_DIM ** -0.5)
        e = jnp.exp(s - jnp.max(s, axis=-1, keepdims=True))
        p = e / jnp.sum(e, axis=-1, keepdims=True)
        probs.append(p)
        outs.append(_dot(p.astype(BF16), kv_ref[N_HEADS + h]))
    return probs, outs


def _branch_x_fwd(proj, kv16, wo4_g, tm):
    n_rows = proj.shape[0]

    def body(q, zx, kv_ref, w_ref, sx_ref, yx_ref):
        _, outs = _attention(q[...].astype(BF16), kv_ref)
        sx = (jax.nn.silu(_f32(zx)) * jnp.concatenate(outs, axis=-1)).astype(BF16)
        sx_ref[...] = sx
        yx_ref[...] = _dot(sx, w_ref[...].reshape(D, D))

    return pl.pallas_call(
        body, name="branch_x_fwd", grid=(n_rows // tm,),
        in_specs=[_rows(tm, G_Q), _rows(tm, G_ZX), _const(kv16.shape), _w_out_spec(2)],
        out_specs=[_rows(tm), _rows(tm)],
        out_shape=[jax.ShapeDtypeStruct((n_rows, D), BF16), jax.ShapeDtypeStruct((n_rows, D), F32)],
        compiler_params=_params(1),
    )(proj, proj, kv16, wo4_g)


def _merge_fwd_bwd(proj, ya, yb, yx, x, target, wo4_g, final_g, tm):
    n_rows = proj.shape[0]
    inv_d = 1.0 / D

    def body(ga, gb, gx, ya_ref, yb_ref, yx_ref, x_ref, t_ref, w_ref, fg_ref,
             dh_ref, dya_ref, dyb_ref, dyx_ref, dp_ref, dw_ref, dfg_ref, sq_ref):
        i = pl.program_id(0)
        wo = w_ref[...].reshape(D, D)
        sig = [jax.nn.sigmoid(_f32(g)) for g in (ga, gb, gx)]
        ys = [ya_ref[...], yb_ref[...], yx_ref[...]]
        m16 = (sig[0] * ys[0] + sig[1] * ys[1] + sig[2] * ys[2]).astype(BF16)
        h = x_ref[...] + _dot(m16, wo)
        r = lax.rsqrt(_mean(h * h) + EPS)
        hn = h * r
        fg = fg_ref[...]
        err = hn * fg - t_ref[...]
        dy = err * inv_d
        dhn = dy * fg
        dh = r * (dhn - hn * _mean(dhn * hn))
        dh_ref[...] = dh
        dh16 = dh.astype(BF16)
        dm = _dot_nt(dh16, wo)
        for n, out in enumerate((dya_ref, dyb_ref, dyx_ref)):
            out[...] = (sig[n] * dm).astype(BF16)
            dp_ref[:, n * D:(n + 1) * D] = (dm * ys[n] * (sig[n] * (1.0 - sig[n]))).astype(BF16)

        @pl.when(i == 0)
        def _():
            dw_ref[...] = jnp.zeros_like(dw_ref)
            dfg_ref[...] = jnp.zeros_like(dfg_ref)
            sq_ref[...] = jnp.zeros_like(sq_ref)

        dw_ref[0] += _dot_tn(m16, dh16)
        dfg_ref[...] += _fold8(dy * hn)
        sq_ref[...] += _fold8(err * err)

    vec = jax.ShapeDtypeStruct((SUBLANES, D), F32)
    return pl.pallas_call(
        body, name="merge_fwd_bwd", grid=(n_rows // tm,),
        in_specs=[_rows(tm, G_GA), _rows(tm, G_GBB), _rows(tm, G_GX), _rows(tm), _rows(tm), _rows(tm), _rows(tm),
                  _rows(tm), _w_out_spec(3), _const((1, D))],
        out_specs=[_rows(tm), _rows(tm), _rows(tm), _rows(tm), pl.BlockSpec((tm, 3 * D), lambda i: (i, 3)),
                   pl.BlockSpec((1, D, D), lambda i: (3, 0, 0)), _const((SUBLANES, D)), _const((SUBLANES, D))],
        out_shape=[jax.ShapeDtypeStruct((n_rows, D), F32), jax.ShapeDtypeStruct((n_rows, D), BF16),
                   jax.ShapeDtypeStruct((n_rows, D), BF16), jax.ShapeDtypeStruct((n_rows, D), BF16),
                   jax.ShapeDtypeStruct((n_rows, N_GROUPS * D), BF16), jax.ShapeDtypeStruct((4, D, D), F32), vec, vec],
        compiler_params=_params(1),
    )(proj, proj, proj, ya, yb, yx, x, target, wo4_g, final_g)


def _branch_a_bwd(dya, proj, sa16, wo4_g, cw_a, dproj, dw4, tm):
    n_rows = proj.shape[0]
    n_tiles = n_rows // tm

    def body(dya_ref, bp, cp, xp, za, sa_ref, dyan, bpn, zan, cph, xph, w_ref, cw_ref, dp_in, dw_in,
             dp_ref, dw_ref, dwa_ref, e1, e2, o_scr):
        del dp_in, dw_in
        i = pl.program_id(0)
        bp, cp, xp, za, bpn, zan, cph, xph = (_f32(r) for r in (bp, cp, xp, za, bpn, zan, cph, xph))
        woa = w_ref[...].reshape(D, D)
        dya16 = dya_ref[...]
        _to_time_major(e1, 0, jnp.where(i > 0, cph[...] * xph[...], 0.0))
        _to_time_major(e1, HALO, cp[...] * xp[...])
        _conv(o_scr, e1, cw_ref, FWD_TAPS_A, tm, UNROLL_A)
        ca = _from_time_major(o_scr, tm)
        dsa = _dot_nt(dya16, woa)
        silu_z, dsilu_z = _silu_and_grad(za[...])
        t = dsa * silu_z
        dp_ref[:, 0 * D:1 * D] = (t * ca).astype(BF16)
        dp_ref[:, 3 * D:4 * D] = (dsa * (bp[...] * ca) * dsilu_z).astype(BF16)
        dsan = _dot_nt(dyan[...], woa)
        dcan = (dsan * jax.nn.silu(zan[...])) * bpn[...]
        _to_time_major(e2, 0, t * bp[...])
        _to_time_major(e2, tm, jnp.where(i < n_tiles - 1, dcan, 0.0))

        @pl.when(i == 0)
        def _():
            dw_ref[...] = jnp.zeros_like(dw_ref)
            dwa_ref[...] = jnp.zeros_like(dwa_ref)

        _conv_wgrad(dwa_ref, e2, e1, FWD_TAPS_A, tm, UNROLL_A)
        dw_ref[0] += _dot_tn(sa_ref[...], dya16)
        _conv(o_scr, e2, cw_ref, BWD_TAPS_A, tm, UNROLL_A)
        dprod = _from_time_major(o_scr, tm)
        dp_ref[:, 1 * D:2 * D] = (dprod * xp[...]).astype(BF16)
        dp_ref[:, 2 * D:3 * D] = (dprod * cp[...]).astype(BF16)

    return pl.pallas_call(
        body, name="branch_a_bwd", grid=(n_tiles,),
        in_specs=[_rows(tm), _rows(tm, G_BA), _rows(tm, G_CA), _rows(tm, G_XA), _rows(tm, G_ZA), _rows(tm),
                  _next_halo(tm, n_rows), _next_halo(tm, n_rows, G_BA), _next_halo(tm, n_rows, G_ZA),
                  _prev_halo(tm, G_CA), _prev_halo(tm, G_XA), _w_out_spec(0), _const(cw_a.shape), ANY, ANY],
        out_specs=[pl.BlockSpec((tm, 4 * D), lambda i: (i, 0)), pl.BlockSpec((1, D, D), lambda i: (0, 0, 0)),
                   _const((K_A * LANE_GROUPS, LANES))],
        out_shape=[jax.ShapeDtypeStruct(dproj.shape, BF16), jax.ShapeDtypeStruct(dw4.shape, F32),
                   jax.ShapeDtypeStruct((K_A * LANE_GROUPS, LANES), F32)],
        input_output_aliases={13: 0, 14: 1},
        scratch_shapes=[_time_major(tm + HALO), _time_major(tm + HALO), _time_major(tm)],
        compiler_params=_params(1),
    )(dya, proj, proj, proj, proj, sa16, dya, proj, proj, proj, proj, wo4_g, cw_a, dproj, dw4)


def _branch_b_bwd(dyb, proj, cb, sb16, wo4_g, cw_b, ln_g, ln_b, dproj, dw4, tm):
    n_rows = proj.shape[0]
    n_tiles = n_rows // tm

    def body(dyb_ref, zb, cb_ref, vb, gb, sb_ref, dybn, zbn, cbn, vbh, gbh, w_ref, cw_ref, lg_ref, lb_ref,
             dp_in, dw_in, dp_ref, dw_ref, dwb_ref, dbb_ref, dlg_ref, dlb_ref, e1, e2, o_scr):
        del dp_in, dw_in
        vb, gb, zbn, vbh, gbh = (_f32(r) for r in (vb, gb, zbn, vbh, gbh))
        i = pl.program_id(0)
        wob = w_ref[...].reshape(D, D)
        lg, lb = lg_ref[...], lb_ref[...]

        def conv_out_grad(dy16, z, c):
            dsb = _dot_nt(dy16, wob)
            xhat, rstd, ln = _layernorm_parts(c, lg, lb)
            sw, dsw = _silu_and_grad(ln)
            sz, dsz = _silu_and_grad(z)
            dln = (dsb * sz) * dsw
            dxhat = dln * lg
            dcb = rstd * (dxhat - _mean(dxhat) - xhat * _mean(dxhat * xhat))
            return dsb * sw * dsz, dln, xhat, dcb

        @pl.when(i == 0)
        def _():
            dw_ref[...] = jnp.zeros_like(dw_ref)
            dwb_ref[...] = jnp.zeros_like(dwb_ref)
            dbb_ref[...] = jnp.zeros_like(dbb_ref)
            dlg_ref[...] = jnp.zeros_like(dlg_ref)
            dlb_ref[...] = jnp.zeros_like(dlb_ref)

        dyb16 = dyb_ref[...]
        dzb, dln, xhat, dcb = conv_out_grad(dyb16, _f32(zb), cb_ref[...])
        dp_ref[:, 2 * D:3 * D] = dzb.astype(BF16)
        _, _, _, dcbn = conv_out_grad(dybn[...], zbn[...], cbn[...])
        _to_time_major(e2, 0, dcb)
        _to_time_major(e2, tm, jnp.where(i < n_tiles - 1, dcbn, 0.0))
        dlg_ref[...] += _fold8(dln * xhat)
        dlb_ref[...] += _fold8(dln)
        dbb_ref[...] += _fold8(dcb)
        dw_ref[0] += _dot_tn(sb_ref[...], dyb16)
        sg = jax.nn.sigmoid(gb[...])
        _to_time_major(e1, 0, jnp.where(i > 0, vbh[...] * jax.nn.sigmoid(gbh[...]), 0.0))
        _to_time_major(e1, HALO, vb[...] * sg)
        _conv_wgrad(dwb_ref, e2, e1, FWD_TAPS_B, tm, UNROLL_WGRAD_B)
        _conv(o_scr, e2, cw_ref, BWD_TAPS_B, tm, UNROLL_B)
        dglu = _from_time_major(o_scr, tm)
        dp_ref[:, 0 * D:1 * D] = (dglu * sg).astype(BF16)
        dp_ref[:, 1 * D:2 * D] = (dglu * vb[...] * (sg * (1.0 - sg))).astype(BF16)

    vec = jax.ShapeDtypeStruct((SUBLANES, D), F32)
    return pl.pallas_call(
        body, name="branch_b_bwd", grid=(n_tiles,),
        in_specs=[_rows(tm), _rows(tm, G_ZB), _rows(tm), _rows(tm, G_VB), _rows(tm, G_GB), _rows(tm),
                  _next_halo(tm, n_rows), _next_halo(tm, n_rows, G_ZB), _next_halo(tm, n_rows),
                  _prev_halo(tm, G_VB), _prev_halo(tm, G_GB), _w_out_spec(1), _const(cw_b.shape), _const((1, D)),
                  _const((1, D)), ANY, ANY],
        out_specs=[pl.BlockSpec((tm, 3 * D), lambda i: (i, 2)), pl.BlockSpec((1, D, D), lambda i: (1, 0, 0)),
                   _const((K_B * LANE_GROUPS, LANES)), _const((SUBLANES, D)), _const((SUBLANES, D)),
                   _const((SUBLANES, D))],
        out_shape=[jax.ShapeDtypeStruct(dproj.shape, BF16), jax.ShapeDtypeStruct(dw4.shape, F32),
                   jax.ShapeDtypeStruct((K_B * LANE_GROUPS, LANES), F32), vec, vec, vec],
        input_output_aliases={15: 0, 16: 1},
        scratch_shapes=[_time_major(tm + HALO), _time_major(tm + HALO), _time_major(tm)],
        compiler_params=_params(1),
    )(dyb, proj, cb, proj, proj, sb16, dyb, proj, cb, proj, proj, wo4_g, cw_b, ln_g, ln_b, dproj, dw4)


def _branch_x_bwd(dyx, proj, sx16, kv16, wo4_g, dproj, dw4, tm):
    n_rows = proj.shape[0]
    scale = HEAD_DIM ** -0.5

    def body(dyx_ref, q, zx, sx_ref, kv_ref, w_ref, dp_in, dw_in, dp_ref, dw_ref, dkv_ref):
        del dp_in, dw_in
        i = pl.program_id(0)
        dyx16 = dyx_ref[...]
        q16 = q[...].astype(BF16)
        probs, outs = _attention(q16, kv_ref)
        dsx = _dot_nt(dyx16, w_ref[...].reshape(D, D))
        silu_z, dsilu_z = _silu_and_grad(_f32(zx))
        dp_ref[:, D:2 * D] = (dsx * jnp.concatenate(outs, axis=-1) * dsilu_z).astype(BF16)
        do16 = (dsx * silu_z).astype(BF16)

        @pl.when(i == 0)
        def _():
            dw_ref[...] = jnp.zeros_like(dw_ref)
            dkv_ref[...] = jnp.zeros_like(dkv_ref)

        for h in range(N_HEADS):
            cols = slice(h * HEAD_DIM, (h + 1) * HEAD_DIM)
            p = probs[h]
            dprob = _dot_nt(do16[:, cols], kv_ref[N_HEADS + h])
            ds16 = ((p * (dprob - jnp.sum(p * dprob, axis=-1, keepdims=True))) * scale).astype(BF16)
            dp_ref[:, cols] = _dot(ds16, kv_ref[h]).astype(BF16)
            dkv_ref[h] += _dot_tn(ds16, q16[:, cols])
            dkv_ref[N_HEADS + h] += _dot_tn(p.astype(BF16), do16[:, cols])
        dw_ref[0] += _dot_tn(sx_ref[...], dyx16)

    return pl.pallas_call(
        body, name="branch_x_bwd", grid=(n_rows // tm,),
        in_specs=[_rows(tm), _rows(tm, G_Q), _rows(tm, G_ZX), _rows(tm), _const(kv16.shape), _w_out_spec(2), ANY, ANY],
        out_specs=[pl.BlockSpec((tm, 2 * D), lambda i: (i, 2)), pl.BlockSpec((1, D, D), lambda i: (2, 0, 0)),
                   _const(kv16.shape)],
        out_shape=[jax.ShapeDtypeStruct(dproj.shape, BF16), jax.ShapeDtypeStruct(dw4.shape, F32),
                   jax.ShapeDtypeStruct(kv16.shape, F32)],
        input_output_aliases={6: 0, 7: 1},
        compiler_params=_params(1),
    )(dyx, proj, proj, sx16, kv16, wo4_g, dproj, dw4)


def _dp_unit(u):
    g = u // 2
    pos = jnp.where(g < G_VB, g, jnp.where(g < G_Q, g + 2, jnp.where(g < G_GA, g - 3, g)))
    return 2 * pos + u % 2


def _scatter_copies(srcs, lands, send, recv):
    x, y, c = lax.axis_index("x"), lax.axis_index("y"), lax.axis_index("c")
    copies = []
    for n in range(N_DEV - 1):
        flip = n + 1
        px = 1 - x if flip & 4 else x
        py = 1 - y if flip & 2 else y
        pc = 1 - c if flip & 1 else c
        for t, (src, land) in enumerate(zip(srcs, lands)):
            copies.append(pltpu.make_async_remote_copy(
                src_ref=src.at[4 * px + 2 * py + pc], dst_ref=land.at[n], send_sem=send.at[t * (N_DEV - 1) + n],
                recv_sem=recv.at[t * (N_DEV - 1) + n], device_id=(px, py, pc), device_id_type=MESH))
    return copies


def _scatter_start(name, arrays, views, n_views):
    n = len(arrays)
    lands = [lax.empty(tuple(N_DEV - 1 if d == N_DEV else d for d in a.shape), a.dtype) for a in arrays]

    def body(*refs):
        src, land, (send, recv) = refs[:n], refs[n:2 * n], refs[2 * n:2 * n + 2]
        token = refs[-1]
        for cp in _scatter_copies(views(src), views(land), send, recv):
            cp.start()
        token[...] = jnp.zeros_like(token)

    sems = pltpu.SemaphoreType.DMA((n_views * (N_DEV - 1),))
    out = pl.pallas_call(
        body, name=name,
        in_specs=[HBM] * (2 * n),
        out_specs=[SEM, SEM] + [HBM] * (2 * n) + [pl.BlockSpec(memory_space=pltpu.VMEM)],
        out_shape=[sems, sems] + [pltpu.HBM(a.shape, a.dtype) for a in arrays + lands]
        + [jax.ShapeDtypeStruct((SUBLANES, LANES), F32)],
        input_output_aliases={k: 2 + k for k in range(2 * n)},
        compiler_params=pltpu.CompilerParams(has_side_effects=SIDE_EFFECT),
    )(*[pltpu.with_memory_space_constraint(a, pltpu.HBM) for a in arrays + lands])
    return dict(name=name, sems=out[:2], moving=out[2:2 + 2 * n], views=views, token=out[-1])


def _scatter_wait(started, after):
    n = len(started["moving"]) // 2
    views = started["views"]

    def body(*refs):
        src, land, (send, recv) = refs[:n], refs[n:2 * n], refs[2 * n:2 * n + 2]
        for cp in _scatter_copies(views(src), views(land), send, recv):
            cp.wait_send()
            cp.wait_recv()

    out = pl.pallas_call(
        body, name=started["name"].replace("start", "wait"),
        in_specs=[HBM] * (2 * n) + [SEM, SEM, ANY],
        out_specs=[HBM] * (2 * n),
        out_shape=[pltpu.HBM(a.shape, a.dtype) for a in started["moving"]],
        input_output_aliases={k: k for k in range(2 * n)},
        compiler_params=pltpu.CompilerParams(has_side_effects=SIDE_EFFECT),
    )(*started["moving"], *started["sems"], after)
    return out[n:]


def _w_in_grad(ut, dproj, tk, token):
    n_rows = dproj.shape[0]
    n_k = n_rows // tk
    per_shard = W_IN_SHARD // UNIT

    def body(ut_ref, dp0, dp1, dp2, token_ref, out_ref, out16_ref, acc):
        del token_ref
        t = pl.program_id(1)

        for r, dp_ref in enumerate((dp0, dp1, dp2)):
            cols = slice(r * UNIT, (r + 1) * UNIT)

            @pl.when(t == 0)
            def _(dp_ref=dp_ref, cols=cols):
                acc[:, cols] = _dot(ut_ref[...], dp_ref[...])

            @pl.when(t > 0)
            def _(dp_ref=dp_ref, cols=cols):
                acc[:, cols] += _dot(ut_ref[...], dp_ref[...])

        @pl.when(t == n_k - 1)
        def _():
            out_ref[0] = acc[...]
            out16_ref[0] = acc[...].astype(BF16)

    def dp_spec(r):
        return pl.BlockSpec((tk, UNIT), lambda q, t: (t, _dp_unit(per_shard * q + r)))

    shard = pl.BlockSpec((1, D, W_IN_SHARD), lambda q, t: (q, 0, 0))
    return pl.pallas_call(
        body, name="w_in_grad", grid=(N_DEV, n_k),
        in_specs=[pl.BlockSpec((D, tk), lambda q, t: (0, t)), dp_spec(0), dp_spec(1), dp_spec(2), ANY],
        out_specs=[shard, shard],
        out_shape=[jax.ShapeDtypeStruct((N_DEV, D, W_IN_SHARD), F32), jax.ShapeDtypeStruct((N_DEV, D, W_IN_SHARD), BF16)],
        scratch_shapes=[pltpu.VMEM((D, W_IN_SHARD), F32)],
        compiler_params=_params(2),
    )(ut, dproj, dproj, dproj, token)


def _x_grad(dproj, win_t, x, dh, norm_g, token, tm):
    n_rows = x.shape[0]
    n_k = N_GROUPS * D // X_GRAD_K

    def body(dp_ref, wt_ref, x_ref, dh_ref, g_ref, token_ref, gx_ref, dg_ref, acc):
        del token_ref
        i, g = pl.program_id(0), pl.program_id(1)

        @pl.when((i == 0) & (g == 0))
        def _():
            dg_ref[...] = jnp.zeros_like(dg_ref)

        @pl.when(g == 0)
        def _():
            acc[...] = _dot(dp_ref[...], wt_ref[...])

        @pl.when(g > 0)
        def _():
            acc[...] += _dot(dp_ref[...], wt_ref[...])

        @pl.when(g == n_k - 1)
        def _():
            du = acc[...]
            xf = x_ref[...]
            r = lax.rsqrt(_mean(xf * xf) + EPS)
            xn = xf * r
            dun = du * g_ref[...]
            gx_ref[...] = dh_ref[...] + r * (dun - xn * _mean(dun * xn))
            dg_ref[...] += _fold8(du * xn)

    return pl.pallas_call(
        body, name="x_grad", grid=(n_rows // tm, n_k),
        in_specs=[pl.BlockSpec((tm, X_GRAD_K), lambda i, g: (i, g)), pl.BlockSpec((X_GRAD_K, D), lambda i, g: (g, 0)),
                  pl.BlockSpec((tm, D), lambda i, g: (i, 0)), pl.BlockSpec((tm, D), lambda i, g: (i, 0)),
                  _const((1, D)), ANY],
        out_specs=[pl.BlockSpec((tm, D), lambda i, g: (i, 0)), _const((SUBLANES, D))],
        out_shape=[jax.ShapeDtypeStruct((n_rows, D), F32), jax.ShapeDtypeStruct((SUBLANES, D), F32)],
        scratch_shapes=[pltpu.VMEM((tm, D), F32)],
        compiler_params=_params(2),
    )(dproj, win_t, x, dh, norm_g, token)


def _local_step(x, mem, target, norm_g, conv_b_b, ln_g, ln_b, mem_g, final_g, shards):
    n_rows = x.shape[0]
    tm = min(512, n_rows)
    big = min(1024, n_rows)
    u16, ut = _rmsnorm_fwd(x, norm_g, big)
    proj, win_t, _, wkv_g, wo4_g, cw_g = _proj_fwd_gather(u16, shards, big)
    cw_rows = cw_g.transpose(1, 0, 2).reshape((SUBLANES + HALO) * LANE_GROUPS, LANES)
    cw_a, cw_b = cw_rows[:SUBLANES * LANE_GROUPS], cw_rows[SUBLANES * LANE_GROUPS:]
    kv16, mn16 = _kv_fwd(mem, mem_g, wkv_g)
    sa16, ya = _branch_a_fwd(proj, wo4_g, cw_a, tm)
    cb, sb16, yb = _branch_b_fwd(proj, wo4_g, cw_b, conv_b_b, ln_g, ln_b, tm)
    sx16, yx = _branch_x_fwd(proj, kv16, wo4_g, tm)
    dh, dya, dyb, dyx, dproj, dw4, dfg, sq = _merge_fwd_bwd(proj, ya, yb, yx, x, target, wo4_g, final_g,
                                                             min(256, n_rows))
    dproj, dw4, dwa = _branch_a_bwd(dya, proj, sa16, wo4_g, cw_a, dproj, dw4, tm)
    dproj, dw4, dwb, dbb, dlg, dlb = _branch_b_bwd(dyb, proj, cb, sb16, wo4_g, cw_b, ln_g, ln_b, dproj, dw4, tm)
    dproj, dw4, dkv = _branch_x_bwd(dyx, proj, sx16, kv16, wo4_g, dproj, dw4, tm)
    dwkv_g, dmg = _kv_bwd(dkv, mem, mem_g, mn16, wkv_g)
    dw4 = dw4.reshape(4, N_DEV, D // N_DEV, D)
    small_moving = _scatter_start("small_grads_start", [dw4.astype(BF16), dwkv_g.astype(BF16)],
                                  lambda refs: [refs[0].at[w] for w in range(4)] + [refs[1]], 5)
    dwin_g, dwin16 = _w_in_grad(ut, dproj, min(2048, n_rows), small_moving["token"])
    w_in_moving = _scatter_start("w_in_grad_start", [dwin16], lambda refs: list(refs), 1)
    gx, dng = _x_grad(dproj, win_t, x, dh, norm_g, w_in_moving["token"], big)
    land4, landkv = _scatter_wait(small_moving, dng)
    landin, = _scatter_wait(w_in_moving, dng)
    small = {SV_NORM_G: dng, SV_CONV_B_B: dbb, SV_LN_G: dlg, SV_LN_B: dlb, SV_MEM_G: dmg, SV_FINAL_G: dfg, SV_LOSS: sq}
    grads = [(dwin_g[None], landin[None]), (dw4, land4), (dwkv_g[None], landkv[None])]
    return gx, grads, small, dwa.reshape(K_A, D), dwb.reshape(K_B, D)


def _allgather_small(small, conv_rows):
    keys = sorted(small)

    def body(*refs):
        parts, (conv_ref, out_ref, mine, send, recv) = refs[:len(keys)], refs[len(keys):]
        x, y, c, chips = _place()
        me, sibling = 4 * x + 2 * y + c, (x, y, 1 - c)
        mine[pl.ds(0, SV_CONV_A), :] = jnp.zeros((SV_CONV_A, D), F32)
        for key, part in zip(keys, parts):
            mine[key:key + 1, :] = jnp.sum(part[...], axis=0, keepdims=True)
        mine[pl.ds(SV_CONV_A, SV_ROWS - SV_CONV_A), :] = conv_ref[...]
        out_ref[me] = mine[...]

        def copy(k, block, to, from_mine=False):
            return pltpu.make_async_remote_copy(
                src_ref=mine if from_mine else out_ref.at[block], dst_ref=out_ref.at[block],
                send_sem=send.at[k], recv_sem=recv.at[k], device_id=to, device_id_type=MESH)

        first = [copy(0, me, sibling, from_mine=True)]
        first += [copy(1 + j, me, (*chip, c), from_mine=True) for j, chip in enumerate(chips)]
        for cp in first:
            cp.start()
        passed = []
        for j, (px, py) in enumerate(chips):
            block = 4 * px + 2 * py + c
            copy(1 + j, block, sibling).wait_recv()
            passed.append(copy(4 + j, block, sibling))
            passed[-1].start()
        copy(0, 4 * x + 2 * y + 1 - c, sibling).wait_recv()
        for j, (px, py) in enumerate(chips):
            copy(4 + j, 4 * px + 2 * py + 1 - c, sibling).wait_recv()
        for cp in first + passed:
            cp.wait_send()

    vmem = pl.BlockSpec(memory_space=pltpu.VMEM)
    return pl.pallas_call(
        body, name="allgather_small",
        in_specs=[vmem] * (len(keys) + 1), out_specs=vmem,
        out_shape=jax.ShapeDtypeStruct((N_DEV, SV_ROWS, D), F32),
        scratch_shapes=[pltpu.VMEM((SV_ROWS, D), F32), pltpu.SemaphoreType.DMA((7,)), pltpu.SemaphoreType.DMA((7,))],
    )(*[small[k] for k in keys], conv_rows)


def _adamw(w, g, m, v):
    m = ADAM_B1 * m + (1.0 - ADAM_B1) * g
    v = ADAM_B2 * v + (1.0 - ADAM_B2) * (g * g)
    m_hat = m / (1.0 - ADAM_B1 ** ADAM_STEP)
    v_hat = v / (1.0 - ADAM_B2 ** ADAM_STEP)
    return -ADAM_LR * (m_hat / (jnp.sqrt(v_hat) + ADAM_EPS) + ADAM_WD * w), m, v


def _adamw_shard(own, landed, piece, k_arr, w, m, v, tr):
    n_r, n_c = w.shape
    n_landed = landed.shape[1]

    def body(k_ref, own_ref, *refs):
        del k_ref
        landed_refs, (w_ref, m_ref, v_ref, g_out, d_out, m_out, v_out) = refs[:n_landed], refs[n_landed:]
        g = own_ref[0, 0]
        for landed_ref in landed_refs:
            g = g + landed_ref[0, 0].astype(F32)
        g_out[...] = g
        d_out[...], m_out[...], v_out[...] = _adamw(w_ref[...], g, m_ref[...], v_ref[...])

    blk = (1, 1, tr, n_c)
    flat = pl.BlockSpec((tr, n_c), lambda r, k: (r, 0))
    return pl.pallas_call(
        body, name="adamw_shard",
        grid_spec=pltpu.PrefetchScalarGridSpec(
            num_scalar_prefetch=1, grid=(n_r // tr,),
            in_specs=[pl.BlockSpec(blk, lambda r, k: (piece, k[0], r, 0))]
            + [pl.BlockSpec(blk, functools.partial(lambda r, k, j: (piece, j, r, 0), j=j)) for j in range(n_landed)]
            + [flat] * 3,
            out_specs=[flat] * 4),
        out_shape=[jax.ShapeDtypeStruct((n_r, n_c), F32)] * 4,
        compiler_params=_params(1),
    )(k_arr, own, *([landed] * n_landed), w, m, v)


def _adamw_small(gathered, k_arr, vectors, conv_a, conv_b):
    n_vec = len(vectors)
    cols = D // N_DEV

    def body(k_ref, full_ref, cols_ref, *refs):
        del k_ref
        ins, outs = refs[:3 * (n_vec + 2)], refs[3 * (n_vec + 2):-2]
        tot, tot_cols = refs[-2:]
        tot[...] = full_ref[0]
        tot_cols[...] = cols_ref[0]
        for dev in range(1, N_DEV):
            tot[...] += full_ref[dev]
            tot_cols[...] += cols_ref[dev]
        loss = (0.5 / D) * jnp.sum(tot[SV_LOSS:SV_LOSS + 1, :])
        outs[0][...] = jnp.full(outs[0].shape, loss, F32)
        grads = [tot[n:n + 1, :] for n in range(n_vec)]
        grads += [tot_cols[pl.ds(SV_CONV_A, K_A), :], tot_cols[pl.ds(SV_CONV_B, K_B), :]]
        for n, g in enumerate(grads):
            w_ref, m_ref, v_ref = ins[3 * n:3 * n + 3]
            g_out, d_out, m_out, v_out = outs[1 + 4 * n:5 + 4 * n]
            g_out[...] = g
            d_out[...], m_out[...], v_out[...] = _adamw(w_ref[...], g, m_ref[...], v_ref[...])

    weights = list(vectors) + [conv_a, conv_b]
    flat_in = [a for wmv in weights for a in wmv]
    out_shape = [jax.ShapeDtypeStruct((SUBLANES, 128), F32)]
    for wmv in weights:
        out_shape += [jax.ShapeDtypeStruct(wmv[0].shape, F32)] * 4
    return pl.pallas_call(
        body, name="adamw_small",
        grid_spec=pltpu.PrefetchScalarGridSpec(
            num_scalar_prefetch=1, grid=(1,),
            in_specs=[pl.BlockSpec((N_DEV, SV_ROWS, D), lambda i, k: (0, 0, 0)),
                      pl.BlockSpec((N_DEV, SV_ROWS, cols), lambda i, k: (0, 0, k[0]))]
            + [pl.BlockSpec(a.shape, lambda i, k: (0, 0)) for a in flat_in],
            out_specs=[pl.BlockSpec(s.shape, lambda i, k: (0, 0)) for s in out_shape],
            scratch_shapes=[pltpu.VMEM((SV_ROWS, D), F32), pltpu.VMEM((SV_ROWS, cols), F32)]),
        out_shape=out_shape,
        compiler_params=_params(1),
    )(k_arr, gathered, gathered, *flat_in)


def kernel(x, mem, norm_g, w_in, conv_a_w, w_out_a, conv_b_w, conv_b_b, ln_b_g, ln_b_b, w_out_b, mem_norm_g, w_kv, w_out_x, w_o, final_g, loss_target, m_norm_g, m_w_in, m_conv_a_w, m_w_out_a, m_conv_b_w, m_conv_b_b, m_ln_b_g, m_ln_b_b, m_w_out_b, m_mem_norm_g, m_w_kv, m_w_out_x, m_w_o, m_final_g, v_norm_g, v_w_in, v_conv_a_w, v_w_out_a, v_conv_b_w, v_conv_b_b, v_ln_b_g, v_ln_b_b, v_w_out_b, v_mem_norm_g, v_w_kv, v_w_out_x, v_w_o, v_final_g):
    xi, yi, ci = lax.axis_index("x"), lax.axis_index("y"), lax.axis_index("c")
    k_arr = jnp.reshape(4 * xi + 2 * yi + ci, (1,)).astype(jnp.int32)

    cw = jnp.concatenate([jnp.pad(conv_a_w[0], ((0, SUBLANES - K_A), (0, 0))),
                          jnp.pad(conv_b_w[0], ((0, HALO - K_B), (0, 0)))], axis=0)
    wo4 = jnp.stack([w_out_a[0], w_out_b[0], w_out_x[0], w_o[0]]).astype(BF16)
    shards = [w_in[0].astype(BF16), w_kv[0].astype(BF16), wo4, cw]

    final_g2 = final_g.reshape(1, D)
    gx, grads, small, dwa, dwb = _local_step(
        x[0], mem[0], loss_target[0], norm_g, conv_b_b, ln_b_g, ln_b_b, mem_norm_g, final_g2, shards)

    conv_rows = jnp.concatenate([jnp.pad(dwa, ((0, SUBLANES - K_A), (0, 0))),
                                 jnp.pad(dwb, ((0, HALO - K_B), (0, 0)))], axis=0)
    gathered_small = _allgather_small(small, conv_rows)

    tiles = [256, D // N_DEV, 256]

    def shard(a, l, w, m, v):
        return _adamw_shard(grads[a][0], grads[a][1], l, k_arr, w[0], m[0], v[0], tiles[a])

    res = {
        "w_in": shard(0, 0, w_in, m_w_in, v_w_in),
        "w_out_a": shard(1, 0, w_out_a, m_w_out_a, v_w_out_a),
        "w_out_b": shard(1, 1, w_out_b, m_w_out_b, v_w_out_b),
        "w_out_x": shard(1, 2, w_out_x, m_w_out_x, v_w_out_x),
        "w_o": shard(1, 3, w_o, m_w_o, v_w_o),
        "w_kv": shard(2, 0, w_kv, m_w_kv, v_w_kv),
    }
    res = {name: tuple(r[None] for r in four) for name, four in res.items()}
    vectors = [(norm_g, m_norm_g, v_norm_g), (conv_b_b, m_conv_b_b, v_conv_b_b), (ln_b_g, m_ln_b_g, v_ln_b_g),
               (ln_b_b, m_ln_b_b, v_ln_b_b), (mem_norm_g, m_mem_norm_g, v_mem_norm_g),
               (final_g2, m_final_g.reshape(1, D), v_final_g.reshape(1, D))]
    out = _adamw_small(gathered_small, k_arr, vectors, (conv_a_w[0], m_conv_a_w[0], v_conv_a_w[0]),
                       (conv_b_w[0], m_conv_b_w[0], v_conv_b_w[0]))
    loss = out[0][0, 0]
    names = ["norm_g", "conv_b_b", "ln_b_g", "ln_b_b", "mem_norm_g", "final_g", "conv_a_w", "conv_b_w"]
    for n, name in enumerate(names):
        four = out[1 + 4 * n:5 + 4 * n]
        if name == "final_g":
            four = [r.reshape(D) for r in four]
        elif name.startswith("conv_") and name.endswith("_w"):
            four = [r[None] for r in four]
        res[name] = tuple(four)

    order = ["norm_g", "w_in", "conv_a_w", "w_out_a", "conv_b_w", "conv_b_b", "ln_b_g", "ln_b_b", "w_out_b",
             "mem_norm_g", "w_kv", "w_out_x", "w_o", "final_g"]
    return (loss, gx[None], *[res[n][0] for n in order], *[res[n][1] for n in order],
            *[res[n][2] for n in order], *[res[n][3] for n in order])
```

```python
import functools

import jax
import jax.numpy as jnp
from jax import lax
from jax.experimental import pallas as pl
from jax.experimental.pallas import tpu as pltpu

F32, BF16 = jnp.float32, jnp.bfloat16
D = 1024
N_DEV = 8
N_HEADS = 4
HEAD_DIM = D // N_HEADS
N_GROUPS = 12
W_IN_SHARD = N_GROUPS * D // N_DEV
UNIT = 512
X_GRAD_K = 2 * D
X_GRAD_PIECES = 4
K_A, K_B = 3, 31
EPS = 1e-6
HALO = 32
SUBLANES = 8
LANES = 128
LANE_GROUPS = D // LANES
UNROLL_A, UNROLL_B = 8, 4
WGRAD_ROWS = 256
WGRAD_TAPS = 16
CONV_PARTIAL_SUMS = 4
VMEM_LIMIT = 56 << 20
MESH = pl.DeviceIdType.MESH
ANY = pl.BlockSpec(memory_space=pl.ANY)
HBM = pl.BlockSpec(memory_space=pltpu.HBM)
SEM = pl.BlockSpec(memory_space=pltpu.SEMAPHORE)
SIDE_EFFECT = pltpu.SideEffectType.DATAFLOW_SIDE_EFFECTING

G_BA, G_CA, G_XA, G_ZA, G_VB, G_GB, G_ZB, G_Q, G_ZX, G_GA, G_GBB, G_GX = range(N_GROUPS)
DP_POS = (0, 1, 2, 3, 6, 7, 8, 4, 5, 9, 10, 11)

ADAM_LR, ADAM_B1, ADAM_B2, ADAM_EPS, ADAM_WD, ADAM_STEP = 0.001, 0.9, 0.999, 1e-08, 0.01, 10

SV_NORM_G, SV_CONV_B_B, SV_LN_G, SV_LN_B, SV_MEM_G, SV_FINAL_G, SV_LOSS = range(7)
SV_CONV_A, SV_CONV_B, SV_ROWS = 8, 16, 48


def _dot(a, b):
    return jnp.dot(a, b, preferred_element_type=F32)


def _dot_nt(a, b):
    return lax.dot_general(a, b, (((1,), (1,)), ((), ())), preferred_element_type=F32)


def _dot_tn(a, b):
    return lax.dot_general(a, b, (((0,), (0,)), ((), ())), preferred_element_type=F32)


def _silu_and_grad(z):
    s = jax.nn.sigmoid(z)
    return z * s, s * (1.0 + z * (1.0 - s))


def _fold8(a):
    return a.reshape(a.shape[0] // SUBLANES, SUBLANES, a.shape[1]).sum(axis=0)


def _mean(a):
    return jnp.mean(a, axis=-1, keepdims=True)


def _f32(ref):
    return ref[...].astype(F32)


def _params(n_grid):
    return pltpu.CompilerParams(dimension_semantics=("arbitrary",) * n_grid, vmem_limit_bytes=VMEM_LIMIT)


def _rows(tm, col=0):
    return pl.BlockSpec((tm, D), lambda i: (i, col))


def _prev_halo(tm, col=0):
    return pl.BlockSpec((HALO, D), lambda i: (jnp.maximum(i * (tm // HALO) - 1, 0), col))


def _next_halo(tm, n_rows, col=0):
    last = n_rows // HALO - 1
    return pl.BlockSpec((HALO, D), lambda i: (jnp.minimum((i + 1) * (tm // HALO), last), col))


def _const(shape):
    return pl.BlockSpec(shape, lambda *_: (0,) * len(shape))


def _w_out_spec(which):
    return pl.BlockSpec((N_DEV, None, D // N_DEV, D), lambda *_: (0, which, 0, 0))


def _to_time_major(t_ref, row0, x):
    n = x.shape[0]
    for j in range(LANE_GROUPS):
        t_ref[pl.ds(row0 * LANE_GROUPS + j, n, stride=LANE_GROUPS), :] = x[:, j * LANES:(j + 1) * LANES]


def _from_time_major(t_ref, n):
    return jnp.concatenate([t_ref[pl.ds(j, n, stride=LANE_GROUPS), :] for j in range(LANE_GROUPS)], axis=-1)


def _row(ref, t):
    start = t * LANE_GROUPS
    if not isinstance(start, int):
        start = pl.multiple_of(start, LANE_GROUPS)
    return ref[pl.ds(start, LANE_GROUPS), :]


def _conv(o_ref, e_ref, w_ref, taps, n_rows, unroll, bias_ref=None):
    weights = [_row(w_ref, k) for k, _ in taps]

    def chunk(c, carry):
        t0 = c * unroll
        n_part = min(CONV_PARTIAL_SUMS, len(taps))
        acc = [[None] * n_part for _ in range(unroll)]
        for n, (_, off) in enumerate(taps):
            for u in range(unroll):
                term = weights[n] * _row(e_ref, t0 + off + u)
                acc[u][n % n_part] = term if acc[u][n % n_part] is None else acc[u][n % n_part] + term
        for u in range(unroll):
            parts = acc[u]
            while len(parts) > 1:
                parts = [parts[n] + parts[n + 1] for n in range(0, len(parts) - 1, 2)] + parts[len(parts) & ~1:]
            out = parts[0]
            if bias_ref is not None:
                out = out + bias_ref[...]
            o_ref[pl.ds(pl.multiple_of((t0 + u) * LANE_GROUPS, LANE_GROUPS), LANE_GROUPS), :] = out
        return carry

    lax.fori_loop(0, n_rows // unroll, chunk, 0)


def _conv_wgrad(dw_ref, d_ref, e_ref, taps, n_rows, unroll):
    for first in range(0, len(taps), WGRAD_TAPS):
        group = taps[first:first + WGRAD_TAPS]
        lo = min(off for _, off in group)
        hi = max(off for _, off in group)

        def chunk(c, accs, group=group, lo=lo, hi=hi):
            t0 = c * unroll
            d = [_row(d_ref, t0 + u) for u in range(unroll)]
            window = [_row(e_ref, t0 + lo + n) for n in range(hi - lo + unroll)]
            accs = list(accs)
            for n, (_, off) in enumerate(group):
                for u in range(unroll):
                    accs[n] = accs[n] + d[u] * window[off - lo + u]
            return tuple(accs)

        zeros = tuple(jnp.zeros((LANE_GROUPS, LANES), F32) for _ in group)
        accs = lax.fori_loop(0, n_rows // unroll, chunk, zeros)
        for (k, _), acc in zip(group, accs):
            dw_ref[pl.ds(k * LANE_GROUPS, LANE_GROUPS), :] += acc


FWD_TAPS_A = [(k, HALO - (K_A - 1) + k) for k in range(K_A)]
BWD_TAPS_A = [(k, K_A - 1 - k) for k in range(K_A)]
FWD_TAPS_B = [(k, HALO - (K_B - 1) + k) for k in range(K_B)]
BWD_TAPS_B = [(k, K_B - 1 - k) for k in range(K_B)]


def _time_major(n_rows):
    return pltpu.VMEM((n_rows * LANE_GROUPS, LANES), F32)


def _kv_fwd(mem, mem_g, wkv_g):
    m_len = mem.shape[0]

    def body(mem_ref, g_ref, w_ref, kv_ref, mn_ref):
        mf = mem_ref[...]
        r = lax.rsqrt(_mean(mf * mf) + EPS)
        mn = ((mf * r) * g_ref[...]).astype(BF16)
        mn_ref[...] = mn
        for b in range(2 * N_HEADS):
            kv_ref[b] = _dot(mn, w_ref[b]).astype(BF16)

    return pl.pallas_call(
        body, name="kv_fwd", grid=(1,),
        in_specs=[_const((m_len, D)), _const((1, D)), _const((2 * N_HEADS, D, HEAD_DIM))],
        out_specs=[_const((2 * N_HEADS, m_len, HEAD_DIM)), _const((m_len, D))],
        out_shape=[jax.ShapeDtypeStruct((2 * N_HEADS, m_len, HEAD_DIM), BF16), jax.ShapeDtypeStruct((m_len, D), BF16)],
        compiler_params=_params(1),
    )(mem, mem_g, wkv_g)


def _kv_bwd(dkv, mem, mem_g, mn16, wkv_g):
    m_len = mem.shape[0]

    def body(dkv_ref, mem_ref, g_ref, mn_ref, w_ref, dw_ref, dg_ref):
        mn = mn_ref[...]
        dmn = jnp.zeros((m_len, D), F32)
        for b in range(2 * N_HEADS):
            d16 = dkv_ref[b].astype(BF16)
            dw_ref[b] = _dot_tn(mn, d16)
            dmn = dmn + _dot_nt(d16, w_ref[b])
        mf = mem_ref[...]
        r = lax.rsqrt(_mean(mf * mf) + EPS)
        dg_ref[...] = _fold8(dmn * (mf * r))

    return pl.pallas_call(
        body, name="kv_bwd", grid=(1,),
        in_specs=[_const((2 * N_HEADS, m_len, HEAD_DIM)), _const((m_len, D)), _const((1, D)), _const((m_len, D)),
                  _const((2 * N_HEADS, D, HEAD_DIM))],
        out_specs=[_const((2 * N_HEADS, D, HEAD_DIM)), _const((SUBLANES, D))],
        out_shape=[jax.ShapeDtypeStruct((2 * N_HEADS, D, HEAD_DIM), F32), jax.ShapeDtypeStruct((SUBLANES, D), F32)],
        compiler_params=_params(1),
    )(dkv, mem, mem_g, mn16, wkv_g)


def _rmsnorm_fwd(x, norm_g, tm):
    n_rows = x.shape[0]

    def body(x_ref, g_ref, u_ref, ut_ref):
        xf = x_ref[...]
        u = (xf * lax.rsqrt(_mean(xf * xf) + EPS)) * g_ref[...]
        u_ref[...] = u.astype(BF16)
        ut_ref[...] = u.T.astype(BF16)

    return pl.pallas_call(
        body, name="rmsnorm_fwd", grid=(n_rows // tm,),
        in_specs=[_rows(tm), _const((1, D))],
        out_specs=[_rows(tm), pl.BlockSpec((D, tm), lambda i: (0, i))],
        out_shape=[jax.ShapeDtypeStruct((n_rows, D), BF16), jax.ShapeDtypeStruct((D, n_rows), BF16)],
        compiler_params=_params(1),
    )(x, norm_g)


def _place():
    x, y, c = lax.axis_index("x"), lax.axis_index("y"), lax.axis_index("c")
    other_chips = [(1 - x, y), (x, 1 - y), (1 - x, 1 - y)]
    return x, y, c, other_chips


def _arrival_order():
    x, y, c, chips = _place()
    order = [4 * x + 2 * y + c, 4 * x + 2 * y + 1 - c]
    for px, py in chips:
        order += [4 * px + 2 * py + c, 4 * px + 2 * py + 1 - c]
    return order


def _proj_fwd_gather(u16, blocks, tm):
    n = len(blocks)
    n_rows = u16.shape[0]
    n_i = n_rows // tm
    per_shard = W_IN_SHARD // UNIT
    assert n_i >= per_shard

    def wt_index(p, i, order):
        return (_dp_unit(per_shard * order[p] + jnp.minimum(i, per_shard - 1)), 0)

    def body(order_ref, u_ref, *refs):
        src, proj_ref, wt_ref, out = refs[:n], refs[n], refs[n + 1], refs[n + 2:2 * n + 2]
        wbuf, stage_sem, send, recv, own_sem = refs[2 * n + 2:]
        p, i = pl.program_id(0), pl.program_id(1)
        x, y, c, chips = _place()
        me, sibling = 4 * x + 2 * y + c, (x, y, 1 - c)

        def copy(t, k, block, to, from_input=False):
            return pltpu.make_async_remote_copy(
                src_ref=src[t] if from_input else out[t].at[block], dst_ref=out[t].at[block],
                send_sem=send.at[t, k], recv_sem=recv.at[t, k], device_id=to, device_id_type=MESH)

        def own_copies():
            return [pltpu.make_async_copy(src[t], out[t].at[me], own_sem.at[t]) for t in range(n)]

        def first_copies():
            first = []
            for t in range(n):
                first.append(copy(t, 0, me, sibling, from_input=True))
                first += [copy(t, 1 + j, me, (*chip, c), from_input=True) for j, chip in enumerate(chips)]
            return first

        def stage(slot, block):
            return pltpu.make_async_copy(out[0].at[block], wbuf.at[slot], stage_sem.at[slot])

        @pl.when((p == 0) & (i == 0))
        def _():
            for cp in own_copies() + first_copies():
                cp.start()
            mine = pltpu.make_async_copy(src[0], wbuf.at[0], stage_sem.at[0])
            mine.start()
            mine.wait()

        @pl.when((p > 0) & (i == 0))
        def _():
            stage(p % 2, order_ref[p]).wait()

        proj_ref[...] = _dot(u_ref[...], wbuf[p % 2])
        for r in range(per_shard):
            @pl.when(i == r)
            def _(r=r):
                wt_ref[...] = wbuf[p % 2, :, r * UNIT:(r + 1) * UNIT].astype(F32).T.astype(BF16)

        for nxt in range(1, N_DEV):
            @pl.when((p == nxt - 1) & (i == n_i - 1))
            def _(nxt=nxt):
                if nxt == 1:
                    block = 4 * x + 2 * y + 1 - c
                    copy(0, 0, block, sibling).wait_recv()
                else:
                    j, passed_on = divmod(nxt - 2, 2)
                    px, py = chips[j]
                    if passed_on:
                        block = 4 * px + 2 * py + 1 - c
                        copy(0, 4 + j, block, sibling).wait_recv()
                    else:
                        block = 4 * px + 2 * py + c
                        copy(0, 1 + j, block, sibling).wait_recv()
                        copy(0, 4 + j, block, sibling).start()
                stage(nxt % 2, block).start()

        @pl.when((p == N_DEV - 1) & (i == n_i - 1))
        def _():
            passed = [copy(0, 4 + j, 4 * px + 2 * py + c, sibling) for j, (px, py) in enumerate(chips)]
            for j, (px, py) in enumerate(chips):
                for t in range(1, n):
                    block = 4 * px + 2 * py + c
                    copy(t, 1 + j, block, sibling).wait_recv()
                    passed.append(copy(t, 4 + j, block, sibling))
                    passed[-1].start()
            for t in range(1, n):
                copy(t, 0, 4 * x + 2 * y + 1 - c, sibling).wait_recv()
                for j, (px, py) in enumerate(chips):
                    copy(t, 4 + j, 4 * px + 2 * py + 1 - c, sibling).wait_recv()
            for cp in first_copies() + passed:
                cp.wait_send()
            for cp in own_copies():
                cp.wait()

    return pl.pallas_call(
        body, name="proj_fwd_gather",
        grid_spec=pltpu.PrefetchScalarGridSpec(
            num_scalar_prefetch=1, grid=(N_DEV, n_i),
            in_specs=[pl.BlockSpec((tm, D), lambda p, i, order: (i, 0))] + [ANY] * n,
            out_specs=[pl.BlockSpec((tm, W_IN_SHARD), lambda p, i, order: (i, order[p])),
                       pl.BlockSpec((UNIT, D), wt_index)] + [ANY] * n,
            scratch_shapes=[pltpu.VMEM((2, D, W_IN_SHARD), BF16), pltpu.SemaphoreType.DMA((2,)),
                            pltpu.SemaphoreType.DMA((n, 7)), pltpu.SemaphoreType.DMA((n, 7)),
                            pltpu.SemaphoreType.DMA((n,))]),
        out_shape=[jax.ShapeDtypeStruct((n_rows, N_GROUPS * D), F32), jax.ShapeDtypeStruct((N_GROUPS * D, D), BF16)]
        + [jax.ShapeDtypeStruct((N_DEV, *b.shape), b.dtype) for b in blocks],
        compiler_params=_params(2),
    )(jnp.stack(_arrival_order()).astype(jnp.int32), u16, *blocks)


def _branch_a_fwd(proj, wo4_g, cw_a, tm):
    n_rows = proj.shape[0]

    def body(bp, cp, xp, za, cph, xph, w_ref, cw_ref, sa_ref, ya_ref, e_scr, o_scr):
        i = pl.program_id(0)
        bp, cp, xp, za, cph, xph = (_f32(r) for r in (bp, cp, xp, za, cph, xph))
        _to_time_major(e_scr, 0, jnp.where(i > 0, cph[...] * xph[...], 0.0))
        _to_time_major(e_scr, HALO, cp[...] * xp[...])
        _conv(o_scr, e_scr, cw_ref, FWD_TAPS_A, tm, UNROLL_A)
        sa = (jax.nn.silu(za[...]) * (bp[...] * _from_time_major(o_scr, tm))).astype(BF16)
        sa_ref[...] = sa
        ya_ref[...] = _dot(sa, w_ref[...].reshape(D, D))

    return pl.pallas_call(
        body, name="branch_a_fwd", grid=(n_rows // tm,),
        in_specs=[_rows(tm, G_BA), _rows(tm, G_CA), _rows(tm, G_XA), _rows(tm, G_ZA),
                  _prev_halo(tm, G_CA), _prev_halo(tm, G_XA), _w_out_spec(0), _const(cw_a.shape)],
        out_specs=[_rows(tm), _rows(tm)],
        out_shape=[jax.ShapeDtypeStruct((n_rows, D), BF16), jax.ShapeDtypeStruct((n_rows, D), F32)],
        scratch_shapes=[_time_major(tm + HALO), _time_major(tm)],
        compiler_params=_params(1),
    )(proj, proj, proj, proj, proj, proj, wo4_g, cw_a)


def _layernorm_parts(cb, lg, lb):
    xc = cb - _mean(cb)
    rstd = lax.rsqrt(_mean(xc * xc) + EPS)
    xhat = xc * rstd
    return xhat, rstd, xhat * lg + lb


def _branch_b_fwd(proj, wo4_g, cw_b, conv_b_b, ln_g, ln_b, tm):
    n_rows = proj.shape[0]

    def body(vb, gb, zb, vbh, gbh, w_ref, cw_ref, bb_ref, lg_ref, lb_ref, cb_ref, sb_ref, yb_ref, e_scr, o_scr):
        i = pl.program_id(0)
        vb, gb, zb, vbh, gbh = (_f32(r) for r in (vb, gb, zb, vbh, gbh))
        _to_time_major(e_scr, 0, jnp.where(i > 0, vbh[...] * jax.nn.sigmoid(gbh[...]), 0.0))
        _to_time_major(e_scr, HALO, vb[...] * jax.nn.sigmoid(gb[...]))
        _conv(o_scr, e_scr, cw_ref, FWD_TAPS_B, tm, UNROLL_B, bias_ref=bb_ref)
        cb = _from_time_major(o_scr, tm)
        cb_ref[...] = cb
        _, _, ln = _layernorm_parts(cb, lg_ref[...], lb_ref[...])
        sb = (jax.nn.silu(zb[...]) * jax.nn.silu(ln)).astype(BF16)
        sb_ref[...] = sb
        yb_ref[...] = _dot(sb, w_ref[...].reshape(D, D))

    return pl.pallas_call(
        body, name="branch_b_fwd", grid=(n_rows // tm,),
        in_specs=[_rows(tm, G_VB), _rows(tm, G_GB), _rows(tm, G_ZB), _prev_halo(tm, G_VB), _prev_halo(tm, G_GB),
                  _w_out_spec(1), _const(cw_b.shape), _const((LANE_GROUPS, LANES)), _const((1, D)), _const((1, D))],
        out_specs=[_rows(tm), _rows(tm), _rows(tm)],
        out_shape=[jax.ShapeDtypeStruct((n_rows, D), F32), jax.ShapeDtypeStruct((n_rows, D), BF16),
                   jax.ShapeDtypeStruct((n_rows, D), F32)],
        scratch_shapes=[_time_major(tm + HALO), _time_major(tm)],
        compiler_params=_params(1),
    )(proj, proj, proj, proj, proj, wo4_g, cw_b, conv_b_b.reshape(LANE_GROUPS, LANES), ln_g, ln_b)


def _attention(q16, kv_ref):
    probs, outs = [], []
    for h in range(N_HEADS):
        s = _dot_nt(q16[:, h * HEAD_DIM:(h + 1) * HEAD_DIM], kv_ref[h]) * (HEAD_DIM ** -0.5)
        e = jnp.exp(s - jnp.max(s, axis=-1, keepdims=True))
        p = e / jnp.sum(e, axis=-1, keepdims=True)
        probs.append(p)
        outs.append(_dot(p.astype(BF16), kv_ref[N_HEADS + h]))
    return probs, outs


def _branch_x_fwd(proj, kv16, wo4_g, tm):
    n_rows = proj.shape[0]

    def body(q, zx, kv_ref, w_ref, sx_ref, yx_ref):
        _, outs = _attention(q[...].astype(BF16), kv_ref)
        sx = (jax.nn.silu(_f32(zx)) * jnp.concatenate(outs, axis=-1)).astype(BF16)
        sx_ref[...] = sx
        yx_ref[...] = _dot(sx, w_ref[...].reshape(D, D))

    return pl.pallas_call(
        body, name="branch_x_fwd", grid=(n_rows // tm,),
        in_specs=[_rows(tm, G_Q), _rows(tm, G_ZX), _const(kv16.shape), _w_out_spec(2)],
        out_specs=[_rows(tm), _rows(tm)],
        out_shape=[jax.ShapeDtypeStruct((n_rows, D), BF16), jax.ShapeDtypeStruct((n_rows, D), F32)],
        compiler_params=_params(1),
    )(proj, proj, kv16, wo4_g)


def _merge_fwd_bwd(proj, ya, yb, yx, x, target, wo4_g, final_g, tm):
    n_rows = proj.shape[0]
    inv_d = 1.0 / D

    def body(ga, gb, gx, ya_ref, yb_ref, yx_ref, x_ref, t_ref, w_ref, fg_ref,
             dh_ref, dya_ref, dyb_ref, dyx_ref, dp_ref, dw_ref, dfg_ref, sq_ref):
        i = pl.program_id(0)
        wo = w_ref[...].reshape(D, D)
        sig = [jax.nn.sigmoid(_f32(g)) for g in (ga, gb, gx)]
        ys = [ya_ref[...], yb_ref[...], yx_ref[...]]
        m16 = (sig[0] * ys[0] + sig[1] * ys[1] + sig[2] * ys[2]).astype(BF16)
        h = x_ref[...] + _dot(m16, wo)
        r = lax.rsqrt(_mean(h * h) + EPS)
        hn = h * r
        fg = fg_ref[...]
        err = hn * fg - t_ref[...]
        dy = err * inv_d
        dhn = dy * fg
        dh = r * (dhn - hn * _mean(dhn * hn))
        dh_ref[...] = dh
        dh16 = dh.astype(BF16)
        dm = _dot_nt(dh16, wo)
        for n, out in enumerate((dya_ref, dyb_ref, dyx_ref)):
            out[...] = (sig[n] * dm).astype(BF16)
            dp_ref[:, n * D:(n + 1) * D] = (dm * ys[n] * (sig[n] * (1.0 - sig[n]))).astype(BF16)

        @pl.when(i == 0)
        def _():
            dw_ref[...] = jnp.zeros_like(dw_ref)
            dfg_ref[...] = jnp.zeros_like(dfg_ref)
            sq_ref[...] = jnp.zeros_like(sq_ref)

        dw_ref[0] += _dot_tn(m16, dh16)
        dfg_ref[...] += _fold8(dy * hn)
        sq_ref[...] += _fold8(err * err)

    vec = jax.ShapeDtypeStruct((SUBLANES, D), F32)
    return pl.pallas_call(
        body, name="merge_fwd_bwd", grid=(n_rows // tm,),
        in_specs=[_rows(tm, G_GA), _rows(tm, G_GBB), _rows(tm, G_GX), _rows(tm), _rows(tm), _rows(tm), _rows(tm),
                  _rows(tm), _w_out_spec(3), _const((1, D))],
        out_specs=[_rows(tm), _rows(tm), _rows(tm), _rows(tm), pl.BlockSpec((tm, 3 * D), lambda i: (i, 3)),
                   pl.BlockSpec((1, D, D), lambda i: (3, 0, 0)), _const((SUBLANES, D)), _const((SUBLANES, D))],
        out_shape=[jax.ShapeDtypeStruct((n_rows, D), F32), jax.ShapeDtypeStruct((n_rows, D), BF16),
                   jax.ShapeDtypeStruct((n_rows, D), BF16), jax.ShapeDtypeStruct((n_rows, D), BF16),
                   jax.ShapeDtypeStruct((n_rows, N_GROUPS * D), BF16), jax.ShapeDtypeStruct((4, D, D), F32), vec, vec],
        compiler_params=_params(1),
    )(proj, proj, proj, ya, yb, yx, x, target, wo4_g, final_g)


def _branch_a_bwd(dya, proj, sa16, wo4_g, cw_a, dproj, dw4, tm):
    n_rows = proj.shape[0]
    n_tiles = n_rows // tm

    def body(dya_ref, bp, cp, xp, za, sa_ref, dyan, bpn, zan, cph, xph, w_ref, cw_ref, dp_in, dw_in,
             dp_ref, dw_ref, dwa_ref, e1, e2, o_scr):
        del dp_in, dw_in
        i = pl.program_id(0)
        bp, cp, xp, za, bpn, zan, cph, xph = (_f32(r) for r in (bp, cp, xp, za, bpn, zan, cph, xph))
        woa = w_ref[...].reshape(D, D)
        dya16 = dya_ref[...]
        _to_time_major(e1, 0, jnp.where(i > 0, cph[...] * xph[...], 0.0))
        _to_time_major(e1, HALO, cp[...] * xp[...])
        _conv(o_scr, e1, cw_ref, FWD_TAPS_A, tm, UNROLL_A)
        ca = _from_time_major(o_scr, tm)
        dsa = _dot_nt(dya16, woa)
        silu_z, dsilu_z = _silu_and_grad(za[...])
        t = dsa * silu_z
        dp_ref[:, 0 * D:1 * D] = (t * ca).astype(BF16)
        dp_ref[:, 3 * D:4 * D] = (dsa * (bp[...] * ca) * dsilu_z).astype(BF16)
        dsan = _dot_nt(dyan[...], woa)
        dcan = (dsan * jax.nn.silu(zan[...])) * bpn[...]
        _to_time_major(e2, 0, t * bp[...])
        _to_time_major(e2, tm, jnp.where(i < n_tiles - 1, dcan, 0.0))

        @pl.when(i == 0)
        def _():
            dw_ref[...] = jnp.zeros_like(dw_ref)
            dwa_ref[...] = jnp.zeros_like(dwa_ref)

        _conv_wgrad(dwa_ref, e2, e1, FWD_TAPS_A, tm, UNROLL_A)
        dw_ref[0] += _dot_tn(sa_ref[...], dya16)
        _conv(o_scr, e2, cw_ref, BWD_TAPS_A, tm, UNROLL_A)
        dprod = _from_time_major(o_scr, tm)
        dp_ref[:, 1 * D:2 * D] = (dprod * xp[...]).astype(BF16)
        dp_ref[:, 2 * D:3 * D] = (dprod * cp[...]).astype(BF16)

    return pl.pallas_call(
        body, name="branch_a_bwd", grid=(n_tiles,),
        in_specs=[_rows(tm), _rows(tm, G_BA), _rows(tm, G_CA), _rows(tm, G_XA), _rows(tm, G_ZA), _rows(tm),
                  _next_halo(tm, n_rows), _next_halo(tm, n_rows, G_BA), _next_halo(tm, n_rows, G_ZA),
                  _prev_halo(tm, G_CA), _prev_halo(tm, G_XA), _w_out_spec(0), _const(cw_a.shape), ANY, ANY],
        out_specs=[pl.BlockSpec((tm, 4 * D), lambda i: (i, 0)), pl.BlockSpec((1, D, D), lambda i: (0, 0, 0)),
                   _const((K_A * LANE_GROUPS, LANES))],
        out_shape=[jax.ShapeDtypeStruct(dproj.shape, BF16), jax.ShapeDtypeStruct(dw4.shape, F32),
                   jax.ShapeDtypeStruct((K_A * LANE_GROUPS, LANES), F32)],
        input_output_aliases={13: 0, 14: 1},
        scratch_shapes=[_time_major(tm + HALO), _time_major(tm + HALO), _time_major(tm)],
        compiler_params=_params(1),
    )(dya, proj, proj, proj, proj, sa16, dya, proj, proj, proj, proj, wo4_g, cw_a, dproj, dw4)


def _branch_b_bwd(dyb, proj, cb, sb16, wo4_g, cw_b, ln_g, ln_b, dproj, dw4, tm):
    n_rows = proj.shape[0]
    n_tiles = n_rows // tm

    def body(dyb_ref, zb, cb_ref, vb, gb, sb_ref, dybn, zbn, cbn, w_ref, cw_ref, lg_ref, lb_ref,
             dp_in, dw_in, dp_ref, dw_ref, dcb_out, glu_out, dbb_ref, dlg_ref, dlb_ref, e2, o_scr):
        del dp_in, dw_in
        vb, gb, zbn = (_f32(r) for r in (vb, gb, zbn))
        i = pl.program_id(0)
        wob = w_ref[...].reshape(D, D)
        lg, lb = lg_ref[...], lb_ref[...]

        def conv_out_grad(dy16, z, c):
            dsb = _dot_nt(dy16, wob)
            xhat, rstd, ln = _layernorm_parts(c, lg, lb)
            sw, dsw = _silu_and_grad(ln)
            sz, dsz = _silu_and_grad(z)
            dln = (dsb * sz) * dsw
            dxhat = dln * lg
            dcb = rstd * (dxhat - _mean(dxhat) - xhat * _mean(dxhat * xhat))
            return dsb * sw * dsz, dln, xhat, dcb

        @pl.when(i == 0)
        def _():
            dw_ref[...] = jnp.zeros_like(dw_ref)
            dbb_ref[...] = jnp.zeros_like(dbb_ref)
            dlg_ref[...] = jnp.zeros_like(dlg_ref)
            dlb_ref[...] = jnp.zeros_like(dlb_ref)

        dyb16 = dyb_ref[...]
        dzb, dln, xhat, dcb = conv_out_grad(dyb16, _f32(zb), cb_ref[...])
        dp_ref[:, 2 * D:3 * D] = dzb.astype(BF16)
        _, _, _, dcbn = conv_out_grad(dybn[...], zbn[...], cbn[...])
        _to_time_major(e2, 0, dcb)
        _to_time_major(e2, tm, jnp.where(i < n_tiles - 1, dcbn, 0.0))
        dlg_ref[...] += _fold8(dln * xhat)
        dlb_ref[...] += _fold8(dln)
        dbb_ref[...] += _fold8(dcb)
        dw_ref[0] += _dot_tn(sb_ref[...], dyb16)
        dcb_out[...] = e2[pl.ds(0, tm * LANE_GROUPS), :]
        sg = jax.nn.sigmoid(gb[...])
        _to_time_major(glu_out, 0, vb[...] * sg)
        _conv(o_scr, e2, cw_ref, BWD_TAPS_B, tm, UNROLL_B)
        dglu = _from_time_major(o_scr, tm)
        dp_ref[:, 0 * D:1 * D] = (dglu * sg).astype(BF16)
        dp_ref[:, 1 * D:2 * D] = (dglu * vb[...] * (sg * (1.0 - sg))).astype(BF16)

    vec = jax.ShapeDtypeStruct((SUBLANES, D), F32)
    rows_tm = pl.BlockSpec((tm * LANE_GROUPS, LANES), lambda i: (i, 0))
    return pl.pallas_call(
        body, name="branch_b_bwd", grid=(n_tiles,),
        in_specs=[_rows(tm), _rows(tm, G_ZB), _rows(tm), _rows(tm, G_VB), _rows(tm, G_GB), _rows(tm),
                  _next_halo(tm, n_rows), _next_halo(tm, n_rows, G_ZB), _next_halo(tm, n_rows),
                  _w_out_spec(1), _const(cw_b.shape), _const((1, D)), _const((1, D)), ANY, ANY],
        out_specs=[pl.BlockSpec((tm, 3 * D), lambda i: (i, 2)), pl.BlockSpec((1, D, D), lambda i: (1, 0, 0)),
                   rows_tm, rows_tm, _const((SUBLANES, D)), _const((SUBLANES, D)), _const((SUBLANES, D))],
        out_shape=[jax.ShapeDtypeStruct(dproj.shape, BF16), jax.ShapeDtypeStruct(dw4.shape, F32),
                   jax.ShapeDtypeStruct((n_rows * LANE_GROUPS, LANES), F32),
                   jax.ShapeDtypeStruct((n_rows * LANE_GROUPS, LANES), F32), vec, vec, vec],
        input_output_aliases={13: 0, 14: 1},
        scratch_shapes=[_time_major(tm + HALO), _time_major(tm)],
        compiler_params=_params(1),
    )(dyb, proj, cb, proj, proj, sb16, dyb, proj, cb, wo4_g, cw_b, ln_g, ln_b, dproj, dw4)


def _branch_x_bwd(dyx, proj, sx16, kv16, wo4_g, dproj, dw4, tm):
    n_rows = proj.shape[0]
    scale = HEAD_DIM ** -0.5

    def body(dyx_ref, q, zx, sx_ref, kv_ref, w_ref, dp_in, dw_in, dp_ref, dw_ref, dkv_ref):
        del dp_in, dw_in
        i = pl.program_id(0)
        dyx16 = dyx_ref[...]
        q16 = q[...].astype(BF16)
        probs, outs = _attention(q16, kv_ref)
        dsx = _dot_nt(dyx16, w_ref[...].reshape(D, D))
        silu_z, dsilu_z = _silu_and_grad(_f32(zx))
        dp_ref[:, D:2 * D] = (dsx * jnp.concatenate(outs, axis=-1) * dsilu_z).astype(BF16)
        do16 = (dsx * silu_z).astype(BF16)

        @pl.when(i == 0)
        def _():
            dw_ref[...] = jnp.zeros_like(dw_ref)
            dkv_ref[...] = jnp.zeros_like(dkv_ref)

        for h in range(N_HEADS):
            cols = slice(h * HEAD_DIM, (h + 1) * HEAD_DIM)
            p = probs[h]
            dprob = _dot_nt(do16[:, cols], kv_ref[N_HEADS + h])
            ds16 = ((p * (dprob - jnp.sum(p * dprob, axis=-1, keepdims=True))) * scale).astype(BF16)
            dp_ref[:, cols] = _dot(ds16, kv_ref[h]).astype(BF16)
            dkv_ref[h] += _dot_tn(ds16, q16[:, cols])
            dkv_ref[N_HEADS + h] += _dot_tn(p.astype(BF16), do16[:, cols])
        dw_ref[0] += _dot_tn(sx_ref[...], dyx16)

    return pl.pallas_call(
        body, name="branch_x_bwd", grid=(n_rows // tm,),
        in_specs=[_rows(tm), _rows(tm, G_Q), _rows(tm, G_ZX), _rows(tm), _const(kv16.shape), _w_out_spec(2), ANY, ANY],
        out_specs=[pl.BlockSpec((tm, 2 * D), lambda i: (i, 2)), pl.BlockSpec((1, D, D), lambda i: (2, 0, 0)),
                   _const(kv16.shape)],
        out_shape=[jax.ShapeDtypeStruct(dproj.shape, BF16), jax.ShapeDtypeStruct(dw4.shape, F32),
                   jax.ShapeDtypeStruct(kv16.shape, F32)],
        input_output_aliases={6: 0, 7: 1},
        compiler_params=_params(1),
    )(dyx, proj, proj, sx16, kv16, wo4_g, dproj, dw4)


def _dp_unit(u):
    g = u // 2
    pos = jnp.where(g < G_VB, g, jnp.where(g < G_Q, g + 2, jnp.where(g < G_GA, g - 3, g)))
    return 2 * pos + u % 2


def _scatter_copies(srcs, lands, send, recv):
    x, y, c = lax.axis_index("x"), lax.axis_index("y"), lax.axis_index("c")
    copies = []
    for n in range(N_DEV - 1):
        flip = n + 1
        px = 1 - x if flip & 4 else x
        py = 1 - y if flip & 2 else y
        pc = 1 - c if flip & 1 else c
        for t, (src, land) in enumerate(zip(srcs, lands)):
            copies.append(pltpu.make_async_remote_copy(
                src_ref=src.at[4 * px + 2 * py + pc], dst_ref=land.at[n], send_sem=send.at[t * (N_DEV - 1) + n],
                recv_sem=recv.at[t * (N_DEV - 1) + n], device_id=(px, py, pc), device_id_type=MESH))
    return copies


def _scatter_start(name, arrays, views, n_views):
    n = len(arrays)
    lands = [lax.empty(tuple(N_DEV - 1 if d == N_DEV else d for d in a.shape), a.dtype) for a in arrays]

    def body(*refs):
        src, land, (send, recv) = refs[:n], refs[n:2 * n], refs[2 * n:2 * n + 2]
        token = refs[-1]
        for cp in _scatter_copies(views(src), views(land), send, recv):
            cp.start()
        token[...] = jnp.zeros_like(token)

    sems = pltpu.SemaphoreType.DMA((n_views * (N_DEV - 1),))
    out = pl.pallas_call(
        body, name=name,
        in_specs=[HBM] * (2 * n),
        out_specs=[SEM, SEM] + [HBM] * (2 * n) + [pl.BlockSpec(memory_space=pltpu.VMEM)],
        out_shape=[sems, sems] + [pltpu.HBM(a.shape, a.dtype) for a in arrays + lands]
        + [jax.ShapeDtypeStruct((SUBLANES, LANES), F32)],
        input_output_aliases={k: 2 + k for k in range(2 * n)},
        compiler_params=pltpu.CompilerParams(has_side_effects=SIDE_EFFECT),
    )(*[pltpu.with_memory_space_constraint(a, pltpu.HBM) for a in arrays + lands])
    return dict(name=name, sems=out[:2], moving=out[2:2 + 2 * n], views=views, token=out[-1])


def _scatter_wait(started, after):
    n = len(started["moving"]) // 2
    views = started["views"]

    def body(*refs):
        src, land, (send, recv) = refs[:n], refs[n:2 * n], refs[2 * n:2 * n + 2]
        for cp in _scatter_copies(views(src), views(land), send, recv):
            cp.wait_send()
            cp.wait_recv()

    out = pl.pallas_call(
        body, name=started["name"].replace("start", "wait"),
        in_specs=[HBM] * (2 * n) + [SEM, SEM, ANY],
        out_specs=[HBM] * (2 * n),
        out_shape=[pltpu.HBM(a.shape, a.dtype) for a in started["moving"]],
        input_output_aliases={k: k for k in range(2 * n)},
        compiler_params=pltpu.CompilerParams(has_side_effects=SIDE_EFFECT),
    )(*started["moving"], *started["sems"], after)
    return out[n:]


def _w_in_grad(ut, dproj, tk, token):
    n_rows = dproj.shape[0]
    n_k = n_rows // tk
    per_shard = W_IN_SHARD // UNIT

    def body(ut_ref, dp0, dp1, dp2, token_ref, out_ref, out16_ref, acc):
        del token_ref
        t = pl.program_id(1)

        for r, dp_ref in enumerate((dp0, dp1, dp2)):
            cols = slice(r * UNIT, (r + 1) * UNIT)

            @pl.when(t == 0)
            def _(dp_ref=dp_ref, cols=cols):
                acc[:, cols] = _dot(ut_ref[...], dp_ref[...])

            @pl.when(t > 0)
            def _(dp_ref=dp_ref, cols=cols):
                acc[:, cols] += _dot(ut_ref[...], dp_ref[...])

        @pl.when(t == n_k - 1)
        def _():
            out_ref[0] = acc[...]
            out16_ref[0] = acc[...].astype(BF16)

    def dp_spec(r):
        return pl.BlockSpec((tk, UNIT), lambda q, t: (t, _dp_unit(per_shard * q + r)))

    shard = pl.BlockSpec((1, D, W_IN_SHARD), lambda q, t: (q, 0, 0))
    return pl.pallas_call(
        body, name="w_in_grad", grid=(N_DEV, n_k),
        in_specs=[pl.BlockSpec((D, tk), lambda q, t: (0, t)), dp_spec(0), dp_spec(1), dp_spec(2), ANY],
        out_specs=[shard, shard],
        out_shape=[jax.ShapeDtypeStruct((N_DEV, D, W_IN_SHARD), F32), jax.ShapeDtypeStruct((N_DEV, D, W_IN_SHARD), BF16)],
        scratch_shapes=[pltpu.VMEM((D, W_IN_SHARD), F32)],
        compiler_params=_params(2),
    )(ut, dproj, dproj, dproj, token)


def _x_grad(dproj, win_t, x, dh, norm_g, token, dcb_tm, glu_tm, tm):
    n_rows = x.shape[0]
    n_k = N_GROUPS * D // X_GRAD_K
    n_w = n_rows // WGRAD_ROWS
    assert n_w <= (n_rows // tm) * n_k

    def body(dp_ref, wt_ref, x_ref, dh_ref, g_ref, token_ref, d_ref, e_ref, eh_ref, gx_ref, dg_ref, dwb_ref, acc):
        del token_ref
        i, g = pl.program_id(0), pl.program_id(1)
        step = i * n_k + g

        @pl.when(step == 0)
        def _():
            dg_ref[...] = jnp.zeros_like(dg_ref)
            dwb_ref[...] = jnp.zeros_like(dwb_ref)

        @pl.when(g == 0)
        def _():
            acc[...] = jnp.zeros_like(acc)

        def conv_weight_grad(rows):
            for first in range(0, K_B, WGRAD_TAPS):
                taps = FWD_TAPS_B[first:first + WGRAD_TAPS]
                sums = [_row(dwb_ref, k) for k, _ in taps]
                window = {}
                for t in rows:
                    d = _row(d_ref, t)
                    for n, (_, off) in enumerate(taps):
                        r = t + off
                        if r not in window:
                            window[r] = (jnp.where(step > 0, _row(eh_ref, r), 0.0) if r < HALO
                                         else _row(e_ref, r - HALO))
                        sums[n] = sums[n] + d * window[r]
                    window.pop(t + taps[0][1], None)
                for (k, _), total in zip(taps, sums):
                    dwb_ref[pl.ds(k * LANE_GROUPS, LANE_GROUPS), :] = total

        def matmul_piece(piece):
            cols = slice(piece * D // X_GRAD_PIECES, (piece + 1) * D // X_GRAD_PIECES)
            acc[:, cols] += _dot(dp_ref[...], wt_ref[:, cols])

        @pl.when(step < n_w)
        def _():
            per_piece = WGRAD_ROWS // X_GRAD_PIECES
            for piece in range(X_GRAD_PIECES):
                conv_weight_grad(range(piece * per_piece, (piece + 1) * per_piece))
                matmul_piece(piece)

        @pl.when(step >= n_w)
        def _():
            for piece in range(X_GRAD_PIECES):
                matmul_piece(piece)

        @pl.when(g == n_k - 1)
        def _():
            du = acc[...]
            xf = x_ref[...]
            r = lax.rsqrt(_mean(xf * xf) + EPS)
            xn = xf * r
            dun = du * g_ref[...]
            gx_ref[...] = dh_ref[...] + r * (dun - xn * _mean(dun * xn))
            dg_ref[...] += _fold8(du * xn)

    def w_block(i, g):
        return jnp.minimum(i * n_k + g, n_w - 1)

    per_halo = WGRAD_ROWS // HALO
    return pl.pallas_call(
        body, name="x_grad", grid=(n_rows // tm, n_k),
        in_specs=[pl.BlockSpec((tm, X_GRAD_K), lambda i, g: (i, g)), pl.BlockSpec((X_GRAD_K, D), lambda i, g: (g, 0)),
                  pl.BlockSpec((tm, D), lambda i, g: (i, 0)), pl.BlockSpec((tm, D), lambda i, g: (i, 0)),
                  _const((1, D)), ANY,
                  pl.BlockSpec((WGRAD_ROWS * LANE_GROUPS, LANES), lambda i, g: (w_block(i, g), 0)),
                  pl.BlockSpec((WGRAD_ROWS * LANE_GROUPS, LANES), lambda i, g: (w_block(i, g), 0)),
                  pl.BlockSpec((HALO * LANE_GROUPS, LANES),
                               lambda i, g: (jnp.maximum(w_block(i, g) * per_halo - 1, 0), 0))],
        out_specs=[pl.BlockSpec((tm, D), lambda i, g: (i, 0)), _const((SUBLANES, D)),
                   _const((K_B * LANE_GROUPS, LANES))],
        out_shape=[jax.ShapeDtypeStruct((n_rows, D), F32), jax.ShapeDtypeStruct((SUBLANES, D), F32),
                   jax.ShapeDtypeStruct((K_B * LANE_GROUPS, LANES), F32)],
        scratch_shapes=[pltpu.VMEM((tm, D), F32)],
        compiler_params=_params(2),
    )(dproj, win_t, x, dh, norm_g, token, dcb_tm, glu_tm, glu_tm)


def _local_step(x, mem, target, norm_g, conv_b_b, ln_g, ln_b, mem_g, final_g, shards):
    n_rows = x.shape[0]
    tm = min(512, n_rows)
    big = min(1024, n_rows)
    u16, ut = _rmsnorm_fwd(x, norm_g, big)
    proj, win_t, _, wkv_g, wo4_g, cw_g = _proj_fwd_gather(u16, shards, min(2048, n_rows))
    cw_rows = cw_g.transpose(1, 0, 2).reshape((SUBLANES + HALO) * LANE_GROUPS, LANES)
    cw_a, cw_b = cw_rows[:SUBLANES * LANE_GROUPS], cw_rows[SUBLANES * LANE_GROUPS:]
    kv16, mn16 = _kv_fwd(mem, mem_g, wkv_g)
    sa16, ya = _branch_a_fwd(proj, wo4_g, cw_a, tm)
    cb, sb16, yb = _branch_b_fwd(proj, wo4_g, cw_b, conv_b_b, ln_g, ln_b, tm)
    sx16, yx = _branch_x_fwd(proj, kv16, wo4_g, tm)
    dh, dya, dyb, dyx, dproj, dw4, dfg, sq = _merge_fwd_bwd(proj, ya, yb, yx, x, target, wo4_g, final_g,
                                                             min(256, n_rows))
    dproj, dw4, dwa = _branch_a_bwd(dya, proj, sa16, wo4_g, cw_a, dproj, dw4, tm)
    dproj, dw4, dcb_tm, glu_tm, dbb, dlg, dlb = _branch_b_bwd(dyb, proj, cb, sb16, wo4_g, cw_b, ln_g, ln_b, dproj,
                                                              dw4, min(256, n_rows))
    dproj, dw4, dkv = _branch_x_bwd(dyx, proj, sx16, kv16, wo4_g, dproj, dw4, tm)
    dwkv_g, dmg = _kv_bwd(dkv, mem, mem_g, mn16, wkv_g)
    dw4 = dw4.reshape(4, N_DEV, D // N_DEV, D)
    small_moving = _scatter_start("small_grads_start", [dw4.astype(BF16), dwkv_g.astype(BF16)],
                                  lambda refs: [refs[0].at[w] for w in range(4)] + [refs[1]], 5)
    dwin_g, dwin16 = _w_in_grad(ut, dproj, min(2048, n_rows), small_moving["token"])
    w_in_moving = _scatter_start("w_in_grad_start", [dwin16], lambda refs: list(refs), 1)
    gx, dng, dwb = _x_grad(dproj, win_t, x, dh, norm_g, w_in_moving["token"], dcb_tm, glu_tm, big)
    land4, landkv = _scatter_wait(small_moving, dng)
    landin, = _scatter_wait(w_in_moving, dng)
    small = {SV_NORM_G: dng, SV_CONV_B_B: dbb, SV_LN_G: dlg, SV_LN_B: dlb, SV_MEM_G: dmg, SV_FINAL_G: dfg, SV_LOSS: sq}
    grads = [(dwin_g[None], landin[None]), (dw4, land4), (dwkv_g[None], landkv[None])]
    return gx, grads, small, dwa.reshape(K_A, D), dwb.reshape(K_B, D)


def _allgather_small(small, conv_rows):
    keys = sorted(small)

    def body(*refs):
        parts, (conv_ref, out_ref, mine, send, recv) = refs[:len(keys)], refs[len(keys):]
        x, y, c, chips = _place()
        me, sibling = 4 * x + 2 * y + c, (x, y, 1 - c)
        mine[pl.ds(0, SV_CONV_A), :] = jnp.zeros((SV_CONV_A, D), F32)
        for key, part in zip(keys, parts):
            mine[key:key + 1, :] = jnp.sum(part[...], axis=0, keepdims=True)
        mine[pl.ds(SV_CONV_A, SV_ROWS - SV_CONV_A), :] = conv_ref[...]
        out_ref[me] = mine[...]

        def copy(k, block, to, from_mine=False):
            return pltpu.make_async_remote_copy(
                src_ref=mine if from_mine else out_ref.at[block], dst_ref=out_ref.at[block],
                send_sem=send.at[k], recv_sem=recv.at[k], device_id=to, device_id_type=MESH)

        first = [copy(0, me, sibling, from_mine=True)]
        first += [copy(1 + j, me, (*chip, c), from_mine=True) for j, chip in enumerate(chips)]
        for cp in first:
            cp.start()
        passed = []
        for j, (px, py) in enumerate(chips):
            block = 4 * px + 2 * py + c
            copy(1 + j, block, sibling).wait_recv()
            passed.append(copy(4 + j, block, sibling))
            passed[-1].start()
        copy(0, 4 * x + 2 * y + 1 - c, sibling).wait_recv()
        for j, (px, py) in enumerate(chips):
            copy(4 + j, 4 * px + 2 * py + 1 - c, sibling).wait_recv()
        for cp in first + passed:
            cp.wait_send()

    vmem = pl.BlockSpec(memory_space=pltpu.VMEM)
    return pl.pallas_call(
        body, name="allgather_small",
        in_specs=[vmem] * (len(keys) + 1), out_specs=vmem,
        out_shape=jax.ShapeDtypeStruct((N_DEV, SV_ROWS, D), F32),
        scratch_shapes=[pltpu.VMEM((SV_ROWS, D), F32), pltpu.SemaphoreType.DMA((7,)), pltpu.SemaphoreType.DMA((7,))],
    )(*[small[k] for k in keys], conv_rows)


def _adamw(w, g, m, v):
    m = ADAM_B1 * m + (1.0 - ADAM_B1) * g
    v = ADAM_B2 * v + (1.0 - ADAM_B2) * (g * g)
    m_hat = m / (1.0 - ADAM_B1 ** ADAM_STEP)
    v_hat = v / (1.0 - ADAM_B2 ** ADAM_STEP)
    return -ADAM_LR * (m_hat / (jnp.sqrt(v_hat) + ADAM_EPS) + ADAM_WD * w), m, v


def _adamw_shard(own, landed, piece, k_arr, w, m, v, tr):
    n_r, n_c = w.shape
    n_landed = landed.shape[1]

    def body(k_ref, own_ref, *refs):
        del k_ref
        landed_refs, (w_ref, m_ref, v_ref, g_out, d_out, m_out, v_out) = refs[:n_landed], refs[n_landed:]
        g = own_ref[0, 0]
        for landed_ref in landed_refs:
            g = g + landed_ref[0, 0].astype(F32)
        g_out[...] = g
        d_out[...], m_out[...], v_out[...] = _adamw(w_ref[...], g, m_ref[...], v_ref[...])

    blk = (1, 1, tr, n_c)
    flat = pl.BlockSpec((tr, n_c), lambda r, k: (r, 0))
    return pl.pallas_call(
        body, name="adamw_shard",
        grid_spec=pltpu.PrefetchScalarGridSpec(
            num_scalar_prefetch=1, grid=(n_r // tr,),
            in_specs=[pl.BlockSpec(blk, lambda r, k: (piece, k[0], r, 0))]
            + [pl.BlockSpec(blk, functools.partial(lambda r, k, j: (piece, j, r, 0), j=j)) for j in range(n_landed)]
            + [flat] * 3,
            out_specs=[flat] * 4),
        out_shape=[jax.ShapeDtypeStruct((n_r, n_c), F32)] * 4,
        compiler_params=_params(1),
    )(k_arr, own, *([landed] * n_landed), w, m, v)


def _adamw_small(gathered, k_arr, vectors, conv_a, conv_b):
    n_vec = len(vectors)
    cols = D // N_DEV

    def body(k_ref, full_ref, cols_ref, *refs):
        del k_ref
        ins, outs = refs[:3 * (n_vec + 2)], refs[3 * (n_vec + 2):-2]
        tot, tot_cols = refs[-2:]
        tot[...] = full_ref[0]
        tot_cols[...] = cols_ref[0]
        for dev in range(1, N_DEV):
            tot[...] += full_ref[dev]
            tot_cols[...] += cols_ref[dev]
        loss = (0.5 / D) * jnp.sum(tot[SV_LOSS:SV_LOSS + 1, :])
        outs[0][...] = jnp.full(outs[0].shape, loss, F32)
        grads = [tot[n:n + 1, :] for n in range(n_vec)]
        grads += [tot_cols[pl.ds(SV_CONV_A, K_A), :], tot_cols[pl.ds(SV_CONV_B, K_B), :]]
        for n, g in enumerate(grads):
            w_ref, m_ref, v_ref = ins[3 * n:3 * n + 3]
            g_out, d_out, m_out, v_out = outs[1 + 4 * n:5 + 4 * n]
            g_out[...] = g
            d_out[...], m_out[...], v_out[...] = _adamw(w_ref[...], g, m_ref[...], v_ref[...])

    weights = list(vectors) + [conv_a, conv_b]
    flat_in = [a for wmv in weights for a in wmv]
    out_shape = [jax.ShapeDtypeStruct((SUBLANES, 128), F32)]
    for wmv in weights:
        out_shape += [jax.ShapeDtypeStruct(wmv[0].shape, F32)] * 4
    return pl.pallas_call(
        body, name="adamw_small",
        grid_spec=pltpu.PrefetchScalarGridSpec(
            num_scalar_prefetch=1, grid=(1,),
            in_specs=[pl.BlockSpec((N_DEV, SV_ROWS, D), lambda i, k: (0, 0, 0)),
                      pl.BlockSpec((N_DEV, SV_ROWS, cols), lambda i, k: (0, 0, k[0]))]
            + [pl.BlockSpec(a.shape, lambda i, k: (0, 0)) for a in flat_in],
            out_specs=[pl.BlockSpec(s.shape, lambda i, k: (0, 0)) for s in out_shape],
            scratch_shapes=[pltpu.VMEM((SV_ROWS, D), F32), pltpu.VMEM((SV_ROWS, cols), F32)]),
        out_shape=out_shape,
        compiler_params=_params(1),
    )(k_arr, gathered, gathered, *flat_in)


def kernel(x, mem, norm_g, w_in, conv_a_w, w_out_a, conv_b_w, conv_b_b, ln_b_g, ln_b_b, w_out_b, mem_norm_g, w_kv, w_out_x, w_o, final_g, loss_target, m_norm_g, m_w_in, m_conv_a_w, m_w_out_a, m_conv_b_w, m_conv_b_b, m_ln_b_g, m_ln_b_b, m_w_out_b, m_mem_norm_g, m_w_kv, m_w_out_x, m_w_o, m_final_g, v_norm_g, v_w_in, v_conv_a_w, v_w_out_a, v_conv_b_w, v_conv_b_b, v_ln_b_g, v_ln_b_b, v_w_out_b, v_mem_norm_g, v_w_kv, v_w_out_x, v_w_o, v_final_g):
    xi, yi, ci = lax.axis_index("x"), lax.axis_index("y"), lax.axis_index("c")
    k_arr = jnp.reshape(4 * xi + 2 * yi + ci, (1,)).astype(jnp.int32)

    cw = jnp.concatenate([jnp.pad(conv_a_w[0], ((0, SUBLANES - K_A), (0, 0))),
                          jnp.pad(conv_b_w[0], ((0, HALO - K_B), (0, 0)))], axis=0)
    wo4 = jnp.stack([w_out_a[0], w_out_b[0], w_out_x[0], w_o[0]]).astype(BF16)
    shards = [w_in[0].astype(BF16), w_kv[0].astype(BF16), wo4, cw]

    final_g2 = final_g.reshape(1, D)
    gx, grads, small, dwa, dwb = _local_step(
        x[0], mem[0], loss_target[0], norm_g, conv_b_b, ln_b_g, ln_b_b, mem_norm_g, final_g2, shards)

    conv_rows = jnp.concatenate([jnp.pad(dwa, ((0, SUBLANES - K_A), (0, 0))),
                                 jnp.pad(dwb, ((0, HALO - K_B), (0, 0)))], axis=0)
    gathered_small = _allgather_small(small, conv_rows)

    tiles = [256, D // N_DEV, 256]

    def shard(a, l, w, m, v):
        return _adamw_shard(grads[a][0], grads[a][1], l, k_arr, w[0], m[0], v[0], tiles[a])

    res = {
        "w_in": shard(0, 0, w_in, m_w_in, v_w_in),
        "w_out_a": shard(1, 0, w_out_a, m_w_out_a, v_w_out_a),
        "w_out_b": shard(1, 1, w_out_b, m_w_out_b, v_w_out_b),
        "w_out_x": shard(1, 2, w_out_x, m_w_out_x, v_w_out_x),
        "w_o": shard(1, 3, w_o, m_w_o, v_w_o),
        "w_kv": shard(2, 0, w_kv, m_w_kv, v_w_kv),
    }
    res = {name: tuple(r[None] for r in four) for name, four in res.items()}
    vectors = [(norm_g, m_norm_g, v_norm_g), (conv_b_b, m_conv_b_b, v_conv_b_b), (ln_b_g, m_ln_b_g, v_ln_b_g),
               (ln_b_b, m_ln_b_b, v_ln_b_b), (mem_norm_g, m_mem_norm_g, v_mem_norm_g),
               (final_g2, m_final_g.reshape(1, D), v_final_g.reshape(1, D))]
    out = _adamw_small(gathered_small, k_arr, vectors, (conv_a_w[0], m_conv_a_w[0], v_conv_a_w[0]),
                       (conv_b_w[0], m_conv_b_w[0], v_conv_b_w[0]))
    loss = out[0][0, 0]
    names = ["norm_g", "conv_b_b", "ln_b_g", "ln_b_b", "mem_norm_g", "final_g", "conv_a_w", "conv_b_w"]
    for n, name in enumerate(names):
        four = out[1 + 4 * n:5 + 4 * n]
        if name == "final_g":
            four = [r.reshape(D) for r in four]
        elif name.startswith("conv_") and name.endswith("_w"):
            four = [r[None] for r in four]
        res[name] = tuple(four)

    order = ["norm_g", "w_in", "conv_a_w", "w_out_a", "conv_b_w", "conv_b_b", "ln_b_g", "ln_b_b", "w_out_b",
             "mem_norm_g", "w_kv", "w_out_x", "w_o", "final_g"]
    return (loss, gx[None], *[res[n][0] for n in order], *[res[n][1] for n in order],
            *[res[n][2] for n in order], *[res[n][3] for n in order])
```

```python
import functools

import jax
import jax.numpy as jnp
from jax import lax
from jax.experimental import pallas as pl
from jax.experimental.pallas import tpu as pltpu

F32, BF16 = jnp.float32, jnp.bfloat16
D = 1024
N_DEV = 8
N_HEADS = 4
HEAD_DIM = D // N_HEADS
N_GROUPS = 12
W_IN_SHARD = N_GROUPS * D // N_DEV
UNIT = 512
X_GRAD_K = 2 * D
K_A, K_B = 3, 31
EPS = 1e-6
HALO = 32
SUBLANES = 8
LANES = 128
LANE_GROUPS = D // LANES
TAP_GROUP = 16
CONV_BLOCK = 32
CONV_PARTIAL_SUMS = 4
VMEM_LIMIT = 56 << 20
MESH = pl.DeviceIdType.MESH
ANY = pl.BlockSpec(memory_space=pl.ANY)
HBM = pl.BlockSpec(memory_space=pltpu.HBM)
SEM = pl.BlockSpec(memory_space=pltpu.SEMAPHORE)
SIDE_EFFECT = pltpu.SideEffectType.DATAFLOW_SIDE_EFFECTING

G_BA, G_CA, G_XA, G_ZA, G_VB, G_GB, G_ZB, G_Q, G_ZX, G_GA, G_GBB, G_GX = range(N_GROUPS)
DP_POS = (0, 1, 2, 3, 6, 7, 8, 4, 5, 9, 10, 11)

ADAM_LR, ADAM_B1, ADAM_B2, ADAM_EPS, ADAM_WD, ADAM_STEP = 0.001, 0.9, 0.999, 1e-08, 0.01, 10

SV_NORM_G, SV_CONV_B_B, SV_LN_G, SV_LN_B, SV_MEM_G, SV_FINAL_G, SV_LOSS = range(7)
SV_CONV_A, SV_CONV_B, SV_ROWS = 8, 16, 48


def _dot(a, b):
    return jnp.dot(a, b, preferred_element_type=F32)


def _dot_nt(a, b):
    return lax.dot_general(a, b, (((1,), (1,)), ((), ())), preferred_element_type=F32)


def _dot_tn(a, b):
    return lax.dot_general(a, b, (((0,), (0,)), ((), ())), preferred_element_type=F32)


def _silu_and_grad(z):
    s = jax.nn.sigmoid(z)
    return z * s, s * (1.0 + z * (1.0 - s))


def _fold8(a):
    return a.reshape(a.shape[0] // SUBLANES, SUBLANES, a.shape[1]).sum(axis=0)


def _mean(a):
    return jnp.mean(a, axis=-1, keepdims=True)


def _f32(ref):
    return ref[...].astype(F32)


def _params(n_grid):
    return pltpu.CompilerParams(dimension_semantics=("arbitrary",) * n_grid, vmem_limit_bytes=VMEM_LIMIT)


def _rows(tm, col=0):
    return pl.BlockSpec((tm, D), lambda i: (i, col))


def _prev_halo(tm, col=0):
    return pl.BlockSpec((HALO, D), lambda i: (jnp.maximum(i * (tm // HALO) - 1, 0), col))


def _next_halo(tm, n_rows, col=0):
    last = n_rows // HALO - 1
    return pl.BlockSpec((HALO, D), lambda i: (jnp.minimum((i + 1) * (tm // HALO), last), col))


def _const(shape):
    return pl.BlockSpec(shape, lambda *_: (0,) * len(shape))


def _w_out_spec(which):
    return pl.BlockSpec((N_DEV, None, D // N_DEV, D), lambda *_: (0, which, 0, 0))


def _to_time_major(t_ref, row0, x):
    n = x.shape[0]
    for j in range(LANE_GROUPS):
        t_ref[pl.ds(row0 * LANE_GROUPS + j, n, stride=LANE_GROUPS), :] = x[:, j * LANES:(j + 1) * LANES]


def _from_time_major(t_ref, n):
    return jnp.concatenate([t_ref[pl.ds(j, n, stride=LANE_GROUPS), :] for j in range(LANE_GROUPS)], axis=-1)


def _row(ref, t):
    start = t * LANE_GROUPS
    if not isinstance(start, int):
        start = pl.multiple_of(start, LANE_GROUPS)
    return ref[pl.ds(start, LANE_GROUPS), :]


def _tap_groups(taps):
    return [taps[first:first + TAP_GROUP] for first in range(0, len(taps), TAP_GROUP)]


def _conv(o_ref, e_ref, w_ref, taps, n_rows, bias_ref=None):
    for n_group, group in enumerate(_tap_groups(taps)):
        weights = [_row(w_ref, k) for k, _ in group]

        def block(c, carry, n_group=n_group, group=group, weights=weights):
            t0 = c * CONV_BLOCK
            window = {}
            for t in range(CONV_BLOCK):
                parts = [None] * min(CONV_PARTIAL_SUMS, len(group))
                for n, (_, off) in enumerate(group):
                    if t + off not in window:
                        window[t + off] = _row(e_ref, t0 + t + off)
                    term = weights[n] * window[t + off]
                    parts[n % len(parts)] = term if parts[n % len(parts)] is None else parts[n % len(parts)] + term
                window.pop(t + min(off for _, off in group), None)
                while len(parts) > 1:
                    parts = [parts[n] + parts[n + 1] for n in range(0, len(parts) - 1, 2)] + parts[len(parts) & ~1:]
                out = parts[0]
                if n_group > 0:
                    out = out + _row(o_ref, t0 + t)
                elif bias_ref is not None:
                    out = out + bias_ref[...]
                o_ref[pl.ds(pl.multiple_of((t0 + t) * LANE_GROUPS, LANE_GROUPS), LANE_GROUPS), :] = out
            return carry

        lax.fori_loop(0, n_rows // CONV_BLOCK, block, 0)


def _conv_wgrad(dw_ref, d_ref, e_ref, taps, n_rows):
    for group in _tap_groups(taps):
        def block(c, sums, group=group):
            t0 = c * CONV_BLOCK
            sums = list(sums)
            window = {}
            for t in range(CONV_BLOCK):
                d = _row(d_ref, t0 + t)
                for n, (_, off) in enumerate(group):
                    if t + off not in window:
                        window[t + off] = _row(e_ref, t0 + t + off)
                    sums[n] = sums[n] + d * window[t + off]
                window.pop(t + min(off for _, off in group), None)
            return tuple(sums)

        sums = lax.fori_loop(0, n_rows // CONV_BLOCK, block, tuple(_row(dw_ref, k) for k, _ in group))
        for (k, _), total in zip(group, sums):
            dw_ref[pl.ds(k * LANE_GROUPS, LANE_GROUPS), :] = total


FWD_TAPS_A = [(k, HALO - (K_A - 1) + k) for k in range(K_A)]
BWD_TAPS_A = [(k, K_A - 1 - k) for k in range(K_A)]
FWD_TAPS_B = [(k, HALO - (K_B - 1) + k) for k in range(K_B)]
BWD_TAPS_B = [(k, K_B - 1 - k) for k in range(K_B)]


def _time_major(n_rows):
    return pltpu.VMEM((n_rows * LANE_GROUPS, LANES), F32)


def _kv_fwd(mem, mem_g, wkv_g):
    m_len = mem.shape[0]

    def body(mem_ref, g_ref, w_ref, kv_ref, mn_ref):
        mf = mem_ref[...]
        r = lax.rsqrt(_mean(mf * mf) + EPS)
        mn = ((mf * r) * g_ref[...]).astype(BF16)
        mn_ref[...] = mn
        for b in range(2 * N_HEADS):
            kv_ref[b] = _dot(mn, w_ref[b]).astype(BF16)

    return pl.pallas_call(
        body, name="kv_fwd", grid=(1,),
        in_specs=[_const((m_len, D)), _const((1, D)), _const((2 * N_HEADS, D, HEAD_DIM))],
        out_specs=[_const((2 * N_HEADS, m_len, HEAD_DIM)), _const((m_len, D))],
        out_shape=[jax.ShapeDtypeStruct((2 * N_HEADS, m_len, HEAD_DIM), BF16), jax.ShapeDtypeStruct((m_len, D), BF16)],
        compiler_params=_params(1),
    )(mem, mem_g, wkv_g)


def _kv_bwd(dkv, mem, mem_g, mn16, wkv_g):
    m_len = mem.shape[0]

    def body(dkv_ref, mem_ref, g_ref, mn_ref, w_ref, dw_ref, dg_ref):
        mn = mn_ref[...]
        dmn = jnp.zeros((m_len, D), F32)
        for b in range(2 * N_HEADS):
            d16 = dkv_ref[b].astype(BF16)
            dw_ref[b] = _dot_tn(mn, d16)
            dmn = dmn + _dot_nt(d16, w_ref[b])
        mf = mem_ref[...]
        r = lax.rsqrt(_mean(mf * mf) + EPS)
        dg_ref[...] = _fold8(dmn * (mf * r))

    return pl.pallas_call(
        body, name="kv_bwd", grid=(1,),
        in_specs=[_const((2 * N_HEADS, m_len, HEAD_DIM)), _const((m_len, D)), _const((1, D)), _const((m_len, D)),
                  _const((2 * N_HEADS, D, HEAD_DIM))],
        out_specs=[_const((2 * N_HEADS, D, HEAD_DIM)), _const((SUBLANES, D))],
        out_shape=[jax.ShapeDtypeStruct((2 * N_HEADS, D, HEAD_DIM), F32), jax.ShapeDtypeStruct((SUBLANES, D), F32)],
        compiler_params=_params(1),
    )(dkv, mem, mem_g, mn16, wkv_g)


def _rmsnorm_fwd(x, norm_g, tm):
    n_rows = x.shape[0]

    def body(x_ref, g_ref, u_ref, ut_ref):
        xf = x_ref[...]
        u = (xf * lax.rsqrt(_mean(xf * xf) + EPS)) * g_ref[...]
        u_ref[...] = u.astype(BF16)
        ut_ref[...] = u.T.astype(BF16)

    return pl.pallas_call(
        body, name="rmsnorm_fwd", grid=(n_rows // tm,),
        in_specs=[_rows(tm), _const((1, D))],
        out_specs=[_rows(tm), pl.BlockSpec((D, tm), lambda i: (0, i))],
        out_shape=[jax.ShapeDtypeStruct((n_rows, D), BF16), jax.ShapeDtypeStruct((D, n_rows), BF16)],
        compiler_params=_params(1),
    )(x, norm_g)


def _place():
    x, y, c = lax.axis_index("x"), lax.axis_index("y"), lax.axis_index("c")
    other_chips = [(1 - x, y), (x, 1 - y), (1 - x, 1 - y)]
    return x, y, c, other_chips


def _arrival_order():
    x, y, c, chips = _place()
    order = [4 * x + 2 * y + c, 4 * x + 2 * y + 1 - c]
    for px, py in chips:
        order += [4 * px + 2 * py + c, 4 * px + 2 * py + 1 - c]
    return order


def _proj_fwd_gather(u16, blocks, tm):
    n = len(blocks)
    n_rows = u16.shape[0]
    n_i = n_rows // tm
    per_shard = W_IN_SHARD // UNIT
    assert n_i >= per_shard

    def wt_index(p, i, order):
        return (_dp_unit(per_shard * order[p] + jnp.minimum(i, per_shard - 1)), 0)

    def body(order_ref, u_ref, *refs):
        src, proj_ref, wt_ref, out = refs[:n], refs[n], refs[n + 1], refs[n + 2:2 * n + 2]
        wbuf, stage_sem, send, recv, own_sem = refs[2 * n + 2:]
        p, i = pl.program_id(0), pl.program_id(1)
        x, y, c, chips = _place()
        me, sibling = 4 * x + 2 * y + c, (x, y, 1 - c)

        def copy(t, k, block, to, from_input=False):
            return pltpu.make_async_remote_copy(
                src_ref=src[t] if from_input else out[t].at[block], dst_ref=out[t].at[block],
                send_sem=send.at[t, k], recv_sem=recv.at[t, k], device_id=to, device_id_type=MESH)

        def own_copies():
            return [pltpu.make_async_copy(src[t], out[t].at[me], own_sem.at[t]) for t in range(n)]

        def first_copies():
            first = []
            for t in range(n):
                first.append(copy(t, 0, me, sibling, from_input=True))
                first += [copy(t, 1 + j, me, (*chip, c), from_input=True) for j, chip in enumerate(chips)]
            return first

        def stage(slot, block):
            return pltpu.make_async_copy(out[0].at[block], wbuf.at[slot], stage_sem.at[slot])

        @pl.when((p == 0) & (i == 0))
        def _():
            for cp in own_copies() + first_copies():
                cp.start()
            mine = pltpu.make_async_copy(src[0], wbuf.at[0], stage_sem.at[0])
            mine.start()
            mine.wait()

        @pl.when((p > 0) & (i == 0))
        def _():
            stage(p % 2, order_ref[p]).wait()

        proj_ref[...] = _dot(u_ref[...], wbuf[p % 2])
        for r in range(per_shard):
            @pl.when(i == r)
            def _(r=r):
                wt_ref[...] = wbuf[p % 2, :, r * UNIT:(r + 1) * UNIT].astype(F32).T.astype(BF16)

        for nxt in range(1, N_DEV):
            @pl.when((p == nxt - 1) & (i == n_i - 1))
            def _(nxt=nxt):
                if nxt == 1:
                    block = 4 * x + 2 * y + 1 - c
                    copy(0, 0, block, sibling).wait_recv()
                else:
                    j, passed_on = divmod(nxt - 2, 2)
                    px, py = chips[j]
                    if passed_on:
                        block = 4 * px + 2 * py + 1 - c
                        copy(0, 4 + j, block, sibling).wait_recv()
                    else:
                        block = 4 * px + 2 * py + c
                        copy(0, 1 + j, block, sibling).wait_recv()
                        copy(0, 4 + j, block, sibling).start()
                stage(nxt % 2, block).start()

        @pl.when((p == N_DEV - 1) & (i == n_i - 1))
        def _():
            passed = [copy(0, 4 + j, 4 * px + 2 * py + c, sibling) for j, (px, py) in enumerate(chips)]
            for j, (px, py) in enumerate(chips):
                for t in range(1, n):
                    block = 4 * px + 2 * py + c
                    copy(t, 1 + j, block, sibling).wait_recv()
                    passed.append(copy(t, 4 + j, block, sibling))
                    passed[-1].start()
            for t in range(1, n):
                copy(t, 0, 4 * x + 2 * y + 1 - c, sibling).wait_recv()
                for j, (px, py) in enumerate(chips):
                    copy(t, 4 + j, 4 * px + 2 * py + 1 - c, sibling).wait_recv()
            for cp in first_copies() + passed:
                cp.wait_send()
            for cp in own_copies():
                cp.wait()

    return pl.pallas_call(
        body, name="proj_fwd_gather",
        grid_spec=pltpu.PrefetchScalarGridSpec(
            num_scalar_prefetch=1, grid=(N_DEV, n_i),
            in_specs=[pl.BlockSpec((tm, D), lambda p, i, order: (i, 0))] + [ANY] * n,
            out_specs=[pl.BlockSpec((tm, W_IN_SHARD), lambda p, i, order: (i, order[p])),
                       pl.BlockSpec((UNIT, D), wt_index)] + [ANY] * n,
            scratch_shapes=[pltpu.VMEM((2, D, W_IN_SHARD), BF16), pltpu.SemaphoreType.DMA((2,)),
                            pltpu.SemaphoreType.DMA((n, 7)), pltpu.SemaphoreType.DMA((n, 7)),
                            pltpu.SemaphoreType.DMA((n,))]),
        out_shape=[jax.ShapeDtypeStruct((n_rows, N_GROUPS * D), F32), jax.ShapeDtypeStruct((N_GROUPS * D, D), BF16)]
        + [jax.ShapeDtypeStruct((N_DEV, *b.shape), b.dtype) for b in blocks],
        compiler_params=_params(2),
    )(jnp.stack(_arrival_order()).astype(jnp.int32), u16, *blocks)


def _branch_a_fwd(proj, wo4_g, cw_a, tm):
    n_rows = proj.shape[0]

    def body(bp, cp, xp, za, cph, xph, w_ref, cw_ref, sa_ref, ya_ref, e_scr, o_scr):
        i = pl.program_id(0)
        bp, cp, xp, za, cph, xph = (_f32(r) for r in (bp, cp, xp, za, cph, xph))
        _to_time_major(e_scr, 0, jnp.where(i > 0, cph[...] * xph[...], 0.0))
        _to_time_major(e_scr, HALO, cp[...] * xp[...])
        _conv(o_scr, e_scr, cw_ref, FWD_TAPS_A, tm)
        sa = (jax.nn.silu(za[...]) * (bp[...] * _from_time_major(o_scr, tm))).astype(BF16)
        sa_ref[...] = sa
        ya_ref[...] = _dot(sa, w_ref[...].reshape(D, D))

    return pl.pallas_call(
        body, name="branch_a_fwd", grid=(n_rows // tm,),
        in_specs=[_rows(tm, G_BA), _rows(tm, G_CA), _rows(tm, G_XA), _rows(tm, G_ZA),
                  _prev_halo(tm, G_CA), _prev_halo(tm, G_XA), _w_out_spec(0), _const(cw_a.shape)],
        out_specs=[_rows(tm), _rows(tm)],
        out_shape=[jax.ShapeDtypeStruct((n_rows, D), BF16), jax.ShapeDtypeStruct((n_rows, D), F32)],
        scratch_shapes=[_time_major(tm + HALO), _time_major(tm)],
        compiler_params=_params(1),
    )(proj, proj, proj, proj, proj, proj, wo4_g, cw_a)


def _layernorm_parts(cb, lg, lb):
    xc = cb - _mean(cb)
    rstd = lax.rsqrt(_mean(xc * xc) + EPS)
    xhat = xc * rstd
    return xhat, rstd, xhat * lg + lb


def _branch_b_fwd(proj, wo4_g, cw_b, conv_b_b, ln_g, ln_b, tm):
    n_rows = proj.shape[0]

    def body(vb, gb, zb, vbh, gbh, w_ref, cw_ref, bb_ref, lg_ref, lb_ref, cb_ref, sb_ref, yb_ref, e_scr, o_scr):
        i = pl.program_id(0)
        vb, gb, zb, vbh, gbh = (_f32(r) for r in (vb, gb, zb, vbh, gbh))
        _to_time_major(e_scr, 0, jnp.where(i > 0, vbh[...] * jax.nn.sigmoid(gbh[...]), 0.0))
        _to_time_major(e_scr, HALO, vb[...] * jax.nn.sigmoid(gb[...]))
        _conv(o_scr, e_scr, cw_ref, FWD_TAPS_B, tm, bias_ref=bb_ref)
        cb = _from_time_major(o_scr, tm)
        cb_ref[...] = cb
        _, _, ln = _layernorm_parts(cb, lg_ref[...], lb_ref[...])
        sb = (jax.nn.silu(zb[...]) * jax.nn.silu(ln)).astype(BF16)
        sb_ref[...] = sb
        yb_ref[...] = _dot(sb, w_ref[...].reshape(D, D))

    return pl.pallas_call(
        body, name="branch_b_fwd", grid=(n_rows // tm,),
        in_specs=[_rows(tm, G_VB), _rows(tm, G_GB), _rows(tm, G_ZB), _prev_halo(tm, G_VB), _prev_halo(tm, G_GB),
                  _w_out_spec(1), _const(cw_b.shape), _const((LANE_GROUPS, LANES)), _const((1, D)), _const((1, D))],
        out_specs=[_rows(tm), _rows(tm), _rows(tm)],
        out_shape=[jax.ShapeDtypeStruct((n_rows, D), F32), jax.ShapeDtypeStruct((n_rows, D), BF16),
                   jax.ShapeDtypeStruct((n_rows, D), F32)],
        scratch_shapes=[_time_major(tm + HALO), _time_major(tm)],
        compiler_params=_params(1),
    )(proj, proj, proj, proj, proj, wo4_g, cw_b, conv_b_b.reshape(LANE_GROUPS, LANES), ln_g, ln_b)


def _attention(q16, kv_ref):
    probs, outs = [], []
    for h in range(N_HEADS):
        s = _dot_nt(q16[:, h * HEAD_DIM:(h + 1) * HEAD_DIM], kv_ref[h]) * (HEAD_DIM ** -0.5)
        e = jnp.exp(s - jnp.max(s, axis=-1, keepdims=True))
        p = e / jnp.sum(e, axis=-1, keepdims=True)
        probs.append(p)
        outs.append(_dot(p.astype(BF16), kv_ref[N_HEADS + h]))
    return probs, outs


def _branch_x_fwd(proj, kv16, wo4_g, tm):
    n_rows = proj.shape[0]

    def body(q, zx, kv_ref, w_ref, sx_ref, yx_ref):
        _, outs = _attention(q[...].astype(BF16), kv_ref)
        sx = (jax.nn.silu(_f32(zx)) * jnp.concatenate(outs, axis=-1)).astype(BF16)
        sx_ref[...] = sx
        yx_ref[...] = _dot(sx, w_ref[...].reshape(D, D))

    return pl.pallas_call(
        body, name="branch_x_fwd", grid=(n_rows // tm,),
        in_specs=[_rows(tm, G_Q), _rows(tm, G_ZX), _const(kv16.shape), _w_out_spec(2)],
        out_specs=[_rows(tm), _rows(tm)],
        out_shape=[jax.ShapeDtypeStruct((n_rows, D), BF16), jax.ShapeDtypeStruct((n_rows, D), F32)],
        compiler_params=_params(1),
    )(proj, proj, kv16, wo4_g)


def _merge_fwd_bwd(proj, ya, yb, yx, x, target, wo4_g, final_g, tm):
    n_rows = proj.shape[0]
    inv_d = 1.0 / D

    def body(ga, gb, gx, ya_ref, yb_ref, yx_ref, x_ref, t_ref, w_ref, fg_ref,
             dh_ref, dya_ref, dyb_ref, dyx_ref, dp_ref, dw_ref, dfg_ref, sq_ref):
        i = pl.program_id(0)
        wo = w_ref[...].reshape(D, D)
        sig = [jax.nn.sigmoid(_f32(g)) for g in (ga, gb, gx)]
        ys = [ya_ref[...], yb_ref[...], yx_ref[...]]
        m16 = (sig[0] * ys[0] + sig[1] * ys[1] + sig[2] * ys[2]).astype(BF16)
        h = x_ref[...] + _dot(m16, wo)
        r = lax.rsqrt(_mean(h * h) + EPS)
        hn = h * r
        fg = fg_ref[...]
        err = hn * fg - t_ref[...]
        dy = err * inv_d
        dhn = dy * fg
        dh = r * (dhn - hn * _mean(dhn * hn))
        dh_ref[...] = dh
        dh16 = dh.astype(BF16)
        dm = _dot_nt(dh16, wo)
        for n, out in enumerate((dya_ref, dyb_ref, dyx_ref)):
            out[...] = (sig[n] * dm).astype(BF16)
            dp_ref[:, n * D:(n + 1) * D] = (dm * ys[n] * (sig[n] * (1.0 - sig[n]))).astype(BF16)

        @pl.when(i == 0)
        def _():
            dw_ref[...] = jnp.zeros_like(dw_ref)
            dfg_ref[...] = jnp.zeros_like(dfg_ref)
            sq_ref[...] = jnp.zeros_like(sq_ref)

        dw_ref[0] += _dot_tn(m16, dh16)
        dfg_ref[...] += _fold8(dy * hn)
        sq_ref[...] += _fold8(err * err)

    vec = jax.ShapeDtypeStruct((SUBLANES, D), F32)
    return pl.pallas_call(
        body, name="merge_fwd_bwd", grid=(n_rows // tm,),
        in_specs=[_rows(tm, G_GA), _rows(tm, G_GBB), _rows(tm, G_GX), _rows(tm), _rows(tm), _rows(tm), _rows(tm),
                  _rows(tm), _w_out_spec(3), _const((1, D))],
        out_specs=[_rows(tm), _rows(tm), _rows(tm), _rows(tm), pl.BlockSpec((tm, 3 * D), lambda i: (i, 3)),
                   pl.BlockSpec((1, D, D), lambda i: (3, 0, 0)), _const((SUBLANES, D)), _const((SUBLANES, D))],
        out_shape=[jax.ShapeDtypeStruct((n_rows, D), F32), jax.ShapeDtypeStruct((n_rows, D), BF16),
                   jax.ShapeDtypeStruct((n_rows, D), BF16), jax.ShapeDtypeStruct((n_rows, D), BF16),
                   jax.ShapeDtypeStruct((n_rows, N_GROUPS * D), BF16), jax.ShapeDtypeStruct((4, D, D), F32), vec, vec],
        compiler_params=_params(1),
    )(proj, proj, proj, ya, yb, yx, x, target, wo4_g, final_g)


def _branch_a_bwd(dya, proj, sa16, wo4_g, cw_a, dproj, dw4, tm):
    n_rows = proj.shape[0]
    n_tiles = n_rows // tm

    def body(dya_ref, bp, cp, xp, za, sa_ref, dyan, bpn, zan, cph, xph, w_ref, cw_ref, dp_in, dw_in,
             dp_ref, dw_ref, dwa_ref, e1, e2, o_scr):
        del dp_in, dw_in
        i = pl.program_id(0)
        bp, cp, xp, za, bpn, zan, cph, xph = (_f32(r) for r in (bp, cp, xp, za, bpn, zan, cph, xph))
        woa = w_ref[...].reshape(D, D)
        dya16 = dya_ref[...]
        _to_time_major(e1, 0, jnp.where(i > 0, cph[...] * xph[...], 0.0))
        _to_time_major(e1, HALO, cp[...] * xp[...])
        _conv(o_scr, e1, cw_ref, FWD_TAPS_A, tm)
        ca = _from_time_major(o_scr, tm)
        dsa = _dot_nt(dya16, woa)
        silu_z, dsilu_z = _silu_and_grad(za[...])
        t = dsa * silu_z
        dp_ref[:, 0 * D:1 * D] = (t * ca).astype(BF16)
        dp_ref[:, 3 * D:4 * D] = (dsa * (bp[...] * ca) * dsilu_z).astype(BF16)
        dsan = _dot_nt(dyan[...], woa)
        dcan = (dsan * jax.nn.silu(zan[...])) * bpn[...]
        _to_time_major(e2, 0, t * bp[...])
        _to_time_major(e2, tm, jnp.where(i < n_tiles - 1, dcan, 0.0))

        @pl.when(i == 0)
        def _():
            dw_ref[...] = jnp.zeros_like(dw_ref)
            dwa_ref[...] = jnp.zeros_like(dwa_ref)

        _conv_wgrad(dwa_ref, e2, e1, FWD_TAPS_A, tm)
        dw_ref[0] += _dot_tn(sa_ref[...], dya16)
        _conv(o_scr, e2, cw_ref, BWD_TAPS_A, tm)
        dprod = _from_time_major(o_scr, tm)
        dp_ref[:, 1 * D:2 * D] = (dprod * xp[...]).astype(BF16)
        dp_ref[:, 2 * D:3 * D] = (dprod * cp[...]).astype(BF16)

    return pl.pallas_call(
        body, name="branch_a_bwd", grid=(n_tiles,),
        in_specs=[_rows(tm), _rows(tm, G_BA), _rows(tm, G_CA), _rows(tm, G_XA), _rows(tm, G_ZA), _rows(tm),
                  _next_halo(tm, n_rows), _next_halo(tm, n_rows, G_BA), _next_halo(tm, n_rows, G_ZA),
                  _prev_halo(tm, G_CA), _prev_halo(tm, G_XA), _w_out_spec(0), _const(cw_a.shape), ANY, ANY],
        out_specs=[pl.BlockSpec((tm, 4 * D), lambda i: (i, 0)), pl.BlockSpec((1, D, D), lambda i: (0, 0, 0)),
                   _const((K_A * LANE_GROUPS, LANES))],
        out_shape=[jax.ShapeDtypeStruct(dproj.shape, BF16), jax.ShapeDtypeStruct(dw4.shape, F32),
                   jax.ShapeDtypeStruct((K_A * LANE_GROUPS, LANES), F32)],
        input_output_aliases={13: 0, 14: 1},
        scratch_shapes=[_time_major(tm + HALO), _time_major(tm + HALO), _time_major(tm)],
        compiler_params=_params(1),
    )(dya, proj, proj, proj, proj, sa16, dya, proj, proj, proj, proj, wo4_g, cw_a, dproj, dw4)


def _branch_b_bwd(dyb, proj, cb, sb16, wo4_g, cw_b, ln_g, ln_b, dproj, dw4, tm):
    n_rows = proj.shape[0]
    n_tiles = n_rows // tm

    def body(dyb_ref, zb, cb_ref, vb, gb, sb_ref, dybn, zbn, cbn, vbh, gbh, w_ref, cw_ref, lg_ref, lb_ref,
             dp_in, dw_in, dp_ref, dw_ref, dwb_ref, dbb_ref, dlg_ref, dlb_ref, e1, e2, o_scr):
        del dp_in, dw_in
        vb, gb, zbn, vbh, gbh = (_f32(r) for r in (vb, gb, zbn, vbh, gbh))
        i = pl.program_id(0)
        wob = w_ref[...].reshape(D, D)
        lg, lb = lg_ref[...], lb_ref[...]

        def conv_out_grad(dy16, z, c):
            dsb = _dot_nt(dy16, wob)
            xhat, rstd, ln = _layernorm_parts(c, lg, lb)
            sw, dsw = _silu_and_grad(ln)
            sz, dsz = _silu_and_grad(z)
            dln = (dsb * sz) * dsw
            dxhat = dln * lg
            dcb = rstd * (dxhat - _mean(dxhat) - xhat * _mean(dxhat * xhat))
            return dsb * sw * dsz, dln, xhat, dcb

        @pl.when(i == 0)
        def _():
            dw_ref[...] = jnp.zeros_like(dw_ref)
            dwb_ref[...] = jnp.zeros_like(dwb_ref)
            dbb_ref[...] = jnp.zeros_like(dbb_ref)
            dlg_ref[...] = jnp.zeros_like(dlg_ref)
            dlb_ref[...] = jnp.zeros_like(dlb_ref)

        dyb16 = dyb_ref[...]
        dzb, dln, xhat, dcb = conv_out_grad(dyb16, _f32(zb), cb_ref[...])
        dp_ref[:, 2 * D:3 * D] = dzb.astype(BF16)
        _, _, _, dcbn = conv_out_grad(dybn[...], zbn[...], cbn[...])
        _to_time_major(e2, 0, dcb)
        _to_time_major(e2, tm, jnp.where(i < n_tiles - 1, dcbn, 0.0))
        dlg_ref[...] += _fold8(dln * xhat)
        dlb_ref[...] += _fold8(dln)
        dbb_ref[...] += _fold8(dcb)
        dw_ref[0] += _dot_tn(sb_ref[...], dyb16)
        sg = jax.nn.sigmoid(gb[...])
        _to_time_major(e1, 0, jnp.where(i > 0, vbh[...] * jax.nn.sigmoid(gbh[...]), 0.0))
        _to_time_major(e1, HALO, vb[...] * sg)
        _conv_wgrad(dwb_ref, e2, e1, FWD_TAPS_B, tm)
        _conv(o_scr, e2, cw_ref, BWD_TAPS_B, tm)
        dglu = _from_time_major(o_scr, tm)
        dp_ref[:, 0 * D:1 * D] = (dglu * sg).astype(BF16)
        dp_ref[:, 1 * D:2 * D] = (dglu * vb[...] * (sg * (1.0 - sg))).astype(BF16)

    vec = jax.ShapeDtypeStruct((SUBLANES, D), F32)
    return pl.pallas_call(
        body, name="branch_b_bwd", grid=(n_tiles,),
        in_specs=[_rows(tm), _rows(tm, G_ZB), _rows(tm), _rows(tm, G_VB), _rows(tm, G_GB), _rows(tm),
                  _next_halo(tm, n_rows), _next_halo(tm, n_rows, G_ZB), _next_halo(tm, n_rows),
                  _prev_halo(tm, G_VB), _prev_halo(tm, G_GB), _w_out_spec(1), _const(cw_b.shape), _const((1, D)),
                  _const((1, D)), ANY, ANY],
        out_specs=[pl.BlockSpec((tm, 3 * D), lambda i: (i, 2)), pl.BlockSpec((1, D, D), lambda i: (1, 0, 0)),
                   _const((K_B * LANE_GROUPS, LANES)), _const((SUBLANES, D)), _const((SUBLANES, D)),
                   _const((SUBLANES, D))],
        out_shape=[jax.ShapeDtypeStruct(dproj.shape, BF16), jax.ShapeDtypeStruct(dw4.shape, F32),
                   jax.ShapeDtypeStruct((K_B * LANE_GROUPS, LANES), F32), vec, vec, vec],
        input_output_aliases={15: 0, 16: 1},
        scratch_shapes=[_time_major(tm + HALO), _time_major(tm + HALO), _time_major(tm)],
        compiler_params=_params(1),
    )(dyb, proj, cb, proj, proj, sb16, dyb, proj, cb, proj, proj, wo4_g, cw_b, ln_g, ln_b, dproj, dw4)


def _branch_x_bwd(dyx, proj, sx16, kv16, wo4_g, dproj, dw4, tm):
    n_rows = proj.shape[0]
    scale = HEAD_DIM ** -0.5

    def body(dyx_ref, q, zx, sx_ref, kv_ref, w_ref, dp_in, dw_in, dp_ref, dw_ref, dkv_ref):
        del dp_in, dw_in
        i = pl.program_id(0)
        dyx16 = dyx_ref[...]
        q16 = q[...].astype(BF16)
        probs, outs = _attention(q16, kv_ref)
        dsx = _dot_nt(dyx16, w_ref[...].reshape(D, D))
        silu_z, dsilu_z = _silu_and_grad(_f32(zx))
        dp_ref[:, D:2 * D] = (dsx * jnp.concatenate(outs, axis=-1) * dsilu_z).astype(BF16)
        do16 = (dsx * silu_z).astype(BF16)

        @pl.when(i == 0)
        def _():
            dw_ref[...] = jnp.zeros_like(dw_ref)
            dkv_ref[...] = jnp.zeros_like(dkv_ref)

        for h in range(N_HEADS):
            cols = slice(h * HEAD_DIM, (h + 1) * HEAD_DIM)
            p = probs[h]
            dprob = _dot_nt(do16[:, cols], kv_ref[N_HEADS + h])
            ds16 = ((p * (dprob - jnp.sum(p * dprob, axis=-1, keepdims=True))) * scale).astype(BF16)
            dp_ref[:, cols] = _dot(ds16, kv_ref[h]).astype(BF16)
            dkv_ref[h] += _dot_tn(ds16, q16[:, cols])
            dkv_ref[N_HEADS + h] += _dot_tn(p.astype(BF16), do16[:, cols])
        dw_ref[0] += _dot_tn(sx_ref[...], dyx16)

    return pl.pallas_call(
        body, name="branch_x_bwd", grid=(n_rows // tm,),
        in_specs=[_rows(tm), _rows(tm, G_Q), _rows(tm, G_ZX), _rows(tm), _const(kv16.shape), _w_out_spec(2), ANY, ANY],
        out_specs=[pl.BlockSpec((tm, 2 * D), lambda i: (i, 2)), pl.BlockSpec((1, D, D), lambda i: (2, 0, 0)),
                   _const(kv16.shape)],
        out_shape=[jax.ShapeDtypeStruct(dproj.shape, BF16), jax.ShapeDtypeStruct(dw4.shape, F32),
                   jax.ShapeDtypeStruct(kv16.shape, F32)],
        input_output_aliases={6: 0, 7: 1},
        compiler_params=_params(1),
    )(dyx, proj, proj, sx16, kv16, wo4_g, dproj, dw4)


def _dp_unit(u):
    g = u // 2
    pos = jnp.where(g < G_VB, g, jnp.where(g < G_Q, g + 2, jnp.where(g < G_GA, g - 3, g)))
    return 2 * pos + u % 2


def _scatter_copies(srcs, lands, send, recv):
    x, y, c = lax.axis_index("x"), lax.axis_index("y"), lax.axis_index("c")
    copies = []
    for n in range(N_DEV - 1):
        flip = n + 1
        px = 1 - x if flip & 4 else x
        py = 1 - y if flip & 2 else y
        pc = 1 - c if flip & 1 else c
        for t, (src, land) in enumerate(zip(srcs, lands)):
            copies.append(pltpu.make_async_remote_copy(
                src_ref=src.at[4 * px + 2 * py + pc], dst_ref=land.at[n], send_sem=send.at[t * (N_DEV - 1) + n],
                recv_sem=recv.at[t * (N_DEV - 1) + n], device_id=(px, py, pc), device_id_type=MESH))
    return copies


def _scatter_start(name, arrays, views, n_views):
    n = len(arrays)
    lands = [lax.empty(tuple(N_DEV - 1 if d == N_DEV else d for d in a.shape), a.dtype) for a in arrays]

    def body(*refs):
        src, land, (send, recv) = refs[:n], refs[n:2 * n], refs[2 * n:2 * n + 2]
        token = refs[-1]
        for cp in _scatter_copies(views(src), views(land), send, recv):
            cp.start()
        token[...] = jnp.zeros_like(token)

    sems = pltpu.SemaphoreType.DMA((n_views * (N_DEV - 1),))
    out = pl.pallas_call(
        body, name=name,
        in_specs=[HBM] * (2 * n),
        out_specs=[SEM, SEM] + [HBM] * (2 * n) + [pl.BlockSpec(memory_space=pltpu.VMEM)],
        out_shape=[sems, sems] + [pltpu.HBM(a.shape, a.dtype) for a in arrays + lands]
        + [jax.ShapeDtypeStruct((SUBLANES, LANES), F32)],
        input_output_aliases={k: 2 + k for k in range(2 * n)},
        compiler_params=pltpu.CompilerParams(has_side_effects=SIDE_EFFECT),
    )(*[pltpu.with_memory_space_constraint(a, pltpu.HBM) for a in arrays + lands])
    return dict(name=name, sems=out[:2], moving=out[2:2 + 2 * n], views=views, token=out[-1])


def _scatter_wait(started, after):
    n = len(started["moving"]) // 2
    views = started["views"]

    def body(*refs):
        src, land, (send, recv) = refs[:n], refs[n:2 * n], refs[2 * n:2 * n + 2]
        for cp in _scatter_copies(views(src), views(land), send, recv):
            cp.wait_send()
            cp.wait_recv()

    out = pl.pallas_call(
        body, name=started["name"].replace("start", "wait"),
        in_specs=[HBM] * (2 * n) + [SEM, SEM, ANY],
        out_specs=[HBM] * (2 * n),
        out_shape=[pltpu.HBM(a.shape, a.dtype) for a in started["moving"]],
        input_output_aliases={k: k for k in range(2 * n)},
        compiler_params=pltpu.CompilerParams(has_side_effects=SIDE_EFFECT),
    )(*started["moving"], *started["sems"], after)
    return out[n:]


def _w_in_grad(ut, dproj, tk, token):
    n_rows = dproj.shape[0]
    n_k = n_rows // tk
    per_shard = W_IN_SHARD // UNIT

    def body(ut_ref, dp0, dp1, dp2, token_ref, out_ref, out16_ref, acc):
        del token_ref
        t = pl.program_id(1)

        for r, dp_ref in enumerate((dp0, dp1, dp2)):
            cols = slice(r * UNIT, (r + 1) * UNIT)

            @pl.when(t == 0)
            def _(dp_ref=dp_ref, cols=cols):
                acc[:, cols] = _dot(ut_ref[...], dp_ref[...])

            @pl.when(t > 0)
            def _(dp_ref=dp_ref, cols=cols):
                acc[:, cols] += _dot(ut_ref[...], dp_ref[...])

        @pl.when(t == n_k - 1)
        def _():
            out_ref[0] = acc[...]
            out16_ref[0] = acc[...].astype(BF16)

    def dp_spec(r):
        return pl.BlockSpec((tk, UNIT), lambda q, t: (t, _dp_unit(per_shard * q + r)))

    shard = pl.BlockSpec((1, D, W_IN_SHARD), lambda q, t: (q, 0, 0))
    return pl.pallas_call(
        body, name="w_in_grad", grid=(N_DEV, n_k),
        in_specs=[pl.BlockSpec((D, tk), lambda q, t: (0, t)), dp_spec(0), dp_spec(1), dp_spec(2), ANY],
        out_specs=[shard, shard],
        out_shape=[jax.ShapeDtypeStruct((N_DEV, D, W_IN_SHARD), F32), jax.ShapeDtypeStruct((N_DEV, D, W_IN_SHARD), BF16)],
        scratch_shapes=[pltpu.VMEM((D, W_IN_SHARD), F32)],
        compiler_params=_params(2),
    )(ut, dproj, dproj, dproj, token)


def _x_grad(dproj, win_t, x, dh, norm_g, token, tm):
    n_rows = x.shape[0]
    n_k = N_GROUPS * D // X_GRAD_K

    def body(dp_ref, wt_ref, x_ref, dh_ref, g_ref, token_ref, gx_ref, dg_ref, acc):
        del token_ref
        i, g = pl.program_id(0), pl.program_id(1)

        @pl.when((i == 0) & (g == 0))
        def _():
            dg_ref[...] = jnp.zeros_like(dg_ref)

        @pl.when(g == 0)
        def _():
            acc[...] = _dot(dp_ref[...], wt_ref[...])

        @pl.when(g > 0)
        def _():
            acc[...] += _dot(dp_ref[...], wt_ref[...])

        @pl.when(g == n_k - 1)
        def _():
            du = acc[...]
            xf = x_ref[...]
            r = lax.rsqrt(_mean(xf * xf) + EPS)
            xn = xf * r
            dun = du * g_ref[...]
            gx_ref[...] = dh_ref[...] + r * (dun - xn * _mean(dun * xn))
            dg_ref[...] += _fold8(du * xn)

    return pl.pallas_call(
        body, name="x_grad", grid=(n_rows // tm, n_k),
        in_specs=[pl.BlockSpec((tm, X_GRAD_K), lambda i, g: (i, g)), pl.BlockSpec((X_GRAD_K, D), lambda i, g: (g, 0)),
                  pl.BlockSpec((tm, D), lambda i, g: (i, 0)), pl.BlockSpec((tm, D), lambda i, g: (i, 0)),
                  _const((1, D)), ANY],
        out_specs=[pl.BlockSpec((tm, D), lambda i, g: (i, 0)), _const((SUBLANES, D))],
        out_shape=[jax.ShapeDtypeStruct((n_rows, D), F32), jax.ShapeDtypeStruct((SUBLANES, D), F32)],
        scratch_shapes=[pltpu.VMEM((tm, D), F32)],
        compiler_params=_params(2),
    )(dproj, win_t, x, dh, norm_g, token)


def _local_step(x, mem, target, norm_g, conv_b_b, ln_g, ln_b, mem_g, final_g, shards):
    n_rows = x.shape[0]
    tm = min(512, n_rows)
    big = min(1024, n_rows)
    u16, ut = _rmsnorm_fwd(x, norm_g, big)
    proj, win_t, _, wkv_g, wo4_g, cw_g = _proj_fwd_gather(u16, shards, min(2048, n_rows))
    cw_rows = cw_g.transpose(1, 0, 2).reshape((SUBLANES + HALO) * LANE_GROUPS, LANES)
    cw_a, cw_b = cw_rows[:SUBLANES * LANE_GROUPS], cw_rows[SUBLANES * LANE_GROUPS:]
    kv16, mn16 = _kv_fwd(mem, mem_g, wkv_g)
    sa16, ya = _branch_a_fwd(proj, wo4_g, cw_a, tm)
    cb, sb16, yb = _branch_b_fwd(proj, wo4_g, cw_b, conv_b_b, ln_g, ln_b, tm)
    sx16, yx = _branch_x_fwd(proj, kv16, wo4_g, tm)
    dh, dya, dyb, dyx, dproj, dw4, dfg, sq = _merge_fwd_bwd(proj, ya, yb, yx, x, target, wo4_g, final_g,
                                                             min(256, n_rows))
    dproj, dw4, dwa = _branch_a_bwd(dya, proj, sa16, wo4_g, cw_a, dproj, dw4, tm)
    dproj, dw4, dwb, dbb, dlg, dlb = _branch_b_bwd(dyb, proj, cb, sb16, wo4_g, cw_b, ln_g, ln_b, dproj, dw4, tm)
    dproj, dw4, dkv = _branch_x_bwd(dyx, proj, sx16, kv16, wo4_g, dproj, dw4, tm)
    dwkv_g, dmg = _kv_bwd(dkv, mem, mem_g, mn16, wkv_g)
    dw4 = dw4.reshape(4, N_DEV, D // N_DEV, D)
    small_moving = _scatter_start("small_grads_start", [dw4.astype(BF16), dwkv_g.astype(BF16)],
                                  lambda refs: [refs[0].at[w] for w in range(4)] + [refs[1]], 5)
    dwin_g, dwin16 = _w_in_grad(ut, dproj, min(2048, n_rows), small_moving["token"])
    w_in_moving = _scatter_start("w_in_grad_start", [dwin16], lambda refs: list(refs), 1)
    gx, dng = _x_grad(dproj, win_t, x, dh, norm_g, w_in_moving["token"], big)
    land4, landkv = _scatter_wait(small_moving, dng)
    landin, = _scatter_wait(w_in_moving, dng)
    small = {SV_NORM_G: dng, SV_CONV_B_B: dbb, SV_LN_G: dlg, SV_LN_B: dlb, SV_MEM_G: dmg, SV_FINAL_G: dfg, SV_LOSS: sq}
    grads = [(dwin_g[None], landin[None]), (dw4, land4), (dwkv_g[None], landkv[None])]
    return gx, grads, small, dwa.reshape(K_A, D), dwb.reshape(K_B, D)


def _allgather_small(small, conv_rows):
    keys = sorted(small)

    def body(*refs):
        parts, (conv_ref, out_ref, mine, send, recv) = refs[:len(keys)], refs[len(keys):]
        x, y, c, chips = _place()
        me, sibling = 4 * x + 2 * y + c, (x, y, 1 - c)
        mine[pl.ds(0, SV_CONV_A), :] = jnp.zeros((SV_CONV_A, D), F32)
        for key, part in zip(keys, parts):
            mine[key:key + 1, :] = jnp.sum(part[...], axis=0, keepdims=True)
        mine[pl.ds(SV_CONV_A, SV_ROWS - SV_CONV_A), :] = conv_ref[...]
        out_ref[me] = mine[...]

        def copy(k, block, to, from_mine=False):
            return pltpu.make_async_remote_copy(
                src_ref=mine if from_mine else out_ref.at[block], dst_ref=out_ref.at[block],
                send_sem=send.at[k], recv_sem=recv.at[k], device_id=to, device_id_type=MESH)

        first = [copy(0, me, sibling, from_mine=True)]
        first += [copy(1 + j, me, (*chip, c), from_mine=True) for j, chip in enumerate(chips)]
        for cp in first:
            cp.start()
        passed = []
        for j, (px, py) in enumerate(chips):
            block = 4 * px + 2 * py + c
            copy(1 + j, block, sibling).wait_recv()
            passed.append(copy(4 + j, block, sibling))
            passed[-1].start()
        copy(0, 4 * x + 2 * y + 1 - c, sibling).wait_recv()
        for j, (px, py) in enumerate(chips):
            copy(4 + j, 4 * px + 2 * py + 1 - c, sibling).wait_recv()
        for cp in first + passed:
            cp.wait_send()

    vmem = pl.BlockSpec(memory_space=pltpu.VMEM)
    return pl.pallas_call(
        body, name="allgather_small",
        in_specs=[vmem] * (len(keys) + 1), out_specs=vmem,
        out_shape=jax.ShapeDtypeStruct((N_DEV, SV_ROWS, D), F32),
        scratch_shapes=[pltpu.VMEM((SV_ROWS, D), F32), pltpu.SemaphoreType.DMA((7,)), pltpu.SemaphoreType.DMA((7,))],
    )(*[small[k] for k in keys], conv_rows)


def _adamw(w, g, m, v):
    m = ADAM_B1 * m + (1.0 - ADAM_B1) * g
    v = ADAM_B2 * v + (1.0 - ADAM_B2) * (g * g)
    m_hat = m / (1.0 - ADAM_B1 ** ADAM_STEP)
    v_hat = v / (1.0 - ADAM_B2 ** ADAM_STEP)
    return -ADAM_LR * (m_hat / (jnp.sqrt(v_hat) + ADAM_EPS) + ADAM_WD * w), m, v


def _adamw_shard(own, landed, piece, k_arr, w, m, v, tr):
    n_r, n_c = w.shape
    n_landed = landed.shape[1]

    def body(k_ref, own_ref, *refs):
        del k_ref
        landed_refs, (w_ref, m_ref, v_ref, g_out, d_out, m_out, v_out) = refs[:n_landed], refs[n_landed:]
        g = own_ref[0, 0]
        for landed_ref in landed_refs:
            g = g + landed_ref[0, 0].astype(F32)
        g_out[...] = g
        d_out[...], m_out[...], v_out[...] = _adamw(w_ref[...], g, m_ref[...], v_ref[...])

    blk = (1, 1, tr, n_c)
    flat = pl.BlockSpec((tr, n_c), lambda r, k: (r, 0))
    return pl.pallas_call(
        body, name="adamw_shard",
        grid_spec=pltpu.PrefetchScalarGridSpec(
            num_scalar_prefetch=1, grid=(n_r // tr,),
            in_specs=[pl.BlockSpec(blk, lambda r, k: (piece, k[0], r, 0))]
            + [pl.BlockSpec(blk, functools.partial(lambda r, k, j: (piece, j, r, 0), j=j)) for j in range(n_landed)]
            + [flat] * 3,
            out_specs=[flat] * 4),
        out_shape=[jax.ShapeDtypeStruct((n_r, n_c), F32)] * 4,
        compiler_params=_params(1),
    )(k_arr, own, *([landed] * n_landed), w, m, v)


def _adamw_small(gathered, k_arr, vectors, conv_a, conv_b):
    n_vec = len(vectors)
    cols = D // N_DEV

    def body(k_ref, full_ref, cols_ref, *refs):
        del k_ref
        ins, outs = refs[:3 * (n_vec + 2)], refs[3 * (n_vec + 2):-2]
        tot, tot_cols = refs[-2:]
        tot[...] = full_ref[0]
        tot_cols[...] = cols_ref[0]
        for dev in range(1, N_DEV):
            tot[...] += full_ref[dev]
            tot_cols[...] += cols_ref[dev]
        loss = (0.5 / D) * jnp.sum(tot[SV_LOSS:SV_LOSS + 1, :])
        outs[0][...] = jnp.full(outs[0].shape, loss, F32)
        grads = [tot[n:n + 1, :] for n in range(n_vec)]
        grads += [tot_cols[pl.ds(SV_CONV_A, K_A), :], tot_cols[pl.ds(SV_CONV_B, K_B), :]]
        for n, g in enumerate(grads):
            w_ref, m_ref, v_ref = ins[3 * n:3 * n + 3]
            g_out, d_out, m_out, v_out = outs[1 + 4 * n:5 + 4 * n]
            g_out[...] = g
            d_out[...], m_out[...], v_out[...] = _adamw(w_ref[...], g, m_ref[...], v_ref[...])

    weights = list(vectors) + [conv_a, conv_b]
    flat_in = [a for wmv in weights for a in wmv]
    out_shape = [jax.ShapeDtypeStruct((SUBLANES, 128), F32)]
    for wmv in weights:
        out_shape += [jax.ShapeDtypeStruct(wmv[0].shape, F32)] * 4
    return pl.pallas_call(
        body, name="adamw_small",
        grid_spec=pltpu.PrefetchScalarGridSpec(
            num_scalar_prefetch=1, grid=(1,),
            in_specs=[pl.BlockSpec((N_DEV, SV_ROWS, D), lambda i, k: (0, 0, 0)),
                      pl.BlockSpec((N_DEV, SV_ROWS, cols), lambda i, k: (0, 0, k[0]))]
            + [pl.BlockSpec(a.shape, lambda i, k: (0, 0)) for a in flat_in],
            out_specs=[pl.BlockSpec(s.shape, lambda i, k: (0, 0)) for s in out_shape],
            scratch_shapes=[pltpu.VMEM((SV_ROWS, D), F32), pltpu.VMEM((SV_ROWS, cols), F32)]),
        out_shape=out_shape,
        compiler_params=_params(1),
    )(k_arr, gathered, gathered, *flat_in)


def kernel(x, mem, norm_g, w_in, conv_a_w, w_out_a, conv_b_w, conv_b_b, ln_b_g, ln_b_b, w_out_b, mem_norm_g, w_kv, w_out_x, w_o, final_g, loss_target, m_norm_g, m_w_in, m_conv_a_w, m_w_out_a, m_conv_b_w, m_conv_b_b, m_ln_b_g, m_ln_b_b, m_w_out_b, m_mem_norm_g, m_w_kv, m_w_out_x, m_w_o, m_final_g, v_norm_g, v_w_in, v_conv_a_w, v_w_out_a, v_conv_b_w, v_conv_b_b, v_ln_b_g, v_ln_b_b, v_w_out_b, v_mem_norm_g, v_w_kv, v_w_out_x, v_w_o, v_final_g):
    xi, yi, ci = lax.axis_index("x"), lax.axis_index("y"), lax.axis_index("c")
    k_arr = jnp.reshape(4 * xi + 2 * yi + ci, (1,)).astype(jnp.int32)

    cw = jnp.concatenate([jnp.pad(conv_a_w[0], ((0, SUBLANES - K_A), (0, 0))),
                          jnp.pad(conv_b_w[0], ((0, HALO - K_B), (0, 0)))], axis=0)
    wo4 = jnp.stack([w_out_a[0], w_out_b[0], w_out_x[0], w_o[0]]).astype(BF16)
    shards = [w_in[0].astype(BF16), w_kv[0].astype(BF16), wo4, cw]

    final_g2 = final_g.reshape(1, D)
    gx, grads, small, dwa, dwb = _local_step(
        x[0], mem[0], loss_target[0], norm_g, conv_b_b, ln_b_g, ln_b_b, mem_norm_g, final_g2, shards)

    conv_rows = jnp.concatenate([jnp.pad(dwa, ((0, SUBLANES - K_A), (0, 0))),
                                 jnp.pad(dwb, ((0, HALO - K_B), (0, 0)))], axis=0)
    gathered_small = _allgather_small(small, conv_rows)

    tiles = [256, D // N_DEV, 256]

    def shard(a, l, w, m, v):
        return _adamw_shard(grads[a][0], grads[a][1], l, k_arr, w[0], m[0], v[0], tiles[a])

    res = {
        "w_in": shard(0, 0, w_in, m_w_in, v_w_in),
        "w_out_a": shard(1, 0, w_out_a, m_w_out_a, v_w_out_a),
        "w_out_b": shard(1, 1, w_out_b, m_w_out_b, v_w_out_b),
        "w_out_x": shard(1, 2, w_out_x, m_w_out_x, v_w_out_x),
        "w_o": shard(1, 3, w_o, m_w_o, v_w_o),
        "w_kv": shard(2, 0, w_kv, m_w_kv, v_w_kv),
    }
    res = {name: tuple(r[None] for r in four) for name, four in res.items()}
    vectors = [(norm_g, m_norm_g, v_norm_g), (conv_b_b, m_conv_b_b, v_conv_b_b), (ln_b_g, m_ln_b_g, v_ln_b_g),
               (ln_b_b, m_ln_b_b, v_ln_b_b), (mem_norm_g, m_mem_norm_g, v_mem_norm_g),
               (final_g2, m_final_g.reshape(1, D), v_final_g.reshape(1, D))]
    out = _adamw_small(gathered_small, k_arr, vectors, (conv_a_w[0], m_conv_a_w[0], v_conv_a_w[0]),
                       (conv_b_w[0], m_conv_b_w[0], v_conv_b_w[0]))
    loss = out[0][0, 0]
    names = ["norm_g", "conv_b_b", "ln_b_g", "ln_b_b", "mem_norm_g", "final_g", "conv_a_w", "conv_b_w"]
    for n, name in enumerate(names):
        four = out[1 + 4 * n:5 + 4 * n]
        if name == "final_g":
            four = [r.reshape(D) for r in four]
        elif name.startswith("conv_") and name.endswith("_w"):
            four = [r[None] for r in four]
        res[name] = tuple(four)

    order = ["norm_g", "w_in", "conv_a_w", "w_out_a", "conv_b_w", "conv_b_b", "ln_b_g", "ln_b_b", "w_out_b",
             "mem_norm_g", "w_kv", "w_out_x", "w_o", "final_g"]
    return (loss, gx[None], *[res[n][0] for n in order], *[res[n][1] for n in order],
            *[res[n][2] for n in order], *[res[n][3] for n in order])
```

```python
import functools

import jax
import jax.numpy as jnp
from jax import lax
from jax.experimental import pallas as pl
from jax.experimental.pallas import tpu as pltpu

F32, BF16 = jnp.float32, jnp.bfloat16
D = 1024
N_DEV = 8
N_HEADS = 4
HEAD_DIM = D // N_HEADS
N_GROUPS = 12
W_IN_SHARD = N_GROUPS * D // N_DEV
UNIT = 512
X_GRAD_K = 2 * D
K_A, K_B = 3, 31
EPS = 1e-6
HALO = 32
SUBLANES = 8
LANES = 128
LANE_GROUPS = D // LANES
TAP_GROUP = 16
CONV_BLOCK = 32
ELEM_ROWS = 16
CONV_PARTIAL_SUMS = 4
VMEM_LIMIT = 56 << 20
MESH = pl.DeviceIdType.MESH
ANY = pl.BlockSpec(memory_space=pl.ANY)
HBM = pl.BlockSpec(memory_space=pltpu.HBM)
SEM = pl.BlockSpec(memory_space=pltpu.SEMAPHORE)
SIDE_EFFECT = pltpu.SideEffectType.DATAFLOW_SIDE_EFFECTING

G_BA, G_CA, G_XA, G_ZA, G_VB, G_GB, G_ZB, G_Q, G_ZX, G_GA, G_GBB, G_GX = range(N_GROUPS)
DP_POS = (0, 1, 2, 3, 6, 7, 8, 4, 5, 9, 10, 11)

ADAM_LR, ADAM_B1, ADAM_B2, ADAM_EPS, ADAM_WD, ADAM_STEP = 0.001, 0.9, 0.999, 1e-08, 0.01, 10

SV_NORM_G, SV_CONV_B_B, SV_LN_G, SV_LN_B, SV_MEM_G, SV_FINAL_G, SV_LOSS = range(7)
SV_CONV_A, SV_CONV_B, SV_ROWS = 8, 16, 48


def _dot(a, b):
    return jnp.dot(a, b, preferred_element_type=F32)


def _dot_nt(a, b):
    return lax.dot_general(a, b, (((1,), (1,)), ((), ())), preferred_element_type=F32)


def _dot_tn(a, b):
    return lax.dot_general(a, b, (((0,), (0,)), ((), ())), preferred_element_type=F32)


def _silu_and_grad(z):
    s = jax.nn.sigmoid(z)
    return z * s, s * (1.0 + z * (1.0 - s))


def _fold8(a):
    return a.reshape(a.shape[0] // SUBLANES, SUBLANES, a.shape[1]).sum(axis=0)


def _mean(a):
    return jnp.mean(a, axis=-1, keepdims=True)


def _f32(ref):
    return ref[...].astype(F32)


def _params(n_grid):
    return pltpu.CompilerParams(dimension_semantics=("arbitrary",) * n_grid, vmem_limit_bytes=VMEM_LIMIT)


def _rows(tm, col=0):
    return pl.BlockSpec((tm, D), lambda i: (i, col))


def _prev_halo(tm, col=0):
    return pl.BlockSpec((HALO, D), lambda i: (jnp.maximum(i * (tm // HALO) - 1, 0), col))


def _next_halo(tm, n_rows, col=0):
    last = n_rows // HALO - 1
    return pl.BlockSpec((HALO, D), lambda i: (jnp.minimum((i + 1) * (tm // HALO), last), col))


def _const(shape):
    return pl.BlockSpec(shape, lambda *_: (0,) * len(shape))


def _w_out_spec(which):
    return pl.BlockSpec((N_DEV, None, D // N_DEV, D), lambda *_: (0, which, 0, 0))


def _to_time_major(t_ref, row0, x):
    n = x.shape[0]
    for j in range(LANE_GROUPS):
        t_ref[pl.ds(row0 * LANE_GROUPS + j, n, stride=LANE_GROUPS), :] = x[:, j * LANES:(j + 1) * LANES]


def _from_time_major(t_ref, n, row0=0):
    return jnp.concatenate([t_ref[pl.ds(row0 * LANE_GROUPS + j, n, stride=LANE_GROUPS), :]
                            for j in range(LANE_GROUPS)], axis=-1)


def _row(ref, t):
    start = t * LANE_GROUPS
    if not isinstance(start, int):
        start = pl.multiple_of(start, LANE_GROUPS)
    return ref[pl.ds(start, LANE_GROUPS), :]


def _tap_groups(taps):
    return [taps[first:first + TAP_GROUP] for first in range(0, len(taps), TAP_GROUP)]


def _conv(o_ref, e_ref, w_ref, taps, n_rows, bias_ref=None):
    for n_group, group in enumerate(_tap_groups(taps)):
        weights = [_row(w_ref, k) for k, _ in group]

        def block(c, carry, n_group=n_group, group=group, weights=weights):
            t0 = c * CONV_BLOCK
            window = {}
            for t in range(CONV_BLOCK):
                parts = [None] * min(CONV_PARTIAL_SUMS, len(group))
                for n, (_, off) in enumerate(group):
                    if t + off not in window:
                        window[t + off] = _row(e_ref, t0 + t + off)
                    term = weights[n] * window[t + off]
                    parts[n % len(parts)] = term if parts[n % len(parts)] is None else parts[n % len(parts)] + term
                window.pop(t + min(off for _, off in group), None)
                while len(parts) > 1:
                    parts = [parts[n] + parts[n + 1] for n in range(0, len(parts) - 1, 2)] + parts[len(parts) & ~1:]
                out = parts[0]
                if n_group > 0:
                    out = out + _row(o_ref, t0 + t)
                elif bias_ref is not None:
                    out = out + bias_ref[...]
                o_ref[pl.ds(pl.multiple_of((t0 + t) * LANE_GROUPS, LANE_GROUPS), LANE_GROUPS), :] = out
            return carry

        lax.fori_loop(0, n_rows // CONV_BLOCK, block, 0)


def _conv_wgrad(dw_ref, d_ref, e_ref, taps, n_rows):
    for group in _tap_groups(taps):
        def block(c, sums, group=group):
            t0 = c * CONV_BLOCK
            sums = list(sums)
            window = {}
            for t in range(CONV_BLOCK):
                d = _row(d_ref, t0 + t)
                for n, (_, off) in enumerate(group):
                    if t + off not in window:
                        window[t + off] = _row(e_ref, t0 + t + off)
                    sums[n] = sums[n] + d * window[t + off]
                window.pop(t + min(off for _, off in group), None)
            return tuple(sums)

        sums = lax.fori_loop(0, n_rows // CONV_BLOCK, block, tuple(_row(dw_ref, k) for k, _ in group))
        for (k, _), total in zip(group, sums):
            dw_ref[pl.ds(k * LANE_GROUPS, LANE_GROUPS), :] = total


FWD_TAPS_A = [(k, HALO - (K_A - 1) + k) for k in range(K_A)]
BWD_TAPS_A = [(k, K_A - 1 - k) for k in range(K_A)]
FWD_TAPS_B = [(k, HALO - (K_B - 1) + k) for k in range(K_B)]
BWD_TAPS_B = [(k, K_B - 1 - k) for k in range(K_B)]


def _time_major(n_rows):
    return pltpu.VMEM((n_rows * LANE_GROUPS, LANES), F32)


def _kv_fwd(mem, mem_g, wkv_g):
    m_len = mem.shape[0]

    def body(mem_ref, g_ref, w_ref, kv_ref, mn_ref):
        mf = mem_ref[...]
        r = lax.rsqrt(_mean(mf * mf) + EPS)
        mn = ((mf * r) * g_ref[...]).astype(BF16)
        mn_ref[...] = mn
        for b in range(2 * N_HEADS):
            kv_ref[b] = _dot(mn, w_ref[b]).astype(BF16)

    return pl.pallas_call(
        body, name="kv_fwd", grid=(1,),
        in_specs=[_const((m_len, D)), _const((1, D)), _const((2 * N_HEADS, D, HEAD_DIM))],
        out_specs=[_const((2 * N_HEADS, m_len, HEAD_DIM)), _const((m_len, D))],
        out_shape=[jax.ShapeDtypeStruct((2 * N_HEADS, m_len, HEAD_DIM), BF16), jax.ShapeDtypeStruct((m_len, D), BF16)],
        compiler_params=_params(1),
    )(mem, mem_g, wkv_g)


def _kv_bwd(dkv, mem, mem_g, mn16, wkv_g):
    m_len = mem.shape[0]

    def body(dkv_ref, mem_ref, g_ref, mn_ref, w_ref, dw_ref, dg_ref):
        mn = mn_ref[...]
        dmn = jnp.zeros((m_len, D), F32)
        for b in range(2 * N_HEADS):
            d16 = dkv_ref[b].astype(BF16)
            dw_ref[b] = _dot_tn(mn, d16)
            dmn = dmn + _dot_nt(d16, w_ref[b])
        mf = mem_ref[...]
        r = lax.rsqrt(_mean(mf * mf) + EPS)
        dg_ref[...] = _fold8(dmn * (mf * r))

    return pl.pallas_call(
        body, name="kv_bwd", grid=(1,),
        in_specs=[_const((2 * N_HEADS, m_len, HEAD_DIM)), _const((m_len, D)), _const((1, D)), _const((m_len, D)),
                  _const((2 * N_HEADS, D, HEAD_DIM))],
        out_specs=[_const((2 * N_HEADS, D, HEAD_DIM)), _const((SUBLANES, D))],
        out_shape=[jax.ShapeDtypeStruct((2 * N_HEADS, D, HEAD_DIM), F32), jax.ShapeDtypeStruct((SUBLANES, D), F32)],
        compiler_params=_params(1),
    )(dkv, mem, mem_g, mn16, wkv_g)


def _rmsnorm_fwd(x, norm_g, tm):
    n_rows = x.shape[0]

    def body(x_ref, g_ref, u_ref, ut_ref):
        xf = x_ref[...]
        u = (xf * lax.rsqrt(_mean(xf * xf) + EPS)) * g_ref[...]
        u_ref[...] = u.astype(BF16)
        ut_ref[...] = u.T.astype(BF16)

    return pl.pallas_call(
        body, name="rmsnorm_fwd", grid=(n_rows // tm,),
        in_specs=[_rows(tm), _const((1, D))],
        out_specs=[_rows(tm), pl.BlockSpec((D, tm), lambda i: (0, i))],
        out_shape=[jax.ShapeDtypeStruct((n_rows, D), BF16), jax.ShapeDtypeStruct((D, n_rows), BF16)],
        compiler_params=_params(1),
    )(x, norm_g)


def _place():
    x, y, c = lax.axis_index("x"), lax.axis_index("y"), lax.axis_index("c")
    other_chips = [(1 - x, y), (x, 1 - y), (1 - x, 1 - y)]
    return x, y, c, other_chips


def _arrival_order():
    x, y, c, chips = _place()
    order = [4 * x + 2 * y + c, 4 * x + 2 * y + 1 - c]
    for px, py in chips:
        order += [4 * px + 2 * py + c, 4 * px + 2 * py + 1 - c]
    return order


def _proj_fwd_gather(u16, blocks, tm):
    n = len(blocks)
    n_rows = u16.shape[0]
    n_i = n_rows // tm
    per_shard = W_IN_SHARD // UNIT
    assert n_i >= per_shard

    def wt_index(p, i, order):
        return (_dp_unit(per_shard * order[p] + jnp.minimum(i, per_shard - 1)), 0)

    def body(order_ref, u_ref, *refs):
        src, proj_ref, wt_ref, out = refs[:n], refs[n], refs[n + 1], refs[n + 2:2 * n + 2]
        wbuf, stage_sem, send, recv, own_sem = refs[2 * n + 2:]
        p, i = pl.program_id(0), pl.program_id(1)
        x, y, c, chips = _place()
        me, sibling = 4 * x + 2 * y + c, (x, y, 1 - c)

        def copy(t, k, block, to, from_input=False):
            return pltpu.make_async_remote_copy(
                src_ref=src[t] if from_input else out[t].at[block], dst_ref=out[t].at[block],
                send_sem=send.at[t, k], recv_sem=recv.at[t, k], device_id=to, device_id_type=MESH)

        def own_copies():
            return [pltpu.make_async_copy(src[t], out[t].at[me], own_sem.at[t]) for t in range(n)]

        def first_copies():
            first = []
            for t in range(n):
                first.append(copy(t, 0, me, sibling, from_input=True))
                first += [copy(t, 1 + j, me, (*chip, c), from_input=True) for j, chip in enumerate(chips)]
            return first

        def stage(slot, block):
            return pltpu.make_async_copy(out[0].at[block], wbuf.at[slot], stage_sem.at[slot])

        @pl.when((p == 0) & (i == 0))
        def _():
            for cp in own_copies() + first_copies():
                cp.start()
            mine = pltpu.make_async_copy(src[0], wbuf.at[0], stage_sem.at[0])
            mine.start()
            mine.wait()

        @pl.when((p > 0) & (i == 0))
        def _():
            stage(p % 2, order_ref[p]).wait()

        proj_ref[...] = _dot(u_ref[...], wbuf[p % 2])
        for r in range(per_shard):
            @pl.when(i == r)
            def _(r=r):
                wt_ref[...] = wbuf[p % 2, :, r * UNIT:(r + 1) * UNIT].astype(F32).T.astype(BF16)

        for nxt in range(1, N_DEV):
            @pl.when((p == nxt - 1) & (i == n_i - 1))
            def _(nxt=nxt):
                if nxt == 1:
                    block = 4 * x + 2 * y + 1 - c
                    copy(0, 0, block, sibling).wait_recv()
                else:
                    j, passed_on = divmod(nxt - 2, 2)
                    px, py = chips[j]
                    if passed_on:
                        block = 4 * px + 2 * py + 1 - c
                        copy(0, 4 + j, block, sibling).wait_recv()
                    else:
                        block = 4 * px + 2 * py + c
                        copy(0, 1 + j, block, sibling).wait_recv()
                        copy(0, 4 + j, block, sibling).start()
                stage(nxt % 2, block).start()

        @pl.when((p == N_DEV - 1) & (i == n_i - 1))
        def _():
            passed = [copy(0, 4 + j, 4 * px + 2 * py + c, sibling) for j, (px, py) in enumerate(chips)]
            for j, (px, py) in enumerate(chips):
                for t in range(1, n):
                    block = 4 * px + 2 * py + c
                    copy(t, 1 + j, block, sibling).wait_recv()
                    passed.append(copy(t, 4 + j, block, sibling))
                    passed[-1].start()
            for t in range(1, n):
                copy(t, 0, 4 * x + 2 * y + 1 - c, sibling).wait_recv()
                for j, (px, py) in enumerate(chips):
                    copy(t, 4 + j, 4 * px + 2 * py + 1 - c, sibling).wait_recv()
            for cp in first_copies() + passed:
                cp.wait_send()
            for cp in own_copies():
                cp.wait()

    return pl.pallas_call(
        body, name="proj_fwd_gather",
        grid_spec=pltpu.PrefetchScalarGridSpec(
            num_scalar_prefetch=1, grid=(N_DEV, n_i),
            in_specs=[pl.BlockSpec((tm, D), lambda p, i, order: (i, 0))] + [ANY] * n,
            out_specs=[pl.BlockSpec((tm, W_IN_SHARD), lambda p, i, order: (i, order[p])),
                       pl.BlockSpec((UNIT, D), wt_index)] + [ANY] * n,
            scratch_shapes=[pltpu.VMEM((2, D, W_IN_SHARD), BF16), pltpu.SemaphoreType.DMA((2,)),
                            pltpu.SemaphoreType.DMA((n, 7)), pltpu.SemaphoreType.DMA((n, 7)),
                            pltpu.SemaphoreType.DMA((n,))]),
        out_shape=[jax.ShapeDtypeStruct((n_rows, N_GROUPS * D), F32), jax.ShapeDtypeStruct((N_GROUPS * D, D), BF16)]
        + [jax.ShapeDtypeStruct((N_DEV, *b.shape), b.dtype) for b in blocks],
        compiler_params=_params(2),
    )(jnp.stack(_arrival_order()).astype(jnp.int32), u16, *blocks)


def _branch_a_fwd(proj, wo4_g, cw_a, tm):
    n_rows = proj.shape[0]

    def body(bp, cp, xp, za, cph, xph, w_ref, cw_ref, sa_ref, ya_ref, e_scr, o_scr):
        i = pl.program_id(0)
        _to_time_major(e_scr, 0, jnp.where(i > 0, _f32(cph) * _f32(xph), 0.0))
        for r0 in range(0, tm, ELEM_ROWS):
            rows = pl.ds(r0, ELEM_ROWS)
            _to_time_major(e_scr, HALO + r0, cp[rows, :].astype(F32) * xp[rows, :].astype(F32))
        _conv(o_scr, e_scr, cw_ref, FWD_TAPS_A, tm)
        for r0 in range(0, tm, ELEM_ROWS):
            rows = pl.ds(r0, ELEM_ROWS)
            ca = _from_time_major(o_scr, ELEM_ROWS, r0)
            sa_ref[rows, :] = (jax.nn.silu(za[rows, :].astype(F32)) * (bp[rows, :].astype(F32) * ca)).astype(BF16)
        ya_ref[...] = _dot(sa_ref[...], w_ref[...].reshape(D, D))

    return pl.pallas_call(
        body, name="branch_a_fwd", grid=(n_rows // tm,),
        in_specs=[_rows(tm, G_BA), _rows(tm, G_CA), _rows(tm, G_XA), _rows(tm, G_ZA),
                  _prev_halo(tm, G_CA), _prev_halo(tm, G_XA), _w_out_spec(0), _const(cw_a.shape)],
        out_specs=[_rows(tm), _rows(tm)],
        out_shape=[jax.ShapeDtypeStruct((n_rows, D), BF16), jax.ShapeDtypeStruct((n_rows, D), F32)],
        scratch_shapes=[_time_major(tm + HALO), _time_major(tm)],
        compiler_params=_params(1),
    )(proj, proj, proj, proj, proj, proj, wo4_g, cw_a)


def _layernorm_parts(cb, lg, lb):
    xc = cb - _mean(cb)
    rstd = lax.rsqrt(_mean(xc * xc) + EPS)
    xhat = xc * rstd
    return xhat, rstd, xhat * lg + lb


def _branch_b_fwd(proj, wo4_g, cw_b, conv_b_b, ln_g, ln_b, tm):
    n_rows = proj.shape[0]

    def body(vb, gb, zb, vbh, gbh, w_ref, cw_ref, bb_ref, lg_ref, lb_ref, cb_ref, sb_ref, yb_ref, e_scr, o_scr):
        i = pl.program_id(0)
        vbh, gbh = _f32(vbh), _f32(gbh)
        lg, lb = lg_ref[...], lb_ref[...]
        _to_time_major(e_scr, 0, jnp.where(i > 0, vbh * jax.nn.sigmoid(gbh), 0.0))
        for r0 in range(0, tm, ELEM_ROWS):
            rows = pl.ds(r0, ELEM_ROWS)
            _to_time_major(e_scr, HALO + r0, vb[rows, :].astype(F32) * jax.nn.sigmoid(gb[rows, :].astype(F32)))
        _conv(o_scr, e_scr, cw_ref, FWD_TAPS_B, tm, bias_ref=bb_ref)
        for r0 in range(0, tm, ELEM_ROWS):
            rows = pl.ds(r0, ELEM_ROWS)
            cb = _from_time_major(o_scr, ELEM_ROWS, r0)
            cb_ref[rows, :] = cb
            _, _, ln = _layernorm_parts(cb, lg, lb)
            sb_ref[rows, :] = (jax.nn.silu(zb[rows, :].astype(F32)) * jax.nn.silu(ln)).astype(BF16)
        yb_ref[...] = _dot(sb_ref[...], w_ref[...].reshape(D, D))

    return pl.pallas_call(
        body, name="branch_b_fwd", grid=(n_rows // tm,),
        in_specs=[_rows(tm, G_VB), _rows(tm, G_GB), _rows(tm, G_ZB), _prev_halo(tm, G_VB), _prev_halo(tm, G_GB),
                  _w_out_spec(1), _const(cw_b.shape), _const((LANE_GROUPS, LANES)), _const((1, D)), _const((1, D))],
        out_specs=[_rows(tm), _rows(tm), _rows(tm)],
        out_shape=[jax.ShapeDtypeStruct((n_rows, D), F32), jax.ShapeDtypeStruct((n_rows, D), BF16),
                   jax.ShapeDtypeStruct((n_rows, D), F32)],
        scratch_shapes=[_time_major(tm + HALO), _time_major(tm)],
        compiler_params=_params(1),
    )(proj, proj, proj, proj, proj, wo4_g, cw_b, conv_b_b.reshape(LANE_GROUPS, LANES), ln_g, ln_b)


def _attention(q16, kv_ref):
    probs, outs = [], []
    for h in range(N_HEADS):
        s = _dot_nt(q16[:, h * HEAD_DIM:(h + 1) * HEAD_DIM], kv_ref[h]) * (HEAD_DIM ** -0.5)
        e = jnp.exp(s - jnp.max(s, axis=-1, keepdims=True))
        p = e / jnp.sum(e, axis=-1, keepdims=True)
        probs.append(p)
        outs.append(_dot(p.astype(BF16), kv_ref[N_HEADS + h]))
    return probs, outs


def _branch_x_fwd(proj, kv16, wo4_g, tm):
    n_rows = proj.shape[0]

    def body(q, zx, kv_ref, w_ref, sx_ref, yx_ref):
        _, outs = _attention(q[...].astype(BF16), kv_ref)
        sx = (jax.nn.silu(_f32(zx)) * jnp.concatenate(outs, axis=-1)).astype(BF16)
        sx_ref[...] = sx
        yx_ref[...] = _dot(sx, w_ref[...].reshape(D, D))

    return pl.pallas_call(
        body, name="branch_x_fwd", grid=(n_rows // tm,),
        in_specs=[_rows(tm, G_Q), _rows(tm, G_ZX), _const(kv16.shape), _w_out_spec(2)],
        out_specs=[_rows(tm), _rows(tm)],
        out_shape=[jax.ShapeDtypeStruct((n_rows, D), BF16), jax.ShapeDtypeStruct((n_rows, D), F32)],
        compiler_params=_params(1),
    )(proj, proj, kv16, wo4_g)


def _merge_fwd_bwd(proj, ya, yb, yx, x, target, wo4_g, final_g, tm):
    n_rows = proj.shape[0]
    inv_d = 1.0 / D

    def body(ga, gb, gx, ya_ref, yb_ref, yx_ref, x_ref, t_ref, w_ref, fg_ref,
             dh_ref, dya_ref, dyb_ref, dyx_ref, dp_ref, dw_ref, dfg_ref, sq_ref):
        i = pl.program_id(0)
        wo = w_ref[...].reshape(D, D)
        sig = [jax.nn.sigmoid(_f32(g)) for g in (ga, gb, gx)]
        ys = [ya_ref[...], yb_ref[...], yx_ref[...]]
        m16 = (sig[0] * ys[0] + sig[1] * ys[1] + sig[2] * ys[2]).astype(BF16)
        h = x_ref[...] + _dot(m16, wo)
        r = lax.rsqrt(_mean(h * h) + EPS)
        hn = h * r
        fg = fg_ref[...]
        err = hn * fg - t_ref[...]
        dy = err * inv_d
        dhn = dy * fg
        dh = r * (dhn - hn * _mean(dhn * hn))
        dh_ref[...] = dh
        dh16 = dh.astype(BF16)
        dm = _dot_nt(dh16, wo)
        for n, out in enumerate((dya_ref, dyb_ref, dyx_ref)):
            out[...] = (sig[n] * dm).astype(BF16)
            dp_ref[:, n * D:(n + 1) * D] = (dm * ys[n] * (sig[n] * (1.0 - sig[n]))).astype(BF16)

        @pl.when(i == 0)
        def _():
            dw_ref[...] = jnp.zeros_like(dw_ref)
            dfg_ref[...] = jnp.zeros_like(dfg_ref)
            sq_ref[...] = jnp.zeros_like(sq_ref)

        dw_ref[0] += _dot_tn(m16, dh16)
        dfg_ref[...] += _fold8(dy * hn)
        sq_ref[...] += _fold8(err * err)

    vec = jax.ShapeDtypeStruct((SUBLANES, D), F32)
    return pl.pallas_call(
        body, name="merge_fwd_bwd", grid=(n_rows // tm,),
        in_specs=[_rows(tm, G_GA), _rows(tm, G_GBB), _rows(tm, G_GX), _rows(tm), _rows(tm), _rows(tm), _rows(tm),
                  _rows(tm), _w_out_spec(3), _const((1, D))],
        out_specs=[_rows(tm), _rows(tm), _rows(tm), _rows(tm), pl.BlockSpec((tm, 3 * D), lambda i: (i, 3)),
                   pl.BlockSpec((1, D, D), lambda i: (3, 0, 0)), _const((SUBLANES, D)), _const((SUBLANES, D))],
        out_shape=[jax.ShapeDtypeStruct((n_rows, D), F32), jax.ShapeDtypeStruct((n_rows, D), BF16),
                   jax.ShapeDtypeStruct((n_rows, D), BF16), jax.ShapeDtypeStruct((n_rows, D), BF16),
                   jax.ShapeDtypeStruct((n_rows, N_GROUPS * D), BF16), jax.ShapeDtypeStruct((4, D, D), F32), vec, vec],
        compiler_params=_params(1),
    )(proj, proj, proj, ya, yb, yx, x, target, wo4_g, final_g)


def _branch_a_bwd(dya, proj, sa16, wo4_g, cw_a, dproj, dw4, tm):
    n_rows = proj.shape[0]
    n_tiles = n_rows // tm

    def body(dya_ref, bp, cp, xp, za, sa_ref, dyan, bpn, zan, cph, xph, w_ref, cw_ref, dp_in, dw_in,
             dp_ref, dw_ref, dwa_ref, e1, e2, o_scr, mm_scr):
        del dp_in, dw_in
        i = pl.program_id(0)
        bpn, zan, cph, xph = (_f32(r) for r in (bpn, zan, cph, xph))
        woa = w_ref[...].reshape(D, D)
        dya16 = dya_ref[...]
        chunks = [pl.ds(r0, ELEM_ROWS) for r0 in range(0, tm, ELEM_ROWS)]
        _to_time_major(e1, 0, jnp.where(i > 0, cph * xph, 0.0))
        for r0, rows in zip(range(0, tm, ELEM_ROWS), chunks):
            _to_time_major(e1, HALO + r0, cp[rows, :].astype(F32) * xp[rows, :].astype(F32))
        _conv(o_scr, e1, cw_ref, FWD_TAPS_A, tm)
        mm_scr[...] = _dot_nt(dya16, woa)
        for r0, rows in zip(range(0, tm, ELEM_ROWS), chunks):
            ca = _from_time_major(o_scr, ELEM_ROWS, r0)
            dsa = mm_scr[rows, :]
            b = bp[rows, :].astype(F32)
            silu_z, dsilu_z = _silu_and_grad(za[rows, :].astype(F32))
            t = dsa * silu_z
            dp_ref[rows, 0 * D:1 * D] = (t * ca).astype(BF16)
            dp_ref[rows, 3 * D:4 * D] = (dsa * (b * ca) * dsilu_z).astype(BF16)
            _to_time_major(e2, r0, t * b)
        dcan = (_dot_nt(dyan[...], woa) * jax.nn.silu(zan)) * bpn
        _to_time_major(e2, tm, jnp.where(i < n_tiles - 1, dcan, 0.0))

        @pl.when(i == 0)
        def _():
            dw_ref[...] = jnp.zeros_like(dw_ref)
            dwa_ref[...] = jnp.zeros_like(dwa_ref)

        _conv_wgrad(dwa_ref, e2, e1, FWD_TAPS_A, tm)
        dw_ref[0] += _dot_tn(sa_ref[...], dya16)
        _conv(o_scr, e2, cw_ref, BWD_TAPS_A, tm)
        for r0, rows in zip(range(0, tm, ELEM_ROWS), chunks):
            dprod = _from_time_major(o_scr, ELEM_ROWS, r0)
            dp_ref[rows, 1 * D:2 * D] = (dprod * xp[rows, :].astype(F32)).astype(BF16)
            dp_ref[rows, 2 * D:3 * D] = (dprod * cp[rows, :].astype(F32)).astype(BF16)

    return pl.pallas_call(
        body, name="branch_a_bwd", grid=(n_tiles,),
        in_specs=[_rows(tm), _rows(tm, G_BA), _rows(tm, G_CA), _rows(tm, G_XA), _rows(tm, G_ZA), _rows(tm),
                  _next_halo(tm, n_rows), _next_halo(tm, n_rows, G_BA), _next_halo(tm, n_rows, G_ZA),
                  _prev_halo(tm, G_CA), _prev_halo(tm, G_XA), _w_out_spec(0), _const(cw_a.shape), ANY, ANY],
        out_specs=[pl.BlockSpec((tm, 4 * D), lambda i: (i, 0)), pl.BlockSpec((1, D, D), lambda i: (0, 0, 0)),
                   _const((K_A * LANE_GROUPS, LANES))],
        out_shape=[jax.ShapeDtypeStruct(dproj.shape, BF16), jax.ShapeDtypeStruct(dw4.shape, F32),
                   jax.ShapeDtypeStruct((K_A * LANE_GROUPS, LANES), F32)],
        input_output_aliases={13: 0, 14: 1},
        scratch_shapes=[_time_major(tm + HALO), _time_major(tm + HALO), _time_major(tm), pltpu.VMEM((tm, D), F32)],
        compiler_params=_params(1),
    )(dya, proj, proj, proj, proj, sa16, dya, proj, proj, proj, proj, wo4_g, cw_a, dproj, dw4)


def _branch_b_bwd(dyb, proj, cb, sb16, wo4_g, cw_b, ln_g, ln_b, dproj, dw4, tm):
    n_rows = proj.shape[0]
    n_tiles = n_rows // tm

    def body(dyb_ref, zb, cb_ref, vb, gb, sb_ref, dybn, zbn, cbn, vbh, gbh, w_ref, cw_ref, lg_ref, lb_ref,
             dp_in, dw_in, dp_ref, dw_ref, dwb_ref, dbb_ref, dlg_ref, dlb_ref, e1, e2, o_scr, mm_scr):
        del dp_in, dw_in
        zbn, vbh, gbh = (_f32(r) for r in (zbn, vbh, gbh))
        i = pl.program_id(0)
        wob = w_ref[...].reshape(D, D)
        lg, lb = lg_ref[...], lb_ref[...]

        def conv_out_grad(dsb, z, c):
            xhat, rstd, ln = _layernorm_parts(c, lg, lb)
            sw, dsw = _silu_and_grad(ln)
            sz, dsz = _silu_and_grad(z)
            dln = (dsb * sz) * dsw
            dxhat = dln * lg
            dcb = rstd * (dxhat - _mean(dxhat) - xhat * _mean(dxhat * xhat))
            return dsb * sw * dsz, dln, xhat, dcb

        @pl.when(i == 0)
        def _():
            dw_ref[...] = jnp.zeros_like(dw_ref)
            dwb_ref[...] = jnp.zeros_like(dwb_ref)
            dbb_ref[...] = jnp.zeros_like(dbb_ref)
            dlg_ref[...] = jnp.zeros_like(dlg_ref)
            dlb_ref[...] = jnp.zeros_like(dlb_ref)

        dyb16 = dyb_ref[...]
        mm_scr[...] = _dot_nt(dyb16, wob)
        dlg, dlb, dbb = (jnp.zeros((SUBLANES, D), F32),) * 3
        for r0 in range(0, tm, ELEM_ROWS):
            rows = pl.ds(r0, ELEM_ROWS)
            dzb, dln, xhat, dcb = conv_out_grad(mm_scr[rows, :], zb[rows, :].astype(F32), cb_ref[rows, :])
            dp_ref[rows, 2 * D:3 * D] = dzb.astype(BF16)
            _to_time_major(e2, r0, dcb)
            dlg, dlb, dbb = dlg + _fold8(dln * xhat), dlb + _fold8(dln), dbb + _fold8(dcb)
        _, _, _, dcbn = conv_out_grad(_dot_nt(dybn[...], wob), zbn[...], cbn[...])
        _to_time_major(e2, tm, jnp.where(i < n_tiles - 1, dcbn, 0.0))
        dlg_ref[...] += dlg
        dlb_ref[...] += dlb
        dbb_ref[...] += dbb
        dw_ref[0] += _dot_tn(sb_ref[...], dyb16)
        _to_time_major(e1, 0, jnp.where(i > 0, vbh[...] * jax.nn.sigmoid(gbh[...]), 0.0))
        for r0 in range(0, tm, ELEM_ROWS):
            rows = pl.ds(r0, ELEM_ROWS)
            glu = vb[rows, :].astype(F32) * jax.nn.sigmoid(gb[rows, :].astype(F32))
            _to_time_major(e1, HALO + r0, glu)
        _conv_wgrad(dwb_ref, e2, e1, FWD_TAPS_B, tm)
        _conv(o_scr, e2, cw_ref, BWD_TAPS_B, tm)
        for r0 in range(0, tm, ELEM_ROWS):
            rows = pl.ds(r0, ELEM_ROWS)
            dglu = _from_time_major(o_scr, ELEM_ROWS, r0)
            sg = jax.nn.sigmoid(gb[rows, :].astype(F32))
            dp_ref[rows, 0 * D:1 * D] = (dglu * sg).astype(BF16)
            dp_ref[rows, 1 * D:2 * D] = (dglu * vb[rows, :].astype(F32) * (sg * (1.0 - sg))).astype(BF16)

    vec = jax.ShapeDtypeStruct((SUBLANES, D), F32)
    return pl.pallas_call(
        body, name="branch_b_bwd", grid=(n_tiles,),
        in_specs=[_rows(tm), _rows(tm, G_ZB), _rows(tm), _rows(tm, G_VB), _rows(tm, G_GB), _rows(tm),
                  _next_halo(tm, n_rows), _next_halo(tm, n_rows, G_ZB), _next_halo(tm, n_rows),
                  _prev_halo(tm, G_VB), _prev_halo(tm, G_GB), _w_out_spec(1), _const(cw_b.shape), _const((1, D)),
                  _const((1, D)), ANY, ANY],
        out_specs=[pl.BlockSpec((tm, 3 * D), lambda i: (i, 2)), pl.BlockSpec((1, D, D), lambda i: (1, 0, 0)),
                   _const((K_B * LANE_GROUPS, LANES)), _const((SUBLANES, D)), _const((SUBLANES, D)),
                   _const((SUBLANES, D))],
        out_shape=[jax.ShapeDtypeStruct(dproj.shape, BF16), jax.ShapeDtypeStruct(dw4.shape, F32),
                   jax.ShapeDtypeStruct((K_B * LANE_GROUPS, LANES), F32), vec, vec, vec],
        input_output_aliases={15: 0, 16: 1},
        scratch_shapes=[_time_major(tm + HALO), _time_major(tm + HALO), _time_major(tm), pltpu.VMEM((tm, D), F32)],
        compiler_params=_params(1),
    )(dyb, proj, cb, proj, proj, sb16, dyb, proj, cb, proj, proj, wo4_g, cw_b, ln_g, ln_b, dproj, dw4)


def _branch_x_bwd(dyx, proj, sx16, kv16, wo4_g, dproj, dw4, tm):
    n_rows = proj.shape[0]
    scale = HEAD_DIM ** -0.5

    def body(dyx_ref, q, zx, sx_ref, kv_ref, w_ref, dp_in, dw_in, dp_ref, dw_ref, dkv_ref):
        del dp_in, dw_in
        i = pl.program_id(0)
        dyx16 = dyx_ref[...]
        q16 = q[...].astype(BF16)
        probs, outs = _attention(q16, kv_ref)
        dsx = _dot_nt(dyx16, w_ref[...].reshape(D, D))
        silu_z, dsilu_z = _silu_and_grad(_f32(zx))
        dp_ref[:, D:2 * D] = (dsx * jnp.concatenate(outs, axis=-1) * dsilu_z).astype(BF16)
        do16 = (dsx * silu_z).astype(BF16)

        @pl.when(i == 0)
        def _():
            dw_ref[...] = jnp.zeros_like(dw_ref)
            dkv_ref[...] = jnp.zeros_like(dkv_ref)

        for h in range(N_HEADS):
            cols = slice(h * HEAD_DIM, (h + 1) * HEAD_DIM)
            p = probs[h]
            dprob = _dot_nt(do16[:, cols], kv_ref[N_HEADS + h])
            ds16 = ((p * (dprob - jnp.sum(p * dprob, axis=-1, keepdims=True))) * scale).astype(BF16)
            dp_ref[:, cols] = _dot(ds16, kv_ref[h]).astype(BF16)
            dkv_ref[h] += _dot_tn(ds16, q16[:, cols])
            dkv_ref[N_HEADS + h] += _dot_tn(p.astype(BF16), do16[:, cols])
        dw_ref[0] += _dot_tn(sx_ref[...], dyx16)

    return pl.pallas_call(
        body, name="branch_x_bwd", grid=(n_rows // tm,),
        in_specs=[_rows(tm), _rows(tm, G_Q), _rows(tm, G_ZX), _rows(tm), _const(kv16.shape), _w_out_spec(2), ANY, ANY],
        out_specs=[pl.BlockSpec((tm, 2 * D), lambda i: (i, 2)), pl.BlockSpec((1, D, D), lambda i: (2, 0, 0)),
                   _const(kv16.shape)],
        out_shape=[jax.ShapeDtypeStruct(dproj.shape, BF16), jax.ShapeDtypeStruct(dw4.shape, F32),
                   jax.ShapeDtypeStruct(kv16.shape, F32)],
        input_output_aliases={6: 0, 7: 1},
        compiler_params=_params(1),
    )(dyx, proj, proj, sx16, kv16, wo4_g, dproj, dw4)


def _dp_unit(u):
    g = u // 2
    pos = jnp.where(g < G_VB, g, jnp.where(g < G_Q, g + 2, jnp.where(g < G_GA, g - 3, g)))
    return 2 * pos + u % 2


def _scatter_copies(srcs, lands, send, recv):
    x, y, c = lax.axis_index("x"), lax.axis_index("y"), lax.axis_index("c")
    copies = []
    for n in range(N_DEV - 1):
        flip = n + 1
        px = 1 - x if flip & 4 else x
        py = 1 - y if flip & 2 else y
        pc = 1 - c if flip & 1 else c
        for t, (src, land) in enumerate(zip(srcs, lands)):
            copies.append(pltpu.make_async_remote_copy(
                src_ref=src.at[4 * px + 2 * py + pc], dst_ref=land.at[n], send_sem=send.at[t * (N_DEV - 1) + n],
                recv_sem=recv.at[t * (N_DEV - 1) + n], device_id=(px, py, pc), device_id_type=MESH))
    return copies


def _scatter_start(name, arrays, views, n_views):
    n = len(arrays)
    lands = [lax.empty(tuple(N_DEV - 1 if d == N_DEV else d for d in a.shape), a.dtype) for a in arrays]

    def body(*refs):
        src, land, (send, recv) = refs[:n], refs[n:2 * n], refs[2 * n:2 * n + 2]
        token = refs[-1]
        for cp in _scatter_copies(views(src), views(land), send, recv):
            cp.start()
        token[...] = jnp.zeros_like(token)

    sems = pltpu.SemaphoreType.DMA((n_views * (N_DEV - 1),))
    out = pl.pallas_call(
        body, name=name,
        in_specs=[HBM] * (2 * n),
        out_specs=[SEM, SEM] + [HBM] * (2 * n) + [pl.BlockSpec(memory_space=pltpu.VMEM)],
        out_shape=[sems, sems] + [pltpu.HBM(a.shape, a.dtype) for a in arrays + lands]
        + [jax.ShapeDtypeStruct((SUBLANES, LANES), F32)],
        input_output_aliases={k: 2 + k for k in range(2 * n)},
        compiler_params=pltpu.CompilerParams(has_side_effects=SIDE_EFFECT),
    )(*[pltpu.with_memory_space_constraint(a, pltpu.HBM) for a in arrays + lands])
    return dict(name=name, sems=out[:2], moving=out[2:2 + 2 * n], views=views, token=out[-1])


def _scatter_wait(started, after):
    n = len(started["moving"]) // 2
    views = started["views"]

    def body(*refs):
        src, land, (send, recv) = refs[:n], refs[n:2 * n], refs[2 * n:2 * n + 2]
        for cp in _scatter_copies(views(src), views(land), send, recv):
            cp.wait_send()
            cp.wait_recv()

    out = pl.pallas_call(
        body, name=started["name"].replace("start", "wait"),
        in_specs=[HBM] * (2 * n) + [SEM, SEM, ANY],
        out_specs=[HBM] * (2 * n),
        out_shape=[pltpu.HBM(a.shape, a.dtype) for a in started["moving"]],
        input_output_aliases={k: k for k in range(2 * n)},
        compiler_params=pltpu.CompilerParams(has_side_effects=SIDE_EFFECT),
    )(*started["moving"], *started["sems"], after)
    return out[n:]


def _w_in_grad(ut, dproj, tk, token):
    n_rows = dproj.shape[0]
    n_k = n_rows // tk
    per_shard = W_IN_SHARD // UNIT

    def body(ut_ref, dp0, dp1, dp2, token_ref, out_ref, out16_ref, acc):
        del token_ref
        t = pl.program_id(1)

        for r, dp_ref in enumerate((dp0, dp1, dp2)):
            cols = slice(r * UNIT, (r + 1) * UNIT)

            @pl.when(t == 0)
            def _(dp_ref=dp_ref, cols=cols):
                acc[:, cols] = _dot(ut_ref[...], dp_ref[...])

            @pl.when(t > 0)
            def _(dp_ref=dp_ref, cols=cols):
                acc[:, cols] += _dot(ut_ref[...], dp_ref[...])

        @pl.when(t == n_k - 1)
        def _():
            out_ref[0] = acc[...]
            out16_ref[0] = acc[...].astype(BF16)

    def dp_spec(r):
        return pl.BlockSpec((tk, UNIT), lambda q, t: (t, _dp_unit(per_shard * q + r)))

    shard = pl.BlockSpec((1, D, W_IN_SHARD), lambda q, t: (q, 0, 0))
    return pl.pallas_call(
        body, name="w_in_grad", grid=(N_DEV, n_k),
        in_specs=[pl.BlockSpec((D, tk), lambda q, t: (0, t)), dp_spec(0), dp_spec(1), dp_spec(2), ANY],
        out_specs=[shard, shard],
        out_shape=[jax.ShapeDtypeStruct((N_DEV, D, W_IN_SHARD), F32), jax.ShapeDtypeStruct((N_DEV, D, W_IN_SHARD), BF16)],
        scratch_shapes=[pltpu.VMEM((D, W_IN_SHARD), F32)],
        compiler_params=_params(2),
    )(ut, dproj, dproj, dproj, token)


def _x_grad(dproj, win_t, x, dh, norm_g, token, tm):
    n_rows = x.shape[0]
    n_k = N_GROUPS * D // X_GRAD_K

    def body(dp_ref, wt_ref, x_ref, dh_ref, g_ref, token_ref, gx_ref, dg_ref, acc):
        del token_ref
        i, g = pl.program_id(0), pl.program_id(1)

        @pl.when((i == 0) & (g == 0))
        def _():
            dg_ref[...] = jnp.zeros_like(dg_ref)

        @pl.when(g == 0)
        def _():
            acc[...] = _dot(dp_ref[...], wt_ref[...])

        @pl.when(g > 0)
        def _():
            acc[...] += _dot(dp_ref[...], wt_ref[...])

        @pl.when(g == n_k - 1)
        def _():
            du = acc[...]
            xf = x_ref[...]
            r = lax.rsqrt(_mean(xf * xf) + EPS)
            xn = xf * r
            dun = du * g_ref[...]
            gx_ref[...] = dh_ref[...] + r * (dun - xn * _mean(dun * xn))
            dg_ref[...] += _fold8(du * xn)

    return pl.pallas_call(
        body, name="x_grad", grid=(n_rows // tm, n_k),
        in_specs=[pl.BlockSpec((tm, X_GRAD_K), lambda i, g: (i, g)), pl.BlockSpec((X_GRAD_K, D), lambda i, g: (g, 0)),
                  pl.BlockSpec((tm, D), lambda i, g: (i, 0)), pl.BlockSpec((tm, D), lambda i, g: (i, 0)),
                  _const((1, D)), ANY],
        out_specs=[pl.BlockSpec((tm, D), lambda i, g: (i, 0)), _const((SUBLANES, D))],
        out_shape=[jax.ShapeDtypeStruct((n_rows, D), F32), jax.ShapeDtypeStruct((SUBLANES, D), F32)],
        scratch_shapes=[pltpu.VMEM((tm, D), F32)],
        compiler_params=_params(2),
    )(dproj, win_t, x, dh, norm_g, token)


def _local_step(x, mem, target, norm_g, conv_b_b, ln_g, ln_b, mem_g, final_g, shards):
    n_rows = x.shape[0]
    tm = min(512, n_rows)
    big = min(1024, n_rows)
    u16, ut = _rmsnorm_fwd(x, norm_g, big)
    proj, win_t, _, wkv_g, wo4_g, cw_g = _proj_fwd_gather(u16, shards, min(2048, n_rows))
    cw_rows = cw_g.transpose(1, 0, 2).reshape((SUBLANES + HALO) * LANE_GROUPS, LANES)
    cw_a, cw_b = cw_rows[:SUBLANES * LANE_GROUPS], cw_rows[SUBLANES * LANE_GROUPS:]
    kv16, mn16 = _kv_fwd(mem, mem_g, wkv_g)
    sa16, ya = _branch_a_fwd(proj, wo4_g, cw_a, tm)
    cb, sb16, yb = _branch_b_fwd(proj, wo4_g, cw_b, conv_b_b, ln_g, ln_b, tm)
    sx16, yx = _branch_x_fwd(proj, kv16, wo4_g, tm)
    dh, dya, dyb, dyx, dproj, dw4, dfg, sq = _merge_fwd_bwd(proj, ya, yb, yx, x, target, wo4_g, final_g,
                                                             min(256, n_rows))
    dproj, dw4, dwa = _branch_a_bwd(dya, proj, sa16, wo4_g, cw_a, dproj, dw4, tm)
    dproj, dw4, dwb, dbb, dlg, dlb = _branch_b_bwd(dyb, proj, cb, sb16, wo4_g, cw_b, ln_g, ln_b, dproj, dw4, tm)
    dproj, dw4, dkv = _branch_x_bwd(dyx, proj, sx16, kv16, wo4_g, dproj, dw4, tm)
    dwkv_g, dmg = _kv_bwd(dkv, mem, mem_g, mn16, wkv_g)
    dw4 = dw4.reshape(4, N_DEV, D // N_DEV, D)
    small_moving = _scatter_start("small_grads_start", [dw4.astype(BF16), dwkv_g.astype(BF16)],
                                  lambda refs: [refs[0].at[w] for w in range(4)] + [refs[1]], 5)
    dwin_g, dwin16 = _w_in_grad(ut, dproj, min(2048, n_rows), small_moving["token"])
    w_in_moving = _scatter_start("w_in_grad_start", [dwin16], lambda refs: list(refs), 1)
    gx, dng = _x_grad(dproj, win_t, x, dh, norm_g, w_in_moving["token"], big)
    land4, landkv = _scatter_wait(small_moving, dng)
    landin, = _scatter_wait(w_in_moving, dng)
    small = {SV_NORM_G: dng, SV_CONV_B_B: dbb, SV_LN_G: dlg, SV_LN_B: dlb, SV_MEM_G: dmg, SV_FINAL_G: dfg, SV_LOSS: sq}
    grads = [(dwin_g[None], landin[None]), (dw4, land4), (dwkv_g[None], landkv[None])]
    return gx, grads, small, dwa.reshape(K_A, D), dwb.reshape(K_B, D)


def _allgather_small(small, conv_rows):
    keys = sorted(small)

    def body(*refs):
        parts, (conv_ref, out_ref, mine, send, recv) = refs[:len(keys)], refs[len(keys):]
        x, y, c, chips = _place()
        me, sibling = 4 * x + 2 * y + c, (x, y, 1 - c)
        mine[pl.ds(0, SV_CONV_A), :] = jnp.zeros((SV_CONV_A, D), F32)
        for key, part in zip(keys, parts):
            mine[key:key + 1, :] = jnp.sum(part[...], axis=0, keepdims=True)
        mine[pl.ds(SV_CONV_A, SV_ROWS - SV_CONV_A), :] = conv_ref[...]
        out_ref[me] = mine[...]

        def copy(k, block, to, from_mine=False):
            return pltpu.make_async_remote_copy(
                src_ref=mine if from_mine else out_ref.at[block], dst_ref=out_ref.at[block],
                send_sem=send.at[k], recv_sem=recv.at[k], device_id=to, device_id_type=MESH)

        first = [copy(0, me, sibling, from_mine=True)]
        first += [copy(1 + j, me, (*chip, c), from_mine=True) for j, chip in enumerate(chips)]
        for cp in first:
            cp.start()
        passed = []
        for j, (px, py) in enumerate(chips):
            block = 4 * px + 2 * py + c
            copy(1 + j, block, sibling).wait_recv()
            passed.append(copy(4 + j, block, sibling))
            passed[-1].start()
        copy(0, 4 * x + 2 * y + 1 - c, sibling).wait_recv()
        for j, (px, py) in enumerate(chips):
            copy(4 + j, 4 * px + 2 * py + 1 - c, sibling).wait_recv()
        for cp in first + passed:
            cp.wait_send()

    vmem = pl.BlockSpec(memory_space=pltpu.VMEM)
    return pl.pallas_call(
        body, name="allgather_small",
        in_specs=[vmem] * (len(keys) + 1), out_specs=vmem,
        out_shape=jax.ShapeDtypeStruct((N_DEV, SV_ROWS, D), F32),
        scratch_shapes=[pltpu.VMEM((SV_ROWS, D), F32), pltpu.SemaphoreType.DMA((7,)), pltpu.SemaphoreType.DMA((7,))],
    )(*[small[k] for k in keys], conv_rows)


def _adamw(w, g, m, v):
    m = ADAM_B1 * m + (1.0 - ADAM_B1) * g
    v = ADAM_B2 * v + (1.0 - ADAM_B2) * (g * g)
    m_hat = m / (1.0 - ADAM_B1 ** ADAM_STEP)
    v_hat = v / (1.0 - ADAM_B2 ** ADAM_STEP)
    return -ADAM_LR * (m_hat / (jnp.sqrt(v_hat) + ADAM_EPS) + ADAM_WD * w), m, v


def _adamw_shard(own, landed, piece, k_arr, w, m, v, tr):
    n_r, n_c = w.shape
    n_landed = landed.shape[1]

    def body(k_ref, own_ref, *refs):
        del k_ref
        landed_refs, (w_ref, m_ref, v_ref, g_out, d_out, m_out, v_out) = refs[:n_landed], refs[n_landed:]
        g = own_ref[0, 0]
        for landed_ref in landed_refs:
            g = g + landed_ref[0, 0].astype(F32)
        g_out[...] = g
        d_out[...], m_out[...], v_out[...] = _adamw(w_ref[...], g, m_ref[...], v_ref[...])

    blk = (1, 1, tr, n_c)
    flat = pl.BlockSpec((tr, n_c), lambda r, k: (r, 0))
    return pl.pallas_call(
        body, name="adamw_shard",
        grid_spec=pltpu.PrefetchScalarGridSpec(
            num_scalar_prefetch=1, grid=(n_r // tr,),
            in_specs=[pl.BlockSpec(blk, lambda r, k: (piece, k[0], r, 0))]
            + [pl.BlockSpec(blk, functools.partial(lambda r, k, j: (piece, j, r, 0), j=j)) for j in range(n_landed)]
            + [flat] * 3,
            out_specs=[flat] * 4),
        out_shape=[jax.ShapeDtypeStruct((n_r, n_c), F32)] * 4,
        compiler_params=_params(1),
    )(k_arr, own, *([landed] * n_landed), w, m, v)


def _adamw_small(gathered, k_arr, vectors, conv_a, conv_b):
    n_vec = len(vectors)
    cols = D // N_DEV

    def body(k_ref, full_ref, cols_ref, *refs):
        del k_ref
        ins, outs = refs[:3 * (n_vec + 2)], refs[3 * (n_vec + 2):-2]
        tot, tot_cols = refs[-2:]
        tot[...] = full_ref[0]
        tot_cols[...] = cols_ref[0]
        for dev in range(1, N_DEV):
            tot[...] += full_ref[dev]
            tot_cols[...] += cols_ref[dev]
        loss = (0.5 / D) * jnp.sum(tot[SV_LOSS:SV_LOSS + 1, :])
        outs[0][...] = jnp.full(outs[0].shape, loss, F32)
        grads = [tot[n:n + 1, :] for n in range(n_vec)]
        grads += [tot_cols[pl.ds(SV_CONV_A, K_A), :], tot_cols[pl.ds(SV_CONV_B, K_B), :]]
        for n, g in enumerate(grads):
            w_ref, m_ref, v_ref = ins[3 * n:3 * n + 3]
            g_out, d_out, m_out, v_out = outs[1 + 4 * n:5 + 4 * n]
            g_out[...] = g
            d_out[...], m_out[...], v_out[...] = _adamw(w_ref[...], g, m_ref[...], v_ref[...])

    weights = list(vectors) + [conv_a, conv_b]
    flat_in = [a for wmv in weights for a in wmv]
    out_shape = [jax.ShapeDtypeStruct((SUBLANES, 128), F32)]
    for wmv in weights:
        out_shape += [jax.ShapeDtypeStruct(wmv[0].shape, F32)] * 4
    return pl.pallas_call(
        body, name="adamw_small",
        grid_spec=pltpu.PrefetchScalarGridSpec(
            num_scalar_prefetch=1, grid=(1,),
            in_specs=[pl.BlockSpec((N_DEV, SV_ROWS, D), lambda i, k: (0, 0, 0)),
                      pl.BlockSpec((N_DEV, SV_ROWS, cols), lambda i, k: (0, 0, k[0]))]
            + [pl.BlockSpec(a.shape, lambda i, k: (0, 0)) for a in flat_in],
            out_specs=[pl.BlockSpec(s.shape, lambda i, k: (0, 0)) for s in out_shape],
            scratch_shapes=[pltpu.VMEM((SV_ROWS, D), F32), pltpu.VMEM((SV_ROWS, cols), F32)]),
        out_shape=out_shape,
        compiler_params=_params(1),
    )(k_arr, gathered, gathered, *flat_in)


def kernel(x, mem, norm_g, w_in, conv_a_w, w_out_a, conv_b_w, conv_b_b, ln_b_g, ln_b_b, w_out_b, mem_norm_g, w_kv, w_out_x, w_o, final_g, loss_target, m_norm_g, m_w_in, m_conv_a_w, m_w_out_a, m_conv_b_w, m_conv_b_b, m_ln_b_g, m_ln_b_b, m_w_out_b, m_mem_norm_g, m_w_kv, m_w_out_x, m_w_o, m_final_g, v_norm_g, v_w_in, v_conv_a_w, v_w_out_a, v_conv_b_w, v_conv_b_b, v_ln_b_g, v_ln_b_b, v_w_out_b, v_mem_norm_g, v_w_kv, v_w_out_x, v_w_o, v_final_g):
    xi, yi, ci = lax.axis_index("x"), lax.axis_index("y"), lax.axis_index("c")
    k_arr = jnp.reshape(4 * xi + 2 * yi + ci, (1,)).astype(jnp.int32)

    cw = jnp.concatenate([jnp.pad(conv_a_w[0], ((0, SUBLANES - K_A), (0, 0))),
                          jnp.pad(conv_b_w[0], ((0, HALO - K_B), (0, 0)))], axis=0)
    wo4 = jnp.stack([w_out_a[0], w_out_b[0], w_out_x[0], w_o[0]]).astype(BF16)
    shards = [w_in[0].astype(BF16), w_kv[0].astype(BF16), wo4, cw]

    final_g2 = final_g.reshape(1, D)
    gx, grads, small, dwa, dwb = _local_step(
        x[0], mem[0], loss_target[0], norm_g, conv_b_b, ln_b_g, ln_b_b, mem_norm_g, final_g2, shards)

    conv_rows = jnp.concatenate([jnp.pad(dwa, ((0, SUBLANES - K_A), (0, 0))),
                                 jnp.pad(dwb, ((0, HALO - K_B), (0, 0)))], axis=0)
    gathered_small = _allgather_small(small, conv_rows)

    tiles = [256, D // N_DEV, 256]

    def shard(a, l, w, m, v):
        return _adamw_shard(grads[a][0], grads[a][1], l, k_arr, w[0], m[0], v[0], tiles[a])

    res = {
        "w_in": shard(0, 0, w_in, m_w_in, v_w_in),
        "w_out_a": shard(1, 0, w_out_a, m_w_out_a, v_w_out_a),
        "w_out_b": shard(1, 1, w_out_b, m_w_out_b, v_w_out_b),
        "w_out_x": shard(1, 2, w_out_x, m_w_out_x, v_w_out_x),
        "w_o": shard(1, 3, w_o, m_w_o, v_w_o),
        "w_kv": shard(2, 0, w_kv, m_w_kv, v_w_kv),
    }
    res = {name: tuple(r[None] for r in four) for name, four in res.items()}
    vectors = [(norm_g, m_norm_g, v_norm_g), (conv_b_b, m_conv_b_b, v_conv_b_b), (ln_b_g, m_ln_b_g, v_ln_b_g),
               (ln_b_b, m_ln_b_b, v_ln_b_b), (mem_norm_g, m_mem_norm_g, v_mem_norm_g),
               (final_g2, m_final_g.reshape(1, D), v_final_g.reshape(1, D))]
    out = _adamw_small(gathered_small, k_arr, vectors, (conv_a_w[0], m_conv_a_w[0], v_conv_a_w[0]),
                       (conv_b_w[0], m_conv_b_w[0], v_conv_b_w[0]))
    loss = out[0][0, 0]
    names = ["norm_g", "conv_b_b", "ln_b_g", "ln_b_b", "mem_norm_g", "final_g", "conv_a_w", "conv_b_w"]
    for n, name in enumerate(names):
        four = out[1 + 4 * n:5 + 4 * n]
        if name == "final_g":
            four = [r.reshape(D) for r in four]
        elif name.startswith("conv_") and name.endswith("_w"):
            four = [r[None] for r in four]
        res[name] = tuple(four)

    order = ["norm_g", "w_in", "conv_a_w", "w_out_a", "conv_b_w", "conv_b_b", "ln_b_g", "ln_b_b", "w_out_b",
             "mem_norm_g", "w_kv", "w_out_x", "w_o", "final_g"]
    return (loss, gx[None], *[res[n][0] for n in order], *[res[n][1] for n in order],
            *[res[n][2] for n in order], *[res[n][3] for n in order])
```

```python
import functools

import jax
import jax.numpy as jnp
from jax import lax
from jax.experimental import pallas as pl
from jax.experimental.pallas import tpu as pltpu

F32, BF16 = jnp.float32, jnp.bfloat16
D = 1024
N_DEV = 8
N_HEADS = 4
HEAD_DIM = D // N_HEADS
N_GROUPS = 12
W_IN_SHARD = N_GROUPS * D // N_DEV
UNIT = 512
X_GRAD_K = 2 * D
K_A, K_B = 3, 31
EPS = 1e-6
HALO = 32
SUBLANES = 8
LANES = 128
LANE_GROUPS = D // LANES
TAP_GROUP = 16
CONV_BLOCK = 32
ELEM_ROWS = 16
CONV_PARTIAL_SUMS = 4
VMEM_LIMIT = 56 << 20
MESH = pl.DeviceIdType.MESH
ANY = pl.BlockSpec(memory_space=pl.ANY)
HBM = pl.BlockSpec(memory_space=pltpu.HBM)
SEM = pl.BlockSpec(memory_space=pltpu.SEMAPHORE)
SIDE_EFFECT = pltpu.SideEffectType.DATAFLOW_SIDE_EFFECTING

G_BA, G_CA, G_XA, G_ZA, G_VB, G_GB, G_ZB, G_Q, G_ZX, G_GA, G_GBB, G_GX = range(N_GROUPS)

ADAM_LR, ADAM_B1, ADAM_B2, ADAM_EPS, ADAM_WD, ADAM_STEP = 0.001, 0.9, 0.999, 1e-08, 0.01, 10

SV_NORM_G, SV_CONV_B_B, SV_LN_G, SV_LN_B, SV_MEM_G, SV_FINAL_G, SV_LOSS = range(7)
SV_CONV_A, SV_CONV_B, SV_ROWS = 8, 16, 48


def _dot(a, b):
    return jnp.dot(a, b, preferred_element_type=F32)


def _dot_nt(a, b):
    return lax.dot_general(a, b, (((1,), (1,)), ((), ())), preferred_element_type=F32)


def _dot_tn(a, b):
    return lax.dot_general(a, b, (((0,), (0,)), ((), ())), preferred_element_type=F32)


def _silu_and_grad(z):
    s = jax.nn.sigmoid(z)
    return z * s, s * (1.0 + z * (1.0 - s))


def _fold8(a):
    return a.reshape(a.shape[0] // SUBLANES, SUBLANES, a.shape[1]).sum(axis=0)


def _mean(a):
    return jnp.mean(a, axis=-1, keepdims=True)


def _f32(ref):
    return ref[...].astype(F32)


def _params(n_grid):
    return pltpu.CompilerParams(dimension_semantics=("arbitrary",) * n_grid, vmem_limit_bytes=VMEM_LIMIT)


def _rows(tm, col=0):
    return pl.BlockSpec((tm, D), lambda i: (i, col))


def _prev_halo(tm, col=0):
    return pl.BlockSpec((HALO, D), lambda i: (jnp.maximum(i * (tm // HALO) - 1, 0), col))


def _next_halo(tm, n_rows, col=0):
    last = n_rows // HALO - 1
    return pl.BlockSpec((HALO, D), lambda i: (jnp.minimum((i + 1) * (tm // HALO), last), col))


def _const(shape):
    return pl.BlockSpec(shape, lambda *_: (0,) * len(shape))


def _w_out_spec(which):
    return pl.BlockSpec((N_DEV, None, D // N_DEV, D), lambda *_: (0, which, 0, 0))


def _to_time_major(t_ref, row0, x):
    n = x.shape[0]
    for j in range(LANE_GROUPS):
        t_ref[pl.ds(row0 * LANE_GROUPS + j, n, stride=LANE_GROUPS), :] = x[:, j * LANES:(j + 1) * LANES]


def _from_time_major(t_ref, n, row0=0):
    return jnp.concatenate([t_ref[pl.ds(row0 * LANE_GROUPS + j, n, stride=LANE_GROUPS), :]
                            for j in range(LANE_GROUPS)], axis=-1)


def _row(ref, t):
    start = t * LANE_GROUPS
    if not isinstance(start, int):
        start = pl.multiple_of(start, LANE_GROUPS)
    return ref[pl.ds(start, LANE_GROUPS), :]


def _tap_groups(taps):
    return [taps[first:first + TAP_GROUP] for first in range(0, len(taps), TAP_GROUP)]


def _conv(o_ref, e_ref, w_ref, taps, n_rows, bias_ref=None):
    for n_group, group in enumerate(_tap_groups(taps)):
        weights = [_row(w_ref, k) for k, _ in group]

        def block(c, carry, n_group=n_group, group=group, weights=weights):
            t0 = c * CONV_BLOCK
            window = {}
            for t in range(CONV_BLOCK):
                parts = [None] * min(CONV_PARTIAL_SUMS, len(group))
                for n, (_, off) in enumerate(group):
                    if t + off not in window:
                        window[t + off] = _row(e_ref, t0 + t + off)
                    term = weights[n] * window[t + off]
                    parts[n % len(parts)] = term if parts[n % len(parts)] is None else parts[n % len(parts)] + term
                window.pop(t + min(off for _, off in group), None)
                while len(parts) > 1:
                    parts = [parts[n] + parts[n + 1] for n in range(0, len(parts) - 1, 2)] + parts[len(parts) & ~1:]
                out = parts[0]
                if n_group > 0:
                    out = out + _row(o_ref, t0 + t)
                elif bias_ref is not None:
                    out = out + bias_ref[...]
                o_ref[pl.ds(pl.multiple_of((t0 + t) * LANE_GROUPS, LANE_GROUPS), LANE_GROUPS), :] = out
            return carry

        lax.fori_loop(0, n_rows // CONV_BLOCK, block, 0)


def _conv_wgrad(dw_ref, d_ref, e_ref, taps, n_rows):
    for group in _tap_groups(taps):
        def block(c, sums, group=group):
            t0 = c * CONV_BLOCK
            sums = list(sums)
            window = {}
            for t in range(CONV_BLOCK):
                d = _row(d_ref, t0 + t)
                for n, (_, off) in enumerate(group):
                    if t + off not in window:
                        window[t + off] = _row(e_ref, t0 + t + off)
                    sums[n] = sums[n] + d * window[t + off]
                window.pop(t + min(off for _, off in group), None)
            return tuple(sums)

        sums = lax.fori_loop(0, n_rows // CONV_BLOCK, block, tuple(_row(dw_ref, k) for k, _ in group))
        for (k, _), total in zip(group, sums):
            dw_ref[pl.ds(k * LANE_GROUPS, LANE_GROUPS), :] = total


FWD_TAPS_A = [(k, HALO - (K_A - 1) + k) for k in range(K_A)]
BWD_TAPS_A = [(k, K_A - 1 - k) for k in range(K_A)]
FWD_TAPS_B = [(k, HALO - (K_B - 1) + k) for k in range(K_B)]
BWD_TAPS_B = [(k, K_B - 1 - k) for k in range(K_B)]


def _time_major(n_rows):
    return pltpu.VMEM((n_rows * LANE_GROUPS, LANES), F32)


def _kv_fwd(mem, mem_g, wkv_g):
    m_len = mem.shape[0]

    def body(mem_ref, g_ref, w_ref, kv_ref, mn_ref):
        mf = mem_ref[...]
        r = lax.rsqrt(_mean(mf * mf) + EPS)
        mn = ((mf * r) * g_ref[...]).astype(BF16)
        mn_ref[...] = mn
        for b in range(2 * N_HEADS):
            kv_ref[b] = _dot(mn, w_ref[b]).astype(BF16)

    return pl.pallas_call(
        body, name="kv_fwd", grid=(1,),
        in_specs=[_const((m_len, D)), _const((1, D)), _const((2 * N_HEADS, D, HEAD_DIM))],
        out_specs=[_const((2 * N_HEADS, m_len, HEAD_DIM)), _const((m_len, D))],
        out_shape=[jax.ShapeDtypeStruct((2 * N_HEADS, m_len, HEAD_DIM), BF16), jax.ShapeDtypeStruct((m_len, D), BF16)],
        compiler_params=_params(1),
    )(mem, mem_g, wkv_g)


def _kv_bwd(dkv, mem, mem_g, mn16, wkv_g):
    m_len = mem.shape[0]

    def body(dkv_ref, mem_ref, g_ref, mn_ref, w_ref, dw_ref, dw16_ref, dg_ref):
        mn = mn_ref[...]
        dmn = jnp.zeros((m_len, D), F32)
        for b in range(2 * N_HEADS):
            d16 = dkv_ref[b].astype(BF16)
            dw = _dot_tn(mn, d16)
            dw_ref[b] = dw
            dw16_ref[b] = dw.astype(BF16)
            dmn = dmn + _dot_nt(d16, w_ref[b])
        mf = mem_ref[...]
        r = lax.rsqrt(_mean(mf * mf) + EPS)
        dg_ref[...] = _fold8(dmn * (mf * r))

    return pl.pallas_call(
        body, name="kv_bwd", grid=(1,),
        in_specs=[_const((2 * N_HEADS, m_len, HEAD_DIM)), _const((m_len, D)), _const((1, D)), _const((m_len, D)),
                  _const((2 * N_HEADS, D, HEAD_DIM))],
        out_specs=[_const((2 * N_HEADS, D, HEAD_DIM)), _const((2 * N_HEADS, D, HEAD_DIM)), _const((SUBLANES, D))],
        out_shape=[jax.ShapeDtypeStruct((2 * N_HEADS, D, HEAD_DIM), F32),
                   jax.ShapeDtypeStruct((2 * N_HEADS, D, HEAD_DIM), BF16), jax.ShapeDtypeStruct((SUBLANES, D), F32)],
        compiler_params=_params(1),
    )(dkv, mem, mem_g, mn16, wkv_g)


def _rmsnorm_fwd(x, norm_g, tm):
    n_rows = x.shape[0]

    def body(x_ref, g_ref, u_ref, ut_ref):
        xf = x_ref[...]
        u = (xf * lax.rsqrt(_mean(xf * xf) + EPS)) * g_ref[...]
        u_ref[...] = u.astype(BF16)
        ut_ref[...] = u.T.astype(BF16)

    return pl.pallas_call(
        body, name="rmsnorm_fwd", grid=(n_rows // tm,),
        in_specs=[_rows(tm), _const((1, D))],
        out_specs=[_rows(tm), pl.BlockSpec((D, tm), lambda i: (0, i))],
        out_shape=[jax.ShapeDtypeStruct((n_rows, D), BF16), jax.ShapeDtypeStruct((D, n_rows), BF16)],
        compiler_params=_params(1),
    )(x, norm_g)


def _place():
    x, y, c = lax.axis_index("x"), lax.axis_index("y"), lax.axis_index("c")
    other_chips = [(1 - x, y), (x, 1 - y), (1 - x, 1 - y)]
    return x, y, c, other_chips


def _arrival_order():
    x, y, c, chips = _place()
    order = [4 * x + 2 * y + c, 4 * x + 2 * y + 1 - c]
    for px, py in chips:
        order += [4 * px + 2 * py + c, 4 * px + 2 * py + 1 - c]
    return order


def _proj_fwd_gather(u16, blocks, tm):
    n = len(blocks)
    n_rows = u16.shape[0]
    n_i = n_rows // tm
    per_shard = W_IN_SHARD // UNIT
    assert n_i >= per_shard

    def wt_index(p, i, order):
        return (_dp_unit(per_shard * order[p] + jnp.minimum(i, per_shard - 1)), 0)

    def body(order_ref, u_ref, *refs):
        src, proj_ref, wt_ref, out = refs[:n], refs[n], refs[n + 1], refs[n + 2:2 * n + 2]
        wbuf, stage_sem, send, recv, own_sem = refs[2 * n + 2:]
        p, i = pl.program_id(0), pl.program_id(1)
        x, y, c, chips = _place()
        me, sibling = 4 * x + 2 * y + c, (x, y, 1 - c)

        def copy(t, k, block, to, from_input=False):
            return pltpu.make_async_remote_copy(
                src_ref=src[t] if from_input else out[t].at[block], dst_ref=out[t].at[block],
                send_sem=send.at[t, k], recv_sem=recv.at[t, k], device_id=to, device_id_type=MESH)

        def own_copies():
            return [pltpu.make_async_copy(src[t], out[t].at[me], own_sem.at[t]) for t in range(n)]

        def first_copies():
            first = []
            for t in range(n):
                first.append(copy(t, 0, me, sibling, from_input=True))
                first += [copy(t, 1 + j, me, (*chip, c), from_input=True) for j, chip in enumerate(chips)]
            return first

        def stage(slot, block):
            return pltpu.make_async_copy(out[0].at[block], wbuf.at[slot], stage_sem.at[slot])

        @pl.when((p == 0) & (i == 0))
        def _():
            for cp in own_copies() + first_copies():
                cp.start()
            mine = pltpu.make_async_copy(src[0], wbuf.at[0], stage_sem.at[0])
            mine.start()
            mine.wait()

        @pl.when((p > 0) & (i == 0))
        def _():
            stage(p % 2, order_ref[p]).wait()

        proj_ref[...] = _dot(u_ref[...], wbuf[p % 2])
        for r in range(per_shard):
            @pl.when(i == r)
            def _(r=r):
                wt_ref[...] = wbuf[p % 2, :, r * UNIT:(r + 1) * UNIT].astype(F32).T.astype(BF16)

        for nxt in range(1, N_DEV):
            @pl.when((p == nxt - 1) & (i == n_i - 1))
            def _(nxt=nxt):
                if nxt == 1:
                    block = 4 * x + 2 * y + 1 - c
                    copy(0, 0, block, sibling).wait_recv()
                else:
                    j, passed_on = divmod(nxt - 2, 2)
                    px, py = chips[j]
                    if passed_on:
                        block = 4 * px + 2 * py + 1 - c
                        copy(0, 4 + j, block, sibling).wait_recv()
                    else:
                        block = 4 * px + 2 * py + c
                        copy(0, 1 + j, block, sibling).wait_recv()
                        copy(0, 4 + j, block, sibling).start()
                stage(nxt % 2, block).start()

        @pl.when((p == N_DEV - 1) & (i == n_i - 1))
        def _():
            passed = [copy(0, 4 + j, 4 * px + 2 * py + c, sibling) for j, (px, py) in enumerate(chips)]
            for j, (px, py) in enumerate(chips):
                for t in range(1, n):
                    block = 4 * px + 2 * py + c
                    copy(t, 1 + j, block, sibling).wait_recv()
                    passed.append(copy(t, 4 + j, block, sibling))
                    passed[-1].start()
            for t in range(1, n):
                copy(t, 0, 4 * x + 2 * y + 1 - c, sibling).wait_recv()
                for j, (px, py) in enumerate(chips):
                    copy(t, 4 + j, 4 * px + 2 * py + 1 - c, sibling).wait_recv()
            for cp in first_copies() + passed:
                cp.wait_send()
            for cp in own_copies():
                cp.wait()

    return pl.pallas_call(
        body, name="proj_fwd_gather",
        grid_spec=pltpu.PrefetchScalarGridSpec(
            num_scalar_prefetch=1, grid=(N_DEV, n_i),
            in_specs=[pl.BlockSpec((tm, D), lambda p, i, order: (i, 0))] + [ANY] * n,
            out_specs=[pl.BlockSpec((tm, W_IN_SHARD), lambda p, i, order: (i, order[p])),
                       pl.BlockSpec((UNIT, D), wt_index)] + [ANY] * n,
            scratch_shapes=[pltpu.VMEM((2, D, W_IN_SHARD), BF16), pltpu.SemaphoreType.DMA((2,)),
                            pltpu.SemaphoreType.DMA((n, 7)), pltpu.SemaphoreType.DMA((n, 7)),
                            pltpu.SemaphoreType.DMA((n,))]),
        out_shape=[jax.ShapeDtypeStruct((n_rows, N_GROUPS * D), F32), jax.ShapeDtypeStruct((N_GROUPS * D, D), BF16)]
        + [jax.ShapeDtypeStruct((N_DEV, *b.shape), b.dtype) for b in blocks],
        compiler_params=_params(2),
    )(jnp.stack(_arrival_order()).astype(jnp.int32), u16, *blocks)


def _branch_a_fwd(proj, wo4_g, cw_a, tm):
    n_rows = proj.shape[0]

    def body(bp, cp, xp, za, cph, xph, w_ref, cw_ref, sa_ref, ya_ref, e_scr, o_scr):
        i = pl.program_id(0)
        _to_time_major(e_scr, 0, jnp.where(i > 0, _f32(cph) * _f32(xph), 0.0))
        for r0 in range(0, tm, ELEM_ROWS):
            rows = pl.ds(r0, ELEM_ROWS)
            _to_time_major(e_scr, HALO + r0, cp[rows, :].astype(F32) * xp[rows, :].astype(F32))
        _conv(o_scr, e_scr, cw_ref, FWD_TAPS_A, tm)
        for r0 in range(0, tm, ELEM_ROWS):
            rows = pl.ds(r0, ELEM_ROWS)
            ca = _from_time_major(o_scr, ELEM_ROWS, r0)
            sa_ref[rows, :] = (jax.nn.silu(za[rows, :].astype(F32)) * (bp[rows, :].astype(F32) * ca)).astype(BF16)
        ya_ref[...] = _dot(sa_ref[...], w_ref[...].reshape(D, D))

    return pl.pallas_call(
        body, name="branch_a_fwd", grid=(n_rows // tm,),
        in_specs=[_rows(tm, G_BA), _rows(tm, G_CA), _rows(tm, G_XA), _rows(tm, G_ZA),
                  _prev_halo(tm, G_CA), _prev_halo(tm, G_XA), _w_out_spec(0), _const(cw_a.shape)],
        out_specs=[_rows(tm), _rows(tm)],
        out_shape=[jax.ShapeDtypeStruct((n_rows, D), BF16), jax.ShapeDtypeStruct((n_rows, D), F32)],
        scratch_shapes=[_time_major(tm + HALO), _time_major(tm)],
        compiler_params=_params(1),
    )(proj, proj, proj, proj, proj, proj, wo4_g, cw_a)


def _layernorm_parts(cb, lg, lb):
    xc = cb - _mean(cb)
    rstd = lax.rsqrt(_mean(xc * xc) + EPS)
    xhat = xc * rstd
    return xhat, rstd, xhat * lg + lb


def _branch_b_fwd(proj, wo4_g, cw_b, conv_b_b, ln_g, ln_b, tm):
    n_rows = proj.shape[0]

    def body(vb, gb, zb, vbh, gbh, w_ref, cw_ref, bb_ref, lg_ref, lb_ref, cb_ref, sb_ref, yb_ref, e_scr, o_scr):
        i = pl.program_id(0)
        vbh, gbh = _f32(vbh), _f32(gbh)
        lg, lb = lg_ref[...], lb_ref[...]
        _to_time_major(e_scr, 0, jnp.where(i > 0, vbh * jax.nn.sigmoid(gbh), 0.0))
        for r0 in range(0, tm, ELEM_ROWS):
            rows = pl.ds(r0, ELEM_ROWS)
            _to_time_major(e_scr, HALO + r0, vb[rows, :].astype(F32) * jax.nn.sigmoid(gb[rows, :].astype(F32)))
        _conv(o_scr, e_scr, cw_ref, FWD_TAPS_B, tm, bias_ref=bb_ref)
        for r0 in range(0, tm, ELEM_ROWS):
            rows = pl.ds(r0, ELEM_ROWS)
            cb = _from_time_major(o_scr, ELEM_ROWS, r0)
            cb_ref[rows, :] = cb
            _, _, ln = _layernorm_parts(cb, lg, lb)
            sb_ref[rows, :] = (jax.nn.silu(zb[rows, :].astype(F32)) * jax.nn.silu(ln)).astype(BF16)
        yb_ref[...] = _dot(sb_ref[...], w_ref[...].reshape(D, D))

    return pl.pallas_call(
        body, name="branch_b_fwd", grid=(n_rows // tm,),
        in_specs=[_rows(tm, G_VB), _rows(tm, G_GB), _rows(tm, G_ZB), _prev_halo(tm, G_VB), _prev_halo(tm, G_GB),
                  _w_out_spec(1), _const(cw_b.shape), _const((LANE_GROUPS, LANES)), _const((1, D)), _const((1, D))],
        out_specs=[_rows(tm), _rows(tm), _rows(tm)],
        out_shape=[jax.ShapeDtypeStruct((n_rows, D), F32), jax.ShapeDtypeStruct((n_rows, D), BF16),
                   jax.ShapeDtypeStruct((n_rows, D), F32)],
        scratch_shapes=[_time_major(tm + HALO), _time_major(tm)],
        compiler_params=_params(1),
    )(proj, proj, proj, proj, proj, wo4_g, cw_b, conv_b_b.reshape(LANE_GROUPS, LANES), ln_g, ln_b)


def _attention(q16, kv_ref):
    probs, outs = [], []
    for h in range(N_HEADS):
        s = _dot_nt(q16[:, h * HEAD_DIM:(h + 1) * HEAD_DIM], kv_ref[h]) * (HEAD_DIM ** -0.5)
        e = jnp.exp(s - jnp.max(s, axis=-1, keepdims=True))
        p = e / jnp.sum(e, axis=-1, keepdims=True)
        probs.append(p)
        outs.append(_dot(p.astype(BF16), kv_ref[N_HEADS + h]))
    return probs, outs


def _branch_x_fwd(proj, kv16, wo4_g, tm):
    n_rows = proj.shape[0]

    def body(q, zx, kv_ref, w_ref, sx_ref, yx_ref):
        _, outs = _attention(q[...].astype(BF16), kv_ref)
        sx = (jax.nn.silu(_f32(zx)) * jnp.concatenate(outs, axis=-1)).astype(BF16)
        sx_ref[...] = sx
        yx_ref[...] = _dot(sx, w_ref[...].reshape(D, D))

    return pl.pallas_call(
        body, name="branch_x_fwd", grid=(n_rows // tm,),
        in_specs=[_rows(tm, G_Q), _rows(tm, G_ZX), _const(kv16.shape), _w_out_spec(2)],
        out_specs=[_rows(tm), _rows(tm)],
        out_shape=[jax.ShapeDtypeStruct((n_rows, D), BF16), jax.ShapeDtypeStruct((n_rows, D), F32)],
        compiler_params=_params(1),
    )(proj, proj, kv16, wo4_g)


def _merge_fwd_bwd(proj, ya, yb, yx, x, target, wo4_g, final_g, tm):
    n_rows = proj.shape[0]
    inv_d = 1.0 / D

    def body(ga, gb, gx, ya_ref, yb_ref, yx_ref, x_ref, t_ref, w_ref, fg_ref,
             dh_ref, dya_ref, dyb_ref, dyx_ref, dp_ref, dw_ref, dfg_ref, sq_ref):
        i = pl.program_id(0)
        wo = w_ref[...].reshape(D, D)
        sig = [jax.nn.sigmoid(_f32(g)) for g in (ga, gb, gx)]
        ys = [ya_ref[...], yb_ref[...], yx_ref[...]]
        m16 = (sig[0] * ys[0] + sig[1] * ys[1] + sig[2] * ys[2]).astype(BF16)
        h = x_ref[...] + _dot(m16, wo)
        r = lax.rsqrt(_mean(h * h) + EPS)
        hn = h * r
        fg = fg_ref[...]
        err = hn * fg - t_ref[...]
        dy = err * inv_d
        dhn = dy * fg
        dh = r * (dhn - hn * _mean(dhn * hn))
        dh_ref[...] = dh
        dh16 = dh.astype(BF16)
        dm = _dot_nt(dh16, wo)
        for n, out in enumerate((dya_ref, dyb_ref, dyx_ref)):
            out[...] = (sig[n] * dm).astype(BF16)
            dp_ref[:, n * D:(n + 1) * D] = (dm * ys[n] * (sig[n] * (1.0 - sig[n]))).astype(BF16)

        @pl.when(i == 0)
        def _():
            dw_ref[...] = jnp.zeros_like(dw_ref)
            dfg_ref[...] = jnp.zeros_like(dfg_ref)
            sq_ref[...] = jnp.zeros_like(sq_ref)

        dw_ref[0] += _dot_tn(m16, dh16)
        dfg_ref[...] += _fold8(dy * hn)
        sq_ref[...] += _fold8(err * err)

    vec = jax.ShapeDtypeStruct((SUBLANES, D), F32)
    return pl.pallas_call(
        body, name="merge_fwd_bwd", grid=(n_rows // tm,),
        in_specs=[_rows(tm, G_GA), _rows(tm, G_GBB), _rows(tm, G_GX), _rows(tm), _rows(tm), _rows(tm), _rows(tm),
                  _rows(tm), _w_out_spec(3), _const((1, D))],
        out_specs=[_rows(tm), _rows(tm), _rows(tm), _rows(tm), pl.BlockSpec((tm, 3 * D), lambda i: (i, 3)),
                   pl.BlockSpec((1, D, D), lambda i: (3, 0, 0)), _const((SUBLANES, D)), _const((SUBLANES, D))],
        out_shape=[jax.ShapeDtypeStruct((n_rows, D), F32), jax.ShapeDtypeStruct((n_rows, D), BF16),
                   jax.ShapeDtypeStruct((n_rows, D), BF16), jax.ShapeDtypeStruct((n_rows, D), BF16),
                   jax.ShapeDtypeStruct((n_rows, N_GROUPS * D), BF16), jax.ShapeDtypeStruct((4, D, D), F32), vec, vec],
        compiler_params=_params(1),
    )(proj, proj, proj, ya, yb, yx, x, target, wo4_g, final_g)


def _branch_a_bwd(dya, proj, sa16, wo4_g, cw_a, dproj, dw4, tm):
    n_rows = proj.shape[0]
    n_tiles = n_rows // tm

    def body(dya_ref, bp, cp, xp, za, sa_ref, dyan, bpn, zan, cph, xph, w_ref, cw_ref, dp_in, dw_in,
             dp_ref, dw_ref, dwa_ref, e1, e2, o_scr, mm_scr):
        del dp_in, dw_in
        i = pl.program_id(0)
        bpn, zan, cph, xph = (_f32(r) for r in (bpn, zan, cph, xph))
        woa = w_ref[...].reshape(D, D)
        dya16 = dya_ref[...]
        chunks = [pl.ds(r0, ELEM_ROWS) for r0 in range(0, tm, ELEM_ROWS)]
        _to_time_major(e1, 0, jnp.where(i > 0, cph * xph, 0.0))
        for r0, rows in zip(range(0, tm, ELEM_ROWS), chunks):
            _to_time_major(e1, HALO + r0, cp[rows, :].astype(F32) * xp[rows, :].astype(F32))
        _conv(o_scr, e1, cw_ref, FWD_TAPS_A, tm)
        mm_scr[...] = _dot_nt(dya16, woa)
        for r0, rows in zip(range(0, tm, ELEM_ROWS), chunks):
            ca = _from_time_major(o_scr, ELEM_ROWS, r0)
            dsa = mm_scr[rows, :]
            b = bp[rows, :].astype(F32)
            silu_z, dsilu_z = _silu_and_grad(za[rows, :].astype(F32))
            t = dsa * silu_z
            dp_ref[rows, 0 * D:1 * D] = (t * ca).astype(BF16)
            dp_ref[rows, 3 * D:4 * D] = (dsa * (b * ca) * dsilu_z).astype(BF16)
            _to_time_major(e2, r0, t * b)
        dcan = (_dot_nt(dyan[...], woa) * jax.nn.silu(zan)) * bpn
        _to_time_major(e2, tm, jnp.where(i < n_tiles - 1, dcan, 0.0))

        @pl.when(i == 0)
        def _():
            dw_ref[...] = jnp.zeros_like(dw_ref)
            dwa_ref[...] = jnp.zeros_like(dwa_ref)

        _conv_wgrad(dwa_ref, e2, e1, FWD_TAPS_A, tm)
        dw_ref[0] += _dot_tn(sa_ref[...], dya16)
        _conv(o_scr, e2, cw_ref, BWD_TAPS_A, tm)
        for r0, rows in zip(range(0, tm, ELEM_ROWS), chunks):
            dprod = _from_time_major(o_scr, ELEM_ROWS, r0)
            dp_ref[rows, 1 * D:2 * D] = (dprod * xp[rows, :].astype(F32)).astype(BF16)
            dp_ref[rows, 2 * D:3 * D] = (dprod * cp[rows, :].astype(F32)).astype(BF16)

    return pl.pallas_call(
        body, name="branch_a_bwd", grid=(n_tiles,),
        in_specs=[_rows(tm), _rows(tm, G_BA), _rows(tm, G_CA), _rows(tm, G_XA), _rows(tm, G_ZA), _rows(tm),
                  _next_halo(tm, n_rows), _next_halo(tm, n_rows, G_BA), _next_halo(tm, n_rows, G_ZA),
                  _prev_halo(tm, G_CA), _prev_halo(tm, G_XA), _w_out_spec(0), _const(cw_a.shape), ANY, ANY],
        out_specs=[pl.BlockSpec((tm, 4 * D), lambda i: (i, 0)), pl.BlockSpec((1, D, D), lambda i: (0, 0, 0)),
                   _const((K_A * LANE_GROUPS, LANES))],
        out_shape=[jax.ShapeDtypeStruct(dproj.shape, BF16), jax.ShapeDtypeStruct(dw4.shape, F32),
                   jax.ShapeDtypeStruct((K_A * LANE_GROUPS, LANES), F32)],
        input_output_aliases={13: 0, 14: 1},
        scratch_shapes=[_time_major(tm + HALO), _time_major(tm + HALO), _time_major(tm), pltpu.VMEM((tm, D), F32)],
        compiler_params=_params(1),
    )(dya, proj, proj, proj, proj, sa16, dya, proj, proj, proj, proj, wo4_g, cw_a, dproj, dw4)


def _branch_b_bwd(dyb, proj, cb, sb16, wo4_g, cw_b, ln_g, ln_b, dproj, dw4, tm):
    n_rows = proj.shape[0]
    n_tiles = n_rows // tm

    def body(dyb_ref, zb, cb_ref, vb, gb, sb_ref, dybn, zbn, cbn, vbh, gbh, w_ref, cw_ref, lg_ref, lb_ref,
             dp_in, dw_in, dp_ref, dw_ref, dwb_ref, dbb_ref, dlg_ref, dlb_ref, e1, e2, o_scr, mm_scr):
        del dp_in, dw_in
        zbn, vbh, gbh = (_f32(r) for r in (zbn, vbh, gbh))
        i = pl.program_id(0)
        wob = w_ref[...].reshape(D, D)
        lg, lb = lg_ref[...], lb_ref[...]

        def conv_out_grad(dsb, z, c):
            xhat, rstd, ln = _layernorm_parts(c, lg, lb)
            sw, dsw = _silu_and_grad(ln)
            sz, dsz = _silu_and_grad(z)
            dln = (dsb * sz) * dsw
            dxhat = dln * lg
            dcb = rstd * (dxhat - _mean(dxhat) - xhat * _mean(dxhat * xhat))
            return dsb * sw * dsz, dln, xhat, dcb

        @pl.when(i == 0)
        def _():
            dw_ref[...] = jnp.zeros_like(dw_ref)
            dwb_ref[...] = jnp.zeros_like(dwb_ref)
            dbb_ref[...] = jnp.zeros_like(dbb_ref)
            dlg_ref[...] = jnp.zeros_like(dlg_ref)
            dlb_ref[...] = jnp.zeros_like(dlb_ref)

        dyb16 = dyb_ref[...]
        mm_scr[...] = _dot_nt(dyb16, wob)
        dlg, dlb, dbb = (jnp.zeros((SUBLANES, D), F32),) * 3
        for r0 in range(0, tm, ELEM_ROWS):
            rows = pl.ds(r0, ELEM_ROWS)
            dzb, dln, xhat, dcb = conv_out_grad(mm_scr[rows, :], zb[rows, :].astype(F32), cb_ref[rows, :])
            dp_ref[rows, 2 * D:3 * D] = dzb.astype(BF16)
            _to_time_major(e2, r0, dcb)
            dlg, dlb, dbb = dlg + _fold8(dln * xhat), dlb + _fold8(dln), dbb + _fold8(dcb)
        _, _, _, dcbn = conv_out_grad(_dot_nt(dybn[...], wob), zbn[...], cbn[...])
        _to_time_major(e2, tm, jnp.where(i < n_tiles - 1, dcbn, 0.0))
        dlg_ref[...] += dlg
        dlb_ref[...] += dlb
        dbb_ref[...] += dbb
        dw_ref[0] += _dot_tn(sb_ref[...], dyb16)
        _to_time_major(e1, 0, jnp.where(i > 0, vbh[...] * jax.nn.sigmoid(gbh[...]), 0.0))
        for r0 in range(0, tm, ELEM_ROWS):
            rows = pl.ds(r0, ELEM_ROWS)
            glu = vb[rows, :].astype(F32) * jax.nn.sigmoid(gb[rows, :].astype(F32))
            _to_time_major(e1, HALO + r0, glu)
        _conv_wgrad(dwb_ref, e2, e1, FWD_TAPS_B, tm)
        _conv(o_scr, e2, cw_ref, BWD_TAPS_B, tm)
        for r0 in range(0, tm, ELEM_ROWS):
            rows = pl.ds(r0, ELEM_ROWS)
            dglu = _from_time_major(o_scr, ELEM_ROWS, r0)
            sg = jax.nn.sigmoid(gb[rows, :].astype(F32))
            dp_ref[rows, 0 * D:1 * D] = (dglu * sg).astype(BF16)
            dp_ref[rows, 1 * D:2 * D] = (dglu * vb[rows, :].astype(F32) * (sg * (1.0 - sg))).astype(BF16)

    vec = jax.ShapeDtypeStruct((SUBLANES, D), F32)
    return pl.pallas_call(
        body, name="branch_b_bwd", grid=(n_tiles,),
        in_specs=[_rows(tm), _rows(tm, G_ZB), _rows(tm), _rows(tm, G_VB), _rows(tm, G_GB), _rows(tm),
                  _next_halo(tm, n_rows), _next_halo(tm, n_rows, G_ZB), _next_halo(tm, n_rows),
                  _prev_halo(tm, G_VB), _prev_halo(tm, G_GB), _w_out_spec(1), _const(cw_b.shape), _const((1, D)),
                  _const((1, D)), ANY, ANY],
        out_specs=[pl.BlockSpec((tm, 3 * D), lambda i: (i, 2)), pl.BlockSpec((1, D, D), lambda i: (1, 0, 0)),
                   _const((K_B * LANE_GROUPS, LANES)), _const((SUBLANES, D)), _const((SUBLANES, D)),
                   _const((SUBLANES, D))],
        out_shape=[jax.ShapeDtypeStruct(dproj.shape, BF16), jax.ShapeDtypeStruct(dw4.shape, F32),
                   jax.ShapeDtypeStruct((K_B * LANE_GROUPS, LANES), F32), vec, vec, vec],
        input_output_aliases={15: 0, 16: 1},
        scratch_shapes=[_time_major(tm + HALO), _time_major(tm + HALO), _time_major(tm), pltpu.VMEM((tm, D), F32)],
        compiler_params=_params(1),
    )(dyb, proj, cb, proj, proj, sb16, dyb, proj, cb, proj, proj, wo4_g, cw_b, ln_g, ln_b, dproj, dw4)


def _branch_x_bwd(dyx, proj, sx16, kv16, wo4_g, dproj, dw4, tm):
    n_rows = proj.shape[0]
    scale = HEAD_DIM ** -0.5

    def body(dyx_ref, q, zx, sx_ref, kv_ref, w_ref, dp_in, dw_in, dp_ref, dw_ref, dkv_ref):
        del dp_in, dw_in
        i = pl.program_id(0)
        dyx16 = dyx_ref[...]
        q16 = q[...].astype(BF16)
        probs, outs = _attention(q16, kv_ref)
        dsx = _dot_nt(dyx16, w_ref[...].reshape(D, D))
        silu_z, dsilu_z = _silu_and_grad(_f32(zx))
        dp_ref[:, D:2 * D] = (dsx * jnp.concatenate(outs, axis=-1) * dsilu_z).astype(BF16)
        do16 = (dsx * silu_z).astype(BF16)

        @pl.when(i == 0)
        def _():
            dw_ref[...] = jnp.zeros_like(dw_ref)
            dkv_ref[...] = jnp.zeros_like(dkv_ref)

        for h in range(N_HEADS):
            cols = slice(h * HEAD_DIM, (h + 1) * HEAD_DIM)
            p = probs[h]
            dprob = _dot_nt(do16[:, cols], kv_ref[N_HEADS + h])
            ds16 = ((p * (dprob - jnp.sum(p * dprob, axis=-1, keepdims=True))) * scale).astype(BF16)
            dp_ref[:, cols] = _dot(ds16, kv_ref[h]).astype(BF16)
            dkv_ref[h] += _dot_tn(ds16, q16[:, cols])
            dkv_ref[N_HEADS + h] += _dot_tn(p.astype(BF16), do16[:, cols])
        dw_ref[0] += _dot_tn(sx_ref[...], dyx16)

    return pl.pallas_call(
        body, name="branch_x_bwd", grid=(n_rows // tm,),
        in_specs=[_rows(tm), _rows(tm, G_Q), _rows(tm, G_ZX), _rows(tm), _const(kv16.shape), _w_out_spec(2), ANY, ANY],
        out_specs=[pl.BlockSpec((tm, 2 * D), lambda i: (i, 2)), pl.BlockSpec((1, D, D), lambda i: (2, 0, 0)),
                   _const(kv16.shape)],
        out_shape=[jax.ShapeDtypeStruct(dproj.shape, BF16), jax.ShapeDtypeStruct(dw4.shape, F32),
                   jax.ShapeDtypeStruct(kv16.shape, F32)],
        input_output_aliases={6: 0, 7: 1},
        compiler_params=_params(1),
    )(dyx, proj, proj, sx16, kv16, wo4_g, dproj, dw4)


def _dp_unit(u):
    g = u // 2
    pos = jnp.where(g < G_VB, g, jnp.where(g < G_Q, g + 2, jnp.where(g < G_GA, g - 3, g)))
    return 2 * pos + u % 2


def _scatter_copies(srcs, lands, send, recv):
    x, y, c = lax.axis_index("x"), lax.axis_index("y"), lax.axis_index("c")
    copies = []
    for n in range(N_DEV - 1):
        flip = n + 1
        px = 1 - x if flip & 4 else x
        py = 1 - y if flip & 2 else y
        pc = 1 - c if flip & 1 else c
        for t, (src, land) in enumerate(zip(srcs, lands)):
            copies.append(pltpu.make_async_remote_copy(
                src_ref=src.at[4 * px + 2 * py + pc], dst_ref=land.at[n], send_sem=send.at[t * (N_DEV - 1) + n],
                recv_sem=recv.at[t * (N_DEV - 1) + n], device_id=(px, py, pc), device_id_type=MESH))
    return copies


def _scatter_start(name, arrays, views, n_views):
    n = len(arrays)
    lands = [lax.empty(tuple(N_DEV - 1 if d == N_DEV else d for d in a.shape), a.dtype) for a in arrays]

    def body(*refs):
        src, land, (send, recv) = refs[:n], refs[n:2 * n], refs[2 * n:2 * n + 2]
        token = refs[-1]
        for cp in _scatter_copies(views(src), views(land), send, recv):
            cp.start()
        token[...] = jnp.zeros_like(token)

    sems = pltpu.SemaphoreType.DMA((n_views * (N_DEV - 1),))
    out = pl.pallas_call(
        body, name=name,
        in_specs=[HBM] * (2 * n),
        out_specs=[SEM, SEM] + [HBM] * (2 * n) + [pl.BlockSpec(memory_space=pltpu.VMEM)],
        out_shape=[sems, sems] + [pltpu.HBM(a.shape, a.dtype) for a in arrays + lands]
        + [jax.ShapeDtypeStruct((SUBLANES, LANES), F32)],
        input_output_aliases={k: 2 + k for k in range(2 * n)},
        compiler_params=pltpu.CompilerParams(has_side_effects=SIDE_EFFECT),
    )(*[pltpu.with_memory_space_constraint(a, pltpu.HBM) for a in arrays + lands])
    return dict(name=name, sems=out[:2], moving=out[2:2 + 2 * n], views=views, token=out[-1])


def _scatter_wait(started, after):
    n = len(started["moving"]) // 2
    views = started["views"]

    def body(*refs):
        src, land, (send, recv) = refs[:n], refs[n:2 * n], refs[2 * n:2 * n + 2]
        for cp in _scatter_copies(views(src), views(land), send, recv):
            cp.wait_send()
            cp.wait_recv()

    out = pl.pallas_call(
        body, name=started["name"].replace("start", "wait"),
        in_specs=[HBM] * (2 * n) + [SEM, SEM, ANY],
        out_specs=[HBM] * (2 * n),
        out_shape=[pltpu.HBM(a.shape, a.dtype) for a in started["moving"]],
        input_output_aliases={k: k for k in range(2 * n)},
        compiler_params=pltpu.CompilerParams(has_side_effects=SIDE_EFFECT),
    )(*started["moving"], *started["sems"], after)
    return out[n:]


def _w_in_grad(ut, dproj, tk, token):
    n_rows = dproj.shape[0]
    n_k = n_rows // tk
    per_shard = W_IN_SHARD // UNIT

    def body(ut_ref, dp0, dp1, dp2, token_ref, out_ref, out16_ref, acc):
        del token_ref
        t = pl.program_id(1)

        for r, dp_ref in enumerate((dp0, dp1, dp2)):
            cols = slice(r * UNIT, (r + 1) * UNIT)

            @pl.when(t == 0)
            def _(dp_ref=dp_ref, cols=cols):
                acc[:, cols] = _dot(ut_ref[...], dp_ref[...])

            @pl.when(t > 0)
            def _(dp_ref=dp_ref, cols=cols):
                acc[:, cols] += _dot(ut_ref[...], dp_ref[...])

        @pl.when(t == n_k - 1)
        def _():
            out_ref[0] = acc[...]
            out16_ref[0] = acc[...].astype(BF16)

    def dp_spec(r):
        return pl.BlockSpec((tk, UNIT), lambda q, t: (t, _dp_unit(per_shard * q + r)))

    shard = pl.BlockSpec((1, D, W_IN_SHARD), lambda q, t: (q, 0, 0))
    return pl.pallas_call(
        body, name="w_in_grad", grid=(N_DEV, n_k),
        in_specs=[pl.BlockSpec((D, tk), lambda q, t: (0, t)), dp_spec(0), dp_spec(1), dp_spec(2), ANY],
        out_specs=[shard, shard],
        out_shape=[jax.ShapeDtypeStruct((N_DEV, D, W_IN_SHARD), F32), jax.ShapeDtypeStruct((N_DEV, D, W_IN_SHARD), BF16)],
        scratch_shapes=[pltpu.VMEM((D, W_IN_SHARD), F32)],
        compiler_params=_params(2),
    )(ut, dproj, dproj, dproj, token)


def _x_grad(dproj, win_t, x, dh, norm_g, token, tm):
    n_rows = x.shape[0]
    n_k = N_GROUPS * D // X_GRAD_K

    def body(dp_ref, wt_ref, x_ref, dh_ref, g_ref, token_ref, gx_ref, dg_ref, acc):
        del token_ref
        i, g = pl.program_id(0), pl.program_id(1)

        @pl.when((i == 0) & (g == 0))
        def _():
            dg_ref[...] = jnp.zeros_like(dg_ref)

        @pl.when(g == 0)
        def _():
            acc[...] = _dot(dp_ref[...], wt_ref[...])

        @pl.when(g > 0)
        def _():
            acc[...] += _dot(dp_ref[...], wt_ref[...])

        @pl.when(g == n_k - 1)
        def _():
            du = acc[...]
            xf = x_ref[...]
            r = lax.rsqrt(_mean(xf * xf) + EPS)
            xn = xf * r
            dun = du * g_ref[...]
            gx_ref[...] = dh_ref[...] + r * (dun - xn * _mean(dun * xn))
            dg_ref[...] += _fold8(du * xn)

    return pl.pallas_call(
        body, name="x_grad", grid=(n_rows // tm, n_k),
        in_specs=[pl.BlockSpec((tm, X_GRAD_K), lambda i, g: (i, g)), pl.BlockSpec((X_GRAD_K, D), lambda i, g: (g, 0)),
                  pl.BlockSpec((tm, D), lambda i, g: (i, 0)), pl.BlockSpec((tm, D), lambda i, g: (i, 0)),
                  _const((1, D)), ANY],
        out_specs=[pl.BlockSpec((tm, D), lambda i, g: (i, 0)), _const((SUBLANES, D))],
        out_shape=[jax.ShapeDtypeStruct((n_rows, D), F32), jax.ShapeDtypeStruct((SUBLANES, D), F32)],
        scratch_shapes=[pltpu.VMEM((tm, D), F32)],
        compiler_params=_params(2),
    )(dproj, win_t, x, dh, norm_g, token)


def _local_step(x, mem, target, norm_g, conv_b_b, ln_g, ln_b, mem_g, final_g, shards):
    n_rows = x.shape[0]
    tm = min(512, n_rows)
    big = min(1024, n_rows)
    u16, ut = _rmsnorm_fwd(x, norm_g, big)
    proj, win_t, _, wkv_g, wo4_g, cw_g = _proj_fwd_gather(u16, shards, min(2048, n_rows))
    cw_rows = cw_g.transpose(1, 0, 2).reshape((SUBLANES + HALO) * LANE_GROUPS, LANES)
    cw_a, cw_b = cw_rows[:SUBLANES * LANE_GROUPS], cw_rows[SUBLANES * LANE_GROUPS:]
    kv16, mn16 = _kv_fwd(mem, mem_g, wkv_g)
    sa16, ya = _branch_a_fwd(proj, wo4_g, cw_a, tm)
    cb, sb16, yb = _branch_b_fwd(proj, wo4_g, cw_b, conv_b_b, ln_g, ln_b, tm)
    sx16, yx = _branch_x_fwd(proj, kv16, wo4_g, tm)
    dh, dya, dyb, dyx, dproj, dw4, dfg, sq = _merge_fwd_bwd(proj, ya, yb, yx, x, target, wo4_g, final_g,
                                                             min(256, n_rows))
    dproj, dw4, dwa = _branch_a_bwd(dya, proj, sa16, wo4_g, cw_a, dproj, dw4, tm)
    dproj, dw4, dwb, dbb, dlg, dlb = _branch_b_bwd(dyb, proj, cb, sb16, wo4_g, cw_b, ln_g, ln_b, dproj, dw4, tm)
    dproj, dw4, dkv = _branch_x_bwd(dyx, proj, sx16, kv16, wo4_g, dproj, dw4, tm)
    dwkv_g, dwkv16, dmg = _kv_bwd(dkv, mem, mem_g, mn16, wkv_g)
    dw4 = dw4.reshape(4, N_DEV, D // N_DEV, D)
    small_moving = _scatter_start("small_grads_start", [dw4.astype(BF16), dwkv16],
                                  lambda refs: [refs[0].at[w] for w in range(4)] + [refs[1]], 5)
    dwin_g, dwin16 = _w_in_grad(ut, dproj, min(2048, n_rows), small_moving["token"])
    w_in_moving = _scatter_start("w_in_grad_start", [dwin16], lambda refs: list(refs), 1)
    gx, dng = _x_grad(dproj, win_t, x, dh, norm_g, w_in_moving["token"], big)
    land4, landkv = _scatter_wait(small_moving, dng)
    landin, = _scatter_wait(w_in_moving, dng)
    small = {SV_NORM_G: dng, SV_CONV_B_B: dbb, SV_LN_G: dlg, SV_LN_B: dlb, SV_MEM_G: dmg, SV_FINAL_G: dfg, SV_LOSS: sq}
    grads = [(dwin_g[None], landin[None]), (dw4, land4), (dwkv_g[None], landkv[None])]
    return gx, grads, small, dwa.reshape(K_A, D), dwb.reshape(K_B, D)


def _allgather_small(small, conv_rows):
    keys = sorted(small)

    def body(*refs):
        parts, (conv_ref, out_ref, mine, send, recv) = refs[:len(keys)], refs[len(keys):]
        x, y, c, chips = _place()
        me, sibling = 4 * x + 2 * y + c, (x, y, 1 - c)
        mine[pl.ds(0, SV_CONV_A), :] = jnp.zeros((SV_CONV_A, D), F32)
        for key, part in zip(keys, parts):
            mine[key:key + 1, :] = jnp.sum(part[...], axis=0, keepdims=True)
        mine[pl.ds(SV_CONV_A, SV_ROWS - SV_CONV_A), :] = conv_ref[...]
        out_ref[me] = mine[...]

        def copy(k, block, to, from_mine=False):
            return pltpu.make_async_remote_copy(
                src_ref=mine if from_mine else out_ref.at[block], dst_ref=out_ref.at[block],
                send_sem=send.at[k], recv_sem=recv.at[k], device_id=to, device_id_type=MESH)

        first = [copy(0, me, sibling, from_mine=True)]
        first += [copy(1 + j, me, (*chip, c), from_mine=True) for j, chip in enumerate(chips)]
        for cp in first:
            cp.start()
        passed = []
        for j, (px, py) in enumerate(chips):
            block = 4 * px + 2 * py + c
            copy(1 + j, block, sibling).wait_recv()
            passed.append(copy(4 + j, block, sibling))
            passed[-1].start()
        copy(0, 4 * x + 2 * y + 1 - c, sibling).wait_recv()
        for j, (px, py) in enumerate(chips):
            copy(4 + j, 4 * px + 2 * py + 1 - c, sibling).wait_recv()
        for cp in first + passed:
            cp.wait_send()

    vmem = pl.BlockSpec(memory_space=pltpu.VMEM)
    return pl.pallas_call(
        body, name="allgather_small",
        in_specs=[vmem] * (len(keys) + 1), out_specs=vmem,
        out_shape=jax.ShapeDtypeStruct((N_DEV, SV_ROWS, D), F32),
        scratch_shapes=[pltpu.VMEM((SV_ROWS, D), F32), pltpu.SemaphoreType.DMA((7,)), pltpu.SemaphoreType.DMA((7,))],
    )(*[small[k] for k in keys], conv_rows)


def _adamw(w, g, m, v):
    m = ADAM_B1 * m + (1.0 - ADAM_B1) * g
    v = ADAM_B2 * v + (1.0 - ADAM_B2) * (g * g)
    m_hat = m / (1.0 - ADAM_B1 ** ADAM_STEP)
    v_hat = v / (1.0 - ADAM_B2 ** ADAM_STEP)
    return -ADAM_LR * (m_hat / (jnp.sqrt(v_hat) + ADAM_EPS) + ADAM_WD * w), m, v


def _adamw_shard(own, landed, piece, k_arr, w, m, v, tr):
    n_r, n_c = w.shape
    n_landed = landed.shape[1]

    def body(k_ref, own_ref, *refs):
        del k_ref
        landed_refs, (w_ref, m_ref, v_ref, g_out, d_out, m_out, v_out) = refs[:n_landed], refs[n_landed:]
        g = own_ref[0, 0]
        for landed_ref in landed_refs:
            g = g + landed_ref[0, 0].astype(F32)
        g_out[...] = g
        d_out[...], m_out[...], v_out[...] = _adamw(w_ref[...], g, m_ref[...], v_ref[...])

    blk = (1, 1, tr, n_c)
    flat = pl.BlockSpec((tr, n_c), lambda r, k: (r, 0))
    return pl.pallas_call(
        body, name="adamw_shard",
        grid_spec=pltpu.PrefetchScalarGridSpec(
            num_scalar_prefetch=1, grid=(n_r // tr,),
            in_specs=[pl.BlockSpec(blk, lambda r, k: (piece, k[0], r, 0))]
            + [pl.BlockSpec(blk, functools.partial(lambda r, k, j: (piece, j, r, 0), j=j)) for j in range(n_landed)]
            + [flat] * 3,
            out_specs=[flat] * 4),
        out_shape=[jax.ShapeDtypeStruct((n_r, n_c), F32)] * 4,
        compiler_params=_params(1),
    )(k_arr, own, *([landed] * n_landed), w, m, v)


def _adamw_small(gathered, k_arr, vectors, conv_a, conv_b):
    n_vec = len(vectors)
    cols = D // N_DEV

    def body(k_ref, full_ref, cols_ref, *refs):
        del k_ref
        ins, outs = refs[:3 * (n_vec + 2)], refs[3 * (n_vec + 2):-2]
        tot, tot_cols = refs[-2:]
        tot[...] = full_ref[0]
        tot_cols[...] = cols_ref[0]
        for dev in range(1, N_DEV):
            tot[...] += full_ref[dev]
            tot_cols[...] += cols_ref[dev]
        loss = (0.5 / D) * jnp.sum(tot[SV_LOSS:SV_LOSS + 1, :])
        outs[0][...] = jnp.full(outs[0].shape, loss, F32)
        grads = [tot[n:n + 1, :] for n in range(n_vec)]
        grads += [tot_cols[pl.ds(SV_CONV_A, K_A), :], tot_cols[pl.ds(SV_CONV_B, K_B), :]]
        for n, g in enumerate(grads):
            w_ref, m_ref, v_ref = ins[3 * n:3 * n + 3]
            g_out, d_out, m_out, v_out = outs[1 + 4 * n:5 + 4 * n]
            g_out[...] = g
            d_out[...], m_out[...], v_out[...] = _adamw(w_ref[...], g, m_ref[...], v_ref[...])

    weights = list(vectors) + [conv_a, conv_b]
    flat_in = [a for wmv in weights for a in wmv]
    out_shape = [jax.ShapeDtypeStruct((SUBLANES, 128), F32)]
    for wmv in weights:
        out_shape += [jax.ShapeDtypeStruct(wmv[0].shape, F32)] * 4
    return pl.pallas_call(
        body, name="adamw_small",
        grid_spec=pltpu.PrefetchScalarGridSpec(
            num_scalar_prefetch=1, grid=(1,),
            in_specs=[pl.BlockSpec((N_DEV, SV_ROWS, D), lambda i, k: (0, 0, 0)),
                      pl.BlockSpec((N_DEV, SV_ROWS, cols), lambda i, k: (0, 0, k[0]))]
            + [pl.BlockSpec(a.shape, lambda i, k: (0, 0)) for a in flat_in],
            out_specs=[pl.BlockSpec(s.shape, lambda i, k: (0, 0)) for s in out_shape],
            scratch_shapes=[pltpu.VMEM((SV_ROWS, D), F32), pltpu.VMEM((SV_ROWS, cols), F32)]),
        out_shape=out_shape,
        compiler_params=_params(1),
    )(k_arr, gathered, gathered, *flat_in)


def kernel(x, mem, norm_g, w_in, conv_a_w, w_out_a, conv_b_w, conv_b_b, ln_b_g, ln_b_b, w_out_b, mem_norm_g, w_kv, w_out_x, w_o, final_g, loss_target, m_norm_g, m_w_in, m_conv_a_w, m_w_out_a, m_conv_b_w, m_conv_b_b, m_ln_b_g, m_ln_b_b, m_w_out_b, m_mem_norm_g, m_w_kv, m_w_out_x, m_w_o, m_final_g, v_norm_g, v_w_in, v_conv_a_w, v_w_out_a, v_conv_b_w, v_conv_b_b, v_ln_b_g, v_ln_b_b, v_w_out_b, v_mem_norm_g, v_w_kv, v_w_out_x, v_w_o, v_final_g):
    xi, yi, ci = lax.axis_index("x"), lax.axis_index("y"), lax.axis_index("c")
    k_arr = jnp.reshape(4 * xi + 2 * yi + ci, (1,)).astype(jnp.int32)

    cw = jnp.concatenate([jnp.pad(conv_a_w[0], ((0, SUBLANES - K_A), (0, 0))),
                          jnp.pad(conv_b_w[0], ((0, HALO - K_B), (0, 0)))], axis=0)
    wo4 = jnp.stack([w_out_a[0], w_out_b[0], w_out_x[0], w_o[0]]).astype(BF16)
    shards = [w_in[0].astype(BF16), w_kv[0].astype(BF16), wo4, cw]

    final_g2 = final_g.reshape(1, D)
    gx, grads, small, dwa, dwb = _local_step(
        x[0], mem[0], loss_target[0], norm_g, conv_b_b, ln_b_g, ln_b_b, mem_norm_g, final_g2, shards)

    conv_rows = jnp.concatenate([jnp.pad(dwa, ((0, SUBLANES - K_A), (0, 0))),
                                 jnp.pad(dwb, ((0, HALO - K_B), (0, 0)))], axis=0)
    gathered_small = _allgather_small(small, conv_rows)

    tiles = [256, D // N_DEV, 256]

    def shard(a, l, w, m, v):
        return _adamw_shard(grads[a][0], grads[a][1], l, k_arr, w[0], m[0], v[0], tiles[a])

    res = {
        "w_in": shard(0, 0, w_in, m_w_in, v_w_in),
        "w_out_a": shard(1, 0, w_out_a, m_w_out_a, v_w_out_a),
        "w_out_b": shard(1, 1, w_out_b, m_w_out_b, v_w_out_b),
        "w_out_x": shard(1, 2, w_out_x, m_w_out_x, v_w_out_x),
        "w_o": shard(1, 3, w_o, m_w_o, v_w_o),
        "w_kv": shard(2, 0, w_kv, m_w_kv, v_w_kv),
    }
    res = {name: tuple(r[None] for r in four) for name, four in res.items()}
    vectors = [(norm_g, m_norm_g, v_norm_g), (conv_b_b, m_conv_b_b, v_conv_b_b), (ln_b_g, m_ln_b_g, v_ln_b_g),
               (ln_b_b, m_ln_b_b, v_ln_b_b), (mem_norm_g, m_mem_norm_g, v_mem_norm_g),
               (final_g2, m_final_g.reshape(1, D), v_final_g.reshape(1, D))]
    out = _adamw_small(gathered_small, k_arr, vectors, (conv_a_w[0], m_conv_a_w[0], v_conv_a_w[0]),
                       (conv_b_w[0], m_conv_b_w[0], v_conv_b_w[0]))
    loss = out[0][0, 0]
    names = ["norm_g", "conv_b_b", "ln_b_g", "ln_b_b", "mem_norm_g", "final_g", "conv_a_w", "conv_b_w"]
    for n, name in enumerate(names):
        four = out[1 + 4 * n:5 + 4 * n]
        if name == "final_g":
            four = [r.reshape(D) for r in four]
        elif name.startswith("conv_") and name.endswith("_w"):
            four = [r[None] for r in four]
        res[name] = tuple(four)

    order = ["norm_g", "w_in", "conv_a_w", "w_out_a", "conv_b_w", "conv_b_b", "ln_b_g", "ln_b_b", "w_out_b",
             "mem_norm_g", "w_kv", "w_out_x", "w_o", "final_g"]
    return (loss, gx[None], *[res[n][0] for n in order], *[res[n][1] for n in order],
            *[res[n][2] for n in order], *[res[n][3] for n in order])
```

```python
import functools

import jax
import jax.numpy as jnp
from jax import lax
from jax.experimental import pallas as pl
from jax.experimental.pallas import tpu as pltpu

F32, BF16 = jnp.float32, jnp.bfloat16
D = 1024
N_DEV = 8
N_HEADS = 4
HEAD_DIM = D // N_HEADS
N_GROUPS = 12
W_IN_SHARD = N_GROUPS * D // N_DEV
UNIT = 512
X_GRAD_K = 2 * D
K_A, K_B = 3, 31
EPS = 1e-6
HALO = 32
SUBLANES = 8
LANES = 128
LANE_GROUPS = D // LANES
TAP_GROUP = 16
CONV_BLOCK = 32
ELEM_ROWS = 16
CONV_PARTIAL_SUMS = 4
VMEM_LIMIT = 56 << 20
MESH = pl.DeviceIdType.MESH
ANY = pl.BlockSpec(memory_space=pl.ANY)
HBM = pl.BlockSpec(memory_space=pltpu.HBM)
SEM = pl.BlockSpec(memory_space=pltpu.SEMAPHORE)
SIDE_EFFECT = pltpu.SideEffectType.DATAFLOW_SIDE_EFFECTING

G_BA, G_CA, G_XA, G_ZA, G_VB, G_GB, G_ZB, G_Q, G_ZX, G_GA, G_GBB, G_GX = range(N_GROUPS)

ADAM_LR, ADAM_B1, ADAM_B2, ADAM_EPS, ADAM_WD, ADAM_STEP = 0.001, 0.9, 0.999, 1e-08, 0.01, 10

SV_NORM_G, SV_CONV_B_B, SV_LN_G, SV_LN_B, SV_MEM_G, SV_FINAL_G, SV_LOSS = range(7)
SV_CONV_A, SV_CONV_B, SV_ROWS = 8, 16, 48


def _dot(a, b):
    return jnp.dot(a, b, preferred_element_type=F32)


def _dot_nt(a, b):
    return lax.dot_general(a, b, (((1,), (1,)), ((), ())), preferred_element_type=F32)


def _dot_tn(a, b):
    return lax.dot_general(a, b, (((0,), (0,)), ((), ())), preferred_element_type=F32)


def _silu_and_grad(z):
    s = jax.nn.sigmoid(z)
    return z * s, s * (1.0 + z * (1.0 - s))


def _fold8(a):
    return a.reshape(a.shape[0] // SUBLANES, SUBLANES, a.shape[1]).sum(axis=0)


def _mean(a):
    return jnp.mean(a, axis=-1, keepdims=True)


def _f32(ref):
    return ref[...].astype(F32)


def _params(n_grid):
    return pltpu.CompilerParams(dimension_semantics=("arbitrary",) * n_grid, vmem_limit_bytes=VMEM_LIMIT)


def _rows(tm, col=0):
    return pl.BlockSpec((tm, D), lambda i: (i, col))


def _prev_halo(tm, col=0):
    return pl.BlockSpec((HALO, D), lambda i: (jnp.maximum(i * (tm // HALO) - 1, 0), col))


def _next_halo(tm, n_rows, col=0):
    last = n_rows // HALO - 1
    return pl.BlockSpec((HALO, D), lambda i: (jnp.minimum((i + 1) * (tm // HALO), last), col))


def _const(shape):
    return pl.BlockSpec(shape, lambda *_: (0,) * len(shape))


def _w_out_spec(which):
    return pl.BlockSpec((N_DEV, None, D // N_DEV, D), lambda *_: (0, which, 0, 0))


def _to_time_major(t_ref, row0, x):
    n = x.shape[0]
    for j in range(LANE_GROUPS):
        t_ref[pl.ds(row0 * LANE_GROUPS + j, n, stride=LANE_GROUPS), :] = x[:, j * LANES:(j + 1) * LANES]


def _from_time_major(t_ref, n, row0=0):
    return jnp.concatenate([t_ref[pl.ds(row0 * LANE_GROUPS + j, n, stride=LANE_GROUPS), :]
                            for j in range(LANE_GROUPS)], axis=-1)


def _row(ref, t):
    start = t * LANE_GROUPS
    if not isinstance(start, int):
        start = pl.multiple_of(start, LANE_GROUPS)
    return ref[pl.ds(start, LANE_GROUPS), :]


def _tap_groups(taps):
    return [taps[first:first + TAP_GROUP] for first in range(0, len(taps), TAP_GROUP)]


def _conv(o_ref, e_ref, w_ref, taps, n_rows, bias_ref=None):
    for n_group, group in enumerate(_tap_groups(taps)):
        weights = [_row(w_ref, k) for k, _ in group]

        def block(c, carry, n_group=n_group, group=group, weights=weights):
            t0 = c * CONV_BLOCK
            window = {}
            for t in range(CONV_BLOCK):
                parts = [None] * min(CONV_PARTIAL_SUMS, len(group))
                for n, (_, off) in enumerate(group):
                    if t + off not in window:
                        window[t + off] = _row(e_ref, t0 + t + off)
                    term = weights[n] * window[t + off]
                    parts[n % len(parts)] = term if parts[n % len(parts)] is None else parts[n % len(parts)] + term
                window.pop(t + min(off for _, off in group), None)
                while len(parts) > 1:
                    parts = [parts[n] + parts[n + 1] for n in range(0, len(parts) - 1, 2)] + parts[len(parts) & ~1:]
                out = parts[0]
                if n_group > 0:
                    out = out + _row(o_ref, t0 + t)
                elif bias_ref is not None:
                    out = out + bias_ref[...]
                o_ref[pl.ds(pl.multiple_of((t0 + t) * LANE_GROUPS, LANE_GROUPS), LANE_GROUPS), :] = out
            return carry

        lax.fori_loop(0, n_rows // CONV_BLOCK, block, 0)


def _conv_wgrad(dw_ref, d_ref, e_ref, taps, n_rows):
    for group in _tap_groups(taps):
        def block(c, sums, group=group):
            t0 = c * CONV_BLOCK
            sums = list(sums)
            window = {}
            for t in range(CONV_BLOCK):
                d = _row(d_ref, t0 + t)
                for n, (_, off) in enumerate(group):
                    if t + off not in window:
                        window[t + off] = _row(e_ref, t0 + t + off)
                    sums[n] = sums[n] + d * window[t + off]
                window.pop(t + min(off for _, off in group), None)
            return tuple(sums)

        sums = lax.fori_loop(0, n_rows // CONV_BLOCK, block, tuple(_row(dw_ref, k) for k, _ in group))
        for (k, _), total in zip(group, sums):
            dw_ref[pl.ds(k * LANE_GROUPS, LANE_GROUPS), :] = total


FWD_TAPS_A = [(k, HALO - (K_A - 1) + k) for k in range(K_A)]
BWD_TAPS_A = [(k, K_A - 1 - k) for k in range(K_A)]
FWD_TAPS_B = [(k, HALO - (K_B - 1) + k) for k in range(K_B)]
BWD_TAPS_B = [(k, K_B - 1 - k) for k in range(K_B)]


def _time_major(n_rows):
    return pltpu.VMEM((n_rows * LANE_GROUPS, LANES), F32)


def _kv_fwd(mem, mem_g, wkv_g):
    m_len = mem.shape[0]

    def body(mem_ref, g_ref, w_ref, kv_ref, mn_ref):
        mf = mem_ref[...]
        r = lax.rsqrt(_mean(mf * mf) + EPS)
        mn = ((mf * r) * g_ref[...]).astype(BF16)
        mn_ref[...] = mn
        for b in range(2 * N_HEADS):
            kv_ref[b] = _dot(mn, w_ref[b]).astype(BF16)

    return pl.pallas_call(
        body, name="kv_fwd", grid=(1,),
        in_specs=[_const((m_len, D)), _const((1, D)), _const((2 * N_HEADS, D, HEAD_DIM))],
        out_specs=[_const((2 * N_HEADS, m_len, HEAD_DIM)), _const((m_len, D))],
        out_shape=[jax.ShapeDtypeStruct((2 * N_HEADS, m_len, HEAD_DIM), BF16), jax.ShapeDtypeStruct((m_len, D), BF16)],
        compiler_params=_params(1),
    )(mem, mem_g, wkv_g)


def _kv_bwd(dkv, mem, mem_g, mn16, wkv_g):
    m_len = mem.shape[0]

    def body(dkv_ref, mem_ref, g_ref, mn_ref, w_ref, dw_ref, dw16_ref, dg_ref):
        mn = mn_ref[...]
        dmn = jnp.zeros((m_len, D), F32)
        for b in range(2 * N_HEADS):
            d16 = dkv_ref[b].astype(BF16)
            dw = _dot_tn(mn, d16)
            dw_ref[b] = dw
            dw16_ref[b] = dw.astype(BF16)
            dmn = dmn + _dot_nt(d16, w_ref[b])
        mf = mem_ref[...]
        r = lax.rsqrt(_mean(mf * mf) + EPS)
        dg_ref[...] = _fold8(dmn * (mf * r))

    return pl.pallas_call(
        body, name="kv_bwd", grid=(1,),
        in_specs=[_const((2 * N_HEADS, m_len, HEAD_DIM)), _const((m_len, D)), _const((1, D)), _const((m_len, D)),
                  _const((2 * N_HEADS, D, HEAD_DIM))],
        out_specs=[_const((2 * N_HEADS, D, HEAD_DIM)), _const((2 * N_HEADS, D, HEAD_DIM)), _const((SUBLANES, D))],
        out_shape=[jax.ShapeDtypeStruct((2 * N_HEADS, D, HEAD_DIM), F32),
                   jax.ShapeDtypeStruct((2 * N_HEADS, D, HEAD_DIM), BF16), jax.ShapeDtypeStruct((SUBLANES, D), F32)],
        compiler_params=_params(1),
    )(dkv, mem, mem_g, mn16, wkv_g)


def _rmsnorm_fwd(x, norm_g, tm):
    n_rows = x.shape[0]

    def body(x_ref, g_ref, u_ref, ut_ref):
        xf = x_ref[...]
        u = (xf * lax.rsqrt(_mean(xf * xf) + EPS)) * g_ref[...]
        u_ref[...] = u.astype(BF16)
        ut_ref[...] = u.T.astype(BF16)

    return pl.pallas_call(
        body, name="rmsnorm_fwd", grid=(n_rows // tm,),
        in_specs=[_rows(tm), _const((1, D))],
        out_specs=[_rows(tm), pl.BlockSpec((D, tm), lambda i: (0, i))],
        out_shape=[jax.ShapeDtypeStruct((n_rows, D), BF16), jax.ShapeDtypeStruct((D, n_rows), BF16)],
        compiler_params=_params(1),
    )(x, norm_g)


def _place():
    x, y, c = lax.axis_index("x"), lax.axis_index("y"), lax.axis_index("c")
    other_chips = [(1 - x, y), (x, 1 - y), (1 - x, 1 - y)]
    return x, y, c, other_chips


def _arrival_order():
    x, y, c, chips = _place()
    order = [4 * x + 2 * y + c, 4 * x + 2 * y + 1 - c]
    for px, py in chips:
        order += [4 * px + 2 * py + c, 4 * px + 2 * py + 1 - c]
    return order


def _proj_fwd_gather(u16, blocks, tm):
    n = len(blocks)
    n_rows = u16.shape[0]
    n_i = n_rows // tm
    per_shard = W_IN_SHARD // UNIT
    assert n_i >= per_shard

    def wt_index(p, i, order):
        return (_dp_unit(per_shard * order[p] + jnp.minimum(i, per_shard - 1)), 0)

    def body(order_ref, u_ref, *refs):
        src, proj_ref, wt_ref, out = refs[:n], refs[n], refs[n + 1], refs[n + 2:2 * n + 2]
        wbuf, stage_sem, send, recv, own_sem = refs[2 * n + 2:]
        p, i = pl.program_id(0), pl.program_id(1)
        x, y, c, chips = _place()
        me, sibling = 4 * x + 2 * y + c, (x, y, 1 - c)

        def copy(t, k, block, to, from_input=False):
            return pltpu.make_async_remote_copy(
                src_ref=src[t] if from_input else out[t].at[block], dst_ref=out[t].at[block],
                send_sem=send.at[t, k], recv_sem=recv.at[t, k], device_id=to, device_id_type=MESH)

        def own_copies():
            return [pltpu.make_async_copy(src[t], out[t].at[me], own_sem.at[t]) for t in range(n)]

        def first_copies():
            first = []
            for t in range(n):
                first.append(copy(t, 0, me, sibling, from_input=True))
                first += [copy(t, 1 + j, me, (*chip, c), from_input=True) for j, chip in enumerate(chips)]
            return first

        def stage(slot, block):
            return pltpu.make_async_copy(out[0].at[block], wbuf.at[slot], stage_sem.at[slot])

        @pl.when((p == 0) & (i == 0))
        def _():
            for cp in own_copies() + first_copies():
                cp.start()
            mine = pltpu.make_async_copy(src[0], wbuf.at[0], stage_sem.at[0])
            mine.start()
            mine.wait()

        @pl.when((p > 0) & (i == 0))
        def _():
            stage(p % 2, order_ref[p]).wait()

        proj_ref[...] = _dot(u_ref[...], wbuf[p % 2])
        for r in range(per_shard):
            @pl.when(i == r)
            def _(r=r):
                wt_ref[...] = wbuf[p % 2, :, r * UNIT:(r + 1) * UNIT].astype(F32).T.astype(BF16)

        for nxt in range(1, N_DEV):
            @pl.when((p == nxt - 1) & (i == n_i - 1))
            def _(nxt=nxt):
                if nxt == 1:
                    block = 4 * x + 2 * y + 1 - c
                    copy(0, 0, block, sibling).wait_recv()
                else:
                    j, passed_on = divmod(nxt - 2, 2)
                    px, py = chips[j]
                    if passed_on:
                        block = 4 * px + 2 * py + 1 - c
                        copy(0, 4 + j, block, sibling).wait_recv()
                    else:
                        block = 4 * px + 2 * py + c
                        copy(0, 1 + j, block, sibling).wait_recv()
                        copy(0, 4 + j, block, sibling).start()
                stage(nxt % 2, block).start()

        @pl.when((p == N_DEV - 1) & (i == n_i - 1))
        def _():
            passed = [copy(0, 4 + j, 4 * px + 2 * py + c, sibling) for j, (px, py) in enumerate(chips)]
            for j, (px, py) in enumerate(chips):
                for t in range(1, n):
                    block = 4 * px + 2 * py + c
                    copy(t, 1 + j, block, sibling).wait_recv()
                    passed.append(copy(t, 4 + j, block, sibling))
                    passed[-1].start()
            for t in range(1, n):
                copy(t, 0, 4 * x + 2 * y + 1 - c, sibling).wait_recv()
                for j, (px, py) in enumerate(chips):
                    copy(t, 4 + j, 4 * px + 2 * py + 1 - c, sibling).wait_recv()
            for cp in first_copies() + passed:
                cp.wait_send()
            for cp in own_copies():
                cp.wait()

    return pl.pallas_call(
        body, name="proj_fwd_gather",
        grid_spec=pltpu.PrefetchScalarGridSpec(
            num_scalar_prefetch=1, grid=(N_DEV, n_i),
            in_specs=[pl.BlockSpec((tm, D), lambda p, i, order: (i, 0))] + [ANY] * n,
            out_specs=[pl.BlockSpec((tm, W_IN_SHARD), lambda p, i, order: (i, order[p])),
                       pl.BlockSpec((UNIT, D), wt_index)] + [ANY] * n,
            scratch_shapes=[pltpu.VMEM((2, D, W_IN_SHARD), BF16), pltpu.SemaphoreType.DMA((2,)),
                            pltpu.SemaphoreType.DMA((n, 7)), pltpu.SemaphoreType.DMA((n, 7)),
                            pltpu.SemaphoreType.DMA((n,))]),
        out_shape=[jax.ShapeDtypeStruct((n_rows, N_GROUPS * D), F32), jax.ShapeDtypeStruct((N_GROUPS * D, D), BF16)]
        + [jax.ShapeDtypeStruct((N_DEV, *b.shape), b.dtype) for b in blocks],
        compiler_params=_params(2),
    )(jnp.stack(_arrival_order()).astype(jnp.int32), u16, *blocks)


def _branch_a_fwd(proj, wo4_g, cw_a, tm):
    n_rows = proj.shape[0]

    def body(bp, cp, xp, za, cph, xph, w_ref, cw_ref, sa_ref, ya_ref, e_scr, o_scr):
        i = pl.program_id(0)
        _to_time_major(e_scr, 0, jnp.where(i > 0, _f32(cph) * _f32(xph), 0.0))
        for r0 in range(0, tm, ELEM_ROWS):
            rows = pl.ds(r0, ELEM_ROWS)
            _to_time_major(e_scr, HALO + r0, cp[rows, :].astype(F32) * xp[rows, :].astype(F32))
        _conv(o_scr, e_scr, cw_ref, FWD_TAPS_A, tm)
        for r0 in range(0, tm, ELEM_ROWS):
            rows = pl.ds(r0, ELEM_ROWS)
            ca = _from_time_major(o_scr, ELEM_ROWS, r0)
            sa_ref[rows, :] = (jax.nn.silu(za[rows, :].astype(F32)) * (bp[rows, :].astype(F32) * ca)).astype(BF16)
        ya_ref[...] = _dot(sa_ref[...], w_ref[...].reshape(D, D))

    return pl.pallas_call(
        body, name="branch_a_fwd", grid=(n_rows // tm,),
        in_specs=[_rows(tm, G_BA), _rows(tm, G_CA), _rows(tm, G_XA), _rows(tm, G_ZA),
                  _prev_halo(tm, G_CA), _prev_halo(tm, G_XA), _w_out_spec(0), _const(cw_a.shape)],
        out_specs=[_rows(tm), _rows(tm)],
        out_shape=[jax.ShapeDtypeStruct((n_rows, D), BF16), jax.ShapeDtypeStruct((n_rows, D), F32)],
        scratch_shapes=[_time_major(tm + HALO), _time_major(tm)],
        compiler_params=_params(1),
    )(proj, proj, proj, proj, proj, proj, wo4_g, cw_a)


def _layernorm_parts(cb, lg, lb):
    xc = cb - _mean(cb)
    rstd = lax.rsqrt(_mean(xc * xc) + EPS)
    xhat = xc * rstd
    return xhat, rstd, xhat * lg + lb


def _branch_b_fwd(proj, wo4_g, cw_b, conv_b_b, ln_g, ln_b, tm):
    n_rows = proj.shape[0]

    def body(vb, gb, zb, vbh, gbh, w_ref, cw_ref, bb_ref, lg_ref, lb_ref, cb_ref, sb_ref, yb_ref, e_scr, o_scr):
        i = pl.program_id(0)
        vbh, gbh = _f32(vbh), _f32(gbh)
        lg, lb = lg_ref[...], lb_ref[...]
        _to_time_major(e_scr, 0, jnp.where(i > 0, vbh * jax.nn.sigmoid(gbh), 0.0))
        for r0 in range(0, tm, ELEM_ROWS):
            rows = pl.ds(r0, ELEM_ROWS)
            _to_time_major(e_scr, HALO + r0, vb[rows, :].astype(F32) * jax.nn.sigmoid(gb[rows, :].astype(F32)))
        _conv(o_scr, e_scr, cw_ref, FWD_TAPS_B, tm, bias_ref=bb_ref)
        for r0 in range(0, tm, ELEM_ROWS):
            rows = pl.ds(r0, ELEM_ROWS)
            cb = _from_time_major(o_scr, ELEM_ROWS, r0)
            cb_ref[rows, :] = cb
            _, _, ln = _layernorm_parts(cb, lg, lb)
            sb_ref[rows, :] = (jax.nn.silu(zb[rows, :].astype(F32)) * jax.nn.silu(ln)).astype(BF16)
        yb_ref[...] = _dot(sb_ref[...], w_ref[...].reshape(D, D))

    return pl.pallas_call(
        body, name="branch_b_fwd", grid=(n_rows // tm,),
        in_specs=[_rows(tm, G_VB), _rows(tm, G_GB), _rows(tm, G_ZB), _prev_halo(tm, G_VB), _prev_halo(tm, G_GB),
                  _w_out_spec(1), _const(cw_b.shape), _const((LANE_GROUPS, LANES)), _const((1, D)), _const((1, D))],
        out_specs=[_rows(tm), _rows(tm), _rows(tm)],
        out_shape=[jax.ShapeDtypeStruct((n_rows, D), F32), jax.ShapeDtypeStruct((n_rows, D), BF16),
                   jax.ShapeDtypeStruct((n_rows, D), F32)],
        scratch_shapes=[_time_major(tm + HALO), _time_major(tm)],
        compiler_params=_params(1),
    )(proj, proj, proj, proj, proj, wo4_g, cw_b, conv_b_b.reshape(LANE_GROUPS, LANES), ln_g, ln_b)


def _attention(q16, kv_ref):
    probs, outs = [], []
    for h in range(N_HEADS):
        s = _dot_nt(q16[:, h * HEAD_DIM:(h + 1) * HEAD_DIM], kv_ref[h]) * (HEAD_DIM ** -0.5)
        e = jnp.exp(s - jnp.max(s, axis=-1, keepdims=True))
        p = e / jnp.sum(e, axis=-1, keepdims=True)
        probs.append(p)
        outs.append(_dot(p.astype(BF16), kv_ref[N_HEADS + h]))
    return probs, outs


def _branch_x_fwd(proj, kv16, wo4_g, tm):
    n_rows = proj.shape[0]

    def body(q, zx, kv_ref, w_ref, sx_ref, yx_ref):
        _, outs = _attention(q[...].astype(BF16), kv_ref)
        sx = (jax.nn.silu(_f32(zx)) * jnp.concatenate(outs, axis=-1)).astype(BF16)
        sx_ref[...] = sx
        yx_ref[...] = _dot(sx, w_ref[...].reshape(D, D))

    return pl.pallas_call(
        body, name="branch_x_fwd", grid=(n_rows // tm,),
        in_specs=[_rows(tm, G_Q), _rows(tm, G_ZX), _const(kv16.shape), _w_out_spec(2)],
        out_specs=[_rows(tm), _rows(tm)],
        out_shape=[jax.ShapeDtypeStruct((n_rows, D), BF16), jax.ShapeDtypeStruct((n_rows, D), F32)],
        compiler_params=_params(1),
    )(proj, proj, kv16, wo4_g)


def _merge_fwd_bwd(proj, ya, yb, yx, x, target, wo4_g, final_g, tm):
    n_rows = proj.shape[0]
    inv_d = 1.0 / D

    def body(ga, gb, gx, ya_ref, yb_ref, yx_ref, x_ref, t_ref, w_ref, fg_ref,
             dh_ref, dya_ref, dyb_ref, dyx_ref, dp_ref, dw_ref, dfg_ref, sq_ref):
        i = pl.program_id(0)
        wo = w_ref[...].reshape(D, D)
        sig = [jax.nn.sigmoid(_f32(g)) for g in (ga, gb, gx)]
        ys = [ya_ref[...], yb_ref[...], yx_ref[...]]
        m16 = (sig[0] * ys[0] + sig[1] * ys[1] + sig[2] * ys[2]).astype(BF16)
        h = x_ref[...] + _dot(m16, wo)
        r = lax.rsqrt(_mean(h * h) + EPS)
        hn = h * r
        fg = fg_ref[...]
        err = hn * fg - t_ref[...]
        dy = err * inv_d
        dhn = dy * fg
        dh = r * (dhn - hn * _mean(dhn * hn))
        dh_ref[...] = dh
        dh16 = dh.astype(BF16)
        dm = _dot_nt(dh16, wo)
        for n, out in enumerate((dya_ref, dyb_ref, dyx_ref)):
            out[...] = (sig[n] * dm).astype(BF16)
            dp_ref[:, n * D:(n + 1) * D] = (dm * ys[n] * (sig[n] * (1.0 - sig[n]))).astype(BF16)

        @pl.when(i == 0)
        def _():
            dw_ref[...] = jnp.zeros_like(dw_ref)
            dfg_ref[...] = jnp.zeros_like(dfg_ref)
            sq_ref[...] = jnp.zeros_like(sq_ref)

        dw_ref[0] += _dot_tn(m16, dh16)
        dfg_ref[...] += _fold8(dy * hn)
        sq_ref[...] += _fold8(err * err)

    vec = jax.ShapeDtypeStruct((SUBLANES, D), F32)
    return pl.pallas_call(
        body, name="merge_fwd_bwd", grid=(n_rows // tm,),
        in_specs=[_rows(tm, G_GA), _rows(tm, G_GBB), _rows(tm, G_GX), _rows(tm), _rows(tm), _rows(tm), _rows(tm),
                  _rows(tm), _w_out_spec(3), _const((1, D))],
        out_specs=[_rows(tm), _rows(tm), _rows(tm), _rows(tm), pl.BlockSpec((tm, 3 * D), lambda i: (i, 3)),
                   pl.BlockSpec((1, D, D), lambda i: (3, 0, 0)), _const((SUBLANES, D)), _const((SUBLANES, D))],
        out_shape=[jax.ShapeDtypeStruct((n_rows, D), F32), jax.ShapeDtypeStruct((n_rows, D), BF16),
                   jax.ShapeDtypeStruct((n_rows, D), BF16), jax.ShapeDtypeStruct((n_rows, D), BF16),
                   jax.ShapeDtypeStruct((n_rows, N_GROUPS * D), BF16), jax.ShapeDtypeStruct((4, D, D), F32), vec, vec],
        compiler_params=_params(1),
    )(proj, proj, proj, ya, yb, yx, x, target, wo4_g, final_g)


def _branch_a_bwd(dya, proj, sa16, wo4_g, cw_a, dproj, dw4, tm):
    n_rows = proj.shape[0]
    n_tiles = n_rows // tm

    def body(dya_ref, bp, cp, xp, za, sa_ref, dyan, bpn, zan, cph, xph, w_ref, cw_ref, dp_in, dw_in,
             dp_ref, dw_ref, dwa_ref, e1, e2, o_scr, mm_scr):
        del dp_in, dw_in
        i = pl.program_id(0)
        bpn, zan, cph, xph = (_f32(r) for r in (bpn, zan, cph, xph))
        woa = w_ref[...].reshape(D, D)
        dya16 = dya_ref[...]
        chunks = [pl.ds(r0, ELEM_ROWS) for r0 in range(0, tm, ELEM_ROWS)]
        _to_time_major(e1, 0, jnp.where(i > 0, cph * xph, 0.0))
        for r0, rows in zip(range(0, tm, ELEM_ROWS), chunks):
            _to_time_major(e1, HALO + r0, cp[rows, :].astype(F32) * xp[rows, :].astype(F32))
        _conv(o_scr, e1, cw_ref, FWD_TAPS_A, tm)
        mm_scr[...] = _dot_nt(dya16, woa)
        for r0, rows in zip(range(0, tm, ELEM_ROWS), chunks):
            ca = _from_time_major(o_scr, ELEM_ROWS, r0)
            dsa = mm_scr[rows, :]
            b = bp[rows, :].astype(F32)
            silu_z, dsilu_z = _silu_and_grad(za[rows, :].astype(F32))
            t = dsa * silu_z
            dp_ref[rows, 0 * D:1 * D] = (t * ca).astype(BF16)
            dp_ref[rows, 3 * D:4 * D] = (dsa * (b * ca) * dsilu_z).astype(BF16)
            _to_time_major(e2, r0, t * b)
        dcan = (_dot_nt(dyan[...], woa) * jax.nn.silu(zan)) * bpn
        _to_time_major(e2, tm, jnp.where(i < n_tiles - 1, dcan, 0.0))

        @pl.when(i == 0)
        def _():
            dw_ref[...] = jnp.zeros_like(dw_ref)
            dwa_ref[...] = jnp.zeros_like(dwa_ref)

        _conv_wgrad(dwa_ref, e2, e1, FWD_TAPS_A, tm)
        dw_ref[0] += _dot_tn(sa_ref[...], dya16)
        _conv(o_scr, e2, cw_ref, BWD_TAPS_A, tm)
        for r0, rows in zip(range(0, tm, ELEM_ROWS), chunks):
            dprod = _from_time_major(o_scr, ELEM_ROWS, r0)
            dp_ref[rows, 1 * D:2 * D] = (dprod * xp[rows, :].astype(F32)).astype(BF16)
            dp_ref[rows, 2 * D:3 * D] = (dprod * cp[rows, :].astype(F32)).astype(BF16)

    return pl.pallas_call(
        body, name="branch_a_bwd", grid=(n_tiles,),
        in_specs=[_rows(tm), _rows(tm, G_BA), _rows(tm, G_CA), _rows(tm, G_XA), _rows(tm, G_ZA), _rows(tm),
                  _next_halo(tm, n_rows), _next_halo(tm, n_rows, G_BA), _next_halo(tm, n_rows, G_ZA),
                  _prev_halo(tm, G_CA), _prev_halo(tm, G_XA), _w_out_spec(0), _const(cw_a.shape), ANY, ANY],
        out_specs=[pl.BlockSpec((tm, 4 * D), lambda i: (i, 0)), pl.BlockSpec((1, D, D), lambda i: (0, 0, 0)),
                   _const((K_A * LANE_GROUPS, LANES))],
        out_shape=[jax.ShapeDtypeStruct(dproj.shape, BF16), jax.ShapeDtypeStruct(dw4.shape, F32),
                   jax.ShapeDtypeStruct((K_A * LANE_GROUPS, LANES), F32)],
        input_output_aliases={13: 0, 14: 1},
        scratch_shapes=[_time_major(tm + HALO), _time_major(tm + HALO), _time_major(tm), pltpu.VMEM((tm, D), F32)],
        compiler_params=_params(1),
    )(dya, proj, proj, proj, proj, sa16, dya, proj, proj, proj, proj, wo4_g, cw_a, dproj, dw4)


def _branch_b_bwd(dyb, proj, cb, sb16, wo4_g, cw_b, ln_g, ln_b, dproj, dw4, tm):
    n_rows = proj.shape[0]
    n_tiles = n_rows // tm

    def body(dyb_ref, zb, cb_ref, vb, gb, sb_ref, dybn, zbn, cbn, vbh, gbh, w_ref, cw_ref, lg_ref, lb_ref,
             dp_in, dw_in, dp_ref, dw_ref, dwb_ref, dbb_ref, dlg_ref, dlb_ref, e1, e2, o_scr, mm_scr):
        del dp_in, dw_in
        zbn, vbh, gbh = (_f32(r) for r in (zbn, vbh, gbh))
        i = pl.program_id(0)
        wob = w_ref[...].reshape(D, D)
        lg, lb = lg_ref[...], lb_ref[...]

        def conv_out_grad(dsb, z, c):
            xhat, rstd, ln = _layernorm_parts(c, lg, lb)
            sw, dsw = _silu_and_grad(ln)
            sz, dsz = _silu_and_grad(z)
            dln = (dsb * sz) * dsw
            dxhat = dln * lg
            dcb = rstd * (dxhat - _mean(dxhat) - xhat * _mean(dxhat * xhat))
            return dsb * sw * dsz, dln, xhat, dcb

        @pl.when(i == 0)
        def _():
            dw_ref[...] = jnp.zeros_like(dw_ref)
            dwb_ref[...] = jnp.zeros_like(dwb_ref)
            dbb_ref[...] = jnp.zeros_like(dbb_ref)
            dlg_ref[...] = jnp.zeros_like(dlg_ref)
            dlb_ref[...] = jnp.zeros_like(dlb_ref)

        dyb16 = dyb_ref[...]
        mm_scr[...] = _dot_nt(dyb16, wob)
        dlg, dlb, dbb = (jnp.zeros((SUBLANES, D), F32),) * 3
        for r0 in range(0, tm, ELEM_ROWS):
            rows = pl.ds(r0, ELEM_ROWS)
            dzb, dln, xhat, dcb = conv_out_grad(mm_scr[rows, :], zb[rows, :].astype(F32), cb_ref[rows, :])
            dp_ref[rows, 2 * D:3 * D] = dzb.astype(BF16)
            _to_time_major(e2, r0, dcb)
            dlg, dlb, dbb = dlg + _fold8(dln * xhat), dlb + _fold8(dln), dbb + _fold8(dcb)
        _, _, _, dcbn = conv_out_grad(_dot_nt(dybn[...], wob), zbn[...], cbn[...])
        _to_time_major(e2, tm, jnp.where(i < n_tiles - 1, dcbn, 0.0))
        dlg_ref[...] += dlg
        dlb_ref[...] += dlb
        dbb_ref[...] += dbb
        dw_ref[0] += _dot_tn(sb_ref[...], dyb16)
        _to_time_major(e1, 0, jnp.where(i > 0, vbh[...] * jax.nn.sigmoid(gbh[...]), 0.0))
        for r0 in range(0, tm, ELEM_ROWS):
            rows = pl.ds(r0, ELEM_ROWS)
            glu = vb[rows, :].astype(F32) * jax.nn.sigmoid(gb[rows, :].astype(F32))
            _to_time_major(e1, HALO + r0, glu)
        _conv_wgrad(dwb_ref, e2, e1, FWD_TAPS_B, tm)
        _conv(o_scr, e2, cw_ref, BWD_TAPS_B, tm)
        for r0 in range(0, tm, ELEM_ROWS):
            rows = pl.ds(r0, ELEM_ROWS)
            dglu = _from_time_major(o_scr, ELEM_ROWS, r0)
            sg = jax.nn.sigmoid(gb[rows, :].astype(F32))
            dp_ref[rows, 0 * D:1 * D] = (dglu * sg).astype(BF16)
            dp_ref[rows, 1 * D:2 * D] = (dglu * vb[rows, :].astype(F32) * (sg * (1.0 - sg))).astype(BF16)

    vec = jax.ShapeDtypeStruct((SUBLANES, D), F32)
    return pl.pallas_call(
        body, name="branch_b_bwd", grid=(n_tiles,),
        in_specs=[_rows(tm), _rows(tm, G_ZB), _rows(tm), _rows(tm, G_VB), _rows(tm, G_GB), _rows(tm),
                  _next_halo(tm, n_rows), _next_halo(tm, n_rows, G_ZB), _next_halo(tm, n_rows),
                  _prev_halo(tm, G_VB), _prev_halo(tm, G_GB), _w_out_spec(1), _const(cw_b.shape), _const((1, D)),
                  _const((1, D)), ANY, ANY],
        out_specs=[pl.BlockSpec((tm, 3 * D), lambda i: (i, 2)), pl.BlockSpec((1, D, D), lambda i: (1, 0, 0)),
                   _const((K_B * LANE_GROUPS, LANES)), _const((SUBLANES, D)), _const((SUBLANES, D)),
                   _const((SUBLANES, D))],
        out_shape=[jax.ShapeDtypeStruct(dproj.shape, BF16), jax.ShapeDtypeStruct(dw4.shape, F32),
                   jax.ShapeDtypeStruct((K_B * LANE_GROUPS, LANES), F32), vec, vec, vec],
        input_output_aliases={15: 0, 16: 1},
        scratch_shapes=[_time_major(tm + HALO), _time_major(tm + HALO), _time_major(tm), pltpu.VMEM((tm, D), F32)],
        compiler_params=_params(1),
    )(dyb, proj, cb, proj, proj, sb16, dyb, proj, cb, proj, proj, wo4_g, cw_b, ln_g, ln_b, dproj, dw4)


def _branch_x_bwd(dyx, proj, sx16, kv16, wo4_g, dproj, dw4, tm):
    n_rows = proj.shape[0]
    scale = HEAD_DIM ** -0.5

    def body(dyx_ref, q, zx, sx_ref, kv_ref, w_ref, dp_in, dw_in, dp_ref, dw_ref, dkv_ref):
        del dp_in, dw_in
        i = pl.program_id(0)
        dyx16 = dyx_ref[...]
        q16 = q[...].astype(BF16)
        probs, outs = _attention(q16, kv_ref)
        dsx = _dot_nt(dyx16, w_ref[...].reshape(D, D))
        silu_z, dsilu_z = _silu_and_grad(_f32(zx))
        dp_ref[:, D:2 * D] = (dsx * jnp.concatenate(outs, axis=-1) * dsilu_z).astype(BF16)
        do16 = (dsx * silu_z).astype(BF16)

        @pl.when(i == 0)
        def _():
            dw_ref[...] = jnp.zeros_like(dw_ref)
            dkv_ref[...] = jnp.zeros_like(dkv_ref)

        for h in range(N_HEADS):
            cols = slice(h * HEAD_DIM, (h + 1) * HEAD_DIM)
            p = probs[h]
            dprob = _dot_nt(do16[:, cols], kv_ref[N_HEADS + h])
            ds16 = ((p * (dprob - jnp.sum(p * dprob, axis=-1, keepdims=True))) * scale).astype(BF16)
            dp_ref[:, cols] = _dot(ds16, kv_ref[h]).astype(BF16)
            dkv_ref[h] += _dot_tn(ds16, q16[:, cols])
            dkv_ref[N_HEADS + h] += _dot_tn(p.astype(BF16), do16[:, cols])
        dw_ref[0] += _dot_tn(sx_ref[...], dyx16)

    return pl.pallas_call(
        body, name="branch_x_bwd", grid=(n_rows // tm,),
        in_specs=[_rows(tm), _rows(tm, G_Q), _rows(tm, G_ZX), _rows(tm), _const(kv16.shape), _w_out_spec(2), ANY, ANY],
        out_specs=[pl.BlockSpec((tm, 2 * D), lambda i: (i, 2)), pl.BlockSpec((1, D, D), lambda i: (2, 0, 0)),
                   _const(kv16.shape)],
        out_shape=[jax.ShapeDtypeStruct(dproj.shape, BF16), jax.ShapeDtypeStruct(dw4.shape, F32),
                   jax.ShapeDtypeStruct(kv16.shape, F32)],
        input_output_aliases={6: 0, 7: 1},
        compiler_params=_params(1),
    )(dyx, proj, proj, sx16, kv16, wo4_g, dproj, dw4)


def _dp_unit(u):
    g = u // 2
    pos = jnp.where(g < G_VB, g, jnp.where(g < G_Q, g + 2, jnp.where(g < G_GA, g - 3, g)))
    return 2 * pos + u % 2


def _scatter_copies(srcs, lands, send, recv):
    x, y, c = lax.axis_index("x"), lax.axis_index("y"), lax.axis_index("c")
    copies = []
    for n in range(N_DEV - 1):
        flip = n + 1
        px = 1 - x if flip & 4 else x
        py = 1 - y if flip & 2 else y
        pc = 1 - c if flip & 1 else c
        for t, (src, land) in enumerate(zip(srcs, lands)):
            copies.append(pltpu.make_async_remote_copy(
                src_ref=src.at[4 * px + 2 * py + pc], dst_ref=land.at[n], send_sem=send.at[t * (N_DEV - 1) + n],
                recv_sem=recv.at[t * (N_DEV - 1) + n], device_id=(px, py, pc), device_id_type=MESH))
    return copies


def _scatter_start(name, arrays, views, n_views):
    n = len(arrays)
    lands = [lax.empty(tuple(N_DEV - 1 if d == N_DEV else d for d in a.shape), a.dtype) for a in arrays]

    def body(*refs):
        src, land, (send, recv) = refs[:n], refs[n:2 * n], refs[2 * n:2 * n + 2]
        token = refs[-1]
        for cp in _scatter_copies(views(src), views(land), send, recv):
            cp.start()
        token[...] = jnp.zeros_like(token)

    sems = pltpu.SemaphoreType.DMA((n_views * (N_DEV - 1),))
    out = pl.pallas_call(
        body, name=name,
        in_specs=[HBM] * (2 * n),
        out_specs=[SEM, SEM] + [HBM] * (2 * n) + [pl.BlockSpec(memory_space=pltpu.VMEM)],
        out_shape=[sems, sems] + [pltpu.HBM(a.shape, a.dtype) for a in arrays + lands]
        + [jax.ShapeDtypeStruct((SUBLANES, LANES), F32)],
        input_output_aliases={k: 2 + k for k in range(2 * n)},
        compiler_params=pltpu.CompilerParams(has_side_effects=SIDE_EFFECT),
    )(*[pltpu.with_memory_space_constraint(a, pltpu.HBM) for a in arrays + lands])
    return dict(name=name, sems=out[:2], moving=out[2:2 + 2 * n], views=views, token=out[-1])


def _scatter_wait(started, after):
    n = len(started["moving"]) // 2
    views = started["views"]

    def body(*refs):
        src, land, (send, recv) = refs[:n], refs[n:2 * n], refs[2 * n:2 * n + 2]
        for cp in _scatter_copies(views(src), views(land), send, recv):
            cp.wait_send()
            cp.wait_recv()

    out = pl.pallas_call(
        body, name=started["name"].replace("start", "wait"),
        in_specs=[HBM] * (2 * n) + [SEM, SEM, ANY],
        out_specs=[HBM] * (2 * n),
        out_shape=[pltpu.HBM(a.shape, a.dtype) for a in started["moving"]],
        input_output_aliases={k: k for k in range(2 * n)},
        compiler_params=pltpu.CompilerParams(has_side_effects=SIDE_EFFECT),
    )(*started["moving"], *started["sems"], after)
    return out[n:]


def _w_in_grad(ut, dproj, tk, token):
    n_rows = dproj.shape[0]
    n_k = n_rows // tk
    per_shard = W_IN_SHARD // UNIT

    def body(ut_ref, dp0, dp1, dp2, token_ref, out_ref, out16_ref, acc):
        del token_ref
        t = pl.program_id(1)

        for r, dp_ref in enumerate((dp0, dp1, dp2)):
            cols = slice(r * UNIT, (r + 1) * UNIT)

            @pl.when(t == 0)
            def _(dp_ref=dp_ref, cols=cols):
                acc[:, cols] = _dot(ut_ref[...], dp_ref[...])

            @pl.when(t > 0)
            def _(dp_ref=dp_ref, cols=cols):
                acc[:, cols] += _dot(ut_ref[...], dp_ref[...])

        @pl.when(t == n_k - 1)
        def _():
            out_ref[0] = acc[...]
            out16_ref[0] = acc[...].astype(BF16)

    def dp_spec(r):
        return pl.BlockSpec((tk, UNIT), lambda q, t: (t, _dp_unit(per_shard * q + r)))

    shard = pl.BlockSpec((1, D, W_IN_SHARD), lambda q, t: (q, 0, 0))
    return pl.pallas_call(
        body, name="w_in_grad", grid=(N_DEV, n_k),
        in_specs=[pl.BlockSpec((D, tk), lambda q, t: (0, t)), dp_spec(0), dp_spec(1), dp_spec(2), ANY],
        out_specs=[shard, shard],
        out_shape=[jax.ShapeDtypeStruct((N_DEV, D, W_IN_SHARD), F32), jax.ShapeDtypeStruct((N_DEV, D, W_IN_SHARD), BF16)],
        scratch_shapes=[pltpu.VMEM((D, W_IN_SHARD), F32)],
        compiler_params=_params(2),
    )(ut, dproj, dproj, dproj, token)


def _x_grad(dproj, win_t, x, dh, norm_g, token, tm):
    n_rows = x.shape[0]
    n_k = N_GROUPS * D // X_GRAD_K

    def body(dp_ref, wt_ref, x_ref, dh_ref, g_ref, token_ref, gx_ref, dg_ref, acc):
        del token_ref
        i, g = pl.program_id(0), pl.program_id(1)

        @pl.when((i == 0) & (g == 0))
        def _():
            dg_ref[...] = jnp.zeros_like(dg_ref)

        @pl.when(g == 0)
        def _():
            acc[...] = _dot(dp_ref[...], wt_ref[...])

        @pl.when(g > 0)
        def _():
            acc[...] += _dot(dp_ref[...], wt_ref[...])

        @pl.when(g == n_k - 1)
        def _():
            du = acc[...]
            xf = x_ref[...]
            r = lax.rsqrt(_mean(xf * xf) + EPS)
            xn = xf * r
            dun = du * g_ref[...]
            gx_ref[...] = dh_ref[...] + r * (dun - xn * _mean(dun * xn))
            dg_ref[...] += _fold8(du * xn)

    return pl.pallas_call(
        body, name="x_grad", grid=(n_rows // tm, n_k),
        in_specs=[pl.BlockSpec((tm, X_GRAD_K), lambda i, g: (i, g)), pl.BlockSpec((X_GRAD_K, D), lambda i, g: (g, 0)),
                  pl.BlockSpec((tm, D), lambda i, g: (i, 0)), pl.BlockSpec((tm, D), lambda i, g: (i, 0)),
                  _const((1, D)), ANY],
        out_specs=[pl.BlockSpec((tm, D), lambda i, g: (i, 0)), _const((SUBLANES, D))],
        out_shape=[jax.ShapeDtypeStruct((n_rows, D), F32), jax.ShapeDtypeStruct((SUBLANES, D), F32)],
        scratch_shapes=[pltpu.VMEM((tm, D), F32)],
        compiler_params=_params(2),
    )(dproj, win_t, x, dh, norm_g, token)


def _local_step(x, mem, target, norm_g, conv_b_b, ln_g, ln_b, mem_g, final_g, shards):
    n_rows = x.shape[0]
    tm = min(512, n_rows)
    big = min(1024, n_rows)
    u16, ut = _rmsnorm_fwd(x, norm_g, big)
    proj, win_t, _, wkv_g, wo4_g, cw_g = _proj_fwd_gather(u16, shards, min(2048, n_rows))
    cw_rows = cw_g.transpose(1, 0, 2).reshape((SUBLANES + HALO) * LANE_GROUPS, LANES)
    cw_a, cw_b = cw_rows[:SUBLANES * LANE_GROUPS], cw_rows[SUBLANES * LANE_GROUPS:]
    kv16, mn16 = _kv_fwd(mem, mem_g, wkv_g)
    sa16, ya = _branch_a_fwd(proj, wo4_g, cw_a, tm)
    cb, sb16, yb = _branch_b_fwd(proj, wo4_g, cw_b, conv_b_b, ln_g, ln_b, tm)
    sx16, yx = _branch_x_fwd(proj, kv16, wo4_g, tm)
    dh, dya, dyb, dyx, dproj, dw4, dfg, sq = _merge_fwd_bwd(proj, ya, yb, yx, x, target, wo4_g, final_g,
                                                             min(256, n_rows))
    dproj, dw4, dwa = _branch_a_bwd(dya, proj, sa16, wo4_g, cw_a, dproj, dw4, tm)
    dproj, dw4, dwb, dbb, dlg, dlb = _branch_b_bwd(dyb, proj, cb, sb16, wo4_g, cw_b, ln_g, ln_b, dproj, dw4, tm)
    dproj, dw4, dkv = _branch_x_bwd(dyx, proj, sx16, kv16, wo4_g, dproj, dw4, tm)
    dwkv_g, dwkv16, dmg = _kv_bwd(dkv, mem, mem_g, mn16, wkv_g)
    dw4 = dw4.reshape(4, N_DEV, D // N_DEV, D)
    small_moving = _scatter_start("small_grads_start", [dw4.astype(BF16), dwkv16],
                                  lambda refs: [refs[0].at[w] for w in range(4)] + [refs[1]], 5)
    dwin_g, dwin16 = _w_in_grad(ut, dproj, min(2048, n_rows), small_moving["token"])
    w_in_moving = _scatter_start("w_in_grad_start", [dwin16], lambda refs: list(refs), 1)
    gx, dng = _x_grad(dproj, win_t, x, dh, norm_g, w_in_moving["token"], big)
    land4, landkv = _scatter_wait(small_moving, dng)
    landin, = _scatter_wait(w_in_moving, dng)
    small = {SV_NORM_G: dng, SV_CONV_B_B: dbb, SV_LN_G: dlg, SV_LN_B: dlb, SV_MEM_G: dmg, SV_FINAL_G: dfg, SV_LOSS: sq}
    grads = [(dwin_g[None], landin[None]), (dw4, land4), (dwkv_g[None], landkv[None])]
    return gx, grads, small, dwa.reshape(K_A, D), dwb.reshape(K_B, D)


def _allgather_small(small, conv_rows):
    keys = sorted(small)

    def body(*refs):
        parts, (conv_ref, out_ref, mine, send, recv) = refs[:len(keys)], refs[len(keys):]
        x, y, c, chips = _place()
        me, sibling = 4 * x + 2 * y + c, (x, y, 1 - c)
        mine[pl.ds(0, SV_CONV_A), :] = jnp.zeros((SV_CONV_A, D), F32)
        for key, part in zip(keys, parts):
            mine[key:key + 1, :] = jnp.sum(part[...], axis=0, keepdims=True)
        mine[pl.ds(SV_CONV_A, SV_ROWS - SV_CONV_A), :] = conv_ref[...]
        out_ref[me] = mine[...]

        def copy(k, block, to, from_mine=False):
            return pltpu.make_async_remote_copy(
                src_ref=mine if from_mine else out_ref.at[block], dst_ref=out_ref.at[block],
                send_sem=send.at[k], recv_sem=recv.at[k], device_id=to, device_id_type=MESH)

        first = [copy(0, me, sibling, from_mine=True)]
        first += [copy(1 + j, me, (*chip, c), from_mine=True) for j, chip in enumerate(chips)]
        for cp in first:
            cp.start()
        passed = []
        for j, (px, py) in enumerate(chips):
            block = 4 * px + 2 * py + c
            copy(1 + j, block, sibling).wait_recv()
            passed.append(copy(4 + j, block, sibling))
            passed[-1].start()
        copy(0, 4 * x + 2 * y + 1 - c, sibling).wait_recv()
        for j, (px, py) in enumerate(chips):
            copy(4 + j, 4 * px + 2 * py + 1 - c, sibling).wait_recv()
        for cp in first + passed:
            cp.wait_send()

    vmem = pl.BlockSpec(memory_space=pltpu.VMEM)
    return pl.pallas_call(
        body, name="allgather_small",
        in_specs=[vmem] * (len(keys) + 1), out_specs=vmem,
        out_shape=jax.ShapeDtypeStruct((N_DEV, SV_ROWS, D), F32),
        scratch_shapes=[pltpu.VMEM((SV_ROWS, D), F32), pltpu.SemaphoreType.DMA((7,)), pltpu.SemaphoreType.DMA((7,))],
    )(*[small[k] for k in keys], conv_rows)


def _adamw(w, g, m, v):
    m = ADAM_B1 * m + (1.0 - ADAM_B1) * g
    v = ADAM_B2 * v + (1.0 - ADAM_B2) * (g * g)
    m_hat = m / (1.0 - ADAM_B1 ** ADAM_STEP)
    v_hat = v / (1.0 - ADAM_B2 ** ADAM_STEP)
    return -ADAM_LR * (m_hat / (jnp.sqrt(v_hat) + ADAM_EPS) + ADAM_WD * w), m, v


def _adamw_shard(own, landed, piece, k_arr, w, m, v, tr):
    n_r, n_c = w.shape
    n_landed = landed.shape[1]

    def body(k_ref, own_ref, *refs):
        del k_ref
        landed_refs, (w_ref, m_ref, v_ref, g_out, d_out, m_out, v_out) = refs[:n_landed], refs[n_landed:]
        g = own_ref[0, 0]
        for landed_ref in landed_refs:
            g = g + landed_ref[0, 0].astype(F32)
        g_out[...] = g
        d_out[...], m_out[...], v_out[...] = _adamw(w_ref[...], g, m_ref[...], v_ref[...])

    blk = (1, 1, tr, n_c)
    flat = pl.BlockSpec((tr, n_c), lambda r, k: (r, 0))
    return pl.pallas_call(
        body, name="adamw_shard",
        grid_spec=pltpu.PrefetchScalarGridSpec(
            num_scalar_prefetch=1, grid=(n_r // tr,),
            in_specs=[pl.BlockSpec(blk, lambda r, k: (piece, k[0], r, 0))]
            + [pl.BlockSpec(blk, functools.partial(lambda r, k, j: (piece, j, r, 0), j=j)) for j in range(n_landed)]
            + [flat] * 3,
            out_specs=[flat] * 4),
        out_shape=[jax.ShapeDtypeStruct((n_r, n_c), F32)] * 4,
        compiler_params=_params(1),
    )(k_arr, own, *([landed] * n_landed), w, m, v)


def _adamw_shards(entries, k_arr):
    n = len(entries)

    def body(k_ref, *refs):
        del k_ref
        ins, outs = refs[:5 * n], refs[5 * n:]
        for e in range(n):
            own_ref, landed_ref, w_ref, m_ref, v_ref = ins[5 * e:5 * e + 5]
            g = own_ref[0, 0]
            for j in range(landed_ref.shape[1]):
                g = g + landed_ref[0, j].astype(F32)
            g_out, d_out, m_out, v_out = outs[4 * e:4 * e + 4]
            g_out[...] = g
            d_out[...], m_out[...], v_out[...] = _adamw(w_ref[...], g, m_ref[...], v_ref[...])

    in_specs, operands, out_specs, out_shape = [], [], [], []
    for own, landed, piece, w, m, v in entries:
        flat = pl.BlockSpec(w.shape, lambda i, k: (0, 0))
        in_specs += [pl.BlockSpec((1, 1, *w.shape), functools.partial(lambda i, k, p: (p, k[0], 0, 0), p=piece)),
                     pl.BlockSpec((1, *landed.shape[1:]), functools.partial(lambda i, k, p: (p, 0, 0, 0), p=piece)),
                     flat, flat, flat]
        operands += [own, landed, w, m, v]
        out_specs += [flat] * 4
        out_shape += [jax.ShapeDtypeStruct(w.shape, F32)] * 4
    out = pl.pallas_call(
        body, name="adamw_shards",
        grid_spec=pltpu.PrefetchScalarGridSpec(num_scalar_prefetch=1, grid=(1,), in_specs=in_specs,
                                               out_specs=out_specs),
        out_shape=out_shape,
        compiler_params=_params(1),
    )(k_arr, *operands)
    return [tuple(out[4 * e:4 * e + 4]) for e in range(n)]


def _adamw_small(gathered, k_arr, vectors, conv_a, conv_b):
    n_vec = len(vectors)
    cols = D // N_DEV

    def body(k_ref, full_ref, cols_ref, *refs):
        del k_ref
        ins, outs = refs[:3 * (n_vec + 2)], refs[3 * (n_vec + 2):-2]
        tot, tot_cols = refs[-2:]
        tot[...] = full_ref[0]
        tot_cols[...] = cols_ref[0]
        for dev in range(1, N_DEV):
            tot[...] += full_ref[dev]
            tot_cols[...] += cols_ref[dev]
        loss = (0.5 / D) * jnp.sum(tot[SV_LOSS:SV_LOSS + 1, :])
        outs[0][...] = jnp.full(outs[0].shape, loss, F32)
        grads = [tot[n:n + 1, :] for n in range(n_vec)]
        grads += [tot_cols[pl.ds(SV_CONV_A, K_A), :], tot_cols[pl.ds(SV_CONV_B, K_B), :]]
        for n, g in enumerate(grads):
            w_ref, m_ref, v_ref = ins[3 * n:3 * n + 3]
            g_out, d_out, m_out, v_out = outs[1 + 4 * n:5 + 4 * n]
            g_out[...] = g
            d_out[...], m_out[...], v_out[...] = _adamw(w_ref[...], g, m_ref[...], v_ref[...])

    weights = list(vectors) + [conv_a, conv_b]
    flat_in = [a for wmv in weights for a in wmv]
    out_shape = [jax.ShapeDtypeStruct((SUBLANES, 128), F32)]
    for wmv in weights:
        out_shape += [jax.ShapeDtypeStruct(wmv[0].shape, F32)] * 4
    return pl.pallas_call(
        body, name="adamw_small",
        grid_spec=pltpu.PrefetchScalarGridSpec(
            num_scalar_prefetch=1, grid=(1,),
            in_specs=[pl.BlockSpec((N_DEV, SV_ROWS, D), lambda i, k: (0, 0, 0)),
                      pl.BlockSpec((N_DEV, SV_ROWS, cols), lambda i, k: (0, 0, k[0]))]
            + [pl.BlockSpec(a.shape, lambda i, k: (0, 0)) for a in flat_in],
            out_specs=[pl.BlockSpec(s.shape, lambda i, k: (0, 0)) for s in out_shape],
            scratch_shapes=[pltpu.VMEM((SV_ROWS, D), F32), pltpu.VMEM((SV_ROWS, cols), F32)]),
        out_shape=out_shape,
        compiler_params=_params(1),
    )(k_arr, gathered, gathered, *flat_in)


def kernel(x, mem, norm_g, w_in, conv_a_w, w_out_a, conv_b_w, conv_b_b, ln_b_g, ln_b_b, w_out_b, mem_norm_g, w_kv, w_out_x, w_o, final_g, loss_target, m_norm_g, m_w_in, m_conv_a_w, m_w_out_a, m_conv_b_w, m_conv_b_b, m_ln_b_g, m_ln_b_b, m_w_out_b, m_mem_norm_g, m_w_kv, m_w_out_x, m_w_o, m_final_g, v_norm_g, v_w_in, v_conv_a_w, v_w_out_a, v_conv_b_w, v_conv_b_b, v_ln_b_g, v_ln_b_b, v_w_out_b, v_mem_norm_g, v_w_kv, v_w_out_x, v_w_o, v_final_g):
    xi, yi, ci = lax.axis_index("x"), lax.axis_index("y"), lax.axis_index("c")
    k_arr = jnp.reshape(4 * xi + 2 * yi + ci, (1,)).astype(jnp.int32)

    cw = jnp.concatenate([jnp.pad(conv_a_w[0], ((0, SUBLANES - K_A), (0, 0))),
                          jnp.pad(conv_b_w[0], ((0, HALO - K_B), (0, 0)))], axis=0)
    wo4 = jnp.stack([w_out_a[0], w_out_b[0], w_out_x[0], w_o[0]]).astype(BF16)
    shards = [w_in[0].astype(BF16), w_kv[0].astype(BF16), wo4, cw]

    final_g2 = final_g.reshape(1, D)
    gx, grads, small, dwa, dwb = _local_step(
        x[0], mem[0], loss_target[0], norm_g, conv_b_b, ln_b_g, ln_b_b, mem_norm_g, final_g2, shards)

    conv_rows = jnp.concatenate([jnp.pad(dwa, ((0, SUBLANES - K_A), (0, 0))),
                                 jnp.pad(dwb, ((0, HALO - K_B), (0, 0)))], axis=0)
    gathered_small = _allgather_small(small, conv_rows)

    res = {"w_in": _adamw_shard(grads[0][0], grads[0][1], 0, k_arr, w_in[0], m_w_in[0], v_w_in[0], 256)}
    small_shards = [("w_out_a", 1, 0, w_out_a, m_w_out_a, v_w_out_a), ("w_out_b", 1, 1, w_out_b, m_w_out_b, v_w_out_b),
                    ("w_out_x", 1, 2, w_out_x, m_w_out_x, v_w_out_x), ("w_o", 1, 3, w_o, m_w_o, v_w_o),
                    ("w_kv", 2, 0, w_kv, m_w_kv, v_w_kv)]
    updated = _adamw_shards([(grads[a][0], grads[a][1], l, w[0], m[0], v[0]) for _, a, l, w, m, v in small_shards],
                            k_arr)
    res.update({name: four for (name, *_), four in zip(small_shards, updated)})
    res = {name: tuple(r[None] for r in four) for name, four in res.items()}
    vectors = [(norm_g, m_norm_g, v_norm_g), (conv_b_b, m_conv_b_b, v_conv_b_b), (ln_b_g, m_ln_b_g, v_ln_b_g),
               (ln_b_b, m_ln_b_b, v_ln_b_b), (mem_norm_g, m_mem_norm_g, v_mem_norm_g),
               (final_g2, m_final_g.reshape(1, D), v_final_g.reshape(1, D))]
    out = _adamw_small(gathered_small, k_arr, vectors, (conv_a_w[0], m_conv_a_w[0], v_conv_a_w[0]),
                       (conv_b_w[0], m_conv_b_w[0], v_conv_b_w[0]))
    loss = out[0][0, 0]
    names = ["norm_g", "conv_b_b", "ln_b_g", "ln_b_b", "mem_norm_g", "final_g", "conv_a_w", "conv_b_w"]
    for n, name in enumerate(names):
        four = out[1 + 4 * n:5 + 4 * n]
        if name == "final_g":
            four = [r.reshape(D) for r in four]
        elif name.startswith("conv_") and name.endswith("_w"):
            four = [r[None] for r in four]
        res[name] = tuple(four)

    order = ["norm_g", "w_in", "conv_a_w", "w_out_a", "conv_b_w", "conv_b_b", "ln_b_g", "ln_b_b", "w_out_b",
             "mem_norm_g", "w_kv", "w_out_x", "w_o", "final_g"]
    return (loss, gx[None], *[res[n][0] for n in order], *[res[n][1] for n in order],
            *[res[n][2] for n in order], *[res[n][3] for n in order])
```

```python
import functools

import jax
import jax.numpy as jnp
from jax import lax
from jax.experimental import pallas as pl
from jax.experimental.pallas import tpu as pltpu

F32, BF16 = jnp.float32, jnp.bfloat16
D = 1024
N_DEV = 8
N_HEADS = 4
HEAD_DIM = D // N_HEADS
N_GROUPS = 12
W_IN_SHARD = N_GROUPS * D // N_DEV
UNIT = 512
X_GRAD_K = 2 * D
K_A, K_B = 3, 31
EPS = 1e-6
HALO = 32
SUBLANES = 8
LANES = 128
LANE_GROUPS = D // LANES
TAP_GROUP = 16
CONV_BLOCK = 32
ELEM_ROWS = 16
CONV_PARTIAL_SUMS = 4
VMEM_LIMIT = 56 << 20
MESH = pl.DeviceIdType.MESH
ANY = pl.BlockSpec(memory_space=pl.ANY)
HBM = pl.BlockSpec(memory_space=pltpu.HBM)
SEM = pl.BlockSpec(memory_space=pltpu.SEMAPHORE)
SIDE_EFFECT = pltpu.SideEffectType.DATAFLOW_SIDE_EFFECTING

G_BA, G_CA, G_XA, G_ZA, G_VB, G_GB, G_ZB, G_Q, G_ZX, G_GA, G_GBB, G_GX = range(N_GROUPS)

ADAM_LR, ADAM_B1, ADAM_B2, ADAM_EPS, ADAM_WD, ADAM_STEP = 0.001, 0.9, 0.999, 1e-08, 0.01, 10

SV_NORM_G, SV_CONV_B_B, SV_LN_G, SV_LN_B, SV_MEM_G, SV_FINAL_G, SV_LOSS = range(7)
SV_CONV_A, SV_CONV_B, SV_ROWS = 8, 16, 48


def _dot(a, b):
    return jnp.dot(a, b, preferred_element_type=F32)


def _dot_nt(a, b):
    return lax.dot_general(a, b, (((1,), (1,)), ((), ())), preferred_element_type=F32)


def _dot_tn(a, b):
    return lax.dot_general(a, b, (((0,), (0,)), ((), ())), preferred_element_type=F32)


def _silu_and_grad(z):
    s = jax.nn.sigmoid(z)
    return z * s, s * (1.0 + z * (1.0 - s))


def _fold8(a):
    return a.reshape(a.shape[0] // SUBLANES, SUBLANES, a.shape[1]).sum(axis=0)


def _mean(a):
    return jnp.mean(a, axis=-1, keepdims=True)


def _f32(ref):
    return ref[...].astype(F32)


def _params(n_grid):
    return pltpu.CompilerParams(dimension_semantics=("arbitrary",) * n_grid, vmem_limit_bytes=VMEM_LIMIT)


def _rows(tm, col=0):
    return pl.BlockSpec((tm, D), lambda i: (i, col))


def _prev_halo(tm, col=0):
    return pl.BlockSpec((HALO, D), lambda i: (jnp.maximum(i * (tm // HALO) - 1, 0), col))


def _next_halo(tm, n_rows, col=0):
    last = n_rows // HALO - 1
    return pl.BlockSpec((HALO, D), lambda i: (jnp.minimum((i + 1) * (tm // HALO), last), col))


def _const(shape):
    return pl.BlockSpec(shape, lambda *_: (0,) * len(shape))


def _w_out_spec(which):
    return pl.BlockSpec((N_DEV, None, D // N_DEV, D), lambda *_: (0, which, 0, 0))


def _to_time_major(t_ref, row0, x):
    n = x.shape[0]
    for j in range(LANE_GROUPS):
        t_ref[pl.ds(row0 * LANE_GROUPS + j, n, stride=LANE_GROUPS), :] = x[:, j * LANES:(j + 1) * LANES]


def _from_time_major(t_ref, n, row0=0):
    return jnp.concatenate([t_ref[pl.ds(row0 * LANE_GROUPS + j, n, stride=LANE_GROUPS), :]
                            for j in range(LANE_GROUPS)], axis=-1)


def _row(ref, t):
    start = t * LANE_GROUPS
    if not isinstance(start, int):
        start = pl.multiple_of(start, LANE_GROUPS)
    return ref[pl.ds(start, LANE_GROUPS), :]


def _tap_groups(taps):
    return [taps[first:first + TAP_GROUP] for first in range(0, len(taps), TAP_GROUP)]


def _conv(o_ref, e_ref, w_ref, taps, n_rows, bias_ref=None):
    for n_group, group in enumerate(_tap_groups(taps)):
        weights = [_row(w_ref, k) for k, _ in group]

        def block(c, carry, n_group=n_group, group=group, weights=weights):
            t0 = c * CONV_BLOCK
            window = {}
            for t in range(CONV_BLOCK):
                parts = [None] * min(CONV_PARTIAL_SUMS, len(group))
                for n, (_, off) in enumerate(group):
                    if t + off not in window:
                        window[t + off] = _row(e_ref, t0 + t + off)
                    term = weights[n] * window[t + off]
                    parts[n % len(parts)] = term if parts[n % len(parts)] is None else parts[n % len(parts)] + term
                window.pop(t + min(off for _, off in group), None)
                while len(parts) > 1:
                    parts = [parts[n] + parts[n + 1] for n in range(0, len(parts) - 1, 2)] + parts[len(parts) & ~1:]
                out = parts[0]
                if n_group > 0:
                    out = out + _row(o_ref, t0 + t)
                elif bias_ref is not None:
                    out = out + bias_ref[...]
                o_ref[pl.ds(pl.multiple_of((t0 + t) * LANE_GROUPS, LANE_GROUPS), LANE_GROUPS), :] = out
            return carry

        lax.fori_loop(0, n_rows // CONV_BLOCK, block, 0)


def _conv_wgrad(dw_ref, d_ref, e_ref, taps, n_rows):
    for group in _tap_groups(taps):
        def block(c, sums, group=group):
            t0 = c * CONV_BLOCK
            sums = list(sums)
            window = {}
            for t in range(CONV_BLOCK):
                d = _row(d_ref, t0 + t)
                for n, (_, off) in enumerate(group):
                    if t + off not in window:
                        window[t + off] = _row(e_ref, t0 + t + off)
                    sums[n] = sums[n] + d * window[t + off]
                window.pop(t + min(off for _, off in group), None)
            return tuple(sums)

        sums = lax.fori_loop(0, n_rows // CONV_BLOCK, block, tuple(_row(dw_ref, k) for k, _ in group))
        for (k, _), total in zip(group, sums):
            dw_ref[pl.ds(k * LANE_GROUPS, LANE_GROUPS), :] = total


FWD_TAPS_A = [(k, HALO - (K_A - 1) + k) for k in range(K_A)]
BWD_TAPS_A = [(k, K_A - 1 - k) for k in range(K_A)]
FWD_TAPS_B = [(k, HALO - (K_B - 1) + k) for k in range(K_B)]
BWD_TAPS_B = [(k, K_B - 1 - k) for k in range(K_B)]


def _time_major(n_rows):
    return pltpu.VMEM((n_rows * LANE_GROUPS, LANES), F32)


def _kv_fwd(mem, mem_g, wkv_g):
    m_len = mem.shape[0]

    def body(mem_ref, g_ref, w_ref, kv_ref, mn_ref):
        mf = mem_ref[...]
        r = lax.rsqrt(_mean(mf * mf) + EPS)
        mn = ((mf * r) * g_ref[...]).astype(BF16)
        mn_ref[...] = mn
        for b in range(2 * N_HEADS):
            kv_ref[b] = _dot(mn, w_ref[b]).astype(BF16)

    return pl.pallas_call(
        body, name="kv_fwd", grid=(1,),
        in_specs=[_const((m_len, D)), _const((1, D)), _const((2 * N_HEADS, D, HEAD_DIM))],
        out_specs=[_const((2 * N_HEADS, m_len, HEAD_DIM)), _const((m_len, D))],
        out_shape=[jax.ShapeDtypeStruct((2 * N_HEADS, m_len, HEAD_DIM), BF16), jax.ShapeDtypeStruct((m_len, D), BF16)],
        compiler_params=_params(1),
    )(mem, mem_g, wkv_g)


def _kv_bwd(dkv, mem, mem_g, mn16, wkv_g):
    m_len = mem.shape[0]

    def body(dkv_ref, mem_ref, g_ref, mn_ref, w_ref, dw_ref, dw16_ref, dg_ref):
        mn = mn_ref[...]
        dmn = jnp.zeros((m_len, D), F32)
        for b in range(2 * N_HEADS):
            d16 = dkv_ref[b].astype(BF16)
            dw = _dot_tn(mn, d16)
            dw_ref[b] = dw
            dw16_ref[b] = dw.astype(BF16)
            dmn = dmn + _dot_nt(d16, w_ref[b])
        mf = mem_ref[...]
        r = lax.rsqrt(_mean(mf * mf) + EPS)
        dg_ref[...] = _fold8(dmn * (mf * r))

    return pl.pallas_call(
        body, name="kv_bwd", grid=(1,),
        in_specs=[_const((2 * N_HEADS, m_len, HEAD_DIM)), _const((m_len, D)), _const((1, D)), _const((m_len, D)),
                  _const((2 * N_HEADS, D, HEAD_DIM))],
        out_specs=[_const((2 * N_HEADS, D, HEAD_DIM)), _const((2 * N_HEADS, D, HEAD_DIM)), _const((SUBLANES, D))],
        out_shape=[jax.ShapeDtypeStruct((2 * N_HEADS, D, HEAD_DIM), F32),
                   jax.ShapeDtypeStruct((2 * N_HEADS, D, HEAD_DIM), BF16), jax.ShapeDtypeStruct((SUBLANES, D), F32)],
        compiler_params=_params(1),
    )(dkv, mem, mem_g, mn16, wkv_g)


def _rmsnorm_fwd(x, norm_g, tm):
    n_rows = x.shape[0]

    def body(x_ref, g_ref, u_ref, ut_ref):
        xf = x_ref[...]
        u = (xf * lax.rsqrt(_mean(xf * xf) + EPS)) * g_ref[...]
        u_ref[...] = u.astype(BF16)
        ut_ref[...] = u.T.astype(BF16)

    return pl.pallas_call(
        body, name="rmsnorm_fwd", grid=(n_rows // tm,),
        in_specs=[_rows(tm), _const((1, D))],
        out_specs=[_rows(tm), pl.BlockSpec((D, tm), lambda i: (0, i))],
        out_shape=[jax.ShapeDtypeStruct((n_rows, D), BF16), jax.ShapeDtypeStruct((D, n_rows), BF16)],
        compiler_params=_params(1),
    )(x, norm_g)


def _place():
    x, y, c = lax.axis_index("x"), lax.axis_index("y"), lax.axis_index("c")
    other_chips = [(1 - x, y), (x, 1 - y), (1 - x, 1 - y)]
    return x, y, c, other_chips


def _arrival_order():
    x, y, c, chips = _place()
    order = [4 * x + 2 * y + c, 4 * x + 2 * y + 1 - c]
    for px, py in chips:
        order += [4 * px + 2 * py + c, 4 * px + 2 * py + 1 - c]
    return order


def _proj_fwd_gather(u16, blocks, tm):
    n = len(blocks)
    n_rows = u16.shape[0]
    n_i = n_rows // tm
    per_shard = W_IN_SHARD // UNIT
    assert n_i >= per_shard

    def wt_index(p, i, order):
        return (_dp_unit(per_shard * order[p] + jnp.minimum(i, per_shard - 1)), 0)

    def body(order_ref, u_ref, *refs):
        src, proj_ref, wt_ref, out = refs[:n], refs[n], refs[n + 1], refs[n + 2:2 * n + 2]
        wbuf, stage_sem, send, recv, own_sem = refs[2 * n + 2:]
        p, i = pl.program_id(0), pl.program_id(1)
        x, y, c, chips = _place()
        me, sibling = 4 * x + 2 * y + c, (x, y, 1 - c)

        def copy(t, k, block, to, from_input=False):
            return pltpu.make_async_remote_copy(
                src_ref=src[t] if from_input else out[t].at[block], dst_ref=out[t].at[block],
                send_sem=send.at[t, k], recv_sem=recv.at[t, k], device_id=to, device_id_type=MESH)

        def own_copies():
            return [pltpu.make_async_copy(src[t], out[t].at[me], own_sem.at[t]) for t in range(n)]

        def first_copies():
            first = []
            for t in range(n):
                first.append(copy(t, 0, me, sibling, from_input=True))
                first += [copy(t, 1 + j, me, (*chip, c), from_input=True) for j, chip in enumerate(chips)]
            return first

        def stage(slot, block):
            return pltpu.make_async_copy(out[0].at[block], wbuf.at[slot], stage_sem.at[slot])

        @pl.when((p == 0) & (i == 0))
        def _():
            for cp in own_copies() + first_copies():
                cp.start()
            mine = pltpu.make_async_copy(src[0], wbuf.at[0], stage_sem.at[0])
            mine.start()
            mine.wait()

        @pl.when((p > 0) & (i == 0))
        def _():
            stage(p % 2, order_ref[p]).wait()

        proj_ref[...] = _dot(u_ref[...], wbuf[p % 2])
        for r in range(per_shard):
            @pl.when(i == r)
            def _(r=r):
                wt_ref[...] = wbuf[p % 2, :, r * UNIT:(r + 1) * UNIT].astype(F32).T.astype(BF16)

        for nxt in range(1, N_DEV):
            @pl.when((p == nxt - 1) & (i == n_i - 1))
            def _(nxt=nxt):
                if nxt == 1:
                    block = 4 * x + 2 * y + 1 - c
                    copy(0, 0, block, sibling).wait_recv()
                else:
                    j, passed_on = divmod(nxt - 2, 2)
                    px, py = chips[j]
                    if passed_on:
                        block = 4 * px + 2 * py + 1 - c
                        copy(0, 4 + j, block, sibling).wait_recv()
                    else:
                        block = 4 * px + 2 * py + c
                        copy(0, 1 + j, block, sibling).wait_recv()
                        copy(0, 4 + j, block, sibling).start()
                stage(nxt % 2, block).start()

        @pl.when((p == N_DEV - 1) & (i == n_i - 1))
        def _():
            passed = [copy(0, 4 + j, 4 * px + 2 * py + c, sibling) for j, (px, py) in enumerate(chips)]
            for j, (px, py) in enumerate(chips):
                for t in range(1, n):
                    block = 4 * px + 2 * py + c
                    copy(t, 1 + j, block, sibling).wait_recv()
                    passed.append(copy(t, 4 + j, block, sibling))
                    passed[-1].start()
            for t in range(1, n):
                copy(t, 0, 4 * x + 2 * y + 1 - c, sibling).wait_recv()
                for j, (px, py) in enumerate(chips):
                    copy(t, 4 + j, 4 * px + 2 * py + 1 - c, sibling).wait_recv()
            for cp in first_copies() + passed:
                cp.wait_send()
            for cp in own_copies():
                cp.wait()

    return pl.pallas_call(
        body, name="proj_fwd_gather",
        grid_spec=pltpu.PrefetchScalarGridSpec(
            num_scalar_prefetch=1, grid=(N_DEV, n_i),
            in_specs=[pl.BlockSpec((tm, D), lambda p, i, order: (i, 0))] + [ANY] * n,
            out_specs=[pl.BlockSpec((tm, W_IN_SHARD), lambda p, i, order: (i, order[p])),
                       pl.BlockSpec((UNIT, D), wt_index)] + [ANY] * n,
            scratch_shapes=[pltpu.VMEM((2, D, W_IN_SHARD), BF16), pltpu.SemaphoreType.DMA((2,)),
                            pltpu.SemaphoreType.DMA((n, 7)), pltpu.SemaphoreType.DMA((n, 7)),
                            pltpu.SemaphoreType.DMA((n,))]),
        out_shape=[jax.ShapeDtypeStruct((n_rows, N_GROUPS * D), F32), jax.ShapeDtypeStruct((N_GROUPS * D, D), BF16)]
        + [jax.ShapeDtypeStruct((N_DEV, *b.shape), b.dtype) for b in blocks],
        compiler_params=_params(2),
    )(jnp.stack(_arrival_order()).astype(jnp.int32), u16, *blocks)


def _branch_a_fwd(proj, wo4_g, cw_a, tm):
    n_rows = proj.shape[0]

    def body(bp, cp, xp, za, cph, xph, w_ref, cw_ref, sa_ref, ya_ref, e_scr, o_scr):
        i = pl.program_id(0)
        _to_time_major(e_scr, 0, jnp.where(i > 0, _f32(cph) * _f32(xph), 0.0))
        for r0 in range(0, tm, ELEM_ROWS):
            rows = pl.ds(r0, ELEM_ROWS)
            _to_time_major(e_scr, HALO + r0, cp[rows, :].astype(F32) * xp[rows, :].astype(F32))
        _conv(o_scr, e_scr, cw_ref, FWD_TAPS_A, tm)
        for r0 in range(0, tm, ELEM_ROWS):
            rows = pl.ds(r0, ELEM_ROWS)
            ca = _from_time_major(o_scr, ELEM_ROWS, r0)
            sa_ref[rows, :] = (jax.nn.silu(za[rows, :].astype(F32)) * (bp[rows, :].astype(F32) * ca)).astype(BF16)
        ya_ref[...] = _dot(sa_ref[...], w_ref[...].reshape(D, D))

    return pl.pallas_call(
        body, name="branch_a_fwd", grid=(n_rows // tm,),
        in_specs=[_rows(tm, G_BA), _rows(tm, G_CA), _rows(tm, G_XA), _rows(tm, G_ZA),
                  _prev_halo(tm, G_CA), _prev_halo(tm, G_XA), _w_out_spec(0), _const(cw_a.shape)],
        out_specs=[_rows(tm), _rows(tm)],
        out_shape=[jax.ShapeDtypeStruct((n_rows, D), BF16), jax.ShapeDtypeStruct((n_rows, D), F32)],
        scratch_shapes=[_time_major(tm + HALO), _time_major(tm)],
        compiler_params=_params(1),
    )(proj, proj, proj, proj, proj, proj, wo4_g, cw_a)


def _layernorm_parts(cb, lg, lb):
    xc = cb - _mean(cb)
    rstd = lax.rsqrt(_mean(xc * xc) + EPS)
    xhat = xc * rstd
    return xhat, rstd, xhat * lg + lb


def _branch_b_fwd(proj, wo4_g, cw_b, conv_b_b, ln_g, ln_b, tm):
    n_rows = proj.shape[0]

    def body(vb, gb, zb, vbh, gbh, w_ref, cw_ref, bb_ref, lg_ref, lb_ref, cb_ref, sb_ref, yb_ref, e_scr, o_scr):
        i = pl.program_id(0)
        vbh, gbh = _f32(vbh), _f32(gbh)
        lg, lb = lg_ref[...], lb_ref[...]
        _to_time_major(e_scr, 0, jnp.where(i > 0, vbh * jax.nn.sigmoid(gbh), 0.0))
        for r0 in range(0, tm, ELEM_ROWS):
            rows = pl.ds(r0, ELEM_ROWS)
            _to_time_major(e_scr, HALO + r0, vb[rows, :].astype(F32) * jax.nn.sigmoid(gb[rows, :].astype(F32)))
        _conv(o_scr, e_scr, cw_ref, FWD_TAPS_B, tm, bias_ref=bb_ref)
        for r0 in range(0, tm, ELEM_ROWS):
            rows = pl.ds(r0, ELEM_ROWS)
            cb = _from_time_major(o_scr, ELEM_ROWS, r0)
            cb_ref[rows, :] = cb
            _, _, ln = _layernorm_parts(cb, lg, lb)
            sb_ref[rows, :] = (jax.nn.silu(zb[rows, :].astype(F32)) * jax.nn.silu(ln)).astype(BF16)
        yb_ref[...] = _dot(sb_ref[...], w_ref[...].reshape(D, D))

    return pl.pallas_call(
        body, name="branch_b_fwd", grid=(n_rows // tm,),
        in_specs=[_rows(tm, G_VB), _rows(tm, G_GB), _rows(tm, G_ZB), _prev_halo(tm, G_VB), _prev_halo(tm, G_GB),
                  _w_out_spec(1), _const(cw_b.shape), _const((LANE_GROUPS, LANES)), _const((1, D)), _const((1, D))],
        out_specs=[_rows(tm), _rows(tm), _rows(tm)],
        out_shape=[jax.ShapeDtypeStruct((n_rows, D), F32), jax.ShapeDtypeStruct((n_rows, D), BF16),
                   jax.ShapeDtypeStruct((n_rows, D), F32)],
        scratch_shapes=[_time_major(tm + HALO), _time_major(tm)],
        compiler_params=_params(1),
    )(proj, proj, proj, proj, proj, wo4_g, cw_b, conv_b_b.reshape(LANE_GROUPS, LANES), ln_g, ln_b)


def _attention(q16, kv_ref):
    probs, outs = [], []
    for h in range(N_HEADS):
        s = _dot_nt(q16[:, h * HEAD_DIM:(h + 1) * HEAD_DIM], kv_ref[h]) * (HEAD_DIM ** -0.5)
        e = jnp.exp(s - jnp.max(s, axis=-1, keepdims=True))
        p = e / jnp.sum(e, axis=-1, keepdims=True)
        probs.append(p)
        outs.append(_dot(p.astype(BF16), kv_ref[N_HEADS + h]))
    return probs, outs


def _branch_x_fwd(proj, kv16, wo4_g, tm):
    n_rows = proj.shape[0]

    def body(q, zx, kv_ref, w_ref, sx_ref, yx_ref, p_ref):
        probs, outs = _attention(q[...].astype(BF16), kv_ref)
        p_ref[...] = jnp.concatenate(probs, axis=-1)
        sx = (jax.nn.silu(_f32(zx)) * jnp.concatenate(outs, axis=-1)).astype(BF16)
        sx_ref[...] = sx
        yx_ref[...] = _dot(sx, w_ref[...].reshape(D, D))

    return pl.pallas_call(
        body, name="branch_x_fwd", grid=(n_rows // tm,),
        in_specs=[_rows(tm, G_Q), _rows(tm, G_ZX), _const(kv16.shape), _w_out_spec(2)],
        out_specs=[_rows(tm), _rows(tm), _rows(tm)],
        out_shape=[jax.ShapeDtypeStruct((n_rows, D), BF16), jax.ShapeDtypeStruct((n_rows, D), F32),
                   jax.ShapeDtypeStruct((n_rows, D), F32)],
        compiler_params=_params(1),
    )(proj, proj, kv16, wo4_g)


def _merge_fwd_bwd(proj, ya, yb, yx, x, target, wo4_g, final_g, tm):
    n_rows = proj.shape[0]
    inv_d = 1.0 / D

    def body(ga, gb, gx, ya_ref, yb_ref, yx_ref, x_ref, t_ref, w_ref, fg_ref,
             dh_ref, dya_ref, dyb_ref, dyx_ref, dp_ref, dw_ref, dfg_ref, sq_ref):
        i = pl.program_id(0)
        wo = w_ref[...].reshape(D, D)
        sig = [jax.nn.sigmoid(_f32(g)) for g in (ga, gb, gx)]
        ys = [ya_ref[...], yb_ref[...], yx_ref[...]]
        m16 = (sig[0] * ys[0] + sig[1] * ys[1] + sig[2] * ys[2]).astype(BF16)
        h = x_ref[...] + _dot(m16, wo)
        r = lax.rsqrt(_mean(h * h) + EPS)
        hn = h * r
        fg = fg_ref[...]
        err = hn * fg - t_ref[...]
        dy = err * inv_d
        dhn = dy * fg
        dh = r * (dhn - hn * _mean(dhn * hn))
        dh_ref[...] = dh
        dh16 = dh.astype(BF16)
        dm = _dot_nt(dh16, wo)
        for n, out in enumerate((dya_ref, dyb_ref, dyx_ref)):
            out[...] = (sig[n] * dm).astype(BF16)
            dp_ref[:, n * D:(n + 1) * D] = (dm * ys[n] * (sig[n] * (1.0 - sig[n]))).astype(BF16)

        @pl.when(i == 0)
        def _():
            dw_ref[...] = jnp.zeros_like(dw_ref)
            dfg_ref[...] = jnp.zeros_like(dfg_ref)
            sq_ref[...] = jnp.zeros_like(sq_ref)

        dw_ref[0] += _dot_tn(m16, dh16)
        dfg_ref[...] += _fold8(dy * hn)
        sq_ref[...] += _fold8(err * err)

    vec = jax.ShapeDtypeStruct((SUBLANES, D), F32)
    return pl.pallas_call(
        body, name="merge_fwd_bwd", grid=(n_rows // tm,),
        in_specs=[_rows(tm, G_GA), _rows(tm, G_GBB), _rows(tm, G_GX), _rows(tm), _rows(tm), _rows(tm), _rows(tm),
                  _rows(tm), _w_out_spec(3), _const((1, D))],
        out_specs=[_rows(tm), _rows(tm), _rows(tm), _rows(tm), pl.BlockSpec((tm, 3 * D), lambda i: (i, 3)),
                   pl.BlockSpec((1, D, D), lambda i: (3, 0, 0)), _const((SUBLANES, D)), _const((SUBLANES, D))],
        out_shape=[jax.ShapeDtypeStruct((n_rows, D), F32), jax.ShapeDtypeStruct((n_rows, D), BF16),
                   jax.ShapeDtypeStruct((n_rows, D), BF16), jax.ShapeDtypeStruct((n_rows, D), BF16),
                   jax.ShapeDtypeStruct((n_rows, N_GROUPS * D), BF16), jax.ShapeDtypeStruct((4, D, D), F32), vec, vec],
        compiler_params=_params(1),
    )(proj, proj, proj, ya, yb, yx, x, target, wo4_g, final_g)


def _branch_a_bwd(dya, proj, sa16, wo4_g, cw_a, dproj, dw4, tm):
    n_rows = proj.shape[0]
    n_tiles = n_rows // tm

    def body(dya_ref, bp, cp, xp, za, sa_ref, dyan, bpn, zan, cph, xph, w_ref, cw_ref, dp_in, dw_in,
             dp_ref, dw_ref, dwa_ref, e1, e2, o_scr, mm_scr):
        del dp_in, dw_in
        i = pl.program_id(0)
        bpn, zan, cph, xph = (_f32(r) for r in (bpn, zan, cph, xph))
        woa = w_ref[...].reshape(D, D)
        dya16 = dya_ref[...]
        chunks = [pl.ds(r0, ELEM_ROWS) for r0 in range(0, tm, ELEM_ROWS)]
        _to_time_major(e1, 0, jnp.where(i > 0, cph * xph, 0.0))
        for r0, rows in zip(range(0, tm, ELEM_ROWS), chunks):
            _to_time_major(e1, HALO + r0, cp[rows, :].astype(F32) * xp[rows, :].astype(F32))
        _conv(o_scr, e1, cw_ref, FWD_TAPS_A, tm)
        mm_scr[...] = _dot_nt(dya16, woa)
        for r0, rows in zip(range(0, tm, ELEM_ROWS), chunks):
            ca = _from_time_major(o_scr, ELEM_ROWS, r0)
            dsa = mm_scr[rows, :]
            b = bp[rows, :].astype(F32)
            silu_z, dsilu_z = _silu_and_grad(za[rows, :].astype(F32))
            t = dsa * silu_z
            dp_ref[rows, 0 * D:1 * D] = (t * ca).astype(BF16)
            dp_ref[rows, 3 * D:4 * D] = (dsa * (b * ca) * dsilu_z).astype(BF16)
            _to_time_major(e2, r0, t * b)
        dcan = (_dot_nt(dyan[...], woa) * jax.nn.silu(zan)) * bpn
        _to_time_major(e2, tm, jnp.where(i < n_tiles - 1, dcan, 0.0))

        @pl.when(i == 0)
        def _():
            dw_ref[...] = jnp.zeros_like(dw_ref)
            dwa_ref[...] = jnp.zeros_like(dwa_ref)

        _conv_wgrad(dwa_ref, e2, e1, FWD_TAPS_A, tm)
        dw_ref[0] += _dot_tn(sa_ref[...], dya16)
        _conv(o_scr, e2, cw_ref, BWD_TAPS_A, tm)
        for r0, rows in zip(range(0, tm, ELEM_ROWS), chunks):
            dprod = _from_time_major(o_scr, ELEM_ROWS, r0)
            dp_ref[rows, 1 * D:2 * D] = (dprod * xp[rows, :].astype(F32)).astype(BF16)
            dp_ref[rows, 2 * D:3 * D] = (dprod * cp[rows, :].astype(F32)).astype(BF16)

    return pl.pallas_call(
        body, name="branch_a_bwd", grid=(n_tiles,),
        in_specs=[_rows(tm), _rows(tm, G_BA), _rows(tm, G_CA), _rows(tm, G_XA), _rows(tm, G_ZA), _rows(tm),
                  _next_halo(tm, n_rows), _next_halo(tm, n_rows, G_BA), _next_halo(tm, n_rows, G_ZA),
                  _prev_halo(tm, G_CA), _prev_halo(tm, G_XA), _w_out_spec(0), _const(cw_a.shape), ANY, ANY],
        out_specs=[pl.BlockSpec((tm, 4 * D), lambda i: (i, 0)), pl.BlockSpec((1, D, D), lambda i: (0, 0, 0)),
                   _const((K_A * LANE_GROUPS, LANES))],
        out_shape=[jax.ShapeDtypeStruct(dproj.shape, BF16), jax.ShapeDtypeStruct(dw4.shape, F32),
                   jax.ShapeDtypeStruct((K_A * LANE_GROUPS, LANES), F32)],
        input_output_aliases={13: 0, 14: 1},
        scratch_shapes=[_time_major(tm + HALO), _time_major(tm + HALO), _time_major(tm), pltpu.VMEM((tm, D), F32)],
        compiler_params=_params(1),
    )(dya, proj, proj, proj, proj, sa16, dya, proj, proj, proj, proj, wo4_g, cw_a, dproj, dw4)


def _branch_b_bwd(dyb, proj, cb, sb16, wo4_g, cw_b, ln_g, ln_b, dproj, dw4, tm):
    n_rows = proj.shape[0]
    n_tiles = n_rows // tm

    def body(dyb_ref, zb, cb_ref, vb, gb, sb_ref, dybn, zbn, cbn, vbh, gbh, w_ref, cw_ref, lg_ref, lb_ref,
             dp_in, dw_in, dp_ref, dw_ref, dwb_ref, dbb_ref, dlg_ref, dlb_ref, e1, e2, o_scr, mm_scr):
        del dp_in, dw_in
        zbn, vbh, gbh = (_f32(r) for r in (zbn, vbh, gbh))
        i = pl.program_id(0)
        wob = w_ref[...].reshape(D, D)
        lg, lb = lg_ref[...], lb_ref[...]

        def conv_out_grad(dsb, z, c):
            xhat, rstd, ln = _layernorm_parts(c, lg, lb)
            sw, dsw = _silu_and_grad(ln)
            sz, dsz = _silu_and_grad(z)
            dln = (dsb * sz) * dsw
            dxhat = dln * lg
            dcb = rstd * (dxhat - _mean(dxhat) - xhat * _mean(dxhat * xhat))
            return dsb * sw * dsz, dln, xhat, dcb

        @pl.when(i == 0)
        def _():
            dw_ref[...] = jnp.zeros_like(dw_ref)
            dwb_ref[...] = jnp.zeros_like(dwb_ref)
            dbb_ref[...] = jnp.zeros_like(dbb_ref)
            dlg_ref[...] = jnp.zeros_like(dlg_ref)
            dlb_ref[...] = jnp.zeros_like(dlb_ref)

        dyb16 = dyb_ref[...]
        mm_scr[...] = _dot_nt(dyb16, wob)
        dlg, dlb, dbb = (jnp.zeros((SUBLANES, D), F32),) * 3
        for r0 in range(0, tm, ELEM_ROWS):
            rows = pl.ds(r0, ELEM_ROWS)
            dzb, dln, xhat, dcb = conv_out_grad(mm_scr[rows, :], zb[rows, :].astype(F32), cb_ref[rows, :])
            dp_ref[rows, 2 * D:3 * D] = dzb.astype(BF16)
            _to_time_major(e2, r0, dcb)
            dlg, dlb, dbb = dlg + _fold8(dln * xhat), dlb + _fold8(dln), dbb + _fold8(dcb)
        _, _, _, dcbn = conv_out_grad(_dot_nt(dybn[...], wob), zbn[...], cbn[...])
        _to_time_major(e2, tm, jnp.where(i < n_tiles - 1, dcbn, 0.0))
        dlg_ref[...] += dlg
        dlb_ref[...] += dlb
        dbb_ref[...] += dbb
        dw_ref[0] += _dot_tn(sb_ref[...], dyb16)
        _to_time_major(e1, 0, jnp.where(i > 0, vbh[...] * jax.nn.sigmoid(gbh[...]), 0.0))
        for r0 in range(0, tm, ELEM_ROWS):
            rows = pl.ds(r0, ELEM_ROWS)
            glu = vb[rows, :].astype(F32) * jax.nn.sigmoid(gb[rows, :].astype(F32))
            _to_time_major(e1, HALO + r0, glu)
        _conv_wgrad(dwb_ref, e2, e1, FWD_TAPS_B, tm)
        _conv(o_scr, e2, cw_ref, BWD_TAPS_B, tm)
        for r0 in range(0, tm, ELEM_ROWS):
            rows = pl.ds(r0, ELEM_ROWS)
            dglu = _from_time_major(o_scr, ELEM_ROWS, r0)
            sg = jax.nn.sigmoid(gb[rows, :].astype(F32))
            dp_ref[rows, 0 * D:1 * D] = (dglu * sg).astype(BF16)
            dp_ref[rows, 1 * D:2 * D] = (dglu * vb[rows, :].astype(F32) * (sg * (1.0 - sg))).astype(BF16)

    vec = jax.ShapeDtypeStruct((SUBLANES, D), F32)
    return pl.pallas_call(
        body, name="branch_b_bwd", grid=(n_tiles,),
        in_specs=[_rows(tm), _rows(tm, G_ZB), _rows(tm), _rows(tm, G_VB), _rows(tm, G_GB), _rows(tm),
                  _next_halo(tm, n_rows), _next_halo(tm, n_rows, G_ZB), _next_halo(tm, n_rows),
                  _prev_halo(tm, G_VB), _prev_halo(tm, G_GB), _w_out_spec(1), _const(cw_b.shape), _const((1, D)),
                  _const((1, D)), ANY, ANY],
        out_specs=[pl.BlockSpec((tm, 3 * D), lambda i: (i, 2)), pl.BlockSpec((1, D, D), lambda i: (1, 0, 0)),
                   _const((K_B * LANE_GROUPS, LANES)), _const((SUBLANES, D)), _const((SUBLANES, D)),
                   _const((SUBLANES, D))],
        out_shape=[jax.ShapeDtypeStruct(dproj.shape, BF16), jax.ShapeDtypeStruct(dw4.shape, F32),
                   jax.ShapeDtypeStruct((K_B * LANE_GROUPS, LANES), F32), vec, vec, vec],
        input_output_aliases={15: 0, 16: 1},
        scratch_shapes=[_time_major(tm + HALO), _time_major(tm + HALO), _time_major(tm), pltpu.VMEM((tm, D), F32)],
        compiler_params=_params(1),
    )(dyb, proj, cb, proj, proj, sb16, dyb, proj, cb, proj, proj, wo4_g, cw_b, ln_g, ln_b, dproj, dw4)


def _branch_x_bwd(dyx, proj, sx16, probs, kv16, wo4_g, dproj, dw4, tm):
    n_rows = proj.shape[0]
    scale = HEAD_DIM ** -0.5

    def body(dyx_ref, q, zx, sx_ref, p_ref, kv_ref, w_ref, dp_in, dw_in, dp_ref, dw_ref, dkv_ref):
        del dp_in, dw_in
        i = pl.program_id(0)
        dyx16 = dyx_ref[...]
        q16 = q[...].astype(BF16)
        probs = [p_ref[:, h * HEAD_DIM:(h + 1) * HEAD_DIM] for h in range(N_HEADS)]
        outs = [_dot(probs[h].astype(BF16), kv_ref[N_HEADS + h]) for h in range(N_HEADS)]
        dsx = _dot_nt(dyx16, w_ref[...].reshape(D, D))
        silu_z, dsilu_z = _silu_and_grad(_f32(zx))
        dp_ref[:, D:2 * D] = (dsx * jnp.concatenate(outs, axis=-1) * dsilu_z).astype(BF16)
        do16 = (dsx * silu_z).astype(BF16)

        @pl.when(i == 0)
        def _():
            dw_ref[...] = jnp.zeros_like(dw_ref)
            dkv_ref[...] = jnp.zeros_like(dkv_ref)

        for h in range(N_HEADS):
            cols = slice(h * HEAD_DIM, (h + 1) * HEAD_DIM)
            p = probs[h]
            dprob = _dot_nt(do16[:, cols], kv_ref[N_HEADS + h])
            ds16 = ((p * (dprob - jnp.sum(p * dprob, axis=-1, keepdims=True))) * scale).astype(BF16)
            dp_ref[:, cols] = _dot(ds16, kv_ref[h]).astype(BF16)
            dkv_ref[h] += _dot_tn(ds16, q16[:, cols])
            dkv_ref[N_HEADS + h] += _dot_tn(p.astype(BF16), do16[:, cols])
        dw_ref[0] += _dot_tn(sx_ref[...], dyx16)

    return pl.pallas_call(
        body, name="branch_x_bwd", grid=(n_rows // tm,),
        in_specs=[_rows(tm), _rows(tm, G_Q), _rows(tm, G_ZX), _rows(tm), _rows(tm), _const(kv16.shape),
                  _w_out_spec(2), ANY, ANY],
        out_specs=[pl.BlockSpec((tm, 2 * D), lambda i: (i, 2)), pl.BlockSpec((1, D, D), lambda i: (2, 0, 0)),
                   _const(kv16.shape)],
        out_shape=[jax.ShapeDtypeStruct(dproj.shape, BF16), jax.ShapeDtypeStruct(dw4.shape, F32),
                   jax.ShapeDtypeStruct(kv16.shape, F32)],
        input_output_aliases={7: 0, 8: 1},
        compiler_params=_params(1),
    )(dyx, proj, proj, sx16, probs, kv16, wo4_g, dproj, dw4)


def _dp_unit(u):
    g = u // 2
    pos = jnp.where(g < G_VB, g, jnp.where(g < G_Q, g + 2, jnp.where(g < G_GA, g - 3, g)))
    return 2 * pos + u % 2


def _scatter_copies(srcs, lands, send, recv):
    x, y, c = lax.axis_index("x"), lax.axis_index("y"), lax.axis_index("c")
    copies = []
    for n in range(N_DEV - 1):
        flip = n + 1
        px = 1 - x if flip & 4 else x
        py = 1 - y if flip & 2 else y
        pc = 1 - c if flip & 1 else c
        for t, (src, land) in enumerate(zip(srcs, lands)):
            copies.append(pltpu.make_async_remote_copy(
                src_ref=src.at[4 * px + 2 * py + pc], dst_ref=land.at[n], send_sem=send.at[t * (N_DEV - 1) + n],
                recv_sem=recv.at[t * (N_DEV - 1) + n], device_id=(px, py, pc), device_id_type=MESH))
    return copies


def _scatter_start(name, arrays, views, n_views):
    n = len(arrays)
    lands = [lax.empty(tuple(N_DEV - 1 if d == N_DEV else d for d in a.shape), a.dtype) for a in arrays]

    def body(*refs):
        src, land, (send, recv) = refs[:n], refs[n:2 * n], refs[2 * n:2 * n + 2]
        token = refs[-1]
        for cp in _scatter_copies(views(src), views(land), send, recv):
            cp.start()
        token[...] = jnp.zeros_like(token)

    sems = pltpu.SemaphoreType.DMA((n_views * (N_DEV - 1),))
    out = pl.pallas_call(
        body, name=name,
        in_specs=[HBM] * (2 * n),
        out_specs=[SEM, SEM] + [HBM] * (2 * n) + [pl.BlockSpec(memory_space=pltpu.VMEM)],
        out_shape=[sems, sems] + [pltpu.HBM(a.shape, a.dtype) for a in arrays + lands]
        + [jax.ShapeDtypeStruct((SUBLANES, LANES), F32)],
        input_output_aliases={k: 2 + k for k in range(2 * n)},
        compiler_params=pltpu.CompilerParams(has_side_effects=SIDE_EFFECT),
    )(*[pltpu.with_memory_space_constraint(a, pltpu.HBM) for a in arrays + lands])
    return dict(name=name, sems=out[:2], moving=out[2:2 + 2 * n], views=views, token=out[-1])


def _scatter_wait(started, after):
    n = len(started["moving"]) // 2
    views = started["views"]

    def body(*refs):
        src, land, (send, recv) = refs[:n], refs[n:2 * n], refs[2 * n:2 * n + 2]
        for cp in _scatter_copies(views(src), views(land), send, recv):
            cp.wait_send()
            cp.wait_recv()

    out = pl.pallas_call(
        body, name=started["name"].replace("start", "wait"),
        in_specs=[HBM] * (2 * n) + [SEM, SEM, ANY],
        out_specs=[HBM] * (2 * n),
        out_shape=[pltpu.HBM(a.shape, a.dtype) for a in started["moving"]],
        input_output_aliases={k: k for k in range(2 * n)},
        compiler_params=pltpu.CompilerParams(has_side_effects=SIDE_EFFECT),
    )(*started["moving"], *started["sems"], after)
    return out[n:]


def _w_in_grad(ut, dproj, tk, token):
    n_rows = dproj.shape[0]
    n_k = n_rows // tk
    per_shard = W_IN_SHARD // UNIT

    def body(ut_ref, dp0, dp1, dp2, token_ref, out_ref, out16_ref, acc):
        del token_ref
        t = pl.program_id(1)

        for r, dp_ref in enumerate((dp0, dp1, dp2)):
            cols = slice(r * UNIT, (r + 1) * UNIT)

            @pl.when(t == 0)
            def _(dp_ref=dp_ref, cols=cols):
                acc[:, cols] = _dot(ut_ref[...], dp_ref[...])

            @pl.when(t > 0)
            def _(dp_ref=dp_ref, cols=cols):
                acc[:, cols] += _dot(ut_ref[...], dp_ref[...])

        @pl.when(t == n_k - 1)
        def _():
            out_ref[0] = acc[...]
            out16_ref[0] = acc[...].astype(BF16)

    def dp_spec(r):
        return pl.BlockSpec((tk, UNIT), lambda q, t: (t, _dp_unit(per_shard * q + r)))

    shard = pl.BlockSpec((1, D, W_IN_SHARD), lambda q, t: (q, 0, 0))
    return pl.pallas_call(
        body, name="w_in_grad", grid=(N_DEV, n_k),
        in_specs=[pl.BlockSpec((D, tk), lambda q, t: (0, t)), dp_spec(0), dp_spec(1), dp_spec(2), ANY],
        out_specs=[shard, shard],
        out_shape=[jax.ShapeDtypeStruct((N_DEV, D, W_IN_SHARD), F32), jax.ShapeDtypeStruct((N_DEV, D, W_IN_SHARD), BF16)],
        scratch_shapes=[pltpu.VMEM((D, W_IN_SHARD), F32)],
        compiler_params=_params(2),
    )(ut, dproj, dproj, dproj, token)


def _x_grad(dproj, win_t, x, dh, norm_g, token, tm):
    n_rows = x.shape[0]
    n_k = N_GROUPS * D // X_GRAD_K

    def body(dp_ref, wt_ref, x_ref, dh_ref, g_ref, token_ref, gx_ref, dg_ref, acc):
        del token_ref
        i, g = pl.program_id(0), pl.program_id(1)

        @pl.when((i == 0) & (g == 0))
        def _():
            dg_ref[...] = jnp.zeros_like(dg_ref)

        @pl.when(g == 0)
        def _():
            acc[...] = _dot(dp_ref[...], wt_ref[...])

        @pl.when(g > 0)
        def _():
            acc[...] += _dot(dp_ref[...], wt_ref[...])

        @pl.when(g == n_k - 1)
        def _():
            du = acc[...]
            xf = x_ref[...]
            r = lax.rsqrt(_mean(xf * xf) + EPS)
            xn = xf * r
            dun = du * g_ref[...]
            gx_ref[...] = dh_ref[...] + r * (dun - xn * _mean(dun * xn))
            dg_ref[...] += _fold8(du * xn)

    return pl.pallas_call(
        body, name="x_grad", grid=(n_rows // tm, n_k),
        in_specs=[pl.BlockSpec((tm, X_GRAD_K), lambda i, g: (i, g)), pl.BlockSpec((X_GRAD_K, D), lambda i, g: (g, 0)),
                  pl.BlockSpec((tm, D), lambda i, g: (i, 0)), pl.BlockSpec((tm, D), lambda i, g: (i, 0)),
                  _const((1, D)), ANY],
        out_specs=[pl.BlockSpec((tm, D), lambda i, g: (i, 0)), _const((SUBLANES, D))],
        out_shape=[jax.ShapeDtypeStruct((n_rows, D), F32), jax.ShapeDtypeStruct((SUBLANES, D), F32)],
        scratch_shapes=[pltpu.VMEM((tm, D), F32)],
        compiler_params=_params(2),
    )(dproj, win_t, x, dh, norm_g, token)


def _local_step(x, mem, target, norm_g, conv_b_b, ln_g, ln_b, mem_g, final_g, shards):
    n_rows = x.shape[0]
    tm = min(512, n_rows)
    big = min(1024, n_rows)
    u16, ut = _rmsnorm_fwd(x, norm_g, big)
    proj, win_t, _, wkv_g, wo4_g, cw_g = _proj_fwd_gather(u16, shards, min(2048, n_rows))
    cw_rows = cw_g.transpose(1, 0, 2).reshape((SUBLANES + HALO) * LANE_GROUPS, LANES)
    cw_a, cw_b = cw_rows[:SUBLANES * LANE_GROUPS], cw_rows[SUBLANES * LANE_GROUPS:]
    kv16, mn16 = _kv_fwd(mem, mem_g, wkv_g)
    sa16, ya = _branch_a_fwd(proj, wo4_g, cw_a, tm)
    cb, sb16, yb = _branch_b_fwd(proj, wo4_g, cw_b, conv_b_b, ln_g, ln_b, tm)
    sx16, yx, probs = _branch_x_fwd(proj, kv16, wo4_g, tm)
    dh, dya, dyb, dyx, dproj, dw4, dfg, sq = _merge_fwd_bwd(proj, ya, yb, yx, x, target, wo4_g, final_g,
                                                             min(256, n_rows))
    dproj, dw4, dwa = _branch_a_bwd(dya, proj, sa16, wo4_g, cw_a, dproj, dw4, tm)
    dproj, dw4, dwb, dbb, dlg, dlb = _branch_b_bwd(dyb, proj, cb, sb16, wo4_g, cw_b, ln_g, ln_b, dproj, dw4, tm)
    dproj, dw4, dkv = _branch_x_bwd(dyx, proj, sx16, probs, kv16, wo4_g, dproj, dw4, tm)
    dwkv_g, dwkv16, dmg = _kv_bwd(dkv, mem, mem_g, mn16, wkv_g)
    dw4 = dw4.reshape(4, N_DEV, D // N_DEV, D)
    small_moving = _scatter_start("small_grads_start", [dw4.astype(BF16), dwkv16],
                                  lambda refs: [refs[0].at[w] for w in range(4)] + [refs[1]], 5)
    dwin_g, dwin16 = _w_in_grad(ut, dproj, min(2048, n_rows), small_moving["token"])
    w_in_moving = _scatter_start("w_in_grad_start", [dwin16], lambda refs: list(refs), 1)
    gx, dng = _x_grad(dproj, win_t, x, dh, norm_g, w_in_moving["token"], big)
    land4, landkv = _scatter_wait(small_moving, dng)
    landin, = _scatter_wait(w_in_moving, dng)
    small = {SV_NORM_G: dng, SV_CONV_B_B: dbb, SV_LN_G: dlg, SV_LN_B: dlb, SV_MEM_G: dmg, SV_FINAL_G: dfg, SV_LOSS: sq}
    grads = [(dwin_g[None], landin[None]), (dw4, land4), (dwkv_g[None], landkv[None])]
    return gx, grads, small, dwa.reshape(K_A, D), dwb.reshape(K_B, D)


def _allgather_small(small, conv_rows):
    keys = sorted(small)

    def body(*refs):
        parts, (conv_ref, out_ref, mine, send, recv) = refs[:len(keys)], refs[len(keys):]
        x, y, c, chips = _place()
        me, sibling = 4 * x + 2 * y + c, (x, y, 1 - c)
        mine[pl.ds(0, SV_CONV_A), :] = jnp.zeros((SV_CONV_A, D), F32)
        for key, part in zip(keys, parts):
            mine[key:key + 1, :] = jnp.sum(part[...], axis=0, keepdims=True)
        mine[pl.ds(SV_CONV_A, SV_ROWS - SV_CONV_A), :] = conv_ref[...]
        out_ref[me] = mine[...]

        def copy(k, block, to, from_mine=False):
            return pltpu.make_async_remote_copy(
                src_ref=mine if from_mine else out_ref.at[block], dst_ref=out_ref.at[block],
                send_sem=send.at[k], recv_sem=recv.at[k], device_id=to, device_id_type=MESH)

        first = [copy(0, me, sibling, from_mine=True)]
        first += [copy(1 + j, me, (*chip, c), from_mine=True) for j, chip in enumerate(chips)]
        for cp in first:
            cp.start()
        passed = []
        for j, (px, py) in enumerate(chips):
            block = 4 * px + 2 * py + c
            copy(1 + j, block, sibling).wait_recv()
            passed.append(copy(4 + j, block, sibling))
            passed[-1].start()
        copy(0, 4 * x + 2 * y + 1 - c, sibling).wait_recv()
        for j, (px, py) in enumerate(chips):
            copy(4 + j, 4 * px + 2 * py + 1 - c, sibling).wait_recv()
        for cp in first + passed:
            cp.wait_send()

    vmem = pl.BlockSpec(memory_space=pltpu.VMEM)
    return pl.pallas_call(
        body, name="allgather_small",
        in_specs=[vmem] * (len(keys) + 1), out_specs=vmem,
        out_shape=jax.ShapeDtypeStruct((N_DEV, SV_ROWS, D), F32),
        scratch_shapes=[pltpu.VMEM((SV_ROWS, D), F32), pltpu.SemaphoreType.DMA((7,)), pltpu.SemaphoreType.DMA((7,))],
    )(*[small[k] for k in keys], conv_rows)


def _adamw(w, g, m, v):
    m = ADAM_B1 * m + (1.0 - ADAM_B1) * g
    v = ADAM_B2 * v + (1.0 - ADAM_B2) * (g * g)
    m_hat = m / (1.0 - ADAM_B1 ** ADAM_STEP)
    v_hat = v / (1.0 - ADAM_B2 ** ADAM_STEP)
    return -ADAM_LR * (m_hat / (jnp.sqrt(v_hat) + ADAM_EPS) + ADAM_WD * w), m, v


def _adamw_shard(own, landed, piece, k_arr, w, m, v, tr):
    n_r, n_c = w.shape
    n_landed = landed.shape[1]

    def body(k_ref, own_ref, *refs):
        del k_ref
        landed_refs, (w_ref, m_ref, v_ref, g_out, d_out, m_out, v_out) = refs[:n_landed], refs[n_landed:]
        g = own_ref[0, 0]
        for landed_ref in landed_refs:
            g = g + landed_ref[0, 0].astype(F32)
        g_out[...] = g
        d_out[...], m_out[...], v_out[...] = _adamw(w_ref[...], g, m_ref[...], v_ref[...])

    blk = (1, 1, tr, n_c)
    flat = pl.BlockSpec((tr, n_c), lambda r, k: (r, 0))
    return pl.pallas_call(
        body, name="adamw_shard",
        grid_spec=pltpu.PrefetchScalarGridSpec(
            num_scalar_prefetch=1, grid=(n_r // tr,),
            in_specs=[pl.BlockSpec(blk, lambda r, k: (piece, k[0], r, 0))]
            + [pl.BlockSpec(blk, functools.partial(lambda r, k, j: (piece, j, r, 0), j=j)) for j in range(n_landed)]
            + [flat] * 3,
            out_specs=[flat] * 4),
        out_shape=[jax.ShapeDtypeStruct((n_r, n_c), F32)] * 4,
        compiler_params=_params(1),
    )(k_arr, own, *([landed] * n_landed), w, m, v)


def _adamw_shards(entries, k_arr):
    n = len(entries)

    def body(k_ref, *refs):
        del k_ref
        ins, outs = refs[:5 * n], refs[5 * n:]
        for e in range(n):
            own_ref, landed_ref, w_ref, m_ref, v_ref = ins[5 * e:5 * e + 5]
            g = own_ref[0, 0]
            for j in range(landed_ref.shape[1]):
                g = g + landed_ref[0, j].astype(F32)
            g_out, d_out, m_out, v_out = outs[4 * e:4 * e + 4]
            g_out[...] = g
            d_out[...], m_out[...], v_out[...] = _adamw(w_ref[...], g, m_ref[...], v_ref[...])

    in_specs, operands, out_specs, out_shape = [], [], [], []
    for own, landed, piece, w, m, v in entries:
        flat = pl.BlockSpec(w.shape, lambda i, k: (0, 0))
        in_specs += [pl.BlockSpec((1, 1, *w.shape), functools.partial(lambda i, k, p: (p, k[0], 0, 0), p=piece)),
                     pl.BlockSpec((1, *landed.shape[1:]), functools.partial(lambda i, k, p: (p, 0, 0, 0), p=piece)),
                     flat, flat, flat]
        operands += [own, landed, w, m, v]
        out_specs += [flat] * 4
        out_shape += [jax.ShapeDtypeStruct(w.shape, F32)] * 4
    out = pl.pallas_call(
        body, name="adamw_shards",
        grid_spec=pltpu.PrefetchScalarGridSpec(num_scalar_prefetch=1, grid=(1,), in_specs=in_specs,
                                               out_specs=out_specs),
        out_shape=out_shape,
        compiler_params=_params(1),
    )(k_arr, *operands)
    return [tuple(out[4 * e:4 * e + 4]) for e in range(n)]


def _adamw_small(gathered, k_arr, vectors, conv_a, conv_b):
    n_vec = len(vectors)
    cols = D // N_DEV

    def body(k_ref, full_ref, cols_ref, *refs):
        del k_ref
        ins, outs = refs[:3 * (n_vec + 2)], refs[3 * (n_vec + 2):-2]
        tot, tot_cols = refs[-2:]
        tot[...] = full_ref[0]
        tot_cols[...] = cols_ref[0]
        for dev in range(1, N_DEV):
            tot[...] += full_ref[dev]
            tot_cols[...] += cols_ref[dev]
        loss = (0.5 / D) * jnp.sum(tot[SV_LOSS:SV_LOSS + 1, :])
        outs[0][...] = jnp.full(outs[0].shape, loss, F32)
        grads = [tot[n:n + 1, :] for n in range(n_vec)]
        grads += [tot_cols[pl.ds(SV_CONV_A, K_A), :], tot_cols[pl.ds(SV_CONV_B, K_B), :]]
        for n, g in enumerate(grads):
            w_ref, m_ref, v_ref = ins[3 * n:3 * n + 3]
            g_out, d_out, m_out, v_out = outs[1 + 4 * n:5 + 4 * n]
            g_out[...] = g
            d_out[...], m_out[...], v_out[...] = _adamw(w_ref[...], g, m_ref[...], v_ref[...])

    weights = list(vectors) + [conv_a, conv_b]
    flat_in = [a for wmv in weights for a in wmv]
    out_shape = [jax.ShapeDtypeStruct((SUBLANES, 128), F32)]
    for wmv in weights:
        out_shape += [jax.ShapeDtypeStruct(wmv[0].shape, F32)] * 4
    return pl.pallas_call(
        body, name="adamw_small",
        grid_spec=pltpu.PrefetchScalarGridSpec(
            num_scalar_prefetch=1, grid=(1,),
            in_specs=[pl.BlockSpec((N_DEV, SV_ROWS, D), lambda i, k: (0, 0, 0)),
                      pl.BlockSpec((N_DEV, SV_ROWS, cols), lambda i, k: (0, 0, k[0]))]
            + [pl.BlockSpec(a.shape, lambda i, k: (0, 0)) for a in flat_in],
            out_specs=[pl.BlockSpec(s.shape, lambda i, k: (0, 0)) for s in out_shape],
            scratch_shapes=[pltpu.VMEM((SV_ROWS, D), F32), pltpu.VMEM((SV_ROWS, cols), F32)]),
        out_shape=out_shape,
        compiler_params=_params(1),
    )(k_arr, gathered, gathered, *flat_in)


def kernel(x, mem, norm_g, w_in, conv_a_w, w_out_a, conv_b_w, conv_b_b, ln_b_g, ln_b_b, w_out_b, mem_norm_g, w_kv, w_out_x, w_o, final_g, loss_target, m_norm_g, m_w_in, m_conv_a_w, m_w_out_a, m_conv_b_w, m_conv_b_b, m_ln_b_g, m_ln_b_b, m_w_out_b, m_mem_norm_g, m_w_kv, m_w_out_x, m_w_o, m_final_g, v_norm_g, v_w_in, v_conv_a_w, v_w_out_a, v_conv_b_w, v_conv_b_b, v_ln_b_g, v_ln_b_b, v_w_out_b, v_mem_norm_g, v_w_kv, v_w_out_x, v_w_o, v_final_g):
    xi, yi, ci = lax.axis_index("x"), lax.axis_index("y"), lax.axis_index("c")
    k_arr = jnp.reshape(4 * xi + 2 * yi + ci, (1,)).astype(jnp.int32)

    cw = jnp.concatenate([jnp.pad(conv_a_w[0], ((0, SUBLANES - K_A), (0, 0))),
                          jnp.pad(conv_b_w[0], ((0, HALO - K_B), (0, 0)))], axis=0)
    wo4 = jnp.stack([w_out_a[0], w_out_b[0], w_out_x[0], w_o[0]]).astype(BF16)
    shards = [w_in[0].astype(BF16), w_kv[0].astype(BF16), wo4, cw]

    final_g2 = final_g.reshape(1, D)
    gx, grads, small, dwa, dwb = _local_step(
        x[0], mem[0], loss_target[0], norm_g, conv_b_b, ln_b_g, ln_b_b, mem_norm_g, final_g2, shards)

    conv_rows = jnp.concatenate([jnp.pad(dwa, ((0, SUBLANES - K_A), (0, 0))),
                                 jnp.pad(dwb, ((0, HALO - K_B), (0, 0)))], axis=0)
    gathered_small = _allgather_small(small, conv_rows)

    res = {"w_in": _adamw_shard(grads[0][0], grads[0][1], 0, k_arr, w_in[0], m_w_in[0], v_w_in[0], 256)}
    small_shards = [("w_out_a", 1, 0, w_out_a, m_w_out_a, v_w_out_a), ("w_out_b", 1, 1, w_out_b, m_w_out_b, v_w_out_b),
                    ("w_out_x", 1, 2, w_out_x, m_w_out_x, v_w_out_x), ("w_o", 1, 3, w_o, m_w_o, v_w_o),
                    ("w_kv", 2, 0, w_kv, m_w_kv, v_w_kv)]
    updated = _adamw_shards([(grads[a][0], grads[a][1], l, w[0], m[0], v[0]) for _, a, l, w, m, v in small_shards],
                            k_arr)
    res.update({name: four for (name, *_), four in zip(small_shards, updated)})
    res = {name: tuple(r[None] for r in four) for name, four in res.items()}
    vectors = [(norm_g, m_norm_g, v_norm_g), (conv_b_b, m_conv_b_b, v_conv_b_b), (ln_b_g, m_ln_b_g, v_ln_b_g),
               (ln_b_b, m_ln_b_b, v_ln_b_b), (mem_norm_g, m_mem_norm_g, v_mem_norm_g),
               (final_g2, m_final_g.reshape(1, D), v_final_g.reshape(1, D))]
    out = _adamw_small(gathered_small, k_arr, vectors, (conv_a_w[0], m_conv_a_w[0], v_conv_a_w[0]),
                       (conv_b_w[0], m_conv_b_w[0], v_conv_b_w[0]))
    loss = out[0][0, 0]
    names = ["norm_g", "conv_b_b", "ln_b_g", "ln_b_b", "mem_norm_g", "final_g", "conv_a_w", "conv_b_w"]
    for n, name in enumerate(names):
        four = out[1 + 4 * n:5 + 4 * n]
        if name == "final_g":
            four = [r.reshape(D) for r in four]
        elif name.startswith("conv_") and name.endswith("_w"):
            four = [r[None] for r in four]
        res[name] = tuple(four)

    order = ["norm_g", "w_in", "conv_a_w", "w_out_a", "conv_b_w", "conv_b_b", "ln_b_g", "ln_b_b", "w_out_b",
             "mem_norm_g", "w_kv", "w_out_x", "w_o", "final_g"]
    return (loss, gx[None], *[res[n][0] for n in order], *[res[n][1] for n in order],
            *[res[n][2] for n in order], *[res[n][3] for n in order])
```

```python
import functools

import jax
import jax.numpy as jnp
from jax import lax
from jax.experimental import pallas as pl
from jax.experimental.pallas import tpu as pltpu

F32, BF16 = jnp.float32, jnp.bfloat16
D = 1024
N_DEV = 8
N_HEADS = 4
HEAD_DIM = D // N_HEADS
N_GROUPS = 12
W_IN_SHARD = N_GROUPS * D // N_DEV
UNIT = 512
X_GRAD_K = 2 * D
K_A, K_B = 3, 31
EPS = 1e-6
HALO = 32
SUBLANES = 8
LANES = 128
LANE_GROUPS = D // LANES
TAP_GROUP = 16
CONV_BLOCK = 32
ELEM_ROWS = 16
CONV_PARTIAL_SUMS = 4
VMEM_LIMIT = 56 << 20
MESH = pl.DeviceIdType.MESH
ANY = pl.BlockSpec(memory_space=pl.ANY)
HBM = pl.BlockSpec(memory_space=pltpu.HBM)
SEM = pl.BlockSpec(memory_space=pltpu.SEMAPHORE)
SIDE_EFFECT = pltpu.SideEffectType.DATAFLOW_SIDE_EFFECTING

G_BA, G_CA, G_XA, G_ZA, G_VB, G_GB, G_ZB, G_Q, G_ZX, G_GA, G_GBB, G_GX = range(N_GROUPS)

ADAM_LR, ADAM_B1, ADAM_B2, ADAM_EPS, ADAM_WD, ADAM_STEP = 0.001, 0.9, 0.999, 1e-08, 0.01, 10

SV_NORM_G, SV_CONV_B_B, SV_LN_G, SV_LN_B, SV_MEM_G, SV_FINAL_G, SV_LOSS = range(7)
SV_CONV_A, SV_CONV_B, SV_ROWS = 8, 16, 48


def _dot(a, b):
    return jnp.dot(a, b, preferred_element_type=F32)


def _dot_nt(a, b):
    return lax.dot_general(a, b, (((1,), (1,)), ((), ())), preferred_element_type=F32)


def _dot_tn(a, b):
    return lax.dot_general(a, b, (((0,), (0,)), ((), ())), preferred_element_type=F32)


def _silu_and_grad(z):
    s = jax.nn.sigmoid(z)
    return z * s, s * (1.0 + z * (1.0 - s))


def _fold8(a):
    return a.reshape(a.shape[0] // SUBLANES, SUBLANES, a.shape[1]).sum(axis=0)


def _mean(a):
    return jnp.mean(a, axis=-1, keepdims=True)


def _f32(ref):
    return ref[...].astype(F32)


def _params(n_grid):
    return pltpu.CompilerParams(dimension_semantics=("arbitrary",) * n_grid, vmem_limit_bytes=VMEM_LIMIT)


def _rows(tm, col=0):
    return pl.BlockSpec((tm, D), lambda i: (i, col))


def _prev_halo(tm, col=0):
    return pl.BlockSpec((HALO, D), lambda i: (jnp.maximum(i * (tm // HALO) - 1, 0), col))


def _next_halo(tm, n_rows, col=0):
    last = n_rows // HALO - 1
    return pl.BlockSpec((HALO, D), lambda i: (jnp.minimum((i + 1) * (tm // HALO), last), col))


def _const(shape):
    return pl.BlockSpec(shape, lambda *_: (0,) * len(shape))


def _w_out_spec(which):
    return pl.BlockSpec((N_DEV, None, D // N_DEV, D), lambda *_: (0, which, 0, 0))


def _to_time_major(t_ref, row0, x):
    n = x.shape[0]
    for j in range(LANE_GROUPS):
        t_ref[pl.ds(row0 * LANE_GROUPS + j, n, stride=LANE_GROUPS), :] = x[:, j * LANES:(j + 1) * LANES]


def _from_time_major(t_ref, n, row0=0):
    return jnp.concatenate([t_ref[pl.ds(row0 * LANE_GROUPS + j, n, stride=LANE_GROUPS), :]
                            for j in range(LANE_GROUPS)], axis=-1)


def _row(ref, t):
    start = t * LANE_GROUPS
    if not isinstance(start, int):
        start = pl.multiple_of(start, LANE_GROUPS)
    return ref[pl.ds(start, LANE_GROUPS), :]


def _tap_groups(taps):
    return [taps[first:first + TAP_GROUP] for first in range(0, len(taps), TAP_GROUP)]


def _conv(o_ref, e_ref, w_ref, taps, n_rows, bias_ref=None):
    for n_group, group in enumerate(_tap_groups(taps)):
        weights = [_row(w_ref, k) for k, _ in group]

        def block(c, carry, n_group=n_group, group=group, weights=weights):
            t0 = c * CONV_BLOCK
            window = {}
            for t in range(CONV_BLOCK):
                parts = [None] * min(CONV_PARTIAL_SUMS, len(group))
                for n, (_, off) in enumerate(group):
                    if t + off not in window:
                        window[t + off] = _row(e_ref, t0 + t + off)
                    term = weights[n] * window[t + off]
                    parts[n % len(parts)] = term if parts[n % len(parts)] is None else parts[n % len(parts)] + term
                window.pop(t + min(off for _, off in group), None)
                while len(parts) > 1:
                    parts = [parts[n] + parts[n + 1] for n in range(0, len(parts) - 1, 2)] + parts[len(parts) & ~1:]
                out = parts[0]
                if n_group > 0:
                    out = out + _row(o_ref, t0 + t)
                elif bias_ref is not None:
                    out = out + bias_ref[...]
                o_ref[pl.ds(pl.multiple_of((t0 + t) * LANE_GROUPS, LANE_GROUPS), LANE_GROUPS), :] = out
            return carry

        lax.fori_loop(0, n_rows // CONV_BLOCK, block, 0)


def _conv_wgrad(dw_ref, d_ref, e_ref, taps, n_rows):
    for group in _tap_groups(taps):
        def block(c, sums, group=group):
            t0 = c * CONV_BLOCK
            sums = list(sums)
            window = {}
            for t in range(CONV_BLOCK):
                d = _row(d_ref, t0 + t)
                for n, (_, off) in enumerate(group):
                    if t + off not in window:
                        window[t + off] = _row(e_ref, t0 + t + off)
                    sums[n] = sums[n] + d * window[t + off]
                window.pop(t + min(off for _, off in group), None)
            return tuple(sums)

        sums = lax.fori_loop(0, n_rows // CONV_BLOCK, block, tuple(_row(dw_ref, k) for k, _ in group))
        for (k, _), total in zip(group, sums):
            dw_ref[pl.ds(k * LANE_GROUPS, LANE_GROUPS), :] = total


FWD_TAPS_A = [(k, HALO - (K_A - 1) + k) for k in range(K_A)]
BWD_TAPS_A = [(k, K_A - 1 - k) for k in range(K_A)]
FWD_TAPS_B = [(k, HALO - (K_B - 1) + k) for k in range(K_B)]
BWD_TAPS_B = [(k, K_B - 1 - k) for k in range(K_B)]


def _time_major(n_rows):
    return pltpu.VMEM((n_rows * LANE_GROUPS, LANES), F32)


def _kv_fwd(mem, mem_g, wkv_g):
    m_len = mem.shape[0]

    def body(mem_ref, g_ref, w_ref, kv_ref, mn_ref):
        mf = mem_ref[...]
        r = lax.rsqrt(_mean(mf * mf) + EPS)
        mn = ((mf * r) * g_ref[...]).astype(BF16)
        mn_ref[...] = mn
        for b in range(2 * N_HEADS):
            kv_ref[b] = _dot(mn, w_ref[b]).astype(BF16)

    return pl.pallas_call(
        body, name="kv_fwd", grid=(1,),
        in_specs=[_const((m_len, D)), _const((1, D)), _const((2 * N_HEADS, D, HEAD_DIM))],
        out_specs=[_const((2 * N_HEADS, m_len, HEAD_DIM)), _const((m_len, D))],
        out_shape=[jax.ShapeDtypeStruct((2 * N_HEADS, m_len, HEAD_DIM), BF16), jax.ShapeDtypeStruct((m_len, D), BF16)],
        compiler_params=_params(1),
    )(mem, mem_g, wkv_g)


def _kv_bwd(dkv, mem, mem_g, mn16, wkv_g):
    m_len = mem.shape[0]

    def body(dkv_ref, mem_ref, g_ref, mn_ref, w_ref, dw_ref, dw16_ref, dg_ref):
        mn = mn_ref[...]
        dmn = jnp.zeros((m_len, D), F32)
        for b in range(2 * N_HEADS):
            d16 = dkv_ref[b].astype(BF16)
            dw = _dot_tn(mn, d16)
            dw_ref[b] = dw
            dw16_ref[b] = dw.astype(BF16)
            dmn = dmn + _dot_nt(d16, w_ref[b])
        mf = mem_ref[...]
        r = lax.rsqrt(_mean(mf * mf) + EPS)
        dg_ref[...] = _fold8(dmn * (mf * r))

    return pl.pallas_call(
        body, name="kv_bwd", grid=(1,),
        in_specs=[_const((2 * N_HEADS, m_len, HEAD_DIM)), _const((m_len, D)), _const((1, D)), _const((m_len, D)),
                  _const((2 * N_HEADS, D, HEAD_DIM))],
        out_specs=[_const((2 * N_HEADS, D, HEAD_DIM)), _const((2 * N_HEADS, D, HEAD_DIM)), _const((SUBLANES, D))],
        out_shape=[jax.ShapeDtypeStruct((2 * N_HEADS, D, HEAD_DIM), F32),
                   jax.ShapeDtypeStruct((2 * N_HEADS, D, HEAD_DIM), BF16), jax.ShapeDtypeStruct((SUBLANES, D), F32)],
        compiler_params=_params(1),
    )(dkv, mem, mem_g, mn16, wkv_g)


def _rmsnorm_fwd(x, norm_g, tm):
    n_rows = x.shape[0]

    def body(x_ref, g_ref, u_ref, ut_ref):
        xf = x_ref[...]
        u = (xf * lax.rsqrt(_mean(xf * xf) + EPS)) * g_ref[...]
        u_ref[...] = u.astype(BF16)
        ut_ref[...] = u.T.astype(BF16)

    return pl.pallas_call(
        body, name="rmsnorm_fwd", grid=(n_rows // tm,),
        in_specs=[_rows(tm), _const((1, D))],
        out_specs=[_rows(tm), pl.BlockSpec((D, tm), lambda i: (0, i))],
        out_shape=[jax.ShapeDtypeStruct((n_rows, D), BF16), jax.ShapeDtypeStruct((D, n_rows), BF16)],
        compiler_params=_params(1),
    )(x, norm_g)


def _place():
    x, y, c = lax.axis_index("x"), lax.axis_index("y"), lax.axis_index("c")
    other_chips = [(1 - x, y), (x, 1 - y), (1 - x, 1 - y)]
    return x, y, c, other_chips


def _arrival_order():
    x, y, c, chips = _place()
    order = [4 * x + 2 * y + c, 4 * x + 2 * y + 1 - c]
    for px, py in chips:
        order += [4 * px + 2 * py + c, 4 * px + 2 * py + 1 - c]
    return order


def _proj_fwd_gather(u16, blocks, tm):
    n = len(blocks)
    n_rows = u16.shape[0]
    n_i = n_rows // tm
    per_shard = W_IN_SHARD // UNIT
    assert n_i >= per_shard

    def wt_index(p, i, order):
        return (_dp_unit(per_shard * order[p] + jnp.minimum(i, per_shard - 1)), 0)

    def body(order_ref, u_ref, *refs):
        src, proj_ref, wt_ref, out = refs[:n], refs[n], refs[n + 1], refs[n + 2:2 * n + 2]
        wbuf, stage_sem, send, recv, own_sem = refs[2 * n + 2:]
        p, i = pl.program_id(0), pl.program_id(1)
        x, y, c, chips = _place()
        me, sibling = 4 * x + 2 * y + c, (x, y, 1 - c)

        def copy(t, k, block, to, from_input=False):
            return pltpu.make_async_remote_copy(
                src_ref=src[t] if from_input else out[t].at[block], dst_ref=out[t].at[block],
                send_sem=send.at[t, k], recv_sem=recv.at[t, k], device_id=to, device_id_type=MESH)

        def own_copies():
            return [pltpu.make_async_copy(src[t], out[t].at[me], own_sem.at[t]) for t in range(n)]

        def first_copies():
            first = []
            for t in range(n):
                first.append(copy(t, 0, me, sibling, from_input=True))
                first += [copy(t, 1 + j, me, (*chip, c), from_input=True) for j, chip in enumerate(chips)]
            return first

        def stage(slot, block):
            return pltpu.make_async_copy(out[0].at[block], wbuf.at[slot], stage_sem.at[slot])

        @pl.when((p == 0) & (i == 0))
        def _():
            for cp in own_copies() + first_copies():
                cp.start()
            mine = pltpu.make_async_copy(src[0], wbuf.at[0], stage_sem.at[0])
            mine.start()
            mine.wait()

        @pl.when((p > 0) & (i == 0))
        def _():
            stage(p % 2, order_ref[p]).wait()

        proj_ref[...] = _dot(u_ref[...], wbuf[p % 2])
        for r in range(per_shard):
            @pl.when(i == r)
            def _(r=r):
                wt_ref[...] = wbuf[p % 2, :, r * UNIT:(r + 1) * UNIT].astype(F32).T.astype(BF16)

        for nxt in range(1, N_DEV):
            @pl.when((p == nxt - 1) & (i == n_i - 1))
            def _(nxt=nxt):
                if nxt == 1:
                    block = 4 * x + 2 * y + 1 - c
                    copy(0, 0, block, sibling).wait_recv()
                else:
                    j, passed_on = divmod(nxt - 2, 2)
                    px, py = chips[j]
                    if passed_on:
                        block = 4 * px + 2 * py + 1 - c
                        copy(0, 4 + j, block, sibling).wait_recv()
                    else:
                        block = 4 * px + 2 * py + c
                        copy(0, 1 + j, block, sibling).wait_recv()
                        copy(0, 4 + j, block, sibling).start()
                stage(nxt % 2, block).start()

        @pl.when((p == N_DEV - 1) & (i == n_i - 1))
        def _():
            passed = [copy(0, 4 + j, 4 * px + 2 * py + c, sibling) for j, (px, py) in enumerate(chips)]
            for j, (px, py) in enumerate(chips):
                for t in range(1, n):
                    block = 4 * px + 2 * py + c
                    copy(t, 1 + j, block, sibling).wait_recv()
                    passed.append(copy(t, 4 + j, block, sibling))
                    passed[-1].start()
            for t in range(1, n):
                copy(t, 0, 4 * x + 2 * y + 1 - c, sibling).wait_recv()
                for j, (px, py) in enumerate(chips):
                    copy(t, 4 + j, 4 * px + 2 * py + 1 - c, sibling).wait_recv()
            for cp in first_copies() + passed:
                cp.wait_send()
            for cp in own_copies():
                cp.wait()

    return pl.pallas_call(
        body, name="proj_fwd_gather",
        grid_spec=pltpu.PrefetchScalarGridSpec(
            num_scalar_prefetch=1, grid=(N_DEV, n_i),
            in_specs=[pl.BlockSpec((tm, D), lambda p, i, order: (i, 0))] + [ANY] * n,
            out_specs=[pl.BlockSpec((tm, W_IN_SHARD), lambda p, i, order: (i, order[p])),
                       pl.BlockSpec((UNIT, D), wt_index)] + [ANY] * n,
            scratch_shapes=[pltpu.VMEM((2, D, W_IN_SHARD), BF16), pltpu.SemaphoreType.DMA((2,)),
                            pltpu.SemaphoreType.DMA((n, 7)), pltpu.SemaphoreType.DMA((n, 7)),
                            pltpu.SemaphoreType.DMA((n,))]),
        out_shape=[jax.ShapeDtypeStruct((n_rows, N_GROUPS * D), F32), jax.ShapeDtypeStruct((N_GROUPS * D, D), BF16)]
        + [jax.ShapeDtypeStruct((N_DEV, *b.shape), b.dtype) for b in blocks],
        compiler_params=_params(2),
    )(jnp.stack(_arrival_order()).astype(jnp.int32), u16, *blocks)


def _branch_a_fwd(proj, wo4_g, cw_a, tm):
    n_rows = proj.shape[0]

    def body(bp, cp, xp, za, cph, xph, w_ref, cw_ref, sa_ref, ya_ref, e_scr, o_scr):
        i = pl.program_id(0)
        _to_time_major(e_scr, 0, jnp.where(i > 0, _f32(cph) * _f32(xph), 0.0))
        for r0 in range(0, tm, ELEM_ROWS):
            rows = pl.ds(r0, ELEM_ROWS)
            _to_time_major(e_scr, HALO + r0, cp[rows, :].astype(F32) * xp[rows, :].astype(F32))
        _conv(o_scr, e_scr, cw_ref, FWD_TAPS_A, tm)
        for r0 in range(0, tm, ELEM_ROWS):
            rows = pl.ds(r0, ELEM_ROWS)
            ca = _from_time_major(o_scr, ELEM_ROWS, r0)
            sa_ref[rows, :] = (jax.nn.silu(za[rows, :].astype(F32)) * (bp[rows, :].astype(F32) * ca)).astype(BF16)
        ya_ref[...] = _dot(sa_ref[...], w_ref[...].reshape(D, D))

    return pl.pallas_call(
        body, name="branch_a_fwd", grid=(n_rows // tm,),
        in_specs=[_rows(tm, G_BA), _rows(tm, G_CA), _rows(tm, G_XA), _rows(tm, G_ZA),
                  _prev_halo(tm, G_CA), _prev_halo(tm, G_XA), _w_out_spec(0), _const(cw_a.shape)],
        out_specs=[_rows(tm), _rows(tm)],
        out_shape=[jax.ShapeDtypeStruct((n_rows, D), BF16), jax.ShapeDtypeStruct((n_rows, D), F32)],
        scratch_shapes=[_time_major(tm + HALO), _time_major(tm)],
        compiler_params=_params(1),
    )(proj, proj, proj, proj, proj, proj, wo4_g, cw_a)


def _layernorm_parts(cb, lg, lb):
    xc = cb - _mean(cb)
    rstd = lax.rsqrt(_mean(xc * xc) + EPS)
    xhat = xc * rstd
    return xhat, rstd, xhat * lg + lb


def _branch_b_fwd(proj, wo4_g, cw_b, conv_b_b, ln_g, ln_b, tm):
    n_rows = proj.shape[0]

    def body(vb, gb, zb, vbh, gbh, w_ref, cw_ref, bb_ref, lg_ref, lb_ref, cb_ref, sb_ref, yb_ref, e_scr, o_scr):
        i = pl.program_id(0)
        vbh, gbh = _f32(vbh), _f32(gbh)
        lg, lb = lg_ref[...], lb_ref[...]
        _to_time_major(e_scr, 0, jnp.where(i > 0, vbh * jax.nn.sigmoid(gbh), 0.0))
        for r0 in range(0, tm, ELEM_ROWS):
            rows = pl.ds(r0, ELEM_ROWS)
            _to_time_major(e_scr, HALO + r0, vb[rows, :].astype(F32) * jax.nn.sigmoid(gb[rows, :].astype(F32)))
        _conv(o_scr, e_scr, cw_ref, FWD_TAPS_B, tm, bias_ref=bb_ref)
        for r0 in range(0, tm, ELEM_ROWS):
            rows = pl.ds(r0, ELEM_ROWS)
            cb = _from_time_major(o_scr, ELEM_ROWS, r0)
            cb_ref[rows, :] = cb
            _, _, ln = _layernorm_parts(cb, lg, lb)
            sb_ref[rows, :] = (jax.nn.silu(zb[rows, :].astype(F32)) * jax.nn.silu(ln)).astype(BF16)
        yb_ref[...] = _dot(sb_ref[...], w_ref[...].reshape(D, D))

    return pl.pallas_call(
        body, name="branch_b_fwd", grid=(n_rows // tm,),
        in_specs=[_rows(tm, G_VB), _rows(tm, G_GB), _rows(tm, G_ZB), _prev_halo(tm, G_VB), _prev_halo(tm, G_GB),
                  _w_out_spec(1), _const(cw_b.shape), _const((LANE_GROUPS, LANES)), _const((1, D)), _const((1, D))],
        out_specs=[_rows(tm), _rows(tm), _rows(tm)],
        out_shape=[jax.ShapeDtypeStruct((n_rows, D), F32), jax.ShapeDtypeStruct((n_rows, D), BF16),
                   jax.ShapeDtypeStruct((n_rows, D), F32)],
        scratch_shapes=[_time_major(tm + HALO), _time_major(tm)],
        compiler_params=_params(1),
    )(proj, proj, proj, proj, proj, wo4_g, cw_b, conv_b_b.reshape(LANE_GROUPS, LANES), ln_g, ln_b)


def _attention(q16, kv_ref):
    probs, outs = [], []
    for h in range(N_HEADS):
        s = _dot_nt(q16[:, h * HEAD_DIM:(h + 1) * HEAD_DIM], kv_ref[h]) * (HEAD_DIM ** -0.5)
        e = jnp.exp(s - jnp.max(s, axis=-1, keepdims=True))
        p = e / jnp.sum(e, axis=-1, keepdims=True)
        probs.append(p)
        outs.append(_dot(p.astype(BF16), kv_ref[N_HEADS + h]))
    return probs, outs


def _branch_x_fwd(proj, kv16, wo4_g, tm):
    n_rows = proj.shape[0]

    def body(q, zx, kv_ref, w_ref, sx_ref, yx_ref, p_ref):
        probs, outs = _attention(q[...].astype(BF16), kv_ref)
        p_ref[...] = jnp.concatenate(probs, axis=-1)
        sx = (jax.nn.silu(_f32(zx)) * jnp.concatenate(outs, axis=-1)).astype(BF16)
        sx_ref[...] = sx
        yx_ref[...] = _dot(sx, w_ref[...].reshape(D, D))

    return pl.pallas_call(
        body, name="branch_x_fwd", grid=(n_rows // tm,),
        in_specs=[_rows(tm, G_Q), _rows(tm, G_ZX), _const(kv16.shape), _w_out_spec(2)],
        out_specs=[_rows(tm), _rows(tm), _rows(tm)],
        out_shape=[jax.ShapeDtypeStruct((n_rows, D), BF16), jax.ShapeDtypeStruct((n_rows, D), F32),
                   jax.ShapeDtypeStruct((n_rows, D), F32)],
        compiler_params=_params(1),
    )(proj, proj, kv16, wo4_g)


def _merge_fwd_bwd(proj, ya, yb, yx, x, target, wo4_g, final_g, tm):
    n_rows = proj.shape[0]
    inv_d = 1.0 / D

    def body(ga, gb, gx, ya_ref, yb_ref, yx_ref, x_ref, t_ref, w_ref, fg_ref,
             dh_ref, dya_ref, dyb_ref, dyx_ref, dp_ref, dw_ref, dfg_ref, sq_ref):
        i = pl.program_id(0)
        wo = w_ref[...].reshape(D, D)
        sig = [jax.nn.sigmoid(_f32(g)) for g in (ga, gb, gx)]
        ys = [ya_ref[...], yb_ref[...], yx_ref[...]]
        m16 = (sig[0] * ys[0] + sig[1] * ys[1] + sig[2] * ys[2]).astype(BF16)
        h = x_ref[...] + _dot(m16, wo)
        r = lax.rsqrt(_mean(h * h) + EPS)
        hn = h * r
        fg = fg_ref[...]
        err = hn * fg - t_ref[...]
        dy = err * inv_d
        dhn = dy * fg
        dh = r * (dhn - hn * _mean(dhn * hn))
        dh_ref[...] = dh
        dh16 = dh.astype(BF16)
        dm = _dot_nt(dh16, wo)
        for n, out in enumerate((dya_ref, dyb_ref, dyx_ref)):
            out[...] = (sig[n] * dm).astype(BF16)
            dp_ref[:, n * D:(n + 1) * D] = (dm * ys[n] * (sig[n] * (1.0 - sig[n]))).astype(BF16)

        @pl.when(i == 0)
        def _():
            dw_ref[...] = jnp.zeros_like(dw_ref)
            dfg_ref[...] = jnp.zeros_like(dfg_ref)
            sq_ref[...] = jnp.zeros_like(sq_ref)

        dw_ref[0] += _dot_tn(m16, dh16)
        dfg_ref[...] += _fold8(dy * hn)
        sq_ref[...] += _fold8(err * err)

    vec = jax.ShapeDtypeStruct((SUBLANES, D), F32)
    return pl.pallas_call(
        body, name="merge_fwd_bwd", grid=(n_rows // tm,),
        in_specs=[_rows(tm, G_GA), _rows(tm, G_GBB), _rows(tm, G_GX), _rows(tm), _rows(tm), _rows(tm), _rows(tm),
                  _rows(tm), _w_out_spec(3), _const((1, D))],
        out_specs=[_rows(tm), _rows(tm), _rows(tm), _rows(tm), pl.BlockSpec((tm, 3 * D), lambda i: (i, 3)),
                   pl.BlockSpec((1, D, D), lambda i: (3, 0, 0)), _const((SUBLANES, D)), _const((SUBLANES, D))],
        out_shape=[jax.ShapeDtypeStruct((n_rows, D), F32), jax.ShapeDtypeStruct((n_rows, D), BF16),
                   jax.ShapeDtypeStruct((n_rows, D), BF16), jax.ShapeDtypeStruct((n_rows, D), BF16),
                   jax.ShapeDtypeStruct((n_rows, N_GROUPS * D), BF16), jax.ShapeDtypeStruct((4, D, D), F32), vec, vec],
        compiler_params=_params(1),
    )(proj, proj, proj, ya, yb, yx, x, target, wo4_g, final_g)


def _branch_a_bwd(dya, proj, sa16, wo4_g, cw_a, dproj, dw4, tm):
    n_rows = proj.shape[0]
    n_tiles = n_rows // tm

    def body(dya_ref, bp, cp, xp, za, sa_ref, dyan, bpn, zan, cph, xph, w_ref, cw_ref, dp_in, dw_in,
             dp_ref, dw_ref, dwa_ref, e1, e2, o_scr, mm_scr):
        del dp_in, dw_in
        i = pl.program_id(0)
        bpn, zan, cph, xph = (_f32(r) for r in (bpn, zan, cph, xph))
        woa = w_ref[...].reshape(D, D)
        dya16 = dya_ref[...]
        chunks = [pl.ds(r0, ELEM_ROWS) for r0 in range(0, tm, ELEM_ROWS)]
        _to_time_major(e1, 0, jnp.where(i > 0, cph * xph, 0.0))
        for r0, rows in zip(range(0, tm, ELEM_ROWS), chunks):
            _to_time_major(e1, HALO + r0, cp[rows, :].astype(F32) * xp[rows, :].astype(F32))
        _conv(o_scr, e1, cw_ref, FWD_TAPS_A, tm)
        mm_scr[...] = _dot_nt(dya16, woa)
        for r0, rows in zip(range(0, tm, ELEM_ROWS), chunks):
            ca = _from_time_major(o_scr, ELEM_ROWS, r0)
            dsa = mm_scr[rows, :]
            b = bp[rows, :].astype(F32)
            silu_z, dsilu_z = _silu_and_grad(za[rows, :].astype(F32))
            t = dsa * silu_z
            dp_ref[rows, 0 * D:1 * D] = (t * ca).astype(BF16)
            dp_ref[rows, 3 * D:4 * D] = (dsa * (b * ca) * dsilu_z).astype(BF16)
            _to_time_major(e2, r0, t * b)
        dcan = (_dot_nt(dyan[...], woa) * jax.nn.silu(zan)) * bpn
        _to_time_major(e2, tm, jnp.where(i < n_tiles - 1, dcan, 0.0))

        @pl.when(i == 0)
        def _():
            dw_ref[...] = jnp.zeros_like(dw_ref)
            dwa_ref[...] = jnp.zeros_like(dwa_ref)

        _conv_wgrad(dwa_ref, e2, e1, FWD_TAPS_A, tm)
        dw_ref[0] += _dot_tn(sa_ref[...], dya16)
        _conv(o_scr, e2, cw_ref, BWD_TAPS_A, tm)
        for r0, rows in zip(range(0, tm, ELEM_ROWS), chunks):
            dprod = _from_time_major(o_scr, ELEM_ROWS, r0)
            dp_ref[rows, 1 * D:2 * D] = (dprod * xp[rows, :].astype(F32)).astype(BF16)
            dp_ref[rows, 2 * D:3 * D] = (dprod * cp[rows, :].astype(F32)).astype(BF16)

    return pl.pallas_call(
        body, name="branch_a_bwd", grid=(n_tiles,),
        in_specs=[_rows(tm), _rows(tm, G_BA), _rows(tm, G_CA), _rows(tm, G_XA), _rows(tm, G_ZA), _rows(tm),
                  _next_halo(tm, n_rows), _next_halo(tm, n_rows, G_BA), _next_halo(tm, n_rows, G_ZA),
                  _prev_halo(tm, G_CA), _prev_halo(tm, G_XA), _w_out_spec(0), _const(cw_a.shape), ANY, ANY],
        out_specs=[pl.BlockSpec((tm, 4 * D), lambda i: (i, 0)), pl.BlockSpec((1, D, D), lambda i: (0, 0, 0)),
                   _const((K_A * LANE_GROUPS, LANES))],
        out_shape=[jax.ShapeDtypeStruct(dproj.shape, BF16), jax.ShapeDtypeStruct(dw4.shape, F32),
                   jax.ShapeDtypeStruct((K_A * LANE_GROUPS, LANES), F32)],
        input_output_aliases={13: 0, 14: 1},
        scratch_shapes=[_time_major(tm + HALO), _time_major(tm + HALO), _time_major(tm), pltpu.VMEM((tm, D), F32)],
        compiler_params=_params(1),
    )(dya, proj, proj, proj, proj, sa16, dya, proj, proj, proj, proj, wo4_g, cw_a, dproj, dw4)


def _branch_b_bwd(dyb, proj, cb, sb16, wo4_g, cw_b, ln_g, ln_b, dproj, dw4, tm):
    n_rows = proj.shape[0]
    n_tiles = n_rows // tm

    def body(dyb_ref, zb, cb_ref, vb, gb, sb_ref, dybn, zbn, cbn, vbh, gbh, w_ref, cw_ref, lg_ref, lb_ref,
             dp_in, dw_in, dp_ref, dw_ref, dwb_ref, dbb_ref, dlg_ref, dlb_ref, e1, e2, o_scr, mm_scr):
        del dp_in, dw_in
        zbn, vbh, gbh = (_f32(r) for r in (zbn, vbh, gbh))
        i = pl.program_id(0)
        wob = w_ref[...].reshape(D, D)
        lg, lb = lg_ref[...], lb_ref[...]

        def conv_out_grad(dsb, z, c):
            xhat, rstd, ln = _layernorm_parts(c, lg, lb)
            sw, dsw = _silu_and_grad(ln)
            sz, dsz = _silu_and_grad(z)
            dln = (dsb * sz) * dsw
            dxhat = dln * lg
            dcb = rstd * (dxhat - _mean(dxhat) - xhat * _mean(dxhat * xhat))
            return dsb * sw * dsz, dln, xhat, dcb

        @pl.when(i == 0)
        def _():
            dw_ref[...] = jnp.zeros_like(dw_ref)
            dwb_ref[...] = jnp.zeros_like(dwb_ref)
            dbb_ref[...] = jnp.zeros_like(dbb_ref)
            dlg_ref[...] = jnp.zeros_like(dlg_ref)
            dlb_ref[...] = jnp.zeros_like(dlb_ref)

        dyb16 = dyb_ref[...]
        mm_scr[...] = _dot_nt(dyb16, wob)
        dlg, dlb, dbb = (jnp.zeros((SUBLANES, D), F32),) * 3
        for r0 in range(0, tm, ELEM_ROWS):
            rows = pl.ds(r0, ELEM_ROWS)
            dzb, dln, xhat, dcb = conv_out_grad(mm_scr[rows, :], zb[rows, :].astype(F32), cb_ref[rows, :])
            dp_ref[rows, 2 * D:3 * D] = dzb.astype(BF16)
            _to_time_major(e2, r0, dcb)
            dlg, dlb, dbb = dlg + _fold8(dln * xhat), dlb + _fold8(dln), dbb + _fold8(dcb)
        _, _, _, dcbn = conv_out_grad(_dot_nt(dybn[...], wob), zbn[...], cbn[...])
        _to_time_major(e2, tm, jnp.where(i < n_tiles - 1, dcbn, 0.0))
        dlg_ref[...] += dlg
        dlb_ref[...] += dlb
        dbb_ref[...] += dbb
        dw_ref[0] += _dot_tn(sb_ref[...], dyb16)
        _to_time_major(e1, 0, jnp.where(i > 0, vbh[...] * jax.nn.sigmoid(gbh[...]), 0.0))
        for r0 in range(0, tm, ELEM_ROWS):
            rows = pl.ds(r0, ELEM_ROWS)
            sg = jax.nn.sigmoid(gb[rows, :].astype(F32))
            mm_scr[rows, :] = sg
            _to_time_major(e1, HALO + r0, vb[rows, :].astype(F32) * sg)
        _conv_wgrad(dwb_ref, e2, e1, FWD_TAPS_B, tm)
        _conv(o_scr, e2, cw_ref, BWD_TAPS_B, tm)
        for r0 in range(0, tm, ELEM_ROWS):
            rows = pl.ds(r0, ELEM_ROWS)
            dglu = _from_time_major(o_scr, ELEM_ROWS, r0)
            sg = mm_scr[rows, :]
            dp_ref[rows, 0 * D:1 * D] = (dglu * sg).astype(BF16)
            dp_ref[rows, 1 * D:2 * D] = (dglu * vb[rows, :].astype(F32) * (sg * (1.0 - sg))).astype(BF16)

    vec = jax.ShapeDtypeStruct((SUBLANES, D), F32)
    return pl.pallas_call(
        body, name="branch_b_bwd", grid=(n_tiles,),
        in_specs=[_rows(tm), _rows(tm, G_ZB), _rows(tm), _rows(tm, G_VB), _rows(tm, G_GB), _rows(tm),
                  _next_halo(tm, n_rows), _next_halo(tm, n_rows, G_ZB), _next_halo(tm, n_rows),
                  _prev_halo(tm, G_VB), _prev_halo(tm, G_GB), _w_out_spec(1), _const(cw_b.shape), _const((1, D)),
                  _const((1, D)), ANY, ANY],
        out_specs=[pl.BlockSpec((tm, 3 * D), lambda i: (i, 2)), pl.BlockSpec((1, D, D), lambda i: (1, 0, 0)),
                   _const((K_B * LANE_GROUPS, LANES)), _const((SUBLANES, D)), _const((SUBLANES, D)),
                   _const((SUBLANES, D))],
        out_shape=[jax.ShapeDtypeStruct(dproj.shape, BF16), jax.ShapeDtypeStruct(dw4.shape, F32),
                   jax.ShapeDtypeStruct((K_B * LANE_GROUPS, LANES), F32), vec, vec, vec],
        input_output_aliases={15: 0, 16: 1},
        scratch_shapes=[_time_major(tm + HALO), _time_major(tm + HALO), _time_major(tm), pltpu.VMEM((tm, D), F32)],
        compiler_params=_params(1),
    )(dyb, proj, cb, proj, proj, sb16, dyb, proj, cb, proj, proj, wo4_g, cw_b, ln_g, ln_b, dproj, dw4)


def _branch_x_bwd(dyx, proj, sx16, probs, kv16, wo4_g, dproj, dw4, tm):
    n_rows = proj.shape[0]
    scale = HEAD_DIM ** -0.5

    def body(dyx_ref, q, zx, sx_ref, p_ref, kv_ref, w_ref, dp_in, dw_in, dp_ref, dw_ref, dkv_ref):
        del dp_in, dw_in
        i = pl.program_id(0)
        dyx16 = dyx_ref[...]
        q16 = q[...].astype(BF16)
        probs = [p_ref[:, h * HEAD_DIM:(h + 1) * HEAD_DIM] for h in range(N_HEADS)]
        outs = [_dot(probs[h].astype(BF16), kv_ref[N_HEADS + h]) for h in range(N_HEADS)]
        dsx = _dot_nt(dyx16, w_ref[...].reshape(D, D))
        silu_z, dsilu_z = _silu_and_grad(_f32(zx))
        dp_ref[:, D:2 * D] = (dsx * jnp.concatenate(outs, axis=-1) * dsilu_z).astype(BF16)
        do16 = (dsx * silu_z).astype(BF16)

        @pl.when(i == 0)
        def _():
            dw_ref[...] = jnp.zeros_like(dw_ref)
            dkv_ref[...] = jnp.zeros_like(dkv_ref)

        for h in range(N_HEADS):
            cols = slice(h * HEAD_DIM, (h + 1) * HEAD_DIM)
            p = probs[h]
            dprob = _dot_nt(do16[:, cols], kv_ref[N_HEADS + h])
            ds16 = ((p * (dprob - jnp.sum(p * dprob, axis=-1, keepdims=True))) * scale).astype(BF16)
            dp_ref[:, cols] = _dot(ds16, kv_ref[h]).astype(BF16)
            dkv_ref[h] += _dot_tn(ds16, q16[:, cols])
            dkv_ref[N_HEADS + h] += _dot_tn(p.astype(BF16), do16[:, cols])
        dw_ref[0] += _dot_tn(sx_ref[...], dyx16)

    return pl.pallas_call(
        body, name="branch_x_bwd", grid=(n_rows // tm,),
        in_specs=[_rows(tm), _rows(tm, G_Q), _rows(tm, G_ZX), _rows(tm), _rows(tm), _const(kv16.shape),
                  _w_out_spec(2), ANY, ANY],
        out_specs=[pl.BlockSpec((tm, 2 * D), lambda i: (i, 2)), pl.BlockSpec((1, D, D), lambda i: (2, 0, 0)),
                   _const(kv16.shape)],
        out_shape=[jax.ShapeDtypeStruct(dproj.shape, BF16), jax.ShapeDtypeStruct(dw4.shape, F32),
                   jax.ShapeDtypeStruct(kv16.shape, F32)],
        input_output_aliases={7: 0, 8: 1},
        compiler_params=_params(1),
    )(dyx, proj, proj, sx16, probs, kv16, wo4_g, dproj, dw4)


def _dp_unit(u):
    g = u // 2
    pos = jnp.where(g < G_VB, g, jnp.where(g < G_Q, g + 2, jnp.where(g < G_GA, g - 3, g)))
    return 2 * pos + u % 2


def _scatter_copies(srcs, lands, send, recv):
    x, y, c = lax.axis_index("x"), lax.axis_index("y"), lax.axis_index("c")
    copies = []
    for n in range(N_DEV - 1):
        flip = n + 1
        px = 1 - x if flip & 4 else x
        py = 1 - y if flip & 2 else y
        pc = 1 - c if flip & 1 else c
        for t, (src, land) in enumerate(zip(srcs, lands)):
            copies.append(pltpu.make_async_remote_copy(
                src_ref=src.at[4 * px + 2 * py + pc], dst_ref=land.at[n], send_sem=send.at[t * (N_DEV - 1) + n],
                recv_sem=recv.at[t * (N_DEV - 1) + n], device_id=(px, py, pc), device_id_type=MESH))
    return copies


def _scatter_start(name, arrays, views, n_views):
    n = len(arrays)
    lands = [lax.empty(tuple(N_DEV - 1 if d == N_DEV else d for d in a.shape), a.dtype) for a in arrays]

    def body(*refs):
        src, land, (send, recv) = refs[:n], refs[n:2 * n], refs[2 * n:2 * n + 2]
        token = refs[-1]
        for cp in _scatter_copies(views(src), views(land), send, recv):
            cp.start()
        token[...] = jnp.zeros_like(token)

    sems = pltpu.SemaphoreType.DMA((n_views * (N_DEV - 1),))
    out = pl.pallas_call(
        body, name=name,
        in_specs=[HBM] * (2 * n),
        out_specs=[SEM, SEM] + [HBM] * (2 * n) + [pl.BlockSpec(memory_space=pltpu.VMEM)],
        out_shape=[sems, sems] + [pltpu.HBM(a.shape, a.dtype) for a in arrays + lands]
        + [jax.ShapeDtypeStruct((SUBLANES, LANES), F32)],
        input_output_aliases={k: 2 + k for k in range(2 * n)},
        compiler_params=pltpu.CompilerParams(has_side_effects=SIDE_EFFECT),
    )(*[pltpu.with_memory_space_constraint(a, pltpu.HBM) for a in arrays + lands])
    return dict(name=name, sems=out[:2], moving=out[2:2 + 2 * n], views=views, token=out[-1])


def _scatter_wait(started, after):
    n = len(started["moving"]) // 2
    views = started["views"]

    def body(*refs):
        src, land, (send, recv) = refs[:n], refs[n:2 * n], refs[2 * n:2 * n + 2]
        for cp in _scatter_copies(views(src), views(land), send, recv):
            cp.wait_send()
            cp.wait_recv()

    out = pl.pallas_call(
        body, name=started["name"].replace("start", "wait"),
        in_specs=[HBM] * (2 * n) + [SEM, SEM, ANY],
        out_specs=[HBM] * (2 * n),
        out_shape=[pltpu.HBM(a.shape, a.dtype) for a in started["moving"]],
        input_output_aliases={k: k for k in range(2 * n)},
        compiler_params=pltpu.CompilerParams(has_side_effects=SIDE_EFFECT),
    )(*started["moving"], *started["sems"], after)
    return out[n:]


def _w_in_grad(ut, dproj, tk, token):
    n_rows = dproj.shape[0]
    n_k = n_rows // tk
    per_shard = W_IN_SHARD // UNIT

    def body(ut_ref, dp0, dp1, dp2, token_ref, out_ref, out16_ref, acc):
        del token_ref
        t = pl.program_id(1)

        for r, dp_ref in enumerate((dp0, dp1, dp2)):
            cols = slice(r * UNIT, (r + 1) * UNIT)

            @pl.when(t == 0)
            def _(dp_ref=dp_ref, cols=cols):
                acc[:, cols] = _dot(ut_ref[...], dp_ref[...])

            @pl.when(t > 0)
            def _(dp_ref=dp_ref, cols=cols):
                acc[:, cols] += _dot(ut_ref[...], dp_ref[...])

        @pl.when(t == n_k - 1)
        def _():
            out_ref[0] = acc[...]
            out16_ref[0] = acc[...].astype(BF16)

    def dp_spec(r):
        return pl.BlockSpec((tk, UNIT), lambda q, t: (t, _dp_unit(per_shard * q + r)))

    shard = pl.BlockSpec((1, D, W_IN_SHARD), lambda q, t: (q, 0, 0))
    return pl.pallas_call(
        body, name="w_in_grad", grid=(N_DEV, n_k),
        in_specs=[pl.BlockSpec((D, tk), lambda q, t: (0, t)), dp_spec(0), dp_spec(1), dp_spec(2), ANY],
        out_specs=[shard, shard],
        out_shape=[jax.ShapeDtypeStruct((N_DEV, D, W_IN_SHARD), F32), jax.ShapeDtypeStruct((N_DEV, D, W_IN_SHARD), BF16)],
        scratch_shapes=[pltpu.VMEM((D, W_IN_SHARD), F32)],
        compiler_params=_params(2),
    )(ut, dproj, dproj, dproj, token)


def _x_grad(dproj, win_t, x, dh, norm_g, token, tm):
    n_rows = x.shape[0]
    n_k = N_GROUPS * D // X_GRAD_K

    def body(dp_ref, wt_ref, x_ref, dh_ref, g_ref, token_ref, gx_ref, dg_ref, acc):
        del token_ref
        i, g = pl.program_id(0), pl.program_id(1)

        @pl.when((i == 0) & (g == 0))
        def _():
            dg_ref[...] = jnp.zeros_like(dg_ref)

        @pl.when(g == 0)
        def _():
            acc[...] = _dot(dp_ref[...], wt_ref[...])

        @pl.when(g > 0)
        def _():
            acc[...] += _dot(dp_ref[...], wt_ref[...])

        @pl.when(g == n_k - 1)
        def _():
            du = acc[...]
            xf = x_ref[...]
            r = lax.rsqrt(_mean(xf * xf) + EPS)
            xn = xf * r
            dun = du * g_ref[...]
            gx_ref[...] = dh_ref[...] + r * (dun - xn * _mean(dun * xn))
            dg_ref[...] += _fold8(du * xn)

    return pl.pallas_call(
        body, name="x_grad", grid=(n_rows // tm, n_k),
        in_specs=[pl.BlockSpec((tm, X_GRAD_K), lambda i, g: (i, g)), pl.BlockSpec((X_GRAD_K, D), lambda i, g: (g, 0)),
                  pl.BlockSpec((tm, D), lambda i, g: (i, 0)), pl.BlockSpec((tm, D), lambda i, g: (i, 0)),
                  _const((1, D)), ANY],
        out_specs=[pl.BlockSpec((tm, D), lambda i, g: (i, 0)), _const((SUBLANES, D))],
        out_shape=[jax.ShapeDtypeStruct((n_rows, D), F32), jax.ShapeDtypeStruct((SUBLANES, D), F32)],
        scratch_shapes=[pltpu.VMEM((tm, D), F32)],
        compiler_params=_params(2),
    )(dproj, win_t, x, dh, norm_g, token)


def _local_step(x, mem, target, norm_g, conv_b_b, ln_g, ln_b, mem_g, final_g, shards):
    n_rows = x.shape[0]
    tm = min(512, n_rows)
    big = min(1024, n_rows)
    u16, ut = _rmsnorm_fwd(x, norm_g, big)
    proj, win_t, _, wkv_g, wo4_g, cw_g = _proj_fwd_gather(u16, shards, min(2048, n_rows))
    cw_rows = cw_g.transpose(1, 0, 2).reshape((SUBLANES + HALO) * LANE_GROUPS, LANES)
    cw_a, cw_b = cw_rows[:SUBLANES * LANE_GROUPS], cw_rows[SUBLANES * LANE_GROUPS:]
    kv16, mn16 = _kv_fwd(mem, mem_g, wkv_g)
    sa16, ya = _branch_a_fwd(proj, wo4_g, cw_a, tm)
    cb, sb16, yb = _branch_b_fwd(proj, wo4_g, cw_b, conv_b_b, ln_g, ln_b, tm)
    sx16, yx, probs = _branch_x_fwd(proj, kv16, wo4_g, big)
    dh, dya, dyb, dyx, dproj, dw4, dfg, sq = _merge_fwd_bwd(proj, ya, yb, yx, x, target, wo4_g, final_g,
                                                             min(256, n_rows))
    dproj, dw4, dwa = _branch_a_bwd(dya, proj, sa16, wo4_g, cw_a, dproj, dw4, tm)
    dproj, dw4, dwb, dbb, dlg, dlb = _branch_b_bwd(dyb, proj, cb, sb16, wo4_g, cw_b, ln_g, ln_b, dproj, dw4, tm)
    dproj, dw4, dkv = _branch_x_bwd(dyx, proj, sx16, probs, kv16, wo4_g, dproj, dw4, tm)
    dwkv_g, dwkv16, dmg = _kv_bwd(dkv, mem, mem_g, mn16, wkv_g)
    dw4 = dw4.reshape(4, N_DEV, D // N_DEV, D)
    small_moving = _scatter_start("small_grads_start", [dw4.astype(BF16), dwkv16],
                                  lambda refs: [refs[0].at[w] for w in range(4)] + [refs[1]], 5)
    dwin_g, dwin16 = _w_in_grad(ut, dproj, min(2048, n_rows), small_moving["token"])
    w_in_moving = _scatter_start("w_in_grad_start", [dwin16], lambda refs: list(refs), 1)
    gx, dng = _x_grad(dproj, win_t, x, dh, norm_g, w_in_moving["token"], big)
    land4, landkv = _scatter_wait(small_moving, dng)
    landin, = _scatter_wait(w_in_moving, dng)
    small = {SV_NORM_G: dng, SV_CONV_B_B: dbb, SV_LN_G: dlg, SV_LN_B: dlb, SV_MEM_G: dmg, SV_FINAL_G: dfg, SV_LOSS: sq}
    grads = [(dwin_g[None], landin[None]), (dw4, land4), (dwkv_g[None], landkv[None])]
    return gx, grads, small, dwa.reshape(K_A, D), dwb.reshape(K_B, D)


def _allgather_small(small, conv_rows):
    keys = sorted(small)

    def body(*refs):
        parts, (conv_ref, out_ref, mine, send, recv) = refs[:len(keys)], refs[len(keys):]
        x, y, c, chips = _place()
        me, sibling = 4 * x + 2 * y + c, (x, y, 1 - c)
        mine[pl.ds(0, SV_CONV_A), :] = jnp.zeros((SV_CONV_A, D), F32)
        for key, part in zip(keys, parts):
            mine[key:key + 1, :] = jnp.sum(part[...], axis=0, keepdims=True)
        mine[pl.ds(SV_CONV_A, SV_ROWS - SV_CONV_A), :] = conv_ref[...]
        out_ref[me] = mine[...]

        def copy(k, block, to, from_mine=False):
            return pltpu.make_async_remote_copy(
                src_ref=mine if from_mine else out_ref.at[block], dst_ref=out_ref.at[block],
                send_sem=send.at[k], recv_sem=recv.at[k], device_id=to, device_id_type=MESH)

        first = [copy(0, me, sibling, from_mine=True)]
        first += [copy(1 + j, me, (*chip, c), from_mine=True) for j, chip in enumerate(chips)]
        for cp in first:
            cp.start()
        passed = []
        for j, (px, py) in enumerate(chips):
            block = 4 * px + 2 * py + c
            copy(1 + j, block, sibling).wait_recv()
            passed.append(copy(4 + j, block, sibling))
            passed[-1].start()
        copy(0, 4 * x + 2 * y + 1 - c, sibling).wait_recv()
        for j, (px, py) in enumerate(chips):
            copy(4 + j, 4 * px + 2 * py + 1 - c, sibling).wait_recv()
        for cp in first + passed:
            cp.wait_send()

    vmem = pl.BlockSpec(memory_space=pltpu.VMEM)
    return pl.pallas_call(
        body, name="allgather_small",
        in_specs=[vmem] * (len(keys) + 1), out_specs=vmem,
        out_shape=jax.ShapeDtypeStruct((N_DEV, SV_ROWS, D), F32),
        scratch_shapes=[pltpu.VMEM((SV_ROWS, D), F32), pltpu.SemaphoreType.DMA((7,)), pltpu.SemaphoreType.DMA((7,))],
    )(*[small[k] for k in keys], conv_rows)


def _adamw(w, g, m, v):
    m = ADAM_B1 * m + (1.0 - ADAM_B1) * g
    v = ADAM_B2 * v + (1.0 - ADAM_B2) * (g * g)
    m_hat = m / (1.0 - ADAM_B1 ** ADAM_STEP)
    v_hat = v / (1.0 - ADAM_B2 ** ADAM_STEP)
    return -ADAM_LR * (m_hat / (jnp.sqrt(v_hat) + ADAM_EPS) + ADAM_WD * w), m, v


def _adamw_shard(own, landed, piece, k_arr, w, m, v, tr):
    n_r, n_c = w.shape
    n_landed = landed.shape[1]

    def body(k_ref, own_ref, *refs):
        del k_ref
        landed_refs, (w_ref, m_ref, v_ref, g_out, d_out, m_out, v_out) = refs[:n_landed], refs[n_landed:]
        g = own_ref[0, 0]
        for landed_ref in landed_refs:
            g = g + landed_ref[0, 0].astype(F32)
        g_out[...] = g
        d_out[...], m_out[...], v_out[...] = _adamw(w_ref[...], g, m_ref[...], v_ref[...])

    blk = (1, 1, tr, n_c)
    flat = pl.BlockSpec((tr, n_c), lambda r, k: (r, 0))
    return pl.pallas_call(
        body, name="adamw_shard",
        grid_spec=pltpu.PrefetchScalarGridSpec(
            num_scalar_prefetch=1, grid=(n_r // tr,),
            in_specs=[pl.BlockSpec(blk, lambda r, k: (piece, k[0], r, 0))]
            + [pl.BlockSpec(blk, functools.partial(lambda r, k, j: (piece, j, r, 0), j=j)) for j in range(n_landed)]
            + [flat] * 3,
            out_specs=[flat] * 4),
        out_shape=[jax.ShapeDtypeStruct((n_r, n_c), F32)] * 4,
        compiler_params=_params(1),
    )(k_arr, own, *([landed] * n_landed), w, m, v)


def _adamw_shards(entries, k_arr):
    n = len(entries)

    def body(k_ref, *refs):
        del k_ref
        ins, outs = refs[:5 * n], refs[5 * n:]
        for e in range(n):
            own_ref, landed_ref, w_ref, m_ref, v_ref = ins[5 * e:5 * e + 5]
            g = own_ref[0, 0]
            for j in range(landed_ref.shape[1]):
                g = g + landed_ref[0, j].astype(F32)
            g_out, d_out, m_out, v_out = outs[4 * e:4 * e + 4]
            g_out[...] = g
            d_out[...], m_out[...], v_out[...] = _adamw(w_ref[...], g, m_ref[...], v_ref[...])

    in_specs, operands, out_specs, out_shape = [], [], [], []
    for own, landed, piece, w, m, v in entries:
        flat = pl.BlockSpec(w.shape, lambda i, k: (0, 0))
        in_specs += [pl.BlockSpec((1, 1, *w.shape), functools.partial(lambda i, k, p: (p, k[0], 0, 0), p=piece)),
                     pl.BlockSpec((1, *landed.shape[1:]), functools.partial(lambda i, k, p: (p, 0, 0, 0), p=piece)),
                     flat, flat, flat]
        operands += [own, landed, w, m, v]
        out_specs += [flat] * 4
        out_shape += [jax.ShapeDtypeStruct(w.shape, F32)] * 4
    out = pl.pallas_call(
        body, name="adamw_shards",
        grid_spec=pltpu.PrefetchScalarGridSpec(num_scalar_prefetch=1, grid=(1,), in_specs=in_specs,
                                               out_specs=out_specs),
        out_shape=out_shape,
        compiler_params=_params(1),
    )(k_arr, *operands)
    return [tuple(out[4 * e:4 * e + 4]) for e in range(n)]


def _adamw_small(gathered, k_arr, vectors, conv_a, conv_b):
    n_vec = len(vectors)
    cols = D // N_DEV

    def body(k_ref, full_ref, cols_ref, *refs):
        del k_ref
        ins, outs = refs[:3 * (n_vec + 2)], refs[3 * (n_vec + 2):-2]
        tot, tot_cols = refs[-2:]
        tot[...] = full_ref[0]
        tot_cols[...] = cols_ref[0]
        for dev in range(1, N_DEV):
            tot[...] += full_ref[dev]
            tot_cols[...] += cols_ref[dev]
        loss = (0.5 / D) * jnp.sum(tot[SV_LOSS:SV_LOSS + 1, :])
        outs[0][...] = jnp.full(outs[0].shape, loss, F32)
        grads = [tot[n:n + 1, :] for n in range(n_vec)]
        grads += [tot_cols[pl.ds(SV_CONV_A, K_A), :], tot_cols[pl.ds(SV_CONV_B, K_B), :]]
        for n, g in enumerate(grads):
            w_ref, m_ref, v_ref = ins[3 * n:3 * n + 3]
            g_out, d_out, m_out, v_out = outs[1 + 4 * n:5 + 4 * n]
            g_out[...] = g
            d_out[...], m_out[...], v_out[...] = _adamw(w_ref[...], g, m_ref[...], v_ref[...])

    weights = list(vectors) + [conv_a, conv_b]
    flat_in = [a for wmv in weights for a in wmv]
    out_shape = [jax.ShapeDtypeStruct((SUBLANES, 128), F32)]
    for wmv in weights:
        out_shape += [jax.ShapeDtypeStruct(wmv[0].shape, F32)] * 4
    return pl.pallas_call(
        body, name="adamw_small",
        grid_spec=pltpu.PrefetchScalarGridSpec(
            num_scalar_prefetch=1, grid=(1,),
            in_specs=[pl.BlockSpec((N_DEV, SV_ROWS, D), lambda i, k: (0, 0, 0)),
                      pl.BlockSpec((N_DEV, SV_ROWS, cols), lambda i, k: (0, 0, k[0]))]
            + [pl.BlockSpec(a.shape, lambda i, k: (0, 0)) for a in flat_in],
            out_specs=[pl.BlockSpec(s.shape, lambda i, k: (0, 0)) for s in out_shape],
            scratch_shapes=[pltpu.VMEM((SV_ROWS, D), F32), pltpu.VMEM((SV_ROWS, cols), F32)]),
        out_shape=out_shape,
        compiler_params=_params(1),
    )(k_arr, gathered, gathered, *flat_in)


def kernel(x, mem, norm_g, w_in, conv_a_w, w_out_a, conv_b_w, conv_b_b, ln_b_g, ln_b_b, w_out_b, mem_norm_g, w_kv, w_out_x, w_o, final_g, loss_target, m_norm_g, m_w_in, m_conv_a_w, m_w_out_a, m_conv_b_w, m_conv_b_b, m_ln_b_g, m_ln_b_b, m_w_out_b, m_mem_norm_g, m_w_kv, m_w_out_x, m_w_o, m_final_g, v_norm_g, v_w_in, v_conv_a_w, v_w_out_a, v_conv_b_w, v_conv_b_b, v_ln_b_g, v_ln_b_b, v_w_out_b, v_mem_norm_g, v_w_kv, v_w_out_x, v_w_o, v_final_g):
    xi, yi, ci = lax.axis_index("x"), lax.axis_index("y"), lax.axis_index("c")
    k_arr = jnp.reshape(4 * xi + 2 * yi + ci, (1,)).astype(jnp.int32)

    cw = jnp.concatenate([jnp.pad(conv_a_w[0], ((0, SUBLANES - K_A), (0, 0))),
                          jnp.pad(conv_b_w[0], ((0, HALO - K_B), (0, 0)))], axis=0)
    wo4 = jnp.stack([w_out_a[0], w_out_b[0], w_out_x[0], w_o[0]]).astype(BF16)
    shards = [w_in[0].astype(BF16), w_kv[0].astype(BF16), wo4, cw]

    final_g2 = final_g.reshape(1, D)
    gx, grads, small, dwa, dwb = _local_step(
        x[0], mem[0], loss_target[0], norm_g, conv_b_b, ln_b_g, ln_b_b, mem_norm_g, final_g2, shards)

    conv_rows = jnp.concatenate([jnp.pad(dwa, ((0, SUBLANES - K_A), (0, 0))),
                                 jnp.pad(dwb, ((0, HALO - K_B), (0, 0)))], axis=0)
    gathered_small = _allgather_small(small, conv_rows)

    res = {"w_in": _adamw_shard(grads[0][0], grads[0][1], 0, k_arr, w_in[0], m_w_in[0], v_w_in[0], 256)}
    small_shards = [("w_out_a", 1, 0, w_out_a, m_w_out_a, v_w_out_a), ("w_out_b", 1, 1, w_out_b, m_w_out_b, v_w_out_b),
                    ("w_out_x", 1, 2, w_out_x, m_w_out_x, v_w_out_x), ("w_o", 1, 3, w_o, m_w_o, v_w_o),
                    ("w_kv", 2, 0, w_kv, m_w_kv, v_w_kv)]
    updated = _adamw_shards([(grads[a][0], grads[a][1], l, w[0], m[0], v[0]) for _, a, l, w, m, v in small_shards],
                            k_arr)
    res.update({name: four for (name, *_), four in zip(small_shards, updated)})
    res = {name: tuple(r[None] for r in four) for name, four in res.items()}
    vectors = [(norm_g, m_norm_g, v_norm_g), (conv_b_b, m_conv_b_b, v_conv_b_b), (ln_b_g, m_ln_b_g, v_ln_b_g),
               (ln_b_b, m_ln_b_b, v_ln_b_b), (mem_norm_g, m_mem_norm_g, v_mem_norm_g),
               (final_g2, m_final_g.reshape(1, D), v_final_g.reshape(1, D))]
    out = _adamw_small(gathered_small, k_arr, vectors, (conv_a_w[0], m_conv_a_w[0], v_conv_a_w[0]),
                       (conv_b_w[0], m_conv_b_w[0], v_conv_b_w[0]))
    loss = out[0][0, 0]
    names = ["norm_g", "conv_b_b", "ln_b_g", "ln_b_b", "mem_norm_g", "final_g", "conv_a_w", "conv_b_w"]
    for n, name in enumerate(names):
        four = out[1 + 4 * n:5 + 4 * n]
        if name == "final_g":
            four = [r.reshape(D) for r in four]
        elif name.startswith("conv_") and name.endswith("_w"):
            four = [r[None] for r in four]
        res[name] = tuple(four)

    order = ["norm_g", "w_in", "conv_a_w", "w_out_a", "conv_b_w", "conv_b_b", "ln_b_g", "ln_b_b", "w_out_b",
             "mem_norm_g", "w_kv", "w_out_x", "w_o", "final_g"]
    return (loss, gx[None], *[res[n][0] for n in order], *[res[n][1] for n in order],
            *[res[n][2] for n in order], *[res[n][3] for n in order])
```

```python
import functools

import jax
import jax.numpy as jnp
from jax import lax
from jax.experimental import pallas as pl
from jax.experimental.pallas import tpu as pltpu

F32, BF16 = jnp.float32, jnp.bfloat16
D = 1024
N_DEV = 8
N_HEADS = 4
HEAD_DIM = D // N_HEADS
N_GROUPS = 12
W_IN_SHARD = N_GROUPS * D // N_DEV
UNIT = 512
X_GRAD_K = 2 * D
K_A, K_B = 3, 31
EPS = 1e-6
HALO = 32
SUBLANES = 8
LANES = 128
LANE_GROUPS = D // LANES
TAP_GROUP = 16
CONV_BLOCK = 32
ELEM_ROWS = 16
CONV_PARTIAL_SUMS = 4
VMEM_LIMIT = 56 << 20
MESH = pl.DeviceIdType.MESH
ANY = pl.BlockSpec(memory_space=pl.ANY)
HBM = pl.BlockSpec(memory_space=pltpu.HBM)
SEM = pl.BlockSpec(memory_space=pltpu.SEMAPHORE)
SIDE_EFFECT = pltpu.SideEffectType.DATAFLOW_SIDE_EFFECTING

G_BA, G_CA, G_XA, G_ZA, G_VB, G_GB, G_ZB, G_Q, G_ZX, G_GA, G_GBB, G_GX = range(N_GROUPS)

ADAM_LR, ADAM_B1, ADAM_B2, ADAM_EPS, ADAM_WD, ADAM_STEP = 0.001, 0.9, 0.999, 1e-08, 0.01, 10

SV_NORM_G, SV_CONV_B_B, SV_LN_G, SV_LN_B, SV_MEM_G, SV_FINAL_G, SV_LOSS = range(7)
SV_CONV_A, SV_CONV_B, SV_ROWS = 8, 16, 48


def _dot(a, b):
    return jnp.dot(a, b, preferred_element_type=F32)


def _dot_nt(a, b):
    return lax.dot_general(a, b, (((1,), (1,)), ((), ())), preferred_element_type=F32)


def _dot_tn(a, b):
    return lax.dot_general(a, b, (((0,), (0,)), ((), ())), preferred_element_type=F32)


def _silu_and_grad(z):
    s = jax.nn.sigmoid(z)
    return z * s, s * (1.0 + z * (1.0 - s))


def _fold8(a):
    return a.reshape(a.shape[0] // SUBLANES, SUBLANES, a.shape[1]).sum(axis=0)


def _mean(a):
    return jnp.mean(a, axis=-1, keepdims=True)


def _f32(ref):
    return ref[...].astype(F32)


def _params(n_grid):
    return pltpu.CompilerParams(dimension_semantics=("arbitrary",) * n_grid, vmem_limit_bytes=VMEM_LIMIT)


def _rows(tm, col=0):
    return pl.BlockSpec((tm, D), lambda i: (i, col))


def _prev_halo(tm, col=0):
    return pl.BlockSpec((HALO, D), lambda i: (jnp.maximum(i * (tm // HALO) - 1, 0), col))


def _next_halo(tm, n_rows, col=0):
    last = n_rows // HALO - 1
    return pl.BlockSpec((HALO, D), lambda i: (jnp.minimum((i + 1) * (tm // HALO), last), col))


def _const(shape):
    return pl.BlockSpec(shape, lambda *_: (0,) * len(shape))


def _w_out_spec(which):
    return pl.BlockSpec((N_DEV, None, D // N_DEV, D), lambda *_: (0, which, 0, 0))


def _to_time_major(t_ref, row0, x):
    n = x.shape[0]
    for j in range(LANE_GROUPS):
        t_ref[pl.ds(row0 * LANE_GROUPS + j, n, stride=LANE_GROUPS), :] = x[:, j * LANES:(j + 1) * LANES]


def _from_time_major(t_ref, n, row0=0):
    return jnp.concatenate([t_ref[pl.ds(row0 * LANE_GROUPS + j, n, stride=LANE_GROUPS), :]
                            for j in range(LANE_GROUPS)], axis=-1)


def _row(ref, t):
    start = t * LANE_GROUPS
    if not isinstance(start, int):
        start = pl.multiple_of(start, LANE_GROUPS)
    return ref[pl.ds(start, LANE_GROUPS), :]


def _tap_groups(taps):
    return [taps[first:first + TAP_GROUP] for first in range(0, len(taps), TAP_GROUP)]


def _conv(o_ref, e_ref, w_ref, taps, n_rows, bias_ref=None):
    for n_group, group in enumerate(_tap_groups(taps)):
        weights = [_row(w_ref, k) for k, _ in group]

        def block(c, carry, n_group=n_group, group=group, weights=weights):
            t0 = c * CONV_BLOCK
            window = {}
            for t in range(CONV_BLOCK):
                parts = [None] * min(CONV_PARTIAL_SUMS, len(group))
                for n, (_, off) in enumerate(group):
                    if t + off not in window:
                        window[t + off] = _row(e_ref, t0 + t + off)
                    term = weights[n] * window[t + off]
                    parts[n % len(parts)] = term if parts[n % len(parts)] is None else parts[n % len(parts)] + term
                window.pop(t + min(off for _, off in group), None)
                while len(parts) > 1:
                    parts = [parts[n] + parts[n + 1] for n in range(0, len(parts) - 1, 2)] + parts[len(parts) & ~1:]
                out = parts[0]
                if n_group > 0:
                    out = out + _row(o_ref, t0 + t)
                elif bias_ref is not None:
                    out = out + bias_ref[...]
                o_ref[pl.ds(pl.multiple_of((t0 + t) * LANE_GROUPS, LANE_GROUPS), LANE_GROUPS), :] = out
            return carry

        lax.fori_loop(0, n_rows // CONV_BLOCK, block, 0)


def _conv_wgrad(dw_ref, d_ref, e_ref, taps, n_rows):
    for group in _tap_groups(taps):
        def block(c, sums, group=group):
            t0 = c * CONV_BLOCK
            sums = list(sums)
            window = {}
            for t in range(CONV_BLOCK):
                d = _row(d_ref, t0 + t)
                for n, (_, off) in enumerate(group):
                    if t + off not in window:
                        window[t + off] = _row(e_ref, t0 + t + off)
                    sums[n] = sums[n] + d * window[t + off]
                window.pop(t + min(off for _, off in group), None)
            return tuple(sums)

        sums = lax.fori_loop(0, n_rows // CONV_BLOCK, block, tuple(_row(dw_ref, k) for k, _ in group))
        for (k, _), total in zip(group, sums):
            dw_ref[pl.ds(k * LANE_GROUPS, LANE_GROUPS), :] = total


FWD_TAPS_A = [(k, HALO - (K_A - 1) + k) for k in range(K_A)]
BWD_TAPS_A = [(k, K_A - 1 - k) for k in range(K_A)]
FWD_TAPS_B = [(k, HALO - (K_B - 1) + k) for k in range(K_B)]
BWD_TAPS_B = [(k, K_B - 1 - k) for k in range(K_B)]


def _time_major(n_rows):
    return pltpu.VMEM((n_rows * LANE_GROUPS, LANES), F32)


def _kv_fwd(mem, mem_g, wkv_g):
    m_len = mem.shape[0]

    def body(mem_ref, g_ref, w_ref, kv_ref, mn_ref):
        mf = mem_ref[...]
        r = lax.rsqrt(_mean(mf * mf) + EPS)
        mn = ((mf * r) * g_ref[...]).astype(BF16)
        mn_ref[...] = mn
        for b in range(2 * N_HEADS):
            kv_ref[b] = _dot(mn, w_ref[b]).astype(BF16)

    return pl.pallas_call(
        body, name="kv_fwd", grid=(1,),
        in_specs=[_const((m_len, D)), _const((1, D)), _const((2 * N_HEADS, D, HEAD_DIM))],
        out_specs=[_const((2 * N_HEADS, m_len, HEAD_DIM)), _const((m_len, D))],
        out_shape=[jax.ShapeDtypeStruct((2 * N_HEADS, m_len, HEAD_DIM), BF16), jax.ShapeDtypeStruct((m_len, D), BF16)],
        compiler_params=_params(1),
    )(mem, mem_g, wkv_g)


def _kv_bwd(dkv, mem, mem_g, mn16, wkv_g):
    m_len = mem.shape[0]

    def body(dkv_ref, mem_ref, g_ref, mn_ref, w_ref, dw_ref, dw16_ref, dg_ref):
        mn = mn_ref[...]
        dmn = jnp.zeros((m_len, D), F32)
        for b in range(2 * N_HEADS):
            d16 = dkv_ref[b].astype(BF16)
            dw = _dot_tn(mn, d16)
            dw_ref[b] = dw
            dw16_ref[b] = dw.astype(BF16)
            dmn = dmn + _dot_nt(d16, w_ref[b])
        mf = mem_ref[...]
        r = lax.rsqrt(_mean(mf * mf) + EPS)
        dg_ref[...] = _fold8(dmn * (mf * r))

    return pl.pallas_call(
        body, name="kv_bwd", grid=(1,),
        in_specs=[_const((2 * N_HEADS, m_len, HEAD_DIM)), _const((m_len, D)), _const((1, D)), _const((m_len, D)),
                  _const((2 * N_HEADS, D, HEAD_DIM))],
        out_specs=[_const((2 * N_HEADS, D, HEAD_DIM)), _const((2 * N_HEADS, D, HEAD_DIM)), _const((SUBLANES, D))],
        out_shape=[jax.ShapeDtypeStruct((2 * N_HEADS, D, HEAD_DIM), F32),
                   jax.ShapeDtypeStruct((2 * N_HEADS, D, HEAD_DIM), BF16), jax.ShapeDtypeStruct((SUBLANES, D), F32)],
        compiler_params=_params(1),
    )(dkv, mem, mem_g, mn16, wkv_g)


def _rmsnorm_fwd(x, norm_g, tm):
    n_rows = x.shape[0]

    def body(x_ref, g_ref, u_ref, ut_ref):
        xf = x_ref[...]
        u = (xf * lax.rsqrt(_mean(xf * xf) + EPS)) * g_ref[...]
        u_ref[...] = u.astype(BF16)
        ut_ref[...] = u.T.astype(BF16)

    return pl.pallas_call(
        body, name="rmsnorm_fwd", grid=(n_rows // tm,),
        in_specs=[_rows(tm), _const((1, D))],
        out_specs=[_rows(tm), pl.BlockSpec((D, tm), lambda i: (0, i))],
        out_shape=[jax.ShapeDtypeStruct((n_rows, D), BF16), jax.ShapeDtypeStruct((D, n_rows), BF16)],
        compiler_params=_params(1),
    )(x, norm_g)


def _place():
    x, y, c = lax.axis_index("x"), lax.axis_index("y"), lax.axis_index("c")
    other_chips = [(1 - x, y), (x, 1 - y), (1 - x, 1 - y)]
    return x, y, c, other_chips


def _arrival_order():
    x, y, c, chips = _place()
    order = [4 * x + 2 * y + c, 4 * x + 2 * y + 1 - c]
    for px, py in chips:
        order += [4 * px + 2 * py + c, 4 * px + 2 * py + 1 - c]
    return order


def _proj_fwd_gather(u16, blocks, tm):
    n = len(blocks)
    n_rows = u16.shape[0]
    n_i = n_rows // tm
    per_shard = W_IN_SHARD // UNIT
    assert n_i >= per_shard

    def wt_index(p, i, order):
        return (_dp_unit(per_shard * order[p] + jnp.minimum(i, per_shard - 1)), 0)

    def body(order_ref, u_ref, *refs):
        src, proj_ref, wt_ref, out = refs[:n], refs[n], refs[n + 1], refs[n + 2:2 * n + 2]
        wbuf, stage_sem, send, recv, own_sem = refs[2 * n + 2:]
        p, i = pl.program_id(0), pl.program_id(1)
        x, y, c, chips = _place()
        me, sibling = 4 * x + 2 * y + c, (x, y, 1 - c)

        def copy(t, k, block, to, from_input=False):
            return pltpu.make_async_remote_copy(
                src_ref=src[t] if from_input else out[t].at[block], dst_ref=out[t].at[block],
                send_sem=send.at[t, k], recv_sem=recv.at[t, k], device_id=to, device_id_type=MESH)

        def own_copies():
            return [pltpu.make_async_copy(src[t], out[t].at[me], own_sem.at[t]) for t in range(n)]

        def first_copies():
            first = []
            for t in range(n):
                first.append(copy(t, 0, me, sibling, from_input=True))
                first += [copy(t, 1 + j, me, (*chip, c), from_input=True) for j, chip in enumerate(chips)]
            return first

        def stage(slot, block):
            return pltpu.make_async_copy(out[0].at[block], wbuf.at[slot], stage_sem.at[slot])

        @pl.when((p == 0) & (i == 0))
        def _():
            for cp in own_copies() + first_copies():
                cp.start()
            mine = pltpu.make_async_copy(src[0], wbuf.at[0], stage_sem.at[0])
            mine.start()
            mine.wait()

        @pl.when((p > 0) & (i == 0))
        def _():
            stage(p % 2, order_ref[p]).wait()

        proj_ref[...] = _dot(u_ref[...], wbuf[p % 2])
        for r in range(per_shard):
            @pl.when(i == r)
            def _(r=r):
                wt_ref[...] = wbuf[p % 2, :, r * UNIT:(r + 1) * UNIT].astype(F32).T.astype(BF16)

        for nxt in range(1, N_DEV):
            @pl.when((p == nxt - 1) & (i == n_i - 1))
            def _(nxt=nxt):
                if nxt == 1:
                    block = 4 * x + 2 * y + 1 - c
                    copy(0, 0, block, sibling).wait_recv()
                else:
                    j, passed_on = divmod(nxt - 2, 2)
                    px, py = chips[j]
                    if passed_on:
                        block = 4 * px + 2 * py + 1 - c
                        copy(0, 4 + j, block, sibling).wait_recv()
                    else:
                        block = 4 * px + 2 * py + c
                        copy(0, 1 + j, block, sibling).wait_recv()
                        copy(0, 4 + j, block, sibling).start()
                stage(nxt % 2, block).start()

        @pl.when((p == N_DEV - 1) & (i == n_i - 1))
        def _():
            passed = [copy(0, 4 + j, 4 * px + 2 * py + c, sibling) for j, (px, py) in enumerate(chips)]
            for j, (px, py) in enumerate(chips):
                for t in range(1, n):
                    block = 4 * px + 2 * py + c
                    copy(t, 1 + j, block, sibling).wait_recv()
                    passed.append(copy(t, 4 + j, block, sibling))
                    passed[-1].start()
            for t in range(1, n):
                copy(t, 0, 4 * x + 2 * y + 1 - c, sibling).wait_recv()
                for j, (px, py) in enumerate(chips):
                    copy(t, 4 + j, 4 * px + 2 * py + 1 - c, sibling).wait_recv()
            for cp in first_copies() + passed:
                cp.wait_send()
            for cp in own_copies():
                cp.wait()

    return pl.pallas_call(
        body, name="proj_fwd_gather",
        grid_spec=pltpu.PrefetchScalarGridSpec(
            num_scalar_prefetch=1, grid=(N_DEV, n_i),
            in_specs=[pl.BlockSpec((tm, D), lambda p, i, order: (i, 0))] + [ANY] * n,
            out_specs=[pl.BlockSpec((tm, W_IN_SHARD), lambda p, i, order: (i, order[p])),
                       pl.BlockSpec((UNIT, D), wt_index)] + [ANY] * n,
            scratch_shapes=[pltpu.VMEM((2, D, W_IN_SHARD), BF16), pltpu.SemaphoreType.DMA((2,)),
                            pltpu.SemaphoreType.DMA((n, 7)), pltpu.SemaphoreType.DMA((n, 7)),
                            pltpu.SemaphoreType.DMA((n,))]),
        out_shape=[jax.ShapeDtypeStruct((n_rows, N_GROUPS * D), F32), jax.ShapeDtypeStruct((N_GROUPS * D, D), BF16)]
        + [jax.ShapeDtypeStruct((N_DEV, *b.shape), b.dtype) for b in blocks],
        compiler_params=_params(2),
    )(jnp.stack(_arrival_order()).astype(jnp.int32), u16, *blocks)


def _branch_a_fwd(proj, wo4_g, cw_a, tm):
    n_rows = proj.shape[0]

    def body(bp, cp, xp, za, cph, xph, w_ref, cw_ref, sa_ref, ya_ref, e_scr, o_scr):
        i = pl.program_id(0)
        _to_time_major(e_scr, 0, jnp.where(i > 0, _f32(cph) * _f32(xph), 0.0))
        for r0 in range(0, tm, ELEM_ROWS):
            rows = pl.ds(r0, ELEM_ROWS)
            _to_time_major(e_scr, HALO + r0, cp[rows, :].astype(F32) * xp[rows, :].astype(F32))
        _conv(o_scr, e_scr, cw_ref, FWD_TAPS_A, tm)
        for r0 in range(0, tm, ELEM_ROWS):
            rows = pl.ds(r0, ELEM_ROWS)
            ca = _from_time_major(o_scr, ELEM_ROWS, r0)
            sa_ref[rows, :] = (jax.nn.silu(za[rows, :].astype(F32)) * (bp[rows, :].astype(F32) * ca)).astype(BF16)
        ya_ref[...] = _dot(sa_ref[...], w_ref[...].reshape(D, D))

    return pl.pallas_call(
        body, name="branch_a_fwd", grid=(n_rows // tm,),
        in_specs=[_rows(tm, G_BA), _rows(tm, G_CA), _rows(tm, G_XA), _rows(tm, G_ZA),
                  _prev_halo(tm, G_CA), _prev_halo(tm, G_XA), _w_out_spec(0), _const(cw_a.shape)],
        out_specs=[_rows(tm), _rows(tm)],
        out_shape=[jax.ShapeDtypeStruct((n_rows, D), BF16), jax.ShapeDtypeStruct((n_rows, D), F32)],
        scratch_shapes=[_time_major(tm + HALO), _time_major(tm)],
        compiler_params=_params(1),
    )(proj, proj, proj, proj, proj, proj, wo4_g, cw_a)


def _layernorm_parts(cb, lg, lb):
    xc = cb - _mean(cb)
    rstd = lax.rsqrt(_mean(xc * xc) + EPS)
    xhat = xc * rstd
    return xhat, rstd, xhat * lg + lb


def _branch_b_fwd(proj, wo4_g, cw_b, conv_b_b, ln_g, ln_b, tm):
    n_rows = proj.shape[0]

    def body(vb, gb, zb, vbh, gbh, w_ref, cw_ref, bb_ref, lg_ref, lb_ref, cb_ref, sb_ref, yb_ref, e_scr, o_scr):
        i = pl.program_id(0)
        vbh, gbh = _f32(vbh), _f32(gbh)
        lg, lb = lg_ref[...], lb_ref[...]
        _to_time_major(e_scr, 0, jnp.where(i > 0, vbh * jax.nn.sigmoid(gbh), 0.0))
        for r0 in range(0, tm, ELEM_ROWS):
            rows = pl.ds(r0, ELEM_ROWS)
            _to_time_major(e_scr, HALO + r0, vb[rows, :].astype(F32) * jax.nn.sigmoid(gb[rows, :].astype(F32)))
        _conv(o_scr, e_scr, cw_ref, FWD_TAPS_B, tm, bias_ref=bb_ref)
        for r0 in range(0, tm, ELEM_ROWS):
            rows = pl.ds(r0, ELEM_ROWS)
            cb = _from_time_major(o_scr, ELEM_ROWS, r0)
            cb_ref[rows, :] = cb
            _, _, ln = _layernorm_parts(cb, lg, lb)
            sb_ref[rows, :] = (jax.nn.silu(zb[rows, :].astype(F32)) * jax.nn.silu(ln)).astype(BF16)
        yb_ref[...] = _dot(sb_ref[...], w_ref[...].reshape(D, D))

    return pl.pallas_call(
        body, name="branch_b_fwd", grid=(n_rows // tm,),
        in_specs=[_rows(tm, G_VB), _rows(tm, G_GB), _rows(tm, G_ZB), _prev_halo(tm, G_VB), _prev_halo(tm, G_GB),
                  _w_out_spec(1), _const(cw_b.shape), _const((LANE_GROUPS, LANES)), _const((1, D)), _const((1, D))],
        out_specs=[_rows(tm), _rows(tm), _rows(tm)],
        out_shape=[jax.ShapeDtypeStruct((n_rows, D), F32), jax.ShapeDtypeStruct((n_rows, D), BF16),
                   jax.ShapeDtypeStruct((n_rows, D), F32)],
        scratch_shapes=[_time_major(tm + HALO), _time_major(tm)],
        compiler_params=_params(1),
    )(proj, proj, proj, proj, proj, wo4_g, cw_b, conv_b_b.reshape(LANE_GROUPS, LANES), ln_g, ln_b)


def _attention(q16, kv_ref):
    probs, outs = [], []
    for h in range(N_HEADS):
        s = _dot_nt(q16[:, h * HEAD_DIM:(h + 1) * HEAD_DIM], kv_ref[h]) * (HEAD_DIM ** -0.5)
        e = jnp.exp(s - jnp.max(s, axis=-1, keepdims=True))
        p = e / jnp.sum(e, axis=-1, keepdims=True)
        probs.append(p)
        outs.append(_dot(p.astype(BF16), kv_ref[N_HEADS + h]))
    return probs, outs


def _branch_x_fwd(proj, kv16, wo4_g, tm):
    n_rows = proj.shape[0]

    def body(q, zx, kv_ref, w_ref, sx_ref, yx_ref, p_ref):
        probs, outs = _attention(q[...].astype(BF16), kv_ref)
        p_ref[...] = jnp.concatenate(probs, axis=-1)
        sx = (jax.nn.silu(_f32(zx)) * jnp.concatenate(outs, axis=-1)).astype(BF16)
        sx_ref[...] = sx
        yx_ref[...] = _dot(sx, w_ref[...].reshape(D, D))

    return pl.pallas_call(
        body, name="branch_x_fwd", grid=(n_rows // tm,),
        in_specs=[_rows(tm, G_Q), _rows(tm, G_ZX), _const(kv16.shape), _w_out_spec(2)],
        out_specs=[_rows(tm), _rows(tm), _rows(tm)],
        out_shape=[jax.ShapeDtypeStruct((n_rows, D), BF16), jax.ShapeDtypeStruct((n_rows, D), F32),
                   jax.ShapeDtypeStruct((n_rows, D), F32)],
        compiler_params=_params(1),
    )(proj, proj, kv16, wo4_g)


def _merge_fwd_bwd(proj, ya, yb, yx, x, target, wo4_g, final_g, tm):
    n_rows = proj.shape[0]
    inv_d = 1.0 / D

    def body(ga, gb, gx, ya_ref, yb_ref, yx_ref, x_ref, t_ref, w_ref, fg_ref,
             dh_ref, dya_ref, dyb_ref, dyx_ref, dp_ref, dw_ref, dfg_ref, sq_ref):
        i = pl.program_id(0)
        wo = w_ref[...].reshape(D, D)
        sig = [jax.nn.sigmoid(_f32(g)) for g in (ga, gb, gx)]
        ys = [ya_ref[...], yb_ref[...], yx_ref[...]]
        m16 = (sig[0] * ys[0] + sig[1] * ys[1] + sig[2] * ys[2]).astype(BF16)
        h = x_ref[...] + _dot(m16, wo)
        r = lax.rsqrt(_mean(h * h) + EPS)
        hn = h * r
        fg = fg_ref[...]
        err = hn * fg - t_ref[...]
        dy = err * inv_d
        dhn = dy * fg
        dh = r * (dhn - hn * _mean(dhn * hn))
        dh_ref[...] = dh
        dh16 = dh.astype(BF16)
        dm = _dot_nt(dh16, wo)
        for n, out in enumerate((dya_ref, dyb_ref, dyx_ref)):
            out[...] = (sig[n] * dm).astype(BF16)
            dp_ref[:, n * D:(n + 1) * D] = (dm * ys[n] * (sig[n] * (1.0 - sig[n]))).astype(BF16)

        @pl.when(i == 0)
        def _():
            dw_ref[...] = jnp.zeros_like(dw_ref)
            dfg_ref[...] = jnp.zeros_like(dfg_ref)
            sq_ref[...] = jnp.zeros_like(sq_ref)

        dw_ref[0] += _dot_tn(m16, dh16)
        dfg_ref[...] += _fold8(dy * hn)
        sq_ref[...] += _fold8(err * err)

    vec = jax.ShapeDtypeStruct((SUBLANES, D), F32)
    return pl.pallas_call(
        body, name="merge_fwd_bwd", grid=(n_rows // tm,),
        in_specs=[_rows(tm, G_GA), _rows(tm, G_GBB), _rows(tm, G_GX), _rows(tm), _rows(tm), _rows(tm), _rows(tm),
                  _rows(tm), _w_out_spec(3), _const((1, D))],
        out_specs=[_rows(tm), _rows(tm), _rows(tm), _rows(tm), pl.BlockSpec((tm, 3 * D), lambda i: (i, 3)),
                   pl.BlockSpec((1, D, D), lambda i: (3, 0, 0)), _const((SUBLANES, D)), _const((SUBLANES, D))],
        out_shape=[jax.ShapeDtypeStruct((n_rows, D), F32), jax.ShapeDtypeStruct((n_rows, D), BF16),
                   jax.ShapeDtypeStruct((n_rows, D), BF16), jax.ShapeDtypeStruct((n_rows, D), BF16),
                   jax.ShapeDtypeStruct((n_rows, N_GROUPS * D), BF16), jax.ShapeDtypeStruct((4, D, D), F32), vec, vec],
        compiler_params=_params(1),
    )(proj, proj, proj, ya, yb, yx, x, target, wo4_g, final_g)


def _branch_a_bwd(dya, proj, sa16, wo4_g, cw_a, dproj, dw4, tm):
    n_rows = proj.shape[0]
    n_tiles = n_rows // tm

    def body(dya_ref, bp, cp, xp, za, sa_ref, dyan, bpn, zan, cph, xph, w_ref, cw_ref, dp_in, dw_in,
             dp_ref, dw_ref, dwa_ref, e1, e2, o_scr, mm_scr):
        del dp_in, dw_in
        i = pl.program_id(0)
        bpn, zan, cph, xph = (_f32(r) for r in (bpn, zan, cph, xph))
        woa = w_ref[...].reshape(D, D)
        dya16 = dya_ref[...]
        chunks = [pl.ds(r0, ELEM_ROWS) for r0 in range(0, tm, ELEM_ROWS)]
        _to_time_major(e1, 0, jnp.where(i > 0, cph * xph, 0.0))
        for r0, rows in zip(range(0, tm, ELEM_ROWS), chunks):
            _to_time_major(e1, HALO + r0, cp[rows, :].astype(F32) * xp[rows, :].astype(F32))
        _conv(o_scr, e1, cw_ref, FWD_TAPS_A, tm)
        mm_scr[...] = _dot_nt(dya16, woa)
        for r0, rows in zip(range(0, tm, ELEM_ROWS), chunks):
            ca = _from_time_major(o_scr, ELEM_ROWS, r0)
            dsa = mm_scr[rows, :]
            b = bp[rows, :].astype(F32)
            silu_z, dsilu_z = _silu_and_grad(za[rows, :].astype(F32))
            t = dsa * silu_z
            dp_ref[rows, 0 * D:1 * D] = (t * ca).astype(BF16)
            dp_ref[rows, 3 * D:4 * D] = (dsa * (b * ca) * dsilu_z).astype(BF16)
            _to_time_major(e2, r0, t * b)
        dcan = (_dot_nt(dyan[...], woa) * jax.nn.silu(zan)) * bpn
        _to_time_major(e2, tm, jnp.where(i < n_tiles - 1, dcan, 0.0))

        @pl.when(i == 0)
        def _():
            dw_ref[...] = jnp.zeros_like(dw_ref)
            dwa_ref[...] = jnp.zeros_like(dwa_ref)

        _conv_wgrad(dwa_ref, e2, e1, FWD_TAPS_A, tm)
        dw_ref[0] += _dot_tn(sa_ref[...], dya16)
        _conv(o_scr, e2, cw_ref, BWD_TAPS_A, tm)
        for r0, rows in zip(range(0, tm, ELEM_ROWS), chunks):
            dprod = _from_time_major(o_scr, ELEM_ROWS, r0)
            dp_ref[rows, 1 * D:2 * D] = (dprod * xp[rows, :].astype(F32)).astype(BF16)
            dp_ref[rows, 2 * D:3 * D] = (dprod * cp[rows, :].astype(F32)).astype(BF16)

    return pl.pallas_call(
        body, name="branch_a_bwd", grid=(n_tiles,),
        in_specs=[_rows(tm), _rows(tm, G_BA), _rows(tm, G_CA), _rows(tm, G_XA), _rows(tm, G_ZA), _rows(tm),
                  _next_halo(tm, n_rows), _next_halo(tm, n_rows, G_BA), _next_halo(tm, n_rows, G_ZA),
                  _prev_halo(tm, G_CA), _prev_halo(tm, G_XA), _w_out_spec(0), _const(cw_a.shape), ANY, ANY],
        out_specs=[pl.BlockSpec((tm, 4 * D), lambda i: (i, 0)), pl.BlockSpec((1, D, D), lambda i: (0, 0, 0)),
                   _const((K_A * LANE_GROUPS, LANES))],
        out_shape=[jax.ShapeDtypeStruct(dproj.shape, BF16), jax.ShapeDtypeStruct(dw4.shape, F32),
                   jax.ShapeDtypeStruct((K_A * LANE_GROUPS, LANES), F32)],
        input_output_aliases={13: 0, 14: 1},
        scratch_shapes=[_time_major(tm + HALO), _time_major(tm + HALO), _time_major(tm), pltpu.VMEM((tm, D), F32)],
        compiler_params=_params(1),
    )(dya, proj, proj, proj, proj, sa16, dya, proj, proj, proj, proj, wo4_g, cw_a, dproj, dw4)


def _branch_b_bwd(dyb, proj, cb, sb16, wo4_g, cw_b, ln_g, ln_b, dproj, dw4, tm):
    n_rows = proj.shape[0]
    n_tiles = n_rows // tm

    def body(dyb_ref, zb, cb_ref, vb, gb, sb_ref, dybn, zbn, cbn, vbh, gbh, w_ref, cw_ref, lg_ref, lb_ref,
             dp_in, dw_in, dp_ref, dw_ref, dwb_ref, dbb_ref, dlg_ref, dlb_ref, e1, e2, o_scr, mm_scr):
        del dp_in, dw_in
        zbn, vbh, gbh = (_f32(r) for r in (zbn, vbh, gbh))
        i = pl.program_id(0)
        wob = w_ref[...].reshape(D, D)
        lg, lb = lg_ref[...], lb_ref[...]

        def conv_out_grad(dsb, z, c):
            xhat, rstd, ln = _layernorm_parts(c, lg, lb)
            sw, dsw = _silu_and_grad(ln)
            sz, dsz = _silu_and_grad(z)
            dln = (dsb * sz) * dsw
            dxhat = dln * lg
            dcb = rstd * (dxhat - _mean(dxhat) - xhat * _mean(dxhat * xhat))
            return dsb * sw * dsz, dln, xhat, dcb

        @pl.when(i == 0)
        def _():
            dw_ref[...] = jnp.zeros_like(dw_ref)
            dwb_ref[...] = jnp.zeros_like(dwb_ref)
            dbb_ref[...] = jnp.zeros_like(dbb_ref)
            dlg_ref[...] = jnp.zeros_like(dlg_ref)
            dlb_ref[...] = jnp.zeros_like(dlb_ref)

        dyb16 = dyb_ref[...]
        mm_scr[...] = _dot_nt(dyb16, wob)
        dlg, dlb, dbb = (jnp.zeros((SUBLANES, D), F32),) * 3
        for r0 in range(0, tm, ELEM_ROWS):
            rows = pl.ds(r0, ELEM_ROWS)
            dzb, dln, xhat, dcb = conv_out_grad(mm_scr[rows, :], zb[rows, :].astype(F32), cb_ref[rows, :])
            dp_ref[rows, 2 * D:3 * D] = dzb.astype(BF16)
            _to_time_major(e2, r0, dcb)
            dlg, dlb, dbb = dlg + _fold8(dln * xhat), dlb + _fold8(dln), dbb + _fold8(dcb)
        _, _, _, dcbn = conv_out_grad(_dot_nt(dybn[...], wob), zbn[...], cbn[...])
        _to_time_major(e2, tm, jnp.where(i < n_tiles - 1, dcbn, 0.0))
        dlg_ref[...] += dlg
        dlb_ref[...] += dlb
        dbb_ref[...] += dbb
        dw_ref[0] += _dot_tn(sb_ref[...], dyb16)
        _to_time_major(e1, 0, jnp.where(i > 0, vbh[...] * jax.nn.sigmoid(gbh[...]), 0.0))
        for r0 in range(0, tm, ELEM_ROWS):
            rows = pl.ds(r0, ELEM_ROWS)
            sg = jax.nn.sigmoid(gb[rows, :].astype(F32))
            mm_scr[rows, :] = sg
            _to_time_major(e1, HALO + r0, vb[rows, :].astype(F32) * sg)
        _conv_wgrad(dwb_ref, e2, e1, FWD_TAPS_B, tm)
        _conv(o_scr, e2, cw_ref, BWD_TAPS_B, tm)
        for r0 in range(0, tm, ELEM_ROWS):
            rows = pl.ds(r0, ELEM_ROWS)
            dglu = _from_time_major(o_scr, ELEM_ROWS, r0)
            sg = mm_scr[rows, :]
            dp_ref[rows, 0 * D:1 * D] = (dglu * sg).astype(BF16)
            dp_ref[rows, 1 * D:2 * D] = (dglu * vb[rows, :].astype(F32) * (sg * (1.0 - sg))).astype(BF16)

    vec = jax.ShapeDtypeStruct((SUBLANES, D), F32)
    return pl.pallas_call(
        body, name="branch_b_bwd", grid=(n_tiles,),
        in_specs=[_rows(tm), _rows(tm, G_ZB), _rows(tm), _rows(tm, G_VB), _rows(tm, G_GB), _rows(tm),
                  _next_halo(tm, n_rows), _next_halo(tm, n_rows, G_ZB), _next_halo(tm, n_rows),
                  _prev_halo(tm, G_VB), _prev_halo(tm, G_GB), _w_out_spec(1), _const(cw_b.shape), _const((1, D)),
                  _const((1, D)), ANY, ANY],
        out_specs=[pl.BlockSpec((tm, 3 * D), lambda i: (i, 2)), pl.BlockSpec((1, D, D), lambda i: (1, 0, 0)),
                   _const((K_B * LANE_GROUPS, LANES)), _const((SUBLANES, D)), _const((SUBLANES, D)),
                   _const((SUBLANES, D))],
        out_shape=[jax.ShapeDtypeStruct(dproj.shape, BF16), jax.ShapeDtypeStruct(dw4.shape, F32),
                   jax.ShapeDtypeStruct((K_B * LANE_GROUPS, LANES), F32), vec, vec, vec],
        input_output_aliases={15: 0, 16: 1},
        scratch_shapes=[_time_major(tm + HALO), _time_major(tm + HALO), _time_major(tm), pltpu.VMEM((tm, D), F32)],
        compiler_params=_params(1),
    )(dyb, proj, cb, proj, proj, sb16, dyb, proj, cb, proj, proj, wo4_g, cw_b, ln_g, ln_b, dproj, dw4)


def _branch_x_bwd(dyx, proj, sx16, probs, kv16, wo4_g, dproj, dw4, tm):
    n_rows = proj.shape[0]
    scale = HEAD_DIM ** -0.5

    def body(dyx_ref, q, zx, sx_ref, p_ref, kv_ref, w_ref, dp_in, dw_in, dp_ref, dw_ref, dkv_ref):
        del dp_in, dw_in
        i = pl.program_id(0)
        dyx16 = dyx_ref[...]
        q16 = q[...].astype(BF16)
        probs = [p_ref[:, h * HEAD_DIM:(h + 1) * HEAD_DIM] for h in range(N_HEADS)]
        outs = [_dot(probs[h].astype(BF16), kv_ref[N_HEADS + h]) for h in range(N_HEADS)]
        dsx = _dot_nt(dyx16, w_ref[...].reshape(D, D))
        silu_z, dsilu_z = _silu_and_grad(_f32(zx))
        dp_ref[:, D:2 * D] = (dsx * jnp.concatenate(outs, axis=-1) * dsilu_z).astype(BF16)
        do16 = (dsx * silu_z).astype(BF16)

        @pl.when(i == 0)
        def _():
            dw_ref[...] = jnp.zeros_like(dw_ref)
            dkv_ref[...] = jnp.zeros_like(dkv_ref)

        for h in range(N_HEADS):
            cols = slice(h * HEAD_DIM, (h + 1) * HEAD_DIM)
            p = probs[h]
            dprob = _dot_nt(do16[:, cols], kv_ref[N_HEADS + h])
            ds16 = ((p * (dprob - jnp.sum(p * dprob, axis=-1, keepdims=True))) * scale).astype(BF16)
            dp_ref[:, cols] = _dot(ds16, kv_ref[h]).astype(BF16)
            dkv_ref[h] += _dot_tn(ds16, q16[:, cols])
            dkv_ref[N_HEADS + h] += _dot_tn(p.astype(BF16), do16[:, cols])
        dw_ref[0] += _dot_tn(sx_ref[...], dyx16)

    return pl.pallas_call(
        body, name="branch_x_bwd", grid=(n_rows // tm,),
        in_specs=[_rows(tm), _rows(tm, G_Q), _rows(tm, G_ZX), _rows(tm), _rows(tm), _const(kv16.shape),
                  _w_out_spec(2), ANY, ANY],
        out_specs=[pl.BlockSpec((tm, 2 * D), lambda i: (i, 2)), pl.BlockSpec((1, D, D), lambda i: (2, 0, 0)),
                   _const(kv16.shape)],
        out_shape=[jax.ShapeDtypeStruct(dproj.shape, BF16), jax.ShapeDtypeStruct(dw4.shape, F32),
                   jax.ShapeDtypeStruct(kv16.shape, F32)],
        input_output_aliases={7: 0, 8: 1},
        compiler_params=_params(1),
    )(dyx, proj, proj, sx16, probs, kv16, wo4_g, dproj, dw4)


def _dp_unit(u):
    g = u // 2
    pos = jnp.where(g < G_VB, g, jnp.where(g < G_Q, g + 2, jnp.where(g < G_GA, g - 3, g)))
    return 2 * pos + u % 2


def _scatter_copies(srcs, lands, send, recv):
    x, y, c = lax.axis_index("x"), lax.axis_index("y"), lax.axis_index("c")
    copies = []
    for n in range(N_DEV - 1):
        flip = n + 1
        px = 1 - x if flip & 4 else x
        py = 1 - y if flip & 2 else y
        pc = 1 - c if flip & 1 else c
        for t, (src, land) in enumerate(zip(srcs, lands)):
            copies.append(pltpu.make_async_remote_copy(
                src_ref=src.at[4 * px + 2 * py + pc], dst_ref=land.at[n], send_sem=send.at[t * (N_DEV - 1) + n],
                recv_sem=recv.at[t * (N_DEV - 1) + n], device_id=(px, py, pc), device_id_type=MESH))
    return copies


def _scatter_start(name, arrays, views, n_views):
    n = len(arrays)
    lands = [lax.empty(tuple(N_DEV - 1 if d == N_DEV else d for d in a.shape), a.dtype) for a in arrays]

    def body(*refs):
        src, land, (send, recv) = refs[:n], refs[n:2 * n], refs[2 * n:2 * n + 2]
        token = refs[-1]
        for cp in _scatter_copies(views(src), views(land), send, recv):
            cp.start()
        token[...] = jnp.zeros_like(token)

    sems = pltpu.SemaphoreType.DMA((n_views * (N_DEV - 1),))
    out = pl.pallas_call(
        body, name=name,
        in_specs=[HBM] * (2 * n),
        out_specs=[SEM, SEM] + [HBM] * (2 * n) + [pl.BlockSpec(memory_space=pltpu.VMEM)],
        out_shape=[sems, sems] + [pltpu.HBM(a.shape, a.dtype) for a in arrays + lands]
        + [jax.ShapeDtypeStruct((SUBLANES, LANES), F32)],
        input_output_aliases={k: 2 + k for k in range(2 * n)},
        compiler_params=pltpu.CompilerParams(has_side_effects=SIDE_EFFECT),
    )(*[pltpu.with_memory_space_constraint(a, pltpu.HBM) for a in arrays + lands])
    return dict(name=name, sems=out[:2], moving=out[2:2 + 2 * n], views=views, token=out[-1])


def _scatter_wait(started, after):
    n = len(started["moving"]) // 2
    views = started["views"]

    def body(*refs):
        src, land, (send, recv) = refs[:n], refs[n:2 * n], refs[2 * n:2 * n + 2]
        for cp in _scatter_copies(views(src), views(land), send, recv):
            cp.wait_send()
            cp.wait_recv()

    out = pl.pallas_call(
        body, name=started["name"].replace("start", "wait"),
        in_specs=[HBM] * (2 * n) + [SEM, SEM, ANY],
        out_specs=[HBM] * (2 * n),
        out_shape=[pltpu.HBM(a.shape, a.dtype) for a in started["moving"]],
        input_output_aliases={k: k for k in range(2 * n)},
        compiler_params=pltpu.CompilerParams(has_side_effects=SIDE_EFFECT),
    )(*started["moving"], *started["sems"], after)
    return out[n:]


def _w_in_grad(ut, dproj, tk, token):
    n_rows = dproj.shape[0]
    n_k = n_rows // tk
    per_shard = W_IN_SHARD // UNIT

    def body(ut_ref, dp0, dp1, dp2, token_ref, out_ref, out16_ref, acc):
        del token_ref
        t = pl.program_id(1)

        for r, dp_ref in enumerate((dp0, dp1, dp2)):
            cols = slice(r * UNIT, (r + 1) * UNIT)

            @pl.when(t == 0)
            def _(dp_ref=dp_ref, cols=cols):
                acc[:, cols] = _dot(ut_ref[...], dp_ref[...])

            @pl.when(t > 0)
            def _(dp_ref=dp_ref, cols=cols):
                acc[:, cols] += _dot(ut_ref[...], dp_ref[...])

        @pl.when(t == n_k - 1)
        def _():
            out_ref[0] = acc[...]
            out16_ref[0] = acc[...].astype(BF16)

    def dp_spec(r):
        return pl.BlockSpec((tk, UNIT), lambda q, t: (t, _dp_unit(per_shard * q + r)))

    shard = pl.BlockSpec((1, D, W_IN_SHARD), lambda q, t: (q, 0, 0))
    return pl.pallas_call(
        body, name="w_in_grad", grid=(N_DEV, n_k),
        in_specs=[pl.BlockSpec((D, tk), lambda q, t: (0, t)), dp_spec(0), dp_spec(1), dp_spec(2), ANY],
        out_specs=[shard, shard],
        out_shape=[jax.ShapeDtypeStruct((N_DEV, D, W_IN_SHARD), F32), jax.ShapeDtypeStruct((N_DEV, D, W_IN_SHARD), BF16)],
        scratch_shapes=[pltpu.VMEM((D, W_IN_SHARD), F32)],
        compiler_params=_params(2),
    )(ut, dproj, dproj, dproj, token)


def _x_grad(dproj, win_t, x, dh, norm_g, token, tm):
    n_rows = x.shape[0]
    n_k = N_GROUPS * D // X_GRAD_K

    def body(dp_ref, wt_ref, x_ref, dh_ref, g_ref, token_ref, gx_ref, dg_ref, acc):
        del token_ref
        i, g = pl.program_id(0), pl.program_id(1)

        @pl.when((i == 0) & (g == 0))
        def _():
            dg_ref[...] = jnp.zeros_like(dg_ref)

        @pl.when(g == 0)
        def _():
            acc[...] = _dot(dp_ref[...], wt_ref[...])

        @pl.when(g > 0)
        def _():
            acc[...] += _dot(dp_ref[...], wt_ref[...])

        @pl.when(g == n_k - 1)
        def _():
            du = acc[...]
            xf = x_ref[...]
            r = lax.rsqrt(_mean(xf * xf) + EPS)
            xn = xf * r
            dun = du * g_ref[...]
            gx_ref[...] = dh_ref[...] + r * (dun - xn * _mean(dun * xn))
            dg_ref[...] += _fold8(du * xn)

    return pl.pallas_call(
        body, name="x_grad", grid=(n_rows // tm, n_k),
        in_specs=[pl.BlockSpec((tm, X_GRAD_K), lambda i, g: (i, g)), pl.BlockSpec((X_GRAD_K, D), lambda i, g: (g, 0)),
                  pl.BlockSpec((tm, D), lambda i, g: (i, 0)), pl.BlockSpec((tm, D), lambda i, g: (i, 0)),
                  _const((1, D)), ANY],
        out_specs=[pl.BlockSpec((tm, D), lambda i, g: (i, 0)), _const((SUBLANES, D))],
        out_shape=[jax.ShapeDtypeStruct((n_rows, D), F32), jax.ShapeDtypeStruct((SUBLANES, D), F32)],
        scratch_shapes=[pltpu.VMEM((tm, D), F32)],
        compiler_params=_params(2),
    )(dproj, win_t, x, dh, norm_g, token)


def _local_step(x, mem, target, norm_g, conv_b_b, ln_g, ln_b, mem_g, final_g, shards):
    n_rows = x.shape[0]
    tm = min(512, n_rows)
    big = min(1024, n_rows)
    u16, ut = _rmsnorm_fwd(x, norm_g, big)
    proj, win_t, _, wkv_g, wo4_g, cw_g = _proj_fwd_gather(u16, shards, min(2048, n_rows))
    cw_rows = cw_g.transpose(1, 0, 2).reshape((SUBLANES + HALO) * LANE_GROUPS, LANES)
    cw_a, cw_b = cw_rows[:SUBLANES * LANE_GROUPS], cw_rows[SUBLANES * LANE_GROUPS:]
    kv16, mn16 = _kv_fwd(mem, mem_g, wkv_g)
    sa16, ya = _branch_a_fwd(proj, wo4_g, cw_a, tm)
    cb, sb16, yb = _branch_b_fwd(proj, wo4_g, cw_b, conv_b_b, ln_g, ln_b, tm)
    sx16, yx, probs = _branch_x_fwd(proj, kv16, wo4_g, big)
    dh, dya, dyb, dyx, dproj, dw4, dfg, sq = _merge_fwd_bwd(proj, ya, yb, yx, x, target, wo4_g, final_g,
                                                             min(256, n_rows))
    dproj, dw4, dwa = _branch_a_bwd(dya, proj, sa16, wo4_g, cw_a, dproj, dw4, tm)
    dproj, dw4, dwb, dbb, dlg, dlb = _branch_b_bwd(dyb, proj, cb, sb16, wo4_g, cw_b, ln_g, ln_b, dproj, dw4, tm)
    dproj, dw4, dkv = _branch_x_bwd(dyx, proj, sx16, probs, kv16, wo4_g, dproj, dw4, tm)
    dwkv_g, dwkv16, dmg = _kv_bwd(dkv, mem, mem_g, mn16, wkv_g)
    dw4 = dw4.reshape(4, N_DEV, D // N_DEV, D)
    small_moving = _scatter_start("small_grads_start", [dw4, dwkv16],
                                  lambda refs: [refs[0].at[w] for w in range(4)] + [refs[1]], 5)
    dwin_g, dwin16 = _w_in_grad(ut, dproj, min(2048, n_rows), small_moving["token"])
    w_in_moving = _scatter_start("w_in_grad_start", [dwin16], lambda refs: list(refs), 1)
    gx, dng = _x_grad(dproj, win_t, x, dh, norm_g, w_in_moving["token"], big)
    land4, landkv = _scatter_wait(small_moving, dng)
    landin, = _scatter_wait(w_in_moving, dng)
    small = {SV_NORM_G: dng, SV_CONV_B_B: dbb, SV_LN_G: dlg, SV_LN_B: dlb, SV_MEM_G: dmg, SV_FINAL_G: dfg, SV_LOSS: sq}
    grads = [(dwin_g[None], landin[None]), (dw4, land4), (dwkv_g[None], landkv[None])]
    return gx, grads, small, dwa.reshape(K_A, D), dwb.reshape(K_B, D)


def _allgather_small(small, conv_rows):
    keys = sorted(small)

    def body(*refs):
        parts, (conv_ref, out_ref, mine, send, recv) = refs[:len(keys)], refs[len(keys):]
        x, y, c, chips = _place()
        me, sibling = 4 * x + 2 * y + c, (x, y, 1 - c)
        mine[pl.ds(0, SV_CONV_A), :] = jnp.zeros((SV_CONV_A, D), F32)
        for key, part in zip(keys, parts):
            mine[key:key + 1, :] = jnp.sum(part[...], axis=0, keepdims=True)
        mine[pl.ds(SV_CONV_A, SV_ROWS - SV_CONV_A), :] = conv_ref[...]
        out_ref[me] = mine[...]

        def copy(k, block, to, from_mine=False):
            return pltpu.make_async_remote_copy(
                src_ref=mine if from_mine else out_ref.at[block], dst_ref=out_ref.at[block],
                send_sem=send.at[k], recv_sem=recv.at[k], device_id=to, device_id_type=MESH)

        first = [copy(0, me, sibling, from_mine=True)]
        first += [copy(1 + j, me, (*chip, c), from_mine=True) for j, chip in enumerate(chips)]
        for cp in first:
            cp.start()
        passed = []
        for j, (px, py) in enumerate(chips):
            block = 4 * px + 2 * py + c
            copy(1 + j, block, sibling).wait_recv()
            passed.append(copy(4 + j, block, sibling))
            passed[-1].start()
        copy(0, 4 * x + 2 * y + 1 - c, sibling).wait_recv()
        for j, (px, py) in enumerate(chips):
            copy(4 + j, 4 * px + 2 * py + 1 - c, sibling).wait_recv()
        for cp in first + passed:
            cp.wait_send()

    vmem = pl.BlockSpec(memory_space=pltpu.VMEM)
    return pl.pallas_call(
        body, name="allgather_small",
        in_specs=[vmem] * (len(keys) + 1), out_specs=vmem,
        out_shape=jax.ShapeDtypeStruct((N_DEV, SV_ROWS, D), F32),
        scratch_shapes=[pltpu.VMEM((SV_ROWS, D), F32), pltpu.SemaphoreType.DMA((7,)), pltpu.SemaphoreType.DMA((7,))],
    )(*[small[k] for k in keys], conv_rows)


def _adamw(w, g, m, v):
    m = ADAM_B1 * m + (1.0 - ADAM_B1) * g
    v = ADAM_B2 * v + (1.0 - ADAM_B2) * (g * g)
    m_hat = m / (1.0 - ADAM_B1 ** ADAM_STEP)
    v_hat = v / (1.0 - ADAM_B2 ** ADAM_STEP)
    return -ADAM_LR * (m_hat / (jnp.sqrt(v_hat) + ADAM_EPS) + ADAM_WD * w), m, v


def _adamw_shard(own, landed, piece, k_arr, w, m, v, tr):
    n_r, n_c = w.shape
    n_landed = landed.shape[1]

    def body(k_ref, own_ref, *refs):
        del k_ref
        landed_refs, (w_ref, m_ref, v_ref, g_out, d_out, m_out, v_out) = refs[:n_landed], refs[n_landed:]
        g = own_ref[0, 0]
        for landed_ref in landed_refs:
            g = g + landed_ref[0, 0].astype(F32)
        g_out[...] = g
        d_out[...], m_out[...], v_out[...] = _adamw(w_ref[...], g, m_ref[...], v_ref[...])

    blk = (1, 1, tr, n_c)
    flat = pl.BlockSpec((tr, n_c), lambda r, k: (r, 0))
    return pl.pallas_call(
        body, name="adamw_shard",
        grid_spec=pltpu.PrefetchScalarGridSpec(
            num_scalar_prefetch=1, grid=(n_r // tr,),
            in_specs=[pl.BlockSpec(blk, lambda r, k: (piece, k[0], r, 0))]
            + [pl.BlockSpec(blk, functools.partial(lambda r, k, j: (piece, j, r, 0), j=j)) for j in range(n_landed)]
            + [flat] * 3,
            out_specs=[flat] * 4),
        out_shape=[jax.ShapeDtypeStruct((n_r, n_c), F32)] * 4,
        compiler_params=_params(1),
    )(k_arr, own, *([landed] * n_landed), w, m, v)


def _adamw_shards(entries, k_arr):
    n = len(entries)

    def body(k_ref, *refs):
        del k_ref
        ins, outs = refs[:5 * n], refs[5 * n:]
        for e in range(n):
            own_ref, landed_ref, w_ref, m_ref, v_ref = ins[5 * e:5 * e + 5]
            g = own_ref[0, 0]
            for j in range(landed_ref.shape[1]):
                g = g + landed_ref[0, j].astype(F32)
            g_out, d_out, m_out, v_out = outs[4 * e:4 * e + 4]
            g_out[...] = g
            d_out[...], m_out[...], v_out[...] = _adamw(w_ref[...], g, m_ref[...], v_ref[...])

    in_specs, operands, out_specs, out_shape = [], [], [], []
    for own, landed, piece, w, m, v in entries:
        flat = pl.BlockSpec(w.shape, lambda i, k: (0, 0))
        in_specs += [pl.BlockSpec((1, 1, *w.shape), functools.partial(lambda i, k, p: (p, k[0], 0, 0), p=piece)),
                     pl.BlockSpec((1, *landed.shape[1:]), functools.partial(lambda i, k, p: (p, 0, 0, 0), p=piece)),
                     flat, flat, flat]
        operands += [own, landed, w, m, v]
        out_specs += [flat] * 4
        out_shape += [jax.ShapeDtypeStruct(w.shape, F32)] * 4
    out = pl.pallas_call(
        body, name="adamw_shards",
        grid_spec=pltpu.PrefetchScalarGridSpec(num_scalar_prefetch=1, grid=(1,), in_specs=in_specs,
                                               out_specs=out_specs),
        out_shape=out_shape,
        compiler_params=_params(1),
    )(k_arr, *operands)
    return [tuple(out[4 * e:4 * e + 4]) for e in range(n)]


def _adamw_small(gathered, k_arr, vectors, conv_a, conv_b):
    n_vec = len(vectors)
    cols = D // N_DEV

    def body(k_ref, full_ref, cols_ref, *refs):
        del k_ref
        ins, outs = refs[:3 * (n_vec + 2)], refs[3 * (n_vec + 2):-2]
        tot, tot_cols = refs[-2:]
        tot[...] = full_ref[0]
        tot_cols[...] = cols_ref[0]
        for dev in range(1, N_DEV):
            tot[...] += full_ref[dev]
            tot_cols[...] += cols_ref[dev]
        loss = (0.5 / D) * jnp.sum(tot[SV_LOSS:SV_LOSS + 1, :])
        outs[0][...] = jnp.full(outs[0].shape, loss, F32)
        grads = [tot[n:n + 1, :] for n in range(n_vec)]
        grads += [tot_cols[pl.ds(SV_CONV_A, K_A), :], tot_cols[pl.ds(SV_CONV_B, K_B), :]]
        for n, g in enumerate(grads):
            w_ref, m_ref, v_ref = ins[3 * n:3 * n + 3]
            g_out, d_out, m_out, v_out = outs[1 + 4 * n:5 + 4 * n]
            g_out[...] = g
            d_out[...], m_out[...], v_out[...] = _adamw(w_ref[...], g, m_ref[...], v_ref[...])

    weights = list(vectors) + [conv_a, conv_b]
    flat_in = [a for wmv in weights for a in wmv]
    out_shape = [jax.ShapeDtypeStruct((SUBLANES, 128), F32)]
    for wmv in weights:
        out_shape += [jax.ShapeDtypeStruct(wmv[0].shape, F32)] * 4
    return pl.pallas_call(
        body, name="adamw_small",
        grid_spec=pltpu.PrefetchScalarGridSpec(
            num_scalar_prefetch=1, grid=(1,),
            in_specs=[pl.BlockSpec((N_DEV, SV_ROWS, D), lambda i, k: (0, 0, 0)),
                      pl.BlockSpec((N_DEV, SV_ROWS, cols), lambda i, k: (0, 0, k[0]))]
            + [pl.BlockSpec(a.shape, lambda i, k: (0, 0)) for a in flat_in],
            out_specs=[pl.BlockSpec(s.shape, lambda i, k: (0, 0)) for s in out_shape],
            scratch_shapes=[pltpu.VMEM((SV_ROWS, D), F32), pltpu.VMEM((SV_ROWS, cols), F32)]),
        out_shape=out_shape,
        compiler_params=_params(1),
    )(k_arr, gathered, gathered, *flat_in)


def kernel(x, mem, norm_g, w_in, conv_a_w, w_out_a, conv_b_w, conv_b_b, ln_b_g, ln_b_b, w_out_b, mem_norm_g, w_kv, w_out_x, w_o, final_g, loss_target, m_norm_g, m_w_in, m_conv_a_w, m_w_out_a, m_conv_b_w, m_conv_b_b, m_ln_b_g, m_ln_b_b, m_w_out_b, m_mem_norm_g, m_w_kv, m_w_out_x, m_w_o, m_final_g, v_norm_g, v_w_in, v_conv_a_w, v_w_out_a, v_conv_b_w, v_conv_b_b, v_ln_b_g, v_ln_b_b, v_w_out_b, v_mem_norm_g, v_w_kv, v_w_out_x, v_w_o, v_final_g):
    xi, yi, ci = lax.axis_index("x"), lax.axis_index("y"), lax.axis_index("c")
    k_arr = jnp.reshape(4 * xi + 2 * yi + ci, (1,)).astype(jnp.int32)

    cw = jnp.concatenate([jnp.pad(conv_a_w[0], ((0, SUBLANES - K_A), (0, 0))),
                          jnp.pad(conv_b_w[0], ((0, HALO - K_B), (0, 0)))], axis=0)
    wo4 = jnp.stack([w_out_a[0], w_out_b[0], w_out_x[0], w_o[0]]).astype(BF16)
    shards = [w_in[0].astype(BF16), w_kv[0].astype(BF16), wo4, cw]

    final_g2 = final_g.reshape(1, D)
    gx, grads, small, dwa, dwb = _local_step(
        x[0], mem[0], loss_target[0], norm_g, conv_b_b, ln_b_g, ln_b_b, mem_norm_g, final_g2, shards)

    conv_rows = jnp.concatenate([jnp.pad(dwa, ((0, SUBLANES - K_A), (0, 0))),
                                 jnp.pad(dwb, ((0, HALO - K_B), (0, 0)))], axis=0)
    gathered_small = _allgather_small(small, conv_rows)

    res = {"w_in": _adamw_shard(grads[0][0], grads[0][1], 0, k_arr, w_in[0], m_w_in[0], v_w_in[0], 256)}
    small_shards = [("w_out_a", 1, 0, w_out_a, m_w_out_a, v_w_out_a), ("w_out_b", 1, 1, w_out_b, m_w_out_b, v_w_out_b),
                    ("w_out_x", 1, 2, w_out_x, m_w_out_x, v_w_out_x), ("w_o", 1, 3, w_o, m_w_o, v_w_o),
                    ("w_kv", 2, 0, w_kv, m_w_kv, v_w_kv)]
    updated = _adamw_shards([(grads[a][0], grads[a][1], l, w[0], m[0], v[0]) for _, a, l, w, m, v in small_shards],
                            k_arr)
    res.update({name: four for (name, *_), four in zip(small_shards, updated)})
    res = {name: tuple(r[None] for r in four) for name, four in res.items()}
    vectors = [(norm_g, m_norm_g, v_norm_g), (conv_b_b, m_conv_b_b, v_conv_b_b), (ln_b_g, m_ln_b_g, v_ln_b_g),
               (ln_b_b, m_ln_b_b, v_ln_b_b), (mem_norm_g, m_mem_norm_g, v_mem_norm_g),
               (final_g2, m_final_g.reshape(1, D), v_final_g.reshape(1, D))]
    out = _adamw_small(gathered_small, k_arr, vectors, (conv_a_w[0], m_conv_a_w[0], v_conv_a_w[0]),
                       (conv_b_w[0], m_conv_b_w[0], v_conv_b_w[0]))
    loss = out[0][0, 0]
    names = ["norm_g", "conv_b_b", "ln_b_g", "ln_b_b", "mem_norm_g", "final_g", "conv_a_w", "conv_b_w"]
    for n, name in enumerate(names):
        four = out[1 + 4 * n:5 + 4 * n]
        if name == "final_g":
            four = [r.reshape(D) for r in four]
        elif name.startswith("conv_") and name.endswith("_w"):
            four = [r[None] for r in four]
        res[name] = tuple(four)

    order = ["norm_g", "w_in", "conv_a_w", "w_out_a", "conv_b_w", "conv_b_b", "ln_b_g", "ln_b_b", "w_out_b",
             "mem_norm_g", "w_kv", "w_out_x", "w_o", "final_g"]
    return (loss, gx[None], *[res[n][0] for n in order], *[res[n][1] for n in order],
            *[res[n][2] for n in order], *[res[n][3] for n in order])
```

```python
import functools

import jax
import jax.numpy as jnp
from jax import lax
from jax.experimental import pallas as pl
from jax.experimental.pallas import tpu as pltpu

F32, BF16 = jnp.float32, jnp.bfloat16
D = 1024
N_DEV = 8
N_HEADS = 4
HEAD_DIM = D // N_HEADS
N_GROUPS = 12
W_IN_SHARD = N_GROUPS * D // N_DEV
UNIT = 512
X_GRAD_K = 2 * D
K_A, K_B = 3, 31
EPS = 1e-6
HALO = 32
SUBLANES = 8
LANES = 128
LANE_GROUPS = D // LANES
TAP_GROUP = 16
CONV_BLOCK = 32
ELEM_ROWS = 16
CONV_PARTIAL_SUMS = 4
VMEM_LIMIT = 56 << 20
MESH = pl.DeviceIdType.MESH
ANY = pl.BlockSpec(memory_space=pl.ANY)
HBM = pl.BlockSpec(memory_space=pltpu.HBM)
SEM = pl.BlockSpec(memory_space=pltpu.SEMAPHORE)
SIDE_EFFECT = pltpu.SideEffectType.DATAFLOW_SIDE_EFFECTING

G_BA, G_CA, G_XA, G_ZA, G_VB, G_GB, G_ZB, G_Q, G_ZX, G_GA, G_GBB, G_GX = range(N_GROUPS)

ADAM_LR, ADAM_B1, ADAM_B2, ADAM_EPS, ADAM_WD, ADAM_STEP = 0.001, 0.9, 0.999, 1e-08, 0.01, 10

SV_NORM_G, SV_CONV_B_B, SV_LN_G, SV_LN_B, SV_MEM_G, SV_FINAL_G, SV_LOSS = range(7)
SV_CONV_A, SV_CONV_B, SV_ROWS = 8, 16, 48


def _dot(a, b):
    return jnp.dot(a, b, preferred_element_type=F32)


def _dot_nt(a, b):
    return lax.dot_general(a, b, (((1,), (1,)), ((), ())), preferred_element_type=F32)


def _dot_tn(a, b):
    return lax.dot_general(a, b, (((0,), (0,)), ((), ())), preferred_element_type=F32)


def _silu_and_grad(z):
    s = jax.nn.sigmoid(z)
    return z * s, s * (1.0 + z * (1.0 - s))


def _fold8(a):
    return a.reshape(a.shape[0] // SUBLANES, SUBLANES, a.shape[1]).sum(axis=0)


def _mean(a):
    return jnp.mean(a, axis=-1, keepdims=True)


def _f32(ref):
    return ref[...].astype(F32)


def _params(n_grid):
    return pltpu.CompilerParams(dimension_semantics=("arbitrary",) * n_grid, vmem_limit_bytes=VMEM_LIMIT)


def _rows(tm, col=0):
    return pl.BlockSpec((tm, D), lambda i: (i, col))


def _prev_halo(tm, col=0):
    return pl.BlockSpec((HALO, D), lambda i: (jnp.maximum(i * (tm // HALO) - 1, 0), col))


def _next_halo(tm, n_rows, col=0):
    last = n_rows // HALO - 1
    return pl.BlockSpec((HALO, D), lambda i: (jnp.minimum((i + 1) * (tm // HALO), last), col))


def _const(shape):
    return pl.BlockSpec(shape, lambda *_: (0,) * len(shape))


def _w_out_spec(which):
    return pl.BlockSpec((N_DEV, None, D // N_DEV, D), lambda *_: (0, which, 0, 0))


def _to_time_major(t_ref, row0, x):
    n = x.shape[0]
    for j in range(LANE_GROUPS):
        t_ref[pl.ds(row0 * LANE_GROUPS + j, n, stride=LANE_GROUPS), :] = x[:, j * LANES:(j + 1) * LANES]


def _from_time_major(t_ref, n, row0=0):
    return jnp.concatenate([t_ref[pl.ds(row0 * LANE_GROUPS + j, n, stride=LANE_GROUPS), :]
                            for j in range(LANE_GROUPS)], axis=-1)


def _row(ref, t):
    start = t * LANE_GROUPS
    if not isinstance(start, int):
        start = pl.multiple_of(start, LANE_GROUPS)
    return ref[pl.ds(start, LANE_GROUPS), :]


def _tap_groups(taps):
    return [taps[first:first + TAP_GROUP] for first in range(0, len(taps), TAP_GROUP)]


def _conv(o_ref, e_ref, w_ref, taps, n_rows, bias_ref=None):
    for n_group, group in enumerate(_tap_groups(taps)):
        weights = [_row(w_ref, k) for k, _ in group]

        def block(c, carry, n_group=n_group, group=group, weights=weights):
            t0 = c * CONV_BLOCK
            window = {}
            for t in range(CONV_BLOCK):
                parts = [None] * min(CONV_PARTIAL_SUMS, len(group))
                for n, (_, off) in enumerate(group):
                    if t + off not in window:
                        window[t + off] = _row(e_ref, t0 + t + off)
                    term = weights[n] * window[t + off]
                    parts[n % len(parts)] = term if parts[n % len(parts)] is None else parts[n % len(parts)] + term
                window.pop(t + min(off for _, off in group), None)
                while len(parts) > 1:
                    parts = [parts[n] + parts[n + 1] for n in range(0, len(parts) - 1, 2)] + parts[len(parts) & ~1:]
                out = parts[0]
                if n_group > 0:
                    out = out + _row(o_ref, t0 + t)
                elif bias_ref is not None:
                    out = out + bias_ref[...]
                o_ref[pl.ds(pl.multiple_of((t0 + t) * LANE_GROUPS, LANE_GROUPS), LANE_GROUPS), :] = out
            return carry

        lax.fori_loop(0, n_rows // CONV_BLOCK, block, 0)


def _conv_wgrad(dw_ref, d_ref, e_ref, taps, n_rows):
    for group in _tap_groups(taps):
        def block(c, sums, group=group):
            t0 = c * CONV_BLOCK
            sums = list(sums)
            window = {}
            for t in range(CONV_BLOCK):
                d = _row(d_ref, t0 + t)
                for n, (_, off) in enumerate(group):
                    if t + off not in window:
                        window[t + off] = _row(e_ref, t0 + t + off)
                    sums[n] = sums[n] + d * window[t + off]
                window.pop(t + min(off for _, off in group), None)
            return tuple(sums)

        sums = lax.fori_loop(0, n_rows // CONV_BLOCK, block, tuple(_row(dw_ref, k) for k, _ in group))
        for (k, _), total in zip(group, sums):
            dw_ref[pl.ds(k * LANE_GROUPS, LANE_GROUPS), :] = total


FWD_TAPS_A = [(k, HALO - (K_A - 1) + k) for k in range(K_A)]
BWD_TAPS_A = [(k, K_A - 1 - k) for k in range(K_A)]
FWD_TAPS_B = [(k, HALO - (K_B - 1) + k) for k in range(K_B)]
BWD_TAPS_B = [(k, K_B - 1 - k) for k in range(K_B)]


def _time_major(n_rows):
    return pltpu.VMEM((n_rows * LANE_GROUPS, LANES), F32)


def _kv_fwd(mem, mem_g, wkv_g):
    m_len = mem.shape[0]

    def body(mem_ref, g_ref, w_ref, kv_ref, mn_ref):
        mf = mem_ref[...]
        r = lax.rsqrt(_mean(mf * mf) + EPS)
        mn = ((mf * r) * g_ref[...]).astype(BF16)
        mn_ref[...] = mn
        for b in range(2 * N_HEADS):
            kv_ref[b] = _dot(mn, w_ref[b]).astype(BF16)

    return pl.pallas_call(
        body, name="kv_fwd", grid=(1,),
        in_specs=[_const((m_len, D)), _const((1, D)), _const((2 * N_HEADS, D, HEAD_DIM))],
        out_specs=[_const((2 * N_HEADS, m_len, HEAD_DIM)), _const((m_len, D))],
        out_shape=[jax.ShapeDtypeStruct((2 * N_HEADS, m_len, HEAD_DIM), BF16), jax.ShapeDtypeStruct((m_len, D), BF16)],
        compiler_params=_params(1),
    )(mem, mem_g, wkv_g)


def _kv_bwd(dkv, mem, mem_g, mn16, wkv_g):
    m_len = mem.shape[0]

    def body(dkv_ref, mem_ref, g_ref, mn_ref, w_ref, dw_ref, dw16_ref, dg_ref):
        mn = mn_ref[...]
        dmn = jnp.zeros((m_len, D), F32)
        for b in range(2 * N_HEADS):
            d16 = dkv_ref[b].astype(BF16)
            dw = _dot_tn(mn, d16)
            dw_ref[b] = dw
            dw16_ref[b] = dw.astype(BF16)
            dmn = dmn + _dot_nt(d16, w_ref[b])
        mf = mem_ref[...]
        r = lax.rsqrt(_mean(mf * mf) + EPS)
        dg_ref[...] = _fold8(dmn * (mf * r))

    return pl.pallas_call(
        body, name="kv_bwd", grid=(1,),
        in_specs=[_const((2 * N_HEADS, m_len, HEAD_DIM)), _const((m_len, D)), _const((1, D)), _const((m_len, D)),
                  _const((2 * N_HEADS, D, HEAD_DIM))],
        out_specs=[_const((2 * N_HEADS, D, HEAD_DIM)), _const((2 * N_HEADS, D, HEAD_DIM)), _const((SUBLANES, D))],
        out_shape=[jax.ShapeDtypeStruct((2 * N_HEADS, D, HEAD_DIM), F32),
                   jax.ShapeDtypeStruct((2 * N_HEADS, D, HEAD_DIM), BF16), jax.ShapeDtypeStruct((SUBLANES, D), F32)],
        compiler_params=_params(1),
    )(dkv, mem, mem_g, mn16, wkv_g)


def _rmsnorm_fwd(x, norm_g, tm):
    n_rows = x.shape[0]

    def body(x_ref, g_ref, u_ref, ut_ref):
        xf = x_ref[...]
        u = (xf * lax.rsqrt(_mean(xf * xf) + EPS)) * g_ref[...]
        u_ref[...] = u.astype(BF16)
        ut_ref[...] = u.T.astype(BF16)

    return pl.pallas_call(
        body, name="rmsnorm_fwd", grid=(n_rows // tm,),
        in_specs=[_rows(tm), _const((1, D))],
        out_specs=[_rows(tm), pl.BlockSpec((D, tm), lambda i: (0, i))],
        out_shape=[jax.ShapeDtypeStruct((n_rows, D), BF16), jax.ShapeDtypeStruct((D, n_rows), BF16)],
        compiler_params=_params(1),
    )(x, norm_g)


def _place():
    x, y, c = lax.axis_index("x"), lax.axis_index("y"), lax.axis_index("c")
    other_chips = [(1 - x, y), (x, 1 - y), (1 - x, 1 - y)]
    return x, y, c, other_chips


def _near_far(x, y, c):
    near_kind = 1 - c

    def chip(kind):
        return x + (1 - kind) * (1 - 2 * x), y + kind * (1 - 2 * y)

    return (chip(near_kind), near_kind), (chip(c), c)


ARRIVALS = (("near", True), ("far", False), ("far", True), ("near", False), ("diag", True), ("diag", False))


def _arrival_order():
    x, y, c, chips = _place()
    (near, _), (far, _) = _near_far(x, y, c)
    chip = {"near": near, "far": far, "diag": chips[2]}
    order = [4 * x + 2 * y + c, 4 * x + 2 * y + 1 - c]
    for which, same_core in ARRIVALS:
        px, py = chip[which]
        order.append(4 * px + 2 * py + (c if same_core else 1 - c))
    return order


def _proj_fwd_gather(u16, blocks, tm):
    n = len(blocks)
    n_rows = u16.shape[0]
    n_i = n_rows // tm
    per_shard = W_IN_SHARD // UNIT
    assert n_i >= per_shard

    def wt_index(p, i, order):
        return (_dp_unit(per_shard * order[p] + jnp.minimum(i, per_shard - 1)), 0)

    def body(order_ref, u_ref, *refs):
        src, proj_ref, wt_ref, out = refs[:n], refs[n], refs[n + 1], refs[n + 2:2 * n + 2]
        wbuf, stage_sem, send, recv, own_sem = refs[2 * n + 2:]
        p, i = pl.program_id(0), pl.program_id(1)
        x, y, c, chips = _place()
        me, sibling = 4 * x + 2 * y + c, (x, y, 1 - c)

        def copy(t, k, block, to, from_input=False):
            return pltpu.make_async_remote_copy(
                src_ref=src[t] if from_input else out[t].at[block], dst_ref=out[t].at[block],
                send_sem=send.at[t, k], recv_sem=recv.at[t, k], device_id=to, device_id_type=MESH)

        def own_copies():
            return [pltpu.make_async_copy(src[t], out[t].at[me], own_sem.at[t]) for t in range(n)]

        def first_copy(t, j):
            return copy(t, 1 + j, me, (*chips[j], c), from_input=True)

        def first_copies(skip=()):
            first = []
            for t in range(n):
                first.append(copy(t, 0, me, sibling, from_input=True))
                first += [first_copy(t, j) for j in range(len(chips)) if (t, j) not in skip]
            return first

        def stage(slot, block):
            return pltpu.make_async_copy(out[0].at[block], wbuf.at[slot], stage_sem.at[slot])

        (near, near_kind), (far, far_kind) = _near_far(x, y, c)

        @pl.when((p == 0) & (i == 0))
        def _():
            for cp in own_copies() + first_copies(skip=((0, 0), (0, 1))):
                cp.start()
            for kind in (0, 1):
                @pl.when(near_kind == kind)
                def _(kind=kind):
                    first_copy(0, kind).start()
            mine = pltpu.make_async_copy(src[0], wbuf.at[0], stage_sem.at[0])
            mine.start()
            mine.wait()

        @pl.when((p == 1) & (i == 0))
        def _():
            for kind in (0, 1):
                @pl.when(far_kind == kind)
                def _(kind=kind):
                    first_copy(0, kind).start()

        @pl.when((p > 0) & (i == 0))
        def _():
            stage(p % 2, order_ref[p]).wait()

        proj_ref[...] = _dot(u_ref[...], wbuf[p % 2])
        for r in range(per_shard):
            @pl.when(i == r)
            def _(r=r):
                wt_ref[...] = wbuf[p % 2, :, r * UNIT:(r + 1) * UNIT].astype(F32).T.astype(BF16)

        for nxt in range(1, N_DEV):
            @pl.when((p == nxt - 1) & (i == n_i - 1))
            def _(nxt=nxt):
                if nxt == 1:
                    block = 4 * x + 2 * y + 1 - c
                    copy(0, 0, block, sibling).wait_recv()
                else:
                    which, same_core = ARRIVALS[nxt - 2]
                    (px, py), kind = {"near": (near, near_kind), "far": (far, far_kind), "diag": (chips[2], 2)}[which]
                    if same_core:
                        block = 4 * px + 2 * py + c
                        copy(0, 1 + kind, block, sibling).wait_recv()
                        copy(0, 4 + kind, block, sibling).start()
                    else:
                        block = 4 * px + 2 * py + 1 - c
                        copy(0, 4 + kind, block, sibling).wait_recv()
                stage(nxt % 2, block).start()

        @pl.when((p == N_DEV - 1) & (i == n_i - 1))
        def _():
            passed = [copy(0, 4 + j, 4 * px + 2 * py + c, sibling) for j, (px, py) in enumerate(chips)]
            for j, (px, py) in enumerate(chips):
                for t in range(1, n):
                    block = 4 * px + 2 * py + c
                    copy(t, 1 + j, block, sibling).wait_recv()
                    passed.append(copy(t, 4 + j, block, sibling))
                    passed[-1].start()
            for t in range(1, n):
                copy(t, 0, 4 * x + 2 * y + 1 - c, sibling).wait_recv()
                for j, (px, py) in enumerate(chips):
                    copy(t, 4 + j, 4 * px + 2 * py + 1 - c, sibling).wait_recv()
            for cp in first_copies() + passed:
                cp.wait_send()
            for cp in own_copies():
                cp.wait()

    return pl.pallas_call(
        body, name="proj_fwd_gather",
        grid_spec=pltpu.PrefetchScalarGridSpec(
            num_scalar_prefetch=1, grid=(N_DEV, n_i),
            in_specs=[pl.BlockSpec((tm, D), lambda p, i, order: (i, 0))] + [ANY] * n,
            out_specs=[pl.BlockSpec((tm, W_IN_SHARD), lambda p, i, order: (i, order[p])),
                       pl.BlockSpec((UNIT, D), wt_index)] + [ANY] * n,
            scratch_shapes=[pltpu.VMEM((2, D, W_IN_SHARD), BF16), pltpu.SemaphoreType.DMA((2,)),
                            pltpu.SemaphoreType.DMA((n, 7)), pltpu.SemaphoreType.DMA((n, 7)),
                            pltpu.SemaphoreType.DMA((n,))]),
        out_shape=[jax.ShapeDtypeStruct((n_rows, N_GROUPS * D), F32), jax.ShapeDtypeStruct((N_GROUPS * D, D), BF16)]
        + [jax.ShapeDtypeStruct((N_DEV, *b.shape), b.dtype) for b in blocks],
        compiler_params=_params(2),
    )(jnp.stack(_arrival_order()).astype(jnp.int32), u16, *blocks)


def _branch_a_fwd(proj, wo4_g, cw_a, tm):
    n_rows = proj.shape[0]

    def body(bp, cp, xp, za, cph, xph, w_ref, cw_ref, sa_ref, ya_ref, e_scr, o_scr):
        i = pl.program_id(0)
        _to_time_major(e_scr, 0, jnp.where(i > 0, _f32(cph) * _f32(xph), 0.0))
        for r0 in range(0, tm, ELEM_ROWS):
            rows = pl.ds(r0, ELEM_ROWS)
            _to_time_major(e_scr, HALO + r0, cp[rows, :].astype(F32) * xp[rows, :].astype(F32))
        _conv(o_scr, e_scr, cw_ref, FWD_TAPS_A, tm)
        for r0 in range(0, tm, ELEM_ROWS):
            rows = pl.ds(r0, ELEM_ROWS)
            ca = _from_time_major(o_scr, ELEM_ROWS, r0)
            sa_ref[rows, :] = (jax.nn.silu(za[rows, :].astype(F32)) * (bp[rows, :].astype(F32) * ca)).astype(BF16)
        ya_ref[...] = _dot(sa_ref[...], w_ref[...].reshape(D, D))

    return pl.pallas_call(
        body, name="branch_a_fwd", grid=(n_rows // tm,),
        in_specs=[_rows(tm, G_BA), _rows(tm, G_CA), _rows(tm, G_XA), _rows(tm, G_ZA),
                  _prev_halo(tm, G_CA), _prev_halo(tm, G_XA), _w_out_spec(0), _const(cw_a.shape)],
        out_specs=[_rows(tm), _rows(tm)],
        out_shape=[jax.ShapeDtypeStruct((n_rows, D), BF16), jax.ShapeDtypeStruct((n_rows, D), F32)],
        scratch_shapes=[_time_major(tm + HALO), _time_major(tm)],
        compiler_params=_params(1),
    )(proj, proj, proj, proj, proj, proj, wo4_g, cw_a)


def _layernorm_parts(cb, lg, lb):
    xc = cb - _mean(cb)
    rstd = lax.rsqrt(_mean(xc * xc) + EPS)
    xhat = xc * rstd
    return xhat, rstd, xhat * lg + lb


def _branch_b_fwd(proj, wo4_g, cw_b, conv_b_b, ln_g, ln_b, tm):
    n_rows = proj.shape[0]

    def body(vb, gb, zb, vbh, gbh, w_ref, cw_ref, bb_ref, lg_ref, lb_ref, cb_ref, sb_ref, yb_ref, e_scr, o_scr):
        i = pl.program_id(0)
        vbh, gbh = _f32(vbh), _f32(gbh)
        lg, lb = lg_ref[...], lb_ref[...]
        _to_time_major(e_scr, 0, jnp.where(i > 0, vbh * jax.nn.sigmoid(gbh), 0.0))
        for r0 in range(0, tm, ELEM_ROWS):
            rows = pl.ds(r0, ELEM_ROWS)
            _to_time_major(e_scr, HALO + r0, vb[rows, :].astype(F32) * jax.nn.sigmoid(gb[rows, :].astype(F32)))
        _conv(o_scr, e_scr, cw_ref, FWD_TAPS_B, tm, bias_ref=bb_ref)
        for r0 in range(0, tm, ELEM_ROWS):
            rows = pl.ds(r0, ELEM_ROWS)
            cb = _from_time_major(o_scr, ELEM_ROWS, r0)
            cb_ref[rows, :] = cb
            _, _, ln = _layernorm_parts(cb, lg, lb)
            sb_ref[rows, :] = (jax.nn.silu(zb[rows, :].astype(F32)) * jax.nn.silu(ln)).astype(BF16)
        yb_ref[...] = _dot(sb_ref[...], w_ref[...].reshape(D, D))

    return pl.pallas_call(
        body, name="branch_b_fwd", grid=(n_rows // tm,),
        in_specs=[_rows(tm, G_VB), _rows(tm, G_GB), _rows(tm, G_ZB), _prev_halo(tm, G_VB), _prev_halo(tm, G_GB),
                  _w_out_spec(1), _const(cw_b.shape), _const((LANE_GROUPS, LANES)), _const((1, D)), _const((1, D))],
        out_specs=[_rows(tm), _rows(tm), _rows(tm)],
        out_shape=[jax.ShapeDtypeStruct((n_rows, D), F32), jax.ShapeDtypeStruct((n_rows, D), BF16),
                   jax.ShapeDtypeStruct((n_rows, D), F32)],
        scratch_shapes=[_time_major(tm + HALO), _time_major(tm)],
        compiler_params=_params(1),
    )(proj, proj, proj, proj, proj, wo4_g, cw_b, conv_b_b.reshape(LANE_GROUPS, LANES), ln_g, ln_b)


def _attention(q16, kv_ref):
    probs, outs = [], []
    for h in range(N_HEADS):
        s = _dot_nt(q16[:, h * HEAD_DIM:(h + 1) * HEAD_DIM], kv_ref[h]) * (HEAD_DIM ** -0.5)
        e = jnp.exp(s - jnp.max(s, axis=-1, keepdims=True))
        p = e / jnp.sum(e, axis=-1, keepdims=True)
        probs.append(p)
        outs.append(_dot(p.astype(BF16), kv_ref[N_HEADS + h]))
    return probs, outs


def _branch_x_fwd(proj, kv16, wo4_g, tm):
    n_rows = proj.shape[0]

    def body(q, zx, kv_ref, w_ref, sx_ref, yx_ref, p_ref):
        probs, outs = _attention(q[...].astype(BF16), kv_ref)
        p_ref[...] = jnp.concatenate(probs, axis=-1)
        sx = (jax.nn.silu(_f32(zx)) * jnp.concatenate(outs, axis=-1)).astype(BF16)
        sx_ref[...] = sx
        yx_ref[...] = _dot(sx, w_ref[...].reshape(D, D))

    return pl.pallas_call(
        body, name="branch_x_fwd", grid=(n_rows // tm,),
        in_specs=[_rows(tm, G_Q), _rows(tm, G_ZX), _const(kv16.shape), _w_out_spec(2)],
        out_specs=[_rows(tm), _rows(tm), _rows(tm)],
        out_shape=[jax.ShapeDtypeStruct((n_rows, D), BF16), jax.ShapeDtypeStruct((n_rows, D), F32),
                   jax.ShapeDtypeStruct((n_rows, D), F32)],
        compiler_params=_params(1),
    )(proj, proj, kv16, wo4_g)


def _merge_fwd_bwd(proj, ya, yb, yx, x, target, wo4_g, final_g, tm):
    n_rows = proj.shape[0]
    inv_d = 1.0 / D

    def body(ga, gb, gx, ya_ref, yb_ref, yx_ref, x_ref, t_ref, w_ref, fg_ref,
             dh_ref, dya_ref, dyb_ref, dyx_ref, dp_ref, dw_ref, dfg_ref, sq_ref):
        i = pl.program_id(0)
        wo = w_ref[...].reshape(D, D)
        sig = [jax.nn.sigmoid(_f32(g)) for g in (ga, gb, gx)]
        ys = [ya_ref[...], yb_ref[...], yx_ref[...]]
        m16 = (sig[0] * ys[0] + sig[1] * ys[1] + sig[2] * ys[2]).astype(BF16)
        h = x_ref[...] + _dot(m16, wo)
        r = lax.rsqrt(_mean(h * h) + EPS)
        hn = h * r
        fg = fg_ref[...]
        err = hn * fg - t_ref[...]
        dy = err * inv_d
        dhn = dy * fg
        dh = r * (dhn - hn * _mean(dhn * hn))
        dh_ref[...] = dh
        dh16 = dh.astype(BF16)
        dm = _dot_nt(dh16, wo)
        for n, out in enumerate((dya_ref, dyb_ref, dyx_ref)):
            out[...] = (sig[n] * dm).astype(BF16)
            dp_ref[:, n * D:(n + 1) * D] = (dm * ys[n] * (sig[n] * (1.0 - sig[n]))).astype(BF16)

        @pl.when(i == 0)
        def _():
            dw_ref[...] = jnp.zeros_like(dw_ref)
            dfg_ref[...] = jnp.zeros_like(dfg_ref)
            sq_ref[...] = jnp.zeros_like(sq_ref)

        dw_ref[0] += _dot_tn(m16, dh16)
        dfg_ref[...] += _fold8(dy * hn)
        sq_ref[...] += _fold8(err * err)

    vec = jax.ShapeDtypeStruct((SUBLANES, D), F32)
    return pl.pallas_call(
        body, name="merge_fwd_bwd", grid=(n_rows // tm,),
        in_specs=[_rows(tm, G_GA), _rows(tm, G_GBB), _rows(tm, G_GX), _rows(tm), _rows(tm), _rows(tm), _rows(tm),
                  _rows(tm), _w_out_spec(3), _const((1, D))],
        out_specs=[_rows(tm), _rows(tm), _rows(tm), _rows(tm), pl.BlockSpec((tm, 3 * D), lambda i: (i, 3)),
                   pl.BlockSpec((1, D, D), lambda i: (3, 0, 0)), _const((SUBLANES, D)), _const((SUBLANES, D))],
        out_shape=[jax.ShapeDtypeStruct((n_rows, D), F32), jax.ShapeDtypeStruct((n_rows, D), BF16),
                   jax.ShapeDtypeStruct((n_rows, D), BF16), jax.ShapeDtypeStruct((n_rows, D), BF16),
                   jax.ShapeDtypeStruct((n_rows, N_GROUPS * D), BF16), jax.ShapeDtypeStruct((4, D, D), F32), vec, vec],
        compiler_params=_params(1),
    )(proj, proj, proj, ya, yb, yx, x, target, wo4_g, final_g)


def _branch_a_bwd(dya, proj, sa16, wo4_g, cw_a, dproj, dw4, tm):
    n_rows = proj.shape[0]
    n_tiles = n_rows // tm

    def body(dya_ref, bp, cp, xp, za, sa_ref, dyan, bpn, zan, cph, xph, w_ref, cw_ref, dp_in, dw_in,
             dp_ref, dw_ref, dwa_ref, e1, e2, o_scr, mm_scr):
        del dp_in, dw_in
        i = pl.program_id(0)
        bpn, zan, cph, xph = (_f32(r) for r in (bpn, zan, cph, xph))
        woa = w_ref[...].reshape(D, D)
        dya16 = dya_ref[...]
        chunks = [pl.ds(r0, ELEM_ROWS) for r0 in range(0, tm, ELEM_ROWS)]
        _to_time_major(e1, 0, jnp.where(i > 0, cph * xph, 0.0))
        for r0, rows in zip(range(0, tm, ELEM_ROWS), chunks):
            _to_time_major(e1, HALO + r0, cp[rows, :].astype(F32) * xp[rows, :].astype(F32))
        _conv(o_scr, e1, cw_ref, FWD_TAPS_A, tm)
        mm_scr[...] = _dot_nt(dya16, woa)
        for r0, rows in zip(range(0, tm, ELEM_ROWS), chunks):
            ca = _from_time_major(o_scr, ELEM_ROWS, r0)
            dsa = mm_scr[rows, :]
            b = bp[rows, :].astype(F32)
            silu_z, dsilu_z = _silu_and_grad(za[rows, :].astype(F32))
            t = dsa * silu_z
            dp_ref[rows, 0 * D:1 * D] = (t * ca).astype(BF16)
            dp_ref[rows, 3 * D:4 * D] = (dsa * (b * ca) * dsilu_z).astype(BF16)
            _to_time_major(e2, r0, t * b)
        dcan = (_dot_nt(dyan[...], woa) * jax.nn.silu(zan)) * bpn
        _to_time_major(e2, tm, jnp.where(i < n_tiles - 1, dcan, 0.0))

        @pl.when(i == 0)
        def _():
            dw_ref[...] = jnp.zeros_like(dw_ref)
            dwa_ref[...] = jnp.zeros_like(dwa_ref)

        _conv_wgrad(dwa_ref, e2, e1, FWD_TAPS_A, tm)
        dw_ref[0] += _dot_tn(sa_ref[...], dya16)
        _conv(o_scr, e2, cw_ref, BWD_TAPS_A, tm)
        for r0, rows in zip(range(0, tm, ELEM_ROWS), chunks):
            dprod = _from_time_major(o_scr, ELEM_ROWS, r0)
            dp_ref[rows, 1 * D:2 * D] = (dprod * xp[rows, :].astype(F32)).astype(BF16)
            dp_ref[rows, 2 * D:3 * D] = (dprod * cp[rows, :].astype(F32)).astype(BF16)

    return pl.pallas_call(
        body, name="branch_a_bwd", grid=(n_tiles,),
        in_specs=[_rows(tm), _rows(tm, G_BA), _rows(tm, G_CA), _rows(tm, G_XA), _rows(tm, G_ZA), _rows(tm),
                  _next_halo(tm, n_rows), _next_halo(tm, n_rows, G_BA), _next_halo(tm, n_rows, G_ZA),
                  _prev_halo(tm, G_CA), _prev_halo(tm, G_XA), _w_out_spec(0), _const(cw_a.shape), ANY, ANY],
        out_specs=[pl.BlockSpec((tm, 4 * D), lambda i: (i, 0)), pl.BlockSpec((1, D, D), lambda i: (0, 0, 0)),
                   _const((K_A * LANE_GROUPS, LANES))],
        out_shape=[jax.ShapeDtypeStruct(dproj.shape, BF16), jax.ShapeDtypeStruct(dw4.shape, F32),
                   jax.ShapeDtypeStruct((K_A * LANE_GROUPS, LANES), F32)],
        input_output_aliases={13: 0, 14: 1},
        scratch_shapes=[_time_major(tm + HALO), _time_major(tm + HALO), _time_major(tm), pltpu.VMEM((tm, D), F32)],
        compiler_params=_params(1),
    )(dya, proj, proj, proj, proj, sa16, dya, proj, proj, proj, proj, wo4_g, cw_a, dproj, dw4)


def _branch_b_bwd(dyb, proj, cb, sb16, wo4_g, cw_b, ln_g, ln_b, dproj, dw4, tm):
    n_rows = proj.shape[0]
    n_tiles = n_rows // tm

    def body(dyb_ref, zb, cb_ref, vb, gb, sb_ref, dybn, zbn, cbn, vbh, gbh, w_ref, cw_ref, lg_ref, lb_ref,
             dp_in, dw_in, dp_ref, dw_ref, dwb_ref, dbb_ref, dlg_ref, dlb_ref, e1, e2, o_scr, mm_scr):
        del dp_in, dw_in
        zbn, vbh, gbh = (_f32(r) for r in (zbn, vbh, gbh))
        i = pl.program_id(0)
        wob = w_ref[...].reshape(D, D)
        lg, lb = lg_ref[...], lb_ref[...]

        def conv_out_grad(dsb, z, c):
            xhat, rstd, ln = _layernorm_parts(c, lg, lb)
            sw, dsw = _silu_and_grad(ln)
            sz, dsz = _silu_and_grad(z)
            dln = (dsb * sz) * dsw
            dxhat = dln * lg
            dcb = rstd * (dxhat - _mean(dxhat) - xhat * _mean(dxhat * xhat))
            return dsb * sw * dsz, dln, xhat, dcb

        @pl.when(i == 0)
        def _():
            dw_ref[...] = jnp.zeros_like(dw_ref)
            dwb_ref[...] = jnp.zeros_like(dwb_ref)
            dbb_ref[...] = jnp.zeros_like(dbb_ref)
            dlg_ref[...] = jnp.zeros_like(dlg_ref)
            dlb_ref[...] = jnp.zeros_like(dlb_ref)

        dyb16 = dyb_ref[...]
        mm_scr[...] = _dot_nt(dyb16, wob)
        dlg, dlb, dbb = (jnp.zeros((SUBLANES, D), F32),) * 3
        for r0 in range(0, tm, ELEM_ROWS):
            rows = pl.ds(r0, ELEM_ROWS)
            dzb, dln, xhat, dcb = conv_out_grad(mm_scr[rows, :], zb[rows, :].astype(F32), cb_ref[rows, :])
            dp_ref[rows, 2 * D:3 * D] = dzb.astype(BF16)
            _to_time_major(e2, r0, dcb)
            dlg, dlb, dbb = dlg + _fold8(dln * xhat), dlb + _fold8(dln), dbb + _fold8(dcb)
        _, _, _, dcbn = conv_out_grad(_dot_nt(dybn[...], wob), zbn[...], cbn[...])
        _to_time_major(e2, tm, jnp.where(i < n_tiles - 1, dcbn, 0.0))
        dlg_ref[...] += dlg
        dlb_ref[...] += dlb
        dbb_ref[...] += dbb
        dw_ref[0] += _dot_tn(sb_ref[...], dyb16)
        _to_time_major(e1, 0, jnp.where(i > 0, vbh[...] * jax.nn.sigmoid(gbh[...]), 0.0))
        for r0 in range(0, tm, ELEM_ROWS):
            rows = pl.ds(r0, ELEM_ROWS)
            sg = jax.nn.sigmoid(gb[rows, :].astype(F32))
            mm_scr[rows, :] = sg
            _to_time_major(e1, HALO + r0, vb[rows, :].astype(F32) * sg)
        _conv_wgrad(dwb_ref, e2, e1, FWD_TAPS_B, tm)
        _conv(o_scr, e2, cw_ref, BWD_TAPS_B, tm)
        for r0 in range(0, tm, ELEM_ROWS):
            rows = pl.ds(r0, ELEM_ROWS)
            dglu = _from_time_major(o_scr, ELEM_ROWS, r0)
            sg = mm_scr[rows, :]
            dp_ref[rows, 0 * D:1 * D] = (dglu * sg).astype(BF16)
            dp_ref[rows, 1 * D:2 * D] = (dglu * vb[rows, :].astype(F32) * (sg * (1.0 - sg))).astype(BF16)

    vec = jax.ShapeDtypeStruct((SUBLANES, D), F32)
    return pl.pallas_call(
        body, name="branch_b_bwd", grid=(n_tiles,),
        in_specs=[_rows(tm), _rows(tm, G_ZB), _rows(tm), _rows(tm, G_VB), _rows(tm, G_GB), _rows(tm),
                  _next_halo(tm, n_rows), _next_halo(tm, n_rows, G_ZB), _next_halo(tm, n_rows),
                  _prev_halo(tm, G_VB), _prev_halo(tm, G_GB), _w_out_spec(1), _const(cw_b.shape), _const((1, D)),
                  _const((1, D)), ANY, ANY],
        out_specs=[pl.BlockSpec((tm, 3 * D), lambda i: (i, 2)), pl.BlockSpec((1, D, D), lambda i: (1, 0, 0)),
                   _const((K_B * LANE_GROUPS, LANES)), _const((SUBLANES, D)), _const((SUBLANES, D)),
                   _const((SUBLANES, D))],
        out_shape=[jax.ShapeDtypeStruct(dproj.shape, BF16), jax.ShapeDtypeStruct(dw4.shape, F32),
                   jax.ShapeDtypeStruct((K_B * LANE_GROUPS, LANES), F32), vec, vec, vec],
        input_output_aliases={15: 0, 16: 1},
        scratch_shapes=[_time_major(tm + HALO), _time_major(tm + HALO), _time_major(tm), pltpu.VMEM((tm, D), F32)],
        compiler_params=_params(1),
    )(dyb, proj, cb, proj, proj, sb16, dyb, proj, cb, proj, proj, wo4_g, cw_b, ln_g, ln_b, dproj, dw4)


def _branch_x_bwd(dyx, proj, sx16, probs, kv16, wo4_g, dproj, dw4, tm):
    n_rows = proj.shape[0]
    scale = HEAD_DIM ** -0.5

    def body(dyx_ref, q, zx, sx_ref, p_ref, kv_ref, w_ref, dp_in, dw_in, dp_ref, dw_ref, dkv_ref):
        del dp_in, dw_in
        i = pl.program_id(0)
        dyx16 = dyx_ref[...]
        q16 = q[...].astype(BF16)
        probs = [p_ref[:, h * HEAD_DIM:(h + 1) * HEAD_DIM] for h in range(N_HEADS)]
        outs = [_dot(probs[h].astype(BF16), kv_ref[N_HEADS + h]) for h in range(N_HEADS)]
        dsx = _dot_nt(dyx16, w_ref[...].reshape(D, D))
        silu_z, dsilu_z = _silu_and_grad(_f32(zx))
        dp_ref[:, D:2 * D] = (dsx * jnp.concatenate(outs, axis=-1) * dsilu_z).astype(BF16)
        do16 = (dsx * silu_z).astype(BF16)

        @pl.when(i == 0)
        def _():
            dw_ref[...] = jnp.zeros_like(dw_ref)
            dkv_ref[...] = jnp.zeros_like(dkv_ref)

        for h in range(N_HEADS):
            cols = slice(h * HEAD_DIM, (h + 1) * HEAD_DIM)
            p = probs[h]
            dprob = _dot_nt(do16[:, cols], kv_ref[N_HEADS + h])
            ds16 = ((p * (dprob - jnp.sum(p * dprob, axis=-1, keepdims=True))) * scale).astype(BF16)
            dp_ref[:, cols] = _dot(ds16, kv_ref[h]).astype(BF16)
            dkv_ref[h] += _dot_tn(ds16, q16[:, cols])
            dkv_ref[N_HEADS + h] += _dot_tn(p.astype(BF16), do16[:, cols])
        dw_ref[0] += _dot_tn(sx_ref[...], dyx16)

    return pl.pallas_call(
        body, name="branch_x_bwd", grid=(n_rows // tm,),
        in_specs=[_rows(tm), _rows(tm, G_Q), _rows(tm, G_ZX), _rows(tm), _rows(tm), _const(kv16.shape),
                  _w_out_spec(2), ANY, ANY],
        out_specs=[pl.BlockSpec((tm, 2 * D), lambda i: (i, 2)), pl.BlockSpec((1, D, D), lambda i: (2, 0, 0)),
                   _const(kv16.shape)],
        out_shape=[jax.ShapeDtypeStruct(dproj.shape, BF16), jax.ShapeDtypeStruct(dw4.shape, F32),
                   jax.ShapeDtypeStruct(kv16.shape, F32)],
        input_output_aliases={7: 0, 8: 1},
        compiler_params=_params(1),
    )(dyx, proj, proj, sx16, probs, kv16, wo4_g, dproj, dw4)


def _dp_unit(u):
    g = u // 2
    pos = jnp.where(g < G_VB, g, jnp.where(g < G_Q, g + 2, jnp.where(g < G_GA, g - 3, g)))
    return 2 * pos + u % 2


def _scatter_copies(srcs, lands, send, recv):
    x, y, c = lax.axis_index("x"), lax.axis_index("y"), lax.axis_index("c")
    copies = []
    for n in range(N_DEV - 1):
        flip = n + 1
        px = 1 - x if flip & 4 else x
        py = 1 - y if flip & 2 else y
        pc = 1 - c if flip & 1 else c
        for t, (src, land) in enumerate(zip(srcs, lands)):
            copies.append(pltpu.make_async_remote_copy(
                src_ref=src.at[4 * px + 2 * py + pc], dst_ref=land.at[n], send_sem=send.at[t * (N_DEV - 1) + n],
                recv_sem=recv.at[t * (N_DEV - 1) + n], device_id=(px, py, pc), device_id_type=MESH))
    return copies


def _scatter_start(name, arrays, views, n_views):
    n = len(arrays)
    lands = [lax.empty(tuple(N_DEV - 1 if d == N_DEV else d for d in a.shape), a.dtype) for a in arrays]

    def body(*refs):
        src, land, (send, recv) = refs[:n], refs[n:2 * n], refs[2 * n:2 * n + 2]
        token = refs[-1]
        for cp in _scatter_copies(views(src), views(land), send, recv):
            cp.start()
        token[...] = jnp.zeros_like(token)

    sems = pltpu.SemaphoreType.DMA((n_views * (N_DEV - 1),))
    out = pl.pallas_call(
        body, name=name,
        in_specs=[HBM] * (2 * n),
        out_specs=[SEM, SEM] + [HBM] * (2 * n) + [pl.BlockSpec(memory_space=pltpu.VMEM)],
        out_shape=[sems, sems] + [pltpu.HBM(a.shape, a.dtype) for a in arrays + lands]
        + [jax.ShapeDtypeStruct((SUBLANES, LANES), F32)],
        input_output_aliases={k: 2 + k for k in range(2 * n)},
        compiler_params=pltpu.CompilerParams(has_side_effects=SIDE_EFFECT),
    )(*[pltpu.with_memory_space_constraint(a, pltpu.HBM) for a in arrays + lands])
    return dict(name=name, sems=out[:2], moving=out[2:2 + 2 * n], views=views, token=out[-1])


def _scatter_wait(started, after):
    n = len(started["moving"]) // 2
    views = started["views"]

    def body(*refs):
        src, land, (send, recv) = refs[:n], refs[n:2 * n], refs[2 * n:2 * n + 2]
        for cp in _scatter_copies(views(src), views(land), send, recv):
            cp.wait_send()
            cp.wait_recv()

    out = pl.pallas_call(
        body, name=started["name"].replace("start", "wait"),
        in_specs=[HBM] * (2 * n) + [SEM, SEM, ANY],
        out_specs=[HBM] * (2 * n),
        out_shape=[pltpu.HBM(a.shape, a.dtype) for a in started["moving"]],
        input_output_aliases={k: k for k in range(2 * n)},
        compiler_params=pltpu.CompilerParams(has_side_effects=SIDE_EFFECT),
    )(*started["moving"], *started["sems"], after)
    return out[n:]


def _w_in_grad(ut, dproj, tk, token):
    n_rows = dproj.shape[0]
    n_k = n_rows // tk
    per_shard = W_IN_SHARD // UNIT

    def body(ut_ref, dp0, dp1, dp2, token_ref, out_ref, out16_ref, acc):
        del token_ref
        t = pl.program_id(1)

        for r, dp_ref in enumerate((dp0, dp1, dp2)):
            cols = slice(r * UNIT, (r + 1) * UNIT)

            @pl.when(t == 0)
            def _(dp_ref=dp_ref, cols=cols):
                acc[:, cols] = _dot(ut_ref[...], dp_ref[...])

            @pl.when(t > 0)
            def _(dp_ref=dp_ref, cols=cols):
                acc[:, cols] += _dot(ut_ref[...], dp_ref[...])

        @pl.when(t == n_k - 1)
        def _():
            out_ref[0] = acc[...]
            out16_ref[0] = acc[...].astype(BF16)

    def dp_spec(r):
        return pl.BlockSpec((tk, UNIT), lambda q, t: (t, _dp_unit(per_shard * q + r)))

    shard = pl.BlockSpec((1, D, W_IN_SHARD), lambda q, t: (q, 0, 0))
    return pl.pallas_call(
        body, name="w_in_grad", grid=(N_DEV, n_k),
        in_specs=[pl.BlockSpec((D, tk), lambda q, t: (0, t)), dp_spec(0), dp_spec(1), dp_spec(2), ANY],
        out_specs=[shard, shard],
        out_shape=[jax.ShapeDtypeStruct((N_DEV, D, W_IN_SHARD), F32), jax.ShapeDtypeStruct((N_DEV, D, W_IN_SHARD), BF16)],
        scratch_shapes=[pltpu.VMEM((D, W_IN_SHARD), F32)],
        compiler_params=_params(2),
    )(ut, dproj, dproj, dproj, token)


def _x_grad(dproj, win_t, x, dh, norm_g, token, tm):
    n_rows = x.shape[0]
    n_k = N_GROUPS * D // X_GRAD_K

    def body(dp_ref, wt_ref, x_ref, dh_ref, g_ref, token_ref, gx_ref, dg_ref, acc):
        del token_ref
        i, g = pl.program_id(0), pl.program_id(1)

        @pl.when((i == 0) & (g == 0))
        def _():
            dg_ref[...] = jnp.zeros_like(dg_ref)

        @pl.when(g == 0)
        def _():
            acc[...] = _dot(dp_ref[...], wt_ref[...])

        @pl.when(g > 0)
        def _():
            acc[...] += _dot(dp_ref[...], wt_ref[...])

        @pl.when(g == n_k - 1)
        def _():
            du = acc[...]
            xf = x_ref[...]
            r = lax.rsqrt(_mean(xf * xf) + EPS)
            xn = xf * r
            dun = du * g_ref[...]
            gx_ref[...] = dh_ref[...] + r * (dun - xn * _mean(dun * xn))
            dg_ref[...] += _fold8(du * xn)

    return pl.pallas_call(
        body, name="x_grad", grid=(n_rows // tm, n_k),
        in_specs=[pl.BlockSpec((tm, X_GRAD_K), lambda i, g: (i, g)), pl.BlockSpec((X_GRAD_K, D), lambda i, g: (g, 0)),
                  pl.BlockSpec((tm, D), lambda i, g: (i, 0)), pl.BlockSpec((tm, D), lambda i, g: (i, 0)),
                  _const((1, D)), ANY],
        out_specs=[pl.BlockSpec((tm, D), lambda i, g: (i, 0)), _const((SUBLANES, D))],
        out_shape=[jax.ShapeDtypeStruct((n_rows, D), F32), jax.ShapeDtypeStruct((SUBLANES, D), F32)],
        scratch_shapes=[pltpu.VMEM((tm, D), F32)],
        compiler_params=_params(2),
    )(dproj, win_t, x, dh, norm_g, token)


def _local_step(x, mem, target, norm_g, conv_b_b, ln_g, ln_b, mem_g, final_g, shards):
    n_rows = x.shape[0]
    tm = min(512, n_rows)
    big = min(1024, n_rows)
    u16, ut = _rmsnorm_fwd(x, norm_g, big)
    proj, win_t, _, wkv_g, wo4_g, cw_g = _proj_fwd_gather(u16, shards, min(2048, n_rows))
    cw_rows = cw_g.transpose(1, 0, 2).reshape((SUBLANES + HALO) * LANE_GROUPS, LANES)
    cw_a, cw_b = cw_rows[:SUBLANES * LANE_GROUPS], cw_rows[SUBLANES * LANE_GROUPS:]
    kv16, mn16 = _kv_fwd(mem, mem_g, wkv_g)
    sa16, ya = _branch_a_fwd(proj, wo4_g, cw_a, tm)
    cb, sb16, yb = _branch_b_fwd(proj, wo4_g, cw_b, conv_b_b, ln_g, ln_b, tm)
    sx16, yx, probs = _branch_x_fwd(proj, kv16, wo4_g, big)
    dh, dya, dyb, dyx, dproj, dw4, dfg, sq = _merge_fwd_bwd(proj, ya, yb, yx, x, target, wo4_g, final_g,
                                                             min(256, n_rows))
    dproj, dw4, dwa = _branch_a_bwd(dya, proj, sa16, wo4_g, cw_a, dproj, dw4, tm)
    dproj, dw4, dwb, dbb, dlg, dlb = _branch_b_bwd(dyb, proj, cb, sb16, wo4_g, cw_b, ln_g, ln_b, dproj, dw4, tm)
    dproj, dw4, dkv = _branch_x_bwd(dyx, proj, sx16, probs, kv16, wo4_g, dproj, dw4, tm)
    dwkv_g, dwkv16, dmg = _kv_bwd(dkv, mem, mem_g, mn16, wkv_g)
    dw4 = dw4.reshape(4, N_DEV, D // N_DEV, D)
    small_moving = _scatter_start("small_grads_start", [dw4.astype(BF16), dwkv16],
                                  lambda refs: [refs[0].at[w] for w in range(4)] + [refs[1]], 5)
    dwin_g, dwin16 = _w_in_grad(ut, dproj, min(2048, n_rows), small_moving["token"])
    w_in_moving = _scatter_start("w_in_grad_start", [dwin16], lambda refs: list(refs), 1)
    gx, dng = _x_grad(dproj, win_t, x, dh, norm_g, w_in_moving["token"], big)
    land4, landkv = _scatter_wait(small_moving, dng)
    landin, = _scatter_wait(w_in_moving, dng)
    small = {SV_NORM_G: dng, SV_CONV_B_B: dbb, SV_LN_G: dlg, SV_LN_B: dlb, SV_MEM_G: dmg, SV_FINAL_G: dfg, SV_LOSS: sq}
    grads = [(dwin_g[None], landin[None]), (dw4, land4), (dwkv_g[None], landkv[None])]
    return gx, grads, small, dwa.reshape(K_A, D), dwb.reshape(K_B, D)


def _allgather_small(small, conv_rows):
    keys = sorted(small)

    def body(*refs):
        parts, (conv_ref, out_ref, mine, send, recv) = refs[:len(keys)], refs[len(keys):]
        x, y, c, chips = _place()
        me, sibling = 4 * x + 2 * y + c, (x, y, 1 - c)
        mine[pl.ds(0, SV_CONV_A), :] = jnp.zeros((SV_CONV_A, D), F32)
        for key, part in zip(keys, parts):
            mine[key:key + 1, :] = jnp.sum(part[...], axis=0, keepdims=True)
        mine[pl.ds(SV_CONV_A, SV_ROWS - SV_CONV_A), :] = conv_ref[...]
        out_ref[me] = mine[...]

        def copy(k, block, to, from_mine=False):
            return pltpu.make_async_remote_copy(
                src_ref=mine if from_mine else out_ref.at[block], dst_ref=out_ref.at[block],
                send_sem=send.at[k], recv_sem=recv.at[k], device_id=to, device_id_type=MESH)

        first = [copy(0, me, sibling, from_mine=True)]
        first += [copy(1 + j, me, (*chip, c), from_mine=True) for j, chip in enumerate(chips)]
        for cp in first:
            cp.start()
        passed = []
        for j, (px, py) in enumerate(chips):
            block = 4 * px + 2 * py + c
            copy(1 + j, block, sibling).wait_recv()
            passed.append(copy(4 + j, block, sibling))
            passed[-1].start()
        copy(0, 4 * x + 2 * y + 1 - c, sibling).wait_recv()
        for j, (px, py) in enumerate(chips):
            copy(4 + j, 4 * px + 2 * py + 1 - c, sibling).wait_recv()
        for cp in first + passed:
            cp.wait_send()

    vmem = pl.BlockSpec(memory_space=pltpu.VMEM)
    return pl.pallas_call(
        body, name="allgather_small",
        in_specs=[vmem] * (len(keys) + 1), out_specs=vmem,
        out_shape=jax.ShapeDtypeStruct((N_DEV, SV_ROWS, D), F32),
        scratch_shapes=[pltpu.VMEM((SV_ROWS, D), F32), pltpu.SemaphoreType.DMA((7,)), pltpu.SemaphoreType.DMA((7,))],
    )(*[small[k] for k in keys], conv_rows)


def _adamw(w, g, m, v):
    m = ADAM_B1 * m + (1.0 - ADAM_B1) * g
    v = ADAM_B2 * v + (1.0 - ADAM_B2) * (g * g)
    m_hat = m / (1.0 - ADAM_B1 ** ADAM_STEP)
    v_hat = v / (1.0 - ADAM_B2 ** ADAM_STEP)
    return -ADAM_LR * (m_hat / (jnp.sqrt(v_hat) + ADAM_EPS) + ADAM_WD * w), m, v


def _adamw_shard(own, landed, piece, k_arr, w, m, v, tr):
    n_r, n_c = w.shape
    n_landed = landed.shape[1]

    def body(k_ref, own_ref, *refs):
        del k_ref
        landed_refs, (w_ref, m_ref, v_ref, g_out, d_out, m_out, v_out) = refs[:n_landed], refs[n_landed:]
        g = own_ref[0, 0]
        for landed_ref in landed_refs:
            g = g + landed_ref[0, 0].astype(F32)
        g_out[...] = g
        d_out[...], m_out[...], v_out[...] = _adamw(w_ref[...], g, m_ref[...], v_ref[...])

    blk = (1, 1, tr, n_c)
    flat = pl.BlockSpec((tr, n_c), lambda r, k: (r, 0))
    return pl.pallas_call(
        body, name="adamw_shard",
        grid_spec=pltpu.PrefetchScalarGridSpec(
            num_scalar_prefetch=1, grid=(n_r // tr,),
            in_specs=[pl.BlockSpec(blk, lambda r, k: (piece, k[0], r, 0))]
            + [pl.BlockSpec(blk, functools.partial(lambda r, k, j: (piece, j, r, 0), j=j)) for j in range(n_landed)]
            + [flat] * 3,
            out_specs=[flat] * 4),
        out_shape=[jax.ShapeDtypeStruct((n_r, n_c), F32)] * 4,
        compiler_params=_params(1),
    )(k_arr, own, *([landed] * n_landed), w, m, v)


def _adamw_shards(entries, k_arr):
    n = len(entries)

    def body(k_ref, *refs):
        del k_ref
        ins, outs = refs[:5 * n], refs[5 * n:]
        for e in range(n):
            own_ref, landed_ref, w_ref, m_ref, v_ref = ins[5 * e:5 * e + 5]
            g = own_ref[0, 0]
            for j in range(landed_ref.shape[1]):
                g = g + landed_ref[0, j].astype(F32)
            g_out, d_out, m_out, v_out = outs[4 * e:4 * e + 4]
            g_out[...] = g
            d_out[...], m_out[...], v_out[...] = _adamw(w_ref[...], g, m_ref[...], v_ref[...])

    in_specs, operands, out_specs, out_shape = [], [], [], []
    for own, landed, piece, w, m, v in entries:
        flat = pl.BlockSpec(w.shape, lambda i, k: (0, 0))
        in_specs += [pl.BlockSpec((1, 1, *w.shape), functools.partial(lambda i, k, p: (p, k[0], 0, 0), p=piece)),
                     pl.BlockSpec((1, *landed.shape[1:]), functools.partial(lambda i, k, p: (p, 0, 0, 0), p=piece)),
                     flat, flat, flat]
        operands += [own, landed, w, m, v]
        out_specs += [flat] * 4
        out_shape += [jax.ShapeDtypeStruct(w.shape, F32)] * 4
    out = pl.pallas_call(
        body, name="adamw_shards",
        grid_spec=pltpu.PrefetchScalarGridSpec(num_scalar_prefetch=1, grid=(1,), in_specs=in_specs,
                                               out_specs=out_specs),
        out_shape=out_shape,
        compiler_params=_params(1),
    )(k_arr, *operands)
    return [tuple(out[4 * e:4 * e + 4]) for e in range(n)]


def _adamw_small(gathered, k_arr, vectors, conv_a, conv_b):
    n_vec = len(vectors)
    cols = D // N_DEV

    def body(k_ref, full_ref, cols_ref, *refs):
        del k_ref
        ins, outs = refs[:3 * (n_vec + 2)], refs[3 * (n_vec + 2):-2]
        tot, tot_cols = refs[-2:]
        tot[...] = full_ref[0]
        tot_cols[...] = cols_ref[0]
        for dev in range(1, N_DEV):
            tot[...] += full_ref[dev]
            tot_cols[...] += cols_ref[dev]
        loss = (0.5 / D) * jnp.sum(tot[SV_LOSS:SV_LOSS + 1, :])
        outs[0][...] = jnp.full(outs[0].shape, loss, F32)
        grads = [tot[n:n + 1, :] for n in range(n_vec)]
        grads += [tot_cols[pl.ds(SV_CONV_A, K_A), :], tot_cols[pl.ds(SV_CONV_B, K_B), :]]
        for n, g in enumerate(grads):
            w_ref, m_ref, v_ref = ins[3 * n:3 * n + 3]
            g_out, d_out, m_out, v_out = outs[1 + 4 * n:5 + 4 * n]
            g_out[...] = g
            d_out[...], m_out[...], v_out[...] = _adamw(w_ref[...], g, m_ref[...], v_ref[...])

    weights = list(vectors) + [conv_a, conv_b]
    flat_in = [a for wmv in weights for a in wmv]
    out_shape = [jax.ShapeDtypeStruct((SUBLANES, 128), F32)]
    for wmv in weights:
        out_shape += [jax.ShapeDtypeStruct(wmv[0].shape, F32)] * 4
    return pl.pallas_call(
        body, name="adamw_small",
        grid_spec=pltpu.PrefetchScalarGridSpec(
            num_scalar_prefetch=1, grid=(1,),
            in_specs=[pl.BlockSpec((N_DEV, SV_ROWS, D), lambda i, k: (0, 0, 0)),
                      pl.BlockSpec((N_DEV, SV_ROWS, cols), lambda i, k: (0, 0, k[0]))]
            + [pl.BlockSpec(a.shape, lambda i, k: (0, 0)) for a in flat_in],
            out_specs=[pl.BlockSpec(s.shape, lambda i, k: (0, 0)) for s in out_shape],
            scratch_shapes=[pltpu.VMEM((SV_ROWS, D), F32), pltpu.VMEM((SV_ROWS, cols), F32)]),
        out_shape=out_shape,
        compiler_params=_params(1),
    )(k_arr, gathered, gathered, *flat_in)


def kernel(x, mem, norm_g, w_in, conv_a_w, w_out_a, conv_b_w, conv_b_b, ln_b_g, ln_b_b, w_out_b, mem_norm_g, w_kv, w_out_x, w_o, final_g, loss_target, m_norm_g, m_w_in, m_conv_a_w, m_w_out_a, m_conv_b_w, m_conv_b_b, m_ln_b_g, m_ln_b_b, m_w_out_b, m_mem_norm_g, m_w_kv, m_w_out_x, m_w_o, m_final_g, v_norm_g, v_w_in, v_conv_a_w, v_w_out_a, v_conv_b_w, v_conv_b_b, v_ln_b_g, v_ln_b_b, v_w_out_b, v_mem_norm_g, v_w_kv, v_w_out_x, v_w_o, v_final_g):
    xi, yi, ci = lax.axis_index("x"), lax.axis_index("y"), lax.axis_index("c")
    k_arr = jnp.reshape(4 * xi + 2 * yi + ci, (1,)).astype(jnp.int32)

    cw = jnp.concatenate([jnp.pad(conv_a_w[0], ((0, SUBLANES - K_A), (0, 0))),
                          jnp.pad(conv_b_w[0], ((0, HALO - K_B), (0, 0)))], axis=0)
    wo4 = jnp.stack([w_out_a[0], w_out_b[0], w_out_x[0], w_o[0]]).astype(BF16)
    shards = [w_in[0].astype(BF16), w_kv[0].astype(BF16), wo4, cw]

    final_g2 = final_g.reshape(1, D)
    gx, grads, small, dwa, dwb = _local_step(
        x[0], mem[0], loss_target[0], norm_g, conv_b_b, ln_b_g, ln_b_b, mem_norm_g, final_g2, shards)

    conv_rows = jnp.concatenate([jnp.pad(dwa, ((0, SUBLANES - K_A), (0, 0))),
                                 jnp.pad(dwb, ((0, HALO - K_B), (0, 0)))], axis=0)
    gathered_small = _allgather_small(small, conv_rows)

    res = {"w_in": _adamw_shard(grads[0][0], grads[0][1], 0, k_arr, w_in[0], m_w_in[0], v_w_in[0], 256)}
    small_shards = [("w_out_a", 1, 0, w_out_a, m_w_out_a, v_w_out_a), ("w_out_b", 1, 1, w_out_b, m_w_out_b, v_w_out_b),
                    ("w_out_x", 1, 2, w_out_x, m_w_out_x, v_w_out_x), ("w_o", 1, 3, w_o, m_w_o, v_w_o),
                    ("w_kv", 2, 0, w_kv, m_w_kv, v_w_kv)]
    updated = _adamw_shards([(grads[a][0], grads[a][1], l, w[0], m[0], v[0]) for _, a, l, w, m, v in small_shards],
                            k_arr)
    res.update({name: four for (name, *_), four in zip(small_shards, updated)})
    res = {name: tuple(r[None] for r in four) for name, four in res.items()}
    vectors = [(norm_g, m_norm_g, v_norm_g), (conv_b_b, m_conv_b_b, v_conv_b_b), (ln_b_g, m_ln_b_g, v_ln_b_g),
               (ln_b_b, m_ln_b_b, v_ln_b_b), (mem_norm_g, m_mem_norm_g, v_mem_norm_g),
               (final_g2, m_final_g.reshape(1, D), v_final_g.reshape(1, D))]
    out = _adamw_small(gathered_small, k_arr, vectors, (conv_a_w[0], m_conv_a_w[0], v_conv_a_w[0]),
                       (conv_b_w[0], m_conv_b_w[0], v_conv_b_w[0]))
    loss = out[0][0, 0]
    names = ["norm_g", "conv_b_b", "ln_b_g", "ln_b_b", "mem_norm_g", "final_g", "conv_a_w", "conv_b_w"]
    for n, name in enumerate(names):
        four = out[1 + 4 * n:5 + 4 * n]
        if name == "final_g":
            four = [r.reshape(D) for r in four]
        elif name.startswith("conv_") and name.endswith("_w"):
            four = [r[None] for r in four]
        res[name] = tuple(four)

    order = ["norm_g", "w_in", "conv_a_w", "w_out_a", "conv_b_w", "conv_b_b", "ln_b_g", "ln_b_b", "w_out_b",
             "mem_norm_g", "w_kv", "w_out_x", "w_o", "final_g"]
    return (loss, gx[None], *[res[n][0] for n in order], *[res[n][1] for n in order],
            *[res[n][2] for n in order], *[res[n][3] for n in order])
```

```python
import functools

import jax
import jax.numpy as jnp
from jax import lax
from jax.experimental import pallas as pl
from jax.experimental.pallas import tpu as pltpu

F32, BF16 = jnp.float32, jnp.bfloat16
D = 1024
N_DEV = 8
N_HEADS = 4
HEAD_DIM = D // N_HEADS
N_GROUPS = 12
W_IN_SHARD = N_GROUPS * D // N_DEV
UNIT = 512
X_GRAD_K = 2 * D
K_A, K_B = 3, 31
EPS = 1e-6
HALO = 32
SUBLANES = 8
LANES = 128
LANE_GROUPS = D // LANES
TAP_GROUP = 16
CONV_BLOCK = 32
ELEM_ROWS = 16
CONV_PARTIAL_SUMS = 4
VMEM_LIMIT = 56 << 20
MESH = pl.DeviceIdType.MESH
ANY = pl.BlockSpec(memory_space=pl.ANY)
HBM = pl.BlockSpec(memory_space=pltpu.HBM)
SEM = pl.BlockSpec(memory_space=pltpu.SEMAPHORE)
SIDE_EFFECT = pltpu.SideEffectType.DATAFLOW_SIDE_EFFECTING

G_BA, G_CA, G_XA, G_ZA, G_VB, G_GB, G_ZB, G_Q, G_ZX, G_GA, G_GBB, G_GX = range(N_GROUPS)

ADAM_LR, ADAM_B1, ADAM_B2, ADAM_EPS, ADAM_WD, ADAM_STEP = 0.001, 0.9, 0.999, 1e-08, 0.01, 10

SV_NORM_G, SV_CONV_B_B, SV_LN_G, SV_LN_B, SV_MEM_G, SV_FINAL_G, SV_LOSS = range(7)
SV_CONV_A, SV_CONV_B, SV_ROWS = 8, 16, 48


def _dot(a, b):
    return jnp.dot(a, b, preferred_element_type=F32)


def _dot_nt(a, b):
    return lax.dot_general(a, b, (((1,), (1,)), ((), ())), preferred_element_type=F32)


def _dot_tn(a, b):
    return lax.dot_general(a, b, (((0,), (0,)), ((), ())), preferred_element_type=F32)


def _silu_and_grad(z):
    s = jax.nn.sigmoid(z)
    return z * s, s * (1.0 + z * (1.0 - s))


def _fold8(a):
    return a.reshape(a.shape[0] // SUBLANES, SUBLANES, a.shape[1]).sum(axis=0)


def _mean(a):
    return jnp.mean(a, axis=-1, keepdims=True)


def _f32(ref):
    return ref[...].astype(F32)


def _params(n_grid):
    return pltpu.CompilerParams(dimension_semantics=("arbitrary",) * n_grid, vmem_limit_bytes=VMEM_LIMIT)


def _rows(tm, col=0):
    return pl.BlockSpec((tm, D), lambda i: (i, col))


def _prev_halo(tm, col=0):
    return pl.BlockSpec((HALO, D), lambda i: (jnp.maximum(i * (tm // HALO) - 1, 0), col))


def _next_halo(tm, n_rows, col=0):
    last = n_rows // HALO - 1
    return pl.BlockSpec((HALO, D), lambda i: (jnp.minimum((i + 1) * (tm // HALO), last), col))


def _const(shape):
    return pl.BlockSpec(shape, lambda *_: (0,) * len(shape))


def _w_out_spec(which):
    return pl.BlockSpec((N_DEV, None, D // N_DEV, D), lambda *_: (0, which, 0, 0))


def _to_time_major(t_ref, row0, x):
    n = x.shape[0]
    for j in range(LANE_GROUPS):
        t_ref[pl.ds(row0 * LANE_GROUPS + j, n, stride=LANE_GROUPS), :] = x[:, j * LANES:(j + 1) * LANES]


def _from_time_major(t_ref, n, row0=0):
    return jnp.concatenate([t_ref[pl.ds(row0 * LANE_GROUPS + j, n, stride=LANE_GROUPS), :]
                            for j in range(LANE_GROUPS)], axis=-1)


def _row(ref, t):
    start = t * LANE_GROUPS
    if not isinstance(start, int):
        start = pl.multiple_of(start, LANE_GROUPS)
    return ref[pl.ds(start, LANE_GROUPS), :]


def _tap_groups(taps):
    return [taps[first:first + TAP_GROUP] for first in range(0, len(taps), TAP_GROUP)]


def _conv(o_ref, e_ref, w_ref, taps, n_rows, bias_ref=None):
    for n_group, group in enumerate(_tap_groups(taps)):
        weights = [_row(w_ref, k) for k, _ in group]

        def block(c, carry, n_group=n_group, group=group, weights=weights):
            t0 = c * CONV_BLOCK
            window = {}
            for t in range(CONV_BLOCK):
                parts = [None] * min(CONV_PARTIAL_SUMS, len(group))
                for n, (_, off) in enumerate(group):
                    if t + off not in window:
                        window[t + off] = _row(e_ref, t0 + t + off)
                    term = weights[n] * window[t + off]
                    parts[n % len(parts)] = term if parts[n % len(parts)] is None else parts[n % len(parts)] + term
                window.pop(t + min(off for _, off in group), None)
                while len(parts) > 1:
                    parts = [parts[n] + parts[n + 1] for n in range(0, len(parts) - 1, 2)] + parts[len(parts) & ~1:]
                out = parts[0]
                if n_group > 0:
                    out = out + _row(o_ref, t0 + t)
                elif bias_ref is not None:
                    out = out + bias_ref[...]
                o_ref[pl.ds(pl.multiple_of((t0 + t) * LANE_GROUPS, LANE_GROUPS), LANE_GROUPS), :] = out
            return carry

        lax.fori_loop(0, n_rows // CONV_BLOCK, block, 0)


def _conv_wgrad(dw_ref, d_ref, e_ref, taps, n_rows):
    for group in _tap_groups(taps):
        def block(c, sums, group=group):
            t0 = c * CONV_BLOCK
            sums = list(sums)
            window = {}
            for t in range(CONV_BLOCK):
                d = _row(d_ref, t0 + t)
                for n, (_, off) in enumerate(group):
                    if t + off not in window:
                        window[t + off] = _row(e_ref, t0 + t + off)
                    sums[n] = sums[n] + d * window[t + off]
                window.pop(t + min(off for _, off in group), None)
            return tuple(sums)

        sums = lax.fori_loop(0, n_rows // CONV_BLOCK, block, tuple(_row(dw_ref, k) for k, _ in group))
        for (k, _), total in zip(group, sums):
            dw_ref[pl.ds(k * LANE_GROUPS, LANE_GROUPS), :] = total


FWD_TAPS_A = [(k, HALO - (K_A - 1) + k) for k in range(K_A)]
BWD_TAPS_A = [(k, K_A - 1 - k) for k in range(K_A)]
FWD_TAPS_B = [(k, HALO - (K_B - 1) + k) for k in range(K_B)]
BWD_TAPS_B = [(k, K_B - 1 - k) for k in range(K_B)]


def _time_major(n_rows):
    return pltpu.VMEM((n_rows * LANE_GROUPS, LANES), F32)


def _kv_fwd(mem, mem_g, wkv_g):
    m_len = mem.shape[0]

    def body(mem_ref, g_ref, w_ref, kv_ref, mn_ref):
        mf = mem_ref[...]
        r = lax.rsqrt(_mean(mf * mf) + EPS)
        mn = ((mf * r) * g_ref[...]).astype(BF16)
        mn_ref[...] = mn
        for b in range(2 * N_HEADS):
            kv_ref[b] = _dot(mn, w_ref[b]).astype(BF16)

    return pl.pallas_call(
        body, name="kv_fwd", grid=(1,),
        in_specs=[_const((m_len, D)), _const((1, D)), _const((2 * N_HEADS, D, HEAD_DIM))],
        out_specs=[_const((2 * N_HEADS, m_len, HEAD_DIM)), _const((m_len, D))],
        out_shape=[jax.ShapeDtypeStruct((2 * N_HEADS, m_len, HEAD_DIM), BF16), jax.ShapeDtypeStruct((m_len, D), BF16)],
        compiler_params=_params(1),
    )(mem, mem_g, wkv_g)


def _kv_bwd(dkv, mem, mem_g, mn16, wkv_g):
    m_len = mem.shape[0]

    def body(dkv_ref, mem_ref, g_ref, mn_ref, w_ref, dw_ref, dw16_ref, dg_ref):
        mn = mn_ref[...]
        dmn = jnp.zeros((m_len, D), F32)
        for b in range(2 * N_HEADS):
            d16 = dkv_ref[b].astype(BF16)
            dw = _dot_tn(mn, d16)
            dw_ref[b] = dw
            dw16_ref[b] = dw.astype(BF16)
            dmn = dmn + _dot_nt(d16, w_ref[b])
        mf = mem_ref[...]
        r = lax.rsqrt(_mean(mf * mf) + EPS)
        dg_ref[...] = _fold8(dmn * (mf * r))

    return pl.pallas_call(
        body, name="kv_bwd", grid=(1,),
        in_specs=[_const((2 * N_HEADS, m_len, HEAD_DIM)), _const((m_len, D)), _const((1, D)), _const((m_len, D)),
                  _const((2 * N_HEADS, D, HEAD_DIM))],
        out_specs=[_const((2 * N_HEADS, D, HEAD_DIM)), _const((2 * N_HEADS, D, HEAD_DIM)), _const((SUBLANES, D))],
        out_shape=[jax.ShapeDtypeStruct((2 * N_HEADS, D, HEAD_DIM), F32),
                   jax.ShapeDtypeStruct((2 * N_HEADS, D, HEAD_DIM), BF16), jax.ShapeDtypeStruct((SUBLANES, D), F32)],
        compiler_params=_params(1),
    )(dkv, mem, mem_g, mn16, wkv_g)


def _rmsnorm_fwd(x, norm_g, tm):
    n_rows = x.shape[0]

    def body(x_ref, g_ref, u_ref, ut_ref):
        xf = x_ref[...]
        u = (xf * lax.rsqrt(_mean(xf * xf) + EPS)) * g_ref[...]
        u_ref[...] = u.astype(BF16)
        ut_ref[...] = u.T.astype(BF16)

    return pl.pallas_call(
        body, name="rmsnorm_fwd", grid=(n_rows // tm,),
        in_specs=[_rows(tm), _const((1, D))],
        out_specs=[_rows(tm), pl.BlockSpec((D, tm), lambda i: (0, i))],
        out_shape=[jax.ShapeDtypeStruct((n_rows, D), BF16), jax.ShapeDtypeStruct((D, n_rows), BF16)],
        compiler_params=_params(1),
    )(x, norm_g)


def _place():
    x, y, c = lax.axis_index("x"), lax.axis_index("y"), lax.axis_index("c")
    other_chips = [(1 - x, y), (x, 1 - y), (1 - x, 1 - y)]
    return x, y, c, other_chips


def _near_far(x, y, c):
    near_kind = 1 - c

    def chip(kind):
        return x + (1 - kind) * (1 - 2 * x), y + kind * (1 - 2 * y)

    return (chip(near_kind), near_kind), (chip(c), c)


ARRIVALS = (("near", True), ("far", False), ("far", True), ("near", False), ("diag", True), ("diag", False))


def _arrival_order():
    x, y, c, chips = _place()
    (near, _), (far, _) = _near_far(x, y, c)
    chip = {"near": near, "far": far, "diag": chips[2]}
    order = [4 * x + 2 * y + c, 4 * x + 2 * y + 1 - c]
    for which, same_core in ARRIVALS:
        px, py = chip[which]
        order.append(4 * px + 2 * py + (c if same_core else 1 - c))
    return order


def _proj_fwd_gather(u16, blocks, tm):
    n = len(blocks)
    n_rows = u16.shape[0]
    n_i = n_rows // tm
    per_shard = W_IN_SHARD // UNIT
    assert n_i >= per_shard

    def wt_index(p, i, order):
        return (_dp_unit(per_shard * order[p] + jnp.minimum(i, per_shard - 1)), 0)

    def body(order_ref, u_ref, *refs):
        src, proj_ref, wt_ref, out = refs[:n], refs[n], refs[n + 1], refs[n + 2:2 * n + 2]
        wbuf, stage_sem, send, recv, own_sem = refs[2 * n + 2:]
        p, i = pl.program_id(0), pl.program_id(1)
        x, y, c, chips = _place()
        me, sibling = 4 * x + 2 * y + c, (x, y, 1 - c)

        def copy(t, k, block, to, from_input=False):
            return pltpu.make_async_remote_copy(
                src_ref=src[t] if from_input else out[t].at[block], dst_ref=out[t].at[block],
                send_sem=send.at[t, k], recv_sem=recv.at[t, k], device_id=to, device_id_type=MESH)

        def own_copies():
            return [pltpu.make_async_copy(src[t], out[t].at[me], own_sem.at[t]) for t in range(n)]

        def first_copy(t, j):
            return copy(t, 1 + j, me, (*chips[j], c), from_input=True)

        def first_copies(skip=()):
            first = []
            for t in range(n):
                first.append(copy(t, 0, me, sibling, from_input=True))
                first += [first_copy(t, j) for j in range(len(chips)) if (t, j) not in skip]
            return first

        def stage(slot, block):
            return pltpu.make_async_copy(out[0].at[block], wbuf.at[slot], stage_sem.at[slot])

        (near, near_kind), (far, far_kind) = _near_far(x, y, c)

        @pl.when((p == 0) & (i == 0))
        def _():
            for cp in own_copies() + [copy(t, 0, me, sibling, from_input=True) for t in range(n)]:
                cp.start()
            for kind in (0, 1):
                @pl.when(near_kind == kind)
                def _(kind=kind):
                    first_copy(0, kind).start()
            mine = pltpu.make_async_copy(src[0], wbuf.at[0], stage_sem.at[0])
            mine.start()
            mine.wait()

        @pl.when((p == 1) & (i == 0))
        def _():
            for kind in (0, 1):
                @pl.when(far_kind == kind)
                def _(kind=kind):
                    first_copy(0, kind).start()
            first_copy(0, 2).start()
            for t in range(1, n):
                for j in range(len(chips)):
                    first_copy(t, j).start()

        @pl.when((p > 0) & (i == 0))
        def _():
            stage(p % 2, order_ref[p]).wait()

        proj_ref[...] = _dot(u_ref[...], wbuf[p % 2])
        for r in range(per_shard):
            @pl.when(i == r)
            def _(r=r):
                wt_ref[...] = wbuf[p % 2, :, r * UNIT:(r + 1) * UNIT].astype(F32).T.astype(BF16)

        for nxt in range(1, N_DEV):
            @pl.when((p == nxt - 1) & (i == n_i - 1))
            def _(nxt=nxt):
                if nxt == 1:
                    block = 4 * x + 2 * y + 1 - c
                    copy(0, 0, block, sibling).wait_recv()
                else:
                    which, same_core = ARRIVALS[nxt - 2]
                    (px, py), kind = {"near": (near, near_kind), "far": (far, far_kind), "diag": (chips[2], 2)}[which]
                    if same_core:
                        block = 4 * px + 2 * py + c
                        copy(0, 1 + kind, block, sibling).wait_recv()
                        copy(0, 4 + kind, block, sibling).start()
                    else:
                        block = 4 * px + 2 * py + 1 - c
                        copy(0, 4 + kind, block, sibling).wait_recv()
                stage(nxt % 2, block).start()

        @pl.when((p == N_DEV - 1) & (i == n_i - 1))
        def _():
            passed = [copy(0, 4 + j, 4 * px + 2 * py + c, sibling) for j, (px, py) in enumerate(chips)]
            for j, (px, py) in enumerate(chips):
                for t in range(1, n):
                    block = 4 * px + 2 * py + c
                    copy(t, 1 + j, block, sibling).wait_recv()
                    passed.append(copy(t, 4 + j, block, sibling))
                    passed[-1].start()
            for t in range(1, n):
                copy(t, 0, 4 * x + 2 * y + 1 - c, sibling).wait_recv()
                for j, (px, py) in enumerate(chips):
                    copy(t, 4 + j, 4 * px + 2 * py + 1 - c, sibling).wait_recv()
            for cp in first_copies() + passed:
                cp.wait_send()
            for cp in own_copies():
                cp.wait()

    return pl.pallas_call(
        body, name="proj_fwd_gather",
        grid_spec=pltpu.PrefetchScalarGridSpec(
            num_scalar_prefetch=1, grid=(N_DEV, n_i),
            in_specs=[pl.BlockSpec((tm, D), lambda p, i, order: (i, 0))] + [ANY] * n,
            out_specs=[pl.BlockSpec((tm, W_IN_SHARD), lambda p, i, order: (i, order[p])),
                       pl.BlockSpec((UNIT, D), wt_index)] + [ANY] * n,
            scratch_shapes=[pltpu.VMEM((2, D, W_IN_SHARD), BF16), pltpu.SemaphoreType.DMA((2,)),
                            pltpu.SemaphoreType.DMA((n, 7)), pltpu.SemaphoreType.DMA((n, 7)),
                            pltpu.SemaphoreType.DMA((n,))]),
        out_shape=[jax.ShapeDtypeStruct((n_rows, N_GROUPS * D), F32), jax.ShapeDtypeStruct((N_GROUPS * D, D), BF16)]
        + [jax.ShapeDtypeStruct((N_DEV, *b.shape), b.dtype) for b in blocks],
        compiler_params=_params(2),
    )(jnp.stack(_arrival_order()).astype(jnp.int32), u16, *blocks)


def _branch_a_fwd(proj, wo4_g, cw_a, tm):
    n_rows = proj.shape[0]

    def body(bp, cp, xp, za, cph, xph, w_ref, cw_ref, sa_ref, ya_ref, e_scr, o_scr):
        i = pl.program_id(0)
        _to_time_major(e_scr, 0, jnp.where(i > 0, _f32(cph) * _f32(xph), 0.0))
        for r0 in range(0, tm, ELEM_ROWS):
            rows = pl.ds(r0, ELEM_ROWS)
            _to_time_major(e_scr, HALO + r0, cp[rows, :].astype(F32) * xp[rows, :].astype(F32))
        _conv(o_scr, e_scr, cw_ref, FWD_TAPS_A, tm)
        for r0 in range(0, tm, ELEM_ROWS):
            rows = pl.ds(r0, ELEM_ROWS)
            ca = _from_time_major(o_scr, ELEM_ROWS, r0)
            sa_ref[rows, :] = (jax.nn.silu(za[rows, :].astype(F32)) * (bp[rows, :].astype(F32) * ca)).astype(BF16)
        ya_ref[...] = _dot(sa_ref[...], w_ref[...].reshape(D, D))

    return pl.pallas_call(
        body, name="branch_a_fwd", grid=(n_rows // tm,),
        in_specs=[_rows(tm, G_BA), _rows(tm, G_CA), _rows(tm, G_XA), _rows(tm, G_ZA),
                  _prev_halo(tm, G_CA), _prev_halo(tm, G_XA), _w_out_spec(0), _const(cw_a.shape)],
        out_specs=[_rows(tm), _rows(tm)],
        out_shape=[jax.ShapeDtypeStruct((n_rows, D), BF16), jax.ShapeDtypeStruct((n_rows, D), F32)],
        scratch_shapes=[_time_major(tm + HALO), _time_major(tm)],
        compiler_params=_params(1),
    )(proj, proj, proj, proj, proj, proj, wo4_g, cw_a)


def _layernorm_parts(cb, lg, lb):
    xc = cb - _mean(cb)
    rstd = lax.rsqrt(_mean(xc * xc) + EPS)
    xhat = xc * rstd
    return xhat, rstd, xhat * lg + lb


def _branch_b_fwd(proj, wo4_g, cw_b, conv_b_b, ln_g, ln_b, tm):
    n_rows = proj.shape[0]

    def body(vb, gb, zb, vbh, gbh, w_ref, cw_ref, bb_ref, lg_ref, lb_ref, cb_ref, sb_ref, yb_ref, e_scr, o_scr):
        i = pl.program_id(0)
        vbh, gbh = _f32(vbh), _f32(gbh)
        lg, lb = lg_ref[...], lb_ref[...]
        _to_time_major(e_scr, 0, jnp.where(i > 0, vbh * jax.nn.sigmoid(gbh), 0.0))
        for r0 in range(0, tm, ELEM_ROWS):
            rows = pl.ds(r0, ELEM_ROWS)
            _to_time_major(e_scr, HALO + r0, vb[rows, :].astype(F32) * jax.nn.sigmoid(gb[rows, :].astype(F32)))
        _conv(o_scr, e_scr, cw_ref, FWD_TAPS_B, tm, bias_ref=bb_ref)
        for r0 in range(0, tm, ELEM_ROWS):
            rows = pl.ds(r0, ELEM_ROWS)
            cb = _from_time_major(o_scr, ELEM_ROWS, r0)
            cb_ref[rows, :] = cb
            _, _, ln = _layernorm_parts(cb, lg, lb)
            sb_ref[rows, :] = (jax.nn.silu(zb[rows, :].astype(F32)) * jax.nn.silu(ln)).astype(BF16)
        yb_ref[...] = _dot(sb_ref[...], w_ref[...].reshape(D, D))

    return pl.pallas_call(
        body, name="branch_b_fwd", grid=(n_rows // tm,),
        in_specs=[_rows(tm, G_VB), _rows(tm, G_GB), _rows(tm, G_ZB), _prev_halo(tm, G_VB), _prev_halo(tm, G_GB),
                  _w_out_spec(1), _const(cw_b.shape), _const((LANE_GROUPS, LANES)), _const((1, D)), _const((1, D))],
        out_specs=[_rows(tm), _rows(tm), _rows(tm)],
        out_shape=[jax.ShapeDtypeStruct((n_rows, D), F32), jax.ShapeDtypeStruct((n_rows, D), BF16),
                   jax.ShapeDtypeStruct((n_rows, D), F32)],
        scratch_shapes=[_time_major(tm + HALO), _time_major(tm)],
        compiler_params=_params(1),
    )(proj, proj, proj, proj, proj, wo4_g, cw_b, conv_b_b.reshape(LANE_GROUPS, LANES), ln_g, ln_b)


def _attention(q16, kv_ref):
    probs, outs = [], []
    for h in range(N_HEADS):
        s = _dot_nt(q16[:, h * HEAD_DIM:(h + 1) * HEAD_DIM], kv_ref[h]) * (HEAD_DIM ** -0.5)
        e = jnp.exp(s - jnp.max(s, axis=-1, keepdims=True))
        p = e / jnp.sum(e, axis=-1, keepdims=True)
        probs.append(p)
        outs.append(_dot(p.astype(BF16), kv_ref[N_HEADS + h]))
    return probs, outs


def _branch_x_fwd(proj, kv16, wo4_g, tm):
    n_rows = proj.shape[0]

    def body(q, zx, kv_ref, w_ref, sx_ref, yx_ref, p_ref):
        probs, outs = _attention(q[...].astype(BF16), kv_ref)
        p_ref[...] = jnp.concatenate(probs, axis=-1)
        sx = (jax.nn.silu(_f32(zx)) * jnp.concatenate(outs, axis=-1)).astype(BF16)
        sx_ref[...] = sx
        yx_ref[...] = _dot(sx, w_ref[...].reshape(D, D))

    return pl.pallas_call(
        body, name="branch_x_fwd", grid=(n_rows // tm,),
        in_specs=[_rows(tm, G_Q), _rows(tm, G_ZX), _const(kv16.shape), _w_out_spec(2)],
        out_specs=[_rows(tm), _rows(tm), _rows(tm)],
        out_shape=[jax.ShapeDtypeStruct((n_rows, D), BF16), jax.ShapeDtypeStruct((n_rows, D), F32),
                   jax.ShapeDtypeStruct((n_rows, D), F32)],
        compiler_params=_params(1),
    )(proj, proj, kv16, wo4_g)


def _merge_fwd_bwd(proj, ya, yb, yx, x, target, wo4_g, final_g, tm):
    n_rows = proj.shape[0]
    inv_d = 1.0 / D

    def body(ga, gb, gx, ya_ref, yb_ref, yx_ref, x_ref, t_ref, w_ref, fg_ref,
             dh_ref, dya_ref, dyb_ref, dyx_ref, dp_ref, dw_ref, dfg_ref, sq_ref):
        i = pl.program_id(0)
        wo = w_ref[...].reshape(D, D)
        sig = [jax.nn.sigmoid(_f32(g)) for g in (ga, gb, gx)]
        ys = [ya_ref[...], yb_ref[...], yx_ref[...]]
        m16 = (sig[0] * ys[0] + sig[1] * ys[1] + sig[2] * ys[2]).astype(BF16)
        h = x_ref[...] + _dot(m16, wo)
        r = lax.rsqrt(_mean(h * h) + EPS)
        hn = h * r
        fg = fg_ref[...]
        err = hn * fg - t_ref[...]
        dy = err * inv_d
        dhn = dy * fg
        dh = r * (dhn - hn * _mean(dhn * hn))
        dh_ref[...] = dh
        dh16 = dh.astype(BF16)
        dm = _dot_nt(dh16, wo)
        for n, out in enumerate((dya_ref, dyb_ref, dyx_ref)):
            out[...] = (sig[n] * dm).astype(BF16)
            dp_ref[:, n * D:(n + 1) * D] = (dm * ys[n] * (sig[n] * (1.0 - sig[n]))).astype(BF16)

        @pl.when(i == 0)
        def _():
            dw_ref[...] = jnp.zeros_like(dw_ref)
            dfg_ref[...] = jnp.zeros_like(dfg_ref)
            sq_ref[...] = jnp.zeros_like(sq_ref)

        dw_ref[0] += _dot_tn(m16, dh16)
        dfg_ref[...] += _fold8(dy * hn)
        sq_ref[...] += _fold8(err * err)

    vec = jax.ShapeDtypeStruct((SUBLANES, D), F32)
    return pl.pallas_call(
        body, name="merge_fwd_bwd", grid=(n_rows // tm,),
        in_specs=[_rows(tm, G_GA), _rows(tm, G_GBB), _rows(tm, G_GX), _rows(tm), _rows(tm), _rows(tm), _rows(tm),
                  _rows(tm), _w_out_spec(3), _const((1, D))],
        out_specs=[_rows(tm), _rows(tm), _rows(tm), _rows(tm), pl.BlockSpec((tm, 3 * D), lambda i: (i, 3)),
                   pl.BlockSpec((1, D, D), lambda i: (3, 0, 0)), _const((SUBLANES, D)), _const((SUBLANES, D))],
        out_shape=[jax.ShapeDtypeStruct((n_rows, D), F32), jax.ShapeDtypeStruct((n_rows, D), BF16),
                   jax.ShapeDtypeStruct((n_rows, D), BF16), jax.ShapeDtypeStruct((n_rows, D), BF16),
                   jax.ShapeDtypeStruct((n_rows, N_GROUPS * D), BF16), jax.ShapeDtypeStruct((4, D, D), F32), vec, vec],
        compiler_params=_params(1),
    )(proj, proj, proj, ya, yb, yx, x, target, wo4_g, final_g)


def _branch_a_bwd(dya, proj, sa16, wo4_g, cw_a, dproj, dw4, tm):
    n_rows = proj.shape[0]
    n_tiles = n_rows // tm

    def body(dya_ref, bp, cp, xp, za, sa_ref, dyan, bpn, zan, cph, xph, w_ref, cw_ref, dp_in, dw_in,
             dp_ref, dw_ref, dwa_ref, e1, e2, o_scr, mm_scr):
        del dp_in, dw_in
        i = pl.program_id(0)
        bpn, zan, cph, xph = (_f32(r) for r in (bpn, zan, cph, xph))
        woa = w_ref[...].reshape(D, D)
        dya16 = dya_ref[...]
        chunks = [pl.ds(r0, ELEM_ROWS) for r0 in range(0, tm, ELEM_ROWS)]
        _to_time_major(e1, 0, jnp.where(i > 0, cph * xph, 0.0))
        for r0, rows in zip(range(0, tm, ELEM_ROWS), chunks):
            _to_time_major(e1, HALO + r0, cp[rows, :].astype(F32) * xp[rows, :].astype(F32))
        _conv(o_scr, e1, cw_ref, FWD_TAPS_A, tm)
        mm_scr[...] = _dot_nt(dya16, woa)
        for r0, rows in zip(range(0, tm, ELEM_ROWS), chunks):
            ca = _from_time_major(o_scr, ELEM_ROWS, r0)
            dsa = mm_scr[rows, :]
            b = bp[rows, :].astype(F32)
            silu_z, dsilu_z = _silu_and_grad(za[rows, :].astype(F32))
            t = dsa * silu_z
            dp_ref[rows, 0 * D:1 * D] = (t * ca).astype(BF16)
            dp_ref[rows, 3 * D:4 * D] = (dsa * (b * ca) * dsilu_z).astype(BF16)
            _to_time_major(e2, r0, t * b)
        dcan = (_dot_nt(dyan[...], woa) * jax.nn.silu(zan)) * bpn
        _to_time_major(e2, tm, jnp.where(i < n_tiles - 1, dcan, 0.0))

        @pl.when(i == 0)
        def _():
            dw_ref[...] = jnp.zeros_like(dw_ref)
            dwa_ref[...] = jnp.zeros_like(dwa_ref)

        _conv_wgrad(dwa_ref, e2, e1, FWD_TAPS_A, tm)
        dw_ref[0] += _dot_tn(sa_ref[...], dya16)
        _conv(o_scr, e2, cw_ref, BWD_TAPS_A, tm)
        for r0, rows in zip(range(0, tm, ELEM_ROWS), chunks):
            dprod = _from_time_major(o_scr, ELEM_ROWS, r0)
            dp_ref[rows, 1 * D:2 * D] = (dprod * xp[rows, :].astype(F32)).astype(BF16)
            dp_ref[rows, 2 * D:3 * D] = (dprod * cp[rows, :].astype(F32)).astype(BF16)

    return pl.pallas_call(
        body, name="branch_a_bwd", grid=(n_tiles,),
        in_specs=[_rows(tm), _rows(tm, G_BA), _rows(tm, G_CA), _rows(tm, G_XA), _rows(tm, G_ZA), _rows(tm),
                  _next_halo(tm, n_rows), _next_halo(tm, n_rows, G_BA), _next_halo(tm, n_rows, G_ZA),
                  _prev_halo(tm, G_CA), _prev_halo(tm, G_XA), _w_out_spec(0), _const(cw_a.shape), ANY, ANY],
        out_specs=[pl.BlockSpec((tm, 4 * D), lambda i: (i, 0)), pl.BlockSpec((1, D, D), lambda i: (0, 0, 0)),
                   _const((K_A * LANE_GROUPS, LANES))],
        out_shape=[jax.ShapeDtypeStruct(dproj.shape, BF16), jax.ShapeDtypeStruct(dw4.shape, F32),
                   jax.ShapeDtypeStruct((K_A * LANE_GROUPS, LANES), F32)],
        input_output_aliases={13: 0, 14: 1},
        scratch_shapes=[_time_major(tm + HALO), _time_major(tm + HALO), _time_major(tm), pltpu.VMEM((tm, D), F32)],
        compiler_params=_params(1),
    )(dya, proj, proj, proj, proj, sa16, dya, proj, proj, proj, proj, wo4_g, cw_a, dproj, dw4)


def _branch_b_bwd(dyb, proj, cb, sb16, wo4_g, cw_b, ln_g, ln_b, dproj, dw4, tm):
    n_rows = proj.shape[0]
    n_tiles = n_rows // tm

    def body(dyb_ref, zb, cb_ref, vb, gb, sb_ref, dybn, zbn, cbn, vbh, gbh, w_ref, cw_ref, lg_ref, lb_ref,
             dp_in, dw_in, dp_ref, dw_ref, dwb_ref, dbb_ref, dlg_ref, dlb_ref, e1, e2, o_scr, mm_scr):
        del dp_in, dw_in
        zbn, vbh, gbh = (_f32(r) for r in (zbn, vbh, gbh))
        i = pl.program_id(0)
        wob = w_ref[...].reshape(D, D)
        lg, lb = lg_ref[...], lb_ref[...]

        def conv_out_grad(dsb, z, c):
            xhat, rstd, ln = _layernorm_parts(c, lg, lb)
            sw, dsw = _silu_and_grad(ln)
            sz, dsz = _silu_and_grad(z)
            dln = (dsb * sz) * dsw
            dxhat = dln * lg
            dcb = rstd * (dxhat - _mean(dxhat) - xhat * _mean(dxhat * xhat))
            return dsb * sw * dsz, dln, xhat, dcb

        @pl.when(i == 0)
        def _():
            dw_ref[...] = jnp.zeros_like(dw_ref)
            dwb_ref[...] = jnp.zeros_like(dwb_ref)
            dbb_ref[...] = jnp.zeros_like(dbb_ref)
            dlg_ref[...] = jnp.zeros_like(dlg_ref)
            dlb_ref[...] = jnp.zeros_like(dlb_ref)

        dyb16 = dyb_ref[...]
        mm_scr[...] = _dot_nt(dyb16, wob)
        dlg, dlb, dbb = (jnp.zeros((SUBLANES, D), F32),) * 3
        for r0 in range(0, tm, ELEM_ROWS):
            rows = pl.ds(r0, ELEM_ROWS)
            dzb, dln, xhat, dcb = conv_out_grad(mm_scr[rows, :], zb[rows, :].astype(F32), cb_ref[rows, :])
            dp_ref[rows, 2 * D:3 * D] = dzb.astype(BF16)
            _to_time_major(e2, r0, dcb)
            dlg, dlb, dbb = dlg + _fold8(dln * xhat), dlb + _fold8(dln), dbb + _fold8(dcb)
        _, _, _, dcbn = conv_out_grad(_dot_nt(dybn[...], wob), zbn[...], cbn[...])
        _to_time_major(e2, tm, jnp.where(i < n_tiles - 1, dcbn, 0.0))
        dlg_ref[...] += dlg
        dlb_ref[...] += dlb
        dbb_ref[...] += dbb
        dw_ref[0] += _dot_tn(sb_ref[...], dyb16)
        _to_time_major(e1, 0, jnp.where(i > 0, vbh[...] * jax.nn.sigmoid(gbh[...]), 0.0))
        for r0 in range(0, tm, ELEM_ROWS):
            rows = pl.ds(r0, ELEM_ROWS)
            sg = jax.nn.sigmoid(gb[rows, :].astype(F32))
            mm_scr[rows, :] = sg
            _to_time_major(e1, HALO + r0, vb[rows, :].astype(F32) * sg)
        _conv_wgrad(dwb_ref, e2, e1, FWD_TAPS_B, tm)
        _conv(o_scr, e2, cw_ref, BWD_TAPS_B, tm)
        for r0 in range(0, tm, ELEM_ROWS):
            rows = pl.ds(r0, ELEM_ROWS)
            dglu = _from_time_major(o_scr, ELEM_ROWS, r0)
            sg = mm_scr[rows, :]
            dp_ref[rows, 0 * D:1 * D] = (dglu * sg).astype(BF16)
            dp_ref[rows, 1 * D:2 * D] = (dglu * vb[rows, :].astype(F32) * (sg * (1.0 - sg))).astype(BF16)

    vec = jax.ShapeDtypeStruct((SUBLANES, D), F32)
    return pl.pallas_call(
        body, name="branch_b_bwd", grid=(n_tiles,),
        in_specs=[_rows(tm), _rows(tm, G_ZB), _rows(tm), _rows(tm, G_VB), _rows(tm, G_GB), _rows(tm),
                  _next_halo(tm, n_rows), _next_halo(tm, n_rows, G_ZB), _next_halo(tm, n_rows),
                  _prev_halo(tm, G_VB), _prev_halo(tm, G_GB), _w_out_spec(1), _const(cw_b.shape), _const((1, D)),
                  _const((1, D)), ANY, ANY],
        out_specs=[pl.BlockSpec((tm, 3 * D), lambda i: (i, 2)), pl.BlockSpec((1, D, D), lambda i: (1, 0, 0)),
                   _const((K_B * LANE_GROUPS, LANES)), _const((SUBLANES, D)), _const((SUBLANES, D)),
                   _const((SUBLANES, D))],
        out_shape=[jax.ShapeDtypeStruct(dproj.shape, BF16), jax.ShapeDtypeStruct(dw4.shape, F32),
                   jax.ShapeDtypeStruct((K_B * LANE_GROUPS, LANES), F32), vec, vec, vec],
        input_output_aliases={15: 0, 16: 1},
        scratch_shapes=[_time_major(tm + HALO), _time_major(tm + HALO), _time_major(tm), pltpu.VMEM((tm, D), F32)],
        compiler_params=_params(1),
    )(dyb, proj, cb, proj, proj, sb16, dyb, proj, cb, proj, proj, wo4_g, cw_b, ln_g, ln_b, dproj, dw4)


def _branch_x_bwd(dyx, proj, sx16, probs, kv16, wo4_g, dproj, dw4, tm):
    n_rows = proj.shape[0]
    scale = HEAD_DIM ** -0.5

    def body(dyx_ref, q, zx, sx_ref, p_ref, kv_ref, w_ref, dp_in, dw_in, dp_ref, dw_ref, dkv_ref):
        del dp_in, dw_in
        i = pl.program_id(0)
        dyx16 = dyx_ref[...]
        q16 = q[...].astype(BF16)
        probs = [p_ref[:, h * HEAD_DIM:(h + 1) * HEAD_DIM] for h in range(N_HEADS)]
        outs = [_dot(probs[h].astype(BF16), kv_ref[N_HEADS + h]) for h in range(N_HEADS)]
        dsx = _dot_nt(dyx16, w_ref[...].reshape(D, D))
        silu_z, dsilu_z = _silu_and_grad(_f32(zx))
        dp_ref[:, D:2 * D] = (dsx * jnp.concatenate(outs, axis=-1) * dsilu_z).astype(BF16)
        do16 = (dsx * silu_z).astype(BF16)

        @pl.when(i == 0)
        def _():
            dw_ref[...] = jnp.zeros_like(dw_ref)
            dkv_ref[...] = jnp.zeros_like(dkv_ref)

        for h in range(N_HEADS):
            cols = slice(h * HEAD_DIM, (h + 1) * HEAD_DIM)
            p = probs[h]
            dprob = _dot_nt(do16[:, cols], kv_ref[N_HEADS + h])
            ds16 = ((p * (dprob - jnp.sum(p * dprob, axis=-1, keepdims=True))) * scale).astype(BF16)
            dp_ref[:, cols] = _dot(ds16, kv_ref[h]).astype(BF16)
            dkv_ref[h] += _dot_tn(ds16, q16[:, cols])
            dkv_ref[N_HEADS + h] += _dot_tn(p.astype(BF16), do16[:, cols])
        dw_ref[0] += _dot_tn(sx_ref[...], dyx16)

    return pl.pallas_call(
        body, name="branch_x_bwd", grid=(n_rows // tm,),
        in_specs=[_rows(tm), _rows(tm, G_Q), _rows(tm, G_ZX), _rows(tm), _rows(tm), _const(kv16.shape),
                  _w_out_spec(2), ANY, ANY],
        out_specs=[pl.BlockSpec((tm, 2 * D), lambda i: (i, 2)), pl.BlockSpec((1, D, D), lambda i: (2, 0, 0)),
                   _const(kv16.shape)],
        out_shape=[jax.ShapeDtypeStruct(dproj.shape, BF16), jax.ShapeDtypeStruct(dw4.shape, F32),
                   jax.ShapeDtypeStruct(kv16.shape, F32)],
        input_output_aliases={7: 0, 8: 1},
        compiler_params=_params(1),
    )(dyx, proj, proj, sx16, probs, kv16, wo4_g, dproj, dw4)


def _dp_unit(u):
    g = u // 2
    pos = jnp.where(g < G_VB, g, jnp.where(g < G_Q, g + 2, jnp.where(g < G_GA, g - 3, g)))
    return 2 * pos + u % 2


def _scatter_copies(srcs, lands, send, recv):
    x, y, c = lax.axis_index("x"), lax.axis_index("y"), lax.axis_index("c")
    copies = []
    for n in range(N_DEV - 1):
        flip = n + 1
        px = 1 - x if flip & 4 else x
        py = 1 - y if flip & 2 else y
        pc = 1 - c if flip & 1 else c
        for t, (src, land) in enumerate(zip(srcs, lands)):
            copies.append(pltpu.make_async_remote_copy(
                src_ref=src.at[4 * px + 2 * py + pc], dst_ref=land.at[n], send_sem=send.at[t * (N_DEV - 1) + n],
                recv_sem=recv.at[t * (N_DEV - 1) + n], device_id=(px, py, pc), device_id_type=MESH))
    return copies


def _scatter_start(name, arrays, views, n_views):
    n = len(arrays)
    lands = [lax.empty(tuple(N_DEV - 1 if d == N_DEV else d for d in a.shape), a.dtype) for a in arrays]

    def body(*refs):
        src, land, (send, recv) = refs[:n], refs[n:2 * n], refs[2 * n:2 * n + 2]
        token = refs[-1]
        for cp in _scatter_copies(views(src), views(land), send, recv):
            cp.start()
        token[...] = jnp.zeros_like(token)

    sems = pltpu.SemaphoreType.DMA((n_views * (N_DEV - 1),))
    out = pl.pallas_call(
        body, name=name,
        in_specs=[HBM] * (2 * n),
        out_specs=[SEM, SEM] + [HBM] * (2 * n) + [pl.BlockSpec(memory_space=pltpu.VMEM)],
        out_shape=[sems, sems] + [pltpu.HBM(a.shape, a.dtype) for a in arrays + lands]
        + [jax.ShapeDtypeStruct((SUBLANES, LANES), F32)],
        input_output_aliases={k: 2 + k for k in range(2 * n)},
        compiler_params=pltpu.CompilerParams(has_side_effects=SIDE_EFFECT),
    )(*[pltpu.with_memory_space_constraint(a, pltpu.HBM) for a in arrays + lands])
    return dict(name=name, sems=out[:2], moving=out[2:2 + 2 * n], views=views, token=out[-1])


def _scatter_wait(started, after):
    n = len(started["moving"]) // 2
    views = started["views"]

    def body(*refs):
        src, land, (send, recv) = refs[:n], refs[n:2 * n], refs[2 * n:2 * n + 2]
        for cp in _scatter_copies(views(src), views(land), send, recv):
            cp.wait_send()
            cp.wait_recv()

    out = pl.pallas_call(
        body, name=started["name"].replace("start", "wait"),
        in_specs=[HBM] * (2 * n) + [SEM, SEM, ANY],
        out_specs=[HBM] * (2 * n),
        out_shape=[pltpu.HBM(a.shape, a.dtype) for a in started["moving"]],
        input_output_aliases={k: k for k in range(2 * n)},
        compiler_params=pltpu.CompilerParams(has_side_effects=SIDE_EFFECT),
    )(*started["moving"], *started["sems"], after)
    return out[n:]


def _w_in_grad(ut, dproj, tk, token):
    n_rows = dproj.shape[0]
    n_k = n_rows // tk
    per_shard = W_IN_SHARD // UNIT

    def body(ut_ref, dp0, dp1, dp2, token_ref, out_ref, out16_ref, acc):
        del token_ref
        t = pl.program_id(1)

        for r, dp_ref in enumerate((dp0, dp1, dp2)):
            cols = slice(r * UNIT, (r + 1) * UNIT)

            @pl.when(t == 0)
            def _(dp_ref=dp_ref, cols=cols):
                acc[:, cols] = _dot(ut_ref[...], dp_ref[...])

            @pl.when(t > 0)
            def _(dp_ref=dp_ref, cols=cols):
                acc[:, cols] += _dot(ut_ref[...], dp_ref[...])

        @pl.when(t == n_k - 1)
        def _():
            out_ref[0] = acc[...]
            out16_ref[0] = acc[...].astype(BF16)

    def dp_spec(r):
        return pl.BlockSpec((tk, UNIT), lambda q, t: (t, _dp_unit(per_shard * q + r)))

    shard = pl.BlockSpec((1, D, W_IN_SHARD), lambda q, t: (q, 0, 0))
    return pl.pallas_call(
        body, name="w_in_grad", grid=(N_DEV, n_k),
        in_specs=[pl.BlockSpec((D, tk), lambda q, t: (0, t)), dp_spec(0), dp_spec(1), dp_spec(2), ANY],
        out_specs=[shard, shard],
        out_shape=[jax.ShapeDtypeStruct((N_DEV, D, W_IN_SHARD), F32), jax.ShapeDtypeStruct((N_DEV, D, W_IN_SHARD), BF16)],
        scratch_shapes=[pltpu.VMEM((D, W_IN_SHARD), F32)],
        compiler_params=_params(2),
    )(ut, dproj, dproj, dproj, token)


def _x_grad(dproj, win_t, x, dh, norm_g, token, tm):
    n_rows = x.shape[0]
    n_k = N_GROUPS * D // X_GRAD_K

    def body(dp_ref, wt_ref, x_ref, dh_ref, g_ref, token_ref, gx_ref, dg_ref, acc):
        del token_ref
        i, g = pl.program_id(0), pl.program_id(1)

        @pl.when((i == 0) & (g == 0))
        def _():
            dg_ref[...] = jnp.zeros_like(dg_ref)

        @pl.when(g == 0)
        def _():
            acc[...] = _dot(dp_ref[...], wt_ref[...])

        @pl.when(g > 0)
        def _():
            acc[...] += _dot(dp_ref[...], wt_ref[...])

        @pl.when(g == n_k - 1)
        def _():
            du = acc[...]
            xf = x_ref[...]
            r = lax.rsqrt(_mean(xf * xf) + EPS)
            xn = xf * r
            dun = du * g_ref[...]
            gx_ref[...] = dh_ref[...] + r * (dun - xn * _mean(dun * xn))
            dg_ref[...] += _fold8(du * xn)

    return pl.pallas_call(
        body, name="x_grad", grid=(n_rows // tm, n_k),
        in_specs=[pl.BlockSpec((tm, X_GRAD_K), lambda i, g: (i, g)), pl.BlockSpec((X_GRAD_K, D), lambda i, g: (g, 0)),
                  pl.BlockSpec((tm, D), lambda i, g: (i, 0)), pl.BlockSpec((tm, D), lambda i, g: (i, 0)),
                  _const((1, D)), ANY],
        out_specs=[pl.BlockSpec((tm, D), lambda i, g: (i, 0)), _const((SUBLANES, D))],
        out_shape=[jax.ShapeDtypeStruct((n_rows, D), F32), jax.ShapeDtypeStruct((SUBLANES, D), F32)],
        scratch_shapes=[pltpu.VMEM((tm, D), F32)],
        compiler_params=_params(2),
    )(dproj, win_t, x, dh, norm_g, token)


def _local_step(x, mem, target, norm_g, conv_b_b, ln_g, ln_b, mem_g, final_g, shards):
    n_rows = x.shape[0]
    tm = min(512, n_rows)
    big = min(1024, n_rows)
    u16, ut = _rmsnorm_fwd(x, norm_g, big)
    proj, win_t, _, wkv_g, wo4_g, cw_g = _proj_fwd_gather(u16, shards, min(2048, n_rows))
    cw_rows = cw_g.transpose(1, 0, 2).reshape((SUBLANES + HALO) * LANE_GROUPS, LANES)
    cw_a, cw_b = cw_rows[:SUBLANES * LANE_GROUPS], cw_rows[SUBLANES * LANE_GROUPS:]
    kv16, mn16 = _kv_fwd(mem, mem_g, wkv_g)
    sa16, ya = _branch_a_fwd(proj, wo4_g, cw_a, tm)
    cb, sb16, yb = _branch_b_fwd(proj, wo4_g, cw_b, conv_b_b, ln_g, ln_b, tm)
    sx16, yx, probs = _branch_x_fwd(proj, kv16, wo4_g, big)
    dh, dya, dyb, dyx, dproj, dw4, dfg, sq = _merge_fwd_bwd(proj, ya, yb, yx, x, target, wo4_g, final_g,
                                                             min(256, n_rows))
    dproj, dw4, dwa = _branch_a_bwd(dya, proj, sa16, wo4_g, cw_a, dproj, dw4, tm)
    dproj, dw4, dwb, dbb, dlg, dlb = _branch_b_bwd(dyb, proj, cb, sb16, wo4_g, cw_b, ln_g, ln_b, dproj, dw4, tm)
    dproj, dw4, dkv = _branch_x_bwd(dyx, proj, sx16, probs, kv16, wo4_g, dproj, dw4, tm)
    dwkv_g, dwkv16, dmg = _kv_bwd(dkv, mem, mem_g, mn16, wkv_g)
    dw4 = dw4.reshape(4, N_DEV, D // N_DEV, D)
    small_moving = _scatter_start("small_grads_start", [dw4.astype(BF16), dwkv16],
                                  lambda refs: [refs[0].at[w] for w in range(4)] + [refs[1]], 5)
    dwin_g, dwin16 = _w_in_grad(ut, dproj, min(2048, n_rows), small_moving["token"])
    w_in_moving = _scatter_start("w_in_grad_start", [dwin16], lambda refs: list(refs), 1)
    gx, dng = _x_grad(dproj, win_t, x, dh, norm_g, w_in_moving["token"], big)
    land4, landkv = _scatter_wait(small_moving, dng)
    landin, = _scatter_wait(w_in_moving, dng)
    small = {SV_NORM_G: dng, SV_CONV_B_B: dbb, SV_LN_G: dlg, SV_LN_B: dlb, SV_MEM_G: dmg, SV_FINAL_G: dfg, SV_LOSS: sq}
    grads = [(dwin_g[None], landin[None]), (dw4, land4), (dwkv_g[None], landkv[None])]
    return gx, grads, small, dwa.reshape(K_A, D), dwb.reshape(K_B, D)


def _allgather_small(small, conv_rows):
    keys = sorted(small)

    def body(*refs):
        parts, (conv_ref, out_ref, mine, send, recv) = refs[:len(keys)], refs[len(keys):]
        x, y, c, chips = _place()
        me, sibling = 4 * x + 2 * y + c, (x, y, 1 - c)
        mine[pl.ds(0, SV_CONV_A), :] = jnp.zeros((SV_CONV_A, D), F32)
        for key, part in zip(keys, parts):
            mine[key:key + 1, :] = jnp.sum(part[...], axis=0, keepdims=True)
        mine[pl.ds(SV_CONV_A, SV_ROWS - SV_CONV_A), :] = conv_ref[...]
        out_ref[me] = mine[...]

        def copy(k, block, to, from_mine=False):
            return pltpu.make_async_remote_copy(
                src_ref=mine if from_mine else out_ref.at[block], dst_ref=out_ref.at[block],
                send_sem=send.at[k], recv_sem=recv.at[k], device_id=to, device_id_type=MESH)

        first = [copy(0, me, sibling, from_mine=True)]
        first += [copy(1 + j, me, (*chip, c), from_mine=True) for j, chip in enumerate(chips)]
        for cp in first:
            cp.start()
        passed = []
        for j, (px, py) in enumerate(chips):
            block = 4 * px + 2 * py + c
            copy(1 + j, block, sibling).wait_recv()
            passed.append(copy(4 + j, block, sibling))
            passed[-1].start()
        copy(0, 4 * x + 2 * y + 1 - c, sibling).wait_recv()
        for j, (px, py) in enumerate(chips):
            copy(4 + j, 4 * px + 2 * py + 1 - c, sibling).wait_recv()
        for cp in first + passed:
            cp.wait_send()

    vmem = pl.BlockSpec(memory_space=pltpu.VMEM)
    return pl.pallas_call(
        body, name="allgather_small",
        in_specs=[vmem] * (len(keys) + 1), out_specs=vmem,
        out_shape=jax.ShapeDtypeStruct((N_DEV, SV_ROWS, D), F32),
        scratch_shapes=[pltpu.VMEM((SV_ROWS, D), F32), pltpu.SemaphoreType.DMA((7,)), pltpu.SemaphoreType.DMA((7,))],
    )(*[small[k] for k in keys], conv_rows)


def _adamw(w, g, m, v):
    m = ADAM_B1 * m + (1.0 - ADAM_B1) * g
    v = ADAM_B2 * v + (1.0 - ADAM_B2) * (g * g)
    m_hat = m / (1.0 - ADAM_B1 ** ADAM_STEP)
    v_hat = v / (1.0 - ADAM_B2 ** ADAM_STEP)
    return -ADAM_LR * (m_hat / (jnp.sqrt(v_hat) + ADAM_EPS) + ADAM_WD * w), m, v


def _adamw_shard(own, landed, piece, k_arr, w, m, v, tr):
    n_r, n_c = w.shape
    n_landed = landed.shape[1]

    def body(k_ref, own_ref, *refs):
        del k_ref
        landed_refs, (w_ref, m_ref, v_ref, g_out, d_out, m_out, v_out) = refs[:n_landed], refs[n_landed:]
        g = own_ref[0, 0]
        for landed_ref in landed_refs:
            g = g + landed_ref[0, 0].astype(F32)
        g_out[...] = g
        d_out[...], m_out[...], v_out[...] = _adamw(w_ref[...], g, m_ref[...], v_ref[...])

    blk = (1, 1, tr, n_c)
    flat = pl.BlockSpec((tr, n_c), lambda r, k: (r, 0))
    return pl.pallas_call(
        body, name="adamw_shard",
        grid_spec=pltpu.PrefetchScalarGridSpec(
            num_scalar_prefetch=1, grid=(n_r // tr,),
            in_specs=[pl.BlockSpec(blk, lambda r, k: (piece, k[0], r, 0))]
            + [pl.BlockSpec(blk, functools.partial(lambda r, k, j: (piece, j, r, 0), j=j)) for j in range(n_landed)]
            + [flat] * 3,
            out_specs=[flat] * 4),
        out_shape=[jax.ShapeDtypeStruct((n_r, n_c), F32)] * 4,
        compiler_params=_params(1),
    )(k_arr, own, *([landed] * n_landed), w, m, v)


def _adamw_shards(entries, k_arr):
    n = len(entries)

    def body(k_ref, *refs):
        del k_ref
        ins, outs = refs[:5 * n], refs[5 * n:]
        for e in range(n):
            own_ref, landed_ref, w_ref, m_ref, v_ref = ins[5 * e:5 * e + 5]
            g = own_ref[0, 0]
            for j in range(landed_ref.shape[1]):
                g = g + landed_ref[0, j].astype(F32)
            g_out, d_out, m_out, v_out = outs[4 * e:4 * e + 4]
            g_out[...] = g
            d_out[...], m_out[...], v_out[...] = _adamw(w_ref[...], g, m_ref[...], v_ref[...])

    in_specs, operands, out_specs, out_shape = [], [], [], []
    for own, landed, piece, w, m, v in entries:
        flat = pl.BlockSpec(w.shape, lambda i, k: (0, 0))
        in_specs += [pl.BlockSpec((1, 1, *w.shape), functools.partial(lambda i, k, p: (p, k[0], 0, 0), p=piece)),
                     pl.BlockSpec((1, *landed.shape[1:]), functools.partial(lambda i, k, p: (p, 0, 0, 0), p=piece)),
                     flat, flat, flat]
        operands += [own, landed, w, m, v]
        out_specs += [flat] * 4
        out_shape += [jax.ShapeDtypeStruct(w.shape, F32)] * 4
    out = pl.pallas_call(
        body, name="adamw_shards",
        grid_spec=pltpu.PrefetchScalarGridSpec(num_scalar_prefetch=1, grid=(1,), in_specs=in_specs,
                                               out_specs=out_specs),
        out_shape=out_shape,
        compiler_params=_params(1),
    )(k_arr, *operands)
    return [tuple(out[4 * e:4 * e + 4]) for e in range(n)]


def _adamw_small(gathered, k_arr, vectors, conv_a, conv_b):
    n_vec = len(vectors)
    cols = D // N_DEV

    def body(k_ref, full_ref, cols_ref, *refs):
        del k_ref
        ins, outs = refs[:3 * (n_vec + 2)], refs[3 * (n_vec + 2):-2]
        tot, tot_cols = refs[-2:]
        tot[...] = full_ref[0]
        tot_cols[...] = cols_ref[0]
        for dev in range(1, N_DEV):
            tot[...] += full_ref[dev]
            tot_cols[...] += cols_ref[dev]
        loss = (0.5 / D) * jnp.sum(tot[SV_LOSS:SV_LOSS + 1, :])
        outs[0][...] = jnp.full(outs[0].shape, loss, F32)
        grads = [tot[n:n + 1, :] for n in range(n_vec)]
        grads += [tot_cols[pl.ds(SV_CONV_A, K_A), :], tot_cols[pl.ds(SV_CONV_B, K_B), :]]
        for n, g in enumerate(grads):
            w_ref, m_ref, v_ref = ins[3 * n:3 * n + 3]
            g_out, d_out, m_out, v_out = outs[1 + 4 * n:5 + 4 * n]
            g_out[...] = g
            d_out[...], m_out[...], v_out[...] = _adamw(w_ref[...], g, m_ref[...], v_ref[...])

    weights = list(vectors) + [conv_a, conv_b]
    flat_in = [a for wmv in weights for a in wmv]
    out_shape = [jax.ShapeDtypeStruct((SUBLANES, 128), F32)]
    for wmv in weights:
        out_shape += [jax.ShapeDtypeStruct(wmv[0].shape, F32)] * 4
    return pl.pallas_call(
        body, name="adamw_small",
        grid_spec=pltpu.PrefetchScalarGridSpec(
            num_scalar_prefetch=1, grid=(1,),
            in_specs=[pl.BlockSpec((N_DEV, SV_ROWS, D), lambda i, k: (0, 0, 0)),
                      pl.BlockSpec((N_DEV, SV_ROWS, cols), lambda i, k: (0, 0, k[0]))]
            + [pl.BlockSpec(a.shape, lambda i, k: (0, 0)) for a in flat_in],
            out_specs=[pl.BlockSpec(s.shape, lambda i, k: (0, 0)) for s in out_shape],
            scratch_shapes=[pltpu.VMEM((SV_ROWS, D), F32), pltpu.VMEM((SV_ROWS, cols), F32)]),
        out_shape=out_shape,
        compiler_params=_params(1),
    )(k_arr, gathered, gathered, *flat_in)


def kernel(x, mem, norm_g, w_in, conv_a_w, w_out_a, conv_b_w, conv_b_b, ln_b_g, ln_b_b, w_out_b, mem_norm_g, w_kv, w_out_x, w_o, final_g, loss_target, m_norm_g, m_w_in, m_conv_a_w, m_w_out_a, m_conv_b_w, m_conv_b_b, m_ln_b_g, m_ln_b_b, m_w_out_b, m_mem_norm_g, m_w_kv, m_w_out_x, m_w_o, m_final_g, v_norm_g, v_w_in, v_conv_a_w, v_w_out_a, v_conv_b_w, v_conv_b_b, v_ln_b_g, v_ln_b_b, v_w_out_b, v_mem_norm_g, v_w_kv, v_w_out_x, v_w_o, v_final_g):
    xi, yi, ci = lax.axis_index("x"), lax.axis_index("y"), lax.axis_index("c")
    k_arr = jnp.reshape(4 * xi + 2 * yi + ci, (1,)).astype(jnp.int32)

    cw = jnp.concatenate([jnp.pad(conv_a_w[0], ((0, SUBLANES - K_A), (0, 0))),
                          jnp.pad(conv_b_w[0], ((0, HALO - K_B), (0, 0)))], axis=0)
    wo4 = jnp.stack([w_out_a[0], w_out_b[0], w_out_x[0], w_o[0]]).astype(BF16)
    shards = [w_in[0].astype(BF16), w_kv[0].astype(BF16), wo4, cw]

    final_g2 = final_g.reshape(1, D)
    gx, grads, small, dwa, dwb = _local_step(
        x[0], mem[0], loss_target[0], norm_g, conv_b_b, ln_b_g, ln_b_b, mem_norm_g, final_g2, shards)

    conv_rows = jnp.concatenate([jnp.pad(dwa, ((0, SUBLANES - K_A), (0, 0))),
                                 jnp.pad(dwb, ((0, HALO - K_B), (0, 0)))], axis=0)
    gathered_small = _allgather_small(small, conv_rows)

    res = {"w_in": _adamw_shard(grads[0][0], grads[0][1], 0, k_arr, w_in[0], m_w_in[0], v_w_in[0], 256)}
    small_shards = [("w_out_a", 1, 0, w_out_a, m_w_out_a, v_w_out_a), ("w_out_b", 1, 1, w_out_b, m_w_out_b, v_w_out_b),
                    ("w_out_x", 1, 2, w_out_x, m_w_out_x, v_w_out_x), ("w_o", 1, 3, w_o, m_w_o, v_w_o),
                    ("w_kv", 2, 0, w_kv, m_w_kv, v_w_kv)]
    updated = _adamw_shards([(grads[a][0], grads[a][1], l, w[0], m[0], v[0]) for _, a, l, w, m, v in small_shards],
                            k_arr)
    res.update({name: four for (name, *_), four in zip(small_shards, updated)})
    res = {name: tuple(r[None] for r in four) for name, four in res.items()}
    vectors = [(norm_g, m_norm_g, v_norm_g), (conv_b_b, m_conv_b_b, v_conv_b_b), (ln_b_g, m_ln_b_g, v_ln_b_g),
               (ln_b_b, m_ln_b_b, v_ln_b_b), (mem_norm_g, m_mem_norm_g, v_mem_norm_g),
               (final_g2, m_final_g.reshape(1, D), v_final_g.reshape(1, D))]
    out = _adamw_small(gathered_small, k_arr, vectors, (conv_a_w[0], m_conv_a_w[0], v_conv_a_w[0]),
                       (conv_b_w[0], m_conv_b_w[0], v_conv_b_w[0]))
    loss = out[0][0, 0]
    names = ["norm_g", "conv_b_b", "ln_b_g", "ln_b_b", "mem_norm_g", "final_g", "conv_a_w", "conv_b_w"]
    for n, name in enumerate(names):
        four = out[1 + 4 * n:5 + 4 * n]
        if name == "final_g":
            four = [r.reshape(D) for r in four]
        elif name.startswith("conv_") and name.endswith("_w"):
            four = [r[None] for r in four]
        res[name] = tuple(four)

    order = ["norm_g", "w_in", "conv_a_w", "w_out_a", "conv_b_w", "conv_b_b", "ln_b_g", "ln_b_b", "w_out_b",
             "mem_norm_g", "w_kv", "w_out_x", "w_o", "final_g"]
    return (loss, gx[None], *[res[n][0] for n in order], *[res[n][1] for n in order],
            *[res[n][2] for n in order], *[res[n][3] for n in order])
```

```python
import functools

import jax
import jax.numpy as jnp
from jax import lax
from jax.experimental import pallas as pl
from jax.experimental.pallas import tpu as pltpu

F32, BF16 = jnp.float32, jnp.bfloat16
D = 1024
N_DEV = 8
N_HEADS = 4
HEAD_DIM = D // N_HEADS
N_GROUPS = 12
W_IN_SHARD = N_GROUPS * D // N_DEV
UNIT = 512
X_GRAD_K = 3 * D
K_A, K_B = 3, 31
EPS = 1e-6
HALO = 32
SUBLANES = 8
LANES = 128
LANE_GROUPS = D // LANES
TAP_GROUP = 16
CONV_BLOCK = 32
ELEM_ROWS = 16
CONV_PARTIAL_SUMS = 4
VMEM_LIMIT = 60 << 20
MESH = pl.DeviceIdType.MESH
ANY = pl.BlockSpec(memory_space=pl.ANY)
HBM = pl.BlockSpec(memory_space=pltpu.HBM)
SEM = pl.BlockSpec(memory_space=pltpu.SEMAPHORE)
SIDE_EFFECT = pltpu.SideEffectType.DATAFLOW_SIDE_EFFECTING

G_BA, G_CA, G_XA, G_ZA, G_VB, G_GB, G_ZB, G_Q, G_ZX, G_GA, G_GBB, G_GX = range(N_GROUPS)

ADAM_LR, ADAM_B1, ADAM_B2, ADAM_EPS, ADAM_WD, ADAM_STEP = 0.001, 0.9, 0.999, 1e-08, 0.01, 10

SV_NORM_G, SV_CONV_B_B, SV_LN_G, SV_LN_B, SV_MEM_G, SV_FINAL_G, SV_LOSS = range(7)
SV_CONV_A, SV_CONV_B, SV_ROWS = 8, 16, 48


def _dot(a, b):
    return jnp.dot(a, b, preferred_element_type=F32)


def _dot_nt(a, b):
    return lax.dot_general(a, b, (((1,), (1,)), ((), ())), preferred_element_type=F32)


def _dot_tn(a, b):
    return lax.dot_general(a, b, (((0,), (0,)), ((), ())), preferred_element_type=F32)


def _silu_and_grad(z):
    s = jax.nn.sigmoid(z)
    return z * s, s * (1.0 + z * (1.0 - s))


def _fold8(a):
    return a.reshape(a.shape[0] // SUBLANES, SUBLANES, a.shape[1]).sum(axis=0)


def _mean(a):
    return jnp.mean(a, axis=-1, keepdims=True)


def _f32(ref):
    return ref[...].astype(F32)


def _params(n_grid):
    return pltpu.CompilerParams(dimension_semantics=("arbitrary",) * n_grid, vmem_limit_bytes=VMEM_LIMIT)


def _rows(tm, col=0):
    return pl.BlockSpec((tm, D), lambda i: (i, col))


def _prev_halo(tm, col=0):
    return pl.BlockSpec((HALO, D), lambda i: (jnp.maximum(i * (tm // HALO) - 1, 0), col))


def _next_halo(tm, n_rows, col=0):
    last = n_rows // HALO - 1
    return pl.BlockSpec((HALO, D), lambda i: (jnp.minimum((i + 1) * (tm // HALO), last), col))


def _const(shape):
    return pl.BlockSpec(shape, lambda *_: (0,) * len(shape))


def _w_out_spec(which):
    return pl.BlockSpec((N_DEV, None, D // N_DEV, D), lambda *_: (0, which, 0, 0))


def _to_time_major(t_ref, row0, x):
    n = x.shape[0]
    for j in range(LANE_GROUPS):
        t_ref[pl.ds(row0 * LANE_GROUPS + j, n, stride=LANE_GROUPS), :] = x[:, j * LANES:(j + 1) * LANES]


def _from_time_major(t_ref, n, row0=0):
    return jnp.concatenate([t_ref[pl.ds(row0 * LANE_GROUPS + j, n, stride=LANE_GROUPS), :]
                            for j in range(LANE_GROUPS)], axis=-1)


def _row(ref, t):
    start = t * LANE_GROUPS
    if not isinstance(start, int):
        start = pl.multiple_of(start, LANE_GROUPS)
    return ref[pl.ds(start, LANE_GROUPS), :]


def _tap_groups(taps):
    return [taps[first:first + TAP_GROUP] for first in range(0, len(taps), TAP_GROUP)]


def _conv(o_ref, e_ref, w_ref, taps, n_rows, bias_ref=None):
    for n_group, group in enumerate(_tap_groups(taps)):
        weights = [_row(w_ref, k) for k, _ in group]

        def block(c, carry, n_group=n_group, group=group, weights=weights):
            t0 = c * CONV_BLOCK
            window = {}
            for t in range(CONV_BLOCK):
                parts = [None] * min(CONV_PARTIAL_SUMS, len(group))
                for n, (_, off) in enumerate(group):
                    if t + off not in window:
                        window[t + off] = _row(e_ref, t0 + t + off)
                    term = weights[n] * window[t + off]
                    parts[n % len(parts)] = term if parts[n % len(parts)] is None else parts[n % len(parts)] + term
                window.pop(t + min(off for _, off in group), None)
                while len(parts) > 1:
                    parts = [parts[n] + parts[n + 1] for n in range(0, len(parts) - 1, 2)] + parts[len(parts) & ~1:]
                out = parts[0]
                if n_group > 0:
                    out = out + _row(o_ref, t0 + t)
                elif bias_ref is not None:
                    out = out + bias_ref[...]
                o_ref[pl.ds(pl.multiple_of((t0 + t) * LANE_GROUPS, LANE_GROUPS), LANE_GROUPS), :] = out
            return carry

        lax.fori_loop(0, n_rows // CONV_BLOCK, block, 0)


def _conv_wgrad(dw_ref, d_ref, e_ref, taps, n_rows):
    for group in _tap_groups(taps):
        def block(c, sums, group=group):
            t0 = c * CONV_BLOCK
            sums = list(sums)
            window = {}
            for t in range(CONV_BLOCK):
                d = _row(d_ref, t0 + t)
                for n, (_, off) in enumerate(group):
                    if t + off not in window:
                        window[t + off] = _row(e_ref, t0 + t + off)
                    sums[n] = sums[n] + d * window[t + off]
                window.pop(t + min(off for _, off in group), None)
            return tuple(sums)

        sums = lax.fori_loop(0, n_rows // CONV_BLOCK, block, tuple(_row(dw_ref, k) for k, _ in group))
        for (k, _), total in zip(group, sums):
            dw_ref[pl.ds(k * LANE_GROUPS, LANE_GROUPS), :] = total


FWD_TAPS_A = [(k, HALO - (K_A - 1) + k) for k in range(K_A)]
BWD_TAPS_A = [(k, K_A - 1 - k) for k in range(K_A)]
FWD_TAPS_B = [(k, HALO - (K_B - 1) + k) for k in range(K_B)]
BWD_TAPS_B = [(k, K_B - 1 - k) for k in range(K_B)]


def _time_major(n_rows):
    return pltpu.VMEM((n_rows * LANE_GROUPS, LANES), F32)


def _kv_fwd(mem, mem_g, wkv_g):
    m_len = mem.shape[0]

    def body(mem_ref, g_ref, w_ref, kv_ref, mn_ref):
        mf = mem_ref[...]
        r = lax.rsqrt(_mean(mf * mf) + EPS)
        mn = ((mf * r) * g_ref[...]).astype(BF16)
        mn_ref[...] = mn
        for b in range(2 * N_HEADS):
            kv_ref[b] = _dot(mn, w_ref[b]).astype(BF16)

    return pl.pallas_call(
        body, name="kv_fwd", grid=(1,),
        in_specs=[_const((m_len, D)), _const((1, D)), _const((2 * N_HEADS, D, HEAD_DIM))],
        out_specs=[_const((2 * N_HEADS, m_len, HEAD_DIM)), _const((m_len, D))],
        out_shape=[jax.ShapeDtypeStruct((2 * N_HEADS, m_len, HEAD_DIM), BF16), jax.ShapeDtypeStruct((m_len, D), BF16)],
        compiler_params=_params(1),
    )(mem, mem_g, wkv_g)


def _kv_bwd(dkv, mem, mem_g, mn16, wkv_g):
    m_len = mem.shape[0]

    def body(dkv_ref, mem_ref, g_ref, mn_ref, w_ref, dw_ref, dw16_ref, dg_ref):
        mn = mn_ref[...]
        dmn = jnp.zeros((m_len, D), F32)
        for b in range(2 * N_HEADS):
            d16 = dkv_ref[b].astype(BF16)
            dw = _dot_tn(mn, d16)
            dw_ref[b] = dw
            dw16_ref[b] = dw.astype(BF16)
            dmn = dmn + _dot_nt(d16, w_ref[b])
        mf = mem_ref[...]
        r = lax.rsqrt(_mean(mf * mf) + EPS)
        dg_ref[...] = _fold8(dmn * (mf * r))

    return pl.pallas_call(
        body, name="kv_bwd", grid=(1,),
        in_specs=[_const((2 * N_HEADS, m_len, HEAD_DIM)), _const((m_len, D)), _const((1, D)), _const((m_len, D)),
                  _const((2 * N_HEADS, D, HEAD_DIM))],
        out_specs=[_const((2 * N_HEADS, D, HEAD_DIM)), _const((2 * N_HEADS, D, HEAD_DIM)), _const((SUBLANES, D))],
        out_shape=[jax.ShapeDtypeStruct((2 * N_HEADS, D, HEAD_DIM), F32),
                   jax.ShapeDtypeStruct((2 * N_HEADS, D, HEAD_DIM), BF16), jax.ShapeDtypeStruct((SUBLANES, D), F32)],
        compiler_params=_params(1),
    )(dkv, mem, mem_g, mn16, wkv_g)


def _rmsnorm_fwd(x, norm_g, tm):
    n_rows = x.shape[0]

    def body(x_ref, g_ref, u_ref, ut_ref):
        xf = x_ref[...]
        u = (xf * lax.rsqrt(_mean(xf * xf) + EPS)) * g_ref[...]
        u_ref[...] = u.astype(BF16)
        ut_ref[...] = u.T.astype(BF16)

    return pl.pallas_call(
        body, name="rmsnorm_fwd", grid=(n_rows // tm,),
        in_specs=[_rows(tm), _const((1, D))],
        out_specs=[_rows(tm), pl.BlockSpec((D, tm), lambda i: (0, i))],
        out_shape=[jax.ShapeDtypeStruct((n_rows, D), BF16), jax.ShapeDtypeStruct((D, n_rows), BF16)],
        compiler_params=_params(1),
    )(x, norm_g)


def _place():
    x, y, c = lax.axis_index("x"), lax.axis_index("y"), lax.axis_index("c")
    other_chips = [(1 - x, y), (x, 1 - y), (1 - x, 1 - y)]
    return x, y, c, other_chips


def _arrival_order():
    x, y, c, chips = _place()
    order = [4 * x + 2 * y + c, 4 * x + 2 * y + 1 - c]
    for px, py in chips:
        order += [4 * px + 2 * py + c, 4 * px + 2 * py + 1 - c]
    return order


def _proj_fwd_gather(u16, blocks, tm):
    n = len(blocks)
    n_rows = u16.shape[0]
    n_i = n_rows // tm
    per_shard = W_IN_SHARD // UNIT
    assert n_i >= per_shard

    def wt_index(p, i, order):
        return (_dp_unit(per_shard * order[p] + jnp.minimum(i, per_shard - 1)), 0)

    def body(order_ref, u_ref, *refs):
        src, proj_ref, wt_ref, out = refs[:n], refs[n], refs[n + 1], refs[n + 2:2 * n + 2]
        wbuf, stage_sem, send, recv, own_sem = refs[2 * n + 2:]
        p, i = pl.program_id(0), pl.program_id(1)
        x, y, c, chips = _place()
        me, sibling = 4 * x + 2 * y + c, (x, y, 1 - c)

        def copy(t, k, block, to, from_input=False):
            return pltpu.make_async_remote_copy(
                src_ref=src[t] if from_input else out[t].at[block], dst_ref=out[t].at[block],
                send_sem=send.at[t, k], recv_sem=recv.at[t, k], device_id=to, device_id_type=MESH)

        def own_copies():
            return [pltpu.make_async_copy(src[t], out[t].at[me], own_sem.at[t]) for t in range(n)]

        def first_copies():
            first = []
            for t in range(n):
                first.append(copy(t, 0, me, sibling, from_input=True))
                first += [copy(t, 1 + j, me, (*chip, c), from_input=True) for j, chip in enumerate(chips)]
            return first

        def stage(slot, block):
            return pltpu.make_async_copy(out[0].at[block], wbuf.at[slot], stage_sem.at[slot])

        @pl.when((p == 0) & (i == 0))
        def _():
            for cp in own_copies() + first_copies():
                cp.start()
            mine = pltpu.make_async_copy(src[0], wbuf.at[0], stage_sem.at[0])
            mine.start()
            mine.wait()

        @pl.when((p > 0) & (i == 0))
        def _():
            stage(p % 2, order_ref[p]).wait()

        proj_ref[...] = _dot(u_ref[...], wbuf[p % 2])
        for r in range(per_shard):
            @pl.when(i == r)
            def _(r=r):
                wt_ref[...] = wbuf[p % 2, :, r * UNIT:(r + 1) * UNIT].astype(F32).T.astype(BF16)

        for nxt in range(1, N_DEV):
            @pl.when((p == nxt - 1) & (i == n_i - 1))
            def _(nxt=nxt):
                if nxt == 1:
                    block = 4 * x + 2 * y + 1 - c
                    copy(0, 0, block, sibling).wait_recv()
                else:
                    j, passed_on = divmod(nxt - 2, 2)
                    px, py = chips[j]
                    if passed_on:
                        block = 4 * px + 2 * py + 1 - c
                        copy(0, 4 + j, block, sibling).wait_recv()
                    else:
                        block = 4 * px + 2 * py + c
                        copy(0, 1 + j, block, sibling).wait_recv()
                        copy(0, 4 + j, block, sibling).start()
                stage(nxt % 2, block).start()

        @pl.when((p == N_DEV - 1) & (i == n_i - 1))
        def _():
            passed = [copy(0, 4 + j, 4 * px + 2 * py + c, sibling) for j, (px, py) in enumerate(chips)]
            for j, (px, py) in enumerate(chips):
                for t in range(1, n):
                    block = 4 * px + 2 * py + c
                    copy(t, 1 + j, block, sibling).wait_recv()
                    passed.append(copy(t, 4 + j, block, sibling))
                    passed[-1].start()
            for t in range(1, n):
                copy(t, 0, 4 * x + 2 * y + 1 - c, sibling).wait_recv()
                for j, (px, py) in enumerate(chips):
                    copy(t, 4 + j, 4 * px + 2 * py + 1 - c, sibling).wait_recv()
            for cp in first_copies() + passed:
                cp.wait_send()
            for cp in own_copies():
                cp.wait()

    return pl.pallas_call(
        body, name="proj_fwd_gather",
        grid_spec=pltpu.PrefetchScalarGridSpec(
            num_scalar_prefetch=1, grid=(N_DEV, n_i),
            in_specs=[pl.BlockSpec((tm, D), lambda p, i, order: (i, 0))] + [ANY] * n,
            out_specs=[pl.BlockSpec((tm, W_IN_SHARD), lambda p, i, order: (i, order[p])),
                       pl.BlockSpec((UNIT, D), wt_index)] + [ANY] * n,
            scratch_shapes=[pltpu.VMEM((2, D, W_IN_SHARD), BF16), pltpu.SemaphoreType.DMA((2,)),
                            pltpu.SemaphoreType.DMA((n, 7)), pltpu.SemaphoreType.DMA((n, 7)),
                            pltpu.SemaphoreType.DMA((n,))]),
        out_shape=[jax.ShapeDtypeStruct((n_rows, N_GROUPS * D), F32), jax.ShapeDtypeStruct((N_GROUPS * D, D), BF16)]
        + [jax.ShapeDtypeStruct((N_DEV, *b.shape), b.dtype) for b in blocks],
        compiler_params=_params(2),
    )(jnp.stack(_arrival_order()).astype(jnp.int32), u16, *blocks)


def _branch_a_fwd(proj, wo4_g, cw_a, tm):
    n_rows = proj.shape[0]

    def body(bp, cp, xp, za, cph, xph, w_ref, cw_ref, sa_ref, ya_ref, e_scr, o_scr):
        i = pl.program_id(0)
        _to_time_major(e_scr, 0, jnp.where(i > 0, _f32(cph) * _f32(xph), 0.0))
        for r0 in range(0, tm, ELEM_ROWS):
            rows = pl.ds(r0, ELEM_ROWS)
            _to_time_major(e_scr, HALO + r0, cp[rows, :].astype(F32) * xp[rows, :].astype(F32))
        _conv(o_scr, e_scr, cw_ref, FWD_TAPS_A, tm)
        for r0 in range(0, tm, ELEM_ROWS):
            rows = pl.ds(r0, ELEM_ROWS)
            ca = _from_time_major(o_scr, ELEM_ROWS, r0)
            sa_ref[rows, :] = (jax.nn.silu(za[rows, :].astype(F32)) * (bp[rows, :].astype(F32) * ca)).astype(BF16)
        ya_ref[...] = _dot(sa_ref[...], w_ref[...].reshape(D, D))

    return pl.pallas_call(
        body, name="branch_a_fwd", grid=(n_rows // tm,),
        in_specs=[_rows(tm, G_BA), _rows(tm, G_CA), _rows(tm, G_XA), _rows(tm, G_ZA),
                  _prev_halo(tm, G_CA), _prev_halo(tm, G_XA), _w_out_spec(0), _const(cw_a.shape)],
        out_specs=[_rows(tm), _rows(tm)],
        out_shape=[jax.ShapeDtypeStruct((n_rows, D), BF16), jax.ShapeDtypeStruct((n_rows, D), F32)],
        scratch_shapes=[_time_major(tm + HALO), _time_major(tm)],
        compiler_params=_params(1),
    )(proj, proj, proj, proj, proj, proj, wo4_g, cw_a)


def _layernorm_parts(cb, lg, lb):
    xc = cb - _mean(cb)
    rstd = lax.rsqrt(_mean(xc * xc) + EPS)
    xhat = xc * rstd
    return xhat, rstd, xhat * lg + lb


def _branch_b_fwd(proj, wo4_g, cw_b, conv_b_b, ln_g, ln_b, tm):
    n_rows = proj.shape[0]

    def body(vb, gb, zb, vbh, gbh, w_ref, cw_ref, bb_ref, lg_ref, lb_ref, cb_ref, sb_ref, yb_ref, e_scr, o_scr):
        i = pl.program_id(0)
        vbh, gbh = _f32(vbh), _f32(gbh)
        lg, lb = lg_ref[...], lb_ref[...]
        _to_time_major(e_scr, 0, jnp.where(i > 0, vbh * jax.nn.sigmoid(gbh), 0.0))
        for r0 in range(0, tm, ELEM_ROWS):
            rows = pl.ds(r0, ELEM_ROWS)
            _to_time_major(e_scr, HALO + r0, vb[rows, :].astype(F32) * jax.nn.sigmoid(gb[rows, :].astype(F32)))
        _conv(o_scr, e_scr, cw_ref, FWD_TAPS_B, tm, bias_ref=bb_ref)
        for r0 in range(0, tm, ELEM_ROWS):
            rows = pl.ds(r0, ELEM_ROWS)
            cb = _from_time_major(o_scr, ELEM_ROWS, r0)
            cb_ref[rows, :] = cb
            _, _, ln = _layernorm_parts(cb, lg, lb)
            sb_ref[rows, :] = (jax.nn.silu(zb[rows, :].astype(F32)) * jax.nn.silu(ln)).astype(BF16)
        yb_ref[...] = _dot(sb_ref[...], w_ref[...].reshape(D, D))

    return pl.pallas_call(
        body, name="branch_b_fwd", grid=(n_rows // tm,),
        in_specs=[_rows(tm, G_VB), _rows(tm, G_GB), _rows(tm, G_ZB), _prev_halo(tm, G_VB), _prev_halo(tm, G_GB),
                  _w_out_spec(1), _const(cw_b.shape), _const((LANE_GROUPS, LANES)), _const((1, D)), _const((1, D))],
        out_specs=[_rows(tm), _rows(tm), _rows(tm)],
        out_shape=[jax.ShapeDtypeStruct((n_rows, D), F32), jax.ShapeDtypeStruct((n_rows, D), BF16),
                   jax.ShapeDtypeStruct((n_rows, D), F32)],
        scratch_shapes=[_time_major(tm + HALO), _time_major(tm)],
        compiler_params=_params(1),
    )(proj, proj, proj, proj, proj, wo4_g, cw_b, conv_b_b.reshape(LANE_GROUPS, LANES), ln_g, ln_b)


def _attention(q16, kv_ref):
    probs, outs = [], []
    for h in range(N_HEADS):
        s = _dot_nt(q16[:, h * HEAD_DIM:(h + 1) * HEAD_DIM], kv_ref[h]) * (HEAD_DIM ** -0.5)
        e = jnp.exp(s - jnp.max(s, axis=-1, keepdims=True))
        p = e / jnp.sum(e, axis=-1, keepdims=True)
        probs.append(p)
        outs.append(_dot(p.astype(BF16), kv_ref[N_HEADS + h]))
    return probs, outs


def _branch_x_fwd(proj, kv16, wo4_g, tm):
    n_rows = proj.shape[0]

    def body(q, zx, kv_ref, w_ref, sx_ref, yx_ref, p_ref):
        probs, outs = _attention(q[...].astype(BF16), kv_ref)
        p_ref[...] = jnp.concatenate(probs, axis=-1)
        sx = (jax.nn.silu(_f32(zx)) * jnp.concatenate(outs, axis=-1)).astype(BF16)
        sx_ref[...] = sx
        yx_ref[...] = _dot(sx, w_ref[...].reshape(D, D))

    return pl.pallas_call(
        body, name="branch_x_fwd", grid=(n_rows // tm,),
        in_specs=[_rows(tm, G_Q), _rows(tm, G_ZX), _const(kv16.shape), _w_out_spec(2)],
        out_specs=[_rows(tm), _rows(tm), _rows(tm)],
        out_shape=[jax.ShapeDtypeStruct((n_rows, D), BF16), jax.ShapeDtypeStruct((n_rows, D), F32),
                   jax.ShapeDtypeStruct((n_rows, D), F32)],
        compiler_params=_params(1),
    )(proj, proj, kv16, wo4_g)


def _merge_fwd_bwd(proj, ya, yb, yx, x, target, wo4_g, final_g, tm):
    n_rows = proj.shape[0]
    inv_d = 1.0 / D

    def body(ga, gb, gx, ya_ref, yb_ref, yx_ref, x_ref, t_ref, w_ref, fg_ref,
             dh_ref, dya_ref, dyb_ref, dyx_ref, dp_ref, dw_ref, dfg_ref, sq_ref):
        i = pl.program_id(0)
        wo = w_ref[...].reshape(D, D)
        sig = [jax.nn.sigmoid(_f32(g)) for g in (ga, gb, gx)]
        ys = [ya_ref[...], yb_ref[...], yx_ref[...]]
        m16 = (sig[0] * ys[0] + sig[1] * ys[1] + sig[2] * ys[2]).astype(BF16)
        h = x_ref[...] + _dot(m16, wo)
        r = lax.rsqrt(_mean(h * h) + EPS)
        hn = h * r
        fg = fg_ref[...]
        err = hn * fg - t_ref[...]
        dy = err * inv_d
        dhn = dy * fg
        dh = r * (dhn - hn * _mean(dhn * hn))
        dh_ref[...] = dh
        dh16 = dh.astype(BF16)
        dm = _dot_nt(dh16, wo)
        for n, out in enumerate((dya_ref, dyb_ref, dyx_ref)):
            out[...] = (sig[n] * dm).astype(BF16)
            dp_ref[:, n * D:(n + 1) * D] = (dm * ys[n] * (sig[n] * (1.0 - sig[n]))).astype(BF16)

        @pl.when(i == 0)
        def _():
            dw_ref[...] = jnp.zeros_like(dw_ref)
            dfg_ref[...] = jnp.zeros_like(dfg_ref)
            sq_ref[...] = jnp.zeros_like(sq_ref)

        dw_ref[0] += _dot_tn(m16, dh16)
        dfg_ref[...] += _fold8(dy * hn)
        sq_ref[...] += _fold8(err * err)

    vec = jax.ShapeDtypeStruct((SUBLANES, D), F32)
    return pl.pallas_call(
        body, name="merge_fwd_bwd", grid=(n_rows // tm,),
        in_specs=[_rows(tm, G_GA), _rows(tm, G_GBB), _rows(tm, G_GX), _rows(tm), _rows(tm), _rows(tm), _rows(tm),
                  _rows(tm), _w_out_spec(3), _const((1, D))],
        out_specs=[_rows(tm), _rows(tm), _rows(tm), _rows(tm), pl.BlockSpec((tm, 3 * D), lambda i: (i, 3)),
                   pl.BlockSpec((1, D, D), lambda i: (3, 0, 0)), _const((SUBLANES, D)), _const((SUBLANES, D))],
        out_shape=[jax.ShapeDtypeStruct((n_rows, D), F32), jax.ShapeDtypeStruct((n_rows, D), BF16),
                   jax.ShapeDtypeStruct((n_rows, D), BF16), jax.ShapeDtypeStruct((n_rows, D), BF16),
                   jax.ShapeDtypeStruct((n_rows, N_GROUPS * D), BF16), jax.ShapeDtypeStruct((4, D, D), F32), vec, vec],
        compiler_params=_params(1),
    )(proj, proj, proj, ya, yb, yx, x, target, wo4_g, final_g)


def _branch_a_bwd(dya, proj, sa16, wo4_g, cw_a, dproj, dw4, tm):
    n_rows = proj.shape[0]
    n_tiles = n_rows // tm

    def body(dya_ref, bp, cp, xp, za, sa_ref, dyan, bpn, zan, cph, xph, w_ref, cw_ref, dp_in, dw_in,
             dp_ref, dw_ref, dwa_ref, e1, e2, o_scr, mm_scr):
        del dp_in, dw_in
        i = pl.program_id(0)
        bpn, zan, cph, xph = (_f32(r) for r in (bpn, zan, cph, xph))
        woa = w_ref[...].reshape(D, D)
        dya16 = dya_ref[...]
        chunks = [pl.ds(r0, ELEM_ROWS) for r0 in range(0, tm, ELEM_ROWS)]
        _to_time_major(e1, 0, jnp.where(i > 0, cph * xph, 0.0))
        for r0, rows in zip(range(0, tm, ELEM_ROWS), chunks):
            _to_time_major(e1, HALO + r0, cp[rows, :].astype(F32) * xp[rows, :].astype(F32))
        _conv(o_scr, e1, cw_ref, FWD_TAPS_A, tm)
        mm_scr[...] = _dot_nt(dya16, woa)
        for r0, rows in zip(range(0, tm, ELEM_ROWS), chunks):
            ca = _from_time_major(o_scr, ELEM_ROWS, r0)
            dsa = mm_scr[rows, :]
            b = bp[rows, :].astype(F32)
            silu_z, dsilu_z = _silu_and_grad(za[rows, :].astype(F32))
            t = dsa * silu_z
            dp_ref[rows, 0 * D:1 * D] = (t * ca).astype(BF16)
            dp_ref[rows, 3 * D:4 * D] = (dsa * (b * ca) * dsilu_z).astype(BF16)
            _to_time_major(e2, r0, t * b)
        dcan = (_dot_nt(dyan[...], woa) * jax.nn.silu(zan)) * bpn
        _to_time_major(e2, tm, jnp.where(i < n_tiles - 1, dcan, 0.0))

        @pl.when(i == 0)
        def _():
            dw_ref[...] = jnp.zeros_like(dw_ref)
            dwa_ref[...] = jnp.zeros_like(dwa_ref)

        _conv_wgrad(dwa_ref, e2, e1, FWD_TAPS_A, tm)
        dw_ref[0] += _dot_tn(sa_ref[...], dya16)
        _conv(o_scr, e2, cw_ref, BWD_TAPS_A, tm)
        for r0, rows in zip(range(0, tm, ELEM_ROWS), chunks):
            dprod = _from_time_major(o_scr, ELEM_ROWS, r0)
            dp_ref[rows, 1 * D:2 * D] = (dprod * xp[rows, :].astype(F32)).astype(BF16)
            dp_ref[rows, 2 * D:3 * D] = (dprod * cp[rows, :].astype(F32)).astype(BF16)

    return pl.pallas_call(
        body, name="branch_a_bwd", grid=(n_tiles,),
        in_specs=[_rows(tm), _rows(tm, G_BA), _rows(tm, G_CA), _rows(tm, G_XA), _rows(tm, G_ZA), _rows(tm),
                  _next_halo(tm, n_rows), _next_halo(tm, n_rows, G_BA), _next_halo(tm, n_rows, G_ZA),
                  _prev_halo(tm, G_CA), _prev_halo(tm, G_XA), _w_out_spec(0), _const(cw_a.shape), ANY, ANY],
        out_specs=[pl.BlockSpec((tm, 4 * D), lambda i: (i, 0)), pl.BlockSpec((1, D, D), lambda i: (0, 0, 0)),
                   _const((K_A * LANE_GROUPS, LANES))],
        out_shape=[jax.ShapeDtypeStruct(dproj.shape, BF16), jax.ShapeDtypeStruct(dw4.shape, F32),
                   jax.ShapeDtypeStruct((K_A * LANE_GROUPS, LANES), F32)],
        input_output_aliases={13: 0, 14: 1},
        scratch_shapes=[_time_major(tm + HALO), _time_major(tm + HALO), _time_major(tm), pltpu.VMEM((tm, D), F32)],
        compiler_params=_params(1),
    )(dya, proj, proj, proj, proj, sa16, dya, proj, proj, proj, proj, wo4_g, cw_a, dproj, dw4)


def _branch_b_bwd(dyb, proj, cb, sb16, wo4_g, cw_b, ln_g, ln_b, dproj, dw4, tm):
    n_rows = proj.shape[0]
    n_tiles = n_rows // tm

    def body(dyb_ref, zb, cb_ref, vb, gb, sb_ref, dybn, zbn, cbn, vbh, gbh, w_ref, cw_ref, lg_ref, lb_ref,
             dp_in, dw_in, dp_ref, dw_ref, dwb_ref, dbb_ref, dlg_ref, dlb_ref, e1, e2, o_scr, mm_scr):
        del dp_in, dw_in
        zbn, vbh, gbh = (_f32(r) for r in (zbn, vbh, gbh))
        i = pl.program_id(0)
        wob = w_ref[...].reshape(D, D)
        lg, lb = lg_ref[...], lb_ref[...]

        def conv_out_grad(dsb, z, c):
            xhat, rstd, ln = _layernorm_parts(c, lg, lb)
            sw, dsw = _silu_and_grad(ln)
            sz, dsz = _silu_and_grad(z)
            dln = (dsb * sz) * dsw
            dxhat = dln * lg
            dcb = rstd * (dxhat - _mean(dxhat) - xhat * _mean(dxhat * xhat))
            return dsb * sw * dsz, dln, xhat, dcb

        @pl.when(i == 0)
        def _():
            dw_ref[...] = jnp.zeros_like(dw_ref)
            dwb_ref[...] = jnp.zeros_like(dwb_ref)
            dbb_ref[...] = jnp.zeros_like(dbb_ref)
            dlg_ref[...] = jnp.zeros_like(dlg_ref)
            dlb_ref[...] = jnp.zeros_like(dlb_ref)

        dyb16 = dyb_ref[...]
        mm_scr[...] = _dot_nt(dyb16, wob)
        dlg, dlb, dbb = (jnp.zeros((SUBLANES, D), F32),) * 3
        for r0 in range(0, tm, ELEM_ROWS):
            rows = pl.ds(r0, ELEM_ROWS)
            dzb, dln, xhat, dcb = conv_out_grad(mm_scr[rows, :], zb[rows, :].astype(F32), cb_ref[rows, :])
            dp_ref[rows, 2 * D:3 * D] = dzb.astype(BF16)
            _to_time_major(e2, r0, dcb)
            dlg, dlb, dbb = dlg + _fold8(dln * xhat), dlb + _fold8(dln), dbb + _fold8(dcb)
        _, _, _, dcbn = conv_out_grad(_dot_nt(dybn[...], wob), zbn[...], cbn[...])
        _to_time_major(e2, tm, jnp.where(i < n_tiles - 1, dcbn, 0.0))
        dlg_ref[...] += dlg
        dlb_ref[...] += dlb
        dbb_ref[...] += dbb
        dw_ref[0] += _dot_tn(sb_ref[...], dyb16)
        _to_time_major(e1, 0, jnp.where(i > 0, vbh[...] * jax.nn.sigmoid(gbh[...]), 0.0))
        for r0 in range(0, tm, ELEM_ROWS):
            rows = pl.ds(r0, ELEM_ROWS)
            sg = jax.nn.sigmoid(gb[rows, :].astype(F32))
            mm_scr[rows, :] = sg
            _to_time_major(e1, HALO + r0, vb[rows, :].astype(F32) * sg)
        _conv_wgrad(dwb_ref, e2, e1, FWD_TAPS_B, tm)
        _conv(o_scr, e2, cw_ref, BWD_TAPS_B, tm)
        for r0 in range(0, tm, ELEM_ROWS):
            rows = pl.ds(r0, ELEM_ROWS)
            dglu = _from_time_major(o_scr, ELEM_ROWS, r0)
            sg = mm_scr[rows, :]
            dp_ref[rows, 0 * D:1 * D] = (dglu * sg).astype(BF16)
            dp_ref[rows, 1 * D:2 * D] = (dglu * vb[rows, :].astype(F32) * (sg * (1.0 - sg))).astype(BF16)

    vec = jax.ShapeDtypeStruct((SUBLANES, D), F32)
    return pl.pallas_call(
        body, name="branch_b_bwd", grid=(n_tiles,),
        in_specs=[_rows(tm), _rows(tm, G_ZB), _rows(tm), _rows(tm, G_VB), _rows(tm, G_GB), _rows(tm),
                  _next_halo(tm, n_rows), _next_halo(tm, n_rows, G_ZB), _next_halo(tm, n_rows),
                  _prev_halo(tm, G_VB), _prev_halo(tm, G_GB), _w_out_spec(1), _const(cw_b.shape), _const((1, D)),
                  _const((1, D)), ANY, ANY],
        out_specs=[pl.BlockSpec((tm, 3 * D), lambda i: (i, 2)), pl.BlockSpec((1, D, D), lambda i: (1, 0, 0)),
                   _const((K_B * LANE_GROUPS, LANES)), _const((SUBLANES, D)), _const((SUBLANES, D)),
                   _const((SUBLANES, D))],
        out_shape=[jax.ShapeDtypeStruct(dproj.shape, BF16), jax.ShapeDtypeStruct(dw4.shape, F32),
                   jax.ShapeDtypeStruct((K_B * LANE_GROUPS, LANES), F32), vec, vec, vec],
        input_output_aliases={15: 0, 16: 1},
        scratch_shapes=[_time_major(tm + HALO), _time_major(tm + HALO), _time_major(tm), pltpu.VMEM((tm, D), F32)],
        compiler_params=_params(1),
    )(dyb, proj, cb, proj, proj, sb16, dyb, proj, cb, proj, proj, wo4_g, cw_b, ln_g, ln_b, dproj, dw4)


def _branch_x_bwd(dyx, proj, sx16, probs, kv16, wo4_g, dproj, dw4, tm):
    n_rows = proj.shape[0]
    scale = HEAD_DIM ** -0.5

    def body(dyx_ref, q, zx, sx_ref, p_ref, kv_ref, w_ref, dp_in, dw_in, dp_ref, dw_ref, dkv_ref):
        del dp_in, dw_in
        i = pl.program_id(0)
        dyx16 = dyx_ref[...]
        q16 = q[...].astype(BF16)
        probs = [p_ref[:, h * HEAD_DIM:(h + 1) * HEAD_DIM] for h in range(N_HEADS)]
        outs = [_dot(probs[h].astype(BF16), kv_ref[N_HEADS + h]) for h in range(N_HEADS)]
        dsx = _dot_nt(dyx16, w_ref[...].reshape(D, D))
        silu_z, dsilu_z = _silu_and_grad(_f32(zx))
        dp_ref[:, D:2 * D] = (dsx * jnp.concatenate(outs, axis=-1) * dsilu_z).astype(BF16)
        do16 = (dsx * silu_z).astype(BF16)

        @pl.when(i == 0)
        def _():
            dw_ref[...] = jnp.zeros_like(dw_ref)
            dkv_ref[...] = jnp.zeros_like(dkv_ref)

        for h in range(N_HEADS):
            cols = slice(h * HEAD_DIM, (h + 1) * HEAD_DIM)
            p = probs[h]
            dprob = _dot_nt(do16[:, cols], kv_ref[N_HEADS + h])
            ds16 = ((p * (dprob - jnp.sum(p * dprob, axis=-1, keepdims=True))) * scale).astype(BF16)
            dp_ref[:, cols] = _dot(ds16, kv_ref[h]).astype(BF16)
            dkv_ref[h] += _dot_tn(ds16, q16[:, cols])
            dkv_ref[N_HEADS + h] += _dot_tn(p.astype(BF16), do16[:, cols])
        dw_ref[0] += _dot_tn(sx_ref[...], dyx16)

    return pl.pallas_call(
        body, name="branch_x_bwd", grid=(n_rows // tm,),
        in_specs=[_rows(tm), _rows(tm, G_Q), _rows(tm, G_ZX), _rows(tm), _rows(tm), _const(kv16.shape),
                  _w_out_spec(2), ANY, ANY],
        out_specs=[pl.BlockSpec((tm, 2 * D), lambda i: (i, 2)), pl.BlockSpec((1, D, D), lambda i: (2, 0, 0)),
                   _const(kv16.shape)],
        out_shape=[jax.ShapeDtypeStruct(dproj.shape, BF16), jax.ShapeDtypeStruct(dw4.shape, F32),
                   jax.ShapeDtypeStruct(kv16.shape, F32)],
        input_output_aliases={7: 0, 8: 1},
        compiler_params=_params(1),
    )(dyx, proj, proj, sx16, probs, kv16, wo4_g, dproj, dw4)


def _dp_unit(u):
    g = u // 2
    pos = jnp.where(g < G_VB, g, jnp.where(g < G_Q, g + 2, jnp.where(g < G_GA, g - 3, g)))
    return 2 * pos + u % 2


def _scatter_copies(srcs, lands, send, recv):
    x, y, c = lax.axis_index("x"), lax.axis_index("y"), lax.axis_index("c")
    copies = []
    for n in range(N_DEV - 1):
        flip = n + 1
        px = 1 - x if flip & 4 else x
        py = 1 - y if flip & 2 else y
        pc = 1 - c if flip & 1 else c
        for t, (src, land) in enumerate(zip(srcs, lands)):
            copies.append(pltpu.make_async_remote_copy(
                src_ref=src.at[4 * px + 2 * py + pc], dst_ref=land.at[n], send_sem=send.at[t * (N_DEV - 1) + n],
                recv_sem=recv.at[t * (N_DEV - 1) + n], device_id=(px, py, pc), device_id_type=MESH))
    return copies


def _scatter_start(name, arrays, views, n_views):
    n = len(arrays)
    lands = [lax.empty(tuple(N_DEV - 1 if d == N_DEV else d for d in a.shape), a.dtype) for a in arrays]

    def body(*refs):
        src, land, (send, recv) = refs[:n], refs[n:2 * n], refs[2 * n:2 * n + 2]
        token = refs[-1]
        for cp in _scatter_copies(views(src), views(land), send, recv):
            cp.start()
        token[...] = jnp.zeros_like(token)

    sems = pltpu.SemaphoreType.DMA((n_views * (N_DEV - 1),))
    out = pl.pallas_call(
        body, name=name,
        in_specs=[HBM] * (2 * n),
        out_specs=[SEM, SEM] + [HBM] * (2 * n) + [pl.BlockSpec(memory_space=pltpu.VMEM)],
        out_shape=[sems, sems] + [pltpu.HBM(a.shape, a.dtype) for a in arrays + lands]
        + [jax.ShapeDtypeStruct((SUBLANES, LANES), F32)],
        input_output_aliases={k: 2 + k for k in range(2 * n)},
        compiler_params=pltpu.CompilerParams(has_side_effects=SIDE_EFFECT),
    )(*[pltpu.with_memory_space_constraint(a, pltpu.HBM) for a in arrays + lands])
    return dict(name=name, sems=out[:2], moving=out[2:2 + 2 * n], views=views, token=out[-1])


def _scatter_wait(started, after):
    n = len(started["moving"]) // 2
    views = started["views"]

    def body(*refs):
        src, land, (send, recv) = refs[:n], refs[n:2 * n], refs[2 * n:2 * n + 2]
        for cp in _scatter_copies(views(src), views(land), send, recv):
            cp.wait_send()
            cp.wait_recv()

    out = pl.pallas_call(
        body, name=started["name"].replace("start", "wait"),
        in_specs=[HBM] * (2 * n) + [SEM, SEM, ANY],
        out_specs=[HBM] * (2 * n),
        out_shape=[pltpu.HBM(a.shape, a.dtype) for a in started["moving"]],
        input_output_aliases={k: k for k in range(2 * n)},
        compiler_params=pltpu.CompilerParams(has_side_effects=SIDE_EFFECT),
    )(*started["moving"], *started["sems"], after)
    return out[n:]


def _w_in_grad(ut, dproj, tk, token):
    n_rows = dproj.shape[0]
    n_k = n_rows // tk
    per_shard = W_IN_SHARD // UNIT

    def body(ut_ref, dp0, dp1, dp2, token_ref, out_ref, out16_ref, acc):
        del token_ref
        t = pl.program_id(1)

        for r, dp_ref in enumerate((dp0, dp1, dp2)):
            cols = slice(r * UNIT, (r + 1) * UNIT)

            @pl.when(t == 0)
            def _(dp_ref=dp_ref, cols=cols):
                acc[:, cols] = _dot(ut_ref[...], dp_ref[...])

            @pl.when(t > 0)
            def _(dp_ref=dp_ref, cols=cols):
                acc[:, cols] += _dot(ut_ref[...], dp_ref[...])

        @pl.when(t == n_k - 1)
        def _():
            out_ref[0] = acc[...]
            out16_ref[0] = acc[...].astype(BF16)

    def dp_spec(r):
        return pl.BlockSpec((tk, UNIT), lambda q, t: (t, _dp_unit(per_shard * q + r)))

    shard = pl.BlockSpec((1, D, W_IN_SHARD), lambda q, t: (q, 0, 0))
    return pl.pallas_call(
        body, name="w_in_grad", grid=(N_DEV, n_k),
        in_specs=[pl.BlockSpec((D, tk), lambda q, t: (0, t)), dp_spec(0), dp_spec(1), dp_spec(2), ANY],
        out_specs=[shard, shard],
        out_shape=[jax.ShapeDtypeStruct((N_DEV, D, W_IN_SHARD), F32), jax.ShapeDtypeStruct((N_DEV, D, W_IN_SHARD), BF16)],
        scratch_shapes=[pltpu.VMEM((D, W_IN_SHARD), F32)],
        compiler_params=_params(2),
    )(ut, dproj, dproj, dproj, token)


def _x_grad(dproj, win_t, x, dh, norm_g, token, tm):
    n_rows = x.shape[0]
    n_k = N_GROUPS * D // X_GRAD_K

    def body(dp_ref, wt_ref, x_ref, dh_ref, g_ref, token_ref, gx_ref, dg_ref, acc):
        del token_ref
        i, g = pl.program_id(0), pl.program_id(1)

        @pl.when((i == 0) & (g == 0))
        def _():
            dg_ref[...] = jnp.zeros_like(dg_ref)

        @pl.when(g == 0)
        def _():
            acc[...] = _dot(dp_ref[...], wt_ref[...])

        @pl.when(g > 0)
        def _():
            acc[...] += _dot(dp_ref[...], wt_ref[...])

        @pl.when(g == n_k - 1)
        def _():
            du = acc[...]
            xf = x_ref[...]
            r = lax.rsqrt(_mean(xf * xf) + EPS)
            xn = xf * r
            dun = du * g_ref[...]
            gx_ref[...] = dh_ref[...] + r * (dun - xn * _mean(dun * xn))
            dg_ref[...] += _fold8(du * xn)

    return pl.pallas_call(
        body, name="x_grad", grid=(n_rows // tm, n_k),
        in_specs=[pl.BlockSpec((tm, X_GRAD_K), lambda i, g: (i, g)), pl.BlockSpec((X_GRAD_K, D), lambda i, g: (g, 0)),
                  pl.BlockSpec((tm, D), lambda i, g: (i, 0)), pl.BlockSpec((tm, D), lambda i, g: (i, 0)),
                  _const((1, D)), ANY],
        out_specs=[pl.BlockSpec((tm, D), lambda i, g: (i, 0)), _const((SUBLANES, D))],
        out_shape=[jax.ShapeDtypeStruct((n_rows, D), F32), jax.ShapeDtypeStruct((SUBLANES, D), F32)],
        scratch_shapes=[pltpu.VMEM((tm, D), F32)],
        compiler_params=_params(2),
    )(dproj, win_t, x, dh, norm_g, token)


def _local_step(x, mem, target, norm_g, conv_b_b, ln_g, ln_b, mem_g, final_g, shards):
    n_rows = x.shape[0]
    tm = min(512, n_rows)
    big = min(1024, n_rows)
    u16, ut = _rmsnorm_fwd(x, norm_g, big)
    proj, win_t, _, wkv_g, wo4_g, cw_g = _proj_fwd_gather(u16, shards, min(2048, n_rows))
    cw_rows = cw_g.transpose(1, 0, 2).reshape((SUBLANES + HALO) * LANE_GROUPS, LANES)
    cw_a, cw_b = cw_rows[:SUBLANES * LANE_GROUPS], cw_rows[SUBLANES * LANE_GROUPS:]
    kv16, mn16 = _kv_fwd(mem, mem_g, wkv_g)
    sa16, ya = _branch_a_fwd(proj, wo4_g, cw_a, tm)
    cb, sb16, yb = _branch_b_fwd(proj, wo4_g, cw_b, conv_b_b, ln_g, ln_b, tm)
    sx16, yx, probs = _branch_x_fwd(proj, kv16, wo4_g, big)
    dh, dya, dyb, dyx, dproj, dw4, dfg, sq = _merge_fwd_bwd(proj, ya, yb, yx, x, target, wo4_g, final_g,
                                                             min(256, n_rows))
    dproj, dw4, dwa = _branch_a_bwd(dya, proj, sa16, wo4_g, cw_a, dproj, dw4, tm)
    dproj, dw4, dwb, dbb, dlg, dlb = _branch_b_bwd(dyb, proj, cb, sb16, wo4_g, cw_b, ln_g, ln_b, dproj, dw4, tm)
    dproj, dw4, dkv = _branch_x_bwd(dyx, proj, sx16, probs, kv16, wo4_g, dproj, dw4, tm)
    dwkv_g, dwkv16, dmg = _kv_bwd(dkv, mem, mem_g, mn16, wkv_g)
    dw4 = dw4.reshape(4, N_DEV, D // N_DEV, D)
    small_moving = _scatter_start("small_grads_start", [dw4.astype(BF16), dwkv16],
                                  lambda refs: [refs[0].at[w] for w in range(4)] + [refs[1]], 5)
    dwin_g, dwin16 = _w_in_grad(ut, dproj, min(2048, n_rows), small_moving["token"])
    w_in_moving = _scatter_start("w_in_grad_start", [dwin16], lambda refs: list(refs), 1)
    gx, dng = _x_grad(dproj, win_t, x, dh, norm_g, w_in_moving["token"], big)
    land4, landkv = _scatter_wait(small_moving, dng)
    landin, = _scatter_wait(w_in_moving, dng)
    small = {SV_NORM_G: dng, SV_CONV_B_B: dbb, SV_LN_G: dlg, SV_LN_B: dlb, SV_MEM_G: dmg, SV_FINAL_G: dfg, SV_LOSS: sq}
    grads = [(dwin_g[None], landin[None]), (dw4, land4), (dwkv_g[None], landkv[None])]
    return gx, grads, small, dwa.reshape(K_A, D), dwb.reshape(K_B, D)


def _allgather_small(small, conv_rows):
    keys = sorted(small)

    def body(*refs):
        parts, (conv_ref, out_ref, mine, send, recv) = refs[:len(keys)], refs[len(keys):]
        x, y, c, chips = _place()
        me, sibling = 4 * x + 2 * y + c, (x, y, 1 - c)
        mine[pl.ds(0, SV_CONV_A), :] = jnp.zeros((SV_CONV_A, D), F32)
        for key, part in zip(keys, parts):
            mine[key:key + 1, :] = jnp.sum(part[...], axis=0, keepdims=True)
        mine[pl.ds(SV_CONV_A, SV_ROWS - SV_CONV_A), :] = conv_ref[...]
        out_ref[me] = mine[...]

        def copy(k, block, to, from_mine=False):
            return pltpu.make_async_remote_copy(
                src_ref=mine if from_mine else out_ref.at[block], dst_ref=out_ref.at[block],
                send_sem=send.at[k], recv_sem=recv.at[k], device_id=to, device_id_type=MESH)

        first = [copy(0, me, sibling, from_mine=True)]
        first += [copy(1 + j, me, (*chip, c), from_mine=True) for j, chip in enumerate(chips)]
        for cp in first:
            cp.start()
        passed = []
        for j, (px, py) in enumerate(chips):
            block = 4 * px + 2 * py + c
            copy(1 + j, block, sibling).wait_recv()
            passed.append(copy(4 + j, block, sibling))
            passed[-1].start()
        copy(0, 4 * x + 2 * y + 1 - c, sibling).wait_recv()
        for j, (px, py) in enumerate(chips):
            copy(4 + j, 4 * px + 2 * py + 1 - c, sibling).wait_recv()
        for cp in first + passed:
            cp.wait_send()

    vmem = pl.BlockSpec(memory_space=pltpu.VMEM)
    return pl.pallas_call(
        body, name="allgather_small",
        in_specs=[vmem] * (len(keys) + 1), out_specs=vmem,
        out_shape=jax.ShapeDtypeStruct((N_DEV, SV_ROWS, D), F32),
        scratch_shapes=[pltpu.VMEM((SV_ROWS, D), F32), pltpu.SemaphoreType.DMA((7,)), pltpu.SemaphoreType.DMA((7,))],
    )(*[small[k] for k in keys], conv_rows)


def _adamw(w, g, m, v):
    m = ADAM_B1 * m + (1.0 - ADAM_B1) * g
    v = ADAM_B2 * v + (1.0 - ADAM_B2) * (g * g)
    m_hat = m / (1.0 - ADAM_B1 ** ADAM_STEP)
    v_hat = v / (1.0 - ADAM_B2 ** ADAM_STEP)
    return -ADAM_LR * (m_hat / (jnp.sqrt(v_hat) + ADAM_EPS) + ADAM_WD * w), m, v


def _adamw_shard(own, landed, piece, k_arr, w, m, v, tr):
    n_r, n_c = w.shape
    n_landed = landed.shape[1]

    def body(k_ref, own_ref, *refs):
        del k_ref
        landed_refs, (w_ref, m_ref, v_ref, g_out, d_out, m_out, v_out) = refs[:n_landed], refs[n_landed:]
        g = own_ref[0, 0]
        for landed_ref in landed_refs:
            g = g + landed_ref[0, 0].astype(F32)
        g_out[...] = g
        d_out[...], m_out[...], v_out[...] = _adamw(w_ref[...], g, m_ref[...], v_ref[...])

    blk = (1, 1, tr, n_c)
    flat = pl.BlockSpec((tr, n_c), lambda r, k: (r, 0))
    return pl.pallas_call(
        body, name="adamw_shard",
        grid_spec=pltpu.PrefetchScalarGridSpec(
            num_scalar_prefetch=1, grid=(n_r // tr,),
            in_specs=[pl.BlockSpec(blk, lambda r, k: (piece, k[0], r, 0))]
            + [pl.BlockSpec(blk, functools.partial(lambda r, k, j: (piece, j, r, 0), j=j)) for j in range(n_landed)]
            + [flat] * 3,
            out_specs=[flat] * 4),
        out_shape=[jax.ShapeDtypeStruct((n_r, n_c), F32)] * 4,
        compiler_params=_params(1),
    )(k_arr, own, *([landed] * n_landed), w, m, v)


def _adamw_shards(entries, k_arr):
    n = len(entries)

    def body(k_ref, *refs):
        del k_ref
        ins, outs = refs[:5 * n], refs[5 * n:]
        for e in range(n):
            own_ref, landed_ref, w_ref, m_ref, v_ref = ins[5 * e:5 * e + 5]
            g = own_ref[0, 0]
            for j in range(landed_ref.shape[1]):
                g = g + landed_ref[0, j].astype(F32)
            g_out, d_out, m_out, v_out = outs[4 * e:4 * e + 4]
            g_out[...] = g
            d_out[...], m_out[...], v_out[...] = _adamw(w_ref[...], g, m_ref[...], v_ref[...])

    in_specs, operands, out_specs, out_shape = [], [], [], []
    for own, landed, piece, w, m, v in entries:
        flat = pl.BlockSpec(w.shape, lambda i, k: (0, 0))
        in_specs += [pl.BlockSpec((1, 1, *w.shape), functools.partial(lambda i, k, p: (p, k[0], 0, 0), p=piece)),
                     pl.BlockSpec((1, *landed.shape[1:]), functools.partial(lambda i, k, p: (p, 0, 0, 0), p=piece)),
                     flat, flat, flat]
        operands += [own, landed, w, m, v]
        out_specs += [flat] * 4
        out_shape += [jax.ShapeDtypeStruct(w.shape, F32)] * 4
    out = pl.pallas_call(
        body, name="adamw_shards",
        grid_spec=pltpu.PrefetchScalarGridSpec(num_scalar_prefetch=1, grid=(1,), in_specs=in_specs,
                                               out_specs=out_specs),
        out_shape=out_shape,
        compiler_params=_params(1),
    )(k_arr, *operands)
    return [tuple(out[4 * e:4 * e + 4]) for e in range(n)]


def _adamw_small(gathered, k_arr, vectors, conv_a, conv_b):
    n_vec = len(vectors)
    cols = D // N_DEV

    def body(k_ref, full_ref, cols_ref, *refs):
        del k_ref
        ins, outs = refs[:3 * (n_vec + 2)], refs[3 * (n_vec + 2):-2]
        tot, tot_cols = refs[-2:]
        tot[...] = full_ref[0]
        tot_cols[...] = cols_ref[0]
        for dev in range(1, N_DEV):
            tot[...] += full_ref[dev]
            tot_cols[...] += cols_ref[dev]
        loss = (0.5 / D) * jnp.sum(tot[SV_LOSS:SV_LOSS + 1, :])
        outs[0][...] = jnp.full(outs[0].shape, loss, F32)
        grads = [tot[n:n + 1, :] for n in range(n_vec)]
        grads += [tot_cols[pl.ds(SV_CONV_A, K_A), :], tot_cols[pl.ds(SV_CONV_B, K_B), :]]
        for n, g in enumerate(grads):
            w_ref, m_ref, v_ref = ins[3 * n:3 * n + 3]
            g_out, d_out, m_out, v_out = outs[1 + 4 * n:5 + 4 * n]
            g_out[...] = g
            d_out[...], m_out[...], v_out[...] = _adamw(w_ref[...], g, m_ref[...], v_ref[...])

    weights = list(vectors) + [conv_a, conv_b]
    flat_in = [a for wmv in weights for a in wmv]
    out_shape = [jax.ShapeDtypeStruct((SUBLANES, 128), F32)]
    for wmv in weights:
        out_shape += [jax.ShapeDtypeStruct(wmv[0].shape, F32)] * 4
    return pl.pallas_call(
        body, name="adamw_small",
        grid_spec=pltpu.PrefetchScalarGridSpec(
            num_scalar_prefetch=1, grid=(1,),
            in_specs=[pl.BlockSpec((N_DEV, SV_ROWS, D), lambda i, k: (0, 0, 0)),
                      pl.BlockSpec((N_DEV, SV_ROWS, cols), lambda i, k: (0, 0, k[0]))]
            + [pl.BlockSpec(a.shape, lambda i, k: (0, 0)) for a in flat_in],
            out_specs=[pl.BlockSpec(s.shape, lambda i, k: (0, 0)) for s in out_shape],
            scratch_shapes=[pltpu.VMEM((SV_ROWS, D), F32), pltpu.VMEM((SV_ROWS, cols), F32)]),
        out_shape=out_shape,
        compiler_params=_params(1),
    )(k_arr, gathered, gathered, *flat_in)


def kernel(x, mem, norm_g, w_in, conv_a_w, w_out_a, conv_b_w, conv_b_b, ln_b_g, ln_b_b, w_out_b, mem_norm_g, w_kv, w_out_x, w_o, final_g, loss_target, m_norm_g, m_w_in, m_conv_a_w, m_w_out_a, m_conv_b_w, m_conv_b_b, m_ln_b_g, m_ln_b_b, m_w_out_b, m_mem_norm_g, m_w_kv, m_w_out_x, m_w_o, m_final_g, v_norm_g, v_w_in, v_conv_a_w, v_w_out_a, v_conv_b_w, v_conv_b_b, v_ln_b_g, v_ln_b_b, v_w_out_b, v_mem_norm_g, v_w_kv, v_w_out_x, v_w_o, v_final_g):
    xi, yi, ci = lax.axis_index("x"), lax.axis_index("y"), lax.axis_index("c")
    k_arr = jnp.reshape(4 * xi + 2 * yi + ci, (1,)).astype(jnp.int32)

    cw = jnp.concatenate([jnp.pad(conv_a_w[0], ((0, SUBLANES - K_A), (0, 0))),
                          jnp.pad(conv_b_w[0], ((0, HALO - K_B), (0, 0)))], axis=0)
    wo4 = jnp.stack([w_out_a[0], w_out_b[0], w_out_x[0], w_o[0]]).astype(BF16)
    shards = [w_in[0].astype(BF16), w_kv[0].astype(BF16), wo4, cw]

    final_g2 = final_g.reshape(1, D)
    gx, grads, small, dwa, dwb = _local_step(
        x[0], mem[0], loss_target[0], norm_g, conv_b_b, ln_b_g, ln_b_b, mem_norm_g, final_g2, shards)

    conv_rows = jnp.concatenate([jnp.pad(dwa, ((0, SUBLANES - K_A), (0, 0))),
                                 jnp.pad(dwb, ((0, HALO - K_B), (0, 0)))], axis=0)
    gathered_small = _allgather_small(small, conv_rows)

    res = {"w_in": _adamw_shard(grads[0][0], grads[0][1], 0, k_arr, w_in[0], m_w_in[0], v_w_in[0], 256)}
    small_shards = [("w_out_a", 1, 0, w_out_a, m_w_out_a, v_w_out_a), ("w_out_b", 1, 1, w_out_b, m_w_out_b, v_w_out_b),
                    ("w_out_x", 1, 2, w_out_x, m_w_out_x, v_w_out_x), ("w_o", 1, 3, w_o, m_w_o, v_w_o),
                    ("w_kv", 2, 0, w_kv, m_w_kv, v_w_kv)]
    updated = _adamw_shards([(grads[a][0], grads[a][1], l, w[0], m[0], v[0]) for _, a, l, w, m, v in small_shards],
                            k_arr)
    res.update({name: four for (name, *_), four in zip(small_shards, updated)})
    res = {name: tuple(r[None] for r in four) for name, four in res.items()}
    vectors = [(norm_g, m_norm_g, v_norm_g), (conv_b_b, m_conv_b_b, v_conv_b_b), (ln_b_g, m_ln_b_g, v_ln_b_g),
               (ln_b_b, m_ln_b_b, v_ln_b_b), (mem_norm_g, m_mem_norm_g, v_mem_norm_g),
               (final_g2, m_final_g.reshape(1, D), v_final_g.reshape(1, D))]
    out = _adamw_small(gathered_small, k_arr, vectors, (conv_a_w[0], m_conv_a_w[0], v_conv_a_w[0]),
                       (conv_b_w[0], m_conv_b_w[0], v_conv_b_w[0]))
    loss = out[0][0, 0]
    names = ["norm_g", "conv_b_b", "ln_b_g", "ln_b_b", "mem_norm_g", "final_g", "conv_a_w", "conv_b_w"]
    for n, name in enumerate(names):
        four = out[1 + 4 * n:5 + 4 * n]
        if name == "final_g":
            four = [r.reshape(D) for r in four]
        elif name.startswith("conv_") and name.endswith("_w"):
            four = [r[None] for r in four]
        res[name] = tuple(four)

    order = ["norm_g", "w_in", "conv_a_w", "w_out_a", "conv_b_w", "conv_b_b", "ln_b_g", "ln_b_b", "w_out_b",
             "mem_norm_g", "w_kv", "w_out_x", "w_o", "final_g"]
    return (loss, gx[None], *[res[n][0] for n in order], *[res[n][1] for n in order],
            *[res[n][2] for n in order], *[res[n][3] for n in order])
```

```python
import functools

import jax
import jax.numpy as jnp
from jax import lax
from jax.experimental import pallas as pl
from jax.experimental.pallas import tpu as pltpu

F32, BF16 = jnp.float32, jnp.bfloat16
D = 1024
N_DEV = 8
N_HEADS = 4
HEAD_DIM = D // N_HEADS
N_GROUPS = 12
W_IN_SHARD = N_GROUPS * D // N_DEV
UNIT = 512
X_GRAD_K = 3 * D
K_A, K_B = 3, 31
EPS = 1e-6
HALO = 32
SUBLANES = 8
LANES = 128
LANE_GROUPS = D // LANES
TAP_GROUP = 16
CONV_BLOCK = 32
ELEM_ROWS = 16
CONV_PARTIAL_SUMS = 4
VMEM_LIMIT = 60 << 20
MESH = pl.DeviceIdType.MESH
ANY = pl.BlockSpec(memory_space=pl.ANY)
HBM = pl.BlockSpec(memory_space=pltpu.HBM)
SEM = pl.BlockSpec(memory_space=pltpu.SEMAPHORE)
SIDE_EFFECT = pltpu.SideEffectType.DATAFLOW_SIDE_EFFECTING

G_BA, G_CA, G_XA, G_ZA, G_VB, G_GB, G_ZB, G_Q, G_ZX, G_GA, G_GBB, G_GX = range(N_GROUPS)

ADAM_LR, ADAM_B1, ADAM_B2, ADAM_EPS, ADAM_WD, ADAM_STEP = 0.001, 0.9, 0.999, 1e-08, 0.01, 10

SV_NORM_G, SV_CONV_B_B, SV_LN_G, SV_LN_B, SV_MEM_G, SV_FINAL_G, SV_LOSS = range(7)
SV_CONV_A, SV_CONV_B, SV_ROWS = 8, 16, 48


def _dot(a, b):
    return jnp.dot(a, b, preferred_element_type=F32)


def _dot_nt(a, b):
    return lax.dot_general(a, b, (((1,), (1,)), ((), ())), preferred_element_type=F32)


def _dot_tn(a, b):
    return lax.dot_general(a, b, (((0,), (0,)), ((), ())), preferred_element_type=F32)


def _silu_and_grad(z):
    s = jax.nn.sigmoid(z)
    return z * s, s * (1.0 + z * (1.0 - s))


def _fold8(a):
    return a.reshape(a.shape[0] // SUBLANES, SUBLANES, a.shape[1]).sum(axis=0)


def _mean(a):
    return jnp.mean(a, axis=-1, keepdims=True)


def _f32(ref):
    return ref[...].astype(F32)


def _params(n_grid):
    return pltpu.CompilerParams(dimension_semantics=("arbitrary",) * n_grid, vmem_limit_bytes=VMEM_LIMIT)


def _rows(tm, col=0):
    return pl.BlockSpec((tm, D), lambda i: (i, col))


def _prev_halo(tm, col=0):
    return pl.BlockSpec((HALO, D), lambda i: (jnp.maximum(i * (tm // HALO) - 1, 0), col))


def _next_halo(tm, n_rows, col=0):
    last = n_rows // HALO - 1
    return pl.BlockSpec((HALO, D), lambda i: (jnp.minimum((i + 1) * (tm // HALO), last), col))


def _const(shape):
    return pl.BlockSpec(shape, lambda *_: (0,) * len(shape))


def _w_out_spec(which):
    return pl.BlockSpec((N_DEV, None, D // N_DEV, D), lambda *_: (0, which, 0, 0))


def _to_time_major(t_ref, row0, x):
    n = x.shape[0]
    for j in range(LANE_GROUPS):
        t_ref[pl.ds(row0 * LANE_GROUPS + j, n, stride=LANE_GROUPS), :] = x[:, j * LANES:(j + 1) * LANES]


def _from_time_major(t_ref, n, row0=0):
    return jnp.concatenate([t_ref[pl.ds(row0 * LANE_GROUPS + j, n, stride=LANE_GROUPS), :]
                            for j in range(LANE_GROUPS)], axis=-1)


def _row(ref, t):
    start = t * LANE_GROUPS
    if not isinstance(start, int):
        start = pl.multiple_of(start, LANE_GROUPS)
    return ref[pl.ds(start, LANE_GROUPS), :]


def _tap_groups(taps):
    return [taps[first:first + TAP_GROUP] for first in range(0, len(taps), TAP_GROUP)]


def _conv(o_ref, e_ref, w_ref, taps, n_rows, bias_ref=None):
    for n_group, group in enumerate(_tap_groups(taps)):
        weights = [_row(w_ref, k) for k, _ in group]

        def block(c, carry, n_group=n_group, group=group, weights=weights):
            t0 = c * CONV_BLOCK
            window = {}
            for t in range(CONV_BLOCK):
                parts = [None] * min(CONV_PARTIAL_SUMS, len(group))
                for n, (_, off) in enumerate(group):
                    if t + off not in window:
                        window[t + off] = _row(e_ref, t0 + t + off)
                    term = weights[n] * window[t + off]
                    parts[n % len(parts)] = term if parts[n % len(parts)] is None else parts[n % len(parts)] + term
                window.pop(t + min(off for _, off in group), None)
                while len(parts) > 1:
                    parts = [parts[n] + parts[n + 1] for n in range(0, len(parts) - 1, 2)] + parts[len(parts) & ~1:]
                out = parts[0]
                if n_group > 0:
                    out = out + _row(o_ref, t0 + t)
                elif bias_ref is not None:
                    out = out + bias_ref[...]
                o_ref[pl.ds(pl.multiple_of((t0 + t) * LANE_GROUPS, LANE_GROUPS), LANE_GROUPS), :] = out
            return carry

        lax.fori_loop(0, n_rows // CONV_BLOCK, block, 0)


def _conv_wgrad(dw_ref, d_ref, e_ref, taps, n_rows):
    for group in _tap_groups(taps):
        def block(c, sums, group=group):
            t0 = c * CONV_BLOCK
            sums = list(sums)
            window = {}
            for t in range(CONV_BLOCK):
                d = _row(d_ref, t0 + t)
                for n, (_, off) in enumerate(group):
                    if t + off not in window:
                        window[t + off] = _row(e_ref, t0 + t + off)
                    sums[n] = sums[n] + d * window[t + off]
                window.pop(t + min(off for _, off in group), None)
            return tuple(sums)

        sums = lax.fori_loop(0, n_rows // CONV_BLOCK, block, tuple(_row(dw_ref, k) for k, _ in group))
        for (k, _), total in zip(group, sums):
            dw_ref[pl.ds(k * LANE_GROUPS, LANE_GROUPS), :] = total


FWD_TAPS_A = [(k, HALO - (K_A - 1) + k) for k in range(K_A)]
BWD_TAPS_A = [(k, K_A - 1 - k) for k in range(K_A)]
FWD_TAPS_B = [(k, HALO - (K_B - 1) + k) for k in range(K_B)]
BWD_TAPS_B = [(k, K_B - 1 - k) for k in range(K_B)]


def _time_major(n_rows):
    return pltpu.VMEM((n_rows * LANE_GROUPS, LANES), F32)


def _kv_fwd(mem, mem_g, wkv_g):
    m_len = mem.shape[0]

    def body(mem_ref, g_ref, w_ref, kv_ref, mn_ref):
        mf = mem_ref[...]
        r = lax.rsqrt(_mean(mf * mf) + EPS)
        mn = ((mf * r) * g_ref[...]).astype(BF16)
        mn_ref[...] = mn
        for b in range(2 * N_HEADS):
            kv_ref[b] = _dot(mn, w_ref[b]).astype(BF16)

    return pl.pallas_call(
        body, name="kv_fwd", grid=(1,),
        in_specs=[_const((m_len, D)), _const((1, D)), _const((2 * N_HEADS, D, HEAD_DIM))],
        out_specs=[_const((2 * N_HEADS, m_len, HEAD_DIM)), _const((m_len, D))],
        out_shape=[jax.ShapeDtypeStruct((2 * N_HEADS, m_len, HEAD_DIM), BF16), jax.ShapeDtypeStruct((m_len, D), BF16)],
        compiler_params=_params(1),
    )(mem, mem_g, wkv_g)


def _kv_bwd(dkv, mem, mem_g, mn16, wkv_g):
    m_len = mem.shape[0]

    def body(dkv_ref, mem_ref, g_ref, mn_ref, w_ref, dw_ref, dw16_ref, dg_ref):
        mn = mn_ref[...]
        dmn = jnp.zeros((m_len, D), F32)
        for b in range(2 * N_HEADS):
            d16 = dkv_ref[b].astype(BF16)
            dw = _dot_tn(mn, d16)
            dw_ref[b] = dw
            dw16_ref[b] = dw.astype(BF16)
            dmn = dmn + _dot_nt(d16, w_ref[b])
        mf = mem_ref[...]
        r = lax.rsqrt(_mean(mf * mf) + EPS)
        dg_ref[...] = _fold8(dmn * (mf * r))

    return pl.pallas_call(
        body, name="kv_bwd", grid=(1,),
        in_specs=[_const((2 * N_HEADS, m_len, HEAD_DIM)), _const((m_len, D)), _const((1, D)), _const((m_len, D)),
                  _const((2 * N_HEADS, D, HEAD_DIM))],
        out_specs=[_const((2 * N_HEADS, D, HEAD_DIM)), _const((2 * N_HEADS, D, HEAD_DIM)), _const((SUBLANES, D))],
        out_shape=[jax.ShapeDtypeStruct((2 * N_HEADS, D, HEAD_DIM), F32),
                   jax.ShapeDtypeStruct((2 * N_HEADS, D, HEAD_DIM), BF16), jax.ShapeDtypeStruct((SUBLANES, D), F32)],
        compiler_params=_params(1),
    )(dkv, mem, mem_g, mn16, wkv_g)


def _rmsnorm_fwd(x, norm_g, tm):
    n_rows = x.shape[0]

    def body(x_ref, g_ref, u_ref, ut_ref):
        xf = x_ref[...]
        u = (xf * lax.rsqrt(_mean(xf * xf) + EPS)) * g_ref[...]
        u_ref[...] = u.astype(BF16)
        ut_ref[...] = u.T.astype(BF16)

    return pl.pallas_call(
        body, name="rmsnorm_fwd", grid=(n_rows // tm,),
        in_specs=[_rows(tm), _const((1, D))],
        out_specs=[_rows(tm), pl.BlockSpec((D, tm), lambda i: (0, i))],
        out_shape=[jax.ShapeDtypeStruct((n_rows, D), BF16), jax.ShapeDtypeStruct((D, n_rows), BF16)],
        compiler_params=_params(1),
    )(x, norm_g)


def _place():
    x, y, c = lax.axis_index("x"), lax.axis_index("y"), lax.axis_index("c")
    other_chips = [(1 - x, y), (x, 1 - y), (1 - x, 1 - y)]
    return x, y, c, other_chips


def _arrival_order():
    x, y, c, chips = _place()
    order = [4 * x + 2 * y + c, 4 * x + 2 * y + 1 - c]
    for px, py in chips:
        order += [4 * px + 2 * py + c, 4 * px + 2 * py + 1 - c]
    return order


def _proj_fwd_gather(u16, blocks, tm):
    n = len(blocks)
    n_rows = u16.shape[0]
    n_i = n_rows // tm
    per_shard = W_IN_SHARD // UNIT
    assert n_i >= per_shard

    def wt_index(p, i, order):
        return (_dp_unit(per_shard * order[p] + jnp.minimum(i, per_shard - 1)), 0)

    def body(order_ref, u_ref, *refs):
        src, proj_ref, wt_ref, out = refs[:n], refs[n], refs[n + 1], refs[n + 2:2 * n + 2]
        wbuf, stage_sem, send, recv, own_sem = refs[2 * n + 2:]
        p, i = pl.program_id(0), pl.program_id(1)
        x, y, c, chips = _place()
        me, sibling = 4 * x + 2 * y + c, (x, y, 1 - c)

        def copy(t, k, block, to, from_input=False):
            return pltpu.make_async_remote_copy(
                src_ref=src[t] if from_input else out[t].at[block], dst_ref=out[t].at[block],
                send_sem=send.at[t, k], recv_sem=recv.at[t, k], device_id=to, device_id_type=MESH)

        def own_copies():
            return [pltpu.make_async_copy(src[t], out[t].at[me], own_sem.at[t]) for t in range(n)]

        def first_copies():
            first = []
            for t in range(n):
                first.append(copy(t, 0, me, sibling, from_input=True))
                first += [copy(t, 1 + j, me, (*chip, c), from_input=True) for j, chip in enumerate(chips)]
            return first

        def stage(slot, block):
            return pltpu.make_async_copy(out[0].at[block], wbuf.at[slot], stage_sem.at[slot])

        @pl.when((p == 0) & (i == 0))
        def _():
            for cp in own_copies() + first_copies():
                cp.start()
            mine = pltpu.make_async_copy(src[0], wbuf.at[0], stage_sem.at[0])
            mine.start()
            mine.wait()

        @pl.when((p > 0) & (i == 0))
        def _():
            stage(p % 2, order_ref[p]).wait()

        proj_ref[...] = _dot(u_ref[...], wbuf[p % 2])
        for r in range(per_shard):
            @pl.when(i == r)
            def _(r=r):
                wt_ref[...] = wbuf[p % 2, :, r * UNIT:(r + 1) * UNIT].astype(F32).T.astype(BF16)

        for nxt in range(1, N_DEV):
            @pl.when((p == nxt - 1) & (i == n_i - 1))
            def _(nxt=nxt):
                if nxt == 1:
                    block = 4 * x + 2 * y + 1 - c
                    copy(0, 0, block, sibling).wait_recv()
                else:
                    j, passed_on = divmod(nxt - 2, 2)
                    px, py = chips[j]
                    if passed_on:
                        block = 4 * px + 2 * py + 1 - c
                        copy(0, 4 + j, block, sibling).wait_recv()
                    else:
                        block = 4 * px + 2 * py + c
                        copy(0, 1 + j, block, sibling).wait_recv()
                        copy(0, 4 + j, block, sibling).start()
                stage(nxt % 2, block).start()

        @pl.when((p == N_DEV - 1) & (i == n_i - 1))
        def _():
            passed = [copy(0, 4 + j, 4 * px + 2 * py + c, sibling) for j, (px, py) in enumerate(chips)]
            for j, (px, py) in enumerate(chips):
                for t in range(1, n):
                    block = 4 * px + 2 * py + c
                    copy(t, 1 + j, block, sibling).wait_recv()
                    passed.append(copy(t, 4 + j, block, sibling))
                    passed[-1].start()
            for t in range(1, n):
                copy(t, 0, 4 * x + 2 * y + 1 - c, sibling).wait_recv()
                for j, (px, py) in enumerate(chips):
                    copy(t, 4 + j, 4 * px + 2 * py + 1 - c, sibling).wait_recv()
            for cp in first_copies() + passed:
                cp.wait_send()
            for cp in own_copies():
                cp.wait()

    return pl.pallas_call(
        body, name="proj_fwd_gather",
        grid_spec=pltpu.PrefetchScalarGridSpec(
            num_scalar_prefetch=1, grid=(N_DEV, n_i),
            in_specs=[pl.BlockSpec((tm, D), lambda p, i, order: (i, 0))] + [ANY] * n,
            out_specs=[pl.BlockSpec((tm, W_IN_SHARD), lambda p, i, order: (i, order[p])),
                       pl.BlockSpec((UNIT, D), wt_index)] + [ANY] * n,
            scratch_shapes=[pltpu.VMEM((2, D, W_IN_SHARD), BF16), pltpu.SemaphoreType.DMA((2,)),
                            pltpu.SemaphoreType.DMA((n, 7)), pltpu.SemaphoreType.DMA((n, 7)),
                            pltpu.SemaphoreType.DMA((n,))]),
        out_shape=[jax.ShapeDtypeStruct((n_rows, N_GROUPS * D), F32), jax.ShapeDtypeStruct((N_GROUPS * D, D), BF16)]
        + [jax.ShapeDtypeStruct((N_DEV, *b.shape), b.dtype) for b in blocks],
        compiler_params=_params(2),
    )(jnp.stack(_arrival_order()).astype(jnp.int32), u16, *blocks)


def _branch_a_fwd(proj, wo4_g, cw_a, tm):
    n_rows = proj.shape[0]

    def body(bp, cp, xp, za, cph, xph, w_ref, cw_ref, sa_ref, ya_ref, e_scr, o_scr):
        i = pl.program_id(0)
        _to_time_major(e_scr, 0, jnp.where(i > 0, _f32(cph) * _f32(xph), 0.0))
        for r0 in range(0, tm, ELEM_ROWS):
            rows = pl.ds(r0, ELEM_ROWS)
            _to_time_major(e_scr, HALO + r0, cp[rows, :].astype(F32) * xp[rows, :].astype(F32))
        _conv(o_scr, e_scr, cw_ref, FWD_TAPS_A, tm)
        for r0 in range(0, tm, ELEM_ROWS):
            rows = pl.ds(r0, ELEM_ROWS)
            ca = _from_time_major(o_scr, ELEM_ROWS, r0)
            sa_ref[rows, :] = (jax.nn.silu(za[rows, :].astype(F32)) * (bp[rows, :].astype(F32) * ca)).astype(BF16)
        ya_ref[...] = _dot(sa_ref[...], w_ref[...].reshape(D, D))

    return pl.pallas_call(
        body, name="branch_a_fwd", grid=(n_rows // tm,),
        in_specs=[_rows(tm, G_BA), _rows(tm, G_CA), _rows(tm, G_XA), _rows(tm, G_ZA),
                  _prev_halo(tm, G_CA), _prev_halo(tm, G_XA), _w_out_spec(0), _const(cw_a.shape)],
        out_specs=[_rows(tm), _rows(tm)],
        out_shape=[jax.ShapeDtypeStruct((n_rows, D), BF16), jax.ShapeDtypeStruct((n_rows, D), F32)],
        scratch_shapes=[_time_major(tm + HALO), _time_major(tm)],
        compiler_params=_params(1),
    )(proj, proj, proj, proj, proj, proj, wo4_g, cw_a)


def _layernorm_parts(cb, lg, lb):
    xc = cb - _mean(cb)
    rstd = lax.rsqrt(_mean(xc * xc) + EPS)
    xhat = xc * rstd
    return xhat, rstd, xhat * lg + lb


def _branch_b_fwd(proj, wo4_g, cw_b, conv_b_b, ln_g, ln_b, tm):
    n_rows = proj.shape[0]

    def body(vb, gb, zb, vbh, gbh, w_ref, cw_ref, bb_ref, lg_ref, lb_ref, cb_ref, sb_ref, yb_ref, e_scr, o_scr):
        i = pl.program_id(0)
        vbh, gbh = _f32(vbh), _f32(gbh)
        lg, lb = lg_ref[...], lb_ref[...]
        _to_time_major(e_scr, 0, jnp.where(i > 0, vbh * jax.nn.sigmoid(gbh), 0.0))
        for r0 in range(0, tm, ELEM_ROWS):
            rows = pl.ds(r0, ELEM_ROWS)
            _to_time_major(e_scr, HALO + r0, vb[rows, :].astype(F32) * jax.nn.sigmoid(gb[rows, :].astype(F32)))
        _conv(o_scr, e_scr, cw_ref, FWD_TAPS_B, tm, bias_ref=bb_ref)
        for r0 in range(0, tm, ELEM_ROWS):
            rows = pl.ds(r0, ELEM_ROWS)
            cb = _from_time_major(o_scr, ELEM_ROWS, r0)
            cb_ref[rows, :] = cb
            _, _, ln = _layernorm_parts(cb, lg, lb)
            sb_ref[rows, :] = (jax.nn.silu(zb[rows, :].astype(F32)) * jax.nn.silu(ln)).astype(BF16)
        yb_ref[...] = _dot(sb_ref[...], w_ref[...].reshape(D, D))

    return pl.pallas_call(
        body, name="branch_b_fwd", grid=(n_rows // tm,),
        in_specs=[_rows(tm, G_VB), _rows(tm, G_GB), _rows(tm, G_ZB), _prev_halo(tm, G_VB), _prev_halo(tm, G_GB),
                  _w_out_spec(1), _const(cw_b.shape), _const((LANE_GROUPS, LANES)), _const((1, D)), _const((1, D))],
        out_specs=[_rows(tm), _rows(tm), _rows(tm)],
        out_shape=[jax.ShapeDtypeStruct((n_rows, D), F32), jax.ShapeDtypeStruct((n_rows, D), BF16),
                   jax.ShapeDtypeStruct((n_rows, D), F32)],
        scratch_shapes=[_time_major(tm + HALO), _time_major(tm)],
        compiler_params=_params(1),
    )(proj, proj, proj, proj, proj, wo4_g, cw_b, conv_b_b.reshape(LANE_GROUPS, LANES), ln_g, ln_b)


def _attention(q16, kv_ref):
    probs, outs = [], []
    for h in range(N_HEADS):
        s = _dot_nt(q16[:, h * HEAD_DIM:(h + 1) * HEAD_DIM], kv_ref[h]) * (HEAD_DIM ** -0.5)
        e = jnp.exp(s - jnp.max(s, axis=-1, keepdims=True))
        p = e / jnp.sum(e, axis=-1, keepdims=True)
        probs.append(p)
        outs.append(_dot(p.astype(BF16), kv_ref[N_HEADS + h]))
    return probs, outs


def _branch_x_fwd(proj, kv16, wo4_g, tm):
    n_rows = proj.shape[0]

    def body(q, zx, kv_ref, w_ref, sx_ref, yx_ref, p_ref):
        probs, outs = _attention(q[...].astype(BF16), kv_ref)
        p_ref[...] = jnp.concatenate(probs, axis=-1)
        sx = (jax.nn.silu(_f32(zx)) * jnp.concatenate(outs, axis=-1)).astype(BF16)
        sx_ref[...] = sx
        yx_ref[...] = _dot(sx, w_ref[...].reshape(D, D))

    return pl.pallas_call(
        body, name="branch_x_fwd", grid=(n_rows // tm,),
        in_specs=[_rows(tm, G_Q), _rows(tm, G_ZX), _const(kv16.shape), _w_out_spec(2)],
        out_specs=[_rows(tm), _rows(tm), _rows(tm)],
        out_shape=[jax.ShapeDtypeStruct((n_rows, D), BF16), jax.ShapeDtypeStruct((n_rows, D), F32),
                   jax.ShapeDtypeStruct((n_rows, D), F32)],
        compiler_params=_params(1),
    )(proj, proj, kv16, wo4_g)


def _merge_fwd_bwd(proj, ya, yb, yx, x, target, wo4_g, final_g, tm):
    n_rows = proj.shape[0]
    inv_d = 1.0 / D

    def body(ga, gb, gx, ya_ref, yb_ref, yx_ref, x_ref, t_ref, w_ref, fg_ref,
             dh_ref, dya_ref, dyb_ref, dyx_ref, dp_ref, dw_ref, dfg_ref, sq_ref):
        i = pl.program_id(0)
        wo = w_ref[...].reshape(D, D)
        sig = [jax.nn.sigmoid(_f32(g)) for g in (ga, gb, gx)]
        ys = [ya_ref[...], yb_ref[...], yx_ref[...]]
        m16 = (sig[0] * ys[0] + sig[1] * ys[1] + sig[2] * ys[2]).astype(BF16)
        h = x_ref[...] + _dot(m16, wo)
        r = lax.rsqrt(_mean(h * h) + EPS)
        hn = h * r
        fg = fg_ref[...]
        err = hn * fg - t_ref[...]
        dy = err * inv_d
        dhn = dy * fg
        dh = r * (dhn - hn * _mean(dhn * hn))
        dh_ref[...] = dh
        dh16 = dh.astype(BF16)
        dm = _dot_nt(dh16, wo)
        for n, out in enumerate((dya_ref, dyb_ref, dyx_ref)):
            out[...] = (sig[n] * dm).astype(BF16)
            dp_ref[:, n * D:(n + 1) * D] = (dm * ys[n] * (sig[n] * (1.0 - sig[n]))).astype(BF16)

        @pl.when(i == 0)
        def _():
            dw_ref[...] = jnp.zeros_like(dw_ref)
            dfg_ref[...] = jnp.zeros_like(dfg_ref)
            sq_ref[...] = jnp.zeros_like(sq_ref)

        dw_ref[0] += _dot_tn(m16, dh16)
        dfg_ref[...] += _fold8(dy * hn)
        sq_ref[...] += _fold8(err * err)

    vec = jax.ShapeDtypeStruct((SUBLANES, D), F32)
    return pl.pallas_call(
        body, name="merge_fwd_bwd", grid=(n_rows // tm,),
        in_specs=[_rows(tm, G_GA), _rows(tm, G_GBB), _rows(tm, G_GX), _rows(tm), _rows(tm), _rows(tm), _rows(tm),
                  _rows(tm), _w_out_spec(3), _const((1, D))],
        out_specs=[_rows(tm), _rows(tm), _rows(tm), _rows(tm), pl.BlockSpec((tm, 3 * D), lambda i: (i, 3)),
                   pl.BlockSpec((1, D, D), lambda i: (3, 0, 0)), _const((SUBLANES, D)), _const((SUBLANES, D))],
        out_shape=[jax.ShapeDtypeStruct((n_rows, D), F32), jax.ShapeDtypeStruct((n_rows, D), BF16),
                   jax.ShapeDtypeStruct((n_rows, D), BF16), jax.ShapeDtypeStruct((n_rows, D), BF16),
                   jax.ShapeDtypeStruct((n_rows, N_GROUPS * D), BF16), jax.ShapeDtypeStruct((4, D, D), F32), vec, vec],
        compiler_params=_params(1),
    )(proj, proj, proj, ya, yb, yx, x, target, wo4_g, final_g)


def _branch_a_bwd(dya, proj, sa16, wo4_g, cw_a, dproj, dw4, tm):
    n_rows = proj.shape[0]
    n_tiles = n_rows // tm

    def body(dya_ref, bp, cp, xp, za, sa_ref, dyan, bpn, zan, cph, xph, w_ref, cw_ref, dp_in, dw_in,
             dp_ref, dw_ref, dwa_ref, e1, e2, o_scr, mm_scr):
        del dp_in, dw_in
        i = pl.program_id(0)
        bpn, zan, cph, xph = (_f32(r) for r in (bpn, zan, cph, xph))
        woa = w_ref[...].reshape(D, D)
        dya16 = dya_ref[...]
        chunks = [pl.ds(r0, ELEM_ROWS) for r0 in range(0, tm, ELEM_ROWS)]
        _to_time_major(e1, 0, jnp.where(i > 0, cph * xph, 0.0))
        for r0, rows in zip(range(0, tm, ELEM_ROWS), chunks):
            _to_time_major(e1, HALO + r0, cp[rows, :].astype(F32) * xp[rows, :].astype(F32))
        _conv(o_scr, e1, cw_ref, FWD_TAPS_A, tm)
        mm_scr[...] = _dot_nt(dya16, woa)
        for r0, rows in zip(range(0, tm, ELEM_ROWS), chunks):
            ca = _from_time_major(o_scr, ELEM_ROWS, r0)
            dsa = mm_scr[rows, :]
            b = bp[rows, :].astype(F32)
            silu_z, dsilu_z = _silu_and_grad(za[rows, :].astype(F32))
            t = dsa * silu_z
            dp_ref[rows, 0 * D:1 * D] = (t * ca).astype(BF16)
            dp_ref[rows, 3 * D:4 * D] = (dsa * (b * ca) * dsilu_z).astype(BF16)
            _to_time_major(e2, r0, t * b)
        dcan = (_dot_nt(dyan[...], woa) * jax.nn.silu(zan)) * bpn
        _to_time_major(e2, tm, jnp.where(i < n_tiles - 1, dcan, 0.0))

        @pl.when(i == 0)
        def _():
            dw_ref[...] = jnp.zeros_like(dw_ref)
            dwa_ref[...] = jnp.zeros_like(dwa_ref)

        _conv_wgrad(dwa_ref, e2, e1, FWD_TAPS_A, tm)
        dw_ref[0] += _dot_tn(sa_ref[...], dya16)
        _conv(o_scr, e2, cw_ref, BWD_TAPS_A, tm)
        for r0, rows in zip(range(0, tm, ELEM_ROWS), chunks):
            dprod = _from_time_major(o_scr, ELEM_ROWS, r0)
            dp_ref[rows, 1 * D:2 * D] = (dprod * xp[rows, :].astype(F32)).astype(BF16)
            dp_ref[rows, 2 * D:3 * D] = (dprod * cp[rows, :].astype(F32)).astype(BF16)

    return pl.pallas_call(
        body, name="branch_a_bwd", grid=(n_tiles,),
        in_specs=[_rows(tm), _rows(tm, G_BA), _rows(tm, G_CA), _rows(tm, G_XA), _rows(tm, G_ZA), _rows(tm),
                  _next_halo(tm, n_rows), _next_halo(tm, n_rows, G_BA), _next_halo(tm, n_rows, G_ZA),
                  _prev_halo(tm, G_CA), _prev_halo(tm, G_XA), _w_out_spec(0), _const(cw_a.shape), ANY, ANY],
        out_specs=[pl.BlockSpec((tm, 4 * D), lambda i: (i, 0)), pl.BlockSpec((1, D, D), lambda i: (0, 0, 0)),
                   _const((K_A * LANE_GROUPS, LANES))],
        out_shape=[jax.ShapeDtypeStruct(dproj.shape, BF16), jax.ShapeDtypeStruct(dw4.shape, F32),
                   jax.ShapeDtypeStruct((K_A * LANE_GROUPS, LANES), F32)],
        input_output_aliases={13: 0, 14: 1},
        scratch_shapes=[_time_major(tm + HALO), _time_major(tm + HALO), _time_major(tm), pltpu.VMEM((tm, D), F32)],
        compiler_params=_params(1),
    )(dya, proj, proj, proj, proj, sa16, dya, proj, proj, proj, proj, wo4_g, cw_a, dproj, dw4)


def _branch_b_bwd(dyb, proj, cb, sb16, wo4_g, cw_b, ln_g, ln_b, dproj, dw4, tm):
    n_rows = proj.shape[0]
    n_tiles = n_rows // tm

    def body(dyb_ref, zb, cb_ref, vb, gb, sb_ref, dybn, zbn, cbn, vbh, gbh, w_ref, cw_ref, lg_ref, lb_ref,
             dp_in, dw_in, dp_ref, dw_ref, dwb_ref, dbb_ref, dlg_ref, dlb_ref, e1, e2, o_scr, mm_scr):
        del dp_in, dw_in
        zbn, vbh, gbh = (_f32(r) for r in (zbn, vbh, gbh))
        i = pl.program_id(0)
        wob = w_ref[...].reshape(D, D)
        lg, lb = lg_ref[...], lb_ref[...]

        def conv_out_grad(dsb, z, c):
            xhat, rstd, ln = _layernorm_parts(c, lg, lb)
            sw, dsw = _silu_and_grad(ln)
            sz, dsz = _silu_and_grad(z)
            dln = (dsb * sz) * dsw
            dxhat = dln * lg
            dcb = rstd * (dxhat - _mean(dxhat) - xhat * _mean(dxhat * xhat))
            return dsb * sw * dsz, dln, xhat, dcb

        @pl.when(i == 0)
        def _():
            dw_ref[...] = jnp.zeros_like(dw_ref)
            dwb_ref[...] = jnp.zeros_like(dwb_ref)
            dbb_ref[...] = jnp.zeros_like(dbb_ref)
            dlg_ref[...] = jnp.zeros_like(dlg_ref)
            dlb_ref[...] = jnp.zeros_like(dlb_ref)

        dyb16 = dyb_ref[...]
        mm_scr[...] = _dot_nt(dyb16, wob)
        dlg, dlb, dbb = (jnp.zeros((SUBLANES, D), F32),) * 3
        for r0 in range(0, tm, ELEM_ROWS):
            rows = pl.ds(r0, ELEM_ROWS)
            dzb, dln, xhat, dcb = conv_out_grad(mm_scr[rows, :], zb[rows, :].astype(F32), cb_ref[rows, :])
            dp_ref[rows, 2 * D:3 * D] = dzb.astype(BF16)
            _to_time_major(e2, r0, dcb)
            dlg, dlb, dbb = dlg + _fold8(dln * xhat), dlb + _fold8(dln), dbb + _fold8(dcb)
        _, _, _, dcbn = conv_out_grad(_dot_nt(dybn[...], wob), zbn[...], cbn[...])
        _to_time_major(e2, tm, jnp.where(i < n_tiles - 1, dcbn, 0.0))
        dlg_ref[...] += dlg
        dlb_ref[...] += dlb
        dbb_ref[...] += dbb
        dw_ref[0] += _dot_tn(sb_ref[...], dyb16)
        _to_time_major(e1, 0, jnp.where(i > 0, vbh[...] * jax.nn.sigmoid(gbh[...]), 0.0))
        for r0 in range(0, tm, ELEM_ROWS):
            rows = pl.ds(r0, ELEM_ROWS)
            sg = jax.nn.sigmoid(gb[rows, :].astype(F32))
            mm_scr[rows, :] = sg
            _to_time_major(e1, HALO + r0, vb[rows, :].astype(F32) * sg)
        _conv_wgrad(dwb_ref, e2, e1, FWD_TAPS_B, tm)
        _conv(o_scr, e2, cw_ref, BWD_TAPS_B, tm)
        for r0 in range(0, tm, ELEM_ROWS):
            rows = pl.ds(r0, ELEM_ROWS)
            dglu = _from_time_major(o_scr, ELEM_ROWS, r0)
            sg = mm_scr[rows, :]
            dp_ref[rows, 0 * D:1 * D] = (dglu * sg).astype(BF16)
            dp_ref[rows, 1 * D:2 * D] = (dglu * vb[rows, :].astype(F32) * (sg * (1.0 - sg))).astype(BF16)

    vec = jax.ShapeDtypeStruct((SUBLANES, D), F32)
    return pl.pallas_call(
        body, name="branch_b_bwd", grid=(n_tiles,),
        in_specs=[_rows(tm), _rows(tm, G_ZB), _rows(tm), _rows(tm, G_VB), _rows(tm, G_GB), _rows(tm),
                  _next_halo(tm, n_rows), _next_halo(tm, n_rows, G_ZB), _next_halo(tm, n_rows),
                  _prev_halo(tm, G_VB), _prev_halo(tm, G_GB), _w_out_spec(1), _const(cw_b.shape), _const((1, D)),
                  _const((1, D)), ANY, ANY],
        out_specs=[pl.BlockSpec((tm, 3 * D), lambda i: (i, 2)), pl.BlockSpec((1, D, D), lambda i: (1, 0, 0)),
                   _const((K_B * LANE_GROUPS, LANES)), _const((SUBLANES, D)), _const((SUBLANES, D)),
                   _const((SUBLANES, D))],
        out_shape=[jax.ShapeDtypeStruct(dproj.shape, BF16), jax.ShapeDtypeStruct(dw4.shape, F32),
                   jax.ShapeDtypeStruct((K_B * LANE_GROUPS, LANES), F32), vec, vec, vec],
        input_output_aliases={15: 0, 16: 1},
        scratch_shapes=[_time_major(tm + HALO), _time_major(tm + HALO), _time_major(tm), pltpu.VMEM((tm, D), F32)],
        compiler_params=_params(1),
    )(dyb, proj, cb, proj, proj, sb16, dyb, proj, cb, proj, proj, wo4_g, cw_b, ln_g, ln_b, dproj, dw4)


def _branch_x_bwd(dyx, proj, sx16, probs, kv16, wo4_g, dproj, dw4, tm):
    n_rows = proj.shape[0]
    scale = HEAD_DIM ** -0.5

    def body(dyx_ref, q, zx, sx_ref, p_ref, kv_ref, w_ref, dp_in, dw_in, dp_ref, dw_ref, dkv_ref):
        del dp_in, dw_in
        i = pl.program_id(0)
        dyx16 = dyx_ref[...]
        q16 = q[...].astype(BF16)
        probs = [p_ref[:, h * HEAD_DIM:(h + 1) * HEAD_DIM] for h in range(N_HEADS)]
        outs = [_dot(probs[h].astype(BF16), kv_ref[N_HEADS + h]) for h in range(N_HEADS)]
        dsx = _dot_nt(dyx16, w_ref[...].reshape(D, D))
        silu_z, dsilu_z = _silu_and_grad(_f32(zx))
        dp_ref[:, D:2 * D] = (dsx * jnp.concatenate(outs, axis=-1) * dsilu_z).astype(BF16)
        do16 = (dsx * silu_z).astype(BF16)

        @pl.when(i == 0)
        def _():
            dw_ref[...] = jnp.zeros_like(dw_ref)
            dkv_ref[...] = jnp.zeros_like(dkv_ref)

        for h in range(N_HEADS):
            cols = slice(h * HEAD_DIM, (h + 1) * HEAD_DIM)
            p = probs[h]
            dprob = _dot_nt(do16[:, cols], kv_ref[N_HEADS + h])
            ds16 = ((p * (dprob - jnp.sum(p * dprob, axis=-1, keepdims=True))) * scale).astype(BF16)
            dp_ref[:, cols] = _dot(ds16, kv_ref[h]).astype(BF16)
            dkv_ref[h] += _dot_tn(ds16, q16[:, cols])
            dkv_ref[N_HEADS + h] += _dot_tn(p.astype(BF16), do16[:, cols])
        dw_ref[0] += _dot_tn(sx_ref[...], dyx16)

    return pl.pallas_call(
        body, name="branch_x_bwd", grid=(n_rows // tm,),
        in_specs=[_rows(tm), _rows(tm, G_Q), _rows(tm, G_ZX), _rows(tm), _rows(tm), _const(kv16.shape),
                  _w_out_spec(2), ANY, ANY],
        out_specs=[pl.BlockSpec((tm, 2 * D), lambda i: (i, 2)), pl.BlockSpec((1, D, D), lambda i: (2, 0, 0)),
                   _const(kv16.shape)],
        out_shape=[jax.ShapeDtypeStruct(dproj.shape, BF16), jax.ShapeDtypeStruct(dw4.shape, F32),
                   jax.ShapeDtypeStruct(kv16.shape, F32)],
        input_output_aliases={7: 0, 8: 1},
        compiler_params=_params(1),
    )(dyx, proj, proj, sx16, probs, kv16, wo4_g, dproj, dw4)


def _dp_unit(u):
    g = u // 2
    pos = jnp.where(g < G_VB, g, jnp.where(g < G_Q, g + 2, jnp.where(g < G_GA, g - 3, g)))
    return 2 * pos + u % 2


def _scatter_copies(srcs, lands, send, recv):
    x, y, c = lax.axis_index("x"), lax.axis_index("y"), lax.axis_index("c")
    copies = []
    for n in range(N_DEV - 1):
        flip = n + 1
        px = 1 - x if flip & 4 else x
        py = 1 - y if flip & 2 else y
        pc = 1 - c if flip & 1 else c
        for t, (src, land) in enumerate(zip(srcs, lands)):
            copies.append(pltpu.make_async_remote_copy(
                src_ref=src.at[4 * px + 2 * py + pc], dst_ref=land.at[n], send_sem=send.at[t * (N_DEV - 1) + n],
                recv_sem=recv.at[t * (N_DEV - 1) + n], device_id=(px, py, pc), device_id_type=MESH))
    return copies


def _scatter_start(name, arrays, views, n_views):
    n = len(arrays)
    lands = [lax.empty(tuple(N_DEV - 1 if d == N_DEV else d for d in a.shape), a.dtype) for a in arrays]

    def body(*refs):
        src, land, (send, recv) = refs[:n], refs[n:2 * n], refs[2 * n:2 * n + 2]
        token = refs[-1]
        for cp in _scatter_copies(views(src), views(land), send, recv):
            cp.start()
        token[...] = jnp.zeros_like(token)

    sems = pltpu.SemaphoreType.DMA((n_views * (N_DEV - 1),))
    out = pl.pallas_call(
        body, name=name,
        in_specs=[HBM] * (2 * n),
        out_specs=[SEM, SEM] + [HBM] * (2 * n) + [pl.BlockSpec(memory_space=pltpu.VMEM)],
        out_shape=[sems, sems] + [pltpu.HBM(a.shape, a.dtype) for a in arrays + lands]
        + [jax.ShapeDtypeStruct((SUBLANES, LANES), F32)],
        input_output_aliases={k: 2 + k for k in range(2 * n)},
        compiler_params=pltpu.CompilerParams(has_side_effects=SIDE_EFFECT),
    )(*[pltpu.with_memory_space_constraint(a, pltpu.HBM) for a in arrays + lands])
    return dict(name=name, sems=out[:2], moving=out[2:2 + 2 * n], views=views, token=out[-1])


def _scatter_wait(started, after):
    n = len(started["moving"]) // 2
    views = started["views"]

    def body(*refs):
        src, land, (send, recv) = refs[:n], refs[n:2 * n], refs[2 * n:2 * n + 2]
        for cp in _scatter_copies(views(src), views(land), send, recv):
            cp.wait_send()
            cp.wait_recv()

    out = pl.pallas_call(
        body, name=started["name"].replace("start", "wait"),
        in_specs=[HBM] * (2 * n) + [SEM, SEM, ANY],
        out_specs=[HBM] * (2 * n),
        out_shape=[pltpu.HBM(a.shape, a.dtype) for a in started["moving"]],
        input_output_aliases={k: k for k in range(2 * n)},
        compiler_params=pltpu.CompilerParams(has_side_effects=SIDE_EFFECT),
    )(*started["moving"], *started["sems"], after)
    return out[n:]


def _w_in_grad(ut, dproj, tk, token):
    n_rows = dproj.shape[0]
    n_k = n_rows // tk
    per_shard = W_IN_SHARD // UNIT

    def body(ut_ref, dp0, dp1, dp2, token_ref, out_ref, out16_ref, acc):
        del token_ref
        t = pl.program_id(1)

        for r, dp_ref in enumerate((dp0, dp1, dp2)):
            cols = slice(r * UNIT, (r + 1) * UNIT)

            @pl.when(t == 0)
            def _(dp_ref=dp_ref, cols=cols):
                acc[:, cols] = _dot(ut_ref[...], dp_ref[...])

            @pl.when(t > 0)
            def _(dp_ref=dp_ref, cols=cols):
                acc[:, cols] += _dot(ut_ref[...], dp_ref[...])

        @pl.when(t == n_k - 1)
        def _():
            out_ref[0] = acc[...]
            out16_ref[0] = acc[...].astype(BF16)

    def dp_spec(r):
        return pl.BlockSpec((tk, UNIT), lambda q, t: (t, _dp_unit(per_shard * q + r)))

    shard = pl.BlockSpec((1, D, W_IN_SHARD), lambda q, t: (q, 0, 0))
    return pl.pallas_call(
        body, name="w_in_grad", grid=(N_DEV, n_k),
        in_specs=[pl.BlockSpec((D, tk), lambda q, t: (0, t)), dp_spec(0), dp_spec(1), dp_spec(2), ANY],
        out_specs=[shard, shard],
        out_shape=[jax.ShapeDtypeStruct((N_DEV, D, W_IN_SHARD), F32), jax.ShapeDtypeStruct((N_DEV, D, W_IN_SHARD), BF16)],
        scratch_shapes=[pltpu.VMEM((D, W_IN_SHARD), F32)],
        compiler_params=_params(2),
    )(ut, dproj, dproj, dproj, token)


def _x_grad(dproj, win_t, x, dh, norm_g, token, tm):
    n_rows = x.shape[0]
    n_k = N_GROUPS * D // X_GRAD_K

    def body(dp_ref, wt_ref, x_ref, dh_ref, g_ref, token_ref, gx_ref, dg_ref, acc):
        del token_ref
        i, g = pl.program_id(0), pl.program_id(1)

        @pl.when((i == 0) & (g == 0))
        def _():
            dg_ref[...] = jnp.zeros_like(dg_ref)

        @pl.when(g == 0)
        def _():
            acc[...] = _dot(dp_ref[...], wt_ref[...])

        @pl.when(g > 0)
        def _():
            acc[...] += _dot(dp_ref[...], wt_ref[...])

        @pl.when(g == n_k - 1)
        def _():
            du = acc[...]
            xf = x_ref[...]
            r = lax.rsqrt(_mean(xf * xf) + EPS)
            xn = xf * r
            dun = du * g_ref[...]
            gx_ref[...] = dh_ref[...] + r * (dun - xn * _mean(dun * xn))
            dg_ref[...] += _fold8(du * xn)

    return pl.pallas_call(
        body, name="x_grad", grid=(n_rows // tm, n_k),
        in_specs=[pl.BlockSpec((tm, X_GRAD_K), lambda i, g: (i, g)), pl.BlockSpec((X_GRAD_K, D), lambda i, g: (g, 0)),
                  pl.BlockSpec((tm, D), lambda i, g: (i, 0)), pl.BlockSpec((tm, D), lambda i, g: (i, 0)),
                  _const((1, D)), ANY],
        out_specs=[pl.BlockSpec((tm, D), lambda i, g: (i, 0)), _const((SUBLANES, D))],
        out_shape=[jax.ShapeDtypeStruct((n_rows, D), F32), jax.ShapeDtypeStruct((SUBLANES, D), F32)],
        scratch_shapes=[pltpu.VMEM((tm, D), F32)],
        compiler_params=_params(2),
    )(dproj, win_t, x, dh, norm_g, token)


def _local_step(x, mem, target, norm_g, conv_b_b, ln_g, ln_b, mem_g, final_g, shards):
    n_rows = x.shape[0]
    tm = min(512, n_rows)
    big = min(1024, n_rows)
    u16, ut = _rmsnorm_fwd(x, norm_g, big)
    proj, win_t, _, wkv_g, wo4_g, cw_g = _proj_fwd_gather(u16, shards, min(2048, n_rows))
    cw_rows = cw_g.transpose(1, 0, 2).reshape((SUBLANES + HALO) * LANE_GROUPS, LANES)
    cw_a, cw_b = cw_rows[:SUBLANES * LANE_GROUPS], cw_rows[SUBLANES * LANE_GROUPS:]
    kv16, mn16 = _kv_fwd(mem, mem_g, wkv_g)
    sa16, ya = _branch_a_fwd(proj, wo4_g, cw_a, big)
    cb, sb16, yb = _branch_b_fwd(proj, wo4_g, cw_b, conv_b_b, ln_g, ln_b, big)
    sx16, yx, probs = _branch_x_fwd(proj, kv16, wo4_g, big)
    dh, dya, dyb, dyx, dproj, dw4, dfg, sq = _merge_fwd_bwd(proj, ya, yb, yx, x, target, wo4_g, final_g,
                                                             min(256, n_rows))
    dproj, dw4, dwa = _branch_a_bwd(dya, proj, sa16, wo4_g, cw_a, dproj, dw4, tm)
    dproj, dw4, dwb, dbb, dlg, dlb = _branch_b_bwd(dyb, proj, cb, sb16, wo4_g, cw_b, ln_g, ln_b, dproj, dw4, tm)
    dproj, dw4, dkv = _branch_x_bwd(dyx, proj, sx16, probs, kv16, wo4_g, dproj, dw4, tm)
    dwkv_g, dwkv16, dmg = _kv_bwd(dkv, mem, mem_g, mn16, wkv_g)
    dw4 = dw4.reshape(4, N_DEV, D // N_DEV, D)
    small_moving = _scatter_start("small_grads_start", [dw4.astype(BF16), dwkv16],
                                  lambda refs: [refs[0].at[w] for w in range(4)] + [refs[1]], 5)
    dwin_g, dwin16 = _w_in_grad(ut, dproj, min(2048, n_rows), small_moving["token"])
    w_in_moving = _scatter_start("w_in_grad_start", [dwin16], lambda refs: list(refs), 1)
    gx, dng = _x_grad(dproj, win_t, x, dh, norm_g, w_in_moving["token"], big)
    land4, landkv = _scatter_wait(small_moving, dng)
    landin, = _scatter_wait(w_in_moving, dng)
    small = {SV_NORM_G: dng, SV_CONV_B_B: dbb, SV_LN_G: dlg, SV_LN_B: dlb, SV_MEM_G: dmg, SV_FINAL_G: dfg, SV_LOSS: sq}
    grads = [(dwin_g[None], landin[None]), (dw4, land4), (dwkv_g[None], landkv[None])]
    return gx, grads, small, dwa.reshape(K_A, D), dwb.reshape(K_B, D)


def _allgather_small(small, conv_rows):
    keys = sorted(small)

    def body(*refs):
        parts, (conv_ref, out_ref, mine, send, recv) = refs[:len(keys)], refs[len(keys):]
        x, y, c, chips = _place()
        me, sibling = 4 * x + 2 * y + c, (x, y, 1 - c)
        mine[pl.ds(0, SV_CONV_A), :] = jnp.zeros((SV_CONV_A, D), F32)
        for key, part in zip(keys, parts):
            mine[key:key + 1, :] = jnp.sum(part[...], axis=0, keepdims=True)
        mine[pl.ds(SV_CONV_A, SV_ROWS - SV_CONV_A), :] = conv_ref[...]
        out_ref[me] = mine[...]

        def copy(k, block, to, from_mine=False):
            return pltpu.make_async_remote_copy(
                src_ref=mine if from_mine else out_ref.at[block], dst_ref=out_ref.at[block],
                send_sem=send.at[k], recv_sem=recv.at[k], device_id=to, device_id_type=MESH)

        first = [copy(0, me, sibling, from_mine=True)]
        first += [copy(1 + j, me, (*chip, c), from_mine=True) for j, chip in enumerate(chips)]
        for cp in first:
            cp.start()
        passed = []
        for j, (px, py) in enumerate(chips):
            block = 4 * px + 2 * py + c
            copy(1 + j, block, sibling).wait_recv()
            passed.append(copy(4 + j, block, sibling))
            passed[-1].start()
        copy(0, 4 * x + 2 * y + 1 - c, sibling).wait_recv()
        for j, (px, py) in enumerate(chips):
            copy(4 + j, 4 * px + 2 * py + 1 - c, sibling).wait_recv()
        for cp in first + passed:
            cp.wait_send()

    vmem = pl.BlockSpec(memory_space=pltpu.VMEM)
    return pl.pallas_call(
        body, name="allgather_small",
        in_specs=[vmem] * (len(keys) + 1), out_specs=vmem,
        out_shape=jax.ShapeDtypeStruct((N_DEV, SV_ROWS, D), F32),
        scratch_shapes=[pltpu.VMEM((SV_ROWS, D), F32), pltpu.SemaphoreType.DMA((7,)), pltpu.SemaphoreType.DMA((7,))],
    )(*[small[k] for k in keys], conv_rows)


def _adamw(w, g, m, v):
    m = ADAM_B1 * m + (1.0 - ADAM_B1) * g
    v = ADAM_B2 * v + (1.0 - ADAM_B2) * (g * g)
    m_hat = m / (1.0 - ADAM_B1 ** ADAM_STEP)
    v_hat = v / (1.0 - ADAM_B2 ** ADAM_STEP)
    return -ADAM_LR * (m_hat / (jnp.sqrt(v_hat) + ADAM_EPS) + ADAM_WD * w), m, v


def _adamw_shard(own, landed, piece, k_arr, w, m, v, tr):
    n_r, n_c = w.shape
    n_landed = landed.shape[1]

    def body(k_ref, own_ref, *refs):
        del k_ref
        landed_refs, (w_ref, m_ref, v_ref, g_out, d_out, m_out, v_out) = refs[:n_landed], refs[n_landed:]
        g = own_ref[0, 0]
        for landed_ref in landed_refs:
            g = g + landed_ref[0, 0].astype(F32)
        g_out[...] = g
        d_out[...], m_out[...], v_out[...] = _adamw(w_ref[...], g, m_ref[...], v_ref[...])

    blk = (1, 1, tr, n_c)
    flat = pl.BlockSpec((tr, n_c), lambda r, k: (r, 0))
    return pl.pallas_call(
        body, name="adamw_shard",
        grid_spec=pltpu.PrefetchScalarGridSpec(
            num_scalar_prefetch=1, grid=(n_r // tr,),
            in_specs=[pl.BlockSpec(blk, lambda r, k: (piece, k[0], r, 0))]
            + [pl.BlockSpec(blk, functools.partial(lambda r, k, j: (piece, j, r, 0), j=j)) for j in range(n_landed)]
            + [flat] * 3,
            out_specs=[flat] * 4),
        out_shape=[jax.ShapeDtypeStruct((n_r, n_c), F32)] * 4,
        compiler_params=_params(1),
    )(k_arr, own, *([landed] * n_landed), w, m, v)


def _adamw_shards(entries, k_arr):
    n = len(entries)

    def body(k_ref, *refs):
        del k_ref
        ins, outs = refs[:5 * n], refs[5 * n:]
        for e in range(n):
            own_ref, landed_ref, w_ref, m_ref, v_ref = ins[5 * e:5 * e + 5]
            g = own_ref[0, 0]
            for j in range(landed_ref.shape[1]):
                g = g + landed_ref[0, j].astype(F32)
            g_out, d_out, m_out, v_out = outs[4 * e:4 * e + 4]
            g_out[...] = g
            d_out[...], m_out[...], v_out[...] = _adamw(w_ref[...], g, m_ref[...], v_ref[...])

    in_specs, operands, out_specs, out_shape = [], [], [], []
    for own, landed, piece, w, m, v in entries:
        flat = pl.BlockSpec(w.shape, lambda i, k: (0, 0))
        in_specs += [pl.BlockSpec((1, 1, *w.shape), functools.partial(lambda i, k, p: (p, k[0], 0, 0), p=piece)),
                     pl.BlockSpec((1, *landed.shape[1:]), functools.partial(lambda i, k, p: (p, 0, 0, 0), p=piece)),
                     flat, flat, flat]
        operands += [own, landed, w, m, v]
        out_specs += [flat] * 4
        out_shape += [jax.ShapeDtypeStruct(w.shape, F32)] * 4
    out = pl.pallas_call(
        body, name="adamw_shards",
        grid_spec=pltpu.PrefetchScalarGridSpec(num_scalar_prefetch=1, grid=(1,), in_specs=in_specs,
                                               out_specs=out_specs),
        out_shape=out_shape,
        compiler_params=_params(1),
    )(k_arr, *operands)
    return [tuple(out[4 * e:4 * e + 4]) for e in range(n)]


def _adamw_small(gathered, k_arr, vectors, conv_a, conv_b):
    n_vec = len(vectors)
    cols = D // N_DEV

    def body(k_ref, full_ref, cols_ref, *refs):
        del k_ref
        ins, outs = refs[:3 * (n_vec + 2)], refs[3 * (n_vec + 2):-2]
        tot, tot_cols = refs[-2:]
        tot[...] = full_ref[0]
        tot_cols[...] = cols_ref[0]
        for dev in range(1, N_DEV):
            tot[...] += full_ref[dev]
            tot_cols[...] += cols_ref[dev]
        loss = (0.5 / D) * jnp.sum(tot[SV_LOSS:SV_LOSS + 1, :])
        outs[0][...] = jnp.full(outs[0].shape, loss, F32)
        grads = [tot[n:n + 1, :] for n in range(n_vec)]
        grads += [tot_cols[pl.ds(SV_CONV_A, K_A), :], tot_cols[pl.ds(SV_CONV_B, K_B), :]]
        for n, g in enumerate(grads):
            w_ref, m_ref, v_ref = ins[3 * n:3 * n + 3]
            g_out, d_out, m_out, v_out = outs[1 + 4 * n:5 + 4 * n]
            g_out[...] = g
            d_out[...], m_out[...], v_out[...] = _adamw(w_ref[...], g, m_ref[...], v_ref[...])

    weights = list(vectors) + [conv_a, conv_b]
    flat_in = [a for wmv in weights for a in wmv]
    out_shape = [jax.ShapeDtypeStruct((SUBLANES, 128), F32)]
    for wmv in weights:
        out_shape += [jax.ShapeDtypeStruct(wmv[0].shape, F32)] * 4
    return pl.pallas_call(
        body, name="adamw_small",
        grid_spec=pltpu.PrefetchScalarGridSpec(
            num_scalar_prefetch=1, grid=(1,),
            in_specs=[pl.BlockSpec((N_DEV, SV_ROWS, D), lambda i, k: (0, 0, 0)),
                      pl.BlockSpec((N_DEV, SV_ROWS, cols), lambda i, k: (0, 0, k[0]))]
            + [pl.BlockSpec(a.shape, lambda i, k: (0, 0)) for a in flat_in],
            out_specs=[pl.BlockSpec(s.shape, lambda i, k: (0, 0)) for s in out_shape],
            scratch_shapes=[pltpu.VMEM((SV_ROWS, D), F32), pltpu.VMEM((SV_ROWS, cols), F32)]),
        out_shape=out_shape,
        compiler_params=_params(1),
    )(k_arr, gathered, gathered, *flat_in)


def kernel(x, mem, norm_g, w_in, conv_a_w, w_out_a, conv_b_w, conv_b_b, ln_b_g, ln_b_b, w_out_b, mem_norm_g, w_kv, w_out_x, w_o, final_g, loss_target, m_norm_g, m_w_in, m_conv_a_w, m_w_out_a, m_conv_b_w, m_conv_b_b, m_ln_b_g, m_ln_b_b, m_w_out_b, m_mem_norm_g, m_w_kv, m_w_out_x, m_w_o, m_final_g, v_norm_g, v_w_in, v_conv_a_w, v_w_out_a, v_conv_b_w, v_conv_b_b, v_ln_b_g, v_ln_b_b, v_w_out_b, v_mem_norm_g, v_w_kv, v_w_out_x, v_w_o, v_final_g):
    xi, yi, ci = lax.axis_index("x"), lax.axis_index("y"), lax.axis_index("c")
    k_arr = jnp.reshape(4 * xi + 2 * yi + ci, (1,)).astype(jnp.int32)

    cw = jnp.concatenate([jnp.pad(conv_a_w[0], ((0, SUBLANES - K_A), (0, 0))),
                          jnp.pad(conv_b_w[0], ((0, HALO - K_B), (0, 0)))], axis=0)
    wo4 = jnp.stack([w_out_a[0], w_out_b[0], w_out_x[0], w_o[0]]).astype(BF16)
    shards = [w_in[0].astype(BF16), w_kv[0].astype(BF16), wo4, cw]

    final_g2 = final_g.reshape(1, D)
    gx, grads, small, dwa, dwb = _local_step(
        x[0], mem[0], loss_target[0], norm_g, conv_b_b, ln_b_g, ln_b_b, mem_norm_g, final_g2, shards)

    conv_rows = jnp.concatenate([jnp.pad(dwa, ((0, SUBLANES - K_A), (0, 0))),
                                 jnp.pad(dwb, ((0, HALO - K_B), (0, 0)))], axis=0)
    gathered_small = _allgather_small(small, conv_rows)

    res = {"w_in": _adamw_shard(grads[0][0], grads[0][1], 0, k_arr, w_in[0], m_w_in[0], v_w_in[0], 256)}
    small_shards = [("w_out_a", 1, 0, w_out_a, m_w_out_a, v_w_out_a), ("w_out_b", 1, 1, w_out_b, m_w_out_b, v_w_out_b),
                    ("w_out_x", 1, 2, w_out_x, m_w_out_x, v_w_out_x), ("w_o", 1, 3, w_o, m_w_o, v_w_o),
                    ("w_kv", 2, 0, w_kv, m_w_kv, v_w_kv)]
    updated = _adamw_shards([(grads[a][0], grads[a][1], l, w[0], m[0], v[0]) for _, a, l, w, m, v in small_shards],
                            k_arr)
    res.update({name: four for (name, *_), four in zip(small_shards, updated)})
    res = {name: tuple(r[None] for r in four) for name, four in res.items()}
    vectors = [(norm_g, m_norm_g, v_norm_g), (conv_b_b, m_conv_b_b, v_conv_b_b), (ln_b_g, m_ln_b_g, v_ln_b_g),
               (ln_b_b, m_ln_b_b, v_ln_b_b), (mem_norm_g, m_mem_norm_g, v_mem_norm_g),
               (final_g2, m_final_g.reshape(1, D), v_final_g.reshape(1, D))]
    out = _adamw_small(gathered_small, k_arr, vectors, (conv_a_w[0], m_conv_a_w[0], v_conv_a_w[0]),
                       (conv_b_w[0], m_conv_b_w[0], v_conv_b_w[0]))
    loss = out[0][0, 0]
    names = ["norm_g", "conv_b_b", "ln_b_g", "ln_b_b", "mem_norm_g", "final_g", "conv_a_w", "conv_b_w"]
    for n, name in enumerate(names):
        four = out[1 + 4 * n:5 + 4 * n]
        if name == "final_g":
            four = [r.reshape(D) for r in four]
        elif name.startswith("conv_") and name.endswith("_w"):
            four = [r[None] for r in four]
        res[name] = tuple(four)

    order = ["norm_g", "w_in", "conv_a_w", "w_out_a", "conv_b_w", "conv_b_b", "ln_b_g", "ln_b_b", "w_out_b",
             "mem_norm_g", "w_kv", "w_out_x", "w_o", "final_g"]
    return (loss, gx[None], *[res[n][0] for n in order], *[res[n][1] for n in order],
            *[res[n][2] for n in order], *[res[n][3] for n in order])
```

```python
import functools

import jax
import jax.numpy as jnp
from jax import lax
from jax.experimental import pallas as pl
from jax.experimental.pallas import tpu as pltpu

F32, BF16 = jnp.float32, jnp.bfloat16
D = 1024
N_DEV = 8
N_HEADS = 4
HEAD_DIM = D // N_HEADS
N_GROUPS = 12
W_IN_SHARD = N_GROUPS * D // N_DEV
UNIT = 512
X_GRAD_K = 3 * D
K_A, K_B = 3, 31
EPS = 1e-6
HALO = 32
SUBLANES = 8
LANES = 128
LANE_GROUPS = D // LANES
TAP_GROUP = 16
CONV_BLOCK = 32
ELEM_ROWS = 16
CONV_PARTIAL_SUMS = 4
VMEM_LIMIT = 60 << 20
MESH = pl.DeviceIdType.MESH
ANY = pl.BlockSpec(memory_space=pl.ANY)
HBM = pl.BlockSpec(memory_space=pltpu.HBM)
SEM = pl.BlockSpec(memory_space=pltpu.SEMAPHORE)
SIDE_EFFECT = pltpu.SideEffectType.DATAFLOW_SIDE_EFFECTING

G_BA, G_CA, G_XA, G_ZA, G_VB, G_GB, G_ZB, G_Q, G_ZX, G_GA, G_GBB, G_GX = range(N_GROUPS)

ADAM_LR, ADAM_B1, ADAM_B2, ADAM_EPS, ADAM_WD, ADAM_STEP = 0.001, 0.9, 0.999, 1e-08, 0.01, 10

SV_NORM_G, SV_CONV_B_B, SV_LN_G, SV_LN_B, SV_MEM_G, SV_FINAL_G, SV_LOSS = range(7)
SV_CONV_A, SV_CONV_B, SV_ROWS = 8, 16, 48


def _dot(a, b):
    return jnp.dot(a, b, preferred_element_type=F32)


def _dot_nt(a, b):
    return lax.dot_general(a, b, (((1,), (1,)), ((), ())), preferred_element_type=F32)


def _dot_tn(a, b):
    return lax.dot_general(a, b, (((0,), (0,)), ((), ())), preferred_element_type=F32)


def _silu_and_grad(z):
    s = jax.nn.sigmoid(z)
    return z * s, s * (1.0 + z * (1.0 - s))


def _fold8(a):
    return a.reshape(a.shape[0] // SUBLANES, SUBLANES, a.shape[1]).sum(axis=0)


def _mean(a):
    return jnp.mean(a, axis=-1, keepdims=True)


def _f32(ref):
    return ref[...].astype(F32)


def _params(n_grid):
    return pltpu.CompilerParams(dimension_semantics=("arbitrary",) * n_grid, vmem_limit_bytes=VMEM_LIMIT)


def _rows(tm, col=0):
    return pl.BlockSpec((tm, D), lambda i: (i, col))


def _prev_halo(tm, col=0):
    return pl.BlockSpec((HALO, D), lambda i: (jnp.maximum(i * (tm // HALO) - 1, 0), col))


def _next_halo(tm, n_rows, col=0):
    last = n_rows // HALO - 1
    return pl.BlockSpec((HALO, D), lambda i: (jnp.minimum((i + 1) * (tm // HALO), last), col))


def _const(shape):
    return pl.BlockSpec(shape, lambda *_: (0,) * len(shape))


def _w_out_spec(which):
    return pl.BlockSpec((N_DEV, None, D // N_DEV, D), lambda *_: (0, which, 0, 0))


def _to_time_major(t_ref, row0, x):
    n = x.shape[0]
    for j in range(LANE_GROUPS):
        t_ref[pl.ds(row0 * LANE_GROUPS + j, n, stride=LANE_GROUPS), :] = x[:, j * LANES:(j + 1) * LANES]


def _from_time_major(t_ref, n, row0=0):
    return jnp.concatenate([t_ref[pl.ds(row0 * LANE_GROUPS + j, n, stride=LANE_GROUPS), :]
                            for j in range(LANE_GROUPS)], axis=-1)


def _row(ref, t):
    start = t * LANE_GROUPS
    if not isinstance(start, int):
        start = pl.multiple_of(start, LANE_GROUPS)
    return ref[pl.ds(start, LANE_GROUPS), :]


def _tap_groups(taps):
    return [taps[first:first + TAP_GROUP] for first in range(0, len(taps), TAP_GROUP)]


def _conv(o_ref, e_ref, w_ref, taps, n_rows, bias_ref=None):
    for n_group, group in enumerate(_tap_groups(taps)):
        weights = [_row(w_ref, k) for k, _ in group]

        def block(c, carry, n_group=n_group, group=group, weights=weights):
            t0 = c * CONV_BLOCK
            window = {}
            for t in range(CONV_BLOCK):
                parts = [None] * min(CONV_PARTIAL_SUMS, len(group))
                for n, (_, off) in enumerate(group):
                    if t + off not in window:
                        window[t + off] = _row(e_ref, t0 + t + off)
                    term = weights[n] * window[t + off]
                    parts[n % len(parts)] = term if parts[n % len(parts)] is None else parts[n % len(parts)] + term
                window.pop(t + min(off for _, off in group), None)
                while len(parts) > 1:
                    parts = [parts[n] + parts[n + 1] for n in range(0, len(parts) - 1, 2)] + parts[len(parts) & ~1:]
                out = parts[0]
                if n_group > 0:
                    out = out + _row(o_ref, t0 + t)
                elif bias_ref is not None:
                    out = out + bias_ref[...]
                o_ref[pl.ds(pl.multiple_of((t0 + t) * LANE_GROUPS, LANE_GROUPS), LANE_GROUPS), :] = out
            return carry

        lax.fori_loop(0, n_rows // CONV_BLOCK, block, 0)


def _conv_wgrad(dw_ref, d_ref, e_ref, taps, n_rows):
    for group in _tap_groups(taps):
        def block(c, sums, group=group):
            t0 = c * CONV_BLOCK
            sums = list(sums)
            window = {}
            for t in range(CONV_BLOCK):
                d = _row(d_ref, t0 + t)
                for n, (_, off) in enumerate(group):
                    if t + off not in window:
                        window[t + off] = _row(e_ref, t0 + t + off)
                    sums[n] = sums[n] + d * window[t + off]
                window.pop(t + min(off for _, off in group), None)
            return tuple(sums)

        sums = lax.fori_loop(0, n_rows // CONV_BLOCK, block, tuple(_row(dw_ref, k) for k, _ in group))
        for (k, _), total in zip(group, sums):
            dw_ref[pl.ds(k * LANE_GROUPS, LANE_GROUPS), :] = total


FWD_TAPS_A = [(k, HALO - (K_A - 1) + k) for k in range(K_A)]
BWD_TAPS_A = [(k, K_A - 1 - k) for k in range(K_A)]
FWD_TAPS_B = [(k, HALO - (K_B - 1) + k) for k in range(K_B)]
BWD_TAPS_B = [(k, K_B - 1 - k) for k in range(K_B)]


def _time_major(n_rows):
    return pltpu.VMEM((n_rows * LANE_GROUPS, LANES), F32)


def _kv_fwd(mem, mem_g, wkv_g):
    m_len = mem.shape[0]

    def body(mem_ref, g_ref, w_ref, kv_ref, mn_ref):
        mf = mem_ref[...]
        r = lax.rsqrt(_mean(mf * mf) + EPS)
        mn = ((mf * r) * g_ref[...]).astype(BF16)
        mn_ref[...] = mn
        for b in range(2 * N_HEADS):
            kv_ref[b] = _dot(mn, w_ref[b]).astype(BF16)

    return pl.pallas_call(
        body, name="kv_fwd", grid=(1,),
        in_specs=[_const((m_len, D)), _const((1, D)), _const((2 * N_HEADS, D, HEAD_DIM))],
        out_specs=[_const((2 * N_HEADS, m_len, HEAD_DIM)), _const((m_len, D))],
        out_shape=[jax.ShapeDtypeStruct((2 * N_HEADS, m_len, HEAD_DIM), BF16), jax.ShapeDtypeStruct((m_len, D), BF16)],
        compiler_params=_params(1),
    )(mem, mem_g, wkv_g)


def _kv_bwd(dkv, mem, mem_g, mn16, wkv_g):
    m_len = mem.shape[0]

    def body(dkv_ref, mem_ref, g_ref, mn_ref, w_ref, dw_ref, dw16_ref, dg_ref):
        mn = mn_ref[...]
        dmn = jnp.zeros((m_len, D), F32)
        for b in range(2 * N_HEADS):
            d16 = dkv_ref[b].astype(BF16)
            dw = _dot_tn(mn, d16)
            dw_ref[b] = dw
            dw16_ref[b] = dw.astype(BF16)
            dmn = dmn + _dot_nt(d16, w_ref[b])
        mf = mem_ref[...]
        r = lax.rsqrt(_mean(mf * mf) + EPS)
        dg_ref[...] = _fold8(dmn * (mf * r))

    return pl.pallas_call(
        body, name="kv_bwd", grid=(1,),
        in_specs=[_const((2 * N_HEADS, m_len, HEAD_DIM)), _const((m_len, D)), _const((1, D)), _const((m_len, D)),
                  _const((2 * N_HEADS, D, HEAD_DIM))],
        out_specs=[_const((2 * N_HEADS, D, HEAD_DIM)), _const((2 * N_HEADS, D, HEAD_DIM)), _const((SUBLANES, D))],
        out_shape=[jax.ShapeDtypeStruct((2 * N_HEADS, D, HEAD_DIM), F32),
                   jax.ShapeDtypeStruct((2 * N_HEADS, D, HEAD_DIM), BF16), jax.ShapeDtypeStruct((SUBLANES, D), F32)],
        compiler_params=_params(1),
    )(dkv, mem, mem_g, mn16, wkv_g)


def _rmsnorm_fwd(x, norm_g, tm):
    n_rows = x.shape[0]

    def body(x_ref, g_ref, u_ref, ut_ref):
        xf = x_ref[...]
        u = (xf * lax.rsqrt(_mean(xf * xf) + EPS)) * g_ref[...]
        u_ref[...] = u.astype(BF16)
        ut_ref[...] = u.T.astype(BF16)

    return pl.pallas_call(
        body, name="rmsnorm_fwd", grid=(n_rows // tm,),
        in_specs=[_rows(tm), _const((1, D))],
        out_specs=[_rows(tm), pl.BlockSpec((D, tm), lambda i: (0, i))],
        out_shape=[jax.ShapeDtypeStruct((n_rows, D), BF16), jax.ShapeDtypeStruct((D, n_rows), BF16)],
        compiler_params=_params(1),
    )(x, norm_g)


def _place():
    x, y, c = lax.axis_index("x"), lax.axis_index("y"), lax.axis_index("c")
    other_chips = [(1 - x, y), (x, 1 - y), (1 - x, 1 - y)]
    return x, y, c, other_chips


def _arrival_order():
    x, y, c, chips = _place()
    order = [4 * x + 2 * y + c, 4 * x + 2 * y + 1 - c]
    for px, py in chips:
        order += [4 * px + 2 * py + c, 4 * px + 2 * py + 1 - c]
    return order


def _proj_fwd_gather(u16, blocks, tm):
    n = len(blocks)
    n_rows = u16.shape[0]
    n_i = n_rows // tm
    per_shard = W_IN_SHARD // UNIT
    assert n_i >= per_shard

    def wt_index(p, i, order):
        return (_dp_unit(per_shard * order[p] + jnp.minimum(i, per_shard - 1)), 0)

    def body(order_ref, u_ref, *refs):
        src, proj_ref, wt_ref, out = refs[:n], refs[n], refs[n + 1], refs[n + 2:2 * n + 2]
        wbuf, stage_sem, send, recv, own_sem, ubuf, u_sem = refs[2 * n + 2:]
        p, i = pl.program_id(0), pl.program_id(1)

        def u_copy(r):
            rows = pl.ds(r * tm, tm)
            return pltpu.make_async_copy(u_ref.at[rows], ubuf.at[rows], u_sem.at[r])
        x, y, c, chips = _place()
        me, sibling = 4 * x + 2 * y + c, (x, y, 1 - c)

        def copy(t, k, block, to, from_input=False):
            return pltpu.make_async_remote_copy(
                src_ref=src[t] if from_input else out[t].at[block], dst_ref=out[t].at[block],
                send_sem=send.at[t, k], recv_sem=recv.at[t, k], device_id=to, device_id_type=MESH)

        def own_copies():
            return [pltpu.make_async_copy(src[t], out[t].at[me], own_sem.at[t]) for t in range(n)]

        def first_copies():
            first = []
            for t in range(n):
                first.append(copy(t, 0, me, sibling, from_input=True))
                first += [copy(t, 1 + j, me, (*chip, c), from_input=True) for j, chip in enumerate(chips)]
            return first

        def stage(slot, block):
            return pltpu.make_async_copy(out[0].at[block], wbuf.at[slot], stage_sem.at[slot])

        @pl.when((p == 0) & (i == 0))
        def _():
            for cp in own_copies() + first_copies():
                cp.start()
            mine = pltpu.make_async_copy(src[0], wbuf.at[0], stage_sem.at[0])
            mine.start()
            for r in range(n_i):
                u_copy(r).start()
            mine.wait()

        @pl.when((p > 0) & (i == 0))
        def _():
            stage(p % 2, order_ref[p]).wait()

        for r in range(n_i):
            @pl.when((p == 0) & (i == r))
            def _(r=r):
                u_copy(r).wait()

        proj_ref[...] = _dot(ubuf[pl.ds(pl.multiple_of(i * tm, tm), tm), :], wbuf[p % 2])
        for r in range(per_shard):
            @pl.when(i == r)
            def _(r=r):
                wt_ref[...] = wbuf[p % 2, :, r * UNIT:(r + 1) * UNIT].astype(F32).T.astype(BF16)

        for nxt in range(1, N_DEV):
            @pl.when((p == nxt - 1) & (i == n_i - 1))
            def _(nxt=nxt):
                if nxt == 1:
                    block = 4 * x + 2 * y + 1 - c
                    copy(0, 0, block, sibling).wait_recv()
                else:
                    j, passed_on = divmod(nxt - 2, 2)
                    px, py = chips[j]
                    if passed_on:
                        block = 4 * px + 2 * py + 1 - c
                        copy(0, 4 + j, block, sibling).wait_recv()
                    else:
                        block = 4 * px + 2 * py + c
                        copy(0, 1 + j, block, sibling).wait_recv()
                        copy(0, 4 + j, block, sibling).start()
                stage(nxt % 2, block).start()

        @pl.when((p == N_DEV - 1) & (i == n_i - 1))
        def _():
            passed = [copy(0, 4 + j, 4 * px + 2 * py + c, sibling) for j, (px, py) in enumerate(chips)]
            for j, (px, py) in enumerate(chips):
                for t in range(1, n):
                    block = 4 * px + 2 * py + c
                    copy(t, 1 + j, block, sibling).wait_recv()
                    passed.append(copy(t, 4 + j, block, sibling))
                    passed[-1].start()
            for t in range(1, n):
                copy(t, 0, 4 * x + 2 * y + 1 - c, sibling).wait_recv()
                for j, (px, py) in enumerate(chips):
                    copy(t, 4 + j, 4 * px + 2 * py + 1 - c, sibling).wait_recv()
            for cp in first_copies() + passed:
                cp.wait_send()
            for cp in own_copies():
                cp.wait()

    return pl.pallas_call(
        body, name="proj_fwd_gather",
        grid_spec=pltpu.PrefetchScalarGridSpec(
            num_scalar_prefetch=1, grid=(N_DEV, n_i),
            in_specs=[ANY] + [ANY] * n,
            out_specs=[pl.BlockSpec((tm, W_IN_SHARD), lambda p, i, order: (i, order[p])),
                       pl.BlockSpec((UNIT, D), wt_index)] + [ANY] * n,
            scratch_shapes=[pltpu.VMEM((2, D, W_IN_SHARD), BF16), pltpu.SemaphoreType.DMA((2,)),
                            pltpu.SemaphoreType.DMA((n, 7)), pltpu.SemaphoreType.DMA((n, 7)),
                            pltpu.SemaphoreType.DMA((n,)), pltpu.VMEM((n_rows, D), BF16),
                            pltpu.SemaphoreType.DMA((n_i,))]),
        out_shape=[jax.ShapeDtypeStruct((n_rows, N_GROUPS * D), F32), jax.ShapeDtypeStruct((N_GROUPS * D, D), BF16)]
        + [jax.ShapeDtypeStruct((N_DEV, *b.shape), b.dtype) for b in blocks],
        compiler_params=_params(2),
    )(jnp.stack(_arrival_order()).astype(jnp.int32), u16, *blocks)


def _branch_a_fwd(proj, wo4_g, cw_a, tm):
    n_rows = proj.shape[0]

    def body(bp, cp, xp, za, cph, xph, w_ref, cw_ref, sa_ref, ya_ref, e_scr, o_scr):
        i = pl.program_id(0)
        _to_time_major(e_scr, 0, jnp.where(i > 0, _f32(cph) * _f32(xph), 0.0))
        for r0 in range(0, tm, ELEM_ROWS):
            rows = pl.ds(r0, ELEM_ROWS)
            _to_time_major(e_scr, HALO + r0, cp[rows, :].astype(F32) * xp[rows, :].astype(F32))
        _conv(o_scr, e_scr, cw_ref, FWD_TAPS_A, tm)
        for r0 in range(0, tm, ELEM_ROWS):
            rows = pl.ds(r0, ELEM_ROWS)
            ca = _from_time_major(o_scr, ELEM_ROWS, r0)
            sa_ref[rows, :] = (jax.nn.silu(za[rows, :].astype(F32)) * (bp[rows, :].astype(F32) * ca)).astype(BF16)
        ya_ref[...] = _dot(sa_ref[...], w_ref[...].reshape(D, D))

    return pl.pallas_call(
        body, name="branch_a_fwd", grid=(n_rows // tm,),
        in_specs=[_rows(tm, G_BA), _rows(tm, G_CA), _rows(tm, G_XA), _rows(tm, G_ZA),
                  _prev_halo(tm, G_CA), _prev_halo(tm, G_XA), _w_out_spec(0), _const(cw_a.shape)],
        out_specs=[_rows(tm), _rows(tm)],
        out_shape=[jax.ShapeDtypeStruct((n_rows, D), BF16), jax.ShapeDtypeStruct((n_rows, D), F32)],
        scratch_shapes=[_time_major(tm + HALO), _time_major(tm)],
        compiler_params=_params(1),
    )(proj, proj, proj, proj, proj, proj, wo4_g, cw_a)


def _layernorm_parts(cb, lg, lb):
    xc = cb - _mean(cb)
    rstd = lax.rsqrt(_mean(xc * xc) + EPS)
    xhat = xc * rstd
    return xhat, rstd, xhat * lg + lb


def _branch_b_fwd(proj, wo4_g, cw_b, conv_b_b, ln_g, ln_b, tm):
    n_rows = proj.shape[0]

    def body(vb, gb, zb, vbh, gbh, w_ref, cw_ref, bb_ref, lg_ref, lb_ref, cb_ref, sb_ref, yb_ref, e_scr, o_scr):
        i = pl.program_id(0)
        vbh, gbh = _f32(vbh), _f32(gbh)
        lg, lb = lg_ref[...], lb_ref[...]
        _to_time_major(e_scr, 0, jnp.where(i > 0, vbh * jax.nn.sigmoid(gbh), 0.0))
        for r0 in range(0, tm, ELEM_ROWS):
            rows = pl.ds(r0, ELEM_ROWS)
            _to_time_major(e_scr, HALO + r0, vb[rows, :].astype(F32) * jax.nn.sigmoid(gb[rows, :].astype(F32)))
        _conv(o_scr, e_scr, cw_ref, FWD_TAPS_B, tm, bias_ref=bb_ref)
        for r0 in range(0, tm, ELEM_ROWS):
            rows = pl.ds(r0, ELEM_ROWS)
            cb = _from_time_major(o_scr, ELEM_ROWS, r0)
            cb_ref[rows, :] = cb
            _, _, ln = _layernorm_parts(cb, lg, lb)
            sb_ref[rows, :] = (jax.nn.silu(zb[rows, :].astype(F32)) * jax.nn.silu(ln)).astype(BF16)
        yb_ref[...] = _dot(sb_ref[...], w_ref[...].reshape(D, D))

    return pl.pallas_call(
        body, name="branch_b_fwd", grid=(n_rows // tm,),
        in_specs=[_rows(tm, G_VB), _rows(tm, G_GB), _rows(tm, G_ZB), _prev_halo(tm, G_VB), _prev_halo(tm, G_GB),
                  _w_out_spec(1), _const(cw_b.shape), _const((LANE_GROUPS, LANES)), _const((1, D)), _const((1, D))],
        out_specs=[_rows(tm), _rows(tm), _rows(tm)],
        out_shape=[jax.ShapeDtypeStruct((n_rows, D), F32), jax.ShapeDtypeStruct((n_rows, D), BF16),
                   jax.ShapeDtypeStruct((n_rows, D), F32)],
        scratch_shapes=[_time_major(tm + HALO), _time_major(tm)],
        compiler_params=_params(1),
    )(proj, proj, proj, proj, proj, wo4_g, cw_b, conv_b_b.reshape(LANE_GROUPS, LANES), ln_g, ln_b)


def _attention(q16, kv_ref):
    probs, outs = [], []
    for h in range(N_HEADS):
        s = _dot_nt(q16[:, h * HEAD_DIM:(h + 1) * HEAD_DIM], kv_ref[h]) * (HEAD_DIM ** -0.5)
        e = jnp.exp(s - jnp.max(s, axis=-1, keepdims=True))
        p = e / jnp.sum(e, axis=-1, keepdims=True)
        probs.append(p)
        outs.append(_dot(p.astype(BF16), kv_ref[N_HEADS + h]))
    return probs, outs


def _branch_x_fwd(proj, kv16, wo4_g, tm):
    n_rows = proj.shape[0]

    def body(q, zx, kv_ref, w_ref, sx_ref, yx_ref, p_ref):
        probs, outs = _attention(q[...].astype(BF16), kv_ref)
        p_ref[...] = jnp.concatenate(probs, axis=-1)
        sx = (jax.nn.silu(_f32(zx)) * jnp.concatenate(outs, axis=-1)).astype(BF16)
        sx_ref[...] = sx
        yx_ref[...] = _dot(sx, w_ref[...].reshape(D, D))

    return pl.pallas_call(
        body, name="branch_x_fwd", grid=(n_rows // tm,),
        in_specs=[_rows(tm, G_Q), _rows(tm, G_ZX), _const(kv16.shape), _w_out_spec(2)],
        out_specs=[_rows(tm), _rows(tm), _rows(tm)],
        out_shape=[jax.ShapeDtypeStruct((n_rows, D), BF16), jax.ShapeDtypeStruct((n_rows, D), F32),
                   jax.ShapeDtypeStruct((n_rows, D), F32)],
        compiler_params=_params(1),
    )(proj, proj, kv16, wo4_g)


def _merge_fwd_bwd(proj, ya, yb, yx, x, target, wo4_g, final_g, tm):
    n_rows = proj.shape[0]
    inv_d = 1.0 / D

    def body(ga, gb, gx, ya_ref, yb_ref, yx_ref, x_ref, t_ref, w_ref, fg_ref,
             dh_ref, dya_ref, dyb_ref, dyx_ref, dp_ref, dw_ref, dfg_ref, sq_ref):
        i = pl.program_id(0)
        wo = w_ref[...].reshape(D, D)
        sig = [jax.nn.sigmoid(_f32(g)) for g in (ga, gb, gx)]
        ys = [ya_ref[...], yb_ref[...], yx_ref[...]]
        m16 = (sig[0] * ys[0] + sig[1] * ys[1] + sig[2] * ys[2]).astype(BF16)
        h = x_ref[...] + _dot(m16, wo)
        r = lax.rsqrt(_mean(h * h) + EPS)
        hn = h * r
        fg = fg_ref[...]
        err = hn * fg - t_ref[...]
        dy = err * inv_d
        dhn = dy * fg
        dh = r * (dhn - hn * _mean(dhn * hn))
        dh_ref[...] = dh
        dh16 = dh.astype(BF16)
        dm = _dot_nt(dh16, wo)
        for n, out in enumerate((dya_ref, dyb_ref, dyx_ref)):
            out[...] = (sig[n] * dm).astype(BF16)
            dp_ref[:, n * D:(n + 1) * D] = (dm * ys[n] * (sig[n] * (1.0 - sig[n]))).astype(BF16)

        @pl.when(i == 0)
        def _():
            dw_ref[...] = jnp.zeros_like(dw_ref)
            dfg_ref[...] = jnp.zeros_like(dfg_ref)
            sq_ref[...] = jnp.zeros_like(sq_ref)

        dw_ref[0] += _dot_tn(m16, dh16)
        dfg_ref[...] += _fold8(dy * hn)
        sq_ref[...] += _fold8(err * err)

    vec = jax.ShapeDtypeStruct((SUBLANES, D), F32)
    return pl.pallas_call(
        body, name="merge_fwd_bwd", grid=(n_rows // tm,),
        in_specs=[_rows(tm, G_GA), _rows(tm, G_GBB), _rows(tm, G_GX), _rows(tm), _rows(tm), _rows(tm), _rows(tm),
                  _rows(tm), _w_out_spec(3), _const((1, D))],
        out_specs=[_rows(tm), _rows(tm), _rows(tm), _rows(tm), pl.BlockSpec((tm, 3 * D), lambda i: (i, 3)),
                   pl.BlockSpec((1, D, D), lambda i: (3, 0, 0)), _const((SUBLANES, D)), _const((SUBLANES, D))],
        out_shape=[jax.ShapeDtypeStruct((n_rows, D), F32), jax.ShapeDtypeStruct((n_rows, D), BF16),
                   jax.ShapeDtypeStruct((n_rows, D), BF16), jax.ShapeDtypeStruct((n_rows, D), BF16),
                   jax.ShapeDtypeStruct((n_rows, N_GROUPS * D), BF16), jax.ShapeDtypeStruct((4, D, D), F32), vec, vec],
        compiler_params=_params(1),
    )(proj, proj, proj, ya, yb, yx, x, target, wo4_g, final_g)


def _branch_a_bwd(dya, proj, sa16, wo4_g, cw_a, dproj, dw4, tm):
    n_rows = proj.shape[0]
    n_tiles = n_rows // tm

    def body(dya_ref, bp, cp, xp, za, sa_ref, dyan, bpn, zan, cph, xph, w_ref, cw_ref, dp_in, dw_in,
             dp_ref, dw_ref, dwa_ref, e1, e2, o_scr, mm_scr):
        del dp_in, dw_in
        i = pl.program_id(0)
        bpn, zan, cph, xph = (_f32(r) for r in (bpn, zan, cph, xph))
        woa = w_ref[...].reshape(D, D)
        dya16 = dya_ref[...]
        chunks = [pl.ds(r0, ELEM_ROWS) for r0 in range(0, tm, ELEM_ROWS)]
        _to_time_major(e1, 0, jnp.where(i > 0, cph * xph, 0.0))
        for r0, rows in zip(range(0, tm, ELEM_ROWS), chunks):
            _to_time_major(e1, HALO + r0, cp[rows, :].astype(F32) * xp[rows, :].astype(F32))
        _conv(o_scr, e1, cw_ref, FWD_TAPS_A, tm)
        mm_scr[...] = _dot_nt(dya16, woa)
        for r0, rows in zip(range(0, tm, ELEM_ROWS), chunks):
            ca = _from_time_major(o_scr, ELEM_ROWS, r0)
            dsa = mm_scr[rows, :]
            b = bp[rows, :].astype(F32)
            silu_z, dsilu_z = _silu_and_grad(za[rows, :].astype(F32))
            t = dsa * silu_z
            dp_ref[rows, 0 * D:1 * D] = (t * ca).astype(BF16)
            dp_ref[rows, 3 * D:4 * D] = (dsa * (b * ca) * dsilu_z).astype(BF16)
            _to_time_major(e2, r0, t * b)
        dcan = (_dot_nt(dyan[...], woa) * jax.nn.silu(zan)) * bpn
        _to_time_major(e2, tm, jnp.where(i < n_tiles - 1, dcan, 0.0))

        @pl.when(i == 0)
        def _():
            dw_ref[...] = jnp.zeros_like(dw_ref)
            dwa_ref[...] = jnp.zeros_like(dwa_ref)

        _conv_wgrad(dwa_ref, e2, e1, FWD_TAPS_A, tm)
        dw_ref[0] += _dot_tn(sa_ref[...], dya16)
        _conv(o_scr, e2, cw_ref, BWD_TAPS_A, tm)
        for r0, rows in zip(range(0, tm, ELEM_ROWS), chunks):
            dprod = _from_time_major(o_scr, ELEM_ROWS, r0)
            dp_ref[rows, 1 * D:2 * D] = (dprod * xp[rows, :].astype(F32)).astype(BF16)
            dp_ref[rows, 2 * D:3 * D] = (dprod * cp[rows, :].astype(F32)).astype(BF16)

    return pl.pallas_call(
        body, name="branch_a_bwd", grid=(n_tiles,),
        in_specs=[_rows(tm), _rows(tm, G_BA), _rows(tm, G_CA), _rows(tm, G_XA), _rows(tm, G_ZA), _rows(tm),
                  _next_halo(tm, n_rows), _next_halo(tm, n_rows, G_BA), _next_halo(tm, n_rows, G_ZA),
                  _prev_halo(tm, G_CA), _prev_halo(tm, G_XA), _w_out_spec(0), _const(cw_a.shape), ANY, ANY],
        out_specs=[pl.BlockSpec((tm, 4 * D), lambda i: (i, 0)), pl.BlockSpec((1, D, D), lambda i: (0, 0, 0)),
                   _const((K_A * LANE_GROUPS, LANES))],
        out_shape=[jax.ShapeDtypeStruct(dproj.shape, BF16), jax.ShapeDtypeStruct(dw4.shape, F32),
                   jax.ShapeDtypeStruct((K_A * LANE_GROUPS, LANES), F32)],
        input_output_aliases={13: 0, 14: 1},
        scratch_shapes=[_time_major(tm + HALO), _time_major(tm + HALO), _time_major(tm), pltpu.VMEM((tm, D), F32)],
        compiler_params=_params(1),
    )(dya, proj, proj, proj, proj, sa16, dya, proj, proj, proj, proj, wo4_g, cw_a, dproj, dw4)


def _branch_b_bwd(dyb, proj, cb, sb16, wo4_g, cw_b, ln_g, ln_b, dproj, dw4, tm):
    n_rows = proj.shape[0]
    n_tiles = n_rows // tm

    def body(dyb_ref, zb, cb_ref, vb, gb, sb_ref, dybn, zbn, cbn, vbh, gbh, w_ref, cw_ref, lg_ref, lb_ref,
             dp_in, dw_in, dp_ref, dw_ref, dwb_ref, dbb_ref, dlg_ref, dlb_ref, e1, e2, o_scr, mm_scr):
        del dp_in, dw_in
        zbn, vbh, gbh = (_f32(r) for r in (zbn, vbh, gbh))
        i = pl.program_id(0)
        wob = w_ref[...].reshape(D, D)
        lg, lb = lg_ref[...], lb_ref[...]

        def conv_out_grad(dsb, z, c):
            xhat, rstd, ln = _layernorm_parts(c, lg, lb)
            sw, dsw = _silu_and_grad(ln)
            sz, dsz = _silu_and_grad(z)
            dln = (dsb * sz) * dsw
            dxhat = dln * lg
            dcb = rstd * (dxhat - _mean(dxhat) - xhat * _mean(dxhat * xhat))
            return dsb * sw * dsz, dln, xhat, dcb

        @pl.when(i == 0)
        def _():
            dw_ref[...] = jnp.zeros_like(dw_ref)
            dwb_ref[...] = jnp.zeros_like(dwb_ref)
            dbb_ref[...] = jnp.zeros_like(dbb_ref)
            dlg_ref[...] = jnp.zeros_like(dlg_ref)
            dlb_ref[...] = jnp.zeros_like(dlb_ref)

        dyb16 = dyb_ref[...]
        mm_scr[...] = _dot_nt(dyb16, wob)
        dlg, dlb, dbb = (jnp.zeros((SUBLANES, D), F32),) * 3
        for r0 in range(0, tm, ELEM_ROWS):
            rows = pl.ds(r0, ELEM_ROWS)
            dzb, dln, xhat, dcb = conv_out_grad(mm_scr[rows, :], zb[rows, :].astype(F32), cb_ref[rows, :])
            dp_ref[rows, 2 * D:3 * D] = dzb.astype(BF16)
            _to_time_major(e2, r0, dcb)
            dlg, dlb, dbb = dlg + _fold8(dln * xhat), dlb + _fold8(dln), dbb + _fold8(dcb)
        _, _, _, dcbn = conv_out_grad(_dot_nt(dybn[...], wob), zbn[...], cbn[...])
        _to_time_major(e2, tm, jnp.where(i < n_tiles - 1, dcbn, 0.0))
        dlg_ref[...] += dlg
        dlb_ref[...] += dlb
        dbb_ref[...] += dbb
        dw_ref[0] += _dot_tn(sb_ref[...], dyb16)
        _to_time_major(e1, 0, jnp.where(i > 0, vbh[...] * jax.nn.sigmoid(gbh[...]), 0.0))
        for r0 in range(0, tm, ELEM_ROWS):
            rows = pl.ds(r0, ELEM_ROWS)
            sg = jax.nn.sigmoid(gb[rows, :].astype(F32))
            mm_scr[rows, :] = sg
            _to_time_major(e1, HALO + r0, vb[rows, :].astype(F32) * sg)
        _conv_wgrad(dwb_ref, e2, e1, FWD_TAPS_B, tm)
        _conv(o_scr, e2, cw_ref, BWD_TAPS_B, tm)
        for r0 in range(0, tm, ELEM_ROWS):
            rows = pl.ds(r0, ELEM_ROWS)
            dglu = _from_time_major(o_scr, ELEM_ROWS, r0)
            sg = mm_scr[rows, :]
            dp_ref[rows, 0 * D:1 * D] = (dglu * sg).astype(BF16)
            dp_ref[rows, 1 * D:2 * D] = (dglu * vb[rows, :].astype(F32) * (sg * (1.0 - sg))).astype(BF16)

    vec = jax.ShapeDtypeStruct((SUBLANES, D), F32)
    return pl.pallas_call(
        body, name="branch_b_bwd", grid=(n_tiles,),
        in_specs=[_rows(tm), _rows(tm, G_ZB), _rows(tm), _rows(tm, G_VB), _rows(tm, G_GB), _rows(tm),
                  _next_halo(tm, n_rows), _next_halo(tm, n_rows, G_ZB), _next_halo(tm, n_rows),
                  _prev_halo(tm, G_VB), _prev_halo(tm, G_GB), _w_out_spec(1), _const(cw_b.shape), _const((1, D)),
                  _const((1, D)), ANY, ANY],
        out_specs=[pl.BlockSpec((tm, 3 * D), lambda i: (i, 2)), pl.BlockSpec((1, D, D), lambda i: (1, 0, 0)),
                   _const((K_B * LANE_GROUPS, LANES)), _const((SUBLANES, D)), _const((SUBLANES, D)),
                   _const((SUBLANES, D))],
        out_shape=[jax.ShapeDtypeStruct(dproj.shape, BF16), jax.ShapeDtypeStruct(dw4.shape, F32),
                   jax.ShapeDtypeStruct((K_B * LANE_GROUPS, LANES), F32), vec, vec, vec],
        input_output_aliases={15: 0, 16: 1},
        scratch_shapes=[_time_major(tm + HALO), _time_major(tm + HALO), _time_major(tm), pltpu.VMEM((tm, D), F32)],
        compiler_params=_params(1),
    )(dyb, proj, cb, proj, proj, sb16, dyb, proj, cb, proj, proj, wo4_g, cw_b, ln_g, ln_b, dproj, dw4)


def _branch_x_bwd(dyx, proj, sx16, probs, kv16, wo4_g, dproj, dw4, tm):
    n_rows = proj.shape[0]
    scale = HEAD_DIM ** -0.5

    def body(dyx_ref, q, zx, sx_ref, p_ref, kv_ref, w_ref, dp_in, dw_in, dp_ref, dw_ref, dkv_ref):
        del dp_in, dw_in
        i = pl.program_id(0)
        dyx16 = dyx_ref[...]
        q16 = q[...].astype(BF16)
        probs = [p_ref[:, h * HEAD_DIM:(h + 1) * HEAD_DIM] for h in range(N_HEADS)]
        outs = [_dot(probs[h].astype(BF16), kv_ref[N_HEADS + h]) for h in range(N_HEADS)]
        dsx = _dot_nt(dyx16, w_ref[...].reshape(D, D))
        silu_z, dsilu_z = _silu_and_grad(_f32(zx))
        dp_ref[:, D:2 * D] = (dsx * jnp.concatenate(outs, axis=-1) * dsilu_z).astype(BF16)
        do16 = (dsx * silu_z).astype(BF16)

        @pl.when(i == 0)
        def _():
            dw_ref[...] = jnp.zeros_like(dw_ref)
            dkv_ref[...] = jnp.zeros_like(dkv_ref)

        for h in range(N_HEADS):
            cols = slice(h * HEAD_DIM, (h + 1) * HEAD_DIM)
            p = probs[h]
            dprob = _dot_nt(do16[:, cols], kv_ref[N_HEADS + h])
            ds16 = ((p * (dprob - jnp.sum(p * dprob, axis=-1, keepdims=True))) * scale).astype(BF16)
            dp_ref[:, cols] = _dot(ds16, kv_ref[h]).astype(BF16)
            dkv_ref[h] += _dot_tn(ds16, q16[:, cols])
            dkv_ref[N_HEADS + h] += _dot_tn(p.astype(BF16), do16[:, cols])
        dw_ref[0] += _dot_tn(sx_ref[...], dyx16)

    return pl.pallas_call(
        body, name="branch_x_bwd", grid=(n_rows // tm,),
        in_specs=[_rows(tm), _rows(tm, G_Q), _rows(tm, G_ZX), _rows(tm), _rows(tm), _const(kv16.shape),
                  _w_out_spec(2), ANY, ANY],
        out_specs=[pl.BlockSpec((tm, 2 * D), lambda i: (i, 2)), pl.BlockSpec((1, D, D), lambda i: (2, 0, 0)),
                   _const(kv16.shape)],
        out_shape=[jax.ShapeDtypeStruct(dproj.shape, BF16), jax.ShapeDtypeStruct(dw4.shape, F32),
                   jax.ShapeDtypeStruct(kv16.shape, F32)],
        input_output_aliases={7: 0, 8: 1},
        compiler_params=_params(1),
    )(dyx, proj, proj, sx16, probs, kv16, wo4_g, dproj, dw4)


def _dp_unit(u):
    g = u // 2
    pos = jnp.where(g < G_VB, g, jnp.where(g < G_Q, g + 2, jnp.where(g < G_GA, g - 3, g)))
    return 2 * pos + u % 2


def _scatter_copies(srcs, lands, send, recv):
    x, y, c = lax.axis_index("x"), lax.axis_index("y"), lax.axis_index("c")
    copies = []
    for n in range(N_DEV - 1):
        flip = n + 1
        px = 1 - x if flip & 4 else x
        py = 1 - y if flip & 2 else y
        pc = 1 - c if flip & 1 else c
        for t, (src, land) in enumerate(zip(srcs, lands)):
            copies.append(pltpu.make_async_remote_copy(
                src_ref=src.at[4 * px + 2 * py + pc], dst_ref=land.at[n], send_sem=send.at[t * (N_DEV - 1) + n],
                recv_sem=recv.at[t * (N_DEV - 1) + n], device_id=(px, py, pc), device_id_type=MESH))
    return copies


def _scatter_start(name, arrays, views, n_views):
    n = len(arrays)
    lands = [lax.empty(tuple(N_DEV - 1 if d == N_DEV else d for d in a.shape), a.dtype) for a in arrays]

    def body(*refs):
        src, land, (send, recv) = refs[:n], refs[n:2 * n], refs[2 * n:2 * n + 2]
        token = refs[-1]
        for cp in _scatter_copies(views(src), views(land), send, recv):
            cp.start()
        token[...] = jnp.zeros_like(token)

    sems = pltpu.SemaphoreType.DMA((n_views * (N_DEV - 1),))
    out = pl.pallas_call(
        body, name=name,
        in_specs=[HBM] * (2 * n),
        out_specs=[SEM, SEM] + [HBM] * (2 * n) + [pl.BlockSpec(memory_space=pltpu.VMEM)],
        out_shape=[sems, sems] + [pltpu.HBM(a.shape, a.dtype) for a in arrays + lands]
        + [jax.ShapeDtypeStruct((SUBLANES, LANES), F32)],
        input_output_aliases={k: 2 + k for k in range(2 * n)},
        compiler_params=pltpu.CompilerParams(has_side_effects=SIDE_EFFECT),
    )(*[pltpu.with_memory_space_constraint(a, pltpu.HBM) for a in arrays + lands])
    return dict(name=name, sems=out[:2], moving=out[2:2 + 2 * n], views=views, token=out[-1])


def _scatter_wait(started, after):
    n = len(started["moving"]) // 2
    views = started["views"]

    def body(*refs):
        src, land, (send, recv) = refs[:n], refs[n:2 * n], refs[2 * n:2 * n + 2]
        for cp in _scatter_copies(views(src), views(land), send, recv):
            cp.wait_send()
            cp.wait_recv()

    out = pl.pallas_call(
        body, name=started["name"].replace("start", "wait"),
        in_specs=[HBM] * (2 * n) + [SEM, SEM, ANY],
        out_specs=[HBM] * (2 * n),
        out_shape=[pltpu.HBM(a.shape, a.dtype) for a in started["moving"]],
        input_output_aliases={k: k for k in range(2 * n)},
        compiler_params=pltpu.CompilerParams(has_side_effects=SIDE_EFFECT),
    )(*started["moving"], *started["sems"], after)
    return out[n:]


def _w_in_grad(ut, dproj, tk, token):
    n_rows = dproj.shape[0]
    n_k = n_rows // tk
    per_shard = W_IN_SHARD // UNIT

    def body(ut_ref, dp0, dp1, dp2, token_ref, out_ref, out16_ref, acc):
        del token_ref
        t = pl.program_id(1)

        for r, dp_ref in enumerate((dp0, dp1, dp2)):
            cols = slice(r * UNIT, (r + 1) * UNIT)

            @pl.when(t == 0)
            def _(dp_ref=dp_ref, cols=cols):
                acc[:, cols] = _dot(ut_ref[...], dp_ref[...])

            @pl.when(t > 0)
            def _(dp_ref=dp_ref, cols=cols):
                acc[:, cols] += _dot(ut_ref[...], dp_ref[...])

        @pl.when(t == n_k - 1)
        def _():
            out_ref[0] = acc[...]
            out16_ref[0] = acc[...].astype(BF16)

    def dp_spec(r):
        return pl.BlockSpec((tk, UNIT), lambda q, t: (t, _dp_unit(per_shard * q + r)))

    shard = pl.BlockSpec((1, D, W_IN_SHARD), lambda q, t: (q, 0, 0))
    return pl.pallas_call(
        body, name="w_in_grad", grid=(N_DEV, n_k),
        in_specs=[pl.BlockSpec((D, tk), lambda q, t: (0, t)), dp_spec(0), dp_spec(1), dp_spec(2), ANY],
        out_specs=[shard, shard],
        out_shape=[jax.ShapeDtypeStruct((N_DEV, D, W_IN_SHARD), F32), jax.ShapeDtypeStruct((N_DEV, D, W_IN_SHARD), BF16)],
        scratch_shapes=[pltpu.VMEM((D, W_IN_SHARD), F32)],
        compiler_params=_params(2),
    )(ut, dproj, dproj, dproj, token)


def _x_grad(dproj, win_t, x, dh, norm_g, token, tm):
    n_rows = x.shape[0]
    n_k = N_GROUPS * D // X_GRAD_K

    def body(dp_ref, wt_ref, x_ref, dh_ref, g_ref, token_ref, gx_ref, dg_ref, acc):
        del token_ref
        i, g = pl.program_id(0), pl.program_id(1)

        @pl.when((i == 0) & (g == 0))
        def _():
            dg_ref[...] = jnp.zeros_like(dg_ref)

        @pl.when(g == 0)
        def _():
            acc[...] = _dot(dp_ref[...], wt_ref[...])

        @pl.when(g > 0)
        def _():
            acc[...] += _dot(dp_ref[...], wt_ref[...])

        @pl.when(g == n_k - 1)
        def _():
            du = acc[...]
            xf = x_ref[...]
            r = lax.rsqrt(_mean(xf * xf) + EPS)
            xn = xf * r
            dun = du * g_ref[...]
            gx_ref[...] = dh_ref[...] + r * (dun - xn * _mean(dun * xn))
            dg_ref[...] += _fold8(du * xn)

    return pl.pallas_call(
        body, name="x_grad", grid=(n_rows // tm, n_k),
        in_specs=[pl.BlockSpec((tm, X_GRAD_K), lambda i, g: (i, g)), pl.BlockSpec((X_GRAD_K, D), lambda i, g: (g, 0)),
                  pl.BlockSpec((tm, D), lambda i, g: (i, 0)), pl.BlockSpec((tm, D), lambda i, g: (i, 0)),
                  _const((1, D)), ANY],
        out_specs=[pl.BlockSpec((tm, D), lambda i, g: (i, 0)), _const((SUBLANES, D))],
        out_shape=[jax.ShapeDtypeStruct((n_rows, D), F32), jax.ShapeDtypeStruct((SUBLANES, D), F32)],
        scratch_shapes=[pltpu.VMEM((tm, D), F32)],
        compiler_params=_params(2),
    )(dproj, win_t, x, dh, norm_g, token)


def _local_step(x, mem, target, norm_g, conv_b_b, ln_g, ln_b, mem_g, final_g, shards):
    n_rows = x.shape[0]
    tm = min(512, n_rows)
    big = min(1024, n_rows)
    u16, ut = _rmsnorm_fwd(x, norm_g, big)
    proj, win_t, _, wkv_g, wo4_g, cw_g = _proj_fwd_gather(u16, shards, min(2048, n_rows))
    cw_rows = cw_g.transpose(1, 0, 2).reshape((SUBLANES + HALO) * LANE_GROUPS, LANES)
    cw_a, cw_b = cw_rows[:SUBLANES * LANE_GROUPS], cw_rows[SUBLANES * LANE_GROUPS:]
    kv16, mn16 = _kv_fwd(mem, mem_g, wkv_g)
    sa16, ya = _branch_a_fwd(proj, wo4_g, cw_a, big)
    cb, sb16, yb = _branch_b_fwd(proj, wo4_g, cw_b, conv_b_b, ln_g, ln_b, big)
    sx16, yx, probs = _branch_x_fwd(proj, kv16, wo4_g, big)
    dh, dya, dyb, dyx, dproj, dw4, dfg, sq = _merge_fwd_bwd(proj, ya, yb, yx, x, target, wo4_g, final_g,
                                                             min(256, n_rows))
    dproj, dw4, dwa = _branch_a_bwd(dya, proj, sa16, wo4_g, cw_a, dproj, dw4, tm)
    dproj, dw4, dwb, dbb, dlg, dlb = _branch_b_bwd(dyb, proj, cb, sb16, wo4_g, cw_b, ln_g, ln_b, dproj, dw4, tm)
    dproj, dw4, dkv = _branch_x_bwd(dyx, proj, sx16, probs, kv16, wo4_g, dproj, dw4, tm)
    dwkv_g, dwkv16, dmg = _kv_bwd(dkv, mem, mem_g, mn16, wkv_g)
    dw4 = dw4.reshape(4, N_DEV, D // N_DEV, D)
    small_moving = _scatter_start("small_grads_start", [dw4.astype(BF16), dwkv16],
                                  lambda refs: [refs[0].at[w] for w in range(4)] + [refs[1]], 5)
    dwin_g, dwin16 = _w_in_grad(ut, dproj, min(2048, n_rows), small_moving["token"])
    w_in_moving = _scatter_start("w_in_grad_start", [dwin16], lambda refs: list(refs), 1)
    gx, dng = _x_grad(dproj, win_t, x, dh, norm_g, w_in_moving["token"], big)
    land4, landkv = _scatter_wait(small_moving, dng)
    landin, = _scatter_wait(w_in_moving, dng)
    small = {SV_NORM_G: dng, SV_CONV_B_B: dbb, SV_LN_G: dlg, SV_LN_B: dlb, SV_MEM_G: dmg, SV_FINAL_G: dfg, SV_LOSS: sq}
    grads = [(dwin_g[None], landin[None]), (dw4, land4), (dwkv_g[None], landkv[None])]
    return gx, grads, small, dwa.reshape(K_A, D), dwb.reshape(K_B, D)


def _allgather_small(small, conv_rows):
    keys = sorted(small)

    def body(*refs):
        parts, (conv_ref, out_ref, mine, send, recv) = refs[:len(keys)], refs[len(keys):]
        x, y, c, chips = _place()
        me, sibling = 4 * x + 2 * y + c, (x, y, 1 - c)
        mine[pl.ds(0, SV_CONV_A), :] = jnp.zeros((SV_CONV_A, D), F32)
        for key, part in zip(keys, parts):
            mine[key:key + 1, :] = jnp.sum(part[...], axis=0, keepdims=True)
        mine[pl.ds(SV_CONV_A, SV_ROWS - SV_CONV_A), :] = conv_ref[...]
        out_ref[me] = mine[...]

        def copy(k, block, to, from_mine=False):
            return pltpu.make_async_remote_copy(
                src_ref=mine if from_mine else out_ref.at[block], dst_ref=out_ref.at[block],
                send_sem=send.at[k], recv_sem=recv.at[k], device_id=to, device_id_type=MESH)

        first = [copy(0, me, sibling, from_mine=True)]
        first += [copy(1 + j, me, (*chip, c), from_mine=True) for j, chip in enumerate(chips)]
        for cp in first:
            cp.start()
        passed = []
        for j, (px, py) in enumerate(chips):
            block = 4 * px + 2 * py + c
            copy(1 + j, block, sibling).wait_recv()
            passed.append(copy(4 + j, block, sibling))
            passed[-1].start()
        copy(0, 4 * x + 2 * y + 1 - c, sibling).wait_recv()
        for j, (px, py) in enumerate(chips):
            copy(4 + j, 4 * px + 2 * py + 1 - c, sibling).wait_recv()
        for cp in first + passed:
            cp.wait_send()

    vmem = pl.BlockSpec(memory_space=pltpu.VMEM)
    return pl.pallas_call(
        body, name="allgather_small",
        in_specs=[vmem] * (len(keys) + 1), out_specs=vmem,
        out_shape=jax.ShapeDtypeStruct((N_DEV, SV_ROWS, D), F32),
        scratch_shapes=[pltpu.VMEM((SV_ROWS, D), F32), pltpu.SemaphoreType.DMA((7,)), pltpu.SemaphoreType.DMA((7,))],
    )(*[small[k] for k in keys], conv_rows)


def _adamw(w, g, m, v):
    m = ADAM_B1 * m + (1.0 - ADAM_B1) * g
    v = ADAM_B2 * v + (1.0 - ADAM_B2) * (g * g)
    m_hat = m / (1.0 - ADAM_B1 ** ADAM_STEP)
    v_hat = v / (1.0 - ADAM_B2 ** ADAM_STEP)
    return -ADAM_LR * (m_hat / (jnp.sqrt(v_hat) + ADAM_EPS) + ADAM_WD * w), m, v


def _adamw_shard(own, landed, piece, k_arr, w, m, v, tr):
    n_r, n_c = w.shape
    n_landed = landed.shape[1]

    def body(k_ref, own_ref, *refs):
        del k_ref
        landed_refs, (w_ref, m_ref, v_ref, g_out, d_out, m_out, v_out) = refs[:n_landed], refs[n_landed:]
        g = own_ref[0, 0]
        for landed_ref in landed_refs:
            g = g + landed_ref[0, 0].astype(F32)
        g_out[...] = g
        d_out[...], m_out[...], v_out[...] = _adamw(w_ref[...], g, m_ref[...], v_ref[...])

    blk = (1, 1, tr, n_c)
    flat = pl.BlockSpec((tr, n_c), lambda r, k: (r, 0))
    return pl.pallas_call(
        body, name="adamw_shard",
        grid_spec=pltpu.PrefetchScalarGridSpec(
            num_scalar_prefetch=1, grid=(n_r // tr,),
            in_specs=[pl.BlockSpec(blk, lambda r, k: (piece, k[0], r, 0))]
            + [pl.BlockSpec(blk, functools.partial(lambda r, k, j: (piece, j, r, 0), j=j)) for j in range(n_landed)]
            + [flat] * 3,
            out_specs=[flat] * 4),
        out_shape=[jax.ShapeDtypeStruct((n_r, n_c), F32)] * 4,
        compiler_params=_params(1),
    )(k_arr, own, *([landed] * n_landed), w, m, v)


def _adamw_shards(entries, k_arr):
    n = len(entries)

    def body(k_ref, *refs):
        del k_ref
        ins, outs = refs[:5 * n], refs[5 * n:]
        for e in range(n):
            own_ref, landed_ref, w_ref, m_ref, v_ref = ins[5 * e:5 * e + 5]
            g = own_ref[0, 0]
            for j in range(landed_ref.shape[1]):
                g = g + landed_ref[0, j].astype(F32)
            g_out, d_out, m_out, v_out = outs[4 * e:4 * e + 4]
            g_out[...] = g
            d_out[...], m_out[...], v_out[...] = _adamw(w_ref[...], g, m_ref[...], v_ref[...])

    in_specs, operands, out_specs, out_shape = [], [], [], []
    for own, landed, piece, w, m, v in entries:
        flat = pl.BlockSpec(w.shape, lambda i, k: (0, 0))
        in_specs += [pl.BlockSpec((1, 1, *w.shape), functools.partial(lambda i, k, p: (p, k[0], 0, 0), p=piece)),
                     pl.BlockSpec((1, *landed.shape[1:]), functools.partial(lambda i, k, p: (p, 0, 0, 0), p=piece)),
                     flat, flat, flat]
        operands += [own, landed, w, m, v]
        out_specs += [flat] * 4
        out_shape += [jax.ShapeDtypeStruct(w.shape, F32)] * 4
    out = pl.pallas_call(
        body, name="adamw_shards",
        grid_spec=pltpu.PrefetchScalarGridSpec(num_scalar_prefetch=1, grid=(1,), in_specs=in_specs,
                                               out_specs=out_specs),
        out_shape=out_shape,
        compiler_params=_params(1),
    )(k_arr, *operands)
    return [tuple(out[4 * e:4 * e + 4]) for e in range(n)]


def _adamw_small(gathered, k_arr, vectors, conv_a, conv_b):
    n_vec = len(vectors)
    cols = D // N_DEV

    def body(k_ref, full_ref, cols_ref, *refs):
        del k_ref
        ins, outs = refs[:3 * (n_vec + 2)], refs[3 * (n_vec + 2):-2]
        tot, tot_cols = refs[-2:]
        tot[...] = full_ref[0]
        tot_cols[...] = cols_ref[0]
        for dev in range(1, N_DEV):
            tot[...] += full_ref[dev]
            tot_cols[...] += cols_ref[dev]
        loss = (0.5 / D) * jnp.sum(tot[SV_LOSS:SV_LOSS + 1, :])
        outs[0][...] = jnp.full(outs[0].shape, loss, F32)
        grads = [tot[n:n + 1, :] for n in range(n_vec)]
        grads += [tot_cols[pl.ds(SV_CONV_A, K_A), :], tot_cols[pl.ds(SV_CONV_B, K_B), :]]
        for n, g in enumerate(grads):
            w_ref, m_ref, v_ref = ins[3 * n:3 * n + 3]
            g_out, d_out, m_out, v_out = outs[1 + 4 * n:5 + 4 * n]
            g_out[...] = g
            d_out[...], m_out[...], v_out[...] = _adamw(w_ref[...], g, m_ref[...], v_ref[...])

    weights = list(vectors) + [conv_a, conv_b]
    flat_in = [a for wmv in weights for a in wmv]
    out_shape = [jax.ShapeDtypeStruct((SUBLANES, 128), F32)]
    for wmv in weights:
        out_shape += [jax.ShapeDtypeStruct(wmv[0].shape, F32)] * 4
    return pl.pallas_call(
        body, name="adamw_small",
        grid_spec=pltpu.PrefetchScalarGridSpec(
            num_scalar_prefetch=1, grid=(1,),
            in_specs=[pl.BlockSpec((N_DEV, SV_ROWS, D), lambda i, k: (0, 0, 0)),
                      pl.BlockSpec((N_DEV, SV_ROWS, cols), lambda i, k: (0, 0, k[0]))]
            + [pl.BlockSpec(a.shape, lambda i, k: (0, 0)) for a in flat_in],
            out_specs=[pl.BlockSpec(s.shape, lambda i, k: (0, 0)) for s in out_shape],
            scratch_shapes=[pltpu.VMEM((SV_ROWS, D), F32), pltpu.VMEM((SV_ROWS, cols), F32)]),
        out_shape=out_shape,
        compiler_params=_params(1),
    )(k_arr, gathered, gathered, *flat_in)


def kernel(x, mem, norm_g, w_in, conv_a_w, w_out_a, conv_b_w, conv_b_b, ln_b_g, ln_b_b, w_out_b, mem_norm_g, w_kv, w_out_x, w_o, final_g, loss_target, m_norm_g, m_w_in, m_conv_a_w, m_w_out_a, m_conv_b_w, m_conv_b_b, m_ln_b_g, m_ln_b_b, m_w_out_b, m_mem_norm_g, m_w_kv, m_w_out_x, m_w_o, m_final_g, v_norm_g, v_w_in, v_conv_a_w, v_w_out_a, v_conv_b_w, v_conv_b_b, v_ln_b_g, v_ln_b_b, v_w_out_b, v_mem_norm_g, v_w_kv, v_w_out_x, v_w_o, v_final_g):
    xi, yi, ci = lax.axis_index("x"), lax.axis_index("y"), lax.axis_index("c")
    k_arr = jnp.reshape(4 * xi + 2 * yi + ci, (1,)).astype(jnp.int32)

    cw = jnp.concatenate([jnp.pad(conv_a_w[0], ((0, SUBLANES - K_A), (0, 0))),
                          jnp.pad(conv_b_w[0], ((0, HALO - K_B), (0, 0)))], axis=0)
    wo4 = jnp.stack([w_out_a[0], w_out_b[0], w_out_x[0], w_o[0]]).astype(BF16)
    shards = [w_in[0].astype(BF16), w_kv[0].astype(BF16), wo4, cw]

    final_g2 = final_g.reshape(1, D)
    gx, grads, small, dwa, dwb = _local_step(
        x[0], mem[0], loss_target[0], norm_g, conv_b_b, ln_b_g, ln_b_b, mem_norm_g, final_g2, shards)

    conv_rows = jnp.concatenate([jnp.pad(dwa, ((0, SUBLANES - K_A), (0, 0))),
                                 jnp.pad(dwb, ((0, HALO - K_B), (0, 0)))], axis=0)
    gathered_small = _allgather_small(small, conv_rows)

    res = {"w_in": _adamw_shard(grads[0][0], grads[0][1], 0, k_arr, w_in[0], m_w_in[0], v_w_in[0], 256)}
    small_shards = [("w_out_a", 1, 0, w_out_a, m_w_out_a, v_w_out_a), ("w_out_b", 1, 1, w_out_b, m_w_out_b, v_w_out_b),
                    ("w_out_x", 1, 2, w_out_x, m_w_out_x, v_w_out_x), ("w_o", 1, 3, w_o, m_w_o, v_w_o),
                    ("w_kv", 2, 0, w_kv, m_w_kv, v_w_kv)]
    updated = _adamw_shards([(grads[a][0], grads[a][1], l, w[0], m[0], v[0]) for _, a, l, w, m, v in small_shards],
                            k_arr)
    res.update({name: four for (name, *_), four in zip(small_shards, updated)})
    res = {name: tuple(r[None] for r in four) for name, four in res.items()}
    vectors = [(norm_g, m_norm_g, v_norm_g), (conv_b_b, m_conv_b_b, v_conv_b_b), (ln_b_g, m_ln_b_g, v_ln_b_g),
               (ln_b_b, m_ln_b_b, v_ln_b_b), (mem_norm_g, m_mem_norm_g, v_mem_norm_g),
               (final_g2, m_final_g.reshape(1, D), v_final_g.reshape(1, D))]
    out = _adamw_small(gathered_small, k_arr, vectors, (conv_a_w[0], m_conv_a_w[0], v_conv_a_w[0]),
                       (conv_b_w[0], m_conv_b_w[0], v_conv_b_w[0]))
    loss = out[0][0, 0]
    names = ["norm_g", "conv_b_b", "ln_b_g", "ln_b_b", "mem_norm_g", "final_g", "conv_a_w", "conv_b_w"]
    for n, name in enumerate(names):
        four = out[1 + 4 * n:5 + 4 * n]
        if name == "final_g":
            four = [r.reshape(D) for r in four]
        elif name.startswith("conv_") and name.endswith("_w"):
            four = [r[None] for r in four]
        res[name] = tuple(four)

    order = ["norm_g", "w_in", "conv_a_w", "w_out_a", "conv_b_w", "conv_b_b", "ln_b_g", "ln_b_b", "w_out_b",
             "mem_norm_g", "w_kv", "w_out_x", "w_o", "final_g"]
    return (loss, gx[None], *[res[n][0] for n in order], *[res[n][1] for n in order],
            *[res[n][2] for n in order], *[res[n][3] for n in order])
```

```python
import functools

import jax
import jax.numpy as jnp
from jax import lax
from jax.experimental import pallas as pl
from jax.experimental.pallas import tpu as pltpu

F32, BF16 = jnp.float32, jnp.bfloat16
D = 1024
N_DEV = 8
N_HEADS = 4
HEAD_DIM = D // N_HEADS
N_GROUPS = 12
W_IN_SHARD = N_GROUPS * D // N_DEV
UNIT = 512
X_GRAD_K = 3 * D
K_A, K_B = 3, 31
EPS = 1e-6
HALO = 32
SUBLANES = 8
LANES = 128
LANE_GROUPS = D // LANES
TAP_GROUP = 16
CONV_BLOCK = 32
ELEM_ROWS = 16
CONV_PARTIAL_SUMS = 4
VMEM_LIMIT = 60 << 20
MESH = pl.DeviceIdType.MESH
ANY = pl.BlockSpec(memory_space=pl.ANY)
HBM = pl.BlockSpec(memory_space=pltpu.HBM)
SEM = pl.BlockSpec(memory_space=pltpu.SEMAPHORE)
SIDE_EFFECT = pltpu.SideEffectType.DATAFLOW_SIDE_EFFECTING

G_BA, G_CA, G_XA, G_ZA, G_VB, G_GB, G_ZB, G_Q, G_ZX, G_GA, G_GBB, G_GX = range(N_GROUPS)

ADAM_LR, ADAM_B1, ADAM_B2, ADAM_EPS, ADAM_WD, ADAM_STEP = 0.001, 0.9, 0.999, 1e-08, 0.01, 10

SV_NORM_G, SV_CONV_B_B, SV_LN_G, SV_LN_B, SV_MEM_G, SV_FINAL_G, SV_LOSS = range(7)
SV_CONV_A, SV_CONV_B, SV_ROWS = 8, 16, 48


def _dot(a, b):
    return jnp.dot(a, b, preferred_element_type=F32)


def _dot_nt(a, b):
    return lax.dot_general(a, b, (((1,), (1,)), ((), ())), preferred_element_type=F32)


def _dot_tn(a, b):
    return lax.dot_general(a, b, (((0,), (0,)), ((), ())), preferred_element_type=F32)


def _silu_and_grad(z):
    s = jax.nn.sigmoid(z)
    return z * s, s * (1.0 + z * (1.0 - s))


def _fold8(a):
    return a.reshape(a.shape[0] // SUBLANES, SUBLANES, a.shape[1]).sum(axis=0)


def _mean(a):
    return jnp.mean(a, axis=-1, keepdims=True)


def _f32(ref):
    return ref[...].astype(F32)


def _params(n_grid):
    return pltpu.CompilerParams(dimension_semantics=("arbitrary",) * n_grid, vmem_limit_bytes=VMEM_LIMIT)


def _rows(tm, col=0):
    return pl.BlockSpec((tm, D), lambda i: (i, col))


def _prev_halo(tm, col=0):
    return pl.BlockSpec((HALO, D), lambda i: (jnp.maximum(i * (tm // HALO) - 1, 0), col))


def _next_halo(tm, n_rows, col=0):
    last = n_rows // HALO - 1
    return pl.BlockSpec((HALO, D), lambda i: (jnp.minimum((i + 1) * (tm // HALO), last), col))


def _const(shape):
    return pl.BlockSpec(shape, lambda *_: (0,) * len(shape))


def _w_out_spec(which):
    return pl.BlockSpec((N_DEV, None, D // N_DEV, D), lambda *_: (0, which, 0, 0))


def _to_time_major(t_ref, row0, x):
    n = x.shape[0]
    for j in range(LANE_GROUPS):
        t_ref[pl.ds(row0 * LANE_GROUPS + j, n, stride=LANE_GROUPS), :] = x[:, j * LANES:(j + 1) * LANES]


def _from_time_major(t_ref, n, row0=0):
    return jnp.concatenate([t_ref[pl.ds(row0 * LANE_GROUPS + j, n, stride=LANE_GROUPS), :]
                            for j in range(LANE_GROUPS)], axis=-1)


def _row(ref, t):
    start = t * LANE_GROUPS
    if not isinstance(start, int):
        start = pl.multiple_of(start, LANE_GROUPS)
    return ref[pl.ds(start, LANE_GROUPS), :]


def _tap_groups(taps):
    return [taps[first:first + TAP_GROUP] for first in range(0, len(taps), TAP_GROUP)]


def _conv(o_ref, e_ref, w_ref, taps, n_rows, bias_ref=None):
    for n_group, group in enumerate(_tap_groups(taps)):
        weights = [_row(w_ref, k) for k, _ in group]

        def block(c, carry, n_group=n_group, group=group, weights=weights):
            t0 = c * CONV_BLOCK
            window = {}
            for t in range(CONV_BLOCK):
                parts = [None] * min(CONV_PARTIAL_SUMS, len(group))
                for n, (_, off) in enumerate(group):
                    if t + off not in window:
                        window[t + off] = _row(e_ref, t0 + t + off)
                    term = weights[n] * window[t + off]
                    parts[n % len(parts)] = term if parts[n % len(parts)] is None else parts[n % len(parts)] + term
                window.pop(t + min(off for _, off in group), None)
                while len(parts) > 1:
                    parts = [parts[n] + parts[n + 1] for n in range(0, len(parts) - 1, 2)] + parts[len(parts) & ~1:]
                out = parts[0]
                if n_group > 0:
                    out = out + _row(o_ref, t0 + t)
                elif bias_ref is not None:
                    out = out + bias_ref[...]
                o_ref[pl.ds(pl.multiple_of((t0 + t) * LANE_GROUPS, LANE_GROUPS), LANE_GROUPS), :] = out
            return carry

        lax.fori_loop(0, n_rows // CONV_BLOCK, block, 0)


def _conv_wgrad(dw_ref, d_ref, e_ref, taps, n_rows):
    for group in _tap_groups(taps):
        def block(c, sums, group=group):
            t0 = c * CONV_BLOCK
            sums = list(sums)
            window = {}
            for t in range(CONV_BLOCK):
                d = _row(d_ref, t0 + t)
                for n, (_, off) in enumerate(group):
                    if t + off not in window:
                        window[t + off] = _row(e_ref, t0 + t + off)
                    sums[n] = sums[n] + d * window[t + off]
                window.pop(t + min(off for _, off in group), None)
            return tuple(sums)

        sums = lax.fori_loop(0, n_rows // CONV_BLOCK, block, tuple(_row(dw_ref, k) for k, _ in group))
        for (k, _), total in zip(group, sums):
            dw_ref[pl.ds(k * LANE_GROUPS, LANE_GROUPS), :] = total


FWD_TAPS_A = [(k, HALO - (K_A - 1) + k) for k in range(K_A)]
BWD_TAPS_A = [(k, K_A - 1 - k) for k in range(K_A)]
FWD_TAPS_B = [(k, HALO - (K_B - 1) + k) for k in range(K_B)]
BWD_TAPS_B = [(k, K_B - 1 - k) for k in range(K_B)]


def _time_major(n_rows):
    return pltpu.VMEM((n_rows * LANE_GROUPS, LANES), F32)


def _kv_fwd(mem, mem_g, wkv_g):
    m_len = mem.shape[0]

    def body(mem_ref, g_ref, w_ref, kv_ref, mn_ref):
        mf = mem_ref[...]
        r = lax.rsqrt(_mean(mf * mf) + EPS)
        mn = ((mf * r) * g_ref[...]).astype(BF16)
        mn_ref[...] = mn
        for b in range(2 * N_HEADS):
            kv_ref[b] = _dot(mn, w_ref[b]).astype(BF16)

    return pl.pallas_call(
        body, name="kv_fwd", grid=(1,),
        in_specs=[_const((m_len, D)), _const((1, D)), _const((2 * N_HEADS, D, HEAD_DIM))],
        out_specs=[_const((2 * N_HEADS, m_len, HEAD_DIM)), _const((m_len, D))],
        out_shape=[jax.ShapeDtypeStruct((2 * N_HEADS, m_len, HEAD_DIM), BF16), jax.ShapeDtypeStruct((m_len, D), BF16)],
        compiler_params=_params(1),
    )(mem, mem_g, wkv_g)


def _kv_bwd(dkv, mem, mem_g, mn16, wkv_g):
    m_len = mem.shape[0]

    def body(dkv_ref, mem_ref, g_ref, mn_ref, w_ref, dw_ref, dw16_ref, dg_ref):
        mn = mn_ref[...]
        dmn = jnp.zeros((m_len, D), F32)
        for b in range(2 * N_HEADS):
            d16 = dkv_ref[b].astype(BF16)
            dw = _dot_tn(mn, d16)
            dw_ref[b] = dw
            dw16_ref[b] = dw.astype(BF16)
            dmn = dmn + _dot_nt(d16, w_ref[b])
        mf = mem_ref[...]
        r = lax.rsqrt(_mean(mf * mf) + EPS)
        dg_ref[...] = _fold8(dmn * (mf * r))

    return pl.pallas_call(
        body, name="kv_bwd", grid=(1,),
        in_specs=[_const((2 * N_HEADS, m_len, HEAD_DIM)), _const((m_len, D)), _const((1, D)), _const((m_len, D)),
                  _const((2 * N_HEADS, D, HEAD_DIM))],
        out_specs=[_const((2 * N_HEADS, D, HEAD_DIM)), _const((2 * N_HEADS, D, HEAD_DIM)), _const((SUBLANES, D))],
        out_shape=[jax.ShapeDtypeStruct((2 * N_HEADS, D, HEAD_DIM), F32),
                   jax.ShapeDtypeStruct((2 * N_HEADS, D, HEAD_DIM), BF16), jax.ShapeDtypeStruct((SUBLANES, D), F32)],
        compiler_params=_params(1),
    )(dkv, mem, mem_g, mn16, wkv_g)


def _rmsnorm_fwd(x, norm_g, tm):
    n_rows = x.shape[0]

    def body(x_ref, g_ref, u_ref, ut_ref):
        xf = x_ref[...]
        u = (xf * lax.rsqrt(_mean(xf * xf) + EPS)) * g_ref[...]
        u_ref[...] = u.astype(BF16)
        ut_ref[...] = u.T.astype(BF16)

    return pl.pallas_call(
        body, name="rmsnorm_fwd", grid=(n_rows // tm,),
        in_specs=[_rows(tm), _const((1, D))],
        out_specs=[_rows(tm), pl.BlockSpec((D, tm), lambda i: (0, i))],
        out_shape=[jax.ShapeDtypeStruct((n_rows, D), BF16), jax.ShapeDtypeStruct((D, n_rows), BF16)],
        compiler_params=_params(1),
    )(x, norm_g)


def _place():
    x, y, c = lax.axis_index("x"), lax.axis_index("y"), lax.axis_index("c")
    other_chips = [(1 - x, y), (x, 1 - y), (1 - x, 1 - y)]
    return x, y, c, other_chips


def _arrival_order():
    x, y, c, chips = _place()
    order = [4 * x + 2 * y + c, 4 * x + 2 * y + 1 - c]
    for px, py in chips:
        order += [4 * px + 2 * py + c, 4 * px + 2 * py + 1 - c]
    return order


def _proj_fwd_gather(u16, blocks, tm):
    n = len(blocks)
    n_rows = u16.shape[0]
    n_i = n_rows // tm
    per_shard = W_IN_SHARD // UNIT
    assert n_i >= per_shard

    def wt_index(p, i, order):
        return (_dp_unit(per_shard * order[p] + jnp.minimum(i, per_shard - 1)), 0)

    def body(order_ref, u_ref, *refs):
        src, proj_ref, wt_ref, out = refs[:n], refs[n], refs[n + 1], refs[n + 2:2 * n + 2]
        wbuf, stage_sem, send, recv, own_sem = refs[2 * n + 2:]
        p, i = pl.program_id(0), pl.program_id(1)
        x, y, c, chips = _place()
        me, sibling = 4 * x + 2 * y + c, (x, y, 1 - c)

        def copy(t, k, block, to, from_input=False):
            return pltpu.make_async_remote_copy(
                src_ref=src[t] if from_input else out[t].at[block], dst_ref=out[t].at[block],
                send_sem=send.at[t, k], recv_sem=recv.at[t, k], device_id=to, device_id_type=MESH)

        def own_copies():
            return [pltpu.make_async_copy(src[t], out[t].at[me], own_sem.at[t]) for t in range(n)]

        def first_copies():
            first = []
            for t in range(n):
                first.append(copy(t, 0, me, sibling, from_input=True))
                first += [copy(t, 1 + j, me, (*chip, c), from_input=True) for j, chip in enumerate(chips)]
            return first

        def stage(slot, block):
            return pltpu.make_async_copy(out[0].at[block], wbuf.at[slot], stage_sem.at[slot])

        @pl.when((p == 0) & (i == 0))
        def _():
            for cp in own_copies() + first_copies():
                cp.start()
            mine = pltpu.make_async_copy(src[0], wbuf.at[0], stage_sem.at[0])
            mine.start()
            mine.wait()

        @pl.when((p > 0) & (i == 0))
        def _():
            stage(p % 2, order_ref[p]).wait()

        proj_ref[...] = _dot(u_ref[...], wbuf[p % 2])
        for r in range(per_shard):
            @pl.when(i == r)
            def _(r=r):
                wt_ref[...] = wbuf[p % 2, :, r * UNIT:(r + 1) * UNIT].astype(F32).T.astype(BF16)

        for nxt in range(1, N_DEV):
            @pl.when((p == nxt - 1) & (i == n_i - 1))
            def _(nxt=nxt):
                if nxt == 1:
                    block = 4 * x + 2 * y + 1 - c
                    copy(0, 0, block, sibling).wait_recv()
                else:
                    j, passed_on = divmod(nxt - 2, 2)
                    px, py = chips[j]
                    if passed_on:
                        block = 4 * px + 2 * py + 1 - c
                        copy(0, 4 + j, block, sibling).wait_recv()
                    else:
                        block = 4 * px + 2 * py + c
                        copy(0, 1 + j, block, sibling).wait_recv()
                        copy(0, 4 + j, block, sibling).start()
                stage(nxt % 2, block).start()

        @pl.when((p == N_DEV - 1) & (i == n_i - 1))
        def _():
            passed = [copy(0, 4 + j, 4 * px + 2 * py + c, sibling) for j, (px, py) in enumerate(chips)]
            for j, (px, py) in enumerate(chips):
                for t in range(1, n):
                    block = 4 * px + 2 * py + c
                    copy(t, 1 + j, block, sibling).wait_recv()
                    passed.append(copy(t, 4 + j, block, sibling))
                    passed[-1].start()
            for t in range(1, n):
                copy(t, 0, 4 * x + 2 * y + 1 - c, sibling).wait_recv()
                for j, (px, py) in enumerate(chips):
                    copy(t, 4 + j, 4 * px + 2 * py + 1 - c, sibling).wait_recv()
            for cp in first_copies() + passed:
                cp.wait_send()
            for cp in own_copies():
                cp.wait()

    return pl.pallas_call(
        body, name="proj_fwd_gather",
        grid_spec=pltpu.PrefetchScalarGridSpec(
            num_scalar_prefetch=1, grid=(N_DEV, n_i),
            in_specs=[pl.BlockSpec((tm, D), lambda p, i, order: (i, 0))] + [ANY] * n,
            out_specs=[pl.BlockSpec((tm, W_IN_SHARD), lambda p, i, order: (i, order[p])),
                       pl.BlockSpec((UNIT, D), wt_index)] + [ANY] * n,
            scratch_shapes=[pltpu.VMEM((2, D, W_IN_SHARD), BF16), pltpu.SemaphoreType.DMA((2,)),
                            pltpu.SemaphoreType.DMA((n, 7)), pltpu.SemaphoreType.DMA((n, 7)),
                            pltpu.SemaphoreType.DMA((n,))]),
        out_shape=[jax.ShapeDtypeStruct((n_rows, N_GROUPS * D), F32), jax.ShapeDtypeStruct((N_GROUPS * D, D), BF16)]
        + [jax.ShapeDtypeStruct((N_DEV, *b.shape), b.dtype) for b in blocks],
        compiler_params=_params(2),
    )(jnp.stack(_arrival_order()).astype(jnp.int32), u16, *blocks)


def _branch_a_fwd(proj, wo4_g, cw_a, tm):
    n_rows = proj.shape[0]

    def body(bp, cp, xp, za, cph, xph, w_ref, cw_ref, sa_ref, ya_ref, e_scr, o_scr):
        i = pl.program_id(0)
        _to_time_major(e_scr, 0, jnp.where(i > 0, _f32(cph) * _f32(xph), 0.0))
        for r0 in range(0, tm, ELEM_ROWS):
            rows = pl.ds(r0, ELEM_ROWS)
            _to_time_major(e_scr, HALO + r0, cp[rows, :].astype(F32) * xp[rows, :].astype(F32))
        _conv(o_scr, e_scr, cw_ref, FWD_TAPS_A, tm)
        for r0 in range(0, tm, ELEM_ROWS):
            rows = pl.ds(r0, ELEM_ROWS)
            ca = _from_time_major(o_scr, ELEM_ROWS, r0)
            sa_ref[rows, :] = (jax.nn.silu(za[rows, :].astype(F32)) * (bp[rows, :].astype(F32) * ca)).astype(BF16)
        ya_ref[...] = _dot(sa_ref[...], w_ref[...].reshape(D, D))

    return pl.pallas_call(
        body, name="branch_a_fwd", grid=(n_rows // tm,),
        in_specs=[_rows(tm, G_BA), _rows(tm, G_CA), _rows(tm, G_XA), _rows(tm, G_ZA),
                  _prev_halo(tm, G_CA), _prev_halo(tm, G_XA), _w_out_spec(0), _const(cw_a.shape)],
        out_specs=[_rows(tm), _rows(tm)],
        out_shape=[jax.ShapeDtypeStruct((n_rows, D), BF16), jax.ShapeDtypeStruct((n_rows, D), F32)],
        scratch_shapes=[_time_major(tm + HALO), _time_major(tm)],
        compiler_params=_params(1),
    )(proj, proj, proj, proj, proj, proj, wo4_g, cw_a)


def _layernorm_parts(cb, lg, lb):
    xc = cb - _mean(cb)
    rstd = lax.rsqrt(_mean(xc * xc) + EPS)
    xhat = xc * rstd
    return xhat, rstd, xhat * lg + lb


def _branch_b_fwd(proj, wo4_g, cw_b, conv_b_b, ln_g, ln_b, tm):
    n_rows = proj.shape[0]

    def body(vb, gb, zb, vbh, gbh, w_ref, cw_ref, bb_ref, lg_ref, lb_ref, cb_ref, sb_ref, yb_ref, e_scr, o_scr):
        i = pl.program_id(0)
        vbh, gbh = _f32(vbh), _f32(gbh)
        lg, lb = lg_ref[...], lb_ref[...]
        _to_time_major(e_scr, 0, jnp.where(i > 0, vbh * jax.nn.sigmoid(gbh), 0.0))
        for r0 in range(0, tm, ELEM_ROWS):
            rows = pl.ds(r0, ELEM_ROWS)
            _to_time_major(e_scr, HALO + r0, vb[rows, :].astype(F32) * jax.nn.sigmoid(gb[rows, :].astype(F32)))
        _conv(o_scr, e_scr, cw_ref, FWD_TAPS_B, tm, bias_ref=bb_ref)
        for r0 in range(0, tm, ELEM_ROWS):
            rows = pl.ds(r0, ELEM_ROWS)
            cb = _from_time_major(o_scr, ELEM_ROWS, r0)
            cb_ref[rows, :] = cb
            _, _, ln = _layernorm_parts(cb, lg, lb)
            sb_ref[rows, :] = (jax.nn.silu(zb[rows, :].astype(F32)) * jax.nn.silu(ln)).astype(BF16)
        yb_ref[...] = _dot(sb_ref[...], w_ref[...].reshape(D, D))

    return pl.pallas_call(
        body, name="branch_b_fwd", grid=(n_rows // tm,),
        in_specs=[_rows(tm, G_VB), _rows(tm, G_GB), _rows(tm, G_ZB), _prev_halo(tm, G_VB), _prev_halo(tm, G_GB),
                  _w_out_spec(1), _const(cw_b.shape), _const((LANE_GROUPS, LANES)), _const((1, D)), _const((1, D))],
        out_specs=[_rows(tm), _rows(tm), _rows(tm)],
        out_shape=[jax.ShapeDtypeStruct((n_rows, D), F32), jax.ShapeDtypeStruct((n_rows, D), BF16),
                   jax.ShapeDtypeStruct((n_rows, D), F32)],
        scratch_shapes=[_time_major(tm + HALO), _time_major(tm)],
        compiler_params=_params(1),
    )(proj, proj, proj, proj, proj, wo4_g, cw_b, conv_b_b.reshape(LANE_GROUPS, LANES), ln_g, ln_b)


def _attention(q16, kv_ref):
    probs, outs = [], []
    for h in range(N_HEADS):
        s = _dot_nt(q16[:, h * HEAD_DIM:(h + 1) * HEAD_DIM], kv_ref[h]) * (HEAD_DIM ** -0.5)
        e = jnp.exp(s - jnp.max(s, axis=-1, keepdims=True))
        p = e / jnp.sum(e, axis=-1, keepdims=True)
        probs.append(p)
        outs.append(_dot(p.astype(BF16), kv_ref[N_HEADS + h]))
    return probs, outs


def _branch_x_fwd(proj, kv16, wo4_g, tm):
    n_rows = proj.shape[0]

    def body(q, zx, kv_ref, w_ref, sx_ref, yx_ref, p_ref):
        probs, outs = _attention(q[...].astype(BF16), kv_ref)
        p_ref[...] = jnp.concatenate(probs, axis=-1)
        sx = (jax.nn.silu(_f32(zx)) * jnp.concatenate(outs, axis=-1)).astype(BF16)
        sx_ref[...] = sx
        yx_ref[...] = _dot(sx, w_ref[...].reshape(D, D))

    return pl.pallas_call(
        body, name="branch_x_fwd", grid=(n_rows // tm,),
        in_specs=[_rows(tm, G_Q), _rows(tm, G_ZX), _const(kv16.shape), _w_out_spec(2)],
        out_specs=[_rows(tm), _rows(tm), _rows(tm)],
        out_shape=[jax.ShapeDtypeStruct((n_rows, D), BF16), jax.ShapeDtypeStruct((n_rows, D), F32),
                   jax.ShapeDtypeStruct((n_rows, D), F32)],
        compiler_params=_params(1),
    )(proj, proj, kv16, wo4_g)


def _merge_fwd_bwd(proj, ya, yb, yx, x, target, wo4_g, final_g, tm):
    n_rows = proj.shape[0]
    inv_d = 1.0 / D

    def body(ga, gb, gx, ya_ref, yb_ref, yx_ref, x_ref, t_ref, w_ref, fg_ref,
             dh_ref, dya_ref, dyb_ref, dyx_ref, dp_ref, dw_ref, dfg_ref, sq_ref):
        i = pl.program_id(0)
        wo = w_ref[...].reshape(D, D)
        sig = [jax.nn.sigmoid(_f32(g)) for g in (ga, gb, gx)]
        ys = [ya_ref[...], yb_ref[...], yx_ref[...]]
        m16 = (sig[0] * ys[0] + sig[1] * ys[1] + sig[2] * ys[2]).astype(BF16)
        h = x_ref[...] + _dot(m16, wo)
        r = lax.rsqrt(_mean(h * h) + EPS)
        hn = h * r
        fg = fg_ref[...]
        err = hn * fg - t_ref[...]
        dy = err * inv_d
        dhn = dy * fg
        dh = r * (dhn - hn * _mean(dhn * hn))
        dh_ref[...] = dh
        dh16 = dh.astype(BF16)
        dm = _dot_nt(dh16, wo)
        for n, out in enumerate((dya_ref, dyb_ref, dyx_ref)):
            out[...] = (sig[n] * dm).astype(BF16)
            dp_ref[:, n * D:(n + 1) * D] = (dm * ys[n] * (sig[n] * (1.0 - sig[n]))).astype(BF16)

        @pl.when(i == 0)
        def _():
            dw_ref[...] = jnp.zeros_like(dw_ref)
            dfg_ref[...] = jnp.zeros_like(dfg_ref)
            sq_ref[...] = jnp.zeros_like(sq_ref)

        dw_ref[0] += _dot_tn(m16, dh16)
        dfg_ref[...] += _fold8(dy * hn)
        sq_ref[...] += _fold8(err * err)

    vec = jax.ShapeDtypeStruct((SUBLANES, D), F32)
    return pl.pallas_call(
        body, name="merge_fwd_bwd", grid=(n_rows // tm,),
        in_specs=[_rows(tm, G_GA), _rows(tm, G_GBB), _rows(tm, G_GX), _rows(tm), _rows(tm), _rows(tm), _rows(tm),
                  _rows(tm), _w_out_spec(3), _const((1, D))],
        out_specs=[_rows(tm), _rows(tm), _rows(tm), _rows(tm), pl.BlockSpec((tm, 3 * D), lambda i: (i, 3)),
                   pl.BlockSpec((1, D, D), lambda i: (3, 0, 0)), _const((SUBLANES, D)), _const((SUBLANES, D))],
        out_shape=[jax.ShapeDtypeStruct((n_rows, D), F32), jax.ShapeDtypeStruct((n_rows, D), BF16),
                   jax.ShapeDtypeStruct((n_rows, D), BF16), jax.ShapeDtypeStruct((n_rows, D), BF16),
                   jax.ShapeDtypeStruct((n_rows, N_GROUPS * D), BF16), jax.ShapeDtypeStruct((4, D, D), F32), vec, vec],
        compiler_params=_params(1),
    )(proj, proj, proj, ya, yb, yx, x, target, wo4_g, final_g)


def _branch_a_bwd(dya, proj, sa16, wo4_g, cw_a, dproj, dw4, tm):
    n_rows = proj.shape[0]
    n_tiles = n_rows // tm

    def body(dya_ref, bp, cp, xp, za, sa_ref, dyan, bpn, zan, cph, xph, w_ref, cw_ref, dp_in, dw_in,
             dp_ref, dw_ref, dwa_ref, e1, e2, o_scr, mm_scr):
        del dp_in, dw_in
        i = pl.program_id(0)
        bpn, zan, cph, xph = (_f32(r) for r in (bpn, zan, cph, xph))
        woa = w_ref[...].reshape(D, D)
        dya16 = dya_ref[...]
        chunks = [pl.ds(r0, ELEM_ROWS) for r0 in range(0, tm, ELEM_ROWS)]
        _to_time_major(e1, 0, jnp.where(i > 0, cph * xph, 0.0))
        for r0, rows in zip(range(0, tm, ELEM_ROWS), chunks):
            _to_time_major(e1, HALO + r0, cp[rows, :].astype(F32) * xp[rows, :].astype(F32))
        _conv(o_scr, e1, cw_ref, FWD_TAPS_A, tm)
        mm_scr[...] = _dot_nt(dya16, woa)
        for r0, rows in zip(range(0, tm, ELEM_ROWS), chunks):
            ca = _from_time_major(o_scr, ELEM_ROWS, r0)
            dsa = mm_scr[rows, :]
            b = bp[rows, :].astype(F32)
            silu_z, dsilu_z = _silu_and_grad(za[rows, :].astype(F32))
            t = dsa * silu_z
            dp_ref[rows, 0 * D:1 * D] = (t * ca).astype(BF16)
            dp_ref[rows, 3 * D:4 * D] = (dsa * (b * ca) * dsilu_z).astype(BF16)
            _to_time_major(e2, r0, t * b)
        dcan = (_dot_nt(dyan[...], woa) * jax.nn.silu(zan)) * bpn
        _to_time_major(e2, tm, jnp.where(i < n_tiles - 1, dcan, 0.0))

        @pl.when(i == 0)
        def _():
            dw_ref[...] = jnp.zeros_like(dw_ref)
            dwa_ref[...] = jnp.zeros_like(dwa_ref)

        _conv_wgrad(dwa_ref, e2, e1, FWD_TAPS_A, tm)
        dw_ref[0] += _dot_tn(sa_ref[...], dya16)
        _conv(o_scr, e2, cw_ref, BWD_TAPS_A, tm)
        for r0, rows in zip(range(0, tm, ELEM_ROWS), chunks):
            dprod = _from_time_major(o_scr, ELEM_ROWS, r0)
            dp_ref[rows, 1 * D:2 * D] = (dprod * xp[rows, :].astype(F32)).astype(BF16)
            dp_ref[rows, 2 * D:3 * D] = (dprod * cp[rows, :].astype(F32)).astype(BF16)

    return pl.pallas_call(
        body, name="branch_a_bwd", grid=(n_tiles,),
        in_specs=[_rows(tm), _rows(tm, G_BA), _rows(tm, G_CA), _rows(tm, G_XA), _rows(tm, G_ZA), _rows(tm),
                  _next_halo(tm, n_rows), _next_halo(tm, n_rows, G_BA), _next_halo(tm, n_rows, G_ZA),
                  _prev_halo(tm, G_CA), _prev_halo(tm, G_XA), _w_out_spec(0), _const(cw_a.shape), ANY, ANY],
        out_specs=[pl.BlockSpec((tm, 4 * D), lambda i: (i, 0)), pl.BlockSpec((1, D, D), lambda i: (0, 0, 0)),
                   _const((K_A * LANE_GROUPS, LANES))],
        out_shape=[jax.ShapeDtypeStruct(dproj.shape, BF16), jax.ShapeDtypeStruct(dw4.shape, F32),
                   jax.ShapeDtypeStruct((K_A * LANE_GROUPS, LANES), F32)],
        input_output_aliases={13: 0, 14: 1},
        scratch_shapes=[_time_major(tm + HALO), _time_major(tm + HALO), _time_major(tm), pltpu.VMEM((tm, D), F32)],
        compiler_params=_params(1),
    )(dya, proj, proj, proj, proj, sa16, dya, proj, proj, proj, proj, wo4_g, cw_a, dproj, dw4)


def _branch_b_bwd(dyb, proj, cb, sb16, wo4_g, cw_b, ln_g, ln_b, dproj, dw4, tm):
    n_rows = proj.shape[0]
    n_tiles = n_rows // tm

    def body(dyb_ref, zb, cb_ref, vb, gb, sb_ref, dybn, zbn, cbn, vbh, gbh, w_ref, cw_ref, lg_ref, lb_ref,
             dp_in, dw_in, dp_ref, dw_ref, dwb_ref, dbb_ref, dlg_ref, dlb_ref, e1, e2, o_scr, mm_scr):
        del dp_in, dw_in
        zbn, vbh, gbh = (_f32(r) for r in (zbn, vbh, gbh))
        i = pl.program_id(0)
        wob = w_ref[...].reshape(D, D)
        lg, lb = lg_ref[...], lb_ref[...]

        def conv_out_grad(dsb, z, c):
            xhat, rstd, ln = _layernorm_parts(c, lg, lb)
            sw, dsw = _silu_and_grad(ln)
            sz, dsz = _silu_and_grad(z)
            dln = (dsb * sz) * dsw
            dxhat = dln * lg
            dcb = rstd * (dxhat - _mean(dxhat) - xhat * _mean(dxhat * xhat))
            return dsb * sw * dsz, dln, xhat, dcb

        @pl.when(i == 0)
        def _():
            dw_ref[...] = jnp.zeros_like(dw_ref)
            dwb_ref[...] = jnp.zeros_like(dwb_ref)
            dbb_ref[...] = jnp.zeros_like(dbb_ref)
            dlg_ref[...] = jnp.zeros_like(dlg_ref)
            dlb_ref[...] = jnp.zeros_like(dlb_ref)

        dyb16 = dyb_ref[...]
        mm_scr[...] = _dot_nt(dyb16, wob)
        dlg, dlb, dbb = (jnp.zeros((SUBLANES, D), F32),) * 3
        for r0 in range(0, tm, ELEM_ROWS):
            rows = pl.ds(r0, ELEM_ROWS)
            dzb, dln, xhat, dcb = conv_out_grad(mm_scr[rows, :], zb[rows, :].astype(F32), cb_ref[rows, :])
            dp_ref[rows, 2 * D:3 * D] = dzb.astype(BF16)
            _to_time_major(e2, r0, dcb)
            dlg, dlb, dbb = dlg + _fold8(dln * xhat), dlb + _fold8(dln), dbb + _fold8(dcb)
        _, _, _, dcbn = conv_out_grad(_dot_nt(dybn[...], wob), zbn[...], cbn[...])
        _to_time_major(e2, tm, jnp.where(i < n_tiles - 1, dcbn, 0.0))
        dlg_ref[...] += dlg
        dlb_ref[...] += dlb
        dbb_ref[...] += dbb
        dw_ref[0] += _dot_tn(sb_ref[...], dyb16)
        _to_time_major(e1, 0, jnp.where(i > 0, vbh[...] * jax.nn.sigmoid(gbh[...]), 0.0))
        for r0 in range(0, tm, ELEM_ROWS):
            rows = pl.ds(r0, ELEM_ROWS)
            sg = jax.nn.sigmoid(gb[rows, :].astype(F32))
            mm_scr[rows, :] = sg
            _to_time_major(e1, HALO + r0, vb[rows, :].astype(F32) * sg)
        _conv_wgrad(dwb_ref, e2, e1, FWD_TAPS_B, tm)
        _conv(o_scr, e2, cw_ref, BWD_TAPS_B, tm)
        for r0 in range(0, tm, ELEM_ROWS):
            rows = pl.ds(r0, ELEM_ROWS)
            dglu = _from_time_major(o_scr, ELEM_ROWS, r0)
            sg = mm_scr[rows, :]
            dp_ref[rows, 0 * D:1 * D] = (dglu * sg).astype(BF16)
            dp_ref[rows, 1 * D:2 * D] = (dglu * vb[rows, :].astype(F32) * (sg * (1.0 - sg))).astype(BF16)

    vec = jax.ShapeDtypeStruct((SUBLANES, D), F32)
    return pl.pallas_call(
        body, name="branch_b_bwd", grid=(n_tiles,),
        in_specs=[_rows(tm), _rows(tm, G_ZB), _rows(tm), _rows(tm, G_VB), _rows(tm, G_GB), _rows(tm),
                  _next_halo(tm, n_rows), _next_halo(tm, n_rows, G_ZB), _next_halo(tm, n_rows),
                  _prev_halo(tm, G_VB), _prev_halo(tm, G_GB), _w_out_spec(1), _const(cw_b.shape), _const((1, D)),
                  _const((1, D)), ANY, ANY],
        out_specs=[pl.BlockSpec((tm, 3 * D), lambda i: (i, 2)), pl.BlockSpec((1, D, D), lambda i: (1, 0, 0)),
                   _const((K_B * LANE_GROUPS, LANES)), _const((SUBLANES, D)), _const((SUBLANES, D)),
                   _const((SUBLANES, D))],
        out_shape=[jax.ShapeDtypeStruct(dproj.shape, BF16), jax.ShapeDtypeStruct(dw4.shape, F32),
                   jax.ShapeDtypeStruct((K_B * LANE_GROUPS, LANES), F32), vec, vec, vec],
        input_output_aliases={15: 0, 16: 1},
        scratch_shapes=[_time_major(tm + HALO), _time_major(tm + HALO), _time_major(tm), pltpu.VMEM((tm, D), F32)],
        compiler_params=_params(1),
    )(dyb, proj, cb, proj, proj, sb16, dyb, proj, cb, proj, proj, wo4_g, cw_b, ln_g, ln_b, dproj, dw4)


def _branch_x_bwd(dyx, proj, sx16, probs, kv16, wo4_g, dproj, dw4, tm):
    n_rows = proj.shape[0]
    scale = HEAD_DIM ** -0.5

    def body(dyx_ref, q, zx, sx_ref, p_ref, kv_ref, w_ref, dp_in, dw_in, dp_ref, dw_ref, dkv_ref):
        del dp_in, dw_in
        i = pl.program_id(0)
        dyx16 = dyx_ref[...]
        q16 = q[...].astype(BF16)
        probs = [p_ref[:, h * HEAD_DIM:(h + 1) * HEAD_DIM] for h in range(N_HEADS)]
        outs = [_dot(probs[h].astype(BF16), kv_ref[N_HEADS + h]) for h in range(N_HEADS)]
        dsx = _dot_nt(dyx16, w_ref[...].reshape(D, D))
        silu_z, dsilu_z = _silu_and_grad(_f32(zx))
        dp_ref[:, D:2 * D] = (dsx * jnp.concatenate(outs, axis=-1) * dsilu_z).astype(BF16)
        do16 = (dsx * silu_z).astype(BF16)

        @pl.when(i == 0)
        def _():
            dw_ref[...] = jnp.zeros_like(dw_ref)
            dkv_ref[...] = jnp.zeros_like(dkv_ref)

        for h in range(N_HEADS):
            cols = slice(h * HEAD_DIM, (h + 1) * HEAD_DIM)
            p = probs[h]
            dprob = _dot_nt(do16[:, cols], kv_ref[N_HEADS + h])
            ds16 = ((p * (dprob - jnp.sum(p * dprob, axis=-1, keepdims=True))) * scale).astype(BF16)
            dp_ref[:, cols] = _dot(ds16, kv_ref[h]).astype(BF16)
            dkv_ref[h] += _dot_tn(ds16, q16[:, cols])
            dkv_ref[N_HEADS + h] += _dot_tn(p.astype(BF16), do16[:, cols])
        dw_ref[0] += _dot_tn(sx_ref[...], dyx16)

    return pl.pallas_call(
        body, name="branch_x_bwd", grid=(n_rows // tm,),
        in_specs=[_rows(tm), _rows(tm, G_Q), _rows(tm, G_ZX), _rows(tm), _rows(tm), _const(kv16.shape),
                  _w_out_spec(2), ANY, ANY],
        out_specs=[pl.BlockSpec((tm, 2 * D), lambda i: (i, 2)), pl.BlockSpec((1, D, D), lambda i: (2, 0, 0)),
                   _const(kv16.shape)],
        out_shape=[jax.ShapeDtypeStruct(dproj.shape, BF16), jax.ShapeDtypeStruct(dw4.shape, F32),
                   jax.ShapeDtypeStruct(kv16.shape, F32)],
        input_output_aliases={7: 0, 8: 1},
        compiler_params=_params(1),
    )(dyx, proj, proj, sx16, probs, kv16, wo4_g, dproj, dw4)


def _dp_unit(u):
    g = u // 2
    pos = jnp.where(g < G_VB, g, jnp.where(g < G_Q, g + 2, jnp.where(g < G_GA, g - 3, g)))
    return 2 * pos + u % 2


def _scatter_copies(srcs, lands, send, recv):
    x, y, c = lax.axis_index("x"), lax.axis_index("y"), lax.axis_index("c")
    copies = []
    for n in range(N_DEV - 1):
        flip = n + 1
        px = 1 - x if flip & 4 else x
        py = 1 - y if flip & 2 else y
        pc = 1 - c if flip & 1 else c
        for t, (src, land) in enumerate(zip(srcs, lands)):
            copies.append(pltpu.make_async_remote_copy(
                src_ref=src.at[4 * px + 2 * py + pc], dst_ref=land.at[n], send_sem=send.at[t * (N_DEV - 1) + n],
                recv_sem=recv.at[t * (N_DEV - 1) + n], device_id=(px, py, pc), device_id_type=MESH))
    return copies


def _scatter_start(name, arrays, views, n_views):
    n = len(arrays)
    lands = [lax.empty(tuple(N_DEV - 1 if d == N_DEV else d for d in a.shape), a.dtype) for a in arrays]

    def body(*refs):
        src, land, (send, recv) = refs[:n], refs[n:2 * n], refs[2 * n:2 * n + 2]
        token = refs[-1]
        for cp in _scatter_copies(views(src), views(land), send, recv):
            cp.start()
        token[...] = jnp.zeros_like(token)

    sems = pltpu.SemaphoreType.DMA((n_views * (N_DEV - 1),))
    out = pl.pallas_call(
        body, name=name,
        in_specs=[HBM] * (2 * n),
        out_specs=[SEM, SEM] + [HBM] * (2 * n) + [pl.BlockSpec(memory_space=pltpu.VMEM)],
        out_shape=[sems, sems] + [pltpu.HBM(a.shape, a.dtype) for a in arrays + lands]
        + [jax.ShapeDtypeStruct((SUBLANES, LANES), F32)],
        input_output_aliases={k: 2 + k for k in range(2 * n)},
        compiler_params=pltpu.CompilerParams(has_side_effects=SIDE_EFFECT),
    )(*[pltpu.with_memory_space_constraint(a, pltpu.HBM) for a in arrays + lands])
    return dict(name=name, sems=out[:2], moving=out[2:2 + 2 * n], views=views, token=out[-1])


def _scatter_wait(started, after):
    n = len(started["moving"]) // 2
    views = started["views"]

    def body(*refs):
        src, land, (send, recv) = refs[:n], refs[n:2 * n], refs[2 * n:2 * n + 2]
        for cp in _scatter_copies(views(src), views(land), send, recv):
            cp.wait_send()
            cp.wait_recv()

    out = pl.pallas_call(
        body, name=started["name"].replace("start", "wait"),
        in_specs=[HBM] * (2 * n) + [SEM, SEM, ANY],
        out_specs=[HBM] * (2 * n),
        out_shape=[pltpu.HBM(a.shape, a.dtype) for a in started["moving"]],
        input_output_aliases={k: k for k in range(2 * n)},
        compiler_params=pltpu.CompilerParams(has_side_effects=SIDE_EFFECT),
    )(*started["moving"], *started["sems"], after)
    return out[n:]


def _w_in_grad(ut, dproj, token):
    n_rows = dproj.shape[0]
    per_shard = W_IN_SHARD // UNIT

    def body(ut_ref, dp_ref, token_ref, out_ref, out16_ref):
        del token_ref
        dw = _dot(ut_ref[...], dp_ref[...])
        out_ref[0] = dw
        out16_ref[0] = dw.astype(BF16)

    unit = pl.BlockSpec((1, D, UNIT), lambda q, r: (q, 0, r))
    return pl.pallas_call(
        body, name="w_in_grad", grid=(N_DEV, per_shard),
        in_specs=[pl.BlockSpec((D, n_rows), lambda q, r: (0, 0), pipeline_mode=pl.Buffered(1)),
                  pl.BlockSpec((n_rows, UNIT), lambda q, r: (0, _dp_unit(per_shard * q + r))), ANY],
        out_specs=[unit, unit],
        out_shape=[jax.ShapeDtypeStruct((N_DEV, D, W_IN_SHARD), F32), jax.ShapeDtypeStruct((N_DEV, D, W_IN_SHARD), BF16)],
        compiler_params=_params(2),
    )(ut, dproj, token)


def _x_grad(dproj, win_t, x, dh, norm_g, token, tm):
    n_rows = x.shape[0]
    n_k = N_GROUPS * D // X_GRAD_K

    def body(dp_ref, wt_ref, x_ref, dh_ref, g_ref, token_ref, gx_ref, dg_ref, acc):
        del token_ref
        i, g = pl.program_id(0), pl.program_id(1)

        @pl.when((i == 0) & (g == 0))
        def _():
            dg_ref[...] = jnp.zeros_like(dg_ref)

        @pl.when(g == 0)
        def _():
            acc[...] = _dot(dp_ref[...], wt_ref[...])

        @pl.when(g > 0)
        def _():
            acc[...] += _dot(dp_ref[...], wt_ref[...])

        @pl.when(g == n_k - 1)
        def _():
            du = acc[...]
            xf = x_ref[...]
            r = lax.rsqrt(_mean(xf * xf) + EPS)
            xn = xf * r
            dun = du * g_ref[...]
            gx_ref[...] = dh_ref[...] + r * (dun - xn * _mean(dun * xn))
            dg_ref[...] += _fold8(du * xn)

    return pl.pallas_call(
        body, name="x_grad", grid=(n_rows // tm, n_k),
        in_specs=[pl.BlockSpec((tm, X_GRAD_K), lambda i, g: (i, g)), pl.BlockSpec((X_GRAD_K, D), lambda i, g: (g, 0)),
                  pl.BlockSpec((tm, D), lambda i, g: (i, 0)), pl.BlockSpec((tm, D), lambda i, g: (i, 0)),
                  _const((1, D)), ANY],
        out_specs=[pl.BlockSpec((tm, D), lambda i, g: (i, 0)), _const((SUBLANES, D))],
        out_shape=[jax.ShapeDtypeStruct((n_rows, D), F32), jax.ShapeDtypeStruct((SUBLANES, D), F32)],
        scratch_shapes=[pltpu.VMEM((tm, D), F32)],
        compiler_params=_params(2),
    )(dproj, win_t, x, dh, norm_g, token)


def _local_step(x, mem, target, norm_g, conv_b_b, ln_g, ln_b, mem_g, final_g, shards):
    n_rows = x.shape[0]
    tm = min(512, n_rows)
    big = min(1024, n_rows)
    u16, ut = _rmsnorm_fwd(x, norm_g, big)
    proj, win_t, _, wkv_g, wo4_g, cw_g = _proj_fwd_gather(u16, shards, min(2048, n_rows))
    cw_rows = cw_g.transpose(1, 0, 2).reshape((SUBLANES + HALO) * LANE_GROUPS, LANES)
    cw_a, cw_b = cw_rows[:SUBLANES * LANE_GROUPS], cw_rows[SUBLANES * LANE_GROUPS:]
    kv16, mn16 = _kv_fwd(mem, mem_g, wkv_g)
    sa16, ya = _branch_a_fwd(proj, wo4_g, cw_a, big)
    cb, sb16, yb = _branch_b_fwd(proj, wo4_g, cw_b, conv_b_b, ln_g, ln_b, big)
    sx16, yx, probs = _branch_x_fwd(proj, kv16, wo4_g, big)
    dh, dya, dyb, dyx, dproj, dw4, dfg, sq = _merge_fwd_bwd(proj, ya, yb, yx, x, target, wo4_g, final_g,
                                                             min(256, n_rows))
    dproj, dw4, dwa = _branch_a_bwd(dya, proj, sa16, wo4_g, cw_a, dproj, dw4, tm)
    dproj, dw4, dwb, dbb, dlg, dlb = _branch_b_bwd(dyb, proj, cb, sb16, wo4_g, cw_b, ln_g, ln_b, dproj, dw4, tm)
    dproj, dw4, dkv = _branch_x_bwd(dyx, proj, sx16, probs, kv16, wo4_g, dproj, dw4, tm)
    dwkv_g, dwkv16, dmg = _kv_bwd(dkv, mem, mem_g, mn16, wkv_g)
    dw4 = dw4.reshape(4, N_DEV, D // N_DEV, D)
    small_moving = _scatter_start("small_grads_start", [dw4.astype(BF16), dwkv16],
                                  lambda refs: [refs[0].at[w] for w in range(4)] + [refs[1]], 5)
    dwin_g, dwin16 = _w_in_grad(ut, dproj, small_moving["token"])
    w_in_moving = _scatter_start("w_in_grad_start", [dwin16], lambda refs: list(refs), 1)
    gx, dng = _x_grad(dproj, win_t, x, dh, norm_g, w_in_moving["token"], big)
    land4, landkv = _scatter_wait(small_moving, dng)
    landin, = _scatter_wait(w_in_moving, dng)
    small = {SV_NORM_G: dng, SV_CONV_B_B: dbb, SV_LN_G: dlg, SV_LN_B: dlb, SV_MEM_G: dmg, SV_FINAL_G: dfg, SV_LOSS: sq}
    grads = [(dwin_g[None], landin[None]), (dw4, land4), (dwkv_g[None], landkv[None])]
    return gx, grads, small, dwa.reshape(K_A, D), dwb.reshape(K_B, D)


def _allgather_small(small, conv_rows):
    keys = sorted(small)

    def body(*refs):
        parts, (conv_ref, out_ref, mine, send, recv) = refs[:len(keys)], refs[len(keys):]
        x, y, c, chips = _place()
        me, sibling = 4 * x + 2 * y + c, (x, y, 1 - c)
        mine[pl.ds(0, SV_CONV_A), :] = jnp.zeros((SV_CONV_A, D), F32)
        for key, part in zip(keys, parts):
            mine[key:key + 1, :] = jnp.sum(part[...], axis=0, keepdims=True)
        mine[pl.ds(SV_CONV_A, SV_ROWS - SV_CONV_A), :] = conv_ref[...]
        out_ref[me] = mine[...]

        def copy(k, block, to, from_mine=False):
            return pltpu.make_async_remote_copy(
                src_ref=mine if from_mine else out_ref.at[block], dst_ref=out_ref.at[block],
                send_sem=send.at[k], recv_sem=recv.at[k], device_id=to, device_id_type=MESH)

        first = [copy(0, me, sibling, from_mine=True)]
        first += [copy(1 + j, me, (*chip, c), from_mine=True) for j, chip in enumerate(chips)]
        for cp in first:
            cp.start()
        passed = []
        for j, (px, py) in enumerate(chips):
            block = 4 * px + 2 * py + c
            copy(1 + j, block, sibling).wait_recv()
            passed.append(copy(4 + j, block, sibling))
            passed[-1].start()
        copy(0, 4 * x + 2 * y + 1 - c, sibling).wait_recv()
        for j, (px, py) in enumerate(chips):
            copy(4 + j, 4 * px + 2 * py + 1 - c, sibling).wait_recv()
        for cp in first + passed:
            cp.wait_send()

    vmem = pl.BlockSpec(memory_space=pltpu.VMEM)
    return pl.pallas_call(
        body, name="allgather_small",
        in_specs=[vmem] * (len(keys) + 1), out_specs=vmem,
        out_shape=jax.ShapeDtypeStruct((N_DEV, SV_ROWS, D), F32),
        scratch_shapes=[pltpu.VMEM((SV_ROWS, D), F32), pltpu.SemaphoreType.DMA((7,)), pltpu.SemaphoreType.DMA((7,))],
    )(*[small[k] for k in keys], conv_rows)


def _adamw(w, g, m, v):
    m = ADAM_B1 * m + (1.0 - ADAM_B1) * g
    v = ADAM_B2 * v + (1.0 - ADAM_B2) * (g * g)
    m_hat = m / (1.0 - ADAM_B1 ** ADAM_STEP)
    v_hat = v / (1.0 - ADAM_B2 ** ADAM_STEP)
    return -ADAM_LR * (m_hat / (jnp.sqrt(v_hat) + ADAM_EPS) + ADAM_WD * w), m, v


def _adamw_shard(own, landed, piece, k_arr, w, m, v, tr):
    n_r, n_c = w.shape
    n_landed = landed.shape[1]

    def body(k_ref, own_ref, *refs):
        del k_ref
        landed_refs, (w_ref, m_ref, v_ref, g_out, d_out, m_out, v_out) = refs[:n_landed], refs[n_landed:]
        g = own_ref[0, 0]
        for landed_ref in landed_refs:
            g = g + landed_ref[0, 0].astype(F32)
        g_out[...] = g
        d_out[...], m_out[...], v_out[...] = _adamw(w_ref[...], g, m_ref[...], v_ref[...])

    blk = (1, 1, tr, n_c)
    flat = pl.BlockSpec((tr, n_c), lambda r, k: (r, 0))
    return pl.pallas_call(
        body, name="adamw_shard",
        grid_spec=pltpu.PrefetchScalarGridSpec(
            num_scalar_prefetch=1, grid=(n_r // tr,),
            in_specs=[pl.BlockSpec(blk, lambda r, k: (piece, k[0], r, 0))]
            + [pl.BlockSpec(blk, functools.partial(lambda r, k, j: (piece, j, r, 0), j=j)) for j in range(n_landed)]
            + [flat] * 3,
            out_specs=[flat] * 4),
        out_shape=[jax.ShapeDtypeStruct((n_r, n_c), F32)] * 4,
        compiler_params=_params(1),
    )(k_arr, own, *([landed] * n_landed), w, m, v)


def _adamw_shards(entries, k_arr):
    n = len(entries)

    def body(k_ref, *refs):
        del k_ref
        ins, outs = refs[:5 * n], refs[5 * n:]
        for e in range(n):
            own_ref, landed_ref, w_ref, m_ref, v_ref = ins[5 * e:5 * e + 5]
            g = own_ref[0, 0]
            for j in range(landed_ref.shape[1]):
                g = g + landed_ref[0, j].astype(F32)
            g_out, d_out, m_out, v_out = outs[4 * e:4 * e + 4]
            g_out[...] = g
            d_out[...], m_out[...], v_out[...] = _adamw(w_ref[...], g, m_ref[...], v_ref[...])

    in_specs, operands, out_specs, out_shape = [], [], [], []
    for own, landed, piece, w, m, v in entries:
        flat = pl.BlockSpec(w.shape, lambda i, k: (0, 0))
        in_specs += [pl.BlockSpec((1, 1, *w.shape), functools.partial(lambda i, k, p: (p, k[0], 0, 0), p=piece)),
                     pl.BlockSpec((1, *landed.shape[1:]), functools.partial(lambda i, k, p: (p, 0, 0, 0), p=piece)),
                     flat, flat, flat]
        operands += [own, landed, w, m, v]
        out_specs += [flat] * 4
        out_shape += [jax.ShapeDtypeStruct(w.shape, F32)] * 4
    out = pl.pallas_call(
        body, name="adamw_shards",
        grid_spec=pltpu.PrefetchScalarGridSpec(num_scalar_prefetch=1, grid=(1,), in_specs=in_specs,
                                               out_specs=out_specs),
        out_shape=out_shape,
        compiler_params=_params(1),
    )(k_arr, *operands)
    return [tuple(out[4 * e:4 * e + 4]) for e in range(n)]


def _adamw_small(gathered, k_arr, vectors, conv_a, conv_b):
    n_vec = len(vectors)
    cols = D // N_DEV

    def body(k_ref, full_ref, cols_ref, *refs):
        del k_ref
        ins, outs = refs[:3 * (n_vec + 2)], refs[3 * (n_vec + 2):-2]
        tot, tot_cols = refs[-2:]
        tot[...] = full_ref[0]
        tot_cols[...] = cols_ref[0]
        for dev in range(1, N_DEV):
            tot[...] += full_ref[dev]
            tot_cols[...] += cols_ref[dev]
        loss = (0.5 / D) * jnp.sum(tot[SV_LOSS:SV_LOSS + 1, :])
        outs[0][...] = jnp.full(outs[0].shape, loss, F32)
        grads = [tot[n:n + 1, :] for n in range(n_vec)]
        grads += [tot_cols[pl.ds(SV_CONV_A, K_A), :], tot_cols[pl.ds(SV_CONV_B, K_B), :]]
        for n, g in enumerate(grads):
            w_ref, m_ref, v_ref = ins[3 * n:3 * n + 3]
            g_out, d_out, m_out, v_out = outs[1 + 4 * n:5 + 4 * n]
            g_out[...] = g
            d_out[...], m_out[...], v_out[...] = _adamw(w_ref[...], g, m_ref[...], v_ref[...])

    weights = list(vectors) + [conv_a, conv_b]
    flat_in = [a for wmv in weights for a in wmv]
    out_shape = [jax.ShapeDtypeStruct((SUBLANES, 128), F32)]
    for wmv in weights:
        out_shape += [jax.ShapeDtypeStruct(wmv[0].shape, F32)] * 4
    return pl.pallas_call(
        body, name="adamw_small",
        grid_spec=pltpu.PrefetchScalarGridSpec(
            num_scalar_prefetch=1, grid=(1,),
            in_specs=[pl.BlockSpec((N_DEV, SV_ROWS, D), lambda i, k: (0, 0, 0)),
                      pl.BlockSpec((N_DEV, SV_ROWS, cols), lambda i, k: (0, 0, k[0]))]
            + [pl.BlockSpec(a.shape, lambda i, k: (0, 0)) for a in flat_in],
            out_specs=[pl.BlockSpec(s.shape, lambda i, k: (0, 0)) for s in out_shape],
            scratch_shapes=[pltpu.VMEM((SV_ROWS, D), F32), pltpu.VMEM((SV_ROWS, cols), F32)]),
        out_shape=out_shape,
        compiler_params=_params(1),
    )(k_arr, gathered, gathered, *flat_in)


def kernel(x, mem, norm_g, w_in, conv_a_w, w_out_a, conv_b_w, conv_b_b, ln_b_g, ln_b_b, w_out_b, mem_norm_g, w_kv, w_out_x, w_o, final_g, loss_target, m_norm_g, m_w_in, m_conv_a_w, m_w_out_a, m_conv_b_w, m_conv_b_b, m_ln_b_g, m_ln_b_b, m_w_out_b, m_mem_norm_g, m_w_kv, m_w_out_x, m_w_o, m_final_g, v_norm_g, v_w_in, v_conv_a_w, v_w_out_a, v_conv_b_w, v_conv_b_b, v_ln_b_g, v_ln_b_b, v_w_out_b, v_mem_norm_g, v_w_kv, v_w_out_x, v_w_o, v_final_g):
    xi, yi, ci = lax.axis_index("x"), lax.axis_index("y"), lax.axis_index("c")
    k_arr = jnp.reshape(4 * xi + 2 * yi + ci, (1,)).astype(jnp.int32)

    cw = jnp.concatenate([jnp.pad(conv_a_w[0], ((0, SUBLANES - K_A), (0, 0))),
                          jnp.pad(conv_b_w[0], ((0, HALO - K_B), (0, 0)))], axis=0)
    wo4 = jnp.stack([w_out_a[0], w_out_b[0], w_out_x[0], w_o[0]]).astype(BF16)
    shards = [w_in[0].astype(BF16), w_kv[0].astype(BF16), wo4, cw]

    final_g2 = final_g.reshape(1, D)
    gx, grads, small, dwa, dwb = _local_step(
        x[0], mem[0], loss_target[0], norm_g, conv_b_b, ln_b_g, ln_b_b, mem_norm_g, final_g2, shards)

    conv_rows = jnp.concatenate([jnp.pad(dwa, ((0, SUBLANES - K_A), (0, 0))),
                                 jnp.pad(dwb, ((0, HALO - K_B), (0, 0)))], axis=0)
    gathered_small = _allgather_small(small, conv_rows)

    res = {"w_in": _adamw_shard(grads[0][0], grads[0][1], 0, k_arr, w_in[0], m_w_in[0], v_w_in[0], 256)}
    small_shards = [("w_out_a", 1, 0, w_out_a, m_w_out_a, v_w_out_a), ("w_out_b", 1, 1, w_out_b, m_w_out_b, v_w_out_b),
                    ("w_out_x", 1, 2, w_out_x, m_w_out_x, v_w_out_x), ("w_o", 1, 3, w_o, m_w_o, v_w_o),
                    ("w_kv", 2, 0, w_kv, m_w_kv, v_w_kv)]
    updated = _adamw_shards([(grads[a][0], grads[a][1], l, w[0], m[0], v[0]) for _, a, l, w, m, v in small_shards],
                            k_arr)
    res.update({name: four for (name, *_), four in zip(small_shards, updated)})
    res = {name: tuple(r[None] for r in four) for name, four in res.items()}
    vectors = [(norm_g, m_norm_g, v_norm_g), (conv_b_b, m_conv_b_b, v_conv_b_b), (ln_b_g, m_ln_b_g, v_ln_b_g),
               (ln_b_b, m_ln_b_b, v_ln_b_b), (mem_norm_g, m_mem_norm_g, v_mem_norm_g),
               (final_g2, m_final_g.reshape(1, D), v_final_g.reshape(1, D))]
    out = _adamw_small(gathered_small, k_arr, vectors, (conv_a_w[0], m_conv_a_w[0], v_conv_a_w[0]),
                       (conv_b_w[0], m_conv_b_w[0], v_conv_b_w[0]))
    loss = out[0][0, 0]
    names = ["norm_g", "conv_b_b", "ln_b_g", "ln_b_b", "mem_norm_g", "final_g", "conv_a_w", "conv_b_w"]
    for n, name in enumerate(names):
        four = out[1 + 4 * n:5 + 4 * n]
        if name == "final_g":
            four = [r.reshape(D) for r in four]
        elif name.startswith("conv_") and name.endswith("_w"):
            four = [r[None] for r in four]
        res[name] = tuple(four)

    order = ["norm_g", "w_in", "conv_a_w", "w_out_a", "conv_b_w", "conv_b_b", "ln_b_g", "ln_b_b", "w_out_b",
             "mem_norm_g", "w_kv", "w_out_x", "w_o", "final_g"]
    return (loss, gx[None], *[res[n][0] for n in order], *[res[n][1] for n in order],
            *[res[n][2] for n in order], *[res[n][3] for n in order])
```
